```python
import math
import jax, jax.numpy as jnp
from jax import lax
import numpy as np

D_MODEL = 1024
BATCH = 8
SEQ = 2048
DEPTH = 1

SSM_WIDTH = D_MODEL // 2
SSM_GROUP_SIZE = 16
SSM_GROUPS = SSM_WIDTH // SSM_GROUP_SIZE
SSM_STATE = 64
N_HEADS = 8
HEAD_DIM = 64
N_KV_GROUPS = 2
HEADS_PER_GROUP = N_HEADS // N_KV_GROUPS
ATTN_WIDTH = N_HEADS * HEAD_DIM
KV_WIDTH = N_KV_GROUPS * HEAD_DIM
N_KV_SETS = 6
CMP_BLOCK = 32
CMP_STRIDE = 16
CMP_HIDDEN = HEAD_DIM
SEL_BLOCK = 64
SEL_TOP_K = 8
SEL_Q_BLOCK = 64
WINDOW = 256
WIN_Q_BLOCK = 128
N_NSA_BRANCHES = 3
N_BRANCHES = 2
FFN_HIDDEN = (8 * D_MODEL + 3 * 256 - 1) // (3 * 256) * 256
IN_SPLITS = (SSM_WIDTH, ATTN_WIDTH, N_KV_SETS * KV_WIDTH, N_NSA_BRANCHES * N_HEADS, N_BRANCHES * D_MODEL)
IN_COLS = SSM_WIDTH + ATTN_WIDTH + N_KV_SETS * KV_WIDTH + N_NSA_BRANCHES * N_HEADS + N_BRANCHES * D_MODEL
DEEPNORM_ALPHA = (2 * DEPTH) ** 0.25
DEEPNORM_BETA = (8 * DEPTH) ** -0.25
LN_EPS = 1e-5
MASK_VALUE = -1e30

kernel_name = 'hybrid_s5_nsa_deepnorm_block'


def _layer_norm(x, gain, bias):
    xf = x.astype(jnp.float32)
    mu = jnp.mean(xf, axis=-1, keepdims=True)
    var = jnp.mean(jnp.square(xf - mu), axis=-1, keepdims=True)
    y = (xf - mu) * lax.rsqrt(var + LN_EPS) * gain.astype(jnp.float32) + bias.astype(jnp.float32)
    return y.astype(x.dtype)


def _masked_softmax(s, mask):
    s = jnp.where(mask, s, MASK_VALUE)
    return jnp.where(mask, jax.nn.softmax(s, axis=-1), 0.0)


def _alibi_slopes():
    h = jnp.arange(1, N_HEADS + 1, dtype=jnp.float32)
    return jnp.exp2(-8.0 * h / N_HEADS).reshape(N_KV_GROUPS, HEADS_PER_GROUP)


def _complex_linear_combine(e1, e2):
    a1r, a1i, b1r, b1i = e1
    a2r, a2i, b2r, b2i = e2
    ar = a2r * a1r - a2i * a1i
    ai = a2r * a1i + a2i * a1r
    br = a2r * b1r - a2i * b1i + b2r
    bi = a2r * b1i + a2i * b1r + b2i
    return ar, ai, br, bi


def _s5_scan(u, a_re, a_im, log_dt, b_re, b_im, c_re, c_im, d_skip):
    bsz, seq, _ = u.shape
    f32 = jnp.float32
    uf = u.astype(f32).reshape(bsz, seq, SSM_GROUPS, SSM_GROUP_SIZE)
    a_re = a_re.astype(f32)
    a_im = a_im.astype(f32)
    dt = jnp.exp(log_dt.astype(f32))[:, None]
    decay = jnp.exp(a_re * dt)
    ab_re = decay * jnp.cos(a_im * dt)
    ab_im = decay * jnp.sin(a_im * dt)
    denom = jnp.square(a_re) + jnp.square(a_im)
    n_re = ab_re - 1.0
    n_im = ab_im
    f_re = (n_re * a_re + n_im * a_im) / denom
    f_im = (n_im * a_re - n_re * a_im) / denom
    bu_re = jnp.einsum('bsgc,gpc->bsgp', uf, b_re.astype(f32))
    bu_im = jnp.einsum('bsgc,gpc->bsgp', uf, b_im.astype(f32))
    in_re = f_re * bu_re - f_im * bu_im
    in_im = f_re * bu_im + f_im * bu_re
    ar = jnp.broadcast_to(ab_re, in_re.shape)
    ai = jnp.broadcast_to(ab_im, in_re.shape)
    _, _, h_re, h_im = lax.associative_scan(_complex_linear_combine, (ar, ai, in_re, in_im), axis=1)
    y = (jnp.einsum('bsgp,gcp->bsgc', h_re, c_re.astype(f32))
         - jnp.einsum('bsgp,gcp->bsgc', h_im, c_im.astype(f32))
         + d_skip.astype(f32) * uf)
    return y.reshape(bsz, seq, SSM_WIDTH).astype(u.dtype)


def _compress(kv, pe, w1, w2):
    bsz, seq = kv.shape[:2]
    r = CMP_BLOCK // CMP_STRIDE
    n_chunks = seq // CMP_STRIDE
    chunks = kv.reshape(bsz, n_chunks, CMP_STRIDE, N_KV_GROUPS, HEAD_DIM)
    blocks = jnp.concatenate([chunks[:, i:n_chunks - r + 1 + i] for i in range(r)], axis=2)
    blocks = blocks + pe[None, None, :, None, :]
    h = jax.nn.gelu(jnp.einsum('bnlgd,lde->bnge', blocks, w1))
    return jnp.einsum('bnge,ed->bngd', h, w2)


def _nsa(q, k_cmp, v_cmp, k_sel, v_sel, k_win, v_win, branch_gate, pe_k, pe_v, w_ck1, w_ck2, w_cv1, w_cv2):
    bsz, seq = q.shape[:2]
    f32 = jnp.float32
    slopes = _alibi_slopes()
    pos = jnp.arange(seq)

    kc = _compress(k_cmp, pe_k, w_ck1, w_ck2)
    vc = _compress(v_cmp, pe_v, w_cv1, w_cv2)
    n_cmp = kc.shape[1]
    cmp_start = jnp.arange(n_cmp) * CMP_STRIDE
    cmp_end = cmp_start + CMP_BLOCK - 1
    dist_c = (pos[:, None] - cmp_end[None, :]).astype(f32)
    s_c = jnp.einsum('bsghd,bngd->bghsn', q, kc).astype(f32)
    s_c = s_c - slopes[None, :, :, None, None] * dist_c
    p_c = _masked_softmax(s_c, dist_c >= 0)
    o_cmp = jnp.einsum('bghsn,bngd->bsghd', p_c, vc)

    n_sel = seq // SEL_BLOCK
    sel_start = jnp.arange(n_sel) * SEL_BLOCK
    overlap = ((cmp_start[:, None] < sel_start[None, :] + SEL_BLOCK)
               & (cmp_start[:, None] + CMP_BLOCK > sel_start[None, :])).astype(f32)
    imp = jnp.einsum('bghsn,nj->bgsj', p_c, overlap)
    blk = jnp.arange(n_sel)
    forced = (blk[None, :] == 0) | (blk[None, :] == (pos // SEL_BLOCK)[:, None])
    future = sel_start[None, :] > pos[:, None]
    score = jnp.where(forced, jnp.inf, jnp.where(future, -jnp.inf, imp))
    top_k = min(SEL_TOP_K, n_sel)
    _, idx = lax.top_k(score, top_k)

    ks_b = k_sel.reshape(bsz, n_sel, SEL_BLOCK, N_KV_GROUPS, HEAD_DIM).transpose(0, 3, 1, 2, 4)
    vs_b = v_sel.reshape(bsz, n_sel, SEL_BLOCK, N_KV_GROUPS, HEAD_DIM).transpose(0, 3, 1, 2, 4)
    nqb = seq // SEL_Q_BLOCK
    q_blocks = q.reshape(bsz, nqb, SEL_Q_BLOCK, N_KV_GROUPS, HEADS_PER_GROUP, HEAD_DIM).transpose(1, 0, 3, 4, 2, 5)
    idx_blocks = idx.reshape(bsz, N_KV_GROUPS, nqb, SEL_Q_BLOCK, top_k).transpose(2, 0, 1, 3, 4)
    b_ix = jnp.arange(bsz)[:, None, None]
    g_ix = jnp.arange(N_KV_GROUPS)[None, :, None]
    tok = jnp.arange(SEL_BLOCK)

    def sel_block(args):
        qb, ib, start = args
        flat = ib.reshape(bsz, N_KV_GROUPS, SEL_Q_BLOCK * top_k)
        kg = ks_b[b_ix, g_ix, flat].reshape(bsz, N_KV_GROUPS, SEL_Q_BLOCK, top_k, SEL_BLOCK, HEAD_DIM)
        vg = vs_b[b_ix, g_ix, flat].reshape(bsz, N_KV_GROUPS, SEL_Q_BLOCK, top_k * SEL_BLOCK, HEAD_DIM)
        s = jnp.einsum('bghqd,bgqkld->bghqkl', qb, kg).astype(f32)
        t = start + jnp.arange(SEL_Q_BLOCK)
        kpos = ib[..., None] * SEL_BLOCK + tok
        dist = (t[None, None, :, None, None] - kpos).astype(f32)
        s = s - slopes[None, :, :, None, None, None] * dist[:, :, None]
        s = s.reshape(bsz, N_KV_GROUPS, HEADS_PER_GROUP, SEL_Q_BLOCK, top_k * SEL_BLOCK)
        valid = (dist >= 0).reshape(bsz, N_KV_GROUPS, 1, SEL_Q_BLOCK, top_k * SEL_BLOCK)
        p = _masked_softmax(s, valid)
        return jnp.einsum('bghqm,bgqmd->bqghd', p, vg)

    o_sel = lax.map(sel_block, (q_blocks, idx_blocks, jnp.arange(nqb) * SEL_Q_BLOCK))
    o_sel = o_sel.transpose(1, 0, 2, 3, 4, 5).reshape(bsz, seq, N_KV_GROUPS, HEADS_PER_GROUP, HEAD_DIM)

    nq = seq // WIN_Q_BLOCK
    n_prev = WINDOW // WIN_Q_BLOCK
    pad = n_prev * WIN_Q_BLOCK
    kp = jnp.pad(k_win, ((0, 0), (pad, 0), (0, 0), (0, 0))).reshape(bsz, nq + n_prev, WIN_Q_BLOCK, N_KV_GROUPS, HEAD_DIM)
    vp = jnp.pad(v_win, ((0, 0), (pad, 0), (0, 0), (0, 0))).reshape(bsz, nq + n_prev, WIN_Q_BLOCK, N_KV_GROUPS, HEAD_DIM)
    k_band = jnp.concatenate([kp[:, i:i + nq] for i in range(n_prev + 1)], axis=2)
    v_band = jnp.concatenate([vp[:, i:i + nq] for i in range(n_prev + 1)], axis=2)
    qw = q.reshape(bsz, nq, WIN_Q_BLOCK, N_KV_GROUPS, HEADS_PER_GROUP, HEAD_DIM)
    s_w = jnp.einsum('bnqghd,bnkgd->bnghqk', qw, k_band).astype(f32)
    qpos = jnp.arange(nq)[:, None] * WIN_Q_BLOCK + jnp.arange(WIN_Q_BLOCK)[None, :]
    kpos = jnp.arange(nq)[:, None] * WIN_Q_BLOCK - pad + jnp.arange((n_prev + 1) * WIN_Q_BLOCK)[None, :]
    dist_w = qpos[:, :, None] - kpos[:, None, :]
    valid_w = (dist_w >= 0) & (dist_w < WINDOW) & (kpos[:, None, :] >= 0)
    s_w = s_w - slopes[None, None, :, :, None, None] * dist_w.astype(f32)[None, :, None, None]
    p_w = _masked_softmax(s_w, valid_w[None, :, None, None])
    o_win = jnp.einsum('bnghqk,bnkgd->bnqghd', p_w, v_band).reshape(bsz, seq, N_KV_GROUPS, HEADS_PER_GROUP, HEAD_DIM)

    bg = branch_gate.astype(f32)
    o = bg[..., 0:1] * o_cmp + bg[..., 1:2] * o_sel + bg[..., 2:3] * o_win
    return o.reshape(bsz, seq, ATTN_WIDTH)


def _hybrid_layer(x, c, w_ada, b_ada, w_in, ssm_a_re, ssm_a_im, ssm_log_dt, ssm_b_re, ssm_b_im,
                  ssm_c_re, ssm_c_im, ssm_d, w_glu_lin, w_glu_gate, cmp_pe_k, cmp_pe_v,
                  w_cmp_k1, w_cmp_k2, w_cmp_v1, w_cmp_v2, w_nsa_proj, w_out, ln1_g, ln1_b,
                  w_ffn_gate, w_ffn_up, w_ffn_down, ln2_g, ln2_b):
    bsz, seq, _ = x.shape
    mod = jax.nn.silu(c) @ w_ada + b_ada
    shift1, scale1, gate1, shift2, scale2, gate2 = [m[:, None, :] for m in jnp.split(mod, 6, axis=-1)]

    u = x * (1.0 + scale1) + shift1
    proj = u @ w_in
    offsets = []
    acc = 0
    for w in IN_SPLITS[:-1]:
        acc += w
        offsets.append(acc)
    u_ssm, q, kv, bgate, mgate = jnp.split(proj, offsets, axis=-1)

    y_ssm = _s5_scan(u_ssm, ssm_a_re, ssm_a_im, ssm_log_dt, ssm_b_re, ssm_b_im, ssm_c_re, ssm_c_im, ssm_d)
    g_ssm = jax.nn.gelu(y_ssm)
    z_ssm = (g_ssm @ w_glu_lin) * jax.nn.sigmoid(g_ssm @ w_glu_gate)

    q = q.reshape(bsz, seq, N_KV_GROUPS, HEADS_PER_GROUP, HEAD_DIM) * (HEAD_DIM ** -0.5)
    kv = kv.reshape(bsz, seq, N_KV_SETS, N_KV_GROUPS, HEAD_DIM)
    bgate = jax.nn.sigmoid(bgate.reshape(bsz, seq, N_KV_GROUPS, HEADS_PER_GROUP, N_NSA_BRANCHES))
    o_nsa = _nsa(q, kv[:, :, 0], kv[:, :, 1], kv[:, :, 2], kv[:, :, 3], kv[:, :, 4], kv[:, :, 5], bgate,
                 cmp_pe_k, cmp_pe_v, w_cmp_k1, w_cmp_k2, w_cmp_v1, w_cmp_v2)
    z_nsa = o_nsa.astype(x.dtype) @ w_nsa_proj

    m_ssm, m_nsa = jnp.split(mgate, 2, axis=-1)
    merged = jax.nn.sigmoid(m_ssm) * z_ssm + jax.nn.sigmoid(m_nsa) * z_nsa
    mix_out = merged @ w_out
    x = _layer_norm(DEEPNORM_ALPHA * x + (1.0 + gate1) * mix_out, ln1_g, ln1_b)

    u2 = x * (1.0 + scale2) + shift2
    ffn = (jax.nn.silu(u2 @ w_ffn_gate) * (u2 @ w_ffn_up)) @ w_ffn_down
    x = _layer_norm(DEEPNORM_ALPHA * x + (1.0 + gate2) * ffn, ln2_g, ln2_b)
    return x


def setup_inputs(seed: int = 0) -> dict:
    key = jax.random.key(seed)
    ks = jax.random.split(key, 32)
    f32 = jnp.float32
    L, G, P, C = DEPTH, SSM_GROUPS, SSM_STATE, SSM_GROUP_SIZE

    def nrm(k, shape, scale):
        return jax.random.normal(k, shape, f32) * scale

    return {
        'x': nrm(ks[0], (BATCH, SEQ, D_MODEL), 1.0),
        'c': nrm(ks[1], (BATCH, D_MODEL), 1.0),
        'w_ada': nrm(ks[2], (L, D_MODEL, 6 * D_MODEL), 0.1 * D_MODEL ** -0.5),
        'b_ada': nrm(ks[3], (L, 6 * D_MODEL), 0.01),
        'w_in': nrm(ks[4], (L, D_MODEL, IN_COLS), D_MODEL ** -0.5),
        'ssm_a_re': -0.5 + nrm(ks[5], (L, G, P), 0.01),
        'ssm_a_im': math.pi * jnp.arange(P, dtype=f32) + nrm(ks[6], (L, G, P), 0.01),
        'ssm_log_dt': jax.random.uniform(ks[7], (L, G), f32, math.log(1e-3), math.log(1e-1)),
        'ssm_b_re': nrm(ks[8], (L, G, P, C), C ** -0.5),
        'ssm_b_im': nrm(ks[9], (L, G, P, C), C ** -0.5),
        'ssm_c_re': nrm(ks[10], (L, G, C, P), P ** -0.5),
        'ssm_c_im': nrm(ks[11], (L, G, C, P), P ** -0.5),
        'ssm_d': nrm(ks[12], (L, G, C), 1.0),
        'w_glu_lin': nrm(ks[13], (L, SSM_WIDTH, D_MODEL), SSM_WIDTH ** -0.5),
        'w_glu_gate': nrm(ks[14], (L, SSM_WIDTH, D_MODEL), SSM_WIDTH ** -0.5),
        'cmp_pe_k': nrm(ks[15], (L, CMP_BLOCK, HEAD_DIM), 0.1),
        'cmp_pe_v': nrm(ks[16], (L, CMP_BLOCK, HEAD_DIM), 0.1),
        'w_cmp_k1': nrm(ks[17], (L, CMP_BLOCK, HEAD_DIM, CMP_HIDDEN), (CMP_BLOCK * HEAD_DIM) ** -0.5),
        'w_cmp_k2': nrm(ks[18], (L, CMP_HIDDEN, HEAD_DIM), CMP_HIDDEN ** -0.5),
        'w_cmp_v1': nrm(ks[19], (L, CMP_BLOCK, HEAD_DIM, CMP_HIDDEN), (CMP_BLOCK * HEAD_DIM) ** -0.5),
        'w_cmp_v2': nrm(ks[20], (L, CMP_HIDDEN, HEAD_DIM), CMP_HIDDEN ** -0.5),
        'w_nsa_proj': nrm(ks[21], (L, ATTN_WIDTH, D_MODEL), ATTN_WIDTH ** -0.5),
        'w_out': nrm(ks[22], (L, D_MODEL, D_MODEL), DEEPNORM_BETA * D_MODEL ** -0.5),
        'ln1_g': 1.0 + nrm(ks[23], (L, D_MODEL), 0.01),
        'ln1_b': nrm(ks[24], (L, D_MODEL), 0.01),
        'w_ffn_gate': nrm(ks[25], (L, D_MODEL, FFN_HIDDEN), D_MODEL ** -0.5),
        'w_ffn_up': nrm(ks[26], (L, D_MODEL, FFN_HIDDEN), D_MODEL ** -0.5),
        'w_ffn_down': nrm(ks[27], (L, FFN_HIDDEN, D_MODEL), DEEPNORM_BETA * FFN_HIDDEN ** -0.5),
        'ln2_g': 1.0 + nrm(ks[28], (L, D_MODEL), 0.01),
        'ln2_b': nrm(ks[29], (L, D_MODEL), 0.01),
    }


def reference(x, c, w_ada, b_ada, w_in, ssm_a_re, ssm_a_im, ssm_log_dt, ssm_b_re, ssm_b_im,
              ssm_c_re, ssm_c_im, ssm_d, w_glu_lin, w_glu_gate, cmp_pe_k, cmp_pe_v,
              w_cmp_k1, w_cmp_k2, w_cmp_v1, w_cmp_v2, w_nsa_proj, w_out, ln1_g, ln1_b,
              w_ffn_gate, w_ffn_up, w_ffn_down, ln2_g, ln2_b):
    for l in range(DEPTH):
        x = _hybrid_layer(x, c, w_ada[l], b_ada[l], w_in[l], ssm_a_re[l], ssm_a_im[l], ssm_log_dt[l],
                          ssm_b_re[l], ssm_b_im[l], ssm_c_re[l], ssm_c_im[l], ssm_d[l],
                          w_glu_lin[l], w_glu_gate[l], cmp_pe_k[l], cmp_pe_v[l],
                          w_cmp_k1[l], w_cmp_k2[l], w_cmp_v1[l], w_cmp_v2[l], w_nsa_proj[l], w_out[l],
                          ln1_g[l], ln1_b[l], w_ffn_gate[l], w_ffn_up[l], w_ffn_down[l], ln2_g[l], ln2_b[l])
    return x
```

```python
import functools
import math

import jax
import jax.numpy as jnp
from jax import lax
from jax.experimental import pallas as pl
from jax.experimental.pallas import tpu as pltpu

F32 = jnp.float32
BF16 = jnp.bfloat16
HIGHEST = lax.Precision.HIGHEST

D_MODEL = 1024
SSM_WIDTH = D_MODEL // 2
SSM_GROUP_SIZE = 16
SSM_GROUPS = SSM_WIDTH // SSM_GROUP_SIZE
SSM_STATE = 64
SSM_CHUNK = 16
SSM_GROUP_TILE = 8
N_HEADS = 8
HEAD_DIM = 64
N_KV_GROUPS = 2
HEADS_PER_GROUP = N_HEADS // N_KV_GROUPS
ATTN_WIDTH = N_HEADS * HEAD_DIM
KV_WIDTH = N_KV_GROUPS * HEAD_DIM
CMP_BLOCK = 32
CMP_STRIDE = 16
SEL_BLOCK = 64
SEL_TOP_K = 8
WINDOW = 256
N_NSA_BRANCHES = 3
GATE_ROWS = 16
FFN_HIDDEN = (8 * D_MODEL + 3 * 256 - 1) // (3 * 256) * 256
DEEPNORM_ALPHA = 2.0 ** 0.25
LN_EPS = 1e-5
MASK_VALUE = -1e30

VMEM_LIMIT = 56 * 1024 * 1024

NAT_COLS = SSM_WIDTH + 4 * KV_WIDTH
TR_ROWS = ATTN_WIDTH + 2 * KV_WIDTH + N_KV_GROUPS * GATE_ROWS


def _sigmoid(x):
    return 1.0 / (1.0 + jnp.exp(-x))


def _gelu(x):
    c = math.sqrt(2.0 / math.pi)
    return 0.5 * x * (1.0 + jnp.tanh(c * (x + 0.044715 * (x * x * x))))


def _layer_norm(y, gain, bias):
    mu = jnp.mean(y, axis=-1, keepdims=True)
    d = y - mu
    var = jnp.mean(d * d, axis=-1, keepdims=True)
    return d * lax.rsqrt(var + LN_EPS) * gain + bias


def _params(n_axes):
    return pltpu.CompilerParams(dimension_semantics=("arbitrary",) * n_axes, vmem_limit_bytes=VMEM_LIMIT)


def _ada_kernel(c_ref, w_ref, b_ref, o_ref):
    c = c_ref[...]
    a = c * _sigmoid(c)
    o_ref[0] = jnp.dot(a, w_ref[...], precision=HIGHEST, preferred_element_type=F32) + b_ref[...]


def _ada_mod(c, w_ada, b_ada):
    bsz = c.shape[0]
    return pl.pallas_call(
        _ada_kernel,
        grid=(6,),
        in_specs=[
            pl.BlockSpec((bsz, D_MODEL), lambda j: (0, 0)),
            pl.BlockSpec((D_MODEL, D_MODEL), lambda j: (0, j)),
            pl.BlockSpec((1, D_MODEL), lambda j: (0, j)),
        ],
        out_specs=pl.BlockSpec((1, bsz, D_MODEL), lambda j: (j, 0, 0)),
        out_shape=jax.ShapeDtypeStruct((6, bsz, D_MODEL), F32),
        compiler_params=_params(1),
        name="ada_mod",
    )(c, w_ada, b_ada.reshape(1, 6 * D_MODEL))


def _in_proj_kernel(x_ref, sc_ref, sh_ref, wn_ref, wt_ref, ussm_ref, kn_ref, qT_ref, vT_ref, bgT_ref, *, kb):
    u = (x_ref[0] * (1.0 + sc_ref[0]) + sh_ref[0]).astype(BF16)
    nat = jnp.dot(u, wn_ref[...], preferred_element_type=F32)
    ussm_ref[0] = nat[:, :SSM_WIDTH]
    for s in range(4):
        for g in range(N_KV_GROUPS):
            lo = SSM_WIDTH + s * KV_WIDTH + g * HEAD_DIM
            kn_ref[0, s, g] = nat[:, lo:lo + HEAD_DIM]
    tr = lax.dot_general(wt_ref[...], u, (((1,), (1,)), ((), ())), preferred_element_type=F32)
    qT_ref[0] = tr[:ATTN_WIDTH] * (HEAD_DIM ** -0.5)
    tm = u.shape[0]
    for s in range(2):
        for g in range(N_KV_GROUPS):
            lo = ATTN_WIDTH + s * KV_WIDTH + g * HEAD_DIM
            for c in range(tm // kb):
                vT_ref[0, s, g, c] = tr[lo:lo + HEAD_DIM, c * kb:(c + 1) * kb]
    bgT_ref[0] = _sigmoid(tr[ATTN_WIDTH + 2 * KV_WIDTH:])


def _in_proj(x, scale1, shift1, w_nat, w_tr, *, tm, kb):
    bsz, seq, _ = x.shape
    n_t = seq // tm
    out_shapes = (
        jax.ShapeDtypeStruct((bsz, seq, SSM_WIDTH), F32),
        jax.ShapeDtypeStruct((bsz, 4, N_KV_GROUPS, seq, HEAD_DIM), F32),
        jax.ShapeDtypeStruct((bsz, ATTN_WIDTH, seq), F32),
        jax.ShapeDtypeStruct((bsz, 2, N_KV_GROUPS, seq // kb, HEAD_DIM, kb), F32),
        jax.ShapeDtypeStruct((bsz, N_KV_GROUPS * GATE_ROWS, seq), F32),
    )
    return pl.pallas_call(
        functools.partial(_in_proj_kernel, kb=kb),
        grid=(bsz, n_t),
        in_specs=[
            pl.BlockSpec((1, tm, D_MODEL), lambda b, i: (b, i, 0)),
            pl.BlockSpec((1, 1, D_MODEL), lambda b, i: (b, 0, 0)),
            pl.BlockSpec((1, 1, D_MODEL), lambda b, i: (b, 0, 0)),
            pl.BlockSpec((D_MODEL, NAT_COLS), lambda b, i: (0, 0)),
            pl.BlockSpec((TR_ROWS, D_MODEL), lambda b, i: (0, 0)),
        ],
        out_specs=(
            pl.BlockSpec((1, tm, SSM_WIDTH), lambda b, i: (b, i, 0)),
            pl.BlockSpec((1, 4, N_KV_GROUPS, tm, HEAD_DIM), lambda b, i: (b, 0, 0, i, 0)),
            pl.BlockSpec((1, ATTN_WIDTH, tm), lambda b, i: (b, 0, i)),
            pl.BlockSpec((1, 2, N_KV_GROUPS, tm // kb, HEAD_DIM, kb), lambda b, i: (b, 0, 0, i, 0, 0)),
            pl.BlockSpec((1, N_KV_GROUPS * GATE_ROWS, tm), lambda b, i: (b, 0, i)),
        ),
        out_shape=out_shapes,
        compiler_params=_params(2),
        name="in_proj",
    )(x, scale1, shift1, w_nat, w_tr)


def _split_w_in(w_in):
    o_q = SSM_WIDTH
    o_kv = o_q + ATTN_WIDTH
    o_bg = o_kv + 6 * KV_WIDTH
    o_mg = o_bg + N_NSA_BRANCHES * N_HEADS
    kv = [w_in[:, o_kv + s * KV_WIDTH:o_kv + (s + 1) * KV_WIDTH] for s in range(6)]
    w_nat = jnp.concatenate([w_in[:, :o_q], kv[0], kv[1], kv[2], kv[4]], axis=1)
    per_group = HEADS_PER_GROUP * N_NSA_BRANCHES
    bg = w_in[:, o_bg:o_mg].reshape(D_MODEL, N_KV_GROUPS, per_group)
    bg = jnp.pad(bg, ((0, 0), (0, 0), (0, GATE_ROWS - per_group))).reshape(D_MODEL, N_KV_GROUPS * GATE_ROWS)
    w_tr = jnp.concatenate([w_in[:, o_q:o_kv], kv[3], kv[5], bg], axis=1).T
    return w_nat.astype(BF16), w_tr.astype(BF16), w_in[:, o_mg:].astype(BF16)


def _ssm_matrices(a_re, a_im, log_dt, b_re, b_im, c_re, c_im, d_skip):
    t_n = SSM_CHUNK
    hp = HIGHEST
    dt = jnp.exp(log_dt.astype(F32))[:, None]
    a_re = a_re.astype(F32)
    a_im = a_im.astype(F32)
    lam_re = a_re * dt
    lam_im = a_im * dt
    decay = jnp.exp(lam_re)
    ab_re = decay * jnp.cos(lam_im)
    ab_im = decay * jnp.sin(lam_im)
    denom = jnp.square(a_re) + jnp.square(a_im)
    n_re = ab_re - 1.0
    n_im = ab_im
    f_re = (n_re * a_re + n_im * a_im) / denom
    f_im = (n_im * a_re - n_re * a_im) / denom
    bb_re = f_re[:, :, None] * b_re - f_im[:, :, None] * b_im
    bb_im = f_re[:, :, None] * b_im + f_im[:, :, None] * b_re
    j = jnp.arange(t_n + 1, dtype=F32)[:, None, None]
    mag = jnp.exp(lam_re[None] * j)
    pw_re = mag * jnp.cos(lam_im[None] * j)
    pw_im = mag * jnp.sin(lam_im[None] * j)
    ca_re = c_re[None] * pw_re[:, :, None, :] - c_im[None] * pw_im[:, :, None, :]
    ca_im = c_re[None] * pw_im[:, :, None, :] + c_im[None] * pw_re[:, :, None, :]
    k_lag = (jnp.einsum('jgcp,gpi->jgci', ca_re[:t_n], bb_re, precision=hp)
             - jnp.einsum('jgcp,gpi->jgci', ca_im[:t_n], bb_im, precision=hp))
    s_ix = jnp.arange(t_n)[:, None]
    t_ix = jnp.arange(t_n)[None, :]
    lag = t_ix - s_ix
    m5 = jnp.where((lag >= 0)[:, :, None, None, None], k_lag[jnp.clip(lag, 0, t_n - 1)], 0.0)
    eye_c = jnp.eye(SSM_GROUP_SIZE, dtype=F32)
    skip = (s_ix == t_ix).astype(F32)[:, :, None, None, None] * (d_skip[None, None, :, :, None] * eye_c)
    m = (m5 + skip).transpose(2, 0, 4, 1, 3).reshape(SSM_GROUPS, t_n * SSM_GROUP_SIZE, t_n * SSM_GROUP_SIZE)
    rev = pw_re[t_n - 1 - jnp.arange(t_n)], pw_im[t_n - 1 - jnp.arange(t_n)]
    p_re = rev[0][:, :, :, None] * bb_re[None] - rev[1][:, :, :, None] * bb_im[None]
    p_im = rev[0][:, :, :, None] * bb_im[None] + rev[1][:, :, :, None] * bb_re[None]
    p_re = p_re.transpose(1, 0, 3, 2).reshape(SSM_GROUPS, t_n * SSM_GROUP_SIZE, SSM_STATE)
    p_im = p_im.transpose(1, 0, 3, 2).reshape(SSM_GROUPS, t_n * SSM_GROUP_SIZE, SSM_STATE)
    q_re = ca_re[1:].transpose(1, 3, 0, 2).reshape(SSM_GROUPS, SSM_STATE, t_n * SSM_GROUP_SIZE)
    q_im = (-ca_im[1:]).transpose(1, 3, 0, 2).reshape(SSM_GROUPS, SSM_STATE, t_n * SSM_GROUP_SIZE)
    return m, p_re, p_im, q_re, q_im, pw_re[t_n], pw_im[t_n]


def _ssm_kernel(u_ref, m_ref, pre_ref, pim_ref, qre_ref, qim_ref, are_ref, aim_ref, o_ref, sre, sim, *, n_chunks):
    n_g = u_ref.shape[1]

    def group_rows(g):
        return pl.ds(pl.multiple_of(g * n_chunks, n_chunks), n_chunks)

    def chunk_states(g, carry):
        u = u_ref[0, g]
        sre[group_rows(g), :] = jnp.dot(u, pre_ref[g], precision=HIGHEST, preferred_element_type=F32)
        sim[group_rows(g), :] = jnp.dot(u, pim_ref[g], precision=HIGHEST, preferred_element_type=F32)
        return carry

    lax.fori_loop(0, n_g, chunk_states, 0)

    a_r = are_ref[0]
    a_i = aim_ref[0]

    def carry_states(c, h):
        h_r, h_i = h
        rows = pl.ds(c, n_g, stride=n_chunks)
        s_r = sre[rows, :]
        s_i = sim[rows, :]
        sre[rows, :] = h_r
        sim[rows, :] = h_i
        return a_r * h_r - a_i * h_i + s_r, a_r * h_i + a_i * h_r + s_i

    zero = jnp.zeros((n_g, SSM_STATE), F32)
    lax.fori_loop(0, n_chunks, carry_states, (zero, zero))

    def outputs(g, carry):
        u = u_ref[0, g]
        y = jnp.dot(u, m_ref[g], precision=HIGHEST, preferred_element_type=F32)
        y = y + jnp.dot(sre[group_rows(g), :], qre_ref[g], precision=HIGHEST, preferred_element_type=F32)
        y = y + jnp.dot(sim[group_rows(g), :], qim_ref[g], precision=HIGHEST, preferred_element_type=F32)
        o_ref[0, g] = _gelu(y)
        return carry

    lax.fori_loop(0, n_g, outputs, 0)


def _ssm(u_chunks, mats):
    bsz, n_groups, n_chunks, width = u_chunks.shape
    m, p_re, p_im, q_re, q_im, a_re, a_im = mats
    gt = SSM_GROUP_TILE
    a_re = a_re.reshape(n_groups // gt, gt, SSM_STATE)
    a_im = a_im.reshape(n_groups // gt, gt, SSM_STATE)
    per_tile = lambda a: pl.BlockSpec((gt,) + a.shape[1:], lambda t, b: (t, 0, 0))
    per_tile1 = lambda a: pl.BlockSpec((1,) + a.shape[1:], lambda t, b: (t, 0, 0))
    return pl.pallas_call(
        functools.partial(_ssm_kernel, n_chunks=n_chunks),
        grid=(n_groups // gt, bsz),
        in_specs=[
            pl.BlockSpec((1, gt, n_chunks, width), lambda t, b: (b, t, 0, 0)),
            per_tile(m), per_tile(p_re), per_tile(p_im), per_tile(q_re), per_tile(q_im),
            per_tile1(a_re), per_tile1(a_im),
        ],
        out_specs=pl.BlockSpec((1, gt, n_chunks, width), lambda t, b: (b, t, 0, 0)),
        out_shape=jax.ShapeDtypeStruct(u_chunks.shape, F32),
        scratch_shapes=[pltpu.VMEM((gt * n_chunks, SSM_STATE), F32), pltpu.VMEM((gt * n_chunks, SSM_STATE), F32)],
        compiler_params=_params(2),
        name="ssm",
    )(u_chunks, m, p_re, p_im, q_re, q_im, a_re, a_im)


def _compress_kernel(kx_ref, vx_ref, pek_ref, pev_ref, wk1_ref, wk2_ref, wv1_ref, wv2_ref, kc_ref, vcT_ref):
    half = CMP_STRIDE * HEAD_DIM

    def mlp(x, pe, w1_ref, w2_ref):
        n = x.shape[0]
        first = jnp.dot(x, w1_ref[:half, :], precision=HIGHEST, preferred_element_type=F32)
        second = jnp.dot(x, w1_ref[half:, :], precision=HIGHEST, preferred_element_type=F32)
        bias = jnp.dot(pe, w1_ref[...], precision=HIGHEST, preferred_element_type=F32)
        h = _gelu(first + pltpu.roll(second, n - 1, 0) + bias)
        return jnp.dot(h, w2_ref[...], precision=HIGHEST, preferred_element_type=F32)

    kc_ref[0, 0] = mlp(kx_ref[0, 0], pek_ref[...], wk1_ref, wk2_ref)
    vcT_ref[0, 0] = mlp(vx_ref[0, 0], pev_ref[...], wv1_ref, wv2_ref).T


def _compress(kx, vx, pe_k, pe_v, wk1, wk2, wv1, wv2):
    bsz, n_g, n_chunks, width = kx.shape
    blk = pl.BlockSpec((1, 1, n_chunks, width), lambda b, g: (b, g, 0, 0))
    full = lambda a: pl.BlockSpec(a.shape, lambda b, g: (0, 0))
    return pl.pallas_call(
        _compress_kernel,
        grid=(bsz, n_g),
        in_specs=[blk, blk, full(pe_k), full(pe_v), full(wk1), full(wk2), full(wv1), full(wv2)],
        out_specs=(
            pl.BlockSpec((1, 1, n_chunks, HEAD_DIM), lambda b, g: (b, g, 0, 0)),
            pl.BlockSpec((1, 1, HEAD_DIM, n_chunks), lambda b, g: (b, g, 0, 0)),
        ),
        out_shape=(
            jax.ShapeDtypeStruct((bsz, n_g, n_chunks, HEAD_DIM), F32),
            jax.ShapeDtypeStruct((bsz, n_g, HEAD_DIM, n_chunks), F32),
        ),
        compiler_params=_params(2),
        name="compress",
    )(kx, vx, pe_k, pe_v, wk1, wk2, wv1, wv2)


def _nsa_kernel(qT_ref, kc_ref, vcT_ref, ks_ref, vsT_ref, kw_ref, vwT_ref, bg_ref, o_ref,
                sel_scr, m_scr, l_scr, acc_scr, out_scr, *, tq, n_cmp, n_sel):
    g = pl.program_id(1)
    i = pl.program_id(2)
    kb_size = tq
    sel_per_kb = kb_size // SEL_BLOCK
    t_row = i * tq + lax.broadcasted_iota(jnp.int32, (1, tq), 1)
    slopes = [jnp.where(g == 0, 2.0 ** -(hh + 1), 2.0 ** -(HEADS_PER_GROUP + hh + 1)).astype(F32)
              for hh in range(HEADS_PER_GROUP)]

    def gate(hh, branch):
        r = hh * N_NSA_BRANCHES + branch
        return bg_ref[0, 0, r:r + 1, :]

    def head_rows(hh):
        return slice(hh * HEAD_DIM, (hh + 1) * HEAD_DIM)

    kc = kc_ref[0, 0]
    vcT = vcT_ref[0, 0]
    cmp_end = lax.broadcasted_iota(jnp.int32, (n_cmp, 1), 0) * CMP_STRIDE + (CMP_BLOCK - 1)
    dist_c = (t_row - cmp_end).astype(F32)
    valid_c = dist_c >= 0.0
    p_sum = jnp.zeros((n_cmp, tq), F32)
    for hh in range(HEADS_PER_GROUP):
        q_h = qT_ref[0, head_rows(hh), :]
        s = jnp.dot(kc, q_h, precision=HIGHEST, preferred_element_type=F32) - slopes[hh] * dist_c
        s = jnp.where(valid_c, s, MASK_VALUE)
        m = jnp.max(s, axis=0, keepdims=True)
        p = jnp.where(valid_c, jnp.exp(s - m), 0.0)
        l = jnp.sum(p, axis=0, keepdims=True)
        p = p / jnp.where(l > 0.0, l, 1.0)
        o_c = jnp.dot(vcT, p, precision=HIGHEST, preferred_element_type=F32)
        out_scr[head_rows(hh), :] = gate(hh, 0) * o_c
        p_sum = p_sum + p

    blk = lax.broadcasted_iota(jnp.int32, (n_sel, n_cmp), 0)
    cmp_ix = lax.broadcasted_iota(jnp.int32, (n_sel, n_cmp), 1)
    ratio = SEL_BLOCK // CMP_STRIDE
    extra = CMP_BLOCK // CMP_STRIDE - 1
    overlap_t = ((cmp_ix >= ratio * blk - extra) & (cmp_ix <= ratio * blk + ratio - 1)).astype(F32)
    imp = jnp.dot(overlap_t, p_sum, precision=HIGHEST, preferred_element_type=F32)

    j_col = lax.broadcasted_iota(jnp.int32, (n_sel, 1), 0)
    forced = (j_col == 0) | (j_col == lax.shift_right_logical(t_row, SEL_BLOCK.bit_length() - 1))
    future = j_col * SEL_BLOCK > t_row
    score = jnp.where(forced, jnp.inf, jnp.where(future, -jnp.inf, imp))
    rank = jnp.zeros((n_sel, tq), F32)
    for jp in range(n_sel):
        other = score[jp:jp + 1, :]
        ahead = (other > score) | ((other == score) & (j_col > jp))
        rank = rank + jnp.where(ahead, 1.0, 0.0)
    sel = jnp.where(rank < float(min(SEL_TOP_K, n_sel)), 1.0, 0.0)
    for c in range(n_sel // sel_per_kb):
        sel_scr[c] = sel[c * sel_per_kb:(c + 1) * sel_per_kb, :]

    def attend(k_ref, vT_ref, kb_lo, kb_hi, branch):
        m_scr[...] = jnp.full(m_scr.shape, MASK_VALUE, F32)
        l_scr[...] = jnp.zeros(l_scr.shape, F32)
        acc_scr[...] = jnp.zeros(acc_scr.shape, F32)

        def body(kb, carry):
            k0 = pl.multiple_of(kb * kb_size, kb_size)
            k = k_ref[0, 0, 0, pl.ds(k0, kb_size), :].astype(BF16)
            vT = vT_ref[0, 0, 0, kb].astype(BF16)
            k_pos = k0 + lax.broadcasted_iota(jnp.int32, (kb_size, 1), 0)
            dist_i = t_row - k_pos
            dist = dist_i.astype(F32)
            if branch == 1:
                rows = sel_scr[kb]
                chosen = jnp.concatenate(
                    [jnp.broadcast_to(rows[r:r + 1, :], (SEL_BLOCK, tq)) for r in range(sel_per_kb)], axis=0)
                valid = (chosen > 0.5) & (dist_i >= 0)
            else:
                valid = (dist_i >= 0) & (dist_i < WINDOW)
            for hh in range(HEADS_PER_GROUP):
                q_h = qT_ref[0, head_rows(hh), :].astype(BF16)
                s = jnp.dot(k, q_h, preferred_element_type=F32) - slopes[hh] * dist
                s = jnp.where(valid, s, MASK_VALUE)
                m_old = m_scr[hh]
                m_new = jnp.maximum(m_old, jnp.max(s, axis=0, keepdims=True))
                alpha = jnp.exp(m_old - m_new)
                p = jnp.where(valid, jnp.exp(s - m_new), 0.0)
                l_scr[hh] = alpha * l_scr[hh] + jnp.sum(p, axis=0, keepdims=True)
                acc_scr[hh] = alpha * acc_scr[hh] + jnp.dot(vT, p.astype(BF16), preferred_element_type=F32)
                m_scr[hh] = m_new
            return carry

        lax.fori_loop(kb_lo, kb_hi, body, 0)
        for hh in range(HEADS_PER_GROUP):
            l = l_scr[hh]
            o_b = acc_scr[hh] / jnp.where(l > 0.0, l, 1.0)
            out_scr[head_rows(hh), :] = out_scr[head_rows(hh), :] + gate(hh, branch) * o_b

    attend(ks_ref, vsT_ref, 0, i + 1, 1)
    attend(kw_ref, vwT_ref, jnp.maximum(i - WINDOW // kb_size, 0), i + 1, 2)

    o_ref[0] = out_scr[...].T


def _nsa(qT, kc, vcT, k_nat, vT_blocks, bgT, *, tq):
    bsz, _, seq = qT.shape
    n_cmp = kc.shape[2]
    n_sel = seq // SEL_BLOCK
    n_kb = seq // tq
    gw = HEADS_PER_GROUP * HEAD_DIM
    bg4 = bgT.reshape(bsz, N_KV_GROUPS, GATE_ROWS, seq)
    return pl.pallas_call(
        functools.partial(_nsa_kernel, tq=tq, n_cmp=n_cmp, n_sel=n_sel),
        grid=(bsz, N_KV_GROUPS, seq // tq),
        in_specs=[
            pl.BlockSpec((1, gw, tq), lambda b, g, i: (b, g, i)),
            pl.BlockSpec((1, 1, n_cmp, HEAD_DIM), lambda b, g, i: (b, g, 0, 0)),
            pl.BlockSpec((1, 1, HEAD_DIM, n_cmp), lambda b, g, i: (b, g, 0, 0)),
            pl.BlockSpec((1, 1, 1, seq, HEAD_DIM), lambda b, g, i: (b, 2, g, 0, 0)),
            pl.BlockSpec((1, 1, 1, n_kb, HEAD_DIM, tq), lambda b, g, i: (b, 0, g, 0, 0, 0)),
            pl.BlockSpec((1, 1, 1, seq, HEAD_DIM), lambda b, g, i: (b, 3, g, 0, 0)),
            pl.BlockSpec((1, 1, 1, n_kb, HEAD_DIM, tq), lambda b, g, i: (b, 1, g, 0, 0, 0)),
            pl.BlockSpec((1, 1, GATE_ROWS, tq), lambda b, g, i: (b, g, 0, i)),
        ],
        out_specs=pl.BlockSpec((1, tq, gw), lambda b, g, i: (b, i, g)),
        out_shape=jax.ShapeDtypeStruct((bsz, seq, ATTN_WIDTH), F32),
        scratch_shapes=[
            pltpu.VMEM((n_kb, tq // SEL_BLOCK, tq), F32),
            pltpu.VMEM((HEADS_PER_GROUP, 1, tq), F32),
            pltpu.VMEM((HEADS_PER_GROUP, 1, tq), F32),
            pltpu.VMEM((HEADS_PER_GROUP, HEAD_DIM, tq), F32),
            pltpu.VMEM((gw, tq), F32),
        ],
        compiler_params=_params(3),
        name="nsa",
    )(qT, kc, vcT, k_nat, vT_blocks, k_nat, vT_blocks, bg4)


def _merge_kernel(x_ref, g_ref, o_ref_in, sc_ref, sh_ref, gt_ref, wm_ref, wl_ref, wg_ref, wn_ref, wo_ref,
                  lg_ref, lb_ref, y_ref):
    x = x_ref[0]
    u = (x * (1.0 + sc_ref[0]) + sh_ref[0]).astype(BF16)
    mg = jnp.dot(u, wm_ref[...], preferred_element_type=F32)
    g = g_ref[0].astype(BF16)
    z_ssm = jnp.dot(g, wl_ref[...], preferred_element_type=F32) * _sigmoid(
        jnp.dot(g, wg_ref[...], preferred_element_type=F32))
    z_nsa = jnp.dot(o_ref_in[0].astype(BF16), wn_ref[...], preferred_element_type=F32)
    merged = _sigmoid(mg[:, :D_MODEL]) * z_ssm + _sigmoid(mg[:, D_MODEL:]) * z_nsa
    mix = jnp.dot(merged.astype(BF16), wo_ref[...], preferred_element_type=F32)
    y = DEEPNORM_ALPHA * x + (1.0 + gt_ref[0]) * mix
    y_ref[0] = _layer_norm(y, lg_ref[...], lb_ref[...])


def _merge(x, g_ssm, o_nsa, scale1, shift1, gate1, w_m, w_lin, w_gate, w_nsa, w_out, ln_g, ln_b, *, tm):
    bsz, seq, _ = x.shape
    row = pl.BlockSpec((1, 1, D_MODEL), lambda b, i: (b, 0, 0))
    full = lambda a: pl.BlockSpec(a.shape, lambda b, i: (0, 0))
    return pl.pallas_call(
        _merge_kernel,
        grid=(bsz, seq // tm),
        in_specs=[
            pl.BlockSpec((1, tm, D_MODEL), lambda b, i: (b, i, 0)),
            pl.BlockSpec((1, tm, SSM_WIDTH), lambda b, i: (b, i, 0)),
            pl.BlockSpec((1, tm, ATTN_WIDTH), lambda b, i: (b, i, 0)),
            row, row, row,
            full(w_m), full(w_lin), full(w_gate), full(w_nsa), full(w_out), full(ln_g), full(ln_b),
        ],
        out_specs=pl.BlockSpec((1, tm, D_MODEL), lambda b, i: (b, i, 0)),
        out_shape=jax.ShapeDtypeStruct(x.shape, F32),
        compiler_params=_params(2),
        name="merge",
    )(x, g_ssm, o_nsa, scale1, shift1, gate1, w_m, w_lin, w_gate, w_nsa, w_out, ln_g, ln_b)


def _ffn_kernel(x_ref, sc_ref, sh_ref, gt_ref, wg_ref, wu_ref, wd_ref, lg_ref, lb_ref, y_ref, u_scr, acc_scr):
    j = pl.program_id(2)

    @pl.when(j == 0)
    def _():
        u_scr[...] = (x_ref[0] * (1.0 + sc_ref[0]) + sh_ref[0]).astype(BF16)
        acc_scr[...] = jnp.zeros(acc_scr.shape, F32)

    u = u_scr[...]
    a = jnp.dot(u, wg_ref[...], preferred_element_type=F32)
    h = (a * _sigmoid(a)) * jnp.dot(u, wu_ref[...], preferred_element_type=F32)
    acc_scr[...] += jnp.dot(h.astype(BF16), wd_ref[...], preferred_element_type=F32)

    @pl.when(j == pl.num_programs(2) - 1)
    def _():
        y = DEEPNORM_ALPHA * x_ref[0] + (1.0 + gt_ref[0]) * acc_scr[...]
        y_ref[0] = _layer_norm(y, lg_ref[...], lb_ref[...])


def _ffn(x, scale2, shift2, gate2, w_gate, w_up, w_down, ln_g, ln_b, *, tm, th):
    bsz, seq, _ = x.shape
    row = pl.BlockSpec((1, 1, D_MODEL), lambda b, i, j: (b, 0, 0))
    vec = pl.BlockSpec((1, D_MODEL), lambda b, i, j: (0, 0))
    return pl.pallas_call(
        _ffn_kernel,
        grid=(bsz, seq // tm, FFN_HIDDEN // th),
        in_specs=[
            pl.BlockSpec((1, tm, D_MODEL), lambda b, i, j: (b, i, 0)),
            row, row, row,
            pl.BlockSpec((D_MODEL, th), lambda b, i, j: (0, j)),
            pl.BlockSpec((D_MODEL, th), lambda b, i, j: (0, j)),
            pl.BlockSpec((th, D_MODEL), lambda b, i, j: (j, 0)),
            vec, vec,
        ],
        out_specs=pl.BlockSpec((1, tm, D_MODEL), lambda b, i, j: (b, i, 0)),
        out_shape=jax.ShapeDtypeStruct(x.shape, F32),
        scratch_shapes=[pltpu.VMEM((tm, D_MODEL), BF16), pltpu.VMEM((tm, D_MODEL), F32)],
        compiler_params=_params(3),
        name="ffn",
    )(x, scale2, shift2, gate2, w_gate, w_up, w_down, ln_g, ln_b)


def _layer(x, c, w_ada, b_ada, w_in, ssm_a_re, ssm_a_im, ssm_log_dt, ssm_b_re, ssm_b_im, ssm_c_re, ssm_c_im,
           ssm_d, w_glu_lin, w_glu_gate, cmp_pe_k, cmp_pe_v, w_cmp_k1, w_cmp_k2, w_cmp_v1, w_cmp_v2,
           w_nsa_proj, w_out, ln1_g, ln1_b, w_ffn_gate, w_ffn_up, w_ffn_down, ln2_g, ln2_b):
    bsz, seq, _ = x.shape
    n_chunks = seq // SSM_CHUNK
    tq = min(256, seq)
    tm = min(512, seq)

    mod = _ada_mod(c, w_ada, b_ada).reshape(6, bsz, 1, D_MODEL)
    shift1, scale1, gate1, shift2, scale2, gate2 = (mod[k] for k in range(6))

    w_nat, w_tr, w_mg = _split_w_in(w_in)
    u_ssm, k_nat, qT, vT_blocks, bgT = _in_proj(x, scale1, shift1, w_nat, w_tr, tm=tm, kb=tq)

    u_chunks = (u_ssm.reshape(bsz, n_chunks, SSM_CHUNK, SSM_GROUPS, SSM_GROUP_SIZE)
                .transpose(0, 3, 1, 2, 4).reshape(bsz, SSM_GROUPS, n_chunks, SSM_CHUNK * SSM_GROUP_SIZE))
    mats = _ssm_matrices(ssm_a_re, ssm_a_im, ssm_log_dt, ssm_b_re, ssm_b_im, ssm_c_re, ssm_c_im, ssm_d)
    g_chunks = _ssm(u_chunks, mats)
    g_ssm = (g_chunks.reshape(bsz, SSM_GROUPS, n_chunks, SSM_CHUNK, SSM_GROUP_SIZE)
             .transpose(0, 2, 3, 1, 4).reshape(bsz, seq, SSM_WIDTH))

    chunk_w = CMP_STRIDE * HEAD_DIM
    kx = k_nat[:, 0].reshape(bsz, N_KV_GROUPS, seq // CMP_STRIDE, chunk_w)
    vx = k_nat[:, 1].reshape(bsz, N_KV_GROUPS, seq // CMP_STRIDE, chunk_w)
    kc, vcT = _compress(kx, vx,
                        cmp_pe_k.reshape(1, CMP_BLOCK * HEAD_DIM), cmp_pe_v.reshape(1, CMP_BLOCK * HEAD_DIM),
                        w_cmp_k1.reshape(CMP_BLOCK * HEAD_DIM, -1), w_cmp_k2,
                        w_cmp_v1.reshape(CMP_BLOCK * HEAD_DIM, -1), w_cmp_v2)
    o_nsa = _nsa(qT, kc, vcT, k_nat, vT_blocks, bgT, tq=tq)

    x1 = _merge(x, g_ssm, o_nsa, scale1, shift1, gate1, w_mg,
                w_glu_lin.astype(BF16), w_glu_gate.astype(BF16), w_nsa_proj.astype(BF16), w_out.astype(BF16),
                ln1_g.reshape(1, D_MODEL), ln1_b.reshape(1, D_MODEL), tm=tm)
    return _ffn(x1, scale2, shift2, gate2, w_ffn_gate.astype(BF16), w_ffn_up.astype(BF16),
                w_ffn_down.astype(BF16), ln2_g.reshape(1, D_MODEL), ln2_b.reshape(1, D_MODEL), tm=tm, th=256)


def kernel(x, c, w_ada, b_ada, w_in, ssm_a_re, ssm_a_im, ssm_log_dt, ssm_b_re, ssm_b_im, ssm_c_re, ssm_c_im,
           ssm_d, w_glu_lin, w_glu_gate, cmp_pe_k, cmp_pe_v, w_cmp_k1, w_cmp_k2, w_cmp_v1, w_cmp_v2,
           w_nsa_proj, w_out, ln1_g, ln1_b, w_ffn_gate, w_ffn_up, w_ffn_down, ln2_g, ln2_b):
    for l in range(w_ada.shape[0]):
        x = _layer(x, c, w_ada[l], b_ada[l], w_in[l], ssm_a_re[l], ssm_a_im[l], ssm_log_dt[l],
                   ssm_b_re[l], ssm_b_im[l], ssm_c_re[l], ssm_c_im[l], ssm_d[l],
                   w_glu_lin[l], w_glu_gate[l], cmp_pe_k[l], cmp_pe_v[l],
                   w_cmp_k1[l], w_cmp_k2[l], w_cmp_v1[l], w_cmp_v2[l], w_nsa_proj[l], w_out[l],
                   ln1_g[l], ln1_b[l], w_ffn_gate[l], w_ffn_up[l], w_ffn_down[l], ln2_g[l], ln2_b[l])
    return x
```

```python
import functools
import math

import jax
import jax.numpy as jnp
from jax import lax
from jax.experimental import pallas as pl
from jax.experimental.pallas import tpu as pltpu

F32 = jnp.float32
BF16 = jnp.bfloat16
HIGHEST = lax.Precision.HIGHEST

D_MODEL = 1024
SSM_WIDTH = D_MODEL // 2
SSM_GROUP_SIZE = 16
SSM_GROUPS = SSM_WIDTH // SSM_GROUP_SIZE
SSM_STATE = 64
SSM_CHUNK = 16
SSM_GROUP_TILE = 8
N_HEADS = 8
HEAD_DIM = 64
N_KV_GROUPS = 2
HEADS_PER_GROUP = N_HEADS // N_KV_GROUPS
ATTN_WIDTH = N_HEADS * HEAD_DIM
KV_WIDTH = N_KV_GROUPS * HEAD_DIM
CMP_BLOCK = 32
CMP_STRIDE = 16
SEL_BLOCK = 64
SEL_TOP_K = 8
WINDOW = 256
N_NSA_BRANCHES = 3
GATE_ROWS = 16
FFN_HIDDEN = (8 * D_MODEL + 3 * 256 - 1) // (3 * 256) * 256
DEEPNORM_ALPHA = 2.0 ** 0.25
LN_EPS = 1e-5
MASK_VALUE = -1e30

VMEM_LIMIT = 56 * 1024 * 1024

KEY_BLOCK = 256
SEL_SHIFT = SEL_BLOCK.bit_length() - 1
SEL_PER_KEY_BLOCK = KEY_BLOCK // SEL_BLOCK
KEY_AUG = 128
ALIBI_COL = HEAD_DIM
N_PIECES = 4
MASK_COL = HEAD_DIM + 16
QUERY_AUG = HEAD_DIM + 32
VAL_AUG = HEAD_DIM + 16
LOG2E = math.log2(math.e)

NAT_COLS = SSM_WIDTH + 2 * KV_WIDTH + 2 * N_KV_GROUPS * KEY_AUG
TR_ROWS = ATTN_WIDTH + 2 * N_KV_GROUPS * VAL_AUG + N_KV_GROUPS * GATE_ROWS


def _bf16_pieces(value, n):
    pieces = []
    rest = value
    for _ in range(n):
        mant, expo = math.frexp(rest)
        piece = math.ldexp(round(mant * 256.0) / 256.0, expo)
        pieces.append(piece)
        rest -= piece
    return pieces


def _sigmoid(x):
    return 1.0 / (1.0 + jnp.exp(-x))


def _gelu(x):
    c = math.sqrt(2.0 / math.pi)
    return 0.5 * x * (1.0 + jnp.tanh(c * (x + 0.044715 * (x * x * x))))


def _layer_norm(y, gain, bias):
    mu = jnp.mean(y, axis=-1, keepdims=True)
    d = y - mu
    var = jnp.mean(d * d, axis=-1, keepdims=True)
    return d * lax.rsqrt(var + LN_EPS) * gain + bias


def _params(n_axes):
    return pltpu.CompilerParams(dimension_semantics=("arbitrary",) * n_axes, vmem_limit_bytes=VMEM_LIMIT)


def _ada_kernel(c_ref, w_ref, b_ref, o_ref):
    c = c_ref[...]
    a = c * _sigmoid(c)
    o_ref[0] = jnp.dot(a, w_ref[...], precision=HIGHEST, preferred_element_type=F32) + b_ref[...]


def _ada_mod(c, w_ada, b_ada):
    bsz = c.shape[0]
    return pl.pallas_call(
        _ada_kernel,
        grid=(6,),
        in_specs=[
            pl.BlockSpec((bsz, D_MODEL), lambda j: (0, 0)),
            pl.BlockSpec((D_MODEL, D_MODEL), lambda j: (0, j)),
            pl.BlockSpec((1, D_MODEL), lambda j: (0, j)),
        ],
        out_specs=pl.BlockSpec((1, bsz, D_MODEL), lambda j: (j, 0, 0)),
        out_shape=jax.ShapeDtypeStruct((6, bsz, D_MODEL), F32),
        compiler_params=_params(1),
        name="ada_mod",
    )(c, w_ada, b_ada.reshape(1, 6 * D_MODEL))


def _in_proj_kernel(x_ref, sc_ref, sh_ref, wn_ref, wt_ref, ussm_ref, kc_ref, kp_ref, qT_ref, vp_ref, bgT_ref):
    i = pl.program_id(1)
    tm = x_ref.shape[1]
    u = (x_ref[0] * (1.0 + sc_ref[0]) + sh_ref[0]).astype(BF16)
    nat = jnp.dot(u, wn_ref[...], preferred_element_type=F32)
    ussm_ref[0] = nat[:, :SSM_WIDTH]
    for s in range(2):
        for g in range(N_KV_GROUPS):
            lo = SSM_WIDTH + s * KV_WIDTH + g * HEAD_DIM
            kc_ref[0, s, g] = nat[:, lo:lo + HEAD_DIM]
    pos = i * tm + lax.broadcasted_iota(jnp.int32, (tm, KEY_AUG), 0)
    col = lax.broadcasted_iota(jnp.int32, (tm, KEY_AUG), 1)
    blk = lax.shift_right_logical(pos, SEL_SHIFT)
    in_a = (col >= ALIBI_COL) & (col < ALIBI_COL + N_PIECES)
    in_b = (col >= ALIBI_COL + N_PIECES) & (col < ALIBI_COL + 2 * N_PIECES)
    hot = col == MASK_COL + (blk & (SEL_PER_KEY_BLOCK - 1))
    aux = jnp.where(in_a, blk * SEL_BLOCK, jnp.where(in_b, pos & (SEL_BLOCK - 1), jnp.where(hot, 1, 0))).astype(F32)
    for s in range(2):
        for g in range(N_KV_GROUPS):
            lo = SSM_WIDTH + 2 * KV_WIDTH + (s * N_KV_GROUPS + g) * KEY_AUG
            kp_ref[0, s, g] = (nat[:, lo:lo + KEY_AUG] + aux).astype(BF16)
    tr = lax.dot_general(wt_ref[...], u, (((1,), (1,)), ((), ())), preferred_element_type=F32)
    qT_ref[0] = tr[:ATTN_WIDTH] * (HEAD_DIM ** -0.5)
    ones_row = jnp.where(lax.broadcasted_iota(jnp.int32, (VAL_AUG, 1), 0) == HEAD_DIM, 1.0, 0.0)
    for s in range(2):
        for g in range(N_KV_GROUPS):
            lo = ATTN_WIDTH + (s * N_KV_GROUPS + g) * VAL_AUG
            v_aug = (tr[lo:lo + VAL_AUG] + ones_row).astype(BF16)
            for c in range(tm // KEY_BLOCK):
                vp_ref[0, s, g, c] = v_aug[:, c * KEY_BLOCK:(c + 1) * KEY_BLOCK]
    bgT_ref[0] = _sigmoid(tr[ATTN_WIDTH + 2 * N_KV_GROUPS * VAL_AUG:])


def _in_proj(x, scale1, shift1, w_nat, w_tr, *, tm):
    bsz, seq, _ = x.shape
    n_t = seq // tm
    n_kb = seq // KEY_BLOCK
    out_shapes = (
        jax.ShapeDtypeStruct((bsz, seq, SSM_WIDTH), F32),
        jax.ShapeDtypeStruct((bsz, 2, N_KV_GROUPS, seq, HEAD_DIM), F32),
        jax.ShapeDtypeStruct((bsz, 2, N_KV_GROUPS, seq, KEY_AUG), BF16),
        jax.ShapeDtypeStruct((bsz, ATTN_WIDTH, seq), F32),
        jax.ShapeDtypeStruct((bsz, 2, N_KV_GROUPS, n_kb, VAL_AUG, KEY_BLOCK), BF16),
        jax.ShapeDtypeStruct((bsz, N_KV_GROUPS * GATE_ROWS, seq), F32),
    )
    return pl.pallas_call(
        _in_proj_kernel,
        grid=(bsz, n_t),
        in_specs=[
            pl.BlockSpec((1, tm, D_MODEL), lambda b, i: (b, i, 0)),
            pl.BlockSpec((1, 1, D_MODEL), lambda b, i: (b, 0, 0)),
            pl.BlockSpec((1, 1, D_MODEL), lambda b, i: (b, 0, 0)),
            pl.BlockSpec((D_MODEL, NAT_COLS), lambda b, i: (0, 0)),
            pl.BlockSpec((TR_ROWS, D_MODEL), lambda b, i: (0, 0)),
        ],
        out_specs=(
            pl.BlockSpec((1, tm, SSM_WIDTH), lambda b, i: (b, i, 0)),
            pl.BlockSpec((1, 2, N_KV_GROUPS, tm, HEAD_DIM), lambda b, i: (b, 0, 0, i, 0)),
            pl.BlockSpec((1, 2, N_KV_GROUPS, tm, KEY_AUG), lambda b, i: (b, 0, 0, i, 0)),
            pl.BlockSpec((1, ATTN_WIDTH, tm), lambda b, i: (b, 0, i)),
            pl.BlockSpec((1, 2, N_KV_GROUPS, tm // KEY_BLOCK, VAL_AUG, KEY_BLOCK), lambda b, i: (b, 0, 0, i, 0, 0)),
            pl.BlockSpec((1, N_KV_GROUPS * GATE_ROWS, tm), lambda b, i: (b, 0, i)),
        ),
        out_shape=out_shapes,
        compiler_params=_params(2),
        name="in_proj",
    )(x, scale1, shift1, w_nat, w_tr)


def _split_w_in(w_in):
    o_q = SSM_WIDTH
    o_kv = o_q + ATTN_WIDTH
    o_bg = o_kv + 6 * KV_WIDTH
    o_mg = o_bg + N_NSA_BRANCHES * N_HEADS
    kv = [w_in[:, o_kv + s * KV_WIDTH:o_kv + (s + 1) * KV_WIDTH].reshape(D_MODEL, N_KV_GROUPS, HEAD_DIM)
          for s in range(6)]
    pad_k = lambda w: jnp.pad(w, ((0, 0), (0, 0), (0, KEY_AUG - HEAD_DIM))).reshape(D_MODEL, N_KV_GROUPS * KEY_AUG)
    pad_v = lambda w: jnp.pad(w, ((0, 0), (0, 0), (0, VAL_AUG - HEAD_DIM))).reshape(D_MODEL, N_KV_GROUPS * VAL_AUG)
    flat = lambda w: w.reshape(D_MODEL, KV_WIDTH)
    w_nat = jnp.concatenate([w_in[:, :o_q], flat(kv[0]), flat(kv[1]), pad_k(kv[2]), pad_k(kv[4])], axis=1)
    per_group = HEADS_PER_GROUP * N_NSA_BRANCHES
    bg = w_in[:, o_bg:o_mg].reshape(D_MODEL, N_KV_GROUPS, per_group)
    bg = jnp.pad(bg, ((0, 0), (0, 0), (0, GATE_ROWS - per_group))).reshape(D_MODEL, N_KV_GROUPS * GATE_ROWS)
    w_tr = jnp.concatenate([w_in[:, o_q:o_kv], pad_v(kv[3]), pad_v(kv[5]), bg], axis=1).T
    return w_nat.astype(BF16), w_tr.astype(BF16), w_in[:, o_mg:].astype(BF16)


def _ssm_matrices(a_re, a_im, log_dt, b_re, b_im, c_re, c_im, d_skip):
    t_n = SSM_CHUNK
    hp = HIGHEST
    dt = jnp.exp(log_dt.astype(F32))[:, None]
    a_re = a_re.astype(F32)
    a_im = a_im.astype(F32)
    lam_re = a_re * dt
    lam_im = a_im * dt
    decay = jnp.exp(lam_re)
    ab_re = decay * jnp.cos(lam_im)
    ab_im = decay * jnp.sin(lam_im)
    denom = jnp.square(a_re) + jnp.square(a_im)
    n_re = ab_re - 1.0
    n_im = ab_im
    f_re = (n_re * a_re + n_im * a_im) / denom
    f_im = (n_im * a_re - n_re * a_im) / denom
    bb_re = f_re[:, :, None] * b_re - f_im[:, :, None] * b_im
    bb_im = f_re[:, :, None] * b_im + f_im[:, :, None] * b_re
    j = jnp.arange(t_n + 1, dtype=F32)[:, None, None]
    mag = jnp.exp(lam_re[None] * j)
    pw_re = mag * jnp.cos(lam_im[None] * j)
    pw_im = mag * jnp.sin(lam_im[None] * j)
    ca_re = c_re[None] * pw_re[:, :, None, :] - c_im[None] * pw_im[:, :, None, :]
    ca_im = c_re[None] * pw_im[:, :, None, :] + c_im[None] * pw_re[:, :, None, :]
    k_lag = (jnp.einsum('jgcp,gpi->jgci', ca_re[:t_n], bb_re, precision=hp)
             - jnp.einsum('jgcp,gpi->jgci', ca_im[:t_n], bb_im, precision=hp))
    s_ix = jnp.arange(t_n)[:, None]
    t_ix = jnp.arange(t_n)[None, :]
    lag = t_ix - s_ix
    m5 = jnp.where((lag >= 0)[:, :, None, None, None], k_lag[jnp.clip(lag, 0, t_n - 1)], 0.0)
    eye_c = jnp.eye(SSM_GROUP_SIZE, dtype=F32)
    skip = (s_ix == t_ix).astype(F32)[:, :, None, None, None] * (d_skip[None, None, :, :, None] * eye_c)
    m = (m5 + skip).transpose(2, 0, 4, 1, 3).reshape(SSM_GROUPS, t_n * SSM_GROUP_SIZE, t_n * SSM_GROUP_SIZE)
    rev = pw_re[t_n - 1 - jnp.arange(t_n)], pw_im[t_n - 1 - jnp.arange(t_n)]
    p_re = rev[0][:, :, :, None] * bb_re[None] - rev[1][:, :, :, None] * bb_im[None]
    p_im = rev[0][:, :, :, None] * bb_im[None] + rev[1][:, :, :, None] * bb_re[None]
    p_re = p_re.transpose(1, 0, 3, 2).reshape(SSM_GROUPS, t_n * SSM_GROUP_SIZE, SSM_STATE)
    p_im = p_im.transpose(1, 0, 3, 2).reshape(SSM_GROUPS, t_n * SSM_GROUP_SIZE, SSM_STATE)
    q_re = ca_re[1:].transpose(1, 3, 0, 2).reshape(SSM_GROUPS, SSM_STATE, t_n * SSM_GROUP_SIZE)
    q_im = (-ca_im[1:]).transpose(1, 3, 0, 2).reshape(SSM_GROUPS, SSM_STATE, t_n * SSM_GROUP_SIZE)
    return m, p_re, p_im, q_re, q_im, pw_re[t_n], pw_im[t_n]


def _ssm_kernel(u_ref, m_ref, pre_ref, pim_ref, qre_ref, qim_ref, are_ref, aim_ref, o_ref, sre, sim, *, n_chunks):
    n_g = u_ref.shape[1]

    def group_rows(g):
        return pl.ds(pl.multiple_of(g * n_chunks, n_chunks), n_chunks)

    def chunk_states(g, carry):
        u = u_ref[0, g]
        sre[group_rows(g), :] = jnp.dot(u, pre_ref[g], precision=HIGHEST, preferred_element_type=F32)
        sim[group_rows(g), :] = jnp.dot(u, pim_ref[g], precision=HIGHEST, preferred_element_type=F32)
        return carry

    lax.fori_loop(0, n_g, chunk_states, 0)

    a_r = are_ref[0]
    a_i = aim_ref[0]

    def carry_states(c, h):
        h_r, h_i = h
        rows = pl.ds(c, n_g, stride=n_chunks)
        s_r = sre[rows, :]
        s_i = sim[rows, :]
        sre[rows, :] = h_r
        sim[rows, :] = h_i
        return a_r * h_r - a_i * h_i + s_r, a_r * h_i + a_i * h_r + s_i

    zero = jnp.zeros((n_g, SSM_STATE), F32)
    lax.fori_loop(0, n_chunks, carry_states, (zero, zero))

    def outputs(g, carry):
        u = u_ref[0, g]
        y = jnp.dot(u, m_ref[g], precision=HIGHEST, preferred_element_type=F32)
        y = y + jnp.dot(sre[group_rows(g), :], qre_ref[g], precision=HIGHEST, preferred_element_type=F32)
        y = y + jnp.dot(sim[group_rows(g), :], qim_ref[g], precision=HIGHEST, preferred_element_type=F32)
        o_ref[0, g] = _gelu(y)
        return carry

    lax.fori_loop(0, n_g, outputs, 0)


def _ssm(u_chunks, mats):
    bsz, n_groups, n_chunks, width = u_chunks.shape
    m, p_re, p_im, q_re, q_im, a_re, a_im = mats
    gt = SSM_GROUP_TILE
    a_re = a_re.reshape(n_groups // gt, gt, SSM_STATE)
    a_im = a_im.reshape(n_groups // gt, gt, SSM_STATE)
    per_tile = lambda a: pl.BlockSpec((gt,) + a.shape[1:], lambda t, b: (t, 0, 0))
    per_tile1 = lambda a: pl.BlockSpec((1,) + a.shape[1:], lambda t, b: (t, 0, 0))
    return pl.pallas_call(
        functools.partial(_ssm_kernel, n_chunks=n_chunks),
        grid=(n_groups // gt, bsz),
        in_specs=[
            pl.BlockSpec((1, gt, n_chunks, width), lambda t, b: (b, t, 0, 0)),
            per_tile(m), per_tile(p_re), per_tile(p_im), per_tile(q_re), per_tile(q_im),
            per_tile1(a_re), per_tile1(a_im),
        ],
        out_specs=pl.BlockSpec((1, gt, n_chunks, width), lambda t, b: (b, t, 0, 0)),
        out_shape=jax.ShapeDtypeStruct(u_chunks.shape, F32),
        scratch_shapes=[pltpu.VMEM((gt * n_chunks, SSM_STATE), F32), pltpu.VMEM((gt * n_chunks, SSM_STATE), F32)],
        compiler_params=_params(2),
        name="ssm",
    )(u_chunks, m, p_re, p_im, q_re, q_im, a_re, a_im)


def _compress_kernel(kx_ref, vx_ref, pek_ref, pev_ref, wk1_ref, wk2_ref, wv1_ref, wv2_ref, kc_ref, vcT_ref):
    half = CMP_STRIDE * HEAD_DIM

    def mlp(x, pe, w1_ref, w2_ref):
        n = x.shape[0]
        first = jnp.dot(x, w1_ref[:half, :], precision=HIGHEST, preferred_element_type=F32)
        second = jnp.dot(x, w1_ref[half:, :], precision=HIGHEST, preferred_element_type=F32)
        bias = jnp.dot(pe, w1_ref[...], precision=HIGHEST, preferred_element_type=F32)
        h = _gelu(first + pltpu.roll(second, n - 1, 0) + bias)
        return jnp.dot(h, w2_ref[...], precision=HIGHEST, preferred_element_type=F32)

    kc = mlp(kx_ref[0, 0], pek_ref[...], wk1_ref, wk2_ref)
    kc_hi = kc.astype(BF16)
    kc_lo = (kc - kc_hi.astype(F32)).astype(BF16)
    kc_ref[0, 0] = jnp.concatenate([kc_hi, kc_hi, kc_lo, jnp.zeros_like(kc_hi)], axis=1)
    vcT_ref[0, 0] = mlp(vx_ref[0, 0], pev_ref[...], wv1_ref, wv2_ref).T.astype(BF16)


def _compress(kx, vx, pe_k, pe_v, wk1, wk2, wv1, wv2):
    bsz, n_g, n_chunks, width = kx.shape
    blk = pl.BlockSpec((1, 1, n_chunks, width), lambda b, g: (b, g, 0, 0))
    full = lambda a: pl.BlockSpec(a.shape, lambda b, g: (0, 0))
    return pl.pallas_call(
        _compress_kernel,
        grid=(bsz, n_g),
        in_specs=[blk, blk, full(pe_k), full(pe_v), full(wk1), full(wk2), full(wv1), full(wv2)],
        out_specs=(
            pl.BlockSpec((1, 1, n_chunks, 4 * HEAD_DIM), lambda b, g: (b, g, 0, 0)),
            pl.BlockSpec((1, 1, HEAD_DIM, n_chunks), lambda b, g: (b, g, 0, 0)),
        ),
        out_shape=(
            jax.ShapeDtypeStruct((bsz, n_g, n_chunks, 4 * HEAD_DIM), BF16),
            jax.ShapeDtypeStruct((bsz, n_g, HEAD_DIM, n_chunks), BF16),
        ),
        compiler_params=_params(2),
        name="compress",
    )(kx, vx, pe_k, pe_v, wk1, wk2, wv1, wv2)


def _nsa_kernel(qT_ref, kc_ref, vcT_ref, ks_ref, vs_ref, kw_ref, vw_ref, bg_ref, o_ref,
                negm_scr, qa_scr, m_scr, acc_scr, out_scr, *, tq, n_cmp, n_sel):
    g = pl.program_id(1)
    i = pl.program_id(2)
    kb_per_tile = tq // KEY_BLOCK
    t_row = i * tq + lax.broadcasted_iota(jnp.int32, (1, tq), 1)
    slopes = [jnp.where(g == 0, 2.0 ** -(hh + 1), 2.0 ** -(HEADS_PER_GROUP + hh + 1)).astype(F32)
              for hh in range(HEADS_PER_GROUP)]

    def gate(hh, branch):
        r = hh * N_NSA_BRANCHES + branch
        return bg_ref[0, 0, r:r + 1, :]

    def head_rows(hh):
        return slice(hh * HEAD_DIM, (hh + 1) * HEAD_DIM)

    def head_lanes(hh):
        return slice(hh * tq, (hh + 1) * tq)

    kc3 = kc_ref[0, 0]
    vcT = vcT_ref[0, 0]
    cmp_end = lax.broadcasted_iota(jnp.int32, (n_cmp, 1), 0) * CMP_STRIDE + (CMP_BLOCK - 1)
    dist_c = (t_row - cmp_end).astype(F32)
    valid_c = dist_c >= 0.0
    p_sum = jnp.zeros((n_cmp, tq), F32)
    piece = lax.broadcasted_iota(jnp.int32, (16, 1), 0)
    log2e_pieces = _bf16_pieces(LOG2E, N_PIECES)
    log2e_col = jnp.zeros((16, 1), F32)
    for k, value in enumerate(log2e_pieces):
        log2e_col = jnp.where((piece == k) | (piece == N_PIECES + k), value, log2e_col)
    for hh in range(HEADS_PER_GROUP):
        q_h = qT_ref[0, head_rows(hh), :]
        q_hi = q_h.astype(BF16)
        q_lo = (q_h - q_hi.astype(F32)).astype(BF16)
        q3 = jnp.concatenate([q_hi, q_lo, q_hi, jnp.zeros_like(q_hi)], axis=0)
        s = jnp.dot(kc3, q3, preferred_element_type=F32) - slopes[hh] * dist_c
        s = jnp.where(valid_c, s, MASK_VALUE)
        m = jnp.max(s, axis=0, keepdims=True)
        p = jnp.where(valid_c, jnp.exp(s - m), 0.0)
        l = jnp.sum(p, axis=0, keepdims=True)
        p = p / jnp.where(l > 0.0, l, 1.0)
        o_c = jnp.dot(vcT, p.astype(BF16), preferred_element_type=F32)
        out_scr[head_rows(hh), :] = gate(hh, 0) * o_c
        p_sum = p_sum + p
        alibi = jnp.broadcast_to(log2e_col * slopes[hh], (16, tq)).astype(BF16)
        qa_scr[:, head_lanes(hh)] = jnp.concatenate([(q_h * LOG2E).astype(BF16), alibi], axis=0)

    blk = lax.broadcasted_iota(jnp.int32, (n_sel, n_cmp), 0)
    cmp_ix = lax.broadcasted_iota(jnp.int32, (n_sel, n_cmp), 1)
    ratio = SEL_BLOCK // CMP_STRIDE
    extra = CMP_BLOCK // CMP_STRIDE - 1
    overlap_t = jnp.where((cmp_ix >= ratio * blk - extra) & (cmp_ix <= ratio * blk + ratio - 1), 1.0, 0.0).astype(BF16)
    imp = jnp.zeros((n_sel, tq), F32)
    rest = p_sum
    for _ in range(3):
        part = rest.astype(BF16)
        imp = imp + jnp.dot(overlap_t, part, preferred_element_type=F32)
        rest = rest - part.astype(F32)

    j_col = lax.broadcasted_iota(jnp.int32, (n_sel, 1), 0)
    forced = (j_col == 0) | (j_col == lax.shift_right_logical(t_row, SEL_SHIFT))
    future = j_col * SEL_BLOCK > t_row
    score = jnp.where(forced, jnp.inf, jnp.where(future, -jnp.inf, imp))
    rank = jnp.zeros((n_sel, tq), F32)
    for jp in range(n_sel):
        other = score[jp:jp + 1, :]
        ahead = (other > score) | ((other == score) & (j_col > jp))
        rank = rank + jnp.where(ahead, 1.0, 0.0)
    neg = jnp.where(rank < float(min(SEL_TOP_K, n_sel)), 0.0, MASK_VALUE)
    pad_rows = jnp.zeros((16 - SEL_PER_KEY_BLOCK, tq), F32)
    for c in range(n_sel // SEL_PER_KEY_BLOCK):
        rows = neg[c * SEL_PER_KEY_BLOCK:(c + 1) * SEL_PER_KEY_BLOCK, :]
        mask_tile = jnp.concatenate([rows, pad_rows], axis=0).astype(BF16)
        negm_scr[c] = jnp.concatenate([mask_tile] * HEADS_PER_GROUP, axis=1)

    lanes = HEADS_PER_GROUP * tq
    rel = ((lax.broadcasted_iota(jnp.int32, (KEY_BLOCK, lanes), 1) & (tq - 1))
           - lax.broadcasted_iota(jnp.int32, (KEY_BLOCK, lanes), 0))
    zero_tail = jnp.zeros((KEY_AUG - QUERY_AUG, lanes), BF16)
    no_mask = jnp.zeros((16, lanes), BF16)

    def reset():
        m_scr[...] = jnp.full(m_scr.shape, MASK_VALUE, F32)
        acc_scr[...] = jnp.zeros(acc_scr.shape, F32)

    def tile(k_ref, v_ref, kb, mask_tile, valid):
        k0 = pl.multiple_of(kb * KEY_BLOCK, KEY_BLOCK)
        k_aug = k_ref[0, 0, 0, pl.ds(k0, KEY_BLOCK), :]
        v_aug = v_ref[0, 0, 0, kb]
        q_aug = jnp.concatenate([qa_scr[...], mask_tile, zero_tail], axis=0)
        s = jnp.dot(k_aug, q_aug, preferred_element_type=F32)
        if valid is not None:
            s = jnp.where(valid, s, MASK_VALUE)
        m_old = m_scr[...]
        m_new = jnp.maximum(m_old, jnp.max(s, axis=0, keepdims=True))
        p = jnp.exp2(s - jnp.maximum(m_new, 0.1 * MASK_VALUE))
        alpha = jnp.exp2(m_old - m_new)
        acc_scr[...] = alpha * acc_scr[...] + jnp.dot(v_aug, p.astype(BF16), preferred_element_type=F32)
        m_scr[...] = m_new

    def finish(branch):
        for hh in range(HEADS_PER_GROUP):
            l = acc_scr[HEAD_DIM:HEAD_DIM + 1, head_lanes(hh)]
            o_b = acc_scr[:HEAD_DIM, head_lanes(hh)] / jnp.where(l > 0.0, l, 1.0)
            out_scr[head_rows(hh), :] = out_scr[head_rows(hh), :] + gate(hh, branch) * o_b

    reset()

    def past_block(kb, carry):
        tile(ks_ref, vs_ref, kb, negm_scr[kb], None)
        return carry

    lax.fori_loop(0, i * kb_per_tile, past_block, 0)
    for d in range(kb_per_tile):
        kb = i * kb_per_tile + d
        tile(ks_ref, vs_ref, kb, negm_scr[kb], rel - d * KEY_BLOCK >= 0)
    finish(1)

    reset()
    for d in range(-(WINDOW // KEY_BLOCK), kb_per_tile):
        dist = rel - d * KEY_BLOCK
        valid = (dist >= 0) & (dist < WINDOW)
        if d < 0:
            @pl.when(i * kb_per_tile + d >= 0)
            def _():
                tile(kw_ref, vw_ref, i * kb_per_tile + d, no_mask, valid)
        else:
            tile(kw_ref, vw_ref, i * kb_per_tile + d, no_mask, valid)
    finish(2)

    o_ref[0] = out_scr[...].T


def _nsa(qT, kc3, vcT, k_aug, v_aug, bgT, *, tq):
    bsz, _, seq = qT.shape
    n_cmp = kc3.shape[2]
    n_sel = seq // SEL_BLOCK
    n_kb = seq // KEY_BLOCK
    gw = HEADS_PER_GROUP * HEAD_DIM
    bg4 = bgT.reshape(bsz, N_KV_GROUPS, GATE_ROWS, seq)
    k_spec = lambda s: pl.BlockSpec((1, 1, 1, seq, KEY_AUG), lambda b, g, i: (b, s, g, 0, 0))
    v_spec = lambda s: pl.BlockSpec((1, 1, 1, n_kb, VAL_AUG, KEY_BLOCK), lambda b, g, i: (b, s, g, 0, 0, 0))
    return pl.pallas_call(
        functools.partial(_nsa_kernel, tq=tq, n_cmp=n_cmp, n_sel=n_sel),
        grid=(bsz, N_KV_GROUPS, seq // tq),
        in_specs=[
            pl.BlockSpec((1, gw, tq), lambda b, g, i: (b, g, i)),
            pl.BlockSpec((1, 1, n_cmp, 4 * HEAD_DIM), lambda b, g, i: (b, g, 0, 0)),
            pl.BlockSpec((1, 1, HEAD_DIM, n_cmp), lambda b, g, i: (b, g, 0, 0)),
            k_spec(0), v_spec(0),
            k_spec(1), v_spec(1),
            pl.BlockSpec((1, 1, GATE_ROWS, tq), lambda b, g, i: (b, g, 0, i)),
        ],
        out_specs=pl.BlockSpec((1, tq, gw), lambda b, g, i: (b, i, g)),
        out_shape=jax.ShapeDtypeStruct((bsz, seq, ATTN_WIDTH), F32),
        scratch_shapes=[
            pltpu.VMEM((n_kb, 16, HEADS_PER_GROUP * tq), BF16),
            pltpu.VMEM((HEAD_DIM + 16, HEADS_PER_GROUP * tq), BF16),
            pltpu.VMEM((1, HEADS_PER_GROUP * tq), F32),
            pltpu.VMEM((VAL_AUG, HEADS_PER_GROUP * tq), F32),
            pltpu.VMEM((gw, tq), F32),
        ],
        compiler_params=_params(3),
        name="nsa",
    )(qT, kc3, vcT, k_aug, v_aug, k_aug, v_aug, bg4)


def _merge_kernel(x_ref, g_ref, o_ref_in, sc_ref, sh_ref, gt_ref, wm_ref, wl_ref, wg_ref, wn_ref, wo_ref,
                  lg_ref, lb_ref, y_ref):
    x = x_ref[0]
    u = (x * (1.0 + sc_ref[0]) + sh_ref[0]).astype(BF16)
    mg = jnp.dot(u, wm_ref[...], preferred_element_type=F32)
    g = g_ref[0].astype(BF16)
    z_ssm = jnp.dot(g, wl_ref[...], preferred_element_type=F32) * _sigmoid(
        jnp.dot(g, wg_ref[...], preferred_element_type=F32))
    z_nsa = jnp.dot(o_ref_in[0].astype(BF16), wn_ref[...], preferred_element_type=F32)
    merged = _sigmoid(mg[:, :D_MODEL]) * z_ssm + _sigmoid(mg[:, D_MODEL:]) * z_nsa
    mix = jnp.dot(merged.astype(BF16), wo_ref[...], preferred_element_type=F32)
    y = DEEPNORM_ALPHA * x + (1.0 + gt_ref[0]) * mix
    y_ref[0] = _layer_norm(y, lg_ref[...], lb_ref[...])


def _merge(x, g_ssm, o_nsa, scale1, shift1, gate1, w_m, w_lin, w_gate, w_nsa, w_out, ln_g, ln_b, *, tm):
    bsz, seq, _ = x.shape
    row = pl.BlockSpec((1, 1, D_MODEL), lambda b, i: (b, 0, 0))
    full = lambda a: pl.BlockSpec(a.shape, lambda b, i: (0, 0))
    return pl.pallas_call(
        _merge_kernel,
        grid=(bsz, seq // tm),
        in_specs=[
            pl.BlockSpec((1, tm, D_MODEL), lambda b, i: (b, i, 0)),
            pl.BlockSpec((1, tm, SSM_WIDTH), lambda b, i: (b, i, 0)),
            pl.BlockSpec((1, tm, ATTN_WIDTH), lambda b, i: (b, i, 0)),
            row, row, row,
            full(w_m), full(w_lin), full(w_gate), full(w_nsa), full(w_out), full(ln_g), full(ln_b),
        ],
        out_specs=pl.BlockSpec((1, tm, D_MODEL), lambda b, i: (b, i, 0)),
        out_shape=jax.ShapeDtypeStruct(x.shape, F32),
        compiler_params=_params(2),
        name="merge",
    )(x, g_ssm, o_nsa, scale1, shift1, gate1, w_m, w_lin, w_gate, w_nsa, w_out, ln_g, ln_b)


def _ffn_kernel(x_ref, sc_ref, sh_ref, gt_ref, wg_ref, wu_ref, wd_ref, lg_ref, lb_ref, y_ref, u_scr, acc_scr):
    j = pl.program_id(2)

    @pl.when(j == 0)
    def _():
        u_scr[...] = (x_ref[0] * (1.0 + sc_ref[0]) + sh_ref[0]).astype(BF16)
        acc_scr[...] = jnp.zeros(acc_scr.shape, F32)

    u = u_scr[...]
    a = jnp.dot(u, wg_ref[...], preferred_element_type=F32)
    h = (a * _sigmoid(a)) * jnp.dot(u, wu_ref[...], preferred_element_type=F32)
    acc_scr[...] += jnp.dot(h.astype(BF16), wd_ref[...], preferred_element_type=F32)

    @pl.when(j == pl.num_programs(2) - 1)
    def _():
        y = DEEPNORM_ALPHA * x_ref[0] + (1.0 + gt_ref[0]) * acc_scr[...]
        y_ref[0] = _layer_norm(y, lg_ref[...], lb_ref[...])


def _ffn(x, scale2, shift2, gate2, w_gate, w_up, w_down, ln_g, ln_b, *, tm, th):
    bsz, seq, _ = x.shape
    row = pl.BlockSpec((1, 1, D_MODEL), lambda b, i, j: (b, 0, 0))
    vec = pl.BlockSpec((1, D_MODEL), lambda b, i, j: (0, 0))
    return pl.pallas_call(
        _ffn_kernel,
        grid=(bsz, seq // tm, FFN_HIDDEN // th),
        in_specs=[
            pl.BlockSpec((1, tm, D_MODEL), lambda b, i, j: (b, i, 0)),
            row, row, row,
            pl.BlockSpec((D_MODEL, th), lambda b, i, j: (0, j)),
            pl.BlockSpec((D_MODEL, th), lambda b, i, j: (0, j)),
            pl.BlockSpec((th, D_MODEL), lambda b, i, j: (j, 0)),
            vec, vec,
        ],
        out_specs=pl.BlockSpec((1, tm, D_MODEL), lambda b, i, j: (b, i, 0)),
        out_shape=jax.ShapeDtypeStruct(x.shape, F32),
        scratch_shapes=[pltpu.VMEM((tm, D_MODEL), BF16), pltpu.VMEM((tm, D_MODEL), F32)],
        compiler_params=_params(3),
        name="ffn",
    )(x, scale2, shift2, gate2, w_gate, w_up, w_down, ln_g, ln_b)


def _layer(x, c, w_ada, b_ada, w_in, ssm_a_re, ssm_a_im, ssm_log_dt, ssm_b_re, ssm_b_im, ssm_c_re, ssm_c_im,
           ssm_d, w_glu_lin, w_glu_gate, cmp_pe_k, cmp_pe_v, w_cmp_k1, w_cmp_k2, w_cmp_v1, w_cmp_v2,
           w_nsa_proj, w_out, ln1_g, ln1_b, w_ffn_gate, w_ffn_up, w_ffn_down, ln2_g, ln2_b):
    bsz, seq, _ = x.shape
    n_chunks = seq // SSM_CHUNK
    tq = min(256, seq)
    tm = min(512, seq)

    mod = _ada_mod(c, w_ada, b_ada).reshape(6, bsz, 1, D_MODEL)
    shift1, scale1, gate1, shift2, scale2, gate2 = (mod[k] for k in range(6))

    w_nat, w_tr, w_mg = _split_w_in(w_in)
    u_ssm, kv_cmp, k_aug, qT, v_aug, bgT = _in_proj(x, scale1, shift1, w_nat, w_tr, tm=tm)

    u_chunks = (u_ssm.reshape(bsz, n_chunks, SSM_CHUNK, SSM_GROUPS, SSM_GROUP_SIZE)
                .transpose(0, 3, 1, 2, 4).reshape(bsz, SSM_GROUPS, n_chunks, SSM_CHUNK * SSM_GROUP_SIZE))
    mats = _ssm_matrices(ssm_a_re, ssm_a_im, ssm_log_dt, ssm_b_re, ssm_b_im, ssm_c_re, ssm_c_im, ssm_d)
    g_chunks = _ssm(u_chunks, mats)
    g_ssm = (g_chunks.reshape(bsz, SSM_GROUPS, n_chunks, SSM_CHUNK, SSM_GROUP_SIZE)
             .transpose(0, 2, 3, 1, 4).reshape(bsz, seq, SSM_WIDTH))

    chunk_w = CMP_STRIDE * HEAD_DIM
    kx = kv_cmp[:, 0].reshape(bsz, N_KV_GROUPS, seq // CMP_STRIDE, chunk_w)
    vx = kv_cmp[:, 1].reshape(bsz, N_KV_GROUPS, seq // CMP_STRIDE, chunk_w)
    kc3, vcT = _compress(kx, vx,
                        cmp_pe_k.reshape(1, CMP_BLOCK * HEAD_DIM), cmp_pe_v.reshape(1, CMP_BLOCK * HEAD_DIM),
                        w_cmp_k1.reshape(CMP_BLOCK * HEAD_DIM, -1), w_cmp_k2,
                        w_cmp_v1.reshape(CMP_BLOCK * HEAD_DIM, -1), w_cmp_v2)
    o_nsa = _nsa(qT, kc3, vcT, k_aug, v_aug, bgT, tq=tq)

    x1 = _merge(x, g_ssm, o_nsa, scale1, shift1, gate1, w_mg,
                w_glu_lin.astype(BF16), w_glu_gate.astype(BF16), w_nsa_proj.astype(BF16), w_out.astype(BF16),
                ln1_g.reshape(1, D_MODEL), ln1_b.reshape(1, D_MODEL), tm=tm)
    return _ffn(x1, scale2, shift2, gate2, w_ffn_gate.astype(BF16), w_ffn_up.astype(BF16),
                w_ffn_down.astype(BF16), ln2_g.reshape(1, D_MODEL), ln2_b.reshape(1, D_MODEL), tm=tm, th=256)


def kernel(x, c, w_ada, b_ada, w_in, ssm_a_re, ssm_a_im, ssm_log_dt, ssm_b_re, ssm_b_im, ssm_c_re, ssm_c_im,
           ssm_d, w_glu_lin, w_glu_gate, cmp_pe_k, cmp_pe_v, w_cmp_k1, w_cmp_k2, w_cmp_v1, w_cmp_v2,
           w_nsa_proj, w_out, ln1_g, ln1_b, w_ffn_gate, w_ffn_up, w_ffn_down, ln2_g, ln2_b):
    for l in range(w_ada.shape[0]):
        x = _layer(x, c, w_ada[l], b_ada[l], w_in[l], ssm_a_re[l], ssm_a_im[l], ssm_log_dt[l],
                   ssm_b_re[l], ssm_b_im[l], ssm_c_re[l], ssm_c_im[l], ssm_d[l],
                   w_glu_lin[l], w_glu_gate[l], cmp_pe_k[l], cmp_pe_v[l],
                   w_cmp_k1[l], w_cmp_k2[l], w_cmp_v1[l], w_cmp_v2[l], w_nsa_proj[l], w_out[l],
                   ln1_g[l], ln1_b[l], w_ffn_gate[l], w_ffn_up[l], w_ffn_down[l], ln2_g[l], ln2_b[l])
    return x
```

```python
import functools
import math

import jax
import jax.numpy as jnp
from jax import lax
from jax.experimental import pallas as pl
from jax.experimental.pallas import tpu as pltpu

F32 = jnp.float32
BF16 = jnp.bfloat16
HIGHEST = lax.Precision.HIGHEST

D_MODEL = 1024
SSM_WIDTH = D_MODEL // 2
SSM_GROUP_SIZE = 16
SSM_GROUPS = SSM_WIDTH // SSM_GROUP_SIZE
SSM_STATE = 64
SSM_CHUNK = 16
SSM_GROUP_TILE = 8
N_HEADS = 8
HEAD_DIM = 64
N_KV_GROUPS = 2
HEADS_PER_GROUP = N_HEADS // N_KV_GROUPS
ATTN_WIDTH = N_HEADS * HEAD_DIM
KV_WIDTH = N_KV_GROUPS * HEAD_DIM
CMP_BLOCK = 32
CMP_STRIDE = 16
SEL_BLOCK = 64
SEL_TOP_K = 8
WINDOW = 256
N_NSA_BRANCHES = 3
GATE_ROWS = 16
FFN_HIDDEN = (8 * D_MODEL + 3 * 256 - 1) // (3 * 256) * 256
DEEPNORM_ALPHA = 2.0 ** 0.25
LN_EPS = 1e-5
MASK_VALUE = -1e30

VMEM_LIMIT = 56 * 1024 * 1024

KEY_BLOCK = 256
SEL_SHIFT = SEL_BLOCK.bit_length() - 1
SEL_PER_KEY_BLOCK = KEY_BLOCK // SEL_BLOCK
KEY_AUG = 128
ALIBI_COL = HEAD_DIM
N_PIECES = 4
MASK_COL = HEAD_DIM + 16
QUERY_AUG = HEAD_DIM + 32
VAL_AUG = HEAD_DIM + 16
LOG2E = math.log2(math.e)

NAT_COLS = SSM_WIDTH + 2 * KV_WIDTH + 2 * N_KV_GROUPS * KEY_AUG
TR_ROWS = ATTN_WIDTH + 2 * N_KV_GROUPS * VAL_AUG + N_KV_GROUPS * GATE_ROWS


def _bf16_pieces(value, n):
    pieces = []
    rest = value
    for _ in range(n):
        mant, expo = math.frexp(rest)
        piece = math.ldexp(round(mant * 256.0) / 256.0, expo)
        pieces.append(piece)
        rest -= piece
    return pieces


def _sigmoid(x):
    return 1.0 / (1.0 + jnp.exp(-x))


def _gelu(x):
    c = math.sqrt(2.0 / math.pi)
    return 0.5 * x * (1.0 + jnp.tanh(c * (x + 0.044715 * (x * x * x))))


def _layer_norm(y, gain, bias):
    mu = jnp.mean(y, axis=-1, keepdims=True)
    d = y - mu
    var = jnp.mean(d * d, axis=-1, keepdims=True)
    return d * lax.rsqrt(var + LN_EPS) * gain + bias


def _params(n_axes, flags=None):
    return pltpu.CompilerParams(dimension_semantics=("arbitrary",) * n_axes, vmem_limit_bytes=VMEM_LIMIT, flags=flags)


def _ada_kernel(c_ref, w_ref, b_ref, o_ref):
    c = c_ref[...]
    a = c * _sigmoid(c)
    o_ref[0] = jnp.dot(a, w_ref[...], precision=HIGHEST, preferred_element_type=F32) + b_ref[...]


def _ada_mod(c, w_ada, b_ada):
    bsz = c.shape[0]
    return pl.pallas_call(
        _ada_kernel,
        grid=(6,),
        in_specs=[
            pl.BlockSpec((bsz, D_MODEL), lambda j: (0, 0)),
            pl.BlockSpec((D_MODEL, D_MODEL), lambda j: (0, j)),
            pl.BlockSpec((1, D_MODEL), lambda j: (0, j)),
        ],
        out_specs=pl.BlockSpec((1, bsz, D_MODEL), lambda j: (j, 0, 0)),
        out_shape=jax.ShapeDtypeStruct((6, bsz, D_MODEL), F32),
        compiler_params=_params(1),
        name="ada_mod",
    )(c, w_ada, b_ada.reshape(1, 6 * D_MODEL))


def _in_proj_kernel(x_ref, sc_ref, sh_ref, wn_ref, wt_ref, ussm_ref, kc_ref, kp_ref, qT_ref, vp_ref, bgT_ref):
    i = pl.program_id(1)
    tm = x_ref.shape[1]
    u = (x_ref[0] * (1.0 + sc_ref[0]) + sh_ref[0]).astype(BF16)
    nat = jnp.dot(u, wn_ref[...], preferred_element_type=F32)
    ussm_ref[0] = nat[:, :SSM_WIDTH]
    for s in range(2):
        for g in range(N_KV_GROUPS):
            lo = SSM_WIDTH + s * KV_WIDTH + g * HEAD_DIM
            kc_ref[0, s, g] = nat[:, lo:lo + HEAD_DIM]
    pos = i * tm + lax.broadcasted_iota(jnp.int32, (tm, KEY_AUG), 0)
    col = lax.broadcasted_iota(jnp.int32, (tm, KEY_AUG), 1)
    blk = lax.shift_right_logical(pos, SEL_SHIFT)
    in_a = (col >= ALIBI_COL) & (col < ALIBI_COL + N_PIECES)
    in_b = (col >= ALIBI_COL + N_PIECES) & (col < ALIBI_COL + 2 * N_PIECES)
    hot = col == MASK_COL + (blk & (SEL_PER_KEY_BLOCK - 1))
    aux = jnp.where(in_a, blk * SEL_BLOCK, jnp.where(in_b, pos & (SEL_BLOCK - 1), jnp.where(hot, 1, 0))).astype(F32)
    for s in range(2):
        for g in range(N_KV_GROUPS):
            lo = SSM_WIDTH + 2 * KV_WIDTH + (s * N_KV_GROUPS + g) * KEY_AUG
            kp_ref[0, s, g] = (nat[:, lo:lo + KEY_AUG] + aux).astype(BF16)
    tr = lax.dot_general(wt_ref[...], u, (((1,), (1,)), ((), ())), preferred_element_type=F32)
    qT_ref[0] = tr[:ATTN_WIDTH] * (HEAD_DIM ** -0.5)
    ones_row = jnp.where(lax.broadcasted_iota(jnp.int32, (VAL_AUG, 1), 0) == HEAD_DIM, 1.0, 0.0)
    for s in range(2):
        for g in range(N_KV_GROUPS):
            lo = ATTN_WIDTH + (s * N_KV_GROUPS + g) * VAL_AUG
            v_aug = (tr[lo:lo + VAL_AUG] + ones_row).astype(BF16)
            for c in range(tm // KEY_BLOCK):
                vp_ref[0, s, g, c] = v_aug[:, c * KEY_BLOCK:(c + 1) * KEY_BLOCK]
    bgT_ref[0] = _sigmoid(tr[ATTN_WIDTH + 2 * N_KV_GROUPS * VAL_AUG:])


def _in_proj(x, scale1, shift1, w_nat, w_tr, *, tm):
    bsz, seq, _ = x.shape
    n_t = seq // tm
    n_kb = seq // KEY_BLOCK
    out_shapes = (
        jax.ShapeDtypeStruct((bsz, seq, SSM_WIDTH), F32),
        jax.ShapeDtypeStruct((bsz, 2, N_KV_GROUPS, seq, HEAD_DIM), F32),
        jax.ShapeDtypeStruct((bsz, 2, N_KV_GROUPS, seq, KEY_AUG), BF16),
        jax.ShapeDtypeStruct((bsz, ATTN_WIDTH, seq), F32),
        jax.ShapeDtypeStruct((bsz, 2, N_KV_GROUPS, n_kb, VAL_AUG, KEY_BLOCK), BF16),
        jax.ShapeDtypeStruct((bsz, N_KV_GROUPS * GATE_ROWS, seq), F32),
    )
    return pl.pallas_call(
        _in_proj_kernel,
        grid=(bsz, n_t),
        in_specs=[
            pl.BlockSpec((1, tm, D_MODEL), lambda b, i: (b, i, 0)),
            pl.BlockSpec((1, 1, D_MODEL), lambda b, i: (b, 0, 0)),
            pl.BlockSpec((1, 1, D_MODEL), lambda b, i: (b, 0, 0)),
            pl.BlockSpec((D_MODEL, NAT_COLS), lambda b, i: (0, 0)),
            pl.BlockSpec((TR_ROWS, D_MODEL), lambda b, i: (0, 0)),
        ],
        out_specs=(
            pl.BlockSpec((1, tm, SSM_WIDTH), lambda b, i: (b, i, 0)),
            pl.BlockSpec((1, 2, N_KV_GROUPS, tm, HEAD_DIM), lambda b, i: (b, 0, 0, i, 0)),
            pl.BlockSpec((1, 2, N_KV_GROUPS, tm, KEY_AUG), lambda b, i: (b, 0, 0, i, 0)),
            pl.BlockSpec((1, ATTN_WIDTH, tm), lambda b, i: (b, 0, i)),
            pl.BlockSpec((1, 2, N_KV_GROUPS, tm // KEY_BLOCK, VAL_AUG, KEY_BLOCK), lambda b, i: (b, 0, 0, i, 0, 0)),
            pl.BlockSpec((1, N_KV_GROUPS * GATE_ROWS, tm), lambda b, i: (b, 0, i)),
        ),
        out_shape=out_shapes,
        compiler_params=_params(2),
        name="in_proj",
    )(x, scale1, shift1, w_nat, w_tr)


def _split_w_in(w_in):
    o_q = SSM_WIDTH
    o_kv = o_q + ATTN_WIDTH
    o_bg = o_kv + 6 * KV_WIDTH
    o_mg = o_bg + N_NSA_BRANCHES * N_HEADS
    kv = [w_in[:, o_kv + s * KV_WIDTH:o_kv + (s + 1) * KV_WIDTH].reshape(D_MODEL, N_KV_GROUPS, HEAD_DIM)
          for s in range(6)]
    pad_k = lambda w: jnp.pad(w, ((0, 0), (0, 0), (0, KEY_AUG - HEAD_DIM))).reshape(D_MODEL, N_KV_GROUPS * KEY_AUG)
    pad_v = lambda w: jnp.pad(w, ((0, 0), (0, 0), (0, VAL_AUG - HEAD_DIM))).reshape(D_MODEL, N_KV_GROUPS * VAL_AUG)
    flat = lambda w: w.reshape(D_MODEL, KV_WIDTH)
    w_nat = jnp.concatenate([w_in[:, :o_q], flat(kv[0]), flat(kv[1]), pad_k(kv[2]), pad_k(kv[4])], axis=1)
    per_group = HEADS_PER_GROUP * N_NSA_BRANCHES
    bg = w_in[:, o_bg:o_mg].reshape(D_MODEL, N_KV_GROUPS, per_group)
    bg = jnp.pad(bg, ((0, 0), (0, 0), (0, GATE_ROWS - per_group))).reshape(D_MODEL, N_KV_GROUPS * GATE_ROWS)
    w_tr = jnp.concatenate([w_in[:, o_q:o_kv], pad_v(kv[3]), pad_v(kv[5]), bg], axis=1).T
    return w_nat.astype(BF16), w_tr.astype(BF16), w_in[:, o_mg:].astype(BF16)


def _ssm_matrices(a_re, a_im, log_dt, b_re, b_im, c_re, c_im, d_skip):
    t_n = SSM_CHUNK
    hp = HIGHEST
    dt = jnp.exp(log_dt.astype(F32))[:, None]
    a_re = a_re.astype(F32)
    a_im = a_im.astype(F32)
    lam_re = a_re * dt
    lam_im = a_im * dt
    decay = jnp.exp(lam_re)
    ab_re = decay * jnp.cos(lam_im)
    ab_im = decay * jnp.sin(lam_im)
    denom = jnp.square(a_re) + jnp.square(a_im)
    n_re = ab_re - 1.0
    n_im = ab_im
    f_re = (n_re * a_re + n_im * a_im) / denom
    f_im = (n_im * a_re - n_re * a_im) / denom
    bb_re = f_re[:, :, None] * b_re - f_im[:, :, None] * b_im
    bb_im = f_re[:, :, None] * b_im + f_im[:, :, None] * b_re
    j = jnp.arange(t_n + 1, dtype=F32)[:, None, None]
    mag = jnp.exp(lam_re[None] * j)
    pw_re = mag * jnp.cos(lam_im[None] * j)
    pw_im = mag * jnp.sin(lam_im[None] * j)
    ca_re = c_re[None] * pw_re[:, :, None, :] - c_im[None] * pw_im[:, :, None, :]
    ca_im = c_re[None] * pw_im[:, :, None, :] + c_im[None] * pw_re[:, :, None, :]
    k_lag = (jnp.einsum('jgcp,gpi->jgci', ca_re[:t_n], bb_re, precision=hp)
             - jnp.einsum('jgcp,gpi->jgci', ca_im[:t_n], bb_im, precision=hp))
    s_ix = jnp.arange(t_n)[:, None]
    t_ix = jnp.arange(t_n)[None, :]
    lag = t_ix - s_ix
    m5 = jnp.where((lag >= 0)[:, :, None, None, None], k_lag[jnp.clip(lag, 0, t_n - 1)], 0.0)
    eye_c = jnp.eye(SSM_GROUP_SIZE, dtype=F32)
    skip = (s_ix == t_ix).astype(F32)[:, :, None, None, None] * (d_skip[None, None, :, :, None] * eye_c)
    m = (m5 + skip).transpose(2, 0, 4, 1, 3).reshape(SSM_GROUPS, t_n * SSM_GROUP_SIZE, t_n * SSM_GROUP_SIZE)
    rev = pw_re[t_n - 1 - jnp.arange(t_n)], pw_im[t_n - 1 - jnp.arange(t_n)]
    p_re = rev[0][:, :, :, None] * bb_re[None] - rev[1][:, :, :, None] * bb_im[None]
    p_im = rev[0][:, :, :, None] * bb_im[None] + rev[1][:, :, :, None] * bb_re[None]
    p_re = p_re.transpose(1, 0, 3, 2).reshape(SSM_GROUPS, t_n * SSM_GROUP_SIZE, SSM_STATE)
    p_im = p_im.transpose(1, 0, 3, 2).reshape(SSM_GROUPS, t_n * SSM_GROUP_SIZE, SSM_STATE)
    q_re = ca_re[1:].transpose(1, 3, 0, 2).reshape(SSM_GROUPS, SSM_STATE, t_n * SSM_GROUP_SIZE)
    q_im = (-ca_im[1:]).transpose(1, 3, 0, 2).reshape(SSM_GROUPS, SSM_STATE, t_n * SSM_GROUP_SIZE)
    tr = lambda a: jnp.swapaxes(a, 1, 2).astype(BF16)
    return tr(m), tr(p_re), tr(p_im), tr(q_re), tr(q_im), pw_re[t_n], pw_im[t_n]


def _ssm_kernel(u_ref, mT_ref, pTre_ref, pTim_ref, qTre_ref, qTim_ref, are_ref, aim_ref, o_ref,
                ut_scr, yt_scr, sre, sim, *, n_chunks):
    n_g = SSM_GROUP_TILE
    t_n = SSM_CHUNK
    c_n = SSM_GROUP_SIZE

    for s in range(t_n):
        x_s = u_ref[0, pl.ds(s, n_chunks, stride=t_n), :]
        ut_scr[:, s * c_n:(s + 1) * c_n, :] = x_s.T.reshape(n_g, c_n, n_chunks).astype(BF16)

    def group_rows(g):
        return pl.ds(pl.multiple_of(g * n_chunks, n_chunks), n_chunks)

    def chunk_states(g, carry):
        ut = ut_scr[g]
        sre[group_rows(g), :] = jnp.dot(pTre_ref[g], ut, preferred_element_type=F32).T
        sim[group_rows(g), :] = jnp.dot(pTim_ref[g], ut, preferred_element_type=F32).T
        return carry

    lax.fori_loop(0, n_g, chunk_states, 0)

    a_r = are_ref[0]
    a_i = aim_ref[0]

    def carry_states(c, h):
        h_r, h_i = h
        rows = pl.ds(c, n_g, stride=n_chunks)
        s_r = sre[rows, :]
        s_i = sim[rows, :]
        sre[rows, :] = h_r
        sim[rows, :] = h_i
        return a_r * h_r - a_i * h_i + s_r, a_r * h_i + a_i * h_r + s_i

    zero = jnp.zeros((n_g, SSM_STATE), F32)
    lax.fori_loop(0, n_chunks, carry_states, (zero, zero))

    nt = (((1,), (1,)), ((), ()))

    def outputs(g, carry):
        yt = jnp.dot(mT_ref[g], ut_scr[g], preferred_element_type=F32)
        yt = yt + lax.dot_general(qTre_ref[g], sre[group_rows(g), :].astype(BF16), nt, preferred_element_type=F32)
        yt = yt + lax.dot_general(qTim_ref[g], sim[group_rows(g), :].astype(BF16), nt, preferred_element_type=F32)
        yt_scr[g] = _gelu(yt)
        return carry

    lax.fori_loop(0, n_g, outputs, 0)

    for t in range(t_n):
        z = yt_scr[:, t * c_n:(t + 1) * c_n, :].reshape(n_g * c_n, n_chunks)
        o_ref[0, pl.ds(t, n_chunks, stride=t_n), :] = z.T


def _ssm(u_ssm, mats):
    bsz, seq, width = u_ssm.shape
    n_chunks = seq // SSM_CHUNK
    mT, pT_re, pT_im, qT_re, qT_im, a_re, a_im = mats
    gt = SSM_GROUP_TILE
    n_tiles = SSM_GROUPS // gt
    rows = SSM_CHUNK * SSM_GROUP_SIZE
    a_re = a_re.reshape(n_tiles, gt, SSM_STATE)
    a_im = a_im.reshape(n_tiles, gt, SSM_STATE)
    per_tile = lambda a: pl.BlockSpec((gt,) + a.shape[1:], lambda t, b: (t, 0, 0))
    per_tile1 = lambda a: pl.BlockSpec((1,) + a.shape[1:], lambda t, b: (t, 0, 0))
    return pl.pallas_call(
        functools.partial(_ssm_kernel, n_chunks=n_chunks),
        grid=(n_tiles, bsz),
        in_specs=[
            pl.BlockSpec((1, seq, gt * SSM_GROUP_SIZE), lambda t, b: (b, 0, t)),
            per_tile(mT), per_tile(pT_re), per_tile(pT_im), per_tile(qT_re), per_tile(qT_im),
            per_tile1(a_re), per_tile1(a_im),
        ],
        out_specs=pl.BlockSpec((1, seq, gt * SSM_GROUP_SIZE), lambda t, b: (b, 0, t)),
        out_shape=jax.ShapeDtypeStruct(u_ssm.shape, F32),
        scratch_shapes=[
            pltpu.VMEM((gt, rows, n_chunks), BF16),
            pltpu.VMEM((gt, rows, n_chunks), F32),
            pltpu.VMEM((gt * n_chunks, SSM_STATE), F32),
            pltpu.VMEM((gt * n_chunks, SSM_STATE), F32),
        ],
        compiler_params=_params(2),
        name="ssm",
    )(u_ssm, mT, pT_re, pT_im, qT_re, qT_im, a_re, a_im)


def _compress_kernel(kx_ref, vx_ref, pek_ref, pev_ref, wk1_ref, wk2_ref, wv1_ref, wv2_ref, kc_ref, vcT_ref):
    half = CMP_STRIDE * HEAD_DIM

    def mlp(x, pe, w1_ref, w2_ref):
        n = x.shape[0]
        first = jnp.dot(x, w1_ref[:half, :], precision=HIGHEST, preferred_element_type=F32)
        second = jnp.dot(x, w1_ref[half:, :], precision=HIGHEST, preferred_element_type=F32)
        bias = jnp.dot(pe, w1_ref[...], precision=HIGHEST, preferred_element_type=F32)
        h = _gelu(first + pltpu.roll(second, n - 1, 0) + bias)
        return jnp.dot(h, w2_ref[...], precision=HIGHEST, preferred_element_type=F32)

    kc = mlp(kx_ref[0, 0], pek_ref[...], wk1_ref, wk2_ref)
    kc_hi = kc.astype(BF16)
    kc_lo = (kc - kc_hi.astype(F32)).astype(BF16)
    kc_ref[0, 0] = jnp.concatenate([kc_hi, kc_hi, kc_lo, jnp.zeros_like(kc_hi)], axis=1)
    vcT_ref[0, 0] = mlp(vx_ref[0, 0], pev_ref[...], wv1_ref, wv2_ref).T.astype(BF16)


def _compress(kx, vx, pe_k, pe_v, wk1, wk2, wv1, wv2):
    bsz, n_g, n_chunks, width = kx.shape
    blk = pl.BlockSpec((1, 1, n_chunks, width), lambda b, g: (b, g, 0, 0))
    full = lambda a: pl.BlockSpec(a.shape, lambda b, g: (0, 0))
    return pl.pallas_call(
        _compress_kernel,
        grid=(bsz, n_g),
        in_specs=[blk, blk, full(pe_k), full(pe_v), full(wk1), full(wk2), full(wv1), full(wv2)],
        out_specs=(
            pl.BlockSpec((1, 1, n_chunks, 4 * HEAD_DIM), lambda b, g: (b, g, 0, 0)),
            pl.BlockSpec((1, 1, HEAD_DIM, n_chunks), lambda b, g: (b, g, 0, 0)),
        ),
        out_shape=(
            jax.ShapeDtypeStruct((bsz, n_g, n_chunks, 4 * HEAD_DIM), BF16),
            jax.ShapeDtypeStruct((bsz, n_g, HEAD_DIM, n_chunks), BF16),
        ),
        compiler_params=_params(2),
        name="compress",
    )(kx, vx, pe_k, pe_v, wk1, wk2, wv1, wv2)


def _nsa_kernel(qT_ref, kc_ref, vcT_ref, ks_ref, vs_ref, kw_ref, vw_ref, bg_ref, o_ref,
                negm_scr, qa_scr, m_scr, acc_scr, out_scr, *, tq, n_cmp, n_sel):
    g = pl.program_id(1)
    i = pl.program_id(2)
    t_row = i * tq + lax.broadcasted_iota(jnp.int32, (1, tq), 1)
    slopes = [jnp.where(g == 0, 2.0 ** -(hh + 1), 2.0 ** -(HEADS_PER_GROUP + hh + 1)).astype(F32)
              for hh in range(HEADS_PER_GROUP)]

    def gate(hh, branch):
        r = hh * N_NSA_BRANCHES + branch
        return bg_ref[0, 0, r:r + 1, :]

    def head_rows(hh):
        return slice(hh * HEAD_DIM, (hh + 1) * HEAD_DIM)

    def head_lanes(hh):
        return slice(hh * tq, (hh + 1) * tq)

    kc3 = kc_ref[0, 0]
    vcT = vcT_ref[0, 0]
    cmp_end = lax.broadcasted_iota(jnp.int32, (n_cmp, 1), 0) * CMP_STRIDE + (CMP_BLOCK - 1)
    dist_c = (t_row - cmp_end).astype(F32)
    valid_c = dist_c >= 0.0
    p_sum = jnp.zeros((n_cmp, tq), F32)
    piece = lax.broadcasted_iota(jnp.int32, (16, 1), 0)
    log2e_pieces = _bf16_pieces(LOG2E, N_PIECES)
    log2e_col = jnp.zeros((16, 1), F32)
    for k, value in enumerate(log2e_pieces):
        log2e_col = jnp.where((piece == k) | (piece == N_PIECES + k), value, log2e_col)
    for hh in range(HEADS_PER_GROUP):
        q_h = qT_ref[0, head_rows(hh), :]
        q_hi = q_h.astype(BF16)
        q_lo = (q_h - q_hi.astype(F32)).astype(BF16)
        q3 = jnp.concatenate([q_hi, q_lo, q_hi, jnp.zeros_like(q_hi)], axis=0)
        s = jnp.dot(kc3, q3, preferred_element_type=F32) - slopes[hh] * dist_c
        s = jnp.where(valid_c, s, MASK_VALUE)
        m = jnp.max(s, axis=0, keepdims=True)
        p = jnp.where(valid_c, jnp.exp(s - m), 0.0)
        l = jnp.sum(p, axis=0, keepdims=True)
        p = p / jnp.where(l > 0.0, l, 1.0)
        o_c = jnp.dot(vcT, p.astype(BF16), preferred_element_type=F32)
        out_scr[head_rows(hh), :] = gate(hh, 0) * o_c
        p_sum = p_sum + p
        alibi = jnp.broadcast_to(log2e_col * slopes[hh], (16, tq)).astype(BF16)
        qa_scr[:, head_lanes(hh)] = jnp.concatenate([(q_h * LOG2E).astype(BF16), alibi], axis=0)

    blk = lax.broadcasted_iota(jnp.int32, (n_sel, n_cmp), 0)
    cmp_ix = lax.broadcasted_iota(jnp.int32, (n_sel, n_cmp), 1)
    ratio = SEL_BLOCK // CMP_STRIDE
    extra = CMP_BLOCK // CMP_STRIDE - 1
    overlap_t = jnp.where((cmp_ix >= ratio * blk - extra) & (cmp_ix <= ratio * blk + ratio - 1), 1.0, 0.0).astype(BF16)
    imp = jnp.zeros((n_sel, tq), F32)
    rest = p_sum
    for _ in range(3):
        part = rest.astype(BF16)
        imp = imp + jnp.dot(overlap_t, part, preferred_element_type=F32)
        rest = rest - part.astype(F32)

    j_col = lax.broadcasted_iota(jnp.int32, (n_sel, 1), 0)
    forced = (j_col == 0) | (j_col == lax.shift_right_logical(t_row, SEL_SHIFT))
    future = j_col * SEL_BLOCK > t_row
    score = jnp.where(forced, jnp.inf, jnp.where(future, -jnp.inf, imp))
    rank = jnp.zeros((n_sel, tq), F32)
    for jp in range(n_sel):
        other = score[jp:jp + 1, :]
        ahead = (other > score) | ((other == score) & (j_col > jp))
        rank = rank + jnp.where(ahead, 1.0, 0.0)
    neg = jnp.where(rank < float(min(SEL_TOP_K, n_sel)), 0.0, MASK_VALUE)
    pad_rows = jnp.zeros((16 - SEL_PER_KEY_BLOCK, tq), F32)
    for c in range(n_sel // SEL_PER_KEY_BLOCK):
        rows = neg[c * SEL_PER_KEY_BLOCK:(c + 1) * SEL_PER_KEY_BLOCK, :]
        mask_tile = jnp.concatenate([rows, pad_rows], axis=0).astype(BF16)
        negm_scr[c] = jnp.concatenate([mask_tile] * HEADS_PER_GROUP, axis=1)

    lanes = HEADS_PER_GROUP * tq
    rel = ((lax.broadcasted_iota(jnp.int32, (KEY_BLOCK, lanes), 1) & (tq - 1))
           - lax.broadcasted_iota(jnp.int32, (KEY_BLOCK, lanes), 0))
    zero_tail = jnp.zeros((KEY_AUG - QUERY_AUG, lanes), BF16)
    no_mask = jnp.zeros((16, lanes), BF16)

    sel_even, sel_odd, win_prev, win_diag = range(4)
    m_scr[...] = jnp.full(m_scr.shape, MASK_VALUE, F32)
    acc_scr[...] = jnp.zeros(acc_scr.shape, F32)

    def tile(k_ref, v_ref, kb, mask_tile, valid, slot):
        k0 = pl.multiple_of(kb * KEY_BLOCK, KEY_BLOCK)
        k_aug = k_ref[0, 0, 0, pl.ds(k0, KEY_BLOCK), :]
        v_aug = v_ref[0, 0, 0, kb]
        q_aug = jnp.concatenate([qa_scr[...], mask_tile, zero_tail], axis=0)
        s = jnp.dot(k_aug, q_aug, preferred_element_type=F32)
        if valid is not None:
            s = jnp.where(valid, s, MASK_VALUE)
        m_old = m_scr[slot]
        m_new = jnp.maximum(m_old, jnp.max(s, axis=0, keepdims=True))
        p = jnp.exp2(s - jnp.maximum(m_new, 0.1 * MASK_VALUE))
        alpha = jnp.exp2(m_old - m_new)
        acc_scr[slot] = alpha * acc_scr[slot] + jnp.dot(v_aug, p.astype(BF16), preferred_element_type=F32)
        m_scr[slot] = m_new

    def finish(branch, slot_a, slot_b):
        m_a = m_scr[slot_a]
        m_b = m_scr[slot_b]
        m = jnp.maximum(m_a, m_b)
        w_a = jnp.exp2(m_a - m)
        w_b = jnp.exp2(m_b - m)
        for hh in range(HEADS_PER_GROUP):
            cols = head_lanes(hh)
            acc = acc_scr[slot_a, :, cols] * w_a[:, cols] + acc_scr[slot_b, :, cols] * w_b[:, cols]
            l = acc[HEAD_DIM:HEAD_DIM + 1, :]
            o_b = acc[:HEAD_DIM, :] / jnp.where(l > 0.0, l, 1.0)
            out_scr[head_rows(hh), :] = out_scr[head_rows(hh), :] + gate(hh, branch) * o_b

    def past_pair(j, carry):
        tile(ks_ref, vs_ref, 2 * j, negm_scr[2 * j], None, sel_even)
        tile(ks_ref, vs_ref, 2 * j + 1, negm_scr[2 * j + 1], None, sel_odd)
        return carry

    lax.fori_loop(0, lax.shift_right_logical(i, 1), past_pair, 0)

    prev = jnp.maximum(i - 1, 0)
    unpaired = jnp.broadcast_to(i & 1, (KEY_BLOCK, lanes)) > 0
    causal = rel >= 0
    off_without_prev = jnp.where(i > 0, 0, WINDOW + KEY_BLOCK)
    dist_prev = rel + KEY_BLOCK + off_without_prev
    tile(ks_ref, vs_ref, prev, negm_scr[prev], unpaired, sel_even)
    tile(ks_ref, vs_ref, i, negm_scr[i], causal, sel_odd)
    tile(kw_ref, vw_ref, prev, no_mask, (dist_prev >= 0) & (dist_prev < WINDOW), win_prev)
    tile(kw_ref, vw_ref, i, no_mask, causal, win_diag)
    finish(1, sel_even, sel_odd)
    finish(2, win_prev, win_diag)

    o_ref[0] = out_scr[...].T


def _nsa(qT, kc3, vcT, k_aug, v_aug, bgT, *, tq):
    bsz, _, seq = qT.shape
    assert tq == KEY_BLOCK == WINDOW and seq % tq == 0
    n_cmp = kc3.shape[2]
    n_sel = seq // SEL_BLOCK
    n_kb = seq // KEY_BLOCK
    gw = HEADS_PER_GROUP * HEAD_DIM
    bg4 = bgT.reshape(bsz, N_KV_GROUPS, GATE_ROWS, seq)
    k_spec = lambda s: pl.BlockSpec((1, 1, 1, seq, KEY_AUG), lambda b, g, i: (b, s, g, 0, 0))
    v_spec = lambda s: pl.BlockSpec((1, 1, 1, n_kb, VAL_AUG, KEY_BLOCK), lambda b, g, i: (b, s, g, 0, 0, 0))
    return pl.pallas_call(
        functools.partial(_nsa_kernel, tq=tq, n_cmp=n_cmp, n_sel=n_sel),
        grid=(bsz, N_KV_GROUPS, seq // tq),
        in_specs=[
            pl.BlockSpec((1, gw, tq), lambda b, g, i: (b, g, i)),
            pl.BlockSpec((1, 1, n_cmp, 4 * HEAD_DIM), lambda b, g, i: (b, g, 0, 0)),
            pl.BlockSpec((1, 1, HEAD_DIM, n_cmp), lambda b, g, i: (b, g, 0, 0)),
            k_spec(0), v_spec(0),
            k_spec(1), v_spec(1),
            pl.BlockSpec((1, 1, GATE_ROWS, tq), lambda b, g, i: (b, g, 0, i)),
        ],
        out_specs=pl.BlockSpec((1, tq, gw), lambda b, g, i: (b, i, g)),
        out_shape=jax.ShapeDtypeStruct((bsz, seq, ATTN_WIDTH), F32),
        scratch_shapes=[
            pltpu.VMEM((n_kb, 16, HEADS_PER_GROUP * tq), BF16),
            pltpu.VMEM((HEAD_DIM + 16, HEADS_PER_GROUP * tq), BF16),
            pltpu.VMEM((4, 1, HEADS_PER_GROUP * tq), F32),
            pltpu.VMEM((4, VAL_AUG, HEADS_PER_GROUP * tq), F32),
            pltpu.VMEM((gw, tq), F32),
        ],
        compiler_params=_params(3),
        name="nsa",
    )(qT, kc3, vcT, k_aug, v_aug, k_aug, v_aug, bg4)


def _merge_kernel(x_ref, g_ref, o_ref_in, sc_ref, sh_ref, gt_ref, wm_ref, wl_ref, wg_ref, wn_ref, wo_ref,
                  lg_ref, lb_ref, y_ref):
    x = x_ref[0]
    u = (x * (1.0 + sc_ref[0]) + sh_ref[0]).astype(BF16)
    mg = jnp.dot(u, wm_ref[...], preferred_element_type=F32)
    g = g_ref[0].astype(BF16)
    z_ssm = jnp.dot(g, wl_ref[...], preferred_element_type=F32) * _sigmoid(
        jnp.dot(g, wg_ref[...], preferred_element_type=F32))
    z_nsa = jnp.dot(o_ref_in[0].astype(BF16), wn_ref[...], preferred_element_type=F32)
    merged = _sigmoid(mg[:, :D_MODEL]) * z_ssm + _sigmoid(mg[:, D_MODEL:]) * z_nsa
    mix = jnp.dot(merged.astype(BF16), wo_ref[...], preferred_element_type=F32)
    y = DEEPNORM_ALPHA * x + (1.0 + gt_ref[0]) * mix
    y_ref[0] = _layer_norm(y, lg_ref[...], lb_ref[...])


def _merge(x, g_ssm, o_nsa, scale1, shift1, gate1, w_m, w_lin, w_gate, w_nsa, w_out, ln_g, ln_b, *, tm):
    bsz, seq, _ = x.shape
    row = pl.BlockSpec((1, 1, D_MODEL), lambda b, i: (b, 0, 0))
    full = lambda a: pl.BlockSpec(a.shape, lambda b, i: (0, 0))
    return pl.pallas_call(
        _merge_kernel,
        grid=(bsz, seq // tm),
        in_specs=[
            pl.BlockSpec((1, tm, D_MODEL), lambda b, i: (b, i, 0)),
            pl.BlockSpec((1, tm, SSM_WIDTH), lambda b, i: (b, i, 0)),
            pl.BlockSpec((1, tm, ATTN_WIDTH), lambda b, i: (b, i, 0)),
            row, row, row,
            full(w_m), full(w_lin), full(w_gate), full(w_nsa), full(w_out), full(ln_g), full(ln_b),
        ],
        out_specs=pl.BlockSpec((1, tm, D_MODEL), lambda b, i: (b, i, 0)),
        out_shape=jax.ShapeDtypeStruct(x.shape, F32),
        compiler_params=_params(2),
        name="merge",
    )(x, g_ssm, o_nsa, scale1, shift1, gate1, w_m, w_lin, w_gate, w_nsa, w_out, ln_g, ln_b)


def _ffn_kernel(x_ref, sc_ref, sh_ref, gt_ref, wg_ref, wu_ref, wd_ref, lg_ref, lb_ref, y_ref, u_scr, acc_scr):
    j = pl.program_id(2)

    @pl.when(j == 0)
    def _():
        u_scr[...] = (x_ref[0] * (1.0 + sc_ref[0]) + sh_ref[0]).astype(BF16)
        acc_scr[...] = jnp.zeros(acc_scr.shape, F32)

    u = u_scr[...]
    a = jnp.dot(u, wg_ref[...], preferred_element_type=F32)
    h = (a * _sigmoid(a)) * jnp.dot(u, wu_ref[...], preferred_element_type=F32)
    acc_scr[...] += jnp.dot(h.astype(BF16), wd_ref[...], preferred_element_type=F32)

    @pl.when(j == pl.num_programs(2) - 1)
    def _():
        y = DEEPNORM_ALPHA * x_ref[0] + (1.0 + gt_ref[0]) * acc_scr[...]
        y_ref[0] = _layer_norm(y, lg_ref[...], lb_ref[...])


def _ffn(x, scale2, shift2, gate2, w_gate, w_up, w_down, ln_g, ln_b, *, tm, th):
    bsz, seq, _ = x.shape
    row = pl.BlockSpec((1, 1, D_MODEL), lambda b, i, j: (b, 0, 0))
    vec = pl.BlockSpec((1, D_MODEL), lambda b, i, j: (0, 0))
    return pl.pallas_call(
        _ffn_kernel,
        grid=(bsz, seq // tm, FFN_HIDDEN // th),
        in_specs=[
            pl.BlockSpec((1, tm, D_MODEL), lambda b, i, j: (b, i, 0)),
            row, row, row,
            pl.BlockSpec((D_MODEL, th), lambda b, i, j: (0, j)),
            pl.BlockSpec((D_MODEL, th), lambda b, i, j: (0, j)),
            pl.BlockSpec((th, D_MODEL), lambda b, i, j: (j, 0)),
            vec, vec,
        ],
        out_specs=pl.BlockSpec((1, tm, D_MODEL), lambda b, i, j: (b, i, 0)),
        out_shape=jax.ShapeDtypeStruct(x.shape, F32),
        scratch_shapes=[pltpu.VMEM((tm, D_MODEL), BF16), pltpu.VMEM((tm, D_MODEL), F32)],
        compiler_params=_params(3),
        name="ffn",
    )(x, scale2, shift2, gate2, w_gate, w_up, w_down, ln_g, ln_b)


def _layer(x, c, w_ada, b_ada, w_in, ssm_a_re, ssm_a_im, ssm_log_dt, ssm_b_re, ssm_b_im, ssm_c_re, ssm_c_im,
           ssm_d, w_glu_lin, w_glu_gate, cmp_pe_k, cmp_pe_v, w_cmp_k1, w_cmp_k2, w_cmp_v1, w_cmp_v2,
           w_nsa_proj, w_out, ln1_g, ln1_b, w_ffn_gate, w_ffn_up, w_ffn_down, ln2_g, ln2_b):
    bsz, seq, _ = x.shape
    n_chunks = seq // SSM_CHUNK
    tq = min(256, seq)
    tm = min(512, seq)

    mod = _ada_mod(c, w_ada, b_ada).reshape(6, bsz, 1, D_MODEL)
    shift1, scale1, gate1, shift2, scale2, gate2 = (mod[k] for k in range(6))

    w_nat, w_tr, w_mg = _split_w_in(w_in)
    u_ssm, kv_cmp, k_aug, qT, v_aug, bgT = _in_proj(x, scale1, shift1, w_nat, w_tr, tm=tm)

    mats = _ssm_matrices(ssm_a_re, ssm_a_im, ssm_log_dt, ssm_b_re, ssm_b_im, ssm_c_re, ssm_c_im, ssm_d)
    g_ssm = _ssm(u_ssm, mats)

    chunk_w = CMP_STRIDE * HEAD_DIM
    kx = kv_cmp[:, 0].reshape(bsz, N_KV_GROUPS, seq // CMP_STRIDE, chunk_w)
    vx = kv_cmp[:, 1].reshape(bsz, N_KV_GROUPS, seq // CMP_STRIDE, chunk_w)
    kc3, vcT = _compress(kx, vx,
                        cmp_pe_k.reshape(1, CMP_BLOCK * HEAD_DIM), cmp_pe_v.reshape(1, CMP_BLOCK * HEAD_DIM),
                        w_cmp_k1.reshape(CMP_BLOCK * HEAD_DIM, -1), w_cmp_k2,
                        w_cmp_v1.reshape(CMP_BLOCK * HEAD_DIM, -1), w_cmp_v2)
    o_nsa = _nsa(qT, kc3, vcT, k_aug, v_aug, bgT, tq=tq)

    x1 = _merge(x, g_ssm, o_nsa, scale1, shift1, gate1, w_mg,
                w_glu_lin.astype(BF16), w_glu_gate.astype(BF16), w_nsa_proj.astype(BF16), w_out.astype(BF16),
                ln1_g.reshape(1, D_MODEL), ln1_b.reshape(1, D_MODEL), tm=tm)
    return _ffn(x1, scale2, shift2, gate2, w_ffn_gate.astype(BF16), w_ffn_up.astype(BF16),
                w_ffn_down.astype(BF16), ln2_g.reshape(1, D_MODEL), ln2_b.reshape(1, D_MODEL), tm=tm, th=256)


def kernel(x, c, w_ada, b_ada, w_in, ssm_a_re, ssm_a_im, ssm_log_dt, ssm_b_re, ssm_b_im, ssm_c_re, ssm_c_im,
           ssm_d, w_glu_lin, w_glu_gate, cmp_pe_k, cmp_pe_v, w_cmp_k1, w_cmp_k2, w_cmp_v1, w_cmp_v2,
           w_nsa_proj, w_out, ln1_g, ln1_b, w_ffn_gate, w_ffn_up, w_ffn_down, ln2_g, ln2_b):
    for l in range(w_ada.shape[0]):
        x = _layer(x, c, w_ada[l], b_ada[l], w_in[l], ssm_a_re[l], ssm_a_im[l], ssm_log_dt[l],
                   ssm_b_re[l], ssm_b_im[l], ssm_c_re[l], ssm_c_im[l], ssm_d[l],
                   w_glu_lin[l], w_glu_gate[l], cmp_pe_k[l], cmp_pe_v[l],
                   w_cmp_k1[l], w_cmp_k2[l], w_cmp_v1[l], w_cmp_v2[l], w_nsa_proj[l], w_out[l],
                   ln1_g[l], ln1_b[l], w_ffn_gate[l], w_ffn_up[l], w_ffn_down[l], ln2_g[l], ln2_b[l])
    return x
```

```python
import functools
import math

import jax
import jax.numpy as jnp
from jax import lax
from jax.experimental import pallas as pl
from jax.experimental.pallas import tpu as pltpu

F32 = jnp.float32
BF16 = jnp.bfloat16
HIGHEST = lax.Precision.HIGHEST

D_MODEL = 1024
SSM_WIDTH = D_MODEL // 2
SSM_GROUP_SIZE = 16
SSM_GROUPS = SSM_WIDTH // SSM_GROUP_SIZE
SSM_STATE = 64
SSM_CHUNK = 16
SSM_GROUP_TILE = 8
N_HEADS = 8
HEAD_DIM = 64
N_KV_GROUPS = 2
HEADS_PER_GROUP = N_HEADS // N_KV_GROUPS
ATTN_WIDTH = N_HEADS * HEAD_DIM
KV_WIDTH = N_KV_GROUPS * HEAD_DIM
CMP_BLOCK = 32
CMP_STRIDE = 16
SEL_BLOCK = 64
SEL_TOP_K = 8
WINDOW = 256
N_NSA_BRANCHES = 3
GATE_ROWS = 16
FFN_HIDDEN = (8 * D_MODEL + 3 * 256 - 1) // (3 * 256) * 256
DEEPNORM_ALPHA = 2.0 ** 0.25
LN_EPS = 1e-5
MASK_VALUE = -1e30

VMEM_LIMIT = 56 * 1024 * 1024

KEY_BLOCK = 256
SEL_SHIFT = SEL_BLOCK.bit_length() - 1
SEL_PER_KEY_BLOCK = KEY_BLOCK // SEL_BLOCK
KEY_AUG = 128
ALIBI_COL = HEAD_DIM
N_PIECES = 4
MASK_COL = HEAD_DIM + 16
QUERY_AUG = HEAD_DIM + 32
VAL_AUG = HEAD_DIM + 16
LOG2E = math.log2(math.e)

NAT_COLS = SSM_WIDTH + 2 * KV_WIDTH + 2 * N_KV_GROUPS * KEY_AUG
TR_ROWS = ATTN_WIDTH + 2 * N_KV_GROUPS * VAL_AUG + N_KV_GROUPS * GATE_ROWS


def _bf16_pieces(value, n):
    pieces = []
    rest = value
    for _ in range(n):
        mant, expo = math.frexp(rest)
        piece = math.ldexp(round(mant * 256.0) / 256.0, expo)
        pieces.append(piece)
        rest -= piece
    return pieces


def _sigmoid(x):
    return 1.0 / (1.0 + jnp.exp(-x))


def _gelu(x):
    c = math.sqrt(2.0 / math.pi)
    return 0.5 * x * (1.0 + jnp.tanh(c * (x + 0.044715 * (x * x * x))))


def _layer_norm(y, gain, bias):
    mu = jnp.mean(y, axis=-1, keepdims=True)
    d = y - mu
    var = jnp.mean(d * d, axis=-1, keepdims=True)
    return d * lax.rsqrt(var + LN_EPS) * gain + bias


def _params(n_axes, flags=None):
    return pltpu.CompilerParams(dimension_semantics=("arbitrary",) * n_axes, vmem_limit_bytes=VMEM_LIMIT, flags=flags)


def _ada_kernel(c_ref, w_ref, b_ref, o_ref):
    c = c_ref[...]
    a = c * _sigmoid(c)
    o_ref[0] = jnp.dot(a, w_ref[...], precision=HIGHEST, preferred_element_type=F32) + b_ref[...]


def _ada_mod(c, w_ada, b_ada):
    bsz = c.shape[0]
    return pl.pallas_call(
        _ada_kernel,
        grid=(6,),
        in_specs=[
            pl.BlockSpec((bsz, D_MODEL), lambda j: (0, 0)),
            pl.BlockSpec((D_MODEL, D_MODEL), lambda j: (0, j)),
            pl.BlockSpec((1, D_MODEL), lambda j: (0, j)),
        ],
        out_specs=pl.BlockSpec((1, bsz, D_MODEL), lambda j: (j, 0, 0)),
        out_shape=jax.ShapeDtypeStruct((6, bsz, D_MODEL), F32),
        compiler_params=_params(1),
        name="ada_mod",
    )(c, w_ada, b_ada.reshape(1, 6 * D_MODEL))


def _in_proj_kernel(x_ref, sc_ref, sh_ref, wn_ref, wt_ref, ussm_ref, kc_ref, kp_ref, qT_ref, vp_ref, bgT_ref):
    i = pl.program_id(1)
    tm = x_ref.shape[1]
    u = (x_ref[0] * (1.0 + sc_ref[0]) + sh_ref[0]).astype(BF16)
    nat = jnp.dot(u, wn_ref[...], preferred_element_type=F32)
    ussm_ref[0] = nat[:, :SSM_WIDTH]
    for s in range(2):
        for g in range(N_KV_GROUPS):
            lo = SSM_WIDTH + s * KV_WIDTH + g * HEAD_DIM
            kc_ref[0, s, g] = nat[:, lo:lo + HEAD_DIM]
    pos = i * tm + lax.broadcasted_iota(jnp.int32, (tm, KEY_AUG), 0)
    col = lax.broadcasted_iota(jnp.int32, (tm, KEY_AUG), 1)
    blk = lax.shift_right_logical(pos, SEL_SHIFT)
    in_a = (col >= ALIBI_COL) & (col < ALIBI_COL + N_PIECES)
    in_b = (col >= ALIBI_COL + N_PIECES) & (col < ALIBI_COL + 2 * N_PIECES)
    hot = col == MASK_COL + (blk & (SEL_PER_KEY_BLOCK - 1))
    aux = jnp.where(in_a, blk * SEL_BLOCK, jnp.where(in_b, pos & (SEL_BLOCK - 1), jnp.where(hot, 1, 0))).astype(F32)
    for s in range(2):
        for g in range(N_KV_GROUPS):
            lo = SSM_WIDTH + 2 * KV_WIDTH + (s * N_KV_GROUPS + g) * KEY_AUG
            kp_ref[0, s, g] = (nat[:, lo:lo + KEY_AUG] + aux).astype(BF16)
    tr = lax.dot_general(wt_ref[...], u, (((1,), (1,)), ((), ())), preferred_element_type=F32)
    qT_ref[0] = tr[:ATTN_WIDTH] * (HEAD_DIM ** -0.5)
    ones_row = jnp.where(lax.broadcasted_iota(jnp.int32, (VAL_AUG, 1), 0) == HEAD_DIM, 1.0, 0.0)
    for s in range(2):
        for g in range(N_KV_GROUPS):
            lo = ATTN_WIDTH + (s * N_KV_GROUPS + g) * VAL_AUG
            v_aug = (tr[lo:lo + VAL_AUG] + ones_row).astype(BF16)
            for c in range(tm // KEY_BLOCK):
                vp_ref[0, s, g, c] = v_aug[:, c * KEY_BLOCK:(c + 1) * KEY_BLOCK]
    bgT_ref[0] = _sigmoid(tr[ATTN_WIDTH + 2 * N_KV_GROUPS * VAL_AUG:])


def _in_proj(x, scale1, shift1, w_nat, w_tr, *, tm):
    bsz, seq, _ = x.shape
    n_t = seq // tm
    n_kb = seq // KEY_BLOCK
    out_shapes = (
        jax.ShapeDtypeStruct((bsz, seq, SSM_WIDTH), F32),
        jax.ShapeDtypeStruct((bsz, 2, N_KV_GROUPS, seq, HEAD_DIM), F32),
        jax.ShapeDtypeStruct((bsz, 2, N_KV_GROUPS, seq, KEY_AUG), BF16),
        jax.ShapeDtypeStruct((bsz, ATTN_WIDTH, seq), F32),
        jax.ShapeDtypeStruct((bsz, 2, N_KV_GROUPS, n_kb, VAL_AUG, KEY_BLOCK), BF16),
        jax.ShapeDtypeStruct((bsz, N_KV_GROUPS * GATE_ROWS, seq), F32),
    )
    return pl.pallas_call(
        _in_proj_kernel,
        grid=(bsz, n_t),
        in_specs=[
            pl.BlockSpec((1, tm, D_MODEL), lambda b, i: (b, i, 0)),
            pl.BlockSpec((1, 1, D_MODEL), lambda b, i: (b, 0, 0)),
            pl.BlockSpec((1, 1, D_MODEL), lambda b, i: (b, 0, 0)),
            pl.BlockSpec((D_MODEL, NAT_COLS), lambda b, i: (0, 0)),
            pl.BlockSpec((TR_ROWS, D_MODEL), lambda b, i: (0, 0)),
        ],
        out_specs=(
            pl.BlockSpec((1, tm, SSM_WIDTH), lambda b, i: (b, i, 0)),
            pl.BlockSpec((1, 2, N_KV_GROUPS, tm, HEAD_DIM), lambda b, i: (b, 0, 0, i, 0)),
            pl.BlockSpec((1, 2, N_KV_GROUPS, tm, KEY_AUG), lambda b, i: (b, 0, 0, i, 0)),
            pl.BlockSpec((1, ATTN_WIDTH, tm), lambda b, i: (b, 0, i)),
            pl.BlockSpec((1, 2, N_KV_GROUPS, tm // KEY_BLOCK, VAL_AUG, KEY_BLOCK), lambda b, i: (b, 0, 0, i, 0, 0)),
            pl.BlockSpec((1, N_KV_GROUPS * GATE_ROWS, tm), lambda b, i: (b, 0, i)),
        ),
        out_shape=out_shapes,
        compiler_params=_params(2),
        name="in_proj",
    )(x, scale1, shift1, w_nat, w_tr)


def _split_w_in(w_in):
    o_q = SSM_WIDTH
    o_kv = o_q + ATTN_WIDTH
    o_bg = o_kv + 6 * KV_WIDTH
    o_mg = o_bg + N_NSA_BRANCHES * N_HEADS
    kv = [w_in[:, o_kv + s * KV_WIDTH:o_kv + (s + 1) * KV_WIDTH].reshape(D_MODEL, N_KV_GROUPS, HEAD_DIM)
          for s in range(6)]
    pad_k = lambda w: jnp.pad(w, ((0, 0), (0, 0), (0, KEY_AUG - HEAD_DIM))).reshape(D_MODEL, N_KV_GROUPS * KEY_AUG)
    pad_v = lambda w: jnp.pad(w, ((0, 0), (0, 0), (0, VAL_AUG - HEAD_DIM))).reshape(D_MODEL, N_KV_GROUPS * VAL_AUG)
    flat = lambda w: w.reshape(D_MODEL, KV_WIDTH)
    w_nat = jnp.concatenate([w_in[:, :o_q], flat(kv[0]), flat(kv[1]), pad_k(kv[2]), pad_k(kv[4])], axis=1)
    per_group = HEADS_PER_GROUP * N_NSA_BRANCHES
    bg = w_in[:, o_bg:o_mg].reshape(D_MODEL, N_KV_GROUPS, per_group)
    bg = jnp.pad(bg, ((0, 0), (0, 0), (0, GATE_ROWS - per_group))).reshape(D_MODEL, N_KV_GROUPS * GATE_ROWS)
    w_tr = jnp.concatenate([w_in[:, o_q:o_kv], pad_v(kv[3]), pad_v(kv[5]), bg], axis=1).T
    return w_nat.astype(BF16), w_tr.astype(BF16), w_in[:, o_mg:].astype(BF16)


def _ssm_prep_kernel(a_col_ref, a_row_ref, logdt_ref, b_ref, c_ref, d_ref,
                     mT_ref, pT_ref, qT_ref, a16_ref):
    t_n = SSM_CHUNK
    c_n = SSM_GROUP_SIZE
    rows = t_n * c_n
    dt = jnp.exp(logdt_ref[0])

    def discretise(a_re, a_im):
        lam_re = a_re * dt
        lam_im = a_im * dt
        decay = jnp.exp(lam_re)
        n_re = decay * jnp.cos(lam_im) - 1.0
        n_im = decay * jnp.sin(lam_im)
        denom = a_re * a_re + a_im * a_im
        return lam_re, lam_im, (n_re * a_re + n_im * a_im) / denom, (n_im * a_re - n_re * a_im) / denom

    def power(lam_re, lam_im, k):
        mag = jnp.exp(lam_re * k)
        return mag * jnp.cos(lam_im * k), mag * jnp.sin(lam_im * k)

    lam_re, lam_im, f_re, f_im = discretise(a_col_ref[0, 0], a_col_ref[0, 1])
    bb_re = f_re * b_ref[0, 0] - f_im * b_ref[0, 1]
    bb_im = f_re * b_ref[0, 1] + f_im * b_ref[0, 0]
    lane = lax.broadcasted_iota(jnp.int32, (1, rows), 1)
    s_lane = lax.shift_right_logical(lane, c_n.bit_length() - 1)
    ci_lane = lane & (c_n - 1)
    expand = jnp.where(lax.broadcasted_iota(jnp.int32, (c_n, rows), 0) == ci_lane, 1.0, 0.0)
    bt_re = jnp.dot(bb_re, expand, precision=HIGHEST, preferred_element_type=F32)
    bt_im = jnp.dot(bb_im, expand, precision=HIGHEST, preferred_element_type=F32)
    pw_re, pw_im = power(lam_re, lam_im, (t_n - 1 - s_lane).astype(F32))
    pT_ref[0, 0] = (pw_re * bt_re - pw_im * bt_im).astype(BF16)
    pT_ref[0, 1] = (pw_re * bt_im + pw_im * bt_re).astype(BF16)

    lam_re_r, lam_im_r, _, _ = discretise(a_row_ref[0, 0], a_row_ref[0, 1])
    row = lax.broadcasted_iota(jnp.int32, (rows, 1), 0)
    t_row = lax.shift_right_logical(row, c_n.bit_length() - 1)
    co_row = row & (c_n - 1)
    ct_re = jnp.concatenate([c_ref[0, 0]] * t_n, axis=0)
    ct_im = jnp.concatenate([c_ref[0, 1]] * t_n, axis=0)

    def c_times_power(k):
        p_re, p_im = power(lam_re_r, lam_im_r, k)
        return ct_re * p_re - ct_im * p_im, ct_re * p_im + ct_im * p_re

    q_re, q_im = c_times_power((t_row + 1).astype(F32))
    qT_ref[0, 0] = q_re.astype(BF16)
    qT_ref[0, 1] = (-q_im).astype(BF16)
    a16_ref[0] = jnp.concatenate(power(lam_re_r, lam_im_r, float(t_n)), axis=0)

    k_re, k_im = c_times_power(t_row.astype(F32))
    strips = (jnp.dot(k_re, bt_re, precision=HIGHEST, preferred_element_type=F32)
              - jnp.dot(k_im, bt_im, precision=HIGHEST, preferred_element_type=F32))
    blocks = []
    for t in range(t_n):
        acc = jnp.zeros((c_n, rows), F32)
        for lag in range(t + 1):
            acc = jnp.where(s_lane == t - lag, strips[lag * c_n:(lag + 1) * c_n, :], acc)
        blocks.append(acc)
    d_rows = jnp.concatenate([d_ref[0]] * t_n, axis=0)
    skip = jnp.where((t_row == s_lane) & (co_row == ci_lane), d_rows, 0.0)
    mT_ref[0] = (jnp.concatenate(blocks, axis=0) + skip).astype(BF16)


def _ssm_matrices(a_re, a_im, log_dt, b_re, b_im, c_re, c_im, d_skip):
    n_g, n_p = a_re.shape
    c_n = SSM_GROUP_SIZE
    rows = SSM_CHUNK * c_n
    a = jnp.stack([a_re, a_im], axis=1).astype(F32)
    spec = lambda shape: pl.BlockSpec((1,) + shape, lambda g: (g,) + (0,) * len(shape))
    mT, pT, qT, a16 = pl.pallas_call(
        _ssm_prep_kernel,
        grid=(n_g,),
        in_specs=[spec((2, n_p, 1)), spec((2, 1, n_p)), spec((1, 1)), spec((2, n_p, c_n)), spec((2, c_n, n_p)),
                  spec((c_n, 1))],
        out_specs=(spec((rows, rows)), spec((2, n_p, rows)), spec((2, rows, n_p)), spec((2, n_p))),
        out_shape=(
            jax.ShapeDtypeStruct((n_g, rows, rows), BF16),
            jax.ShapeDtypeStruct((n_g, 2, n_p, rows), BF16),
            jax.ShapeDtypeStruct((n_g, 2, rows, n_p), BF16),
            jax.ShapeDtypeStruct((n_g, 2, n_p), F32),
        ),
        compiler_params=_params(1),
        name="ssm_prep",
    )(a.reshape(n_g, 2, n_p, 1), a.reshape(n_g, 2, 1, n_p), log_dt.astype(F32).reshape(n_g, 1, 1),
      jnp.stack([b_re, b_im], axis=1).astype(F32), jnp.stack([c_re, c_im], axis=1).astype(F32),
      d_skip.astype(F32).reshape(n_g, c_n, 1))
    return mT, pT, qT, a16


def _ssm_kernel(u_ref, mT_ref, pT_ref, qT_ref, a16_ref, o_ref, ut_scr, yt_scr, sre, sim, *, n_chunks):
    n_g = SSM_GROUP_TILE
    t_n = SSM_CHUNK
    c_n = SSM_GROUP_SIZE

    for s in range(t_n):
        x_s = u_ref[0, pl.ds(s, n_chunks, stride=t_n), :]
        ut_scr[:, s * c_n:(s + 1) * c_n, :] = x_s.T.reshape(n_g, c_n, n_chunks).astype(BF16)

    def group_rows(g):
        return pl.ds(pl.multiple_of(g * n_chunks, n_chunks), n_chunks)

    def chunk_states(g, carry):
        ut = ut_scr[g]
        sre[group_rows(g), :] = jnp.dot(pT_ref[g, 0], ut, preferred_element_type=F32).T
        sim[group_rows(g), :] = jnp.dot(pT_ref[g, 1], ut, preferred_element_type=F32).T
        return carry

    lax.fori_loop(0, n_g, chunk_states, 0)

    a_r = a16_ref[pl.ds(0, n_g, stride=2), :]
    a_i = a16_ref[pl.ds(1, n_g, stride=2), :]

    def carry_states(c, h):
        h_r, h_i = h
        rows = pl.ds(c, n_g, stride=n_chunks)
        s_r = sre[rows, :]
        s_i = sim[rows, :]
        sre[rows, :] = h_r
        sim[rows, :] = h_i
        return a_r * h_r - a_i * h_i + s_r, a_r * h_i + a_i * h_r + s_i

    zero = jnp.zeros((n_g, SSM_STATE), F32)
    lax.fori_loop(0, n_chunks, carry_states, (zero, zero))

    nt = (((1,), (1,)), ((), ()))

    def outputs(g, carry):
        yt = jnp.dot(mT_ref[g], ut_scr[g], preferred_element_type=F32)
        yt = yt + lax.dot_general(qT_ref[g, 0], sre[group_rows(g), :].astype(BF16), nt, preferred_element_type=F32)
        yt = yt + lax.dot_general(qT_ref[g, 1], sim[group_rows(g), :].astype(BF16), nt, preferred_element_type=F32)
        yt_scr[g] = _gelu(yt)
        return carry

    lax.fori_loop(0, n_g, outputs, 0)

    for t in range(t_n):
        z = yt_scr[:, t * c_n:(t + 1) * c_n, :].reshape(n_g * c_n, n_chunks)
        o_ref[0, pl.ds(t, n_chunks, stride=t_n), :] = z.T


def _ssm(u_ssm, mats):
    bsz, seq, width = u_ssm.shape
    n_chunks = seq // SSM_CHUNK
    mT, pT, qT, a16 = mats
    gt = SSM_GROUP_TILE
    n_tiles = SSM_GROUPS // gt
    rows = SSM_CHUNK * SSM_GROUP_SIZE
    a16 = a16.reshape(2 * SSM_GROUPS, SSM_STATE)
    per_tile = lambda a: pl.BlockSpec((gt,) + a.shape[1:], lambda t, b: (t,) + (0,) * (a.ndim - 1))
    return pl.pallas_call(
        functools.partial(_ssm_kernel, n_chunks=n_chunks),
        grid=(n_tiles, bsz),
        in_specs=[
            pl.BlockSpec((1, seq, gt * SSM_GROUP_SIZE), lambda t, b: (b, 0, t)),
            per_tile(mT), per_tile(pT), per_tile(qT),
            pl.BlockSpec((2 * gt, SSM_STATE), lambda t, b: (t, 0)),
        ],
        out_specs=pl.BlockSpec((1, seq, gt * SSM_GROUP_SIZE), lambda t, b: (b, 0, t)),
        out_shape=jax.ShapeDtypeStruct(u_ssm.shape, F32),
        scratch_shapes=[
            pltpu.VMEM((gt, rows, n_chunks), BF16),
            pltpu.VMEM((gt, rows, n_chunks), F32),
            pltpu.VMEM((gt * n_chunks, SSM_STATE), F32),
            pltpu.VMEM((gt * n_chunks, SSM_STATE), F32),
        ],
        compiler_params=_params(2),
        name="ssm",
    )(u_ssm, mT, pT, qT, a16)


def _compress_kernel(kx_ref, vx_ref, pek_ref, pev_ref, wk1_ref, wk2_ref, wv1_ref, wv2_ref, kc_ref, vcT_ref):
    half = CMP_STRIDE * HEAD_DIM

    def mlp(x, pe, w1_ref, w2_ref):
        n = x.shape[0]
        first = jnp.dot(x, w1_ref[:half, :], precision=HIGHEST, preferred_element_type=F32)
        second = jnp.dot(x, w1_ref[half:, :], precision=HIGHEST, preferred_element_type=F32)
        bias = jnp.dot(pe, w1_ref[...], precision=HIGHEST, preferred_element_type=F32)
        h = _gelu(first + pltpu.roll(second, n - 1, 0) + bias)
        return jnp.dot(h, w2_ref[...], precision=HIGHEST, preferred_element_type=F32)

    kc = mlp(kx_ref[0, 0, 0], pek_ref[...], wk1_ref, wk2_ref)
    kc_hi = kc.astype(BF16)
    kc_lo = (kc - kc_hi.astype(F32)).astype(BF16)
    kc_ref[0, 0] = jnp.concatenate([kc_hi, kc_hi, kc_lo, jnp.zeros_like(kc_hi)], axis=1)
    vcT_ref[0, 0] = mlp(vx_ref[0, 0, 0], pev_ref[...], wv1_ref, wv2_ref).T.astype(BF16)


def _compress(kv_chunks, pe_k, pe_v, wk1, wk2, wv1, wv2):
    bsz, _, n_g, n_chunks, width = kv_chunks.shape
    blk = lambda s: pl.BlockSpec((1, 1, 1, n_chunks, width), lambda b, g: (b, s, g, 0, 0))
    full = lambda a: pl.BlockSpec(a.shape, lambda b, g: (0, 0))
    return pl.pallas_call(
        _compress_kernel,
        grid=(bsz, n_g),
        in_specs=[blk(0), blk(1), full(pe_k), full(pe_v), full(wk1), full(wk2), full(wv1), full(wv2)],
        out_specs=(
            pl.BlockSpec((1, 1, n_chunks, 4 * HEAD_DIM), lambda b, g: (b, g, 0, 0)),
            pl.BlockSpec((1, 1, HEAD_DIM, n_chunks), lambda b, g: (b, g, 0, 0)),
        ),
        out_shape=(
            jax.ShapeDtypeStruct((bsz, n_g, n_chunks, 4 * HEAD_DIM), BF16),
            jax.ShapeDtypeStruct((bsz, n_g, HEAD_DIM, n_chunks), BF16),
        ),
        compiler_params=_params(2),
        name="compress",
    )(kv_chunks, kv_chunks, pe_k, pe_v, wk1, wk2, wv1, wv2)


def _nsa_kernel(qT_ref, kc_ref, vcT_ref, ks_ref, vs_ref, kw_ref, vw_ref, bg_ref, o_ref,
                negm_scr, qa_scr, m_scr, acc_scr, out_scr, *, tq, n_cmp, n_sel):
    g = pl.program_id(1)
    i = pl.program_id(2)
    t_row = i * tq + lax.broadcasted_iota(jnp.int32, (1, tq), 1)
    slopes = [jnp.where(g == 0, 2.0 ** -(hh + 1), 2.0 ** -(HEADS_PER_GROUP + hh + 1)).astype(F32)
              for hh in range(HEADS_PER_GROUP)]

    def gate(hh, branch):
        r = hh * N_NSA_BRANCHES + branch
        return bg_ref[0, 0, r:r + 1, :]

    def head_rows(hh):
        return slice(hh * HEAD_DIM, (hh + 1) * HEAD_DIM)

    def head_lanes(hh):
        return slice(hh * tq, (hh + 1) * tq)

    kc3 = kc_ref[0, 0]
    vcT = vcT_ref[0, 0]
    cmp_end = lax.broadcasted_iota(jnp.int32, (n_cmp, 1), 0) * CMP_STRIDE + (CMP_BLOCK - 1)
    dist_c = (t_row - cmp_end).astype(F32)
    valid_c = dist_c >= 0.0
    p_sum = jnp.zeros((n_cmp, tq), F32)
    piece = lax.broadcasted_iota(jnp.int32, (16, 1), 0)
    log2e_pieces = _bf16_pieces(LOG2E, N_PIECES)
    log2e_col = jnp.zeros((16, 1), F32)
    for k, value in enumerate(log2e_pieces):
        log2e_col = jnp.where((piece == k) | (piece == N_PIECES + k), value, log2e_col)
    for hh in range(HEADS_PER_GROUP):
        q_h = qT_ref[0, head_rows(hh), :]
        q_hi = q_h.astype(BF16)
        q_lo = (q_h - q_hi.astype(F32)).astype(BF16)
        q3 = jnp.concatenate([q_hi, q_lo, q_hi, jnp.zeros_like(q_hi)], axis=0)
        s = jnp.dot(kc3, q3, preferred_element_type=F32) - slopes[hh] * dist_c
        s = jnp.where(valid_c, s, MASK_VALUE)
        m = jnp.max(s, axis=0, keepdims=True)
        p = jnp.where(valid_c, jnp.exp(s - m), 0.0)
        l = jnp.sum(p, axis=0, keepdims=True)
        p = p / jnp.where(l > 0.0, l, 1.0)
        o_c = jnp.dot(vcT, p.astype(BF16), preferred_element_type=F32)
        out_scr[head_rows(hh), :] = gate(hh, 0) * o_c
        p_sum = p_sum + p
        alibi = jnp.broadcast_to(log2e_col * slopes[hh], (16, tq)).astype(BF16)
        qa_scr[:, head_lanes(hh)] = jnp.concatenate([(q_h * LOG2E).astype(BF16), alibi], axis=0)

    blk = lax.broadcasted_iota(jnp.int32, (n_sel, n_cmp), 0)
    cmp_ix = lax.broadcasted_iota(jnp.int32, (n_sel, n_cmp), 1)
    ratio = SEL_BLOCK // CMP_STRIDE
    extra = CMP_BLOCK // CMP_STRIDE - 1
    overlap_t = jnp.where((cmp_ix >= ratio * blk - extra) & (cmp_ix <= ratio * blk + ratio - 1), 1.0, 0.0).astype(BF16)
    imp = jnp.zeros((n_sel, tq), F32)
    rest = p_sum
    for _ in range(3):
        part = rest.astype(BF16)
        imp = imp + jnp.dot(overlap_t, part, preferred_element_type=F32)
        rest = rest - part.astype(F32)

    j_col = lax.broadcasted_iota(jnp.int32, (n_sel, 1), 0)
    forced = (j_col == 0) | (j_col == lax.shift_right_logical(t_row, SEL_SHIFT))
    future = j_col * SEL_BLOCK > t_row
    score = jnp.where(forced, jnp.inf, jnp.where(future, -jnp.inf, imp))
    rank = jnp.zeros((n_sel, tq), F32)
    for jp in range(n_sel):
        other = score[jp:jp + 1, :]
        ahead = (other > score) | ((other == score) & (j_col > jp))
        rank = rank + jnp.where(ahead, 1.0, 0.0)
    neg = jnp.where(rank < float(min(SEL_TOP_K, n_sel)), 0.0, MASK_VALUE)
    pad_rows = jnp.zeros((16 - SEL_PER_KEY_BLOCK, tq), F32)
    for c in range(n_sel // SEL_PER_KEY_BLOCK):
        rows = neg[c * SEL_PER_KEY_BLOCK:(c + 1) * SEL_PER_KEY_BLOCK, :]
        mask_tile = jnp.concatenate([rows, pad_rows], axis=0).astype(BF16)
        negm_scr[c] = jnp.concatenate([mask_tile] * HEADS_PER_GROUP, axis=1)

    lanes = HEADS_PER_GROUP * tq
    rel = ((lax.broadcasted_iota(jnp.int32, (KEY_BLOCK, lanes), 1) & (tq - 1))
           - lax.broadcasted_iota(jnp.int32, (KEY_BLOCK, lanes), 0))
    zero_tail = jnp.zeros((KEY_AUG - QUERY_AUG, lanes), BF16)
    no_mask = jnp.zeros((16, lanes), BF16)

    sel_even, sel_odd, win_prev, win_diag = range(4)
    m_scr[...] = jnp.full(m_scr.shape, MASK_VALUE, F32)
    acc_scr[...] = jnp.zeros(acc_scr.shape, F32)

    def tile(k_ref, v_ref, kb, mask_tile, valid, slot):
        k0 = pl.multiple_of(kb * KEY_BLOCK, KEY_BLOCK)
        k_aug = k_ref[0, 0, 0, pl.ds(k0, KEY_BLOCK), :]
        v_aug = v_ref[0, 0, 0, kb]
        q_aug = jnp.concatenate([qa_scr[...], mask_tile, zero_tail], axis=0)
        s = jnp.dot(k_aug, q_aug, preferred_element_type=F32)
        if valid is not None:
            s = jnp.where(valid, s, MASK_VALUE)
        m_old = m_scr[slot]
        m_new = jnp.maximum(m_old, jnp.max(s, axis=0, keepdims=True))
        p = jnp.exp2(s - jnp.maximum(m_new, 0.1 * MASK_VALUE))
        alpha = jnp.exp2(m_old - m_new)
        acc_scr[slot] = alpha * acc_scr[slot] + jnp.dot(v_aug, p.astype(BF16), preferred_element_type=F32)
        m_scr[slot] = m_new

    def finish(branch, slot_a, slot_b):
        m_a = m_scr[slot_a]
        m_b = m_scr[slot_b]
        m = jnp.maximum(m_a, m_b)
        w_a = jnp.exp2(m_a - m)
        w_b = jnp.exp2(m_b - m)
        for hh in range(HEADS_PER_GROUP):
            cols = head_lanes(hh)
            acc = acc_scr[slot_a, :, cols] * w_a[:, cols] + acc_scr[slot_b, :, cols] * w_b[:, cols]
            l = acc[HEAD_DIM:HEAD_DIM + 1, :]
            o_b = acc[:HEAD_DIM, :] / jnp.where(l > 0.0, l, 1.0)
            out_scr[head_rows(hh), :] = out_scr[head_rows(hh), :] + gate(hh, branch) * o_b

    def past_pair(j, carry):
        tile(ks_ref, vs_ref, 2 * j, negm_scr[2 * j], None, sel_even)
        tile(ks_ref, vs_ref, 2 * j + 1, negm_scr[2 * j + 1], None, sel_odd)
        return carry

    lax.fori_loop(0, lax.shift_right_logical(i, 1), past_pair, 0)

    prev = jnp.maximum(i - 1, 0)
    unpaired = jnp.broadcast_to(i & 1, (KEY_BLOCK, lanes)) > 0
    causal = rel >= 0
    off_without_prev = jnp.where(i > 0, 0, WINDOW + KEY_BLOCK)
    dist_prev = rel + KEY_BLOCK + off_without_prev
    tile(ks_ref, vs_ref, prev, negm_scr[prev], unpaired, sel_even)
    tile(ks_ref, vs_ref, i, negm_scr[i], causal, sel_odd)
    tile(kw_ref, vw_ref, prev, no_mask, (dist_prev >= 0) & (dist_prev < WINDOW), win_prev)
    tile(kw_ref, vw_ref, i, no_mask, causal, win_diag)
    finish(1, sel_even, sel_odd)
    finish(2, win_prev, win_diag)

    o_ref[0] = out_scr[...].T


def _nsa(qT, kc3, vcT, k_aug, v_aug, bgT, *, tq):
    bsz, _, seq = qT.shape
    assert tq == KEY_BLOCK == WINDOW and seq % tq == 0
    n_cmp = kc3.shape[2]
    n_sel = seq // SEL_BLOCK
    n_kb = seq // KEY_BLOCK
    gw = HEADS_PER_GROUP * HEAD_DIM
    bg4 = bgT.reshape(bsz, N_KV_GROUPS, GATE_ROWS, seq)
    k_spec = lambda s: pl.BlockSpec((1, 1, 1, seq, KEY_AUG), lambda b, g, i: (b, s, g, 0, 0))
    v_spec = lambda s: pl.BlockSpec((1, 1, 1, n_kb, VAL_AUG, KEY_BLOCK), lambda b, g, i: (b, s, g, 0, 0, 0))
    return pl.pallas_call(
        functools.partial(_nsa_kernel, tq=tq, n_cmp=n_cmp, n_sel=n_sel),
        grid=(bsz, N_KV_GROUPS, seq // tq),
        in_specs=[
            pl.BlockSpec((1, gw, tq), lambda b, g, i: (b, g, i)),
            pl.BlockSpec((1, 1, n_cmp, 4 * HEAD_DIM), lambda b, g, i: (b, g, 0, 0)),
            pl.BlockSpec((1, 1, HEAD_DIM, n_cmp), lambda b, g, i: (b, g, 0, 0)),
            k_spec(0), v_spec(0),
            k_spec(1), v_spec(1),
            pl.BlockSpec((1, 1, GATE_ROWS, tq), lambda b, g, i: (b, g, 0, i)),
        ],
        out_specs=pl.BlockSpec((1, tq, gw), lambda b, g, i: (b, i, g)),
        out_shape=jax.ShapeDtypeStruct((bsz, seq, ATTN_WIDTH), F32),
        scratch_shapes=[
            pltpu.VMEM((n_kb, 16, HEADS_PER_GROUP * tq), BF16),
            pltpu.VMEM((HEAD_DIM + 16, HEADS_PER_GROUP * tq), BF16),
            pltpu.VMEM((4, 1, HEADS_PER_GROUP * tq), F32),
            pltpu.VMEM((4, VAL_AUG, HEADS_PER_GROUP * tq), F32),
            pltpu.VMEM((gw, tq), F32),
        ],
        compiler_params=_params(3),
        name="nsa",
    )(qT, kc3, vcT, k_aug, v_aug, k_aug, v_aug, bg4)


def _merge_kernel(x_ref, g_ref, o_ref_in, sc_ref, sh_ref, gt_ref, wm_ref, wl_ref, wg_ref, wn_ref, wo_ref,
                  lg_ref, lb_ref, y_ref):
    x = x_ref[0]
    u = (x * (1.0 + sc_ref[0]) + sh_ref[0]).astype(BF16)
    mg = jnp.dot(u, wm_ref[...], preferred_element_type=F32)
    g = g_ref[0].astype(BF16)
    z_ssm = jnp.dot(g, wl_ref[...], preferred_element_type=F32) * _sigmoid(
        jnp.dot(g, wg_ref[...], preferred_element_type=F32))
    z_nsa = jnp.dot(o_ref_in[0].astype(BF16), wn_ref[...], preferred_element_type=F32)
    merged = _sigmoid(mg[:, :D_MODEL]) * z_ssm + _sigmoid(mg[:, D_MODEL:]) * z_nsa
    mix = jnp.dot(merged.astype(BF16), wo_ref[...], preferred_element_type=F32)
    y = DEEPNORM_ALPHA * x + (1.0 + gt_ref[0]) * mix
    y_ref[0] = _layer_norm(y, lg_ref[...], lb_ref[...])


def _merge(x, g_ssm, o_nsa, scale1, shift1, gate1, w_m, w_lin, w_gate, w_nsa, w_out, ln_g, ln_b, *, tm):
    bsz, seq, _ = x.shape
    row = pl.BlockSpec((1, 1, D_MODEL), lambda b, i: (b, 0, 0))
    full = lambda a: pl.BlockSpec(a.shape, lambda b, i: (0, 0))
    return pl.pallas_call(
        _merge_kernel,
        grid=(bsz, seq // tm),
        in_specs=[
            pl.BlockSpec((1, tm, D_MODEL), lambda b, i: (b, i, 0)),
            pl.BlockSpec((1, tm, SSM_WIDTH), lambda b, i: (b, i, 0)),
            pl.BlockSpec((1, tm, ATTN_WIDTH), lambda b, i: (b, i, 0)),
            row, row, row,
            full(w_m), full(w_lin), full(w_gate), full(w_nsa), full(w_out), full(ln_g), full(ln_b),
        ],
        out_specs=pl.BlockSpec((1, tm, D_MODEL), lambda b, i: (b, i, 0)),
        out_shape=jax.ShapeDtypeStruct(x.shape, F32),
        compiler_params=_params(2),
        name="merge",
    )(x, g_ssm, o_nsa, scale1, shift1, gate1, w_m, w_lin, w_gate, w_nsa, w_out, ln_g, ln_b)


def _ffn_kernel(x_ref, sc_ref, sh_ref, gt_ref, wg_ref, wu_ref, wd_ref, lg_ref, lb_ref, y_ref, *, th):
    x = x_ref[0]
    u = (x * (1.0 + sc_ref[0]) + sh_ref[0]).astype(BF16)
    ffn = jnp.zeros(x.shape, F32)
    for c in range(FFN_HIDDEN // th):
        cols = slice(c * th, (c + 1) * th)
        a = jnp.dot(u, wg_ref[:, cols], preferred_element_type=F32)
        h = (a * _sigmoid(a)) * jnp.dot(u, wu_ref[:, cols], preferred_element_type=F32)
        ffn = ffn + jnp.dot(h.astype(BF16), wd_ref[cols, :], preferred_element_type=F32)
    y = DEEPNORM_ALPHA * x + (1.0 + gt_ref[0]) * ffn
    y_ref[0] = _layer_norm(y, lg_ref[...], lb_ref[...])


def _ffn(x, scale2, shift2, gate2, w_gate, w_up, w_down, ln_g, ln_b, *, tm, th):
    bsz, seq, _ = x.shape
    row = pl.BlockSpec((1, 1, D_MODEL), lambda b, i: (b, 0, 0))
    resident = lambda a: pl.BlockSpec(a.shape, lambda b, i: (0, 0), pipeline_mode=pl.Buffered(1))
    return pl.pallas_call(
        functools.partial(_ffn_kernel, th=th),
        grid=(bsz, seq // tm),
        in_specs=[
            pl.BlockSpec((1, tm, D_MODEL), lambda b, i: (b, i, 0)),
            row, row, row,
            resident(w_gate), resident(w_up), resident(w_down), resident(ln_g), resident(ln_b),
        ],
        out_specs=pl.BlockSpec((1, tm, D_MODEL), lambda b, i: (b, i, 0)),
        out_shape=jax.ShapeDtypeStruct(x.shape, F32),
        compiler_params=_params(2),
        name="ffn",
    )(x, scale2, shift2, gate2, w_gate, w_up, w_down, ln_g, ln_b)


def _layer(x, c, w_ada, b_ada, w_in, ssm_a_re, ssm_a_im, ssm_log_dt, ssm_b_re, ssm_b_im, ssm_c_re, ssm_c_im,
           ssm_d, w_glu_lin, w_glu_gate, cmp_pe_k, cmp_pe_v, w_cmp_k1, w_cmp_k2, w_cmp_v1, w_cmp_v2,
           w_nsa_proj, w_out, ln1_g, ln1_b, w_ffn_gate, w_ffn_up, w_ffn_down, ln2_g, ln2_b):
    bsz, seq, _ = x.shape
    n_chunks = seq // SSM_CHUNK
    tq = min(256, seq)
    tm = min(512, seq)

    mod = _ada_mod(c, w_ada, b_ada).reshape(6, bsz, 1, D_MODEL)
    shift1, scale1, gate1, shift2, scale2, gate2 = (mod[k] for k in range(6))

    w_nat, w_tr, w_mg = _split_w_in(w_in)
    u_ssm, kv_cmp, k_aug, qT, v_aug, bgT = _in_proj(x, scale1, shift1, w_nat, w_tr, tm=tm)

    mats = _ssm_matrices(ssm_a_re, ssm_a_im, ssm_log_dt, ssm_b_re, ssm_b_im, ssm_c_re, ssm_c_im, ssm_d)
    g_ssm = _ssm(u_ssm, mats)

    chunk_w = CMP_STRIDE * HEAD_DIM
    kv_chunks = kv_cmp.reshape(bsz, 2, N_KV_GROUPS, seq // CMP_STRIDE, chunk_w)
    kc3, vcT = _compress(kv_chunks,
                        cmp_pe_k.reshape(1, CMP_BLOCK * HEAD_DIM), cmp_pe_v.reshape(1, CMP_BLOCK * HEAD_DIM),
                        w_cmp_k1.reshape(CMP_BLOCK * HEAD_DIM, -1), w_cmp_k2,
                        w_cmp_v1.reshape(CMP_BLOCK * HEAD_DIM, -1), w_cmp_v2)
    o_nsa = _nsa(qT, kc3, vcT, k_aug, v_aug, bgT, tq=tq)

    x1 = _merge(x, g_ssm, o_nsa, scale1, shift1, gate1, w_mg,
                w_glu_lin.astype(BF16), w_glu_gate.astype(BF16), w_nsa_proj.astype(BF16), w_out.astype(BF16),
                ln1_g.reshape(1, D_MODEL), ln1_b.reshape(1, D_MODEL), tm=tm)
    return _ffn(x1, scale2, shift2, gate2, w_ffn_gate.astype(BF16), w_ffn_up.astype(BF16),
                w_ffn_down.astype(BF16), ln2_g.reshape(1, D_MODEL), ln2_b.reshape(1, D_MODEL), tm=tm, th=FFN_HIDDEN // 2)


def kernel(x, c, w_ada, b_ada, w_in, ssm_a_re, ssm_a_im, ssm_log_dt, ssm_b_re, ssm_b_im, ssm_c_re, ssm_c_im,
           ssm_d, w_glu_lin, w_glu_gate, cmp_pe_k, cmp_pe_v, w_cmp_k1, w_cmp_k2, w_cmp_v1, w_cmp_v2,
           w_nsa_proj, w_out, ln1_g, ln1_b, w_ffn_gate, w_ffn_up, w_ffn_down, ln2_g, ln2_b):
    for l in range(w_ada.shape[0]):
        x = _layer(x, c, w_ada[l], b_ada[l], w_in[l], ssm_a_re[l], ssm_a_im[l], ssm_log_dt[l],
                   ssm_b_re[l], ssm_b_im[l], ssm_c_re[l], ssm_c_im[l], ssm_d[l],
                   w_glu_lin[l], w_glu_gate[l], cmp_pe_k[l], cmp_pe_v[l],
                   w_cmp_k1[l], w_cmp_k2[l], w_cmp_v1[l], w_cmp_v2[l], w_nsa_proj[l], w_out[l],
                   ln1_g[l], ln1_b[l], w_ffn_gate[l], w_ffn_up[l], w_ffn_down[l], ln2_g[l], ln2_b[l])
    return x
```

```python
import functools
import math

import jax
import jax.numpy as jnp
from jax import lax
from jax.experimental import pallas as pl
from jax.experimental.pallas import tpu as pltpu

F32 = jnp.float32
BF16 = jnp.bfloat16
HIGHEST = lax.Precision.HIGHEST

D_MODEL = 1024
SSM_WIDTH = D_MODEL // 2
SSM_GROUP_SIZE = 16
SSM_GROUPS = SSM_WIDTH // SSM_GROUP_SIZE
SSM_STATE = 64
SSM_CHUNK = 16
SSM_GROUP_TILE = 8
N_HEADS = 8
HEAD_DIM = 64
N_KV_GROUPS = 2
HEADS_PER_GROUP = N_HEADS // N_KV_GROUPS
ATTN_WIDTH = N_HEADS * HEAD_DIM
KV_WIDTH = N_KV_GROUPS * HEAD_DIM
CMP_BLOCK = 32
CMP_STRIDE = 16
SEL_BLOCK = 64
SEL_TOP_K = 8
WINDOW = 256
N_NSA_BRANCHES = 3
GATE_ROWS = 16
FFN_HIDDEN = (8 * D_MODEL + 3 * 256 - 1) // (3 * 256) * 256
DEEPNORM_ALPHA = 2.0 ** 0.25
LN_EPS = 1e-5
MASK_VALUE = -1e30

VMEM_LIMIT = 56 * 1024 * 1024

KEY_BLOCK = 256
SEL_SHIFT = SEL_BLOCK.bit_length() - 1
SEL_PER_KEY_BLOCK = KEY_BLOCK // SEL_BLOCK
KEY_AUG = 128
ALIBI_COL = HEAD_DIM
N_PIECES = 4
MASK_COL = HEAD_DIM + 16
QUERY_AUG = HEAD_DIM + 32
VAL_AUG = HEAD_DIM + 16
LOG2E = math.log2(math.e)
SCORES_AHEAD = 5

NAT_COLS = SSM_WIDTH + 2 * KV_WIDTH + 2 * N_KV_GROUPS * KEY_AUG
TR_ROWS = ATTN_WIDTH + 2 * N_KV_GROUPS * VAL_AUG + N_KV_GROUPS * GATE_ROWS


def _bf16_pieces(value, n):
    pieces = []
    rest = value
    for _ in range(n):
        mant, expo = math.frexp(rest)
        piece = math.ldexp(round(mant * 256.0) / 256.0, expo)
        pieces.append(piece)
        rest -= piece
    return pieces


def _sigmoid(x):
    return 1.0 / (1.0 + jnp.exp(-x))


def _gelu(x):
    c = math.sqrt(2.0 / math.pi)
    return 0.5 * x * (1.0 + jnp.tanh(c * (x + 0.044715 * (x * x * x))))


def _layer_norm(y, gain, bias):
    mu = jnp.mean(y, axis=-1, keepdims=True)
    d = y - mu
    var = jnp.mean(d * d, axis=-1, keepdims=True)
    return d * lax.rsqrt(var + LN_EPS) * gain + bias


def _params(n_axes, flags=None):
    return pltpu.CompilerParams(dimension_semantics=("arbitrary",) * n_axes, vmem_limit_bytes=VMEM_LIMIT, flags=flags)


def _ada_kernel(c_ref, w_ref, b_ref, o_ref):
    c = c_ref[...]
    a = c * _sigmoid(c)
    o_ref[0] = jnp.dot(a, w_ref[...], precision=HIGHEST, preferred_element_type=F32) + b_ref[...]


def _ada_mod(c, w_ada, b_ada):
    bsz = c.shape[0]
    return pl.pallas_call(
        _ada_kernel,
        grid=(6,),
        in_specs=[
            pl.BlockSpec((bsz, D_MODEL), lambda j: (0, 0)),
            pl.BlockSpec((D_MODEL, D_MODEL), lambda j: (0, j)),
            pl.BlockSpec((1, D_MODEL), lambda j: (0, j)),
        ],
        out_specs=pl.BlockSpec((1, bsz, D_MODEL), lambda j: (j, 0, 0)),
        out_shape=jax.ShapeDtypeStruct((6, bsz, D_MODEL), F32),
        compiler_params=_params(1),
        name="ada_mod",
    )(c, w_ada, b_ada.reshape(1, 6 * D_MODEL))


def _in_proj_kernel(x_ref, sc_ref, sh_ref, wn_ref, wt_ref, ussm_ref, kc_ref, kp_ref, qT_ref, vp_ref, bgT_ref):
    i = pl.program_id(1)
    tm = x_ref.shape[1]
    u = (x_ref[0] * (1.0 + sc_ref[0]) + sh_ref[0]).astype(BF16)
    nat = jnp.dot(u, wn_ref[...], preferred_element_type=F32)
    ussm_ref[0] = nat[:, :SSM_WIDTH]
    for s in range(2):
        for g in range(N_KV_GROUPS):
            lo = SSM_WIDTH + s * KV_WIDTH + g * HEAD_DIM
            kc_ref[0, s, g] = nat[:, lo:lo + HEAD_DIM]
    pos = i * tm + lax.broadcasted_iota(jnp.int32, (tm, KEY_AUG), 0)
    col = lax.broadcasted_iota(jnp.int32, (tm, KEY_AUG), 1)
    blk = lax.shift_right_logical(pos, SEL_SHIFT)
    in_a = (col >= ALIBI_COL) & (col < ALIBI_COL + N_PIECES)
    in_b = (col >= ALIBI_COL + N_PIECES) & (col < ALIBI_COL + 2 * N_PIECES)
    hot = col == MASK_COL + (blk & (SEL_PER_KEY_BLOCK - 1))
    aux = jnp.where(in_a, blk * SEL_BLOCK, jnp.where(in_b, pos & (SEL_BLOCK - 1), jnp.where(hot, 1, 0))).astype(F32)
    for s in range(2):
        for g in range(N_KV_GROUPS):
            lo = SSM_WIDTH + 2 * KV_WIDTH + (s * N_KV_GROUPS + g) * KEY_AUG
            kp_ref[0, s, g] = (nat[:, lo:lo + KEY_AUG] + aux).astype(BF16)
    tr = lax.dot_general(wt_ref[...], u, (((1,), (1,)), ((), ())), preferred_element_type=F32)
    qT_ref[0] = tr[:ATTN_WIDTH] * (HEAD_DIM ** -0.5)
    ones_row = jnp.where(lax.broadcasted_iota(jnp.int32, (VAL_AUG, 1), 0) == HEAD_DIM, 1.0, 0.0)
    for s in range(2):
        for g in range(N_KV_GROUPS):
            lo = ATTN_WIDTH + (s * N_KV_GROUPS + g) * VAL_AUG
            v_aug = (tr[lo:lo + VAL_AUG] + ones_row).astype(BF16)
            for c in range(tm // KEY_BLOCK):
                vp_ref[0, s, g, c] = v_aug[:, c * KEY_BLOCK:(c + 1) * KEY_BLOCK]
    bgT_ref[0] = _sigmoid(tr[ATTN_WIDTH + 2 * N_KV_GROUPS * VAL_AUG:])


def _in_proj(x, scale1, shift1, w_nat, w_tr, *, tm):
    bsz, seq, _ = x.shape
    n_t = seq // tm
    n_kb = seq // KEY_BLOCK
    out_shapes = (
        jax.ShapeDtypeStruct((bsz, seq, SSM_WIDTH), F32),
        jax.ShapeDtypeStruct((bsz, 2, N_KV_GROUPS, seq, HEAD_DIM), F32),
        jax.ShapeDtypeStruct((bsz, 2, N_KV_GROUPS, seq, KEY_AUG), BF16),
        jax.ShapeDtypeStruct((bsz, ATTN_WIDTH, seq), F32),
        jax.ShapeDtypeStruct((bsz, 2, N_KV_GROUPS, n_kb, VAL_AUG, KEY_BLOCK), BF16),
        jax.ShapeDtypeStruct((bsz, N_KV_GROUPS * GATE_ROWS, seq), F32),
    )
    return pl.pallas_call(
        _in_proj_kernel,
        grid=(bsz, n_t),
        in_specs=[
            pl.BlockSpec((1, tm, D_MODEL), lambda b, i: (b, i, 0)),
            pl.BlockSpec((1, 1, D_MODEL), lambda b, i: (b, 0, 0)),
            pl.BlockSpec((1, 1, D_MODEL), lambda b, i: (b, 0, 0)),
            pl.BlockSpec((D_MODEL, NAT_COLS), lambda b, i: (0, 0)),
            pl.BlockSpec((TR_ROWS, D_MODEL), lambda b, i: (0, 0)),
        ],
        out_specs=(
            pl.BlockSpec((1, tm, SSM_WIDTH), lambda b, i: (b, i, 0)),
            pl.BlockSpec((1, 2, N_KV_GROUPS, tm, HEAD_DIM), lambda b, i: (b, 0, 0, i, 0)),
            pl.BlockSpec((1, 2, N_KV_GROUPS, tm, KEY_AUG), lambda b, i: (b, 0, 0, i, 0)),
            pl.BlockSpec((1, ATTN_WIDTH, tm), lambda b, i: (b, 0, i)),
            pl.BlockSpec((1, 2, N_KV_GROUPS, tm // KEY_BLOCK, VAL_AUG, KEY_BLOCK), lambda b, i: (b, 0, 0, i, 0, 0)),
            pl.BlockSpec((1, N_KV_GROUPS * GATE_ROWS, tm), lambda b, i: (b, 0, i)),
        ),
        out_shape=out_shapes,
        compiler_params=_params(2),
        name="in_proj",
    )(x, scale1, shift1, w_nat, w_tr)


def _split_w_in(w_in):
    o_q = SSM_WIDTH
    o_kv = o_q + ATTN_WIDTH
    o_bg = o_kv + 6 * KV_WIDTH
    o_mg = o_bg + N_NSA_BRANCHES * N_HEADS
    kv = [w_in[:, o_kv + s * KV_WIDTH:o_kv + (s + 1) * KV_WIDTH].reshape(D_MODEL, N_KV_GROUPS, HEAD_DIM)
          for s in range(6)]
    pad_k = lambda w: jnp.pad(w, ((0, 0), (0, 0), (0, KEY_AUG - HEAD_DIM))).reshape(D_MODEL, N_KV_GROUPS * KEY_AUG)
    pad_v = lambda w: jnp.pad(w, ((0, 0), (0, 0), (0, VAL_AUG - HEAD_DIM))).reshape(D_MODEL, N_KV_GROUPS * VAL_AUG)
    flat = lambda w: w.reshape(D_MODEL, KV_WIDTH)
    w_nat = jnp.concatenate([w_in[:, :o_q], flat(kv[0]), flat(kv[1]), pad_k(kv[2]), pad_k(kv[4])], axis=1)
    per_group = HEADS_PER_GROUP * N_NSA_BRANCHES
    bg = w_in[:, o_bg:o_mg].reshape(D_MODEL, N_KV_GROUPS, per_group)
    bg = jnp.pad(bg, ((0, 0), (0, 0), (0, GATE_ROWS - per_group))).reshape(D_MODEL, N_KV_GROUPS * GATE_ROWS)
    w_tr = jnp.concatenate([w_in[:, o_q:o_kv], pad_v(kv[3]), pad_v(kv[5]), bg], axis=1).T
    return w_nat.astype(BF16), w_tr.astype(BF16), w_in[:, o_mg:].astype(BF16)


def _ssm_prep_kernel(a_col_ref, a_row_ref, logdt_ref, b_ref, c_ref, d_ref,
                     mT_ref, pT_ref, qT_ref, a16_ref):
    t_n = SSM_CHUNK
    c_n = SSM_GROUP_SIZE
    rows = t_n * c_n
    dt = jnp.exp(logdt_ref[0])

    def discretise(a_re, a_im):
        lam_re = a_re * dt
        lam_im = a_im * dt
        decay = jnp.exp(lam_re)
        n_re = decay * jnp.cos(lam_im) - 1.0
        n_im = decay * jnp.sin(lam_im)
        denom = a_re * a_re + a_im * a_im
        return lam_re, lam_im, (n_re * a_re + n_im * a_im) / denom, (n_im * a_re - n_re * a_im) / denom

    def power(lam_re, lam_im, k):
        mag = jnp.exp(lam_re * k)
        return mag * jnp.cos(lam_im * k), mag * jnp.sin(lam_im * k)

    lam_re, lam_im, f_re, f_im = discretise(a_col_ref[0, 0], a_col_ref[0, 1])
    bb_re = f_re * b_ref[0, 0] - f_im * b_ref[0, 1]
    bb_im = f_re * b_ref[0, 1] + f_im * b_ref[0, 0]
    lane = lax.broadcasted_iota(jnp.int32, (1, rows), 1)
    s_lane = lax.shift_right_logical(lane, c_n.bit_length() - 1)
    ci_lane = lane & (c_n - 1)
    expand = jnp.where(lax.broadcasted_iota(jnp.int32, (c_n, rows), 0) == ci_lane, 1.0, 0.0)
    bt_re = jnp.dot(bb_re, expand, precision=HIGHEST, preferred_element_type=F32)
    bt_im = jnp.dot(bb_im, expand, precision=HIGHEST, preferred_element_type=F32)
    pw_re, pw_im = power(lam_re, lam_im, (t_n - 1 - s_lane).astype(F32))
    pT_ref[0, 0] = (pw_re * bt_re - pw_im * bt_im).astype(BF16)
    pT_ref[0, 1] = (pw_re * bt_im + pw_im * bt_re).astype(BF16)

    lam_re_r, lam_im_r, _, _ = discretise(a_row_ref[0, 0], a_row_ref[0, 1])
    row = lax.broadcasted_iota(jnp.int32, (rows, 1), 0)
    t_row = lax.shift_right_logical(row, c_n.bit_length() - 1)
    co_row = row & (c_n - 1)
    ct_re = jnp.concatenate([c_ref[0, 0]] * t_n, axis=0)
    ct_im = jnp.concatenate([c_ref[0, 1]] * t_n, axis=0)

    def c_times_power(k):
        p_re, p_im = power(lam_re_r, lam_im_r, k)
        return ct_re * p_re - ct_im * p_im, ct_re * p_im + ct_im * p_re

    q_re, q_im = c_times_power((t_row + 1).astype(F32))
    qT_ref[0, 0] = q_re.astype(BF16)
    qT_ref[0, 1] = (-q_im).astype(BF16)
    a16_ref[0] = jnp.concatenate(power(lam_re_r, lam_im_r, float(t_n)), axis=0)

    k_re, k_im = c_times_power(t_row.astype(F32))
    strips = (jnp.dot(k_re, bt_re, precision=HIGHEST, preferred_element_type=F32)
              - jnp.dot(k_im, bt_im, precision=HIGHEST, preferred_element_type=F32))
    blocks = []
    for t in range(t_n):
        acc = jnp.zeros((c_n, rows), F32)
        for lag in range(t + 1):
            acc = jnp.where(s_lane == t - lag, strips[lag * c_n:(lag + 1) * c_n, :], acc)
        blocks.append(acc)
    d_rows = jnp.concatenate([d_ref[0]] * t_n, axis=0)
    skip = jnp.where((t_row == s_lane) & (co_row == ci_lane), d_rows, 0.0)
    mT_ref[0] = (jnp.concatenate(blocks, axis=0) + skip).astype(BF16)


def _ssm_matrices(a_re, a_im, log_dt, b_re, b_im, c_re, c_im, d_skip):
    n_g, n_p = a_re.shape
    c_n = SSM_GROUP_SIZE
    rows = SSM_CHUNK * c_n
    a = jnp.stack([a_re, a_im], axis=1).astype(F32)
    spec = lambda shape: pl.BlockSpec((1,) + shape, lambda g: (g,) + (0,) * len(shape))
    mT, pT, qT, a16 = pl.pallas_call(
        _ssm_prep_kernel,
        grid=(n_g,),
        in_specs=[spec((2, n_p, 1)), spec((2, 1, n_p)), spec((1, 1)), spec((2, n_p, c_n)), spec((2, c_n, n_p)),
                  spec((c_n, 1))],
        out_specs=(spec((rows, rows)), spec((2, n_p, rows)), spec((2, rows, n_p)), spec((2, n_p))),
        out_shape=(
            jax.ShapeDtypeStruct((n_g, rows, rows), BF16),
            jax.ShapeDtypeStruct((n_g, 2, n_p, rows), BF16),
            jax.ShapeDtypeStruct((n_g, 2, rows, n_p), BF16),
            jax.ShapeDtypeStruct((n_g, 2, n_p), F32),
        ),
        compiler_params=_params(1),
        name="ssm_prep",
    )(a.reshape(n_g, 2, n_p, 1), a.reshape(n_g, 2, 1, n_p), log_dt.astype(F32).reshape(n_g, 1, 1),
      jnp.stack([b_re, b_im], axis=1).astype(F32), jnp.stack([c_re, c_im], axis=1).astype(F32),
      d_skip.astype(F32).reshape(n_g, c_n, 1))
    return mT, pT, qT, a16


def _ssm_kernel(u_ref, mT_ref, pT_ref, qT_ref, a16_ref, o_ref, ut_scr, yt_scr, sre, sim, *, n_chunks):
    n_g = SSM_GROUP_TILE
    t_n = SSM_CHUNK
    c_n = SSM_GROUP_SIZE

    for s in range(t_n):
        x_s = u_ref[0, pl.ds(s, n_chunks, stride=t_n), :]
        ut_scr[:, s * c_n:(s + 1) * c_n, :] = x_s.T.reshape(n_g, c_n, n_chunks).astype(BF16)

    def group_rows(g):
        return pl.ds(g * n_chunks, n_chunks)

    def chunk_states(g, carry):
        ut = ut_scr[g]
        sre[group_rows(g), :] = jnp.dot(pT_ref[g, 0], ut, preferred_element_type=F32).T
        sim[group_rows(g), :] = jnp.dot(pT_ref[g, 1], ut, preferred_element_type=F32).T
        return carry

    for g in range(n_g):
        chunk_states(g, 0)

    a_r = a16_ref[pl.ds(0, n_g, stride=2), :]
    a_i = a16_ref[pl.ds(1, n_g, stride=2), :]

    def carry_states(c, h):
        h_r, h_i = h
        rows = pl.ds(c, n_g, stride=n_chunks)
        s_r = sre[rows, :]
        s_i = sim[rows, :]
        sre[rows, :] = h_r
        sim[rows, :] = h_i
        return a_r * h_r - a_i * h_i + s_r, a_r * h_i + a_i * h_r + s_i

    zero = jnp.zeros((n_g, SSM_STATE), F32)
    lax.fori_loop(0, n_chunks, carry_states, (zero, zero), unroll=4)

    nt = (((1,), (1,)), ((), ()))

    def outputs(g, carry):
        yt = jnp.dot(mT_ref[g], ut_scr[g], preferred_element_type=F32)
        yt = yt + lax.dot_general(qT_ref[g, 0], sre[group_rows(g), :].astype(BF16), nt, preferred_element_type=F32)
        yt = yt + lax.dot_general(qT_ref[g, 1], sim[group_rows(g), :].astype(BF16), nt, preferred_element_type=F32)
        yt_scr[g] = _gelu(yt)
        return carry

    for g in range(n_g):
        outputs(g, 0)

    for t in range(t_n):
        z = yt_scr[:, t * c_n:(t + 1) * c_n, :].reshape(n_g * c_n, n_chunks)
        o_ref[0, pl.ds(t, n_chunks, stride=t_n), :] = z.T


def _ssm(u_ssm, mats):
    bsz, seq, width = u_ssm.shape
    n_chunks = seq // SSM_CHUNK
    mT, pT, qT, a16 = mats
    gt = SSM_GROUP_TILE
    n_tiles = SSM_GROUPS // gt
    rows = SSM_CHUNK * SSM_GROUP_SIZE
    a16 = a16.reshape(2 * SSM_GROUPS, SSM_STATE)
    per_tile = lambda a: pl.BlockSpec((gt,) + a.shape[1:], lambda t, b: (t,) + (0,) * (a.ndim - 1))
    return pl.pallas_call(
        functools.partial(_ssm_kernel, n_chunks=n_chunks),
        grid=(n_tiles, bsz),
        in_specs=[
            pl.BlockSpec((1, seq, gt * SSM_GROUP_SIZE), lambda t, b: (b, 0, t)),
            per_tile(mT), per_tile(pT), per_tile(qT),
            pl.BlockSpec((2 * gt, SSM_STATE), lambda t, b: (t, 0)),
        ],
        out_specs=pl.BlockSpec((1, seq, gt * SSM_GROUP_SIZE), lambda t, b: (b, 0, t)),
        out_shape=jax.ShapeDtypeStruct(u_ssm.shape, F32),
        scratch_shapes=[
            pltpu.VMEM((gt, rows, n_chunks), BF16),
            pltpu.VMEM((gt, rows, n_chunks), F32),
            pltpu.VMEM((gt * n_chunks, SSM_STATE), F32),
            pltpu.VMEM((gt * n_chunks, SSM_STATE), F32),
        ],
        compiler_params=_params(2),
        name="ssm",
    )(u_ssm, mT, pT, qT, a16)


def _compress_kernel(kx_ref, vx_ref, pek_ref, pev_ref, wk1_ref, wk2_ref, wv1_ref, wv2_ref, kc_ref, vcT_ref):
    half = CMP_STRIDE * HEAD_DIM

    def mlp(x, pe, w1_ref, w2_ref):
        n = x.shape[0]
        first = jnp.dot(x, w1_ref[:half, :], precision=HIGHEST, preferred_element_type=F32)
        second = jnp.dot(x, w1_ref[half:, :], precision=HIGHEST, preferred_element_type=F32)
        bias = jnp.dot(pe, w1_ref[...], precision=HIGHEST, preferred_element_type=F32)
        h = _gelu(first + pltpu.roll(second, n - 1, 0) + bias)
        return jnp.dot(h, w2_ref[...], precision=HIGHEST, preferred_element_type=F32)

    kc = mlp(kx_ref[0, 0, 0], pek_ref[...], wk1_ref, wk2_ref)
    kc_hi = kc.astype(BF16)
    kc_lo = (kc - kc_hi.astype(F32)).astype(BF16)
    kc_ref[0, 0] = jnp.concatenate([kc_hi, kc_hi, kc_lo, jnp.zeros_like(kc_hi)], axis=1)
    vcT_ref[0, 0] = mlp(vx_ref[0, 0, 0], pev_ref[...], wv1_ref, wv2_ref).T.astype(BF16)


def _compress(kv_chunks, pe_k, pe_v, wk1, wk2, wv1, wv2):
    bsz, _, n_g, n_chunks, width = kv_chunks.shape
    blk = lambda s: pl.BlockSpec((1, 1, 1, n_chunks, width), lambda b, g: (b, s, g, 0, 0))
    full = lambda a: pl.BlockSpec(a.shape, lambda b, g: (0, 0))
    return pl.pallas_call(
        _compress_kernel,
        grid=(bsz, n_g),
        in_specs=[blk(0), blk(1), full(pe_k), full(pe_v), full(wk1), full(wk2), full(wv1), full(wv2)],
        out_specs=(
            pl.BlockSpec((1, 1, n_chunks, 4 * HEAD_DIM), lambda b, g: (b, g, 0, 0)),
            pl.BlockSpec((1, 1, HEAD_DIM, n_chunks), lambda b, g: (b, g, 0, 0)),
        ),
        out_shape=(
            jax.ShapeDtypeStruct((bsz, n_g, n_chunks, 4 * HEAD_DIM), BF16),
            jax.ShapeDtypeStruct((bsz, n_g, HEAD_DIM, n_chunks), BF16),
        ),
        compiler_params=_params(2),
        name="compress",
    )(kv_chunks, kv_chunks, pe_k, pe_v, wk1, wk2, wv1, wv2)


def _nsa_kernel(qT_ref, kc_ref, vcT_ref, ks_ref, vs_ref, kw_ref, vw_ref, bg_ref, o_ref,
                negm_scr, qa_scr, m_scr, acc_scr, out_scr, *, tq, n_cmp, n_sel):
    g = pl.program_id(1)
    i = pl.program_id(2)
    t_row = i * tq + lax.broadcasted_iota(jnp.int32, (1, tq), 1)
    slopes = [jnp.where(g == 0, 2.0 ** -(hh + 1), 2.0 ** -(HEADS_PER_GROUP + hh + 1)).astype(F32)
              for hh in range(HEADS_PER_GROUP)]

    def gate(hh, branch):
        r = hh * N_NSA_BRANCHES + branch
        return bg_ref[0, 0, r:r + 1, :]

    def head_rows(hh):
        return slice(hh * HEAD_DIM, (hh + 1) * HEAD_DIM)

    def head_lanes(hh):
        return slice(hh * tq, (hh + 1) * tq)

    kc3 = kc_ref[0, 0]
    vcT = vcT_ref[0, 0]
    cmp_end = lax.broadcasted_iota(jnp.int32, (n_cmp, 1), 0) * CMP_STRIDE + (CMP_BLOCK - 1)
    dist_c = (t_row - cmp_end).astype(F32)
    valid_c = dist_c >= 0.0
    p_sum = jnp.zeros((n_cmp, tq), F32)
    piece = lax.broadcasted_iota(jnp.int32, (16, 1), 0)
    log2e_pieces = _bf16_pieces(LOG2E, N_PIECES)
    log2e_col = jnp.zeros((16, 1), F32)
    for k, value in enumerate(log2e_pieces):
        log2e_col = jnp.where((piece == k) | (piece == N_PIECES + k), value, log2e_col)
    for hh in range(HEADS_PER_GROUP):
        q_h = qT_ref[0, head_rows(hh), :]
        q_hi = q_h.astype(BF16)
        q_lo = (q_h - q_hi.astype(F32)).astype(BF16)
        q3 = jnp.concatenate([q_hi, q_lo, q_hi, jnp.zeros_like(q_hi)], axis=0)
        s = jnp.dot(kc3, q3, preferred_element_type=F32) - slopes[hh] * dist_c
        s = jnp.where(valid_c, s, MASK_VALUE)
        m = jnp.max(s, axis=0, keepdims=True)
        p = jnp.where(valid_c, jnp.exp(s - m), 0.0)
        l = jnp.sum(p, axis=0, keepdims=True)
        p = p / jnp.where(l > 0.0, l, 1.0)
        o_c = jnp.dot(vcT, p.astype(BF16), preferred_element_type=F32)
        out_scr[head_rows(hh), :] = gate(hh, 0) * o_c
        p_sum = p_sum + p
        alibi = jnp.broadcast_to(log2e_col * slopes[hh], (16, tq)).astype(BF16)
        qa_scr[:, head_lanes(hh)] = jnp.concatenate([(q_h * LOG2E).astype(BF16), alibi], axis=0)

    blk = lax.broadcasted_iota(jnp.int32, (n_sel, n_cmp), 0)
    cmp_ix = lax.broadcasted_iota(jnp.int32, (n_sel, n_cmp), 1)
    ratio = SEL_BLOCK // CMP_STRIDE
    extra = CMP_BLOCK // CMP_STRIDE - 1
    overlap_t = jnp.where((cmp_ix >= ratio * blk - extra) & (cmp_ix <= ratio * blk + ratio - 1), 1.0, 0.0).astype(BF16)
    imp = jnp.zeros((n_sel, tq), F32)
    rest = p_sum
    for _ in range(3):
        part = rest.astype(BF16)
        imp = imp + jnp.dot(overlap_t, part, preferred_element_type=F32)
        rest = rest - part.astype(F32)

    j_col = lax.broadcasted_iota(jnp.int32, (n_sel, 1), 0)
    forced = (j_col == 0) | (j_col == lax.shift_right_logical(t_row, SEL_SHIFT))
    future = j_col * SEL_BLOCK > t_row
    score = jnp.where(forced, jnp.inf, jnp.where(future, -jnp.inf, imp))
    rank = jnp.zeros((n_sel, tq), F32)
    for jp in range(n_sel):
        other = score[jp:jp + 1, :]
        ahead = (other > score) | ((other == score) & (j_col > jp))
        rank = rank + jnp.where(ahead, 1.0, 0.0)
    neg = jnp.where(rank < float(min(SEL_TOP_K, n_sel)), 0.0, MASK_VALUE)
    pad_rows = jnp.zeros((16 - SEL_PER_KEY_BLOCK, tq), F32)
    for c in range(n_sel // SEL_PER_KEY_BLOCK):
        rows = neg[c * SEL_PER_KEY_BLOCK:(c + 1) * SEL_PER_KEY_BLOCK, :]
        negm_scr[c] = jnp.concatenate([rows, pad_rows], axis=0).astype(BF16)

    rel = (lax.broadcasted_iota(jnp.int32, (KEY_BLOCK, tq), 1)
           - lax.broadcasted_iota(jnp.int32, (KEY_BLOCK, tq), 0))
    zero_tail = jnp.zeros((KEY_AUG - QUERY_AUG, tq), BF16)
    no_mask = jnp.zeros((16, tq), BF16)

    sel_even, sel_odd, win_prev, win_diag = range(4)
    m_scr[...] = jnp.full(m_scr.shape, MASK_VALUE, F32)
    acc_scr[...] = jnp.zeros(acc_scr.shape, F32)

    def scores(job, hh):
        k_ref, _, kb, mask_tile, _, _ = job
        k0 = pl.multiple_of(kb * KEY_BLOCK, KEY_BLOCK)
        k_aug = k_ref[0, 0, 0, pl.ds(k0, KEY_BLOCK), :]
        q_aug = jnp.concatenate([qa_scr[:, head_lanes(hh)], mask_tile, zero_tail], axis=0)
        return jnp.dot(k_aug, q_aug, preferred_element_type=F32)

    def accumulate(job, hh, s):
        _, v_ref, kb, _, valid, slot = job
        cols = head_lanes(hh)
        if valid is not None:
            s = jnp.where(valid, s, MASK_VALUE)
        m_old = m_scr[slot, :, cols]
        m_new = jnp.maximum(m_old, jnp.max(s, axis=0, keepdims=True))
        p = jnp.exp2(s - jnp.maximum(m_new, 0.1 * MASK_VALUE))
        alpha = jnp.exp2(m_old - m_new)
        pv = jnp.dot(v_ref[0, 0, 0, kb], p.astype(BF16), preferred_element_type=F32)
        acc_scr[slot, :, cols] = alpha * acc_scr[slot, :, cols] + pv
        m_scr[slot, :, cols] = m_new

    def run(jobs):
        units = [(job, hh) for job in jobs for hh in range(HEADS_PER_GROUP)]
        pending = [scores(*unit) for unit in units[:SCORES_AHEAD]]
        for n, (job, hh) in enumerate(units):
            if n + SCORES_AHEAD < len(units):
                pending.append(scores(*units[n + SCORES_AHEAD]))
            accumulate(job, hh, pending.pop(0))

    def finish(branch, slot_a, slot_b):
        m_a = m_scr[slot_a]
        m_b = m_scr[slot_b]
        m = jnp.maximum(m_a, m_b)
        w_a = jnp.exp2(m_a - m)
        w_b = jnp.exp2(m_b - m)
        for hh in range(HEADS_PER_GROUP):
            cols = head_lanes(hh)
            acc = acc_scr[slot_a, :, cols] * w_a[:, cols] + acc_scr[slot_b, :, cols] * w_b[:, cols]
            l = acc[HEAD_DIM:HEAD_DIM + 1, :]
            o_b = acc[:HEAD_DIM, :] / jnp.where(l > 0.0, l, 1.0)
            out_scr[head_rows(hh), :] = out_scr[head_rows(hh), :] + gate(hh, branch) * o_b

    def past_pair(j, carry):
        run([(ks_ref, vs_ref, 2 * j, negm_scr[2 * j], None, sel_even),
             (ks_ref, vs_ref, 2 * j + 1, negm_scr[2 * j + 1], None, sel_odd)])
        return carry

    lax.fori_loop(0, lax.shift_right_logical(i, 1), past_pair, 0)

    prev = jnp.maximum(i - 1, 0)
    unpaired = jnp.broadcast_to(i & 1, (KEY_BLOCK, tq)) > 0
    causal = rel >= 0
    off_without_prev = jnp.where(i > 0, 0, WINDOW + KEY_BLOCK)
    dist_prev = rel + KEY_BLOCK + off_without_prev
    run([(ks_ref, vs_ref, prev, negm_scr[prev], unpaired, sel_even),
         (ks_ref, vs_ref, i, negm_scr[i], causal, sel_odd),
         (kw_ref, vw_ref, prev, no_mask, (dist_prev >= 0) & (dist_prev < WINDOW), win_prev),
         (kw_ref, vw_ref, i, no_mask, causal, win_diag)])
    finish(1, sel_even, sel_odd)
    finish(2, win_prev, win_diag)

    o_ref[0] = out_scr[...].T


def _nsa(qT, kc3, vcT, k_aug, v_aug, bgT, *, tq):
    bsz, _, seq = qT.shape
    assert tq == KEY_BLOCK == WINDOW and seq % tq == 0
    n_cmp = kc3.shape[2]
    n_sel = seq // SEL_BLOCK
    n_kb = seq // KEY_BLOCK
    gw = HEADS_PER_GROUP * HEAD_DIM
    bg4 = bgT.reshape(bsz, N_KV_GROUPS, GATE_ROWS, seq)
    k_spec = lambda s: pl.BlockSpec((1, 1, 1, seq, KEY_AUG), lambda b, g, i: (b, s, g, 0, 0))
    v_spec = lambda s: pl.BlockSpec((1, 1, 1, n_kb, VAL_AUG, KEY_BLOCK), lambda b, g, i: (b, s, g, 0, 0, 0))
    return pl.pallas_call(
        functools.partial(_nsa_kernel, tq=tq, n_cmp=n_cmp, n_sel=n_sel),
        grid=(bsz, N_KV_GROUPS, seq // tq),
        in_specs=[
            pl.BlockSpec((1, gw, tq), lambda b, g, i: (b, g, i)),
            pl.BlockSpec((1, 1, n_cmp, 4 * HEAD_DIM), lambda b, g, i: (b, g, 0, 0)),
            pl.BlockSpec((1, 1, HEAD_DIM, n_cmp), lambda b, g, i: (b, g, 0, 0)),
            k_spec(0), v_spec(0),
            k_spec(1), v_spec(1),
            pl.BlockSpec((1, 1, GATE_ROWS, tq), lambda b, g, i: (b, g, 0, i)),
        ],
        out_specs=pl.BlockSpec((1, tq, gw), lambda b, g, i: (b, i, g)),
        out_shape=jax.ShapeDtypeStruct((bsz, seq, ATTN_WIDTH), F32),
        scratch_shapes=[
            pltpu.VMEM((n_kb, 16, tq), BF16),
            pltpu.VMEM((HEAD_DIM + 16, HEADS_PER_GROUP * tq), BF16),
            pltpu.VMEM((4, 1, HEADS_PER_GROUP * tq), F32),
            pltpu.VMEM((4, VAL_AUG, HEADS_PER_GROUP * tq), F32),
            pltpu.VMEM((gw, tq), F32),
        ],
        compiler_params=_params(3),
        name="nsa",
    )(qT, kc3, vcT, k_aug, v_aug, k_aug, v_aug, bg4)


def _merge_kernel(x_ref, g_ref, o_ref_in, sc_ref, sh_ref, gt_ref, wm_ref, wl_ref, wg_ref, wn_ref, wo_ref,
                  lg_ref, lb_ref, y_ref):
    x = x_ref[0]
    u = (x * (1.0 + sc_ref[0]) + sh_ref[0]).astype(BF16)
    mg = jnp.dot(u, wm_ref[...], preferred_element_type=F32)
    g = g_ref[0].astype(BF16)
    z_ssm = jnp.dot(g, wl_ref[...], preferred_element_type=F32) * _sigmoid(
        jnp.dot(g, wg_ref[...], preferred_element_type=F32))
    z_nsa = jnp.dot(o_ref_in[0].astype(BF16), wn_ref[...], preferred_element_type=F32)
    merged = _sigmoid(mg[:, :D_MODEL]) * z_ssm + _sigmoid(mg[:, D_MODEL:]) * z_nsa
    mix = jnp.dot(merged.astype(BF16), wo_ref[...], preferred_element_type=F32)
    y = DEEPNORM_ALPHA * x + (1.0 + gt_ref[0]) * mix
    y_ref[0] = _layer_norm(y, lg_ref[...], lb_ref[...])


def _merge(x, g_ssm, o_nsa, scale1, shift1, gate1, w_m, w_lin, w_gate, w_nsa, w_out, ln_g, ln_b, *, tm):
    bsz, seq, _ = x.shape
    row = pl.BlockSpec((1, 1, D_MODEL), lambda b, i: (b, 0, 0))
    full = lambda a: pl.BlockSpec(a.shape, lambda b, i: (0, 0))
    return pl.pallas_call(
        _merge_kernel,
        grid=(bsz, seq // tm),
        in_specs=[
            pl.BlockSpec((1, tm, D_MODEL), lambda b, i: (b, i, 0)),
            pl.BlockSpec((1, tm, SSM_WIDTH), lambda b, i: (b, i, 0)),
            pl.BlockSpec((1, tm, ATTN_WIDTH), lambda b, i: (b, i, 0)),
            row, row, row,
            full(w_m), full(w_lin), full(w_gate), full(w_nsa), full(w_out), full(ln_g), full(ln_b),
        ],
        out_specs=pl.BlockSpec((1, tm, D_MODEL), lambda b, i: (b, i, 0)),
        out_shape=jax.ShapeDtypeStruct(x.shape, F32),
        compiler_params=_params(2),
        name="merge",
    )(x, g_ssm, o_nsa, scale1, shift1, gate1, w_m, w_lin, w_gate, w_nsa, w_out, ln_g, ln_b)


def _ffn_kernel(x_ref, sc_ref, sh_ref, gt_ref, wg_ref, wu_ref, wd_ref, lg_ref, lb_ref, y_ref, *, th):
    x = x_ref[0]
    u = (x * (1.0 + sc_ref[0]) + sh_ref[0]).astype(BF16)
    ffn = jnp.zeros(x.shape, F32)
    for c in range(FFN_HIDDEN // th):
        cols = slice(c * th, (c + 1) * th)
        a = jnp.dot(u, wg_ref[:, cols], preferred_element_type=F32)
        h = (a * _sigmoid(a)) * jnp.dot(u, wu_ref[:, cols], preferred_element_type=F32)
        ffn = ffn + jnp.dot(h.astype(BF16), wd_ref[cols, :], preferred_element_type=F32)
    y = DEEPNORM_ALPHA * x + (1.0 + gt_ref[0]) * ffn
    y_ref[0] = _layer_norm(y, lg_ref[...], lb_ref[...])


def _ffn(x, scale2, shift2, gate2, w_gate, w_up, w_down, ln_g, ln_b, *, tm, th):
    bsz, seq, _ = x.shape
    row = pl.BlockSpec((1, 1, D_MODEL), lambda b, i: (b, 0, 0))
    resident = lambda a: pl.BlockSpec(a.shape, lambda b, i: (0, 0), pipeline_mode=pl.Buffered(1))
    return pl.pallas_call(
        functools.partial(_ffn_kernel, th=th),
        grid=(bsz, seq // tm),
        in_specs=[
            pl.BlockSpec((1, tm, D_MODEL), lambda b, i: (b, i, 0)),
            row, row, row,
            resident(w_gate), resident(w_up), resident(w_down), resident(ln_g), resident(ln_b),
        ],
        out_specs=pl.BlockSpec((1, tm, D_MODEL), lambda b, i: (b, i, 0)),
        out_shape=jax.ShapeDtypeStruct(x.shape, F32),
        compiler_params=_params(2),
        name="ffn",
    )(x, scale2, shift2, gate2, w_gate, w_up, w_down, ln_g, ln_b)


def _layer(x, c, w_ada, b_ada, w_in, ssm_a_re, ssm_a_im, ssm_log_dt, ssm_b_re, ssm_b_im, ssm_c_re, ssm_c_im,
           ssm_d, w_glu_lin, w_glu_gate, cmp_pe_k, cmp_pe_v, w_cmp_k1, w_cmp_k2, w_cmp_v1, w_cmp_v2,
           w_nsa_proj, w_out, ln1_g, ln1_b, w_ffn_gate, w_ffn_up, w_ffn_down, ln2_g, ln2_b):
    bsz, seq, _ = x.shape
    n_chunks = seq // SSM_CHUNK
    tq = min(256, seq)
    tm = min(512, seq)

    mod = _ada_mod(c, w_ada, b_ada).reshape(6, bsz, 1, D_MODEL)
    shift1, scale1, gate1, shift2, scale2, gate2 = (mod[k] for k in range(6))

    w_nat, w_tr, w_mg = _split_w_in(w_in)
    u_ssm, kv_cmp, k_aug, qT, v_aug, bgT = _in_proj(x, scale1, shift1, w_nat, w_tr, tm=tm)

    mats = _ssm_matrices(ssm_a_re, ssm_a_im, ssm_log_dt, ssm_b_re, ssm_b_im, ssm_c_re, ssm_c_im, ssm_d)
    g_ssm = _ssm(u_ssm, mats)

    chunk_w = CMP_STRIDE * HEAD_DIM
    kv_chunks = kv_cmp.reshape(bsz, 2, N_KV_GROUPS, seq // CMP_STRIDE, chunk_w)
    kc3, vcT = _compress(kv_chunks,
                        cmp_pe_k.reshape(1, CMP_BLOCK * HEAD_DIM), cmp_pe_v.reshape(1, CMP_BLOCK * HEAD_DIM),
                        w_cmp_k1.reshape(CMP_BLOCK * HEAD_DIM, -1), w_cmp_k2,
                        w_cmp_v1.reshape(CMP_BLOCK * HEAD_DIM, -1), w_cmp_v2)
    o_nsa = _nsa(qT, kc3, vcT, k_aug, v_aug, bgT, tq=tq)

    x1 = _merge(x, g_ssm, o_nsa, scale1, shift1, gate1, w_mg,
                w_glu_lin.astype(BF16), w_glu_gate.astype(BF16), w_nsa_proj.astype(BF16), w_out.astype(BF16),
                ln1_g.reshape(1, D_MODEL), ln1_b.reshape(1, D_MODEL), tm=tm)
    return _ffn(x1, scale2, shift2, gate2, w_ffn_gate.astype(BF16), w_ffn_up.astype(BF16),
                w_ffn_down.astype(BF16), ln2_g.reshape(1, D_MODEL), ln2_b.reshape(1, D_MODEL), tm=tm, th=FFN_HIDDEN // 2)


def kernel(x, c, w_ada, b_ada, w_in, ssm_a_re, ssm_a_im, ssm_log_dt, ssm_b_re, ssm_b_im, ssm_c_re, ssm_c_im,
           ssm_d, w_glu_lin, w_glu_gate, cmp_pe_k, cmp_pe_v, w_cmp_k1, w_cmp_k2, w_cmp_v1, w_cmp_v2,
           w_nsa_proj, w_out, ln1_g, ln1_b, w_ffn_gate, w_ffn_up, w_ffn_down, ln2_g, ln2_b):
    for l in range(w_ada.shape[0]):
        x = _layer(x, c, w_ada[l], b_ada[l], w_in[l], ssm_a_re[l], ssm_a_im[l], ssm_log_dt[l],
                   ssm_b_re[l], ssm_b_im[l], ssm_c_re[l], ssm_c_im[l], ssm_d[l],
                   w_glu_lin[l], w_glu_gate[l], cmp_pe_k[l], cmp_pe_v[l],
                   w_cmp_k1[l], w_cmp_k2[l], w_cmp_v1[l], w_cmp_v2[l], w_nsa_proj[l], w_out[l],
                   ln1_g[l], ln1_b[l], w_ffn_gate[l], w_ffn_up[l], w_ffn_down[l], ln2_g[l], ln2_b[l])
    return x
```

```python
import functools
import math

import jax
import jax.numpy as jnp
from jax import lax
from jax.experimental import pallas as pl
from jax.experimental.pallas import tpu as pltpu

F32 = jnp.float32
BF16 = jnp.bfloat16
HIGHEST = lax.Precision.HIGHEST

D_MODEL = 1024
SSM_WIDTH = D_MODEL // 2
SSM_GROUP_SIZE = 16
SSM_GROUPS = SSM_WIDTH // SSM_GROUP_SIZE
SSM_STATE = 64
SSM_CHUNK = 16
SSM_GROUP_TILE = 8
N_HEADS = 8
HEAD_DIM = 64
N_KV_GROUPS = 2
HEADS_PER_GROUP = N_HEADS // N_KV_GROUPS
ATTN_WIDTH = N_HEADS * HEAD_DIM
KV_WIDTH = N_KV_GROUPS * HEAD_DIM
CMP_BLOCK = 32
CMP_STRIDE = 16
SEL_BLOCK = 64
SEL_TOP_K = 8
WINDOW = 256
N_NSA_BRANCHES = 3
GATE_ROWS = 16
FFN_HIDDEN = (8 * D_MODEL + 3 * 256 - 1) // (3 * 256) * 256
DEEPNORM_ALPHA = 2.0 ** 0.25
LN_EPS = 1e-5
MASK_VALUE = -1e30

VMEM_LIMIT = 56 * 1024 * 1024

KEY_BLOCK = 256
SEL_SHIFT = SEL_BLOCK.bit_length() - 1
SEL_PER_KEY_BLOCK = KEY_BLOCK // SEL_BLOCK
KEY_AUG = 128
ALIBI_COL = HEAD_DIM
N_PIECES = 4
MASK_COL = HEAD_DIM + 16
QUERY_AUG = HEAD_DIM + 32
VAL_AUG = HEAD_DIM + 16
LOG2E = math.log2(math.e)
SCORES_AHEAD = 5

NAT_COLS = SSM_WIDTH + 2 * KV_WIDTH + 2 * N_KV_GROUPS * KEY_AUG
TR_ROWS = ATTN_WIDTH + 2 * N_KV_GROUPS * VAL_AUG + N_KV_GROUPS * GATE_ROWS


def _bf16_pieces(value, n):
    pieces = []
    rest = value
    for _ in range(n):
        mant, expo = math.frexp(rest)
        piece = math.ldexp(round(mant * 256.0) / 256.0, expo)
        pieces.append(piece)
        rest -= piece
    return pieces


def _sigmoid(x):
    return 1.0 / (1.0 + jnp.exp(-x))


def _gelu(x):
    c = math.sqrt(2.0 / math.pi)
    return 0.5 * x * (1.0 + jnp.tanh(c * (x + 0.044715 * (x * x * x))))


def _layer_norm(y, gain, bias):
    mu = jnp.mean(y, axis=-1, keepdims=True)
    d = y - mu
    var = jnp.mean(d * d, axis=-1, keepdims=True)
    return d * lax.rsqrt(var + LN_EPS) * gain + bias


def _params(n_axes, flags=None):
    return pltpu.CompilerParams(dimension_semantics=("arbitrary",) * n_axes, vmem_limit_bytes=VMEM_LIMIT, flags=flags)


def _ada_kernel(c_ref, w_ref, b_ref, o_ref):
    c = c_ref[...]
    a = c * _sigmoid(c)
    o_ref[0] = jnp.dot(a, w_ref[...], precision=HIGHEST, preferred_element_type=F32) + b_ref[...]


def _ada_mod(c, w_ada, b_ada):
    bsz = c.shape[0]
    return pl.pallas_call(
        _ada_kernel,
        grid=(6,),
        in_specs=[
            pl.BlockSpec((bsz, D_MODEL), lambda j: (0, 0)),
            pl.BlockSpec((D_MODEL, D_MODEL), lambda j: (0, j)),
            pl.BlockSpec((1, D_MODEL), lambda j: (0, j)),
        ],
        out_specs=pl.BlockSpec((1, bsz, D_MODEL), lambda j: (j, 0, 0)),
        out_shape=jax.ShapeDtypeStruct((6, bsz, D_MODEL), F32),
        compiler_params=_params(1),
        name="ada_mod",
    )(c, w_ada, b_ada.reshape(1, 6 * D_MODEL))


def _in_proj_kernel(x_ref, sc_ref, sh_ref, wn_ref, wt_ref, ussm_ref, kc_ref, kp_ref, qT_ref, vp_ref, bgT_ref):
    i = pl.program_id(1)
    tm = x_ref.shape[1]
    u = (x_ref[0] * (1.0 + sc_ref[0]) + sh_ref[0]).astype(BF16)
    nat = jnp.dot(u, wn_ref[...], preferred_element_type=F32)
    ussm_ref[0] = nat[:, :SSM_WIDTH]
    for s in range(2):
        for g in range(N_KV_GROUPS):
            lo = SSM_WIDTH + s * KV_WIDTH + g * HEAD_DIM
            kc_ref[0, s, g] = nat[:, lo:lo + HEAD_DIM]
    pos = i * tm + lax.broadcasted_iota(jnp.int32, (tm, KEY_AUG), 0)
    col = lax.broadcasted_iota(jnp.int32, (tm, KEY_AUG), 1)
    blk = lax.shift_right_logical(pos, SEL_SHIFT)
    in_a = (col >= ALIBI_COL) & (col < ALIBI_COL + N_PIECES)
    in_b = (col >= ALIBI_COL + N_PIECES) & (col < ALIBI_COL + 2 * N_PIECES)
    hot = col == MASK_COL + (blk & (SEL_PER_KEY_BLOCK - 1))
    aux = jnp.where(in_a, blk * SEL_BLOCK, jnp.where(in_b, pos & (SEL_BLOCK - 1), jnp.where(hot, 1, 0))).astype(F32)
    for s in range(2):
        for g in range(N_KV_GROUPS):
            lo = SSM_WIDTH + 2 * KV_WIDTH + (s * N_KV_GROUPS + g) * KEY_AUG
            kp_ref[0, s, g] = (nat[:, lo:lo + KEY_AUG] + aux).astype(BF16)
    tr = lax.dot_general(wt_ref[...], u, (((1,), (1,)), ((), ())), preferred_element_type=F32)
    qT_ref[0] = tr[:ATTN_WIDTH] * (HEAD_DIM ** -0.5)
    ones_row = jnp.where(lax.broadcasted_iota(jnp.int32, (VAL_AUG, 1), 0) == HEAD_DIM, 1.0, 0.0)
    for s in range(2):
        for g in range(N_KV_GROUPS):
            lo = ATTN_WIDTH + (s * N_KV_GROUPS + g) * VAL_AUG
            v_aug = (tr[lo:lo + VAL_AUG] + ones_row).astype(BF16)
            for c in range(tm // KEY_BLOCK):
                vp_ref[0, s, g, c] = v_aug[:, c * KEY_BLOCK:(c + 1) * KEY_BLOCK]
    bgT_ref[0] = _sigmoid(tr[ATTN_WIDTH + 2 * N_KV_GROUPS * VAL_AUG:])


def _in_proj(x, scale1, shift1, w_nat, w_tr, *, tm):
    bsz, seq, _ = x.shape
    n_t = seq // tm
    n_kb = seq // KEY_BLOCK
    out_shapes = (
        jax.ShapeDtypeStruct((bsz, seq, SSM_WIDTH), F32),
        jax.ShapeDtypeStruct((bsz, 2, N_KV_GROUPS, seq, HEAD_DIM), F32),
        jax.ShapeDtypeStruct((bsz, 2, N_KV_GROUPS, seq, KEY_AUG), BF16),
        jax.ShapeDtypeStruct((bsz, ATTN_WIDTH, seq), F32),
        jax.ShapeDtypeStruct((bsz, 2, N_KV_GROUPS, n_kb, VAL_AUG, KEY_BLOCK), BF16),
        jax.ShapeDtypeStruct((bsz, N_KV_GROUPS * GATE_ROWS, seq), F32),
    )
    return pl.pallas_call(
        _in_proj_kernel,
        grid=(bsz, n_t),
        in_specs=[
            pl.BlockSpec((1, tm, D_MODEL), lambda b, i: (b, i, 0)),
            pl.BlockSpec((1, 1, D_MODEL), lambda b, i: (b, 0, 0)),
            pl.BlockSpec((1, 1, D_MODEL), lambda b, i: (b, 0, 0)),
            pl.BlockSpec((D_MODEL, NAT_COLS), lambda b, i: (0, 0)),
            pl.BlockSpec((TR_ROWS, D_MODEL), lambda b, i: (0, 0)),
        ],
        out_specs=(
            pl.BlockSpec((1, tm, SSM_WIDTH), lambda b, i: (b, i, 0)),
            pl.BlockSpec((1, 2, N_KV_GROUPS, tm, HEAD_DIM), lambda b, i: (b, 0, 0, i, 0)),
            pl.BlockSpec((1, 2, N_KV_GROUPS, tm, KEY_AUG), lambda b, i: (b, 0, 0, i, 0)),
            pl.BlockSpec((1, ATTN_WIDTH, tm), lambda b, i: (b, 0, i)),
            pl.BlockSpec((1, 2, N_KV_GROUPS, tm // KEY_BLOCK, VAL_AUG, KEY_BLOCK), lambda b, i: (b, 0, 0, i, 0, 0)),
            pl.BlockSpec((1, N_KV_GROUPS * GATE_ROWS, tm), lambda b, i: (b, 0, i)),
        ),
        out_shape=out_shapes,
        compiler_params=_params(2),
        name="in_proj",
    )(x, scale1, shift1, w_nat, w_tr)


def _split_w_in(w_in):
    o_q = SSM_WIDTH
    o_kv = o_q + ATTN_WIDTH
    o_bg = o_kv + 6 * KV_WIDTH
    o_mg = o_bg + N_NSA_BRANCHES * N_HEADS
    kv = [w_in[:, o_kv + s * KV_WIDTH:o_kv + (s + 1) * KV_WIDTH].reshape(D_MODEL, N_KV_GROUPS, HEAD_DIM)
          for s in range(6)]
    pad_k = lambda w: jnp.pad(w, ((0, 0), (0, 0), (0, KEY_AUG - HEAD_DIM))).reshape(D_MODEL, N_KV_GROUPS * KEY_AUG)
    pad_v = lambda w: jnp.pad(w, ((0, 0), (0, 0), (0, VAL_AUG - HEAD_DIM))).reshape(D_MODEL, N_KV_GROUPS * VAL_AUG)
    flat = lambda w: w.reshape(D_MODEL, KV_WIDTH)
    w_nat = jnp.concatenate([w_in[:, :o_q], flat(kv[0]), flat(kv[1]), pad_k(kv[2]), pad_k(kv[4])], axis=1)
    per_group = HEADS_PER_GROUP * N_NSA_BRANCHES
    bg = w_in[:, o_bg:o_mg].reshape(D_MODEL, N_KV_GROUPS, per_group)
    bg = jnp.pad(bg, ((0, 0), (0, 0), (0, GATE_ROWS - per_group))).reshape(D_MODEL, N_KV_GROUPS * GATE_ROWS)
    w_tr = jnp.concatenate([w_in[:, o_q:o_kv], pad_v(kv[3]), pad_v(kv[5]), bg], axis=1).T
    return w_nat.astype(BF16), w_tr.astype(BF16), w_in[:, o_mg:].astype(BF16)


def _ssm_prep_kernel(a_ref, logdt_ref, bT_ref, c_ref, d_ref, mT_ref, pT_ref, qT_ref, a16_ref):
    t_n = SSM_CHUNK
    c_n = SSM_GROUP_SIZE
    n_p = SSM_STATE
    rows = t_n * c_n
    dt = jnp.exp(logdt_ref[0])
    a_re = a_ref[0, 0]
    a_im = a_ref[0, 1]
    decay = jnp.exp(a_re * dt)
    ab_re = decay * jnp.cos(a_im * dt)
    ab_im = decay * jnp.sin(a_im * dt)
    n_re = ab_re - 1.0
    denom = a_re * a_re + a_im * a_im
    f_re = (n_re * a_re + ab_im * a_im) / denom
    f_im = (ab_im * a_re - n_re * a_im) / denom
    powers = [(jnp.ones_like(ab_re), jnp.zeros_like(ab_re))]
    for _ in range(t_n):
        p_re, p_im = powers[-1]
        powers.append((p_re * ab_re - p_im * ab_im, p_re * ab_im + p_im * ab_re))

    def per_token(first, step):
        picks = [powers[first + step * k] for k in range(t_n)]
        return (jnp.concatenate([jnp.broadcast_to(p[0], (c_n, 2 * n_p)) for p in picks], axis=0),
                jnp.concatenate([jnp.broadcast_to(p[1], (c_n, 2 * n_p)) for p in picks], axis=0))

    def cmul(x_re, x_im, y_re, y_im):
        return x_re * y_re - x_im * y_im, x_re * y_im + x_im * y_re

    bb_re, bb_im = cmul(f_re, f_im, bT_ref[0, 0], bT_ref[0, 1])
    bt_re = jnp.concatenate([bb_re] * t_n, axis=0)
    bt_im = jnp.concatenate([bb_im] * t_n, axis=0)
    ct_re = jnp.concatenate([c_ref[0, 0]] * t_n, axis=0)
    ct_im = jnp.concatenate([c_ref[0, 1]] * t_n, axis=0)

    p_re, p_im = cmul(*per_token(t_n - 1, -1), bt_re, bt_im)
    low_half = lax.broadcasted_iota(jnp.int32, (1, 2 * n_p), 1) < n_p
    pT_ref[0] = jnp.where(low_half, p_re, p_im).T.astype(BF16)

    q_re, q_im = cmul(ct_re, ct_im, *per_token(1, 1))
    qT_ref[0, 0] = q_re[:, :n_p].astype(BF16)
    qT_ref[0, 1] = (-q_im[:, :n_p]).astype(BF16)
    a16_ref[0] = jnp.concatenate([powers[t_n][0][:, :n_p], powers[t_n][1][:, :n_p]], axis=0)

    k_re, k_im = cmul(ct_re, ct_im, *per_token(0, 1))
    nt = (((1,), (1,)), ((), ()))
    strips = (lax.dot_general(k_re[:, :n_p], bt_re[:, :n_p], nt, precision=HIGHEST, preferred_element_type=F32)
              - lax.dot_general(k_im[:, :n_p], bt_im[:, :n_p], nt, precision=HIGHEST, preferred_element_type=F32))
    lane = lax.broadcasted_iota(jnp.int32, (1, rows), 1)
    s_lane = lax.shift_right_logical(lane, c_n.bit_length() - 1)
    ci_lane = lane & (c_n - 1)
    row = lax.broadcasted_iota(jnp.int32, (rows, 1), 0)
    t_row = lax.shift_right_logical(row, c_n.bit_length() - 1)
    co_row = row & (c_n - 1)
    blocks = []
    for t in range(t_n):
        acc = jnp.zeros((c_n, rows), F32)
        for lag in range(t + 1):
            acc = jnp.where(s_lane == t - lag, strips[lag * c_n:(lag + 1) * c_n, :], acc)
        blocks.append(acc)
    d_rows = jnp.concatenate([d_ref[0]] * t_n, axis=0)
    skip = jnp.where((t_row == s_lane) & (co_row == ci_lane), d_rows, 0.0)
    mT_ref[0] = (jnp.concatenate(blocks, axis=0) + skip).astype(BF16)


def _ssm_matrices(a_re, a_im, log_dt, b_re, b_im, c_re, c_im, d_skip):
    n_g, n_p = a_re.shape
    c_n = SSM_GROUP_SIZE
    rows = SSM_CHUNK * c_n
    twice = lambda x: jnp.concatenate([x, x], axis=-1).astype(F32)
    a = twice(jnp.stack([a_re, a_im], axis=1)).reshape(n_g, 2, 1, 2 * n_p)
    bT = twice(jnp.swapaxes(jnp.stack([b_re, b_im], axis=1), 2, 3))
    c = twice(jnp.stack([c_re, c_im], axis=1))
    spec = lambda shape: pl.BlockSpec((1,) + shape, lambda g: (g,) + (0,) * len(shape))
    mT, pT, qT, a16 = pl.pallas_call(
        _ssm_prep_kernel,
        grid=(n_g,),
        in_specs=[spec((2, 1, 2 * n_p)), spec((1, 1)), spec((2, c_n, 2 * n_p)), spec((2, c_n, 2 * n_p)),
                  spec((c_n, 1))],
        out_specs=(spec((rows, rows)), spec((2 * n_p, rows)), spec((2, rows, n_p)), spec((2, n_p))),
        out_shape=(
            jax.ShapeDtypeStruct((n_g, rows, rows), BF16),
            jax.ShapeDtypeStruct((n_g, 2 * n_p, rows), BF16),
            jax.ShapeDtypeStruct((n_g, 2, rows, n_p), BF16),
            jax.ShapeDtypeStruct((n_g, 2, n_p), F32),
        ),
        compiler_params=_params(1),
        name="ssm_prep",
    )(a, log_dt.astype(F32).reshape(n_g, 1, 1), bT, c, d_skip.astype(F32).reshape(n_g, c_n, 1))
    return mT, pT.reshape(n_g, 2, n_p, rows), qT, a16


def _ssm_kernel(u_ref, mT_ref, pT_ref, qT_ref, a16_ref, o_ref, ut_scr, yt_scr, sre, sim, *, n_chunks):
    n_g = SSM_GROUP_TILE
    t_n = SSM_CHUNK
    c_n = SSM_GROUP_SIZE

    for s in range(t_n):
        x_s = u_ref[0, pl.ds(s, n_chunks, stride=t_n), :]
        ut_scr[:, s * c_n:(s + 1) * c_n, :] = x_s.T.reshape(n_g, c_n, n_chunks).astype(BF16)

    def group_rows(g):
        return pl.ds(g * n_chunks, n_chunks)

    def chunk_states(g, carry):
        ut = ut_scr[g]
        sre[group_rows(g), :] = jnp.dot(pT_ref[g, 0], ut, preferred_element_type=F32).T
        sim[group_rows(g), :] = jnp.dot(pT_ref[g, 1], ut, preferred_element_type=F32).T
        return carry

    for g in range(n_g):
        chunk_states(g, 0)

    a_r = a16_ref[pl.ds(0, n_g, stride=2), :]
    a_i = a16_ref[pl.ds(1, n_g, stride=2), :]

    def carry_states(c, h):
        h_r, h_i = h
        rows = pl.ds(c, n_g, stride=n_chunks)
        s_r = sre[rows, :]
        s_i = sim[rows, :]
        sre[rows, :] = h_r
        sim[rows, :] = h_i
        return a_r * h_r - a_i * h_i + s_r, a_r * h_i + a_i * h_r + s_i

    zero = jnp.zeros((n_g, SSM_STATE), F32)
    lax.fori_loop(0, n_chunks, carry_states, (zero, zero), unroll=4)

    nt = (((1,), (1,)), ((), ()))

    def outputs(g, carry):
        yt = jnp.dot(mT_ref[g], ut_scr[g], preferred_element_type=F32)
        yt = yt + lax.dot_general(qT_ref[g, 0], sre[group_rows(g), :].astype(BF16), nt, preferred_element_type=F32)
        yt = yt + lax.dot_general(qT_ref[g, 1], sim[group_rows(g), :].astype(BF16), nt, preferred_element_type=F32)
        yt_scr[g] = _gelu(yt)
        return carry

    for g in range(n_g):
        outputs(g, 0)

    for t in range(t_n):
        z = yt_scr[:, t * c_n:(t + 1) * c_n, :].reshape(n_g * c_n, n_chunks)
        o_ref[0, pl.ds(t, n_chunks, stride=t_n), :] = z.T


def _ssm(u_ssm, mats):
    bsz, seq, width = u_ssm.shape
    n_chunks = seq // SSM_CHUNK
    mT, pT, qT, a16 = mats
    gt = SSM_GROUP_TILE
    n_tiles = SSM_GROUPS // gt
    rows = SSM_CHUNK * SSM_GROUP_SIZE
    a16 = a16.reshape(2 * SSM_GROUPS, SSM_STATE)
    per_tile = lambda a: pl.BlockSpec((gt,) + a.shape[1:], lambda t, b: (t,) + (0,) * (a.ndim - 1))
    return pl.pallas_call(
        functools.partial(_ssm_kernel, n_chunks=n_chunks),
        grid=(n_tiles, bsz),
        in_specs=[
            pl.BlockSpec((1, seq, gt * SSM_GROUP_SIZE), lambda t, b: (b, 0, t)),
            per_tile(mT), per_tile(pT), per_tile(qT),
            pl.BlockSpec((2 * gt, SSM_STATE), lambda t, b: (t, 0)),
        ],
        out_specs=pl.BlockSpec((1, seq, gt * SSM_GROUP_SIZE), lambda t, b: (b, 0, t)),
        out_shape=jax.ShapeDtypeStruct(u_ssm.shape, F32),
        scratch_shapes=[
            pltpu.VMEM((gt, rows, n_chunks), BF16),
            pltpu.VMEM((gt, rows, n_chunks), F32),
            pltpu.VMEM((gt * n_chunks, SSM_STATE), F32),
            pltpu.VMEM((gt * n_chunks, SSM_STATE), F32),
        ],
        compiler_params=_params(2),
        name="ssm",
    )(u_ssm, mT, pT, qT, a16)


def _compress_kernel(kx_ref, vx_ref, pek_ref, pev_ref, wk1_ref, wk2_ref, wv1_ref, wv2_ref, kc_ref, vcT_ref):
    half = CMP_STRIDE * HEAD_DIM

    def mlp(x_ref, pe, w1_ref, w2_ref):
        n = x_ref.shape[3] // CMP_STRIDE
        x = jnp.concatenate([x_ref[0, 0, 0, pl.ds(l, n, stride=CMP_STRIDE), :] for l in range(CMP_STRIDE)], axis=1)
        first = jnp.dot(x, w1_ref[:half, :], precision=HIGHEST, preferred_element_type=F32)
        second = jnp.dot(x, w1_ref[half:, :], precision=HIGHEST, preferred_element_type=F32)
        bias = jnp.dot(pe, w1_ref[...], precision=HIGHEST, preferred_element_type=F32)
        h = _gelu(first + pltpu.roll(second, n - 1, 0) + bias)
        return jnp.dot(h, w2_ref[...], precision=HIGHEST, preferred_element_type=F32)

    kc = mlp(kx_ref, pek_ref[...], wk1_ref, wk2_ref)
    kc_hi = kc.astype(BF16)
    kc_lo = (kc - kc_hi.astype(F32)).astype(BF16)
    kc_ref[0, 0] = jnp.concatenate([kc_hi, kc_hi, kc_lo, jnp.zeros_like(kc_hi)], axis=1)
    vcT_ref[0, 0] = mlp(vx_ref, pev_ref[...], wv1_ref, wv2_ref).T.astype(BF16)


def _compress(kv, pe_k, pe_v, wk1, wk2, wv1, wv2):
    bsz, _, n_g, seq, width = kv.shape
    n_chunks = seq // CMP_STRIDE
    blk = lambda s: pl.BlockSpec((1, 1, 1, seq, width), lambda b, g: (b, s, g, 0, 0))
    full = lambda a: pl.BlockSpec(a.shape, lambda b, g: (0, 0))
    return pl.pallas_call(
        _compress_kernel,
        grid=(bsz, n_g),
        in_specs=[blk(0), blk(1), full(pe_k), full(pe_v), full(wk1), full(wk2), full(wv1), full(wv2)],
        out_specs=(
            pl.BlockSpec((1, 1, n_chunks, 4 * HEAD_DIM), lambda b, g: (b, g, 0, 0)),
            pl.BlockSpec((1, 1, HEAD_DIM, n_chunks), lambda b, g: (b, g, 0, 0)),
        ),
        out_shape=(
            jax.ShapeDtypeStruct((bsz, n_g, n_chunks, 4 * HEAD_DIM), BF16),
            jax.ShapeDtypeStruct((bsz, n_g, HEAD_DIM, n_chunks), BF16),
        ),
        compiler_params=_params(2),
        name="compress",
    )(kv, kv, pe_k, pe_v, wk1, wk2, wv1, wv2)


def _nsa_kernel(qT_ref, kc_ref, vcT_ref, ks_ref, vs_ref, kw_ref, vw_ref, bg_ref, o_ref,
                negm_scr, qa_scr, m_scr, acc_scr, out_scr, *, tq, n_cmp, n_sel):
    g = pl.program_id(1)
    i = pl.program_id(2)
    t_row = i * tq + lax.broadcasted_iota(jnp.int32, (1, tq), 1)
    slopes = [jnp.where(g == 0, 2.0 ** -(hh + 1), 2.0 ** -(HEADS_PER_GROUP + hh + 1)).astype(F32)
              for hh in range(HEADS_PER_GROUP)]

    def gate(hh, branch):
        r = hh * N_NSA_BRANCHES + branch
        return bg_ref[0, 0, r:r + 1, :]

    def head_rows(hh):
        return slice(hh * HEAD_DIM, (hh + 1) * HEAD_DIM)

    def head_lanes(hh):
        return slice(hh * tq, (hh + 1) * tq)

    kc3 = kc_ref[0, 0]
    vcT = vcT_ref[0, 0]
    cmp_end = lax.broadcasted_iota(jnp.int32, (n_cmp, 1), 0) * CMP_STRIDE + (CMP_BLOCK - 1)
    dist_c = (t_row - cmp_end).astype(F32)
    valid_c = dist_c >= 0.0
    piece = lax.broadcasted_iota(jnp.int32, (16, 1), 0)
    log2e_pieces = _bf16_pieces(LOG2E, N_PIECES)
    log2e_col = jnp.zeros((16, 1), F32)
    for k, value in enumerate(log2e_pieces):
        log2e_col = jnp.where((piece == k) | (piece == N_PIECES + k), value, log2e_col)
    for hh in range(HEADS_PER_GROUP):
        alibi = jnp.broadcast_to(log2e_col * slopes[hh], (16, tq)).astype(BF16)
        q_l2 = (qT_ref[0, head_rows(hh), :] * LOG2E).astype(BF16)
        qa_scr[:, head_lanes(hh)] = jnp.concatenate([q_l2, alibi], axis=0)

    def cmp_scores(hh):
        q_h = qT_ref[0, head_rows(hh), :]
        q_hi = q_h.astype(BF16)
        q_lo = (q_h - q_hi.astype(F32)).astype(BF16)
        q3 = jnp.concatenate([q_hi, q_lo, q_hi, jnp.zeros_like(q_hi)], axis=0)
        return jnp.dot(kc3, q3, preferred_element_type=F32)

    p_heads = []

    def cmp_attend(hh, s):
        s = jnp.where(valid_c, s - slopes[hh] * dist_c, MASK_VALUE)
        m = jnp.max(s, axis=0, keepdims=True)
        p = jnp.where(valid_c, jnp.exp(s - m), 0.0)
        l = jnp.sum(p, axis=0, keepdims=True)
        p = p / jnp.where(l > 0.0, l, 1.0)
        o_c = jnp.dot(vcT, p.astype(BF16), preferred_element_type=F32)
        out_scr[head_rows(hh), :] = gate(hh, 0) * o_c
        p_heads.append(p)

    def select_blocks(_):
        p_sum = sum(p_heads[1:], p_heads[0])
        blk = lax.broadcasted_iota(jnp.int32, (n_sel, n_cmp), 0)
        cmp_ix = lax.broadcasted_iota(jnp.int32, (n_sel, n_cmp), 1)
        ratio = SEL_BLOCK // CMP_STRIDE
        extra = CMP_BLOCK // CMP_STRIDE - 1
        overlap_t = jnp.where((cmp_ix >= ratio * blk - extra) & (cmp_ix <= ratio * blk + ratio - 1),
                              1.0, 0.0).astype(BF16)
        imp = jnp.zeros((n_sel, tq), F32)
        rest = p_sum
        for _ in range(3):
            part = rest.astype(BF16)
            imp = imp + jnp.dot(overlap_t, part, preferred_element_type=F32)
            rest = rest - part.astype(F32)

        j_col = lax.broadcasted_iota(jnp.int32, (n_sel, 1), 0)
        forced = (j_col == 0) | (j_col == lax.shift_right_logical(t_row, SEL_SHIFT))
        future = j_col * SEL_BLOCK > t_row
        score = jnp.where(forced, jnp.inf, jnp.where(future, -jnp.inf, imp))
        rank = jnp.zeros((n_sel, tq), F32)
        for jp in range(n_sel):
            other = score[jp:jp + 1, :]
            ahead = (other > score) | ((other == score) & (j_col > jp))
            rank = rank + jnp.where(ahead, 1.0, 0.0)
        neg = jnp.where(rank < float(min(SEL_TOP_K, n_sel)), 0.0, MASK_VALUE)
        pad_rows = jnp.zeros((16 - SEL_PER_KEY_BLOCK, tq), F32)
        for c in range(n_sel // SEL_PER_KEY_BLOCK):
            rows = neg[c * SEL_PER_KEY_BLOCK:(c + 1) * SEL_PER_KEY_BLOCK, :]
            negm_scr[c] = jnp.concatenate([rows, pad_rows], axis=0).astype(BF16)

    rel = (lax.broadcasted_iota(jnp.int32, (KEY_BLOCK, tq), 1)
           - lax.broadcasted_iota(jnp.int32, (KEY_BLOCK, tq), 0))
    zero_tail = jnp.zeros((KEY_AUG - QUERY_AUG, tq), BF16)
    no_mask = jnp.zeros((16, tq), BF16)

    sel_even, sel_odd, win_prev, win_diag = range(4)
    m_scr[...] = jnp.full(m_scr.shape, MASK_VALUE, F32)
    acc_scr[...] = jnp.zeros(acc_scr.shape, F32)

    def scores(job, hh):
        k_ref, _, kb, block_masked, _, _ = job
        k0 = pl.multiple_of(kb * KEY_BLOCK, KEY_BLOCK)
        k_aug = k_ref[0, 0, 0, pl.ds(k0, KEY_BLOCK), :]
        mask_tile = negm_scr[kb] if block_masked else no_mask
        q_aug = jnp.concatenate([qa_scr[:, head_lanes(hh)], mask_tile, zero_tail], axis=0)
        return jnp.dot(k_aug, q_aug, preferred_element_type=F32)

    def accumulate(job, hh, s):
        _, v_ref, kb, _, valid, slot = job
        cols = head_lanes(hh)
        if valid is not None:
            s = jnp.where(valid, s, MASK_VALUE)
        m_old = m_scr[slot, :, cols]
        m_new = jnp.maximum(m_old, jnp.max(s, axis=0, keepdims=True))
        p = jnp.exp2(s - jnp.maximum(m_new, 0.1 * MASK_VALUE))
        alpha = jnp.exp2(m_old - m_new)
        pv = jnp.dot(v_ref[0, 0, 0, kb], p.astype(BF16), preferred_element_type=F32)
        acc_scr[slot, :, cols] = alpha * acc_scr[slot, :, cols] + pv
        m_scr[slot, :, cols] = m_new

    def pipeline(units):
        issue = lambda unit: None if unit[0] is None else unit[0]()
        pending = [issue(unit) for unit in units[:SCORES_AHEAD]]
        for n, unit in enumerate(units):
            if n + SCORES_AHEAD < len(units):
                pending.append(issue(units[n + SCORES_AHEAD]))
            unit[1](pending.pop(0))

    def attention_units(jobs):
        return [(functools.partial(scores, job, hh), functools.partial(accumulate, job, hh))
                for job in jobs for hh in range(HEADS_PER_GROUP)]

    def finish(branch, slot_a, slot_b):
        m_a = m_scr[slot_a]
        m_b = m_scr[slot_b]
        m = jnp.maximum(m_a, m_b)
        w_a = jnp.exp2(m_a - m)
        w_b = jnp.exp2(m_b - m)
        for hh in range(HEADS_PER_GROUP):
            cols = head_lanes(hh)
            acc = acc_scr[slot_a, :, cols] * w_a[:, cols] + acc_scr[slot_b, :, cols] * w_b[:, cols]
            l = acc[HEAD_DIM:HEAD_DIM + 1, :]
            o_b = acc[:HEAD_DIM, :] / jnp.where(l > 0.0, l, 1.0)
            out_scr[head_rows(hh), :] = out_scr[head_rows(hh), :] + gate(hh, branch) * o_b

    prev = jnp.maximum(i - 1, 0)
    unpaired = jnp.broadcast_to(i & 1, (KEY_BLOCK, tq)) > 0
    causal = rel >= 0
    off_without_prev = jnp.where(i > 0, 0, WINDOW + KEY_BLOCK)
    dist_prev = rel + KEY_BLOCK + off_without_prev
    pipeline(
        [(functools.partial(cmp_scores, hh), functools.partial(cmp_attend, hh)) for hh in range(HEADS_PER_GROUP)]
        + [(None, select_blocks)]
        + attention_units([(kw_ref, vw_ref, prev, False, (dist_prev >= 0) & (dist_prev < WINDOW), win_prev),
                           (kw_ref, vw_ref, i, False, causal, win_diag),
                           (ks_ref, vs_ref, prev, True, unpaired, sel_even),
                           (ks_ref, vs_ref, i, True, causal, sel_odd)]))

    def past_pair(j, carry):
        pipeline(attention_units([(ks_ref, vs_ref, 2 * j, True, None, sel_even),
                                  (ks_ref, vs_ref, 2 * j + 1, True, None, sel_odd)]))
        return carry

    lax.fori_loop(0, lax.shift_right_logical(i, 1), past_pair, 0)
    finish(1, sel_even, sel_odd)
    finish(2, win_prev, win_diag)

    o_ref[0] = out_scr[...].T


def _nsa(qT, kc3, vcT, k_aug, v_aug, bgT, *, tq):
    bsz, _, seq = qT.shape
    assert tq == KEY_BLOCK == WINDOW and seq % tq == 0
    n_cmp = kc3.shape[2]
    n_sel = seq // SEL_BLOCK
    n_kb = seq // KEY_BLOCK
    gw = HEADS_PER_GROUP * HEAD_DIM
    bg4 = bgT.reshape(bsz, N_KV_GROUPS, GATE_ROWS, seq)
    k_spec = lambda s: pl.BlockSpec((1, 1, 1, seq, KEY_AUG), lambda b, g, i: (b, s, g, 0, 0))
    v_spec = lambda s: pl.BlockSpec((1, 1, 1, n_kb, VAL_AUG, KEY_BLOCK), lambda b, g, i: (b, s, g, 0, 0, 0))
    return pl.pallas_call(
        functools.partial(_nsa_kernel, tq=tq, n_cmp=n_cmp, n_sel=n_sel),
        grid=(bsz, N_KV_GROUPS, seq // tq),
        in_specs=[
            pl.BlockSpec((1, gw, tq), lambda b, g, i: (b, g, i)),
            pl.BlockSpec((1, 1, n_cmp, 4 * HEAD_DIM), lambda b, g, i: (b, g, 0, 0)),
            pl.BlockSpec((1, 1, HEAD_DIM, n_cmp), lambda b, g, i: (b, g, 0, 0)),
            k_spec(0), v_spec(0),
            k_spec(1), v_spec(1),
            pl.BlockSpec((1, 1, GATE_ROWS, tq), lambda b, g, i: (b, g, 0, i)),
        ],
        out_specs=pl.BlockSpec((1, tq, gw), lambda b, g, i: (b, i, g)),
        out_shape=jax.ShapeDtypeStruct((bsz, seq, ATTN_WIDTH), F32),
        scratch_shapes=[
            pltpu.VMEM((n_kb, 16, tq), BF16),
            pltpu.VMEM((HEAD_DIM + 16, HEADS_PER_GROUP * tq), BF16),
            pltpu.VMEM((4, 1, HEADS_PER_GROUP * tq), F32),
            pltpu.VMEM((4, VAL_AUG, HEADS_PER_GROUP * tq), F32),
            pltpu.VMEM((gw, tq), F32),
        ],
        compiler_params=_params(3),
        name="nsa",
    )(qT, kc3, vcT, k_aug, v_aug, k_aug, v_aug, bg4)


def _merge_kernel(x_ref, g_ref, o_ref_in, sc_ref, sh_ref, gt_ref, wm_ref, wl_ref, wg_ref, wn_ref, wo_ref,
                  lg_ref, lb_ref, y_ref):
    x = x_ref[0]
    u = (x * (1.0 + sc_ref[0]) + sh_ref[0]).astype(BF16)
    mg = jnp.dot(u, wm_ref[...], preferred_element_type=F32)
    g = g_ref[0].astype(BF16)
    z_ssm = jnp.dot(g, wl_ref[...], preferred_element_type=F32) * _sigmoid(
        jnp.dot(g, wg_ref[...], preferred_element_type=F32))
    z_nsa = jnp.dot(o_ref_in[0].astype(BF16), wn_ref[...], preferred_element_type=F32)
    merged = _sigmoid(mg[:, :D_MODEL]) * z_ssm + _sigmoid(mg[:, D_MODEL:]) * z_nsa
    mix = jnp.dot(merged.astype(BF16), wo_ref[...], preferred_element_type=F32)
    y = DEEPNORM_ALPHA * x + (1.0 + gt_ref[0]) * mix
    y_ref[0] = _layer_norm(y, lg_ref[...], lb_ref[...])


def _merge(x, g_ssm, o_nsa, scale1, shift1, gate1, w_m, w_lin, w_gate, w_nsa, w_out, ln_g, ln_b, *, tm):
    bsz, seq, _ = x.shape
    row = pl.BlockSpec((1, 1, D_MODEL), lambda b, i: (b, 0, 0))
    full = lambda a: pl.BlockSpec(a.shape, lambda b, i: (0, 0))
    return pl.pallas_call(
        _merge_kernel,
        grid=(bsz, seq // tm),
        in_specs=[
            pl.BlockSpec((1, tm, D_MODEL), lambda b, i: (b, i, 0)),
            pl.BlockSpec((1, tm, SSM_WIDTH), lambda b, i: (b, i, 0)),
            pl.BlockSpec((1, tm, ATTN_WIDTH), lambda b, i: (b, i, 0)),
            row, row, row,
            full(w_m), full(w_lin), full(w_gate), full(w_nsa), full(w_out), full(ln_g), full(ln_b),
        ],
        out_specs=pl.BlockSpec((1, tm, D_MODEL), lambda b, i: (b, i, 0)),
        out_shape=jax.ShapeDtypeStruct(x.shape, F32),
        compiler_params=_params(2),
        name="merge",
    )(x, g_ssm, o_nsa, scale1, shift1, gate1, w_m, w_lin, w_gate, w_nsa, w_out, ln_g, ln_b)


def _ffn_kernel(x_ref, sc_ref, sh_ref, gt_ref, wg_ref, wu_ref, wd_ref, lg_ref, lb_ref, y_ref, *, th):
    x = x_ref[0]
    u = (x * (1.0 + sc_ref[0]) + sh_ref[0]).astype(BF16)
    ffn = jnp.zeros(x.shape, F32)
    for c in range(FFN_HIDDEN // th):
        cols = slice(c * th, (c + 1) * th)
        a = jnp.dot(u, wg_ref[:, cols], preferred_element_type=F32)
        h = (a * _sigmoid(a)) * jnp.dot(u, wu_ref[:, cols], preferred_element_type=F32)
        ffn = ffn + jnp.dot(h.astype(BF16), wd_ref[cols, :], preferred_element_type=F32)
    y = DEEPNORM_ALPHA * x + (1.0 + gt_ref[0]) * ffn
    y_ref[0] = _layer_norm(y, lg_ref[...], lb_ref[...])


def _ffn(x, scale2, shift2, gate2, w_gate, w_up, w_down, ln_g, ln_b, *, tm, th):
    bsz, seq, _ = x.shape
    row = pl.BlockSpec((1, 1, D_MODEL), lambda b, i: (b, 0, 0))
    resident = lambda a: pl.BlockSpec(a.shape, lambda b, i: (0, 0), pipeline_mode=pl.Buffered(1))
    return pl.pallas_call(
        functools.partial(_ffn_kernel, th=th),
        grid=(bsz, seq // tm),
        in_specs=[
            pl.BlockSpec((1, tm, D_MODEL), lambda b, i: (b, i, 0)),
            row, row, row,
            resident(w_gate), resident(w_up), resident(w_down), resident(ln_g), resident(ln_b),
        ],
        out_specs=pl.BlockSpec((1, tm, D_MODEL), lambda b, i: (b, i, 0)),
        out_shape=jax.ShapeDtypeStruct(x.shape, F32),
        compiler_params=_params(2),
        name="ffn",
    )(x, scale2, shift2, gate2, w_gate, w_up, w_down, ln_g, ln_b)


def _layer(x, c, w_ada, b_ada, w_in, ssm_a_re, ssm_a_im, ssm_log_dt, ssm_b_re, ssm_b_im, ssm_c_re, ssm_c_im,
           ssm_d, w_glu_lin, w_glu_gate, cmp_pe_k, cmp_pe_v, w_cmp_k1, w_cmp_k2, w_cmp_v1, w_cmp_v2,
           w_nsa_proj, w_out, ln1_g, ln1_b, w_ffn_gate, w_ffn_up, w_ffn_down, ln2_g, ln2_b):
    bsz, seq, _ = x.shape
    n_chunks = seq // SSM_CHUNK
    tq = min(256, seq)
    tm = min(512, seq)

    mod = _ada_mod(c, w_ada, b_ada).reshape(6, bsz, 1, D_MODEL)
    shift1, scale1, gate1, shift2, scale2, gate2 = (mod[k] for k in range(6))

    w_nat, w_tr, w_mg = _split_w_in(w_in)
    u_ssm, kv_cmp, k_aug, qT, v_aug, bgT = _in_proj(x, scale1, shift1, w_nat, w_tr, tm=tm)

    mats = _ssm_matrices(ssm_a_re, ssm_a_im, ssm_log_dt, ssm_b_re, ssm_b_im, ssm_c_re, ssm_c_im, ssm_d)
    g_ssm = _ssm(u_ssm, mats)

    kc3, vcT = _compress(kv_cmp,
                        cmp_pe_k.reshape(1, CMP_BLOCK * HEAD_DIM), cmp_pe_v.reshape(1, CMP_BLOCK * HEAD_DIM),
                        w_cmp_k1.reshape(CMP_BLOCK * HEAD_DIM, -1), w_cmp_k2,
                        w_cmp_v1.reshape(CMP_BLOCK * HEAD_DIM, -1), w_cmp_v2)
    o_nsa = _nsa(qT, kc3, vcT, k_aug, v_aug, bgT, tq=tq)

    x1 = _merge(x, g_ssm, o_nsa, scale1, shift1, gate1, w_mg,
                w_glu_lin.astype(BF16), w_glu_gate.astype(BF16), w_nsa_proj.astype(BF16), w_out.astype(BF16),
                ln1_g.reshape(1, D_MODEL), ln1_b.reshape(1, D_MODEL), tm=tm)
    return _ffn(x1, scale2, shift2, gate2, w_ffn_gate.astype(BF16), w_ffn_up.astype(BF16),
                w_ffn_down.astype(BF16), ln2_g.reshape(1, D_MODEL), ln2_b.reshape(1, D_MODEL), tm=tm, th=FFN_HIDDEN // 2)


def kernel(x, c, w_ada, b_ada, w_in, ssm_a_re, ssm_a_im, ssm_log_dt, ssm_b_re, ssm_b_im, ssm_c_re, ssm_c_im,
           ssm_d, w_glu_lin, w_glu_gate, cmp_pe_k, cmp_pe_v, w_cmp_k1, w_cmp_k2, w_cmp_v1, w_cmp_v2,
           w_nsa_proj, w_out, ln1_g, ln1_b, w_ffn_gate, w_ffn_up, w_ffn_down, ln2_g, ln2_b):
    for l in range(w_ada.shape[0]):
        x = _layer(x, c, w_ada[l], b_ada[l], w_in[l], ssm_a_re[l], ssm_a_im[l], ssm_log_dt[l],
                   ssm_b_re[l], ssm_b_im[l], ssm_c_re[l], ssm_c_im[l], ssm_d[l],
                   w_glu_lin[l], w_glu_gate[l], cmp_pe_k[l], cmp_pe_v[l],
                   w_cmp_k1[l], w_cmp_k2[l], w_cmp_v1[l], w_cmp_v2[l], w_nsa_proj[l], w_out[l],
                   ln1_g[l], ln1_b[l], w_ffn_gate[l], w_ffn_up[l], w_ffn_down[l], ln2_g[l], ln2_b[l])
    return x
```

```python
import functools
import math

import jax
import jax.numpy as jnp
from jax import lax
from jax.experimental import pallas as pl
from jax.experimental.pallas import tpu as pltpu

F32 = jnp.float32
BF16 = jnp.bfloat16
HIGHEST = lax.Precision.HIGHEST

D_MODEL = 1024
SSM_WIDTH = D_MODEL // 2
SSM_GROUP_SIZE = 16
SSM_GROUPS = SSM_WIDTH // SSM_GROUP_SIZE
SSM_STATE = 64
SSM_CHUNK = 16
SSM_GROUP_TILE = 8
N_HEADS = 8
HEAD_DIM = 64
N_KV_GROUPS = 2
HEADS_PER_GROUP = N_HEADS // N_KV_GROUPS
ATTN_WIDTH = N_HEADS * HEAD_DIM
KV_WIDTH = N_KV_GROUPS * HEAD_DIM
CMP_BLOCK = 32
CMP_STRIDE = 16
SEL_BLOCK = 64
SEL_TOP_K = 8
WINDOW = 256
N_NSA_BRANCHES = 3
GATE_ROWS = 16
FFN_HIDDEN = (8 * D_MODEL + 3 * 256 - 1) // (3 * 256) * 256
DEEPNORM_ALPHA = 2.0 ** 0.25
LN_EPS = 1e-5
MASK_VALUE = -1e30

VMEM_LIMIT = 56 * 1024 * 1024
MXU_WIDTH = 256

KEY_BLOCK = 256
SEL_SHIFT = SEL_BLOCK.bit_length() - 1
SEL_PER_KEY_BLOCK = KEY_BLOCK // SEL_BLOCK
KEY_AUG = 128
ALIBI_COL = HEAD_DIM
N_PIECES = 4
MASK_COL = HEAD_DIM + 16
QUERY_AUG = HEAD_DIM + 32
VAL_AUG = HEAD_DIM + 16
LOG2E = math.log2(math.e)
SCORES_AHEAD = 5

NAT_COLS = SSM_WIDTH + 2 * KV_WIDTH + 2 * N_KV_GROUPS * KEY_AUG
TR_ROWS = ATTN_WIDTH + 2 * N_KV_GROUPS * VAL_AUG + N_KV_GROUPS * GATE_ROWS


def _bf16_pieces(value, n):
    pieces = []
    rest = value
    for _ in range(n):
        mant, expo = math.frexp(rest)
        piece = math.ldexp(round(mant * 256.0) / 256.0, expo)
        pieces.append(piece)
        rest -= piece
    return pieces


def _sigmoid(x):
    return 1.0 / (1.0 + jnp.exp(-x))


def _gelu(x):
    c = math.sqrt(2.0 / math.pi)
    return 0.5 * x * (1.0 + jnp.tanh(c * (x + 0.044715 * (x * x * x))))


def _layer_norm(y, gain, bias):
    mu = jnp.mean(y, axis=-1, keepdims=True)
    d = y - mu
    var = jnp.mean(d * d, axis=-1, keepdims=True)
    return d * lax.rsqrt(var + LN_EPS) * gain + bias


def _params(n_axes, flags=None):
    return pltpu.CompilerParams(dimension_semantics=("arbitrary",) * n_axes, vmem_limit_bytes=VMEM_LIMIT, flags=flags)


def _ada_kernel(c_ref, w_ref, b_ref, o_ref):
    c = c_ref[...]
    a = c * _sigmoid(c)
    o_ref[0] = jnp.dot(a, w_ref[...], precision=HIGHEST, preferred_element_type=F32) + b_ref[...]


def _ada_mod(c, w_ada, b_ada):
    bsz = c.shape[0]
    return pl.pallas_call(
        _ada_kernel,
        grid=(6,),
        in_specs=[
            pl.BlockSpec((bsz, D_MODEL), lambda j: (0, 0)),
            pl.BlockSpec((D_MODEL, D_MODEL), lambda j: (0, j)),
            pl.BlockSpec((1, D_MODEL), lambda j: (0, j)),
        ],
        out_specs=pl.BlockSpec((1, bsz, D_MODEL), lambda j: (j, 0, 0)),
        out_shape=jax.ShapeDtypeStruct((6, bsz, D_MODEL), F32),
        compiler_params=_params(1),
        name="ada_mod",
    )(c, w_ada, b_ada.reshape(1, 6 * D_MODEL))


def _in_proj_kernel(x_ref, sc_ref, sh_ref, wn_ref, wt_ref, ussm_ref, kc_ref, kp_ref, qT_ref, vp_ref, bgT_ref):
    i = pl.program_id(1)
    tm = x_ref.shape[1]
    u = (x_ref[0] * (1.0 + sc_ref[0]) + sh_ref[0]).astype(BF16)
    nat = jnp.dot(u, wn_ref[...], preferred_element_type=F32)
    ussm_ref[0] = nat[:, :SSM_WIDTH]
    for s in range(2):
        for g in range(N_KV_GROUPS):
            lo = SSM_WIDTH + s * KV_WIDTH + g * HEAD_DIM
            kc_ref[0, s, g] = nat[:, lo:lo + HEAD_DIM]
    pos = i * tm + lax.broadcasted_iota(jnp.int32, (tm, KEY_AUG), 0)
    col = lax.broadcasted_iota(jnp.int32, (tm, KEY_AUG), 1)
    blk = lax.shift_right_logical(pos, SEL_SHIFT)
    in_a = (col >= ALIBI_COL) & (col < ALIBI_COL + N_PIECES)
    in_b = (col >= ALIBI_COL + N_PIECES) & (col < ALIBI_COL + 2 * N_PIECES)
    hot = col == MASK_COL + (blk & (SEL_PER_KEY_BLOCK - 1))
    aux = jnp.where(in_a, blk * SEL_BLOCK, jnp.where(in_b, pos & (SEL_BLOCK - 1), jnp.where(hot, 1, 0))).astype(F32)
    for s in range(2):
        for g in range(N_KV_GROUPS):
            lo = SSM_WIDTH + 2 * KV_WIDTH + (s * N_KV_GROUPS + g) * KEY_AUG
            kp_ref[0, s, g] = (nat[:, lo:lo + KEY_AUG] + aux).astype(BF16)
    tr = lax.dot_general(wt_ref[...], u, (((1,), (1,)), ((), ())), preferred_element_type=F32)
    qT_ref[0] = tr[:ATTN_WIDTH] * (HEAD_DIM ** -0.5)
    ones_row = jnp.where(lax.broadcasted_iota(jnp.int32, (VAL_AUG, 1), 0) == HEAD_DIM, 1.0, 0.0)
    for s in range(2):
        for g in range(N_KV_GROUPS):
            lo = ATTN_WIDTH + (s * N_KV_GROUPS + g) * VAL_AUG
            v_aug = (tr[lo:lo + VAL_AUG] + ones_row).astype(BF16)
            for c in range(tm // KEY_BLOCK):
                vp_ref[0, s, g, c] = v_aug[:, c * KEY_BLOCK:(c + 1) * KEY_BLOCK]
    bgT_ref[0] = _sigmoid(tr[ATTN_WIDTH + 2 * N_KV_GROUPS * VAL_AUG:])


def _in_proj(x, scale1, shift1, w_nat, w_tr, *, tm):
    bsz, seq, _ = x.shape
    n_t = seq // tm
    n_kb = seq // KEY_BLOCK
    out_shapes = (
        jax.ShapeDtypeStruct((bsz, seq, SSM_WIDTH), F32),
        jax.ShapeDtypeStruct((bsz, 2, N_KV_GROUPS, seq, HEAD_DIM), F32),
        jax.ShapeDtypeStruct((bsz, 2, N_KV_GROUPS, seq, KEY_AUG), BF16),
        jax.ShapeDtypeStruct((bsz, ATTN_WIDTH, seq), F32),
        jax.ShapeDtypeStruct((bsz, 2, N_KV_GROUPS, n_kb, VAL_AUG, KEY_BLOCK), BF16),
        jax.ShapeDtypeStruct((bsz, N_KV_GROUPS * GATE_ROWS, seq), F32),
    )
    return pl.pallas_call(
        _in_proj_kernel,
        grid=(bsz, n_t),
        in_specs=[
            pl.BlockSpec((1, tm, D_MODEL), lambda b, i: (b, i, 0)),
            pl.BlockSpec((1, 1, D_MODEL), lambda b, i: (b, 0, 0)),
            pl.BlockSpec((1, 1, D_MODEL), lambda b, i: (b, 0, 0)),
            pl.BlockSpec((D_MODEL, NAT_COLS), lambda b, i: (0, 0)),
            pl.BlockSpec((TR_ROWS, D_MODEL), lambda b, i: (0, 0)),
        ],
        out_specs=(
            pl.BlockSpec((1, tm, SSM_WIDTH), lambda b, i: (b, i, 0)),
            pl.BlockSpec((1, 2, N_KV_GROUPS, tm, HEAD_DIM), lambda b, i: (b, 0, 0, i, 0)),
            pl.BlockSpec((1, 2, N_KV_GROUPS, tm, KEY_AUG), lambda b, i: (b, 0, 0, i, 0)),
            pl.BlockSpec((1, ATTN_WIDTH, tm), lambda b, i: (b, 0, i)),
            pl.BlockSpec((1, 2, N_KV_GROUPS, tm // KEY_BLOCK, VAL_AUG, KEY_BLOCK), lambda b, i: (b, 0, 0, i, 0, 0)),
            pl.BlockSpec((1, N_KV_GROUPS * GATE_ROWS, tm), lambda b, i: (b, 0, i)),
        ),
        out_shape=out_shapes,
        compiler_params=_params(2),
        name="in_proj",
    )(x, scale1, shift1, w_nat, w_tr)


def _split_w_in(w_in):
    o_q = SSM_WIDTH
    o_kv = o_q + ATTN_WIDTH
    o_bg = o_kv + 6 * KV_WIDTH
    o_mg = o_bg + N_NSA_BRANCHES * N_HEADS
    kv = [w_in[:, o_kv + s * KV_WIDTH:o_kv + (s + 1) * KV_WIDTH].reshape(D_MODEL, N_KV_GROUPS, HEAD_DIM)
          for s in range(6)]
    pad_k = lambda w: jnp.pad(w, ((0, 0), (0, 0), (0, KEY_AUG - HEAD_DIM))).reshape(D_MODEL, N_KV_GROUPS * KEY_AUG)
    pad_v = lambda w: jnp.pad(w, ((0, 0), (0, 0), (0, VAL_AUG - HEAD_DIM))).reshape(D_MODEL, N_KV_GROUPS * VAL_AUG)
    flat = lambda w: w.reshape(D_MODEL, KV_WIDTH)
    w_nat = jnp.concatenate([w_in[:, :o_q], flat(kv[0]), flat(kv[1]), pad_k(kv[2]), pad_k(kv[4])], axis=1)
    per_group = HEADS_PER_GROUP * N_NSA_BRANCHES
    bg = w_in[:, o_bg:o_mg].reshape(D_MODEL, N_KV_GROUPS, per_group)
    bg = jnp.pad(bg, ((0, 0), (0, 0), (0, GATE_ROWS - per_group))).reshape(D_MODEL, N_KV_GROUPS * GATE_ROWS)
    w_tr = jnp.concatenate([w_in[:, o_q:o_kv], pad_v(kv[3]), pad_v(kv[5]), bg], axis=1).T
    return w_nat.astype(BF16), w_tr.astype(BF16), w_in[:, o_mg:].astype(BF16)


def _ssm_prep_kernel(a_ref, logdt_ref, bT_ref, c_ref, d_ref, mT_ref, pT_ref, qT_ref, a16_ref):
    t_n = SSM_CHUNK
    c_n = SSM_GROUP_SIZE
    n_p = SSM_STATE
    rows = t_n * c_n
    dt = jnp.exp(logdt_ref[0])
    a_re = a_ref[0, 0]
    a_im = a_ref[0, 1]
    decay = jnp.exp(a_re * dt)
    ab_re = decay * jnp.cos(a_im * dt)
    ab_im = decay * jnp.sin(a_im * dt)
    n_re = ab_re - 1.0
    denom = a_re * a_re + a_im * a_im
    f_re = (n_re * a_re + ab_im * a_im) / denom
    f_im = (ab_im * a_re - n_re * a_im) / denom
    powers = [(jnp.ones_like(ab_re), jnp.zeros_like(ab_re))]
    for _ in range(t_n):
        p_re, p_im = powers[-1]
        powers.append((p_re * ab_re - p_im * ab_im, p_re * ab_im + p_im * ab_re))

    def per_token(first, step):
        picks = [powers[first + step * k] for k in range(t_n)]
        return (jnp.concatenate([jnp.broadcast_to(p[0], (c_n, 2 * n_p)) for p in picks], axis=0),
                jnp.concatenate([jnp.broadcast_to(p[1], (c_n, 2 * n_p)) for p in picks], axis=0))

    def cmul(x_re, x_im, y_re, y_im):
        return x_re * y_re - x_im * y_im, x_re * y_im + x_im * y_re

    bb_re, bb_im = cmul(f_re, f_im, bT_ref[0, 0], bT_ref[0, 1])
    bt_re = jnp.concatenate([bb_re] * t_n, axis=0)
    bt_im = jnp.concatenate([bb_im] * t_n, axis=0)
    ct_re = jnp.concatenate([c_ref[0, 0]] * t_n, axis=0)
    ct_im = jnp.concatenate([c_ref[0, 1]] * t_n, axis=0)

    p_re, p_im = cmul(*per_token(t_n - 1, -1), bt_re, bt_im)
    low_half = lax.broadcasted_iota(jnp.int32, (1, 2 * n_p), 1) < n_p
    pT_ref[0] = jnp.where(low_half, p_re, p_im).T.astype(BF16)

    q_re, q_im = cmul(ct_re, ct_im, *per_token(1, 1))
    qT_ref[0, 0] = q_re[:, :n_p].astype(BF16)
    qT_ref[0, 1] = (-q_im[:, :n_p]).astype(BF16)
    a16_ref[0] = jnp.concatenate([powers[t_n][0][:, :n_p], powers[t_n][1][:, :n_p]], axis=0)

    k_re, k_im = cmul(ct_re, ct_im, *per_token(0, 1))
    nt = (((1,), (1,)), ((), ()))
    strips = (lax.dot_general(k_re[:, :n_p], bt_re[:, :n_p], nt, precision=HIGHEST, preferred_element_type=F32)
              - lax.dot_general(k_im[:, :n_p], bt_im[:, :n_p], nt, precision=HIGHEST, preferred_element_type=F32))
    lane = lax.broadcasted_iota(jnp.int32, (1, rows), 1)
    s_lane = lax.shift_right_logical(lane, c_n.bit_length() - 1)
    ci_lane = lane & (c_n - 1)
    row = lax.broadcasted_iota(jnp.int32, (rows, 1), 0)
    t_row = lax.shift_right_logical(row, c_n.bit_length() - 1)
    co_row = row & (c_n - 1)
    blocks = []
    for t in range(t_n):
        acc = jnp.zeros((c_n, rows), F32)
        for lag in range(t + 1):
            acc = jnp.where(s_lane == t - lag, strips[lag * c_n:(lag + 1) * c_n, :], acc)
        blocks.append(acc)
    d_rows = jnp.concatenate([d_ref[0]] * t_n, axis=0)
    skip = jnp.where((t_row == s_lane) & (co_row == ci_lane), d_rows, 0.0)
    mT_ref[0] = (jnp.concatenate(blocks, axis=0) + skip).astype(BF16)


def _ssm_matrices(a_re, a_im, log_dt, b_re, b_im, c_re, c_im, d_skip):
    n_g, n_p = a_re.shape
    c_n = SSM_GROUP_SIZE
    rows = SSM_CHUNK * c_n
    twice = lambda x: jnp.concatenate([x, x], axis=-1).astype(F32)
    a = twice(jnp.stack([a_re, a_im], axis=1)).reshape(n_g, 2, 1, 2 * n_p)
    bT = twice(jnp.swapaxes(jnp.stack([b_re, b_im], axis=1), 2, 3))
    c = twice(jnp.stack([c_re, c_im], axis=1))
    spec = lambda shape: pl.BlockSpec((1,) + shape, lambda g: (g,) + (0,) * len(shape))
    mT, pT, qT, a16 = pl.pallas_call(
        _ssm_prep_kernel,
        grid=(n_g,),
        in_specs=[spec((2, 1, 2 * n_p)), spec((1, 1)), spec((2, c_n, 2 * n_p)), spec((2, c_n, 2 * n_p)),
                  spec((c_n, 1))],
        out_specs=(spec((rows, rows)), spec((2 * n_p, rows)), spec((2, rows, n_p)), spec((2, n_p))),
        out_shape=(
            jax.ShapeDtypeStruct((n_g, rows, rows), BF16),
            jax.ShapeDtypeStruct((n_g, 2 * n_p, rows), BF16),
            jax.ShapeDtypeStruct((n_g, 2, rows, n_p), BF16),
            jax.ShapeDtypeStruct((n_g, 2, n_p), F32),
        ),
        compiler_params=_params(1),
        name="ssm_prep",
    )(a, log_dt.astype(F32).reshape(n_g, 1, 1), bT, c, d_skip.astype(F32).reshape(n_g, c_n, 1))
    return mT, pT.reshape(n_g, 2, n_p, rows), qT, a16


def _ssm_kernel(u_ref, mT_ref, pT_ref, qT_ref, a16_ref, o_ref, ut_scr, yt_scr, sre, sim, *, n_chunks):
    n_g = SSM_GROUP_TILE
    t_n = SSM_CHUNK
    c_n = SSM_GROUP_SIZE

    for s in range(t_n):
        x_s = u_ref[0, pl.ds(s, n_chunks, stride=t_n), :]
        ut_scr[:, s * c_n:(s + 1) * c_n, :] = x_s.T.reshape(n_g, c_n, n_chunks).astype(BF16)

    def group_rows(g):
        return pl.ds(g * n_chunks, n_chunks)

    def chunk_states(g, carry):
        ut = ut_scr[g]
        sre[group_rows(g), :] = jnp.dot(pT_ref[g, 0], ut, preferred_element_type=F32).T
        sim[group_rows(g), :] = jnp.dot(pT_ref[g, 1], ut, preferred_element_type=F32).T
        return carry

    for g in range(n_g):
        chunk_states(g, 0)

    a_r = a16_ref[pl.ds(0, n_g, stride=2), :]
    a_i = a16_ref[pl.ds(1, n_g, stride=2), :]

    def carry_states(c, h):
        h_r, h_i = h
        rows = pl.ds(c, n_g, stride=n_chunks)
        s_r = sre[rows, :]
        s_i = sim[rows, :]
        sre[rows, :] = h_r
        sim[rows, :] = h_i
        return a_r * h_r - a_i * h_i + s_r, a_r * h_i + a_i * h_r + s_i

    zero = jnp.zeros((n_g, SSM_STATE), F32)
    lax.fori_loop(0, n_chunks, carry_states, (zero, zero), unroll=4)

    nt = (((1,), (1,)), ((), ()))

    def outputs(g, carry):
        yt = jnp.dot(mT_ref[g], ut_scr[g], preferred_element_type=F32)
        yt = yt + lax.dot_general(qT_ref[g, 0], sre[group_rows(g), :].astype(BF16), nt, preferred_element_type=F32)
        yt = yt + lax.dot_general(qT_ref[g, 1], sim[group_rows(g), :].astype(BF16), nt, preferred_element_type=F32)
        yt_scr[g] = _gelu(yt)
        return carry

    for g in range(n_g):
        outputs(g, 0)

    for t in range(t_n):
        z = yt_scr[:, t * c_n:(t + 1) * c_n, :].reshape(n_g * c_n, n_chunks)
        o_ref[0, pl.ds(t, n_chunks, stride=t_n), :] = z.T


def _ssm(u_ssm, mats):
    bsz, seq, width = u_ssm.shape
    n_chunks = seq // SSM_CHUNK
    mT, pT, qT, a16 = mats
    gt = SSM_GROUP_TILE
    n_tiles = SSM_GROUPS // gt
    rows = SSM_CHUNK * SSM_GROUP_SIZE
    a16 = a16.reshape(2 * SSM_GROUPS, SSM_STATE)
    per_tile = lambda a: pl.BlockSpec((gt,) + a.shape[1:], lambda t, b: (t,) + (0,) * (a.ndim - 1))
    return pl.pallas_call(
        functools.partial(_ssm_kernel, n_chunks=n_chunks),
        grid=(n_tiles, bsz),
        in_specs=[
            pl.BlockSpec((1, seq, gt * SSM_GROUP_SIZE), lambda t, b: (b, 0, t)),
            per_tile(mT), per_tile(pT), per_tile(qT),
            pl.BlockSpec((2 * gt, SSM_STATE), lambda t, b: (t, 0)),
        ],
        out_specs=pl.BlockSpec((1, seq, gt * SSM_GROUP_SIZE), lambda t, b: (b, 0, t)),
        out_shape=jax.ShapeDtypeStruct(u_ssm.shape, F32),
        scratch_shapes=[
            pltpu.VMEM((gt, rows, n_chunks), BF16),
            pltpu.VMEM((gt, rows, n_chunks), F32),
            pltpu.VMEM((gt * n_chunks, SSM_STATE), F32),
            pltpu.VMEM((gt * n_chunks, SSM_STATE), F32),
        ],
        compiler_params=_params(2),
        name="ssm",
    )(u_ssm, mT, pT, qT, a16)


def _compress_kernel(kx_ref, vx_ref, pek_ref, pev_ref, wk1_ref, wk2_ref, wv1_ref, wv2_ref, kc_ref, vcT_ref):
    half = CMP_STRIDE * HEAD_DIM

    def mlp(x_ref, pe, w1_ref, w2_ref):
        n = x_ref.shape[3] // CMP_STRIDE
        x = jnp.concatenate([x_ref[0, 0, 0, pl.ds(l, n, stride=CMP_STRIDE), :] for l in range(CMP_STRIDE)], axis=1)
        first = jnp.dot(x, w1_ref[:half, :], precision=HIGHEST, preferred_element_type=F32)
        second = jnp.dot(x, w1_ref[half:, :], precision=HIGHEST, preferred_element_type=F32)
        bias = jnp.dot(pe, w1_ref[...], precision=HIGHEST, preferred_element_type=F32)
        h = _gelu(first + pltpu.roll(second, n - 1, 0) + bias)
        return jnp.dot(h, w2_ref[...], precision=HIGHEST, preferred_element_type=F32)

    kc = mlp(kx_ref, pek_ref[...], wk1_ref, wk2_ref)
    kc_hi = kc.astype(BF16)
    kc_lo = (kc - kc_hi.astype(F32)).astype(BF16)
    kc_ref[0, 0] = jnp.concatenate([kc_hi, kc_hi, kc_lo, jnp.zeros_like(kc_hi)], axis=1)
    vcT_ref[0, 0] = mlp(vx_ref, pev_ref[...], wv1_ref, wv2_ref).T.astype(BF16)


def _compress(kv, pe_k, pe_v, wk1, wk2, wv1, wv2):
    bsz, _, n_g, seq, width = kv.shape
    n_chunks = seq // CMP_STRIDE
    blk = lambda s: pl.BlockSpec((1, 1, 1, seq, width), lambda b, g: (b, s, g, 0, 0))
    full = lambda a: pl.BlockSpec(a.shape, lambda b, g: (0, 0))
    return pl.pallas_call(
        _compress_kernel,
        grid=(bsz, n_g),
        in_specs=[blk(0), blk(1), full(pe_k), full(pe_v), full(wk1), full(wk2), full(wv1), full(wv2)],
        out_specs=(
            pl.BlockSpec((1, 1, n_chunks, 4 * HEAD_DIM), lambda b, g: (b, g, 0, 0)),
            pl.BlockSpec((1, 1, HEAD_DIM, n_chunks), lambda b, g: (b, g, 0, 0)),
        ),
        out_shape=(
            jax.ShapeDtypeStruct((bsz, n_g, n_chunks, 4 * HEAD_DIM), BF16),
            jax.ShapeDtypeStruct((bsz, n_g, HEAD_DIM, n_chunks), BF16),
        ),
        compiler_params=_params(2),
        name="compress",
    )(kv, kv, pe_k, pe_v, wk1, wk2, wv1, wv2)


def _nsa_kernel(qT_ref, kc_ref, vcT_ref, ks_ref, vs_ref, kw_ref, vw_ref, bg_ref, o_ref,
                negm_scr, qa_scr, m_scr, acc_scr, out_scr, *, tq, n_cmp, n_sel):
    g = pl.program_id(1)
    i = pl.program_id(2)
    t_row = i * tq + lax.broadcasted_iota(jnp.int32, (1, tq), 1)
    slopes = [jnp.where(g == 0, 2.0 ** -(hh + 1), 2.0 ** -(HEADS_PER_GROUP + hh + 1)).astype(F32)
              for hh in range(HEADS_PER_GROUP)]

    def gate(hh, branch):
        r = hh * N_NSA_BRANCHES + branch
        return bg_ref[0, 0, r:r + 1, :]

    def head_rows(hh):
        return slice(hh * HEAD_DIM, (hh + 1) * HEAD_DIM)

    def head_lanes(hh):
        return slice(hh * tq, (hh + 1) * tq)

    kc3 = kc_ref[0, 0]
    vcT = vcT_ref[0, 0]
    cmp_end = lax.broadcasted_iota(jnp.int32, (n_cmp, 1), 0) * CMP_STRIDE + (CMP_BLOCK - 1)
    dist_c = (t_row - cmp_end).astype(F32)
    valid_c = dist_c >= 0.0
    piece = lax.broadcasted_iota(jnp.int32, (16, 1), 0)
    log2e_pieces = _bf16_pieces(LOG2E, N_PIECES)
    log2e_col = jnp.zeros((16, 1), F32)
    for k, value in enumerate(log2e_pieces):
        log2e_col = jnp.where((piece == k) | (piece == N_PIECES + k), value, log2e_col)
    for hh in range(HEADS_PER_GROUP):
        alibi = jnp.broadcast_to(log2e_col * slopes[hh], (16, tq)).astype(BF16)
        q_l2 = (qT_ref[0, head_rows(hh), :] * LOG2E).astype(BF16)
        qa_scr[:, head_lanes(hh)] = jnp.concatenate([q_l2, alibi], axis=0)

    def cmp_scores(hh):
        q_h = qT_ref[0, head_rows(hh), :]
        q_hi = q_h.astype(BF16)
        q_lo = (q_h - q_hi.astype(F32)).astype(BF16)
        q3 = jnp.concatenate([q_hi, q_lo, q_hi, jnp.zeros_like(q_hi)], axis=0)
        return jnp.dot(kc3, q3, preferred_element_type=F32)

    p_heads = []

    def cmp_attend(hh, s):
        s = jnp.where(valid_c, s - slopes[hh] * dist_c, MASK_VALUE)
        m = jnp.max(s, axis=0, keepdims=True)
        p = jnp.where(valid_c, jnp.exp(s - m), 0.0)
        l = jnp.sum(p, axis=0, keepdims=True)
        p = p / jnp.where(l > 0.0, l, 1.0)
        o_c = jnp.dot(vcT, p.astype(BF16), preferred_element_type=F32)
        out_scr[head_rows(hh), :] = gate(hh, 0) * o_c
        p_heads.append(p)

    def select_blocks(_):
        p_sum = sum(p_heads[1:], p_heads[0])
        blk = lax.broadcasted_iota(jnp.int32, (n_sel, n_cmp), 0)
        cmp_ix = lax.broadcasted_iota(jnp.int32, (n_sel, n_cmp), 1)
        ratio = SEL_BLOCK // CMP_STRIDE
        extra = CMP_BLOCK // CMP_STRIDE - 1
        overlap_t = jnp.where((cmp_ix >= ratio * blk - extra) & (cmp_ix <= ratio * blk + ratio - 1),
                              1.0, 0.0).astype(BF16)
        imp = jnp.zeros((n_sel, tq), F32)
        rest = p_sum
        for _ in range(3):
            part = rest.astype(BF16)
            imp = imp + jnp.dot(overlap_t, part, preferred_element_type=F32)
            rest = rest - part.astype(F32)

        j_col = lax.broadcasted_iota(jnp.int32, (n_sel, 1), 0)
        forced = (j_col == 0) | (j_col == lax.shift_right_logical(t_row, SEL_SHIFT))
        future = j_col * SEL_BLOCK > t_row
        score = jnp.where(forced, jnp.inf, jnp.where(future, -jnp.inf, imp))
        rank = jnp.zeros((n_sel, tq), F32)
        for jp in range(n_sel):
            other = score[jp:jp + 1, :]
            ahead = (other > score) | ((other == score) & (j_col > jp))
            rank = rank + jnp.where(ahead, 1.0, 0.0)
        neg = jnp.where(rank < float(min(SEL_TOP_K, n_sel)), 0.0, MASK_VALUE)
        pad_rows = jnp.zeros((16 - SEL_PER_KEY_BLOCK, tq), F32)
        for c in range(n_sel // SEL_PER_KEY_BLOCK):
            rows = neg[c * SEL_PER_KEY_BLOCK:(c + 1) * SEL_PER_KEY_BLOCK, :]
            negm_scr[c] = jnp.concatenate([rows, pad_rows], axis=0).astype(BF16)

    rel = (lax.broadcasted_iota(jnp.int32, (KEY_BLOCK, tq), 1)
           - lax.broadcasted_iota(jnp.int32, (KEY_BLOCK, tq), 0))
    zero_tail = jnp.zeros((KEY_AUG - QUERY_AUG, tq), BF16)
    no_mask = jnp.zeros((16, tq), BF16)

    sel_even, sel_odd, win_prev, win_diag = range(4)
    m_scr[...] = jnp.full(m_scr.shape, MASK_VALUE, F32)
    acc_scr[...] = jnp.zeros(acc_scr.shape, F32)

    def scores(job, hh):
        k_ref, _, kb, block_masked, _, _ = job
        k0 = pl.multiple_of(kb * KEY_BLOCK, KEY_BLOCK)
        k_aug = k_ref[0, 0, 0, pl.ds(k0, KEY_BLOCK), :]
        mask_tile = negm_scr[kb] if block_masked else no_mask
        q_aug = jnp.concatenate([qa_scr[:, head_lanes(hh)], mask_tile, zero_tail], axis=0)
        return jnp.dot(k_aug, q_aug, preferred_element_type=F32)

    def accumulate(job, hh, s):
        _, v_ref, kb, _, valid, slot = job
        cols = head_lanes(hh)
        if valid is not None:
            s = jnp.where(valid, s, MASK_VALUE)
        m_old = m_scr[slot, :, cols]
        m_new = jnp.maximum(m_old, jnp.max(s, axis=0, keepdims=True))
        p = jnp.exp2(s - jnp.maximum(m_new, 0.1 * MASK_VALUE))
        alpha = jnp.exp2(m_old - m_new)
        pv = jnp.dot(v_ref[0, 0, 0, kb], p.astype(BF16), preferred_element_type=F32)
        acc_scr[slot, :, cols] = alpha * acc_scr[slot, :, cols] + pv
        m_scr[slot, :, cols] = m_new

    def pipeline(units):
        issue = lambda unit: None if unit[0] is None else unit[0]()
        pending = [issue(unit) for unit in units[:SCORES_AHEAD]]
        for n, unit in enumerate(units):
            if n + SCORES_AHEAD < len(units):
                pending.append(issue(units[n + SCORES_AHEAD]))
            unit[1](pending.pop(0))

    def attention_units(jobs):
        return [(functools.partial(scores, job, hh), functools.partial(accumulate, job, hh))
                for job in jobs for hh in range(HEADS_PER_GROUP)]

    def finish(branch, slot_a, slot_b):
        m_a = m_scr[slot_a]
        m_b = m_scr[slot_b]
        m = jnp.maximum(m_a, m_b)
        w_a = jnp.exp2(m_a - m)
        w_b = jnp.exp2(m_b - m)
        for hh in range(HEADS_PER_GROUP):
            cols = head_lanes(hh)
            acc = acc_scr[slot_a, :, cols] * w_a[:, cols] + acc_scr[slot_b, :, cols] * w_b[:, cols]
            l = acc[HEAD_DIM:HEAD_DIM + 1, :]
            o_b = acc[:HEAD_DIM, :] / jnp.where(l > 0.0, l, 1.0)
            out_scr[head_rows(hh), :] = out_scr[head_rows(hh), :] + gate(hh, branch) * o_b

    prev = jnp.maximum(i - 1, 0)
    unpaired = jnp.broadcast_to(i & 1, (KEY_BLOCK, tq)) > 0
    causal = rel >= 0
    off_without_prev = jnp.where(i > 0, 0, WINDOW + KEY_BLOCK)
    dist_prev = rel + KEY_BLOCK + off_without_prev
    pipeline(
        [(functools.partial(cmp_scores, hh), functools.partial(cmp_attend, hh)) for hh in range(HEADS_PER_GROUP)]
        + [(None, select_blocks)]
        + attention_units([(kw_ref, vw_ref, prev, False, (dist_prev >= 0) & (dist_prev < WINDOW), win_prev),
                           (kw_ref, vw_ref, i, False, causal, win_diag),
                           (ks_ref, vs_ref, prev, True, unpaired, sel_even),
                           (ks_ref, vs_ref, i, True, causal, sel_odd)]))

    def past_pair(j, carry):
        pipeline(attention_units([(ks_ref, vs_ref, 2 * j, True, None, sel_even),
                                  (ks_ref, vs_ref, 2 * j + 1, True, None, sel_odd)]))
        return carry

    lax.fori_loop(0, lax.shift_right_logical(i, 1), past_pair, 0)
    finish(1, sel_even, sel_odd)
    finish(2, win_prev, win_diag)

    o_ref[0] = out_scr[...].T


def _nsa(qT, kc3, vcT, k_aug, v_aug, bgT, *, tq):
    bsz, _, seq = qT.shape
    assert tq == KEY_BLOCK == WINDOW and seq % tq == 0
    n_cmp = kc3.shape[2]
    n_sel = seq // SEL_BLOCK
    n_kb = seq // KEY_BLOCK
    gw = HEADS_PER_GROUP * HEAD_DIM
    bg4 = bgT.reshape(bsz, N_KV_GROUPS, GATE_ROWS, seq)
    k_spec = lambda s: pl.BlockSpec((1, 1, 1, seq, KEY_AUG), lambda b, g, i: (b, s, g, 0, 0))
    v_spec = lambda s: pl.BlockSpec((1, 1, 1, n_kb, VAL_AUG, KEY_BLOCK), lambda b, g, i: (b, s, g, 0, 0, 0))
    return pl.pallas_call(
        functools.partial(_nsa_kernel, tq=tq, n_cmp=n_cmp, n_sel=n_sel),
        grid=(bsz, N_KV_GROUPS, seq // tq),
        in_specs=[
            pl.BlockSpec((1, gw, tq), lambda b, g, i: (b, g, i)),
            pl.BlockSpec((1, 1, n_cmp, 4 * HEAD_DIM), lambda b, g, i: (b, g, 0, 0)),
            pl.BlockSpec((1, 1, HEAD_DIM, n_cmp), lambda b, g, i: (b, g, 0, 0)),
            k_spec(0), v_spec(0),
            k_spec(1), v_spec(1),
            pl.BlockSpec((1, 1, GATE_ROWS, tq), lambda b, g, i: (b, g, 0, i)),
        ],
        out_specs=pl.BlockSpec((1, tq, gw), lambda b, g, i: (b, i, g)),
        out_shape=jax.ShapeDtypeStruct((bsz, seq, ATTN_WIDTH), F32),
        scratch_shapes=[
            pltpu.VMEM((n_kb, 16, tq), BF16),
            pltpu.VMEM((HEAD_DIM + 16, HEADS_PER_GROUP * tq), BF16),
            pltpu.VMEM((4, 1, HEADS_PER_GROUP * tq), F32),
            pltpu.VMEM((4, VAL_AUG, HEADS_PER_GROUP * tq), F32),
            pltpu.VMEM((gw, tq), F32),
        ],
        compiler_params=_params(3),
        name="nsa",
    )(qT, kc3, vcT, k_aug, v_aug, k_aug, v_aug, bg4)


def _merge_kernel(x_ref, g_ref, o_ref_in, sc_ref, sh_ref, gt_ref, wm_ref, wl_ref, wg_ref, wn_ref, wo_ref,
                  lg_ref, lb_ref, y_ref):
    tm = x_ref.shape[1]
    row_blocks = [slice(r * (tm // 2), (r + 1) * (tm // 2)) for r in range(2)]
    halves = [(0, D_MODEL // 2), (D_MODEL // 2, D_MODEL)]
    dot = functools.partial(jnp.dot, preferred_element_type=F32)

    def first_stage(rows):
        x = x_ref[0, rows, :]
        u = (x * (1.0 + sc_ref[0]) + sh_ref[0]).astype(BF16)
        g = g_ref[0, rows, :].astype(BF16)
        o = o_ref_in[0, rows, :].astype(BF16)
        return x, [(dot(u, wm_ref[:, lo:hi]), dot(u, wm_ref[:, D_MODEL + lo:D_MODEL + hi]),
                    dot(g, wl_ref[:, lo:hi]), dot(g, wg_ref[:, lo:hi]), dot(o, wn_ref[:, lo:hi])) for lo, hi in halves]

    def second_stage(stage):
        mix = None
        for (lo, hi), (m_ssm, m_nsa, lin, gate, z_nsa) in zip(halves, stage):
            merged = _sigmoid(m_ssm) * (lin * _sigmoid(gate)) + _sigmoid(m_nsa) * z_nsa
            part = dot(merged.astype(BF16), wo_ref[lo:hi, :])
            mix = part if mix is None else mix + part
        return mix

    firsts = [first_stage(rows) for rows in row_blocks]
    mixes = [second_stage(stage) for _, stage in firsts]
    for rows, (x, _), mix in zip(row_blocks, firsts, mixes):
        y = DEEPNORM_ALPHA * x + (1.0 + gt_ref[0]) * mix
        y_ref[0, rows, :] = _layer_norm(y, lg_ref[...], lb_ref[...])


def _merge(x, g_ssm, o_nsa, scale1, shift1, gate1, w_m, w_lin, w_gate, w_nsa, w_out, ln_g, ln_b, *, tm):
    bsz, seq, _ = x.shape
    row = pl.BlockSpec((1, 1, D_MODEL), lambda b, i: (b, 0, 0))
    full = lambda a: pl.BlockSpec(a.shape, lambda b, i: (0, 0), pipeline_mode=pl.Buffered(1))
    return pl.pallas_call(
        _merge_kernel,
        grid=(bsz, seq // tm),
        in_specs=[
            pl.BlockSpec((1, tm, D_MODEL), lambda b, i: (b, i, 0)),
            pl.BlockSpec((1, tm, SSM_WIDTH), lambda b, i: (b, i, 0)),
            pl.BlockSpec((1, tm, ATTN_WIDTH), lambda b, i: (b, i, 0)),
            row, row, row,
            full(w_m), full(w_lin), full(w_gate), full(w_nsa), full(w_out), full(ln_g), full(ln_b),
        ],
        out_specs=pl.BlockSpec((1, tm, D_MODEL), lambda b, i: (b, i, 0)),
        out_shape=jax.ShapeDtypeStruct(x.shape, F32),
        compiler_params=_params(2),
        name="merge",
    )(x, g_ssm, o_nsa, scale1, shift1, gate1, w_m, w_lin, w_gate, w_nsa, w_out, ln_g, ln_b)


def _ffn_kernel(x_ref, sc_ref, sh_ref, gt_ref, wg_ref, wu_ref, wd_ref, lg_ref, lb_ref, y_ref, *, n_chunks):
    tm = x_ref.shape[1]
    row_blocks = [slice(r * (tm // 2), (r + 1) * (tm // 2)) for r in range(2)]
    tiles = FFN_HIDDEN // MXU_WIDTH
    edges = [MXU_WIDTH * ((tiles * c + n_chunks - 1) // n_chunks) for c in range(n_chunks + 1)]
    chunks = [slice(lo, hi) for lo, hi in zip(edges[:-1], edges[1:])]
    dot = functools.partial(jnp.dot, preferred_element_type=F32)

    def gate_up(rows):
        x = x_ref[0, rows, :]
        u = (x * (1.0 + sc_ref[0]) + sh_ref[0]).astype(BF16)
        return x, [(dot(u, wg_ref[:, cols]), dot(u, wu_ref[:, cols])) for cols in chunks]

    def down(products):
        ffn = None
        for cols, (a, b) in zip(chunks, products):
            part = dot(((a * _sigmoid(a)) * b).astype(BF16), wd_ref[cols, :])
            ffn = part if ffn is None else ffn + part
        return ffn

    firsts = [gate_up(rows) for rows in row_blocks]
    ffns = [down(products) for _, products in firsts]
    for rows, (x, _), ffn in zip(row_blocks, firsts, ffns):
        y = DEEPNORM_ALPHA * x + (1.0 + gt_ref[0]) * ffn
        y_ref[0, rows, :] = _layer_norm(y, lg_ref[...], lb_ref[...])


def _ffn(x, scale2, shift2, gate2, w_gate, w_up, w_down, ln_g, ln_b, *, tm, n_chunks):
    bsz, seq, _ = x.shape
    row = pl.BlockSpec((1, 1, D_MODEL), lambda b, i: (b, 0, 0))
    resident = lambda a: pl.BlockSpec(a.shape, lambda b, i: (0, 0), pipeline_mode=pl.Buffered(1))
    return pl.pallas_call(
        functools.partial(_ffn_kernel, n_chunks=n_chunks),
        grid=(bsz, seq // tm),
        in_specs=[
            pl.BlockSpec((1, tm, D_MODEL), lambda b, i: (b, i, 0)),
            row, row, row,
            resident(w_gate), resident(w_up), resident(w_down), resident(ln_g), resident(ln_b),
        ],
        out_specs=pl.BlockSpec((1, tm, D_MODEL), lambda b, i: (b, i, 0)),
        out_shape=jax.ShapeDtypeStruct(x.shape, F32),
        compiler_params=_params(2),
        name="ffn",
    )(x, scale2, shift2, gate2, w_gate, w_up, w_down, ln_g, ln_b)


def _layer(x, c, w_ada, b_ada, w_in, ssm_a_re, ssm_a_im, ssm_log_dt, ssm_b_re, ssm_b_im, ssm_c_re, ssm_c_im,
           ssm_d, w_glu_lin, w_glu_gate, cmp_pe_k, cmp_pe_v, w_cmp_k1, w_cmp_k2, w_cmp_v1, w_cmp_v2,
           w_nsa_proj, w_out, ln1_g, ln1_b, w_ffn_gate, w_ffn_up, w_ffn_down, ln2_g, ln2_b):
    bsz, seq, _ = x.shape
    n_chunks = seq // SSM_CHUNK
    tq = min(256, seq)
    tm = min(512, seq)

    mod = _ada_mod(c, w_ada, b_ada).reshape(6, bsz, 1, D_MODEL)
    shift1, scale1, gate1, shift2, scale2, gate2 = (mod[k] for k in range(6))

    w_nat, w_tr, w_mg = _split_w_in(w_in)
    u_ssm, kv_cmp, k_aug, qT, v_aug, bgT = _in_proj(x, scale1, shift1, w_nat, w_tr, tm=tm)

    mats = _ssm_matrices(ssm_a_re, ssm_a_im, ssm_log_dt, ssm_b_re, ssm_b_im, ssm_c_re, ssm_c_im, ssm_d)
    g_ssm = _ssm(u_ssm, mats)

    kc3, vcT = _compress(kv_cmp,
                        cmp_pe_k.reshape(1, CMP_BLOCK * HEAD_DIM), cmp_pe_v.reshape(1, CMP_BLOCK * HEAD_DIM),
                        w_cmp_k1.reshape(CMP_BLOCK * HEAD_DIM, -1), w_cmp_k2,
                        w_cmp_v1.reshape(CMP_BLOCK * HEAD_DIM, -1), w_cmp_v2)
    o_nsa = _nsa(qT, kc3, vcT, k_aug, v_aug, bgT, tq=tq)

    x1 = _merge(x, g_ssm, o_nsa, scale1, shift1, gate1, w_mg,
                w_glu_lin.astype(BF16), w_glu_gate.astype(BF16), w_nsa_proj.astype(BF16), w_out.astype(BF16),
                ln1_g.reshape(1, D_MODEL), ln1_b.reshape(1, D_MODEL), tm=tm)
    return _ffn(x1, scale2, shift2, gate2, w_ffn_gate.astype(BF16), w_ffn_up.astype(BF16),
                w_ffn_down.astype(BF16), ln2_g.reshape(1, D_MODEL), ln2_b.reshape(1, D_MODEL), tm=tm, n_chunks=2)


def kernel(x, c, w_ada, b_ada, w_in, ssm_a_re, ssm_a_im, ssm_log_dt, ssm_b_re, ssm_b_im, ssm_c_re, ssm_c_im,
           ssm_d, w_glu_lin, w_glu_gate, cmp_pe_k, cmp_pe_v, w_cmp_k1, w_cmp_k2, w_cmp_v1, w_cmp_v2,
           w_nsa_proj, w_out, ln1_g, ln1_b, w_ffn_gate, w_ffn_up, w_ffn_down, ln2_g, ln2_b):
    for l in range(w_ada.shape[0]):
        x = _layer(x, c, w_ada[l], b_ada[l], w_in[l], ssm_a_re[l], ssm_a_im[l], ssm_log_dt[l],
                   ssm_b_re[l], ssm_b_im[l], ssm_c_re[l], ssm_c_im[l], ssm_d[l],
                   w_glu_lin[l], w_glu_gate[l], cmp_pe_k[l], cmp_pe_v[l],
                   w_cmp_k1[l], w_cmp_k2[l], w_cmp_v1[l], w_cmp_v2[l], w_nsa_proj[l], w_out[l],
                   ln1_g[l], ln1_b[l], w_ffn_gate[l], w_ffn_up[l], w_ffn_down[l], ln2_g[l], ln2_b[l])
    return x
```

```python
import functools
import math

import jax
import jax.numpy as jnp
from jax import lax
from jax.experimental import pallas as pl
from jax.experimental.pallas import tpu as pltpu

F32 = jnp.float32
BF16 = jnp.bfloat16
HIGHEST = lax.Precision.HIGHEST

D_MODEL = 1024
SSM_WIDTH = D_MODEL // 2
SSM_GROUP_SIZE = 16
SSM_GROUPS = SSM_WIDTH // SSM_GROUP_SIZE
SSM_STATE = 64
SSM_CHUNK = 16
SSM_GROUP_TILE = 8
SSM_STATE_ROW_PAD = 8
N_HEADS = 8
HEAD_DIM = 64
N_KV_GROUPS = 2
HEADS_PER_GROUP = N_HEADS // N_KV_GROUPS
ATTN_WIDTH = N_HEADS * HEAD_DIM
KV_WIDTH = N_KV_GROUPS * HEAD_DIM
CMP_BLOCK = 32
CMP_STRIDE = 16
CMP_HIDDEN = HEAD_DIM
SEL_BLOCK = 64
SEL_TOP_K = 8
WINDOW = 256
N_NSA_BRANCHES = 3
GATE_ROWS = 16
FFN_HIDDEN = (8 * D_MODEL + 3 * 256 - 1) // (3 * 256) * 256
DEEPNORM_ALPHA = 2.0 ** 0.25
LN_EPS = 1e-5
MASK_VALUE = -1e30

VMEM_LIMIT = 56 * 1024 * 1024
MXU_WIDTH = 256
BF16_ROWS = 16

KEY_BLOCK = 256
SEL_SHIFT = SEL_BLOCK.bit_length() - 1
SEL_PER_KEY_BLOCK = KEY_BLOCK // SEL_BLOCK
KEY_AUG = 128
ALIBI_COL = HEAD_DIM
N_PIECES = 4
MASK_COL = HEAD_DIM + 16
QUERY_AUG = HEAD_DIM + 32
VAL_AUG = HEAD_DIM + 16
LOG2E = math.log2(math.e)
SCORES_AHEAD = 7

NAT_COLS = SSM_WIDTH + 2 * KV_WIDTH + 2 * N_KV_GROUPS * KEY_AUG
TR_ROWS = ATTN_WIDTH + 2 * N_KV_GROUPS * VAL_AUG + N_KV_GROUPS * GATE_ROWS


def _bf16_pieces(value, n):
    pieces = []
    rest = value
    for _ in range(n):
        mant, expo = math.frexp(rest)
        piece = math.ldexp(round(mant * 256.0) / 256.0, expo)
        pieces.append(piece)
        rest -= piece
    return pieces


def _sigmoid(x):
    return 1.0 / (1.0 + jnp.exp(-x))


def _gelu(x):
    c = math.sqrt(2.0 / math.pi)
    return 0.5 * x * (1.0 + jnp.tanh(c * (x + 0.044715 * (x * x * x))))


def _layer_norm(y, gain, bias):
    mu = jnp.mean(y, axis=-1, keepdims=True)
    d = y - mu
    var = jnp.mean(d * d, axis=-1, keepdims=True)
    return d * lax.rsqrt(var + LN_EPS) * gain + bias


def _params(n_axes):
    return pltpu.CompilerParams(dimension_semantics=("arbitrary",) * n_axes, vmem_limit_bytes=VMEM_LIMIT)


def _ada_kernel(c_ref, w_ref, b_ref, o_ref):
    c = c_ref[...]
    a = c * _sigmoid(c)
    o_ref[0] = jnp.dot(a, w_ref[...], precision=HIGHEST, preferred_element_type=F32) + b_ref[...]


def _ada_mod(c, w_ada, b_ada):
    bsz = c.shape[0]
    return pl.pallas_call(
        _ada_kernel,
        grid=(6,),
        in_specs=[
            pl.BlockSpec((bsz, D_MODEL), lambda j: (0, 0)),
            pl.BlockSpec((D_MODEL, D_MODEL), lambda j: (0, j)),
            pl.BlockSpec((1, D_MODEL), lambda j: (0, j)),
        ],
        out_specs=pl.BlockSpec((1, bsz, D_MODEL), lambda j: (j, 0, 0)),
        out_shape=jax.ShapeDtypeStruct((6, bsz, D_MODEL), F32),
        compiler_params=_params(1),
        name="ada_mod",
    )(c, w_ada, b_ada.reshape(1, 6 * D_MODEL))


def _in_proj_kernel(x_ref, sc_ref, sh_ref, wn_ref, wt_ref, ussm_ref, kc_ref, kp_ref, qT_ref, vp_ref, bgT_ref):
    i = pl.program_id(1)
    tm = x_ref.shape[1]
    u = (x_ref[0] * (1.0 + sc_ref[0]) + sh_ref[0]).astype(BF16)
    nat = jnp.dot(u, wn_ref[...], preferred_element_type=F32)
    ussm_ref[0] = nat[:, :SSM_WIDTH]
    for s in range(2):
        for g in range(N_KV_GROUPS):
            lo = SSM_WIDTH + s * KV_WIDTH + g * HEAD_DIM
            kc_ref[0, s, g] = nat[:, lo:lo + HEAD_DIM]
    pos = i * tm + lax.broadcasted_iota(jnp.int32, (tm, KEY_AUG), 0)
    col = lax.broadcasted_iota(jnp.int32, (tm, KEY_AUG), 1)
    blk = lax.shift_right_logical(pos, SEL_SHIFT)
    in_a = (col >= ALIBI_COL) & (col < ALIBI_COL + N_PIECES)
    in_b = (col >= ALIBI_COL + N_PIECES) & (col < ALIBI_COL + 2 * N_PIECES)
    hot = col == MASK_COL + (blk & (SEL_PER_KEY_BLOCK - 1))
    aux = jnp.where(in_a, blk * SEL_BLOCK, jnp.where(in_b, pos & (SEL_BLOCK - 1), jnp.where(hot, 1, 0))).astype(F32)
    for s in range(2):
        for g in range(N_KV_GROUPS):
            lo = SSM_WIDTH + 2 * KV_WIDTH + (s * N_KV_GROUPS + g) * KEY_AUG
            kp_ref[0, s, g] = (nat[:, lo:lo + KEY_AUG] + aux).astype(BF16)
    tr = lax.dot_general(wt_ref[...], u, (((1,), (1,)), ((), ())), preferred_element_type=F32)
    qT_ref[0] = tr[:ATTN_WIDTH] * (HEAD_DIM ** -0.5)
    ones_row = jnp.where(lax.broadcasted_iota(jnp.int32, (VAL_AUG, 1), 0) == HEAD_DIM, 1.0, 0.0)
    for s in range(2):
        for g in range(N_KV_GROUPS):
            lo = ATTN_WIDTH + (s * N_KV_GROUPS + g) * VAL_AUG
            v_aug = (tr[lo:lo + VAL_AUG] + ones_row).astype(BF16)
            for c in range(tm // KEY_BLOCK):
                vp_ref[0, s, g, c] = v_aug[:, c * KEY_BLOCK:(c + 1) * KEY_BLOCK]
    bgT_ref[0] = _sigmoid(tr[ATTN_WIDTH + 2 * N_KV_GROUPS * VAL_AUG:])


def _in_proj(x, scale1, shift1, w_nat, w_tr, *, tm):
    bsz, seq, _ = x.shape
    n_t = seq // tm
    n_kb = seq // KEY_BLOCK
    out_shapes = (
        jax.ShapeDtypeStruct((bsz, seq, SSM_WIDTH), F32),
        jax.ShapeDtypeStruct((bsz, 2, N_KV_GROUPS, seq, HEAD_DIM), F32),
        jax.ShapeDtypeStruct((bsz, 2, N_KV_GROUPS, seq, KEY_AUG), BF16),
        jax.ShapeDtypeStruct((bsz, ATTN_WIDTH, seq), F32),
        jax.ShapeDtypeStruct((bsz, 2, N_KV_GROUPS, n_kb, VAL_AUG, KEY_BLOCK), BF16),
        jax.ShapeDtypeStruct((bsz, N_KV_GROUPS * GATE_ROWS, seq), F32),
    )
    return pl.pallas_call(
        _in_proj_kernel,
        grid=(bsz, n_t),
        in_specs=[
            pl.BlockSpec((1, tm, D_MODEL), lambda b, i: (b, i, 0)),
            pl.BlockSpec((1, 1, D_MODEL), lambda b, i: (b, 0, 0)),
            pl.BlockSpec((1, 1, D_MODEL), lambda b, i: (b, 0, 0)),
            pl.BlockSpec((D_MODEL, NAT_COLS), lambda b, i: (0, 0)),
            pl.BlockSpec((TR_ROWS, D_MODEL), lambda b, i: (0, 0)),
        ],
        out_specs=(
            pl.BlockSpec((1, tm, SSM_WIDTH), lambda b, i: (b, i, 0)),
            pl.BlockSpec((1, 2, N_KV_GROUPS, tm, HEAD_DIM), lambda b, i: (b, 0, 0, i, 0)),
            pl.BlockSpec((1, 2, N_KV_GROUPS, tm, KEY_AUG), lambda b, i: (b, 0, 0, i, 0)),
            pl.BlockSpec((1, ATTN_WIDTH, tm), lambda b, i: (b, 0, i)),
            pl.BlockSpec((1, 2, N_KV_GROUPS, tm // KEY_BLOCK, VAL_AUG, KEY_BLOCK), lambda b, i: (b, 0, 0, i, 0, 0)),
            pl.BlockSpec((1, N_KV_GROUPS * GATE_ROWS, tm), lambda b, i: (b, 0, i)),
        ),
        out_shape=out_shapes,
        compiler_params=_params(2),
        name="in_proj",
    )(x, scale1, shift1, w_nat, w_tr)


def _split_w_in(w_in):
    o_q = SSM_WIDTH
    o_kv = o_q + ATTN_WIDTH
    o_bg = o_kv + 6 * KV_WIDTH
    o_mg = o_bg + N_NSA_BRANCHES * N_HEADS
    kv = [w_in[:, o_kv + s * KV_WIDTH:o_kv + (s + 1) * KV_WIDTH].reshape(D_MODEL, N_KV_GROUPS, HEAD_DIM)
          for s in range(6)]
    pad_k = lambda w: jnp.pad(w, ((0, 0), (0, 0), (0, KEY_AUG - HEAD_DIM))).reshape(D_MODEL, N_KV_GROUPS * KEY_AUG)
    pad_v = lambda w: jnp.pad(w, ((0, 0), (0, 0), (0, VAL_AUG - HEAD_DIM))).reshape(D_MODEL, N_KV_GROUPS * VAL_AUG)
    flat = lambda w: w.reshape(D_MODEL, KV_WIDTH)
    w_nat = jnp.concatenate([w_in[:, :o_q], flat(kv[0]), flat(kv[1]), pad_k(kv[2]), pad_k(kv[4])], axis=1)
    per_group = HEADS_PER_GROUP * N_NSA_BRANCHES
    bg = w_in[:, o_bg:o_mg].reshape(D_MODEL, N_KV_GROUPS, per_group)
    bg = jnp.pad(bg, ((0, 0), (0, 0), (0, GATE_ROWS - per_group))).reshape(D_MODEL, N_KV_GROUPS * GATE_ROWS)
    w_tr = jnp.concatenate([w_in[:, o_q:o_kv], pad_v(kv[3]), pad_v(kv[5]), bg], axis=1).T
    return w_nat.astype(BF16), w_tr.astype(BF16), w_in[:, o_mg:].astype(BF16)


def _ssm_prep_kernel(a_ref, logdt_ref, bT_ref, c_ref, d_ref, mT_ref, pT_ref, qT_ref, a16_ref):
    t_n = SSM_CHUNK
    c_n = SSM_GROUP_SIZE
    n_p = SSM_STATE
    rows = t_n * c_n
    dt = jnp.exp(logdt_ref[0])
    a_re = a_ref[0, 0]
    a_im = a_ref[0, 1]
    decay = jnp.exp(a_re * dt)
    ab_re = decay * jnp.cos(a_im * dt)
    ab_im = decay * jnp.sin(a_im * dt)
    n_re = ab_re - 1.0
    denom = a_re * a_re + a_im * a_im
    f_re = (n_re * a_re + ab_im * a_im) / denom
    f_im = (ab_im * a_re - n_re * a_im) / denom
    powers = [(jnp.ones_like(ab_re), jnp.zeros_like(ab_re))]
    for _ in range(t_n):
        p_re, p_im = powers[-1]
        powers.append((p_re * ab_re - p_im * ab_im, p_re * ab_im + p_im * ab_re))

    def per_token(first, step):
        picks = [powers[first + step * k] for k in range(t_n)]
        return (jnp.concatenate([jnp.broadcast_to(p[0], (c_n, 2 * n_p)) for p in picks], axis=0),
                jnp.concatenate([jnp.broadcast_to(p[1], (c_n, 2 * n_p)) for p in picks], axis=0))

    def cmul(x_re, x_im, y_re, y_im):
        return x_re * y_re - x_im * y_im, x_re * y_im + x_im * y_re

    bb_re, bb_im = cmul(f_re, f_im, bT_ref[0, 0], bT_ref[0, 1])
    bt_re = jnp.concatenate([bb_re] * t_n, axis=0)
    bt_im = jnp.concatenate([bb_im] * t_n, axis=0)
    ct_re = jnp.concatenate([c_ref[0, 0]] * t_n, axis=0)
    ct_im = jnp.concatenate([c_ref[0, 1]] * t_n, axis=0)

    p_re, p_im = cmul(*per_token(t_n - 1, -1), bt_re, bt_im)
    low_half = lax.broadcasted_iota(jnp.int32, (1, 2 * n_p), 1) < n_p
    pT_ref[0] = jnp.where(low_half, p_re, p_im).T.astype(BF16)

    q_re, q_im = cmul(ct_re, ct_im, *per_token(1, 1))
    qT_ref[0, 0] = q_re[:, :n_p].astype(BF16)
    qT_ref[0, 1] = (-q_im[:, :n_p]).astype(BF16)
    a16_ref[0] = jnp.concatenate([powers[t_n][0][:, :n_p], powers[t_n][1][:, :n_p]], axis=0)

    k_re, k_im = cmul(ct_re, ct_im, *per_token(0, 1))
    nt = (((1,), (1,)), ((), ()))
    strips = (lax.dot_general(k_re[:, :n_p], bt_re[:, :n_p], nt, precision=HIGHEST, preferred_element_type=F32)
              - lax.dot_general(k_im[:, :n_p], bt_im[:, :n_p], nt, precision=HIGHEST, preferred_element_type=F32))
    lane = lax.broadcasted_iota(jnp.int32, (1, rows), 1)
    s_lane = lax.shift_right_logical(lane, c_n.bit_length() - 1)
    ci_lane = lane & (c_n - 1)
    row = lax.broadcasted_iota(jnp.int32, (rows, 1), 0)
    t_row = lax.shift_right_logical(row, c_n.bit_length() - 1)
    co_row = row & (c_n - 1)
    blocks = []
    for t in range(t_n):
        acc = jnp.zeros((c_n, rows), F32)
        for lag in range(t + 1):
            acc = jnp.where(s_lane == t - lag, strips[lag * c_n:(lag + 1) * c_n, :], acc)
        blocks.append(acc)
    d_rows = jnp.concatenate([d_ref[0]] * t_n, axis=0)
    skip = jnp.where((t_row == s_lane) & (co_row == ci_lane), d_rows, 0.0)
    mT_ref[0] = (jnp.concatenate(blocks, axis=0) + skip).astype(BF16)


def _ssm_matrices(a_re, a_im, log_dt, b_re, b_im, c_re, c_im, d_skip):
    n_g, n_p = a_re.shape
    c_n = SSM_GROUP_SIZE
    rows = SSM_CHUNK * c_n
    twice = lambda x: jnp.concatenate([x, x], axis=-1).astype(F32)
    a = twice(jnp.stack([a_re, a_im], axis=1)).reshape(n_g, 2, 1, 2 * n_p)
    bT = twice(jnp.swapaxes(jnp.stack([b_re, b_im], axis=1), 2, 3))
    c = twice(jnp.stack([c_re, c_im], axis=1))
    spec = lambda shape: pl.BlockSpec((1,) + shape, lambda g: (g,) + (0,) * len(shape))
    mT, pT, qT, a16 = pl.pallas_call(
        _ssm_prep_kernel,
        grid=(n_g,),
        in_specs=[spec((2, 1, 2 * n_p)), spec((1, 1)), spec((2, c_n, 2 * n_p)), spec((2, c_n, 2 * n_p)),
                  spec((c_n, 1))],
        out_specs=(spec((rows, rows)), spec((2 * n_p, rows)), spec((2, rows, n_p)), spec((2, n_p))),
        out_shape=(
            jax.ShapeDtypeStruct((n_g, rows, rows), BF16),
            jax.ShapeDtypeStruct((n_g, 2 * n_p, rows), BF16),
            jax.ShapeDtypeStruct((n_g, 2, rows, n_p), BF16),
            jax.ShapeDtypeStruct((n_g, 2, n_p), F32),
        ),
        compiler_params=_params(1),
        name="ssm_prep",
    )(a, log_dt.astype(F32).reshape(n_g, 1, 1), bT, c, d_skip.astype(F32).reshape(n_g, c_n, 1))
    return mT, pT.reshape(n_g, 2, n_p, rows), qT, a16


def _ssm_kernel(u_ref, mT_ref, pT_ref, qT_ref, a16_ref, o_ref, ut_scr, yt_scr, sre, sim, *, n_chunks):
    n_g = SSM_GROUP_TILE
    t_n = SSM_CHUNK
    c_n = SSM_GROUP_SIZE

    for s in range(t_n):
        x_s = u_ref[0, pl.ds(s, n_chunks, stride=t_n), :]
        ut_scr[:, s * c_n:(s + 1) * c_n, :] = x_s.T.reshape(n_g, c_n, n_chunks).astype(BF16)

    pitch = n_chunks + SSM_STATE_ROW_PAD

    def group_rows(g):
        return pl.ds(g * pitch, n_chunks)

    def chunk_states(g, carry):
        ut = ut_scr[g]
        sre[group_rows(g), :] = jnp.dot(pT_ref[g, 0], ut, preferred_element_type=F32).T
        sim[group_rows(g), :] = jnp.dot(pT_ref[g, 1], ut, preferred_element_type=F32).T
        return carry

    for g in range(n_g):
        chunk_states(g, 0)

    a_r = a16_ref[pl.ds(0, n_g, stride=2), :]
    a_i = a16_ref[pl.ds(1, n_g, stride=2), :]

    def carry_states(c, h):
        h_r, h_i = h
        rows = pl.ds(c, n_g, stride=pitch)
        s_r = sre[rows, :]
        s_i = sim[rows, :]
        sre[rows, :] = h_r
        sim[rows, :] = h_i
        return a_r * h_r - a_i * h_i + s_r, a_r * h_i + a_i * h_r + s_i

    zero = jnp.zeros((n_g, SSM_STATE), F32)
    lax.fori_loop(0, n_chunks, carry_states, (zero, zero), unroll=4)

    nt = (((1,), (1,)), ((), ()))

    def outputs(g, carry):
        yt = jnp.dot(mT_ref[g], ut_scr[g], preferred_element_type=F32)
        yt = yt + lax.dot_general(qT_ref[g, 0], sre[group_rows(g), :].astype(BF16), nt, preferred_element_type=F32)
        yt = yt + lax.dot_general(qT_ref[g, 1], sim[group_rows(g), :].astype(BF16), nt, preferred_element_type=F32)
        yt_scr[g] = _gelu(yt)
        return carry

    for g in range(n_g):
        outputs(g, 0)

    for t in range(t_n):
        z = yt_scr[:, t * c_n:(t + 1) * c_n, :].reshape(n_g * c_n, n_chunks)
        o_ref[0, pl.ds(t, n_chunks, stride=t_n), :] = z.T


def _ssm(u_ssm, mats):
    bsz, seq, width = u_ssm.shape
    n_chunks = seq // SSM_CHUNK
    mT, pT, qT, a16 = mats
    gt = SSM_GROUP_TILE
    n_tiles = SSM_GROUPS // gt
    rows = SSM_CHUNK * SSM_GROUP_SIZE
    a16 = a16.reshape(2 * SSM_GROUPS, SSM_STATE)
    per_tile = lambda a: pl.BlockSpec((gt,) + a.shape[1:], lambda t, b: (t,) + (0,) * (a.ndim - 1))
    return pl.pallas_call(
        functools.partial(_ssm_kernel, n_chunks=n_chunks),
        grid=(n_tiles, bsz),
        in_specs=[
            pl.BlockSpec((1, seq, gt * SSM_GROUP_SIZE), lambda t, b: (b, 0, t)),
            per_tile(mT), per_tile(pT), per_tile(qT),
            pl.BlockSpec((2 * gt, SSM_STATE), lambda t, b: (t, 0)),
        ],
        out_specs=pl.BlockSpec((1, seq, gt * SSM_GROUP_SIZE), lambda t, b: (b, 0, t)),
        out_shape=jax.ShapeDtypeStruct(u_ssm.shape, F32),
        scratch_shapes=[
            pltpu.VMEM((gt, rows, n_chunks), BF16),
            pltpu.VMEM((gt, rows, n_chunks), F32),
            pltpu.VMEM((gt * (n_chunks + SSM_STATE_ROW_PAD), SSM_STATE), F32),
            pltpu.VMEM((gt * (n_chunks + SSM_STATE_ROW_PAD), SSM_STATE), F32),
        ],
        compiler_params=_params(2),
        name="ssm",
    )(u_ssm, mT, pT, qT, a16)


def _split3(x):
    hi = x.astype(BF16)
    lo = (x - hi.astype(F32)).astype(BF16)
    return jnp.concatenate([hi, lo, hi], axis=-1)


def _stack3(w):
    hi = w.astype(BF16)
    lo = (w - hi.astype(F32)).astype(BF16)
    return jnp.concatenate([hi, hi, lo], axis=0)


def _compress_kernel(kx_ref, vx_ref, pek_ref, pev_ref, wk1_ref, wk2_ref, wv1_ref, wv2_ref, kc_ref, vcT_ref):
    def mlp(x_ref, pe_ref, w1_ref, w2_ref):
        n = x_ref.shape[3] // CMP_STRIDE
        x = jnp.concatenate([x_ref[0, 0, 0, pl.ds(l, n, stride=CMP_STRIDE), :] for l in range(CMP_STRIDE)], axis=1)
        both = jnp.dot(_split3(x), w1_ref[...], preferred_element_type=F32)
        pe_both = jnp.dot(_split3(pe_ref[...]), w1_ref[...], preferred_element_type=F32)
        bias = pe_both[0:1, :CMP_HIDDEN] + pe_both[1:2, CMP_HIDDEN:]
        h = _gelu(both[:, :CMP_HIDDEN] + pltpu.roll(both[:, CMP_HIDDEN:], n - 1, 0) + bias)
        return jnp.dot(_split3(h), w2_ref[...], preferred_element_type=F32)

    kc = mlp(kx_ref, pek_ref, wk1_ref, wk2_ref)
    kc_hi = kc.astype(BF16)
    kc_lo = (kc - kc_hi.astype(F32)).astype(BF16)
    kc_ref[0, 0] = jnp.concatenate([kc_hi, kc_hi, kc_lo, jnp.zeros_like(kc_hi)], axis=1)
    vcT_ref[0, 0] = mlp(vx_ref, pev_ref, wv1_ref, wv2_ref).T.astype(BF16)


def _compress(kv, pe_k, pe_v, wk1, wk2, wv1, wv2):
    bsz, _, n_g, seq, width = kv.shape
    n_chunks = seq // CMP_STRIDE
    half = CMP_STRIDE * HEAD_DIM
    side_by_side = lambda w1: _stack3(jnp.concatenate([w1[:half], w1[half:]], axis=1))
    two_rows = lambda pe: jnp.pad(pe.reshape(2, half), ((0, BF16_ROWS - 2), (0, 0)))
    pe_k, pe_v = two_rows(pe_k), two_rows(pe_v)
    wk1, wv1, wk2, wv2 = side_by_side(wk1), side_by_side(wv1), _stack3(wk2), _stack3(wv2)
    blk = lambda s: pl.BlockSpec((1, 1, 1, seq, width), lambda b, g: (b, s, g, 0, 0))
    full = lambda a: pl.BlockSpec(a.shape, lambda b, g: (0, 0))
    return pl.pallas_call(
        _compress_kernel,
        grid=(bsz, n_g),
        in_specs=[blk(0), blk(1), full(pe_k), full(pe_v), full(wk1), full(wk2), full(wv1), full(wv2)],
        out_specs=(
            pl.BlockSpec((1, 1, n_chunks, 4 * HEAD_DIM), lambda b, g: (b, g, 0, 0)),
            pl.BlockSpec((1, 1, HEAD_DIM, n_chunks), lambda b, g: (b, g, 0, 0)),
        ),
        out_shape=(
            jax.ShapeDtypeStruct((bsz, n_g, n_chunks, 4 * HEAD_DIM), BF16),
            jax.ShapeDtypeStruct((bsz, n_g, HEAD_DIM, n_chunks), BF16),
        ),
        compiler_params=_params(2),
        name="compress",
    )(kv, kv, pe_k, pe_v, wk1, wk2, wv1, wv2)


def _nsa_kernel(qT_ref, kc_ref, vcT_ref, ks_ref, vs_ref, kw_ref, vw_ref, bg_ref, o_ref,
                negm_scr, qa_scr, m_scr, acc_scr, out_scr, *, tq, n_cmp, n_sel):
    g = pl.program_id(1)
    i = pl.program_id(2)
    t_row = i * tq + lax.broadcasted_iota(jnp.int32, (1, tq), 1)
    slopes = [jnp.where(g == 0, 2.0 ** -(hh + 1), 2.0 ** -(HEADS_PER_GROUP + hh + 1)).astype(F32)
              for hh in range(HEADS_PER_GROUP)]

    def gate(hh, branch):
        r = hh * N_NSA_BRANCHES + branch
        return bg_ref[0, 0, r:r + 1, :]

    def head_rows(hh):
        return slice(hh * HEAD_DIM, (hh + 1) * HEAD_DIM)

    def head_lanes(hh):
        return slice(hh * tq, (hh + 1) * tq)

    kc3 = kc_ref[0, 0]
    vcT = vcT_ref[0, 0]
    cmp_end = lax.broadcasted_iota(jnp.int32, (n_cmp, 1), 0) * CMP_STRIDE + (CMP_BLOCK - 1)
    dist_c = (t_row - cmp_end).astype(F32)
    valid_c = dist_c >= 0.0
    piece = lax.broadcasted_iota(jnp.int32, (16, 1), 0)
    log2e_pieces = _bf16_pieces(LOG2E, N_PIECES)
    log2e_col = jnp.zeros((16, 1), F32)
    for k, value in enumerate(log2e_pieces):
        log2e_col = jnp.where((piece == k) | (piece == N_PIECES + k), value, log2e_col)
    for hh in range(HEADS_PER_GROUP):
        alibi = jnp.broadcast_to(log2e_col * slopes[hh], (16, tq)).astype(BF16)
        q_l2 = (qT_ref[0, head_rows(hh), :] * LOG2E).astype(BF16)
        qa_scr[:, head_lanes(hh)] = jnp.concatenate([q_l2, alibi], axis=0)

    def cmp_scores(hh):
        q_h = qT_ref[0, head_rows(hh), :]
        q_hi = q_h.astype(BF16)
        q_lo = (q_h - q_hi.astype(F32)).astype(BF16)
        q3 = jnp.concatenate([q_hi, q_lo, q_hi, jnp.zeros_like(q_hi)], axis=0)
        return jnp.dot(kc3, q3, preferred_element_type=F32)

    p_heads = []

    def cmp_attend(hh, s):
        s = jnp.where(valid_c, s - slopes[hh] * dist_c, MASK_VALUE)
        m = jnp.max(s, axis=0, keepdims=True)
        p = jnp.where(valid_c, jnp.exp(s - m), 0.0)
        l = jnp.sum(p, axis=0, keepdims=True)
        p = p / jnp.where(l > 0.0, l, 1.0)
        o_c = jnp.dot(vcT, p.astype(BF16), preferred_element_type=F32)
        out_scr[head_rows(hh), :] = gate(hh, 0) * o_c
        p_heads.append(p)

    def select_blocks(_):
        p_sum = sum(p_heads[1:], p_heads[0])
        blk = lax.broadcasted_iota(jnp.int32, (n_sel, n_cmp), 0)
        cmp_ix = lax.broadcasted_iota(jnp.int32, (n_sel, n_cmp), 1)
        ratio = SEL_BLOCK // CMP_STRIDE
        extra = CMP_BLOCK // CMP_STRIDE - 1
        overlap_t = jnp.where((cmp_ix >= ratio * blk - extra) & (cmp_ix <= ratio * blk + ratio - 1),
                              1.0, 0.0).astype(BF16)
        imp = jnp.zeros((n_sel, tq), F32)
        rest = p_sum
        for _ in range(3):
            part = rest.astype(BF16)
            imp = imp + jnp.dot(overlap_t, part, preferred_element_type=F32)
            rest = rest - part.astype(F32)

        j_col = lax.broadcasted_iota(jnp.int32, (n_sel, 1), 0)
        forced = (j_col == 0) | (j_col == lax.shift_right_logical(t_row, SEL_SHIFT))
        future = j_col * SEL_BLOCK > t_row
        score = jnp.where(forced, jnp.inf, jnp.where(future, -jnp.inf, imp))
        rank = jnp.zeros((n_sel, tq), F32)
        for jp in range(n_sel):
            other = score[jp:jp + 1, :]
            ahead = (other > score) | ((other == score) & (j_col > jp))
            rank = rank + jnp.where(ahead, 1.0, 0.0)
        neg = jnp.where(rank < float(min(SEL_TOP_K, n_sel)), 0.0, MASK_VALUE)
        pad_rows = jnp.zeros((16 - SEL_PER_KEY_BLOCK, tq), F32)
        for c in range(n_sel // SEL_PER_KEY_BLOCK):
            rows = neg[c * SEL_PER_KEY_BLOCK:(c + 1) * SEL_PER_KEY_BLOCK, :]
            negm_scr[c] = jnp.concatenate([rows, pad_rows], axis=0).astype(BF16)

    rel = (lax.broadcasted_iota(jnp.int32, (KEY_BLOCK, tq), 1)
           - lax.broadcasted_iota(jnp.int32, (KEY_BLOCK, tq), 0))
    zero_tail = jnp.zeros((KEY_AUG - QUERY_AUG, tq), BF16)
    no_mask = jnp.zeros((16, tq), BF16)

    sel_even, sel_odd, win_prev, win_diag = range(4)
    m_scr[...] = jnp.full(m_scr.shape, MASK_VALUE, F32)
    acc_scr[...] = jnp.zeros(acc_scr.shape, F32)

    def scores(job, hh):
        k_ref, _, kb, block_masked, _, _ = job
        k0 = pl.multiple_of(kb * KEY_BLOCK, KEY_BLOCK)
        k_aug = k_ref[0, 0, 0, pl.ds(k0, KEY_BLOCK), :]
        mask_tile = negm_scr[kb] if block_masked else no_mask
        q_aug = jnp.concatenate([qa_scr[:, head_lanes(hh)], mask_tile, zero_tail], axis=0)
        return jnp.dot(k_aug, q_aug, preferred_element_type=F32)

    def accumulate(job, hh, s):
        _, v_ref, kb, _, valid, slot = job
        cols = head_lanes(hh)
        if valid is not None:
            s = jnp.where(valid, s, MASK_VALUE)
        m_old = m_scr[slot, :, cols]
        m_new = jnp.maximum(m_old, jnp.max(s, axis=0, keepdims=True))
        p = jnp.exp2(s - jnp.maximum(m_new, 0.1 * MASK_VALUE))
        alpha = jnp.exp2(m_old - m_new)
        pv = jnp.dot(v_ref[0, 0, 0, kb], p.astype(BF16), preferred_element_type=F32)
        acc_scr[slot, :, cols] = alpha * acc_scr[slot, :, cols] + pv
        m_scr[slot, :, cols] = m_new

    def pipeline(units):
        issue = lambda unit: None if unit[0] is None else unit[0]()
        pending = [issue(unit) for unit in units[:SCORES_AHEAD]]
        for n, unit in enumerate(units):
            if n + SCORES_AHEAD < len(units):
                pending.append(issue(units[n + SCORES_AHEAD]))
            unit[1](pending.pop(0))

    def attention_units(jobs):
        return [(functools.partial(scores, job, hh), functools.partial(accumulate, job, hh))
                for job in jobs for hh in range(HEADS_PER_GROUP)]

    def finish(branch, slot_a, slot_b):
        m_a = m_scr[slot_a]
        m_b = m_scr[slot_b]
        m = jnp.maximum(m_a, m_b)
        w_a = jnp.exp2(m_a - m)
        w_b = jnp.exp2(m_b - m)
        for hh in range(HEADS_PER_GROUP):
            cols = head_lanes(hh)
            acc = acc_scr[slot_a, :, cols] * w_a[:, cols] + acc_scr[slot_b, :, cols] * w_b[:, cols]
            l = acc[HEAD_DIM:HEAD_DIM + 1, :]
            o_b = acc[:HEAD_DIM, :] / jnp.where(l > 0.0, l, 1.0)
            out_scr[head_rows(hh), :] = out_scr[head_rows(hh), :] + gate(hh, branch) * o_b

    prev = jnp.maximum(i - 1, 0)
    unpaired = jnp.broadcast_to(i & 1, (KEY_BLOCK, tq)) > 0
    causal = rel >= 0
    off_without_prev = jnp.where(i > 0, 0, WINDOW + KEY_BLOCK)
    dist_prev = rel + KEY_BLOCK + off_without_prev
    pipeline(
        [(functools.partial(cmp_scores, hh), functools.partial(cmp_attend, hh)) for hh in range(HEADS_PER_GROUP)]
        + [(None, select_blocks)]
        + attention_units([(kw_ref, vw_ref, prev, False, (dist_prev >= 0) & (dist_prev < WINDOW), win_prev),
                           (kw_ref, vw_ref, i, False, causal, win_diag),
                           (ks_ref, vs_ref, prev, True, unpaired, sel_even),
                           (ks_ref, vs_ref, i, True, causal, sel_odd)]))

    def past_pair(j, carry):
        pipeline(attention_units([(ks_ref, vs_ref, 2 * j, True, None, sel_even),
                                  (ks_ref, vs_ref, 2 * j + 1, True, None, sel_odd)]))
        return carry

    lax.fori_loop(0, lax.shift_right_logical(i, 1), past_pair, 0)
    finish(1, sel_even, sel_odd)
    finish(2, win_prev, win_diag)

    o_ref[0] = out_scr[...].T


def _nsa(qT, kc3, vcT, k_aug, v_aug, bgT, *, tq):
    bsz, _, seq = qT.shape
    assert tq == KEY_BLOCK == WINDOW and seq % tq == 0
    n_cmp = kc3.shape[2]
    n_sel = seq // SEL_BLOCK
    n_kb = seq // KEY_BLOCK
    gw = HEADS_PER_GROUP * HEAD_DIM
    bg4 = bgT.reshape(bsz, N_KV_GROUPS, GATE_ROWS, seq)
    k_spec = lambda s: pl.BlockSpec((1, 1, 1, seq, KEY_AUG), lambda b, g, i: (b, s, g, 0, 0))
    v_spec = lambda s: pl.BlockSpec((1, 1, 1, n_kb, VAL_AUG, KEY_BLOCK), lambda b, g, i: (b, s, g, 0, 0, 0))
    return pl.pallas_call(
        functools.partial(_nsa_kernel, tq=tq, n_cmp=n_cmp, n_sel=n_sel),
        grid=(bsz, N_KV_GROUPS, seq // tq),
        in_specs=[
            pl.BlockSpec((1, gw, tq), lambda b, g, i: (b, g, i)),
            pl.BlockSpec((1, 1, n_cmp, 4 * HEAD_DIM), lambda b, g, i: (b, g, 0, 0)),
            pl.BlockSpec((1, 1, HEAD_DIM, n_cmp), lambda b, g, i: (b, g, 0, 0)),
            k_spec(0), v_spec(0),
            k_spec(1), v_spec(1),
            pl.BlockSpec((1, 1, GATE_ROWS, tq), lambda b, g, i: (b, g, 0, i)),
        ],
        out_specs=pl.BlockSpec((1, tq, gw), lambda b, g, i: (b, i, g)),
        out_shape=jax.ShapeDtypeStruct((bsz, seq, ATTN_WIDTH), F32),
        scratch_shapes=[
            pltpu.VMEM((n_kb, 16, tq), BF16),
            pltpu.VMEM((HEAD_DIM + 16, HEADS_PER_GROUP * tq), BF16),
            pltpu.VMEM((4, 1, HEADS_PER_GROUP * tq), F32),
            pltpu.VMEM((4, VAL_AUG, HEADS_PER_GROUP * tq), F32),
            pltpu.VMEM((gw, tq), F32),
        ],
        compiler_params=_params(3),
        name="nsa",
    )(qT, kc3, vcT, k_aug, v_aug, k_aug, v_aug, bg4)


def _merge_kernel(x_ref, g_ref, o_ref_in, sc_ref, sh_ref, gt_ref, wm_ref, wl_ref, wg_ref, wn_ref, wo_ref,
                  lg_ref, lb_ref, y_ref):
    tm = x_ref.shape[1]
    row_blocks = [slice(r * (tm // 2), (r + 1) * (tm // 2)) for r in range(2)]
    halves = [(0, D_MODEL // 2), (D_MODEL // 2, D_MODEL)]
    dot = functools.partial(jnp.dot, preferred_element_type=F32)

    def first_stage(rows):
        x = x_ref[0, rows, :]
        u = (x * (1.0 + sc_ref[0]) + sh_ref[0]).astype(BF16)
        g = g_ref[0, rows, :].astype(BF16)
        o = o_ref_in[0, rows, :].astype(BF16)
        return x, [(dot(u, wm_ref[:, lo:hi]), dot(u, wm_ref[:, D_MODEL + lo:D_MODEL + hi]),
                    dot(g, wl_ref[:, lo:hi]), dot(g, wg_ref[:, lo:hi]), dot(o, wn_ref[:, lo:hi])) for lo, hi in halves]

    def second_stage(stage):
        mix = None
        for (lo, hi), (m_ssm, m_nsa, lin, gate, z_nsa) in zip(halves, stage):
            merged = _sigmoid(m_ssm) * (lin * _sigmoid(gate)) + _sigmoid(m_nsa) * z_nsa
            part = dot(merged.astype(BF16), wo_ref[lo:hi, :])
            mix = part if mix is None else mix + part
        return mix

    firsts = [first_stage(rows) for rows in row_blocks]
    mixes = [second_stage(stage) for _, stage in firsts]
    for rows, (x, _), mix in zip(row_blocks, firsts, mixes):
        y = DEEPNORM_ALPHA * x + (1.0 + gt_ref[0]) * mix
        y_ref[0, rows, :] = _layer_norm(y, lg_ref[...], lb_ref[...])


def _merge(x, g_ssm, o_nsa, scale1, shift1, gate1, w_m, w_lin, w_gate, w_nsa, w_out, ln_g, ln_b, *, tm):
    bsz, seq, _ = x.shape
    row = pl.BlockSpec((1, 1, D_MODEL), lambda b, i: (b, 0, 0))
    full = lambda a: pl.BlockSpec(a.shape, lambda b, i: (0, 0), pipeline_mode=pl.Buffered(1))
    return pl.pallas_call(
        _merge_kernel,
        grid=(bsz, seq // tm),
        in_specs=[
            pl.BlockSpec((1, tm, D_MODEL), lambda b, i: (b, i, 0)),
            pl.BlockSpec((1, tm, SSM_WIDTH), lambda b, i: (b, i, 0)),
            pl.BlockSpec((1, tm, ATTN_WIDTH), lambda b, i: (b, i, 0)),
            row, row, row,
            full(w_m), full(w_lin), full(w_gate), full(w_nsa), full(w_out), full(ln_g), full(ln_b),
        ],
        out_specs=pl.BlockSpec((1, tm, D_MODEL), lambda b, i: (b, i, 0)),
        out_shape=jax.ShapeDtypeStruct(x.shape, F32),
        compiler_params=_params(2),
        name="merge",
    )(x, g_ssm, o_nsa, scale1, shift1, gate1, w_m, w_lin, w_gate, w_nsa, w_out, ln_g, ln_b)


def _ffn_kernel(x_ref, sc_ref, sh_ref, gt_ref, wg_ref, wu_ref, wd_ref, lg_ref, lb_ref, y_ref, *, n_chunks):
    tm = x_ref.shape[1]
    row_blocks = [slice(r * (tm // 2), (r + 1) * (tm // 2)) for r in range(2)]
    tiles = FFN_HIDDEN // MXU_WIDTH
    edges = [MXU_WIDTH * ((tiles * c + n_chunks - 1) // n_chunks) for c in range(n_chunks + 1)]
    chunks = [slice(lo, hi) for lo, hi in zip(edges[:-1], edges[1:])]
    dot = functools.partial(jnp.dot, preferred_element_type=F32)

    def gate_up(rows):
        x = x_ref[0, rows, :]
        u = (x * (1.0 + sc_ref[0]) + sh_ref[0]).astype(BF16)
        return x, [(dot(u, wg_ref[:, cols]), dot(u, wu_ref[:, cols])) for cols in chunks]

    def down(products):
        ffn = None
        for cols, (a, b) in zip(chunks, products):
            part = dot(((a * _sigmoid(a)) * b).astype(BF16), wd_ref[cols, :])
            ffn = part if ffn is None else ffn + part
        return ffn

    firsts = [gate_up(rows) for rows in row_blocks]
    ffns = [down(products) for _, products in firsts]
    for rows, (x, _), ffn in zip(row_blocks, firsts, ffns):
        y = DEEPNORM_ALPHA * x + (1.0 + gt_ref[0]) * ffn
        y_ref[0, rows, :] = _layer_norm(y, lg_ref[...], lb_ref[...])


def _ffn(x, scale2, shift2, gate2, w_gate, w_up, w_down, ln_g, ln_b, *, tm, n_chunks):
    bsz, seq, _ = x.shape
    row = pl.BlockSpec((1, 1, D_MODEL), lambda b, i: (b, 0, 0))
    resident = lambda a: pl.BlockSpec(a.shape, lambda b, i: (0, 0), pipeline_mode=pl.Buffered(1))
    return pl.pallas_call(
        functools.partial(_ffn_kernel, n_chunks=n_chunks),
        grid=(bsz, seq // tm),
        in_specs=[
            pl.BlockSpec((1, tm, D_MODEL), lambda b, i: (b, i, 0)),
            row, row, row,
            resident(w_gate), resident(w_up), resident(w_down), resident(ln_g), resident(ln_b),
        ],
        out_specs=pl.BlockSpec((1, tm, D_MODEL), lambda b, i: (b, i, 0)),
        out_shape=jax.ShapeDtypeStruct(x.shape, F32),
        compiler_params=_params(2),
        name="ffn",
    )(x, scale2, shift2, gate2, w_gate, w_up, w_down, ln_g, ln_b)


def _layer(x, c, w_ada, b_ada, w_in, ssm_a_re, ssm_a_im, ssm_log_dt, ssm_b_re, ssm_b_im, ssm_c_re, ssm_c_im,
           ssm_d, w_glu_lin, w_glu_gate, cmp_pe_k, cmp_pe_v, w_cmp_k1, w_cmp_k2, w_cmp_v1, w_cmp_v2,
           w_nsa_proj, w_out, ln1_g, ln1_b, w_ffn_gate, w_ffn_up, w_ffn_down, ln2_g, ln2_b):
    bsz, seq, _ = x.shape
    n_chunks = seq // SSM_CHUNK
    tq = min(256, seq)
    tm = min(512, seq)

    mod = _ada_mod(c, w_ada, b_ada).reshape(6, bsz, 1, D_MODEL)
    shift1, scale1, gate1, shift2, scale2, gate2 = (mod[k] for k in range(6))

    w_nat, w_tr, w_mg = _split_w_in(w_in)
    u_ssm, kv_cmp, k_aug, qT, v_aug, bgT = _in_proj(x, scale1, shift1, w_nat, w_tr, tm=tm)

    mats = _ssm_matrices(ssm_a_re, ssm_a_im, ssm_log_dt, ssm_b_re, ssm_b_im, ssm_c_re, ssm_c_im, ssm_d)
    g_ssm = _ssm(u_ssm, mats)

    kc3, vcT = _compress(kv_cmp,
                        cmp_pe_k.reshape(1, CMP_BLOCK * HEAD_DIM), cmp_pe_v.reshape(1, CMP_BLOCK * HEAD_DIM),
                        w_cmp_k1.reshape(CMP_BLOCK * HEAD_DIM, -1), w_cmp_k2,
                        w_cmp_v1.reshape(CMP_BLOCK * HEAD_DIM, -1), w_cmp_v2)
    o_nsa = _nsa(qT, kc3, vcT, k_aug, v_aug, bgT, tq=tq)

    x1 = _merge(x, g_ssm, o_nsa, scale1, shift1, gate1, w_mg,
                w_glu_lin.astype(BF16), w_glu_gate.astype(BF16), w_nsa_proj.astype(BF16), w_out.astype(BF16),
                ln1_g.reshape(1, D_MODEL), ln1_b.reshape(1, D_MODEL), tm=tm)
    return _ffn(x1, scale2, shift2, gate2, w_ffn_gate.astype(BF16), w_ffn_up.astype(BF16),
                w_ffn_down.astype(BF16), ln2_g.reshape(1, D_MODEL), ln2_b.reshape(1, D_MODEL), tm=tm, n_chunks=2)


def kernel(x, c, w_ada, b_ada, w_in, ssm_a_re, ssm_a_im, ssm_log_dt, ssm_b_re, ssm_b_im, ssm_c_re, ssm_c_im,
           ssm_d, w_glu_lin, w_glu_gate, cmp_pe_k, cmp_pe_v, w_cmp_k1, w_cmp_k2, w_cmp_v1, w_cmp_v2,
           w_nsa_proj, w_out, ln1_g, ln1_b, w_ffn_gate, w_ffn_up, w_ffn_down, ln2_g, ln2_b):
    for l in range(w_ada.shape[0]):
        x = _layer(x, c, w_ada[l], b_ada[l], w_in[l], ssm_a_re[l], ssm_a_im[l], ssm_log_dt[l],
                   ssm_b_re[l], ssm_b_im[l], ssm_c_re[l], ssm_c_im[l], ssm_d[l],
                   w_glu_lin[l], w_glu_gate[l], cmp_pe_k[l], cmp_pe_v[l],
                   w_cmp_k1[l], w_cmp_k2[l], w_cmp_v1[l], w_cmp_v2[l], w_nsa_proj[l], w_out[l],
                   ln1_g[l], ln1_b[l], w_ffn_gate[l], w_ffn_up[l], w_ffn_down[l], ln2_g[l], ln2_b[l])
    return x
```

```python
import functools
import math

import jax
import jax.numpy as jnp
from jax import lax
from jax.experimental import pallas as pl
from jax.experimental.pallas import tpu as pltpu

F32 = jnp.float32
BF16 = jnp.bfloat16
HIGHEST = lax.Precision.HIGHEST

D_MODEL = 1024
SSM_WIDTH = D_MODEL // 2
SSM_GROUP_SIZE = 16
SSM_GROUPS = SSM_WIDTH // SSM_GROUP_SIZE
SSM_STATE = 64
SSM_CHUNK = 16
SSM_GROUP_TILE = 8
SSM_STATE_ROW_PAD = 8
N_HEADS = 8
HEAD_DIM = 64
N_KV_GROUPS = 2
HEADS_PER_GROUP = N_HEADS // N_KV_GROUPS
ATTN_WIDTH = N_HEADS * HEAD_DIM
KV_WIDTH = N_KV_GROUPS * HEAD_DIM
CMP_BLOCK = 32
CMP_STRIDE = 16
CMP_HIDDEN = HEAD_DIM
SEL_BLOCK = 64
SEL_TOP_K = 8
WINDOW = 256
N_NSA_BRANCHES = 3
GATE_ROWS = 16
FFN_HIDDEN = (8 * D_MODEL + 3 * 256 - 1) // (3 * 256) * 256
DEEPNORM_ALPHA = 2.0 ** 0.25
LN_EPS = 1e-5
MASK_VALUE = -1e30

VMEM_LIMIT = 56 * 1024 * 1024
MXU_WIDTH = 256
BF16_ROWS = 16

KEY_BLOCK = 256
SEL_SHIFT = SEL_BLOCK.bit_length() - 1
SEL_PER_KEY_BLOCK = KEY_BLOCK // SEL_BLOCK
KEY_AUG = 128
ALIBI_COL = HEAD_DIM
N_PIECES = 4
MASK_COL = HEAD_DIM + 16
QUERY_AUG = HEAD_DIM + 32
VAL_AUG = HEAD_DIM + 16
LOG2E = math.log2(math.e)
SCORES_AHEAD = 5

NAT_COLS = SSM_WIDTH + 2 * KV_WIDTH + 2 * N_KV_GROUPS * KEY_AUG
TR_ROWS = ATTN_WIDTH + 2 * N_KV_GROUPS * VAL_AUG + N_KV_GROUPS * GATE_ROWS


def _bf16_pieces(value, n):
    pieces = []
    rest = value
    for _ in range(n):
        mant, expo = math.frexp(rest)
        piece = math.ldexp(round(mant * 256.0) / 256.0, expo)
        pieces.append(piece)
        rest -= piece
    return pieces


def _sigmoid(x):
    return 1.0 / (1.0 + jnp.exp(-x))


def _gelu(x):
    c = math.sqrt(2.0 / math.pi)
    return 0.5 * x * (1.0 + jnp.tanh(c * (x + 0.044715 * (x * x * x))))


def _layer_norm(y, gain, bias):
    mu = jnp.mean(y, axis=-1, keepdims=True)
    d = y - mu
    var = jnp.mean(d * d, axis=-1, keepdims=True)
    return d * lax.rsqrt(var + LN_EPS) * gain + bias


def _params(n_axes):
    return pltpu.CompilerParams(dimension_semantics=("arbitrary",) * n_axes, vmem_limit_bytes=VMEM_LIMIT)


def _ada_kernel(c_ref, w_ref, b_ref, o_ref):
    c = c_ref[...]
    a = c * _sigmoid(c)
    o_ref[0] = jnp.dot(a, w_ref[...], precision=HIGHEST, preferred_element_type=F32) + b_ref[...]


def _ada_mod(c, w_ada, b_ada):
    bsz = c.shape[0]
    return pl.pallas_call(
        _ada_kernel,
        grid=(6,),
        in_specs=[
            pl.BlockSpec((bsz, D_MODEL), lambda j: (0, 0)),
            pl.BlockSpec((D_MODEL, D_MODEL), lambda j: (0, j)),
            pl.BlockSpec((1, D_MODEL), lambda j: (0, j)),
        ],
        out_specs=pl.BlockSpec((1, bsz, D_MODEL), lambda j: (j, 0, 0)),
        out_shape=jax.ShapeDtypeStruct((6, bsz, D_MODEL), F32),
        compiler_params=_params(1),
        name="ada_mod",
    )(c, w_ada, b_ada.reshape(1, 6 * D_MODEL))


def _in_proj_kernel(x_ref, sc_ref, sh_ref, wn_ref, wt_ref, ussm_ref, kc_ref, kp_ref, qT_ref, vp_ref, bgT_ref):
    i = pl.program_id(1)
    tm = x_ref.shape[1]
    u = (x_ref[0] * (1.0 + sc_ref[0]) + sh_ref[0]).astype(BF16)
    nat = jnp.dot(u, wn_ref[...], preferred_element_type=F32)
    ussm_ref[0] = nat[:, :SSM_WIDTH]
    for s in range(2):
        for g in range(N_KV_GROUPS):
            lo = SSM_WIDTH + s * KV_WIDTH + g * HEAD_DIM
            kc_ref[0, s, g] = nat[:, lo:lo + HEAD_DIM]
    pos = i * tm + lax.broadcasted_iota(jnp.int32, (tm, KEY_AUG), 0)
    col = lax.broadcasted_iota(jnp.int32, (tm, KEY_AUG), 1)
    blk = lax.shift_right_logical(pos, SEL_SHIFT)
    in_a = (col >= ALIBI_COL) & (col < ALIBI_COL + N_PIECES)
    in_b = (col >= ALIBI_COL + N_PIECES) & (col < ALIBI_COL + 2 * N_PIECES)
    hot = col == MASK_COL + (blk & (SEL_PER_KEY_BLOCK - 1))
    aux = jnp.where(in_a, blk * SEL_BLOCK, jnp.where(in_b, pos & (SEL_BLOCK - 1), jnp.where(hot, 1, 0))).astype(F32)
    for s in range(2):
        for g in range(N_KV_GROUPS):
            lo = SSM_WIDTH + 2 * KV_WIDTH + (s * N_KV_GROUPS + g) * KEY_AUG
            kp_ref[0, s, g] = (nat[:, lo:lo + KEY_AUG] + aux).astype(BF16)
    tr = lax.dot_general(wt_ref[...], u, (((1,), (1,)), ((), ())), preferred_element_type=F32)
    qT_ref[0] = tr[:ATTN_WIDTH] * (HEAD_DIM ** -0.5)
    ones_row = jnp.where(lax.broadcasted_iota(jnp.int32, (VAL_AUG, 1), 0) == HEAD_DIM, 1.0, 0.0)
    for s in range(2):
        for g in range(N_KV_GROUPS):
            lo = ATTN_WIDTH + (s * N_KV_GROUPS + g) * VAL_AUG
            v_aug = (tr[lo:lo + VAL_AUG] + ones_row).astype(BF16)
            for c in range(tm // KEY_BLOCK):
                vp_ref[0, s, g, c] = v_aug[:, c * KEY_BLOCK:(c + 1) * KEY_BLOCK]
    bgT_ref[0] = _sigmoid(tr[ATTN_WIDTH + 2 * N_KV_GROUPS * VAL_AUG:])


def _in_proj(x, scale1, shift1, w_nat, w_tr, *, tm):
    bsz, seq, _ = x.shape
    n_t = seq // tm
    n_kb = seq // KEY_BLOCK
    out_shapes = (
        jax.ShapeDtypeStruct((bsz, seq, SSM_WIDTH), F32),
        jax.ShapeDtypeStruct((bsz, 2, N_KV_GROUPS, seq, HEAD_DIM), F32),
        jax.ShapeDtypeStruct((bsz, 2, N_KV_GROUPS, seq, KEY_AUG), BF16),
        jax.ShapeDtypeStruct((bsz, ATTN_WIDTH, seq), F32),
        jax.ShapeDtypeStruct((bsz, 2, N_KV_GROUPS, n_kb, VAL_AUG, KEY_BLOCK), BF16),
        jax.ShapeDtypeStruct((bsz, N_KV_GROUPS * GATE_ROWS, seq), F32),
    )
    return pl.pallas_call(
        _in_proj_kernel,
        grid=(bsz, n_t),
        in_specs=[
            pl.BlockSpec((1, tm, D_MODEL), lambda b, i: (b, i, 0)),
            pl.BlockSpec((1, 1, D_MODEL), lambda b, i: (b, 0, 0)),
            pl.BlockSpec((1, 1, D_MODEL), lambda b, i: (b, 0, 0)),
            pl.BlockSpec((D_MODEL, NAT_COLS), lambda b, i: (0, 0)),
            pl.BlockSpec((TR_ROWS, D_MODEL), lambda b, i: (0, 0)),
        ],
        out_specs=(
            pl.BlockSpec((1, tm, SSM_WIDTH), lambda b, i: (b, i, 0)),
            pl.BlockSpec((1, 2, N_KV_GROUPS, tm, HEAD_DIM), lambda b, i: (b, 0, 0, i, 0)),
            pl.BlockSpec((1, 2, N_KV_GROUPS, tm, KEY_AUG), lambda b, i: (b, 0, 0, i, 0)),
            pl.BlockSpec((1, ATTN_WIDTH, tm), lambda b, i: (b, 0, i)),
            pl.BlockSpec((1, 2, N_KV_GROUPS, tm // KEY_BLOCK, VAL_AUG, KEY_BLOCK), lambda b, i: (b, 0, 0, i, 0, 0)),
            pl.BlockSpec((1, N_KV_GROUPS * GATE_ROWS, tm), lambda b, i: (b, 0, i)),
        ),
        out_shape=out_shapes,
        compiler_params=_params(2),
        name="in_proj",
    )(x, scale1, shift1, w_nat, w_tr)


def _split_w_in(w_in):
    o_q = SSM_WIDTH
    o_kv = o_q + ATTN_WIDTH
    o_bg = o_kv + 6 * KV_WIDTH
    o_mg = o_bg + N_NSA_BRANCHES * N_HEADS
    kv = [w_in[:, o_kv + s * KV_WIDTH:o_kv + (s + 1) * KV_WIDTH].reshape(D_MODEL, N_KV_GROUPS, HEAD_DIM)
          for s in range(6)]
    pad_k = lambda w: jnp.pad(w, ((0, 0), (0, 0), (0, KEY_AUG - HEAD_DIM))).reshape(D_MODEL, N_KV_GROUPS * KEY_AUG)
    pad_v = lambda w: jnp.pad(w, ((0, 0), (0, 0), (0, VAL_AUG - HEAD_DIM))).reshape(D_MODEL, N_KV_GROUPS * VAL_AUG)
    flat = lambda w: w.reshape(D_MODEL, KV_WIDTH)
    w_nat = jnp.concatenate([w_in[:, :o_q], flat(kv[0]), flat(kv[1]), pad_k(kv[2]), pad_k(kv[4])], axis=1)
    per_group = HEADS_PER_GROUP * N_NSA_BRANCHES
    bg = w_in[:, o_bg:o_mg].reshape(D_MODEL, N_KV_GROUPS, per_group)
    bg = jnp.pad(bg, ((0, 0), (0, 0), (0, GATE_ROWS - per_group))).reshape(D_MODEL, N_KV_GROUPS * GATE_ROWS)
    w_tr = jnp.concatenate([w_in[:, o_q:o_kv], pad_v(kv[3]), pad_v(kv[5]), bg], axis=1).T
    return w_nat.astype(BF16), w_tr.astype(BF16), w_in[:, o_mg:].astype(BF16)


def _ssm_prep_kernel(a_ref, logdt_ref, bT_ref, c_ref, d_ref, mT_ref, pT_ref, qT_ref, a16_ref):
    t_n = SSM_CHUNK
    c_n = SSM_GROUP_SIZE
    n_p = SSM_STATE
    rows = t_n * c_n
    dt = jnp.exp(logdt_ref[0])
    a_re = a_ref[0, 0]
    a_im = a_ref[0, 1]
    decay = jnp.exp(a_re * dt)
    ab_re = decay * jnp.cos(a_im * dt)
    ab_im = decay * jnp.sin(a_im * dt)
    n_re = ab_re - 1.0
    denom = a_re * a_re + a_im * a_im
    f_re = (n_re * a_re + ab_im * a_im) / denom
    f_im = (ab_im * a_re - n_re * a_im) / denom
    powers = [(jnp.ones_like(ab_re), jnp.zeros_like(ab_re))]
    for _ in range(t_n):
        p_re, p_im = powers[-1]
        powers.append((p_re * ab_re - p_im * ab_im, p_re * ab_im + p_im * ab_re))

    def per_token(first, step):
        picks = [powers[first + step * k] for k in range(t_n)]
        return (jnp.concatenate([jnp.broadcast_to(p[0], (c_n, 2 * n_p)) for p in picks], axis=0),
                jnp.concatenate([jnp.broadcast_to(p[1], (c_n, 2 * n_p)) for p in picks], axis=0))

    def cmul(x_re, x_im, y_re, y_im):
        return x_re * y_re - x_im * y_im, x_re * y_im + x_im * y_re

    bb_re, bb_im = cmul(f_re, f_im, bT_ref[0, 0], bT_ref[0, 1])
    bt_re = jnp.concatenate([bb_re] * t_n, axis=0)
    bt_im = jnp.concatenate([bb_im] * t_n, axis=0)
    ct_re = jnp.concatenate([c_ref[0, 0]] * t_n, axis=0)
    ct_im = jnp.concatenate([c_ref[0, 1]] * t_n, axis=0)

    p_re, p_im = cmul(*per_token(t_n - 1, -1), bt_re, bt_im)
    low_half = lax.broadcasted_iota(jnp.int32, (1, 2 * n_p), 1) < n_p
    pT_ref[0] = jnp.where(low_half, p_re, p_im).T.astype(BF16)

    q_re, q_im = cmul(ct_re, ct_im, *per_token(1, 1))
    qT_ref[0, 0] = q_re[:, :n_p].astype(BF16)
    qT_ref[0, 1] = (-q_im[:, :n_p]).astype(BF16)
    a16_ref[0] = jnp.concatenate([powers[t_n][0][:, :n_p], powers[t_n][1][:, :n_p]], axis=0)

    k_re, k_im = cmul(ct_re, ct_im, *per_token(0, 1))
    nt = (((1,), (1,)), ((), ()))
    strips = (lax.dot_general(k_re[:, :n_p], bt_re[:, :n_p], nt, precision=HIGHEST, preferred_element_type=F32)
              - lax.dot_general(k_im[:, :n_p], bt_im[:, :n_p], nt, precision=HIGHEST, preferred_element_type=F32))
    lane = lax.broadcasted_iota(jnp.int32, (1, rows), 1)
    s_lane = lax.shift_right_logical(lane, c_n.bit_length() - 1)
    ci_lane = lane & (c_n - 1)
    row = lax.broadcasted_iota(jnp.int32, (rows, 1), 0)
    t_row = lax.shift_right_logical(row, c_n.bit_length() - 1)
    co_row = row & (c_n - 1)
    blocks = []
    for t in range(t_n):
        acc = jnp.zeros((c_n, rows), F32)
        for lag in range(t + 1):
            acc = jnp.where(s_lane == t - lag, strips[lag * c_n:(lag + 1) * c_n, :], acc)
        blocks.append(acc)
    d_rows = jnp.concatenate([d_ref[0]] * t_n, axis=0)
    skip = jnp.where((t_row == s_lane) & (co_row == ci_lane), d_rows, 0.0)
    mT_ref[0] = (jnp.concatenate(blocks, axis=0) + skip).astype(BF16)


def _ssm_matrices(a_re, a_im, log_dt, b_re, b_im, c_re, c_im, d_skip):
    n_g, n_p = a_re.shape
    c_n = SSM_GROUP_SIZE
    rows = SSM_CHUNK * c_n
    twice = lambda x: jnp.concatenate([x, x], axis=-1).astype(F32)
    a = twice(jnp.stack([a_re, a_im], axis=1)).reshape(n_g, 2, 1, 2 * n_p)
    bT = twice(jnp.swapaxes(jnp.stack([b_re, b_im], axis=1), 2, 3))
    c = twice(jnp.stack([c_re, c_im], axis=1))
    spec = lambda shape: pl.BlockSpec((1,) + shape, lambda g: (g,) + (0,) * len(shape))
    mT, pT, qT, a16 = pl.pallas_call(
        _ssm_prep_kernel,
        grid=(n_g,),
        in_specs=[spec((2, 1, 2 * n_p)), spec((1, 1)), spec((2, c_n, 2 * n_p)), spec((2, c_n, 2 * n_p)),
                  spec((c_n, 1))],
        out_specs=(spec((rows, rows)), spec((2 * n_p, rows)), spec((2, rows, n_p)), spec((2, n_p))),
        out_shape=(
            jax.ShapeDtypeStruct((n_g, rows, rows), BF16),
            jax.ShapeDtypeStruct((n_g, 2 * n_p, rows), BF16),
            jax.ShapeDtypeStruct((n_g, 2, rows, n_p), BF16),
            jax.ShapeDtypeStruct((n_g, 2, n_p), F32),
        ),
        compiler_params=_params(1),
        name="ssm_prep",
    )(a, log_dt.astype(F32).reshape(n_g, 1, 1), bT, c, d_skip.astype(F32).reshape(n_g, c_n, 1))
    return mT, pT.reshape(n_g, 2, n_p, rows), qT, a16


def _ssm_kernel(u_ref, mT_ref, pT_ref, qT_ref, a16_ref, o_ref, ut_scr, yt_scr, sre, sim, *, n_chunks):
    n_g = SSM_GROUP_TILE
    t_n = SSM_CHUNK
    c_n = SSM_GROUP_SIZE

    for s in range(t_n):
        x_s = u_ref[0, pl.ds(s, n_chunks, stride=t_n), :]
        ut_scr[:, s * c_n:(s + 1) * c_n, :] = x_s.T.reshape(n_g, c_n, n_chunks).astype(BF16)

    pitch = n_chunks + SSM_STATE_ROW_PAD

    def group_rows(g):
        return pl.ds(g * pitch, n_chunks)

    def chunk_states(g, carry):
        ut = ut_scr[g]
        sre[group_rows(g), :] = jnp.dot(pT_ref[g, 0], ut, preferred_element_type=F32).T
        sim[group_rows(g), :] = jnp.dot(pT_ref[g, 1], ut, preferred_element_type=F32).T
        return carry

    for g in range(n_g):
        chunk_states(g, 0)

    a_r = a16_ref[pl.ds(0, n_g, stride=2), :]
    a_i = a16_ref[pl.ds(1, n_g, stride=2), :]

    def carry_states(c, h):
        h_r, h_i = h
        rows = pl.ds(c, n_g, stride=pitch)
        s_r = sre[rows, :]
        s_i = sim[rows, :]
        sre[rows, :] = h_r
        sim[rows, :] = h_i
        return a_r * h_r - a_i * h_i + s_r, a_r * h_i + a_i * h_r + s_i

    zero = jnp.zeros((n_g, SSM_STATE), F32)
    lax.fori_loop(0, n_chunks, carry_states, (zero, zero), unroll=4)

    nt = (((1,), (1,)), ((), ()))

    def outputs(g, carry):
        yt = jnp.dot(mT_ref[g], ut_scr[g], preferred_element_type=F32)
        yt = yt + lax.dot_general(qT_ref[g, 0], sre[group_rows(g), :].astype(BF16), nt, preferred_element_type=F32)
        yt = yt + lax.dot_general(qT_ref[g, 1], sim[group_rows(g), :].astype(BF16), nt, preferred_element_type=F32)
        yt_scr[g] = _gelu(yt)
        return carry

    for g in range(n_g):
        outputs(g, 0)

    for t in range(t_n):
        z = yt_scr[:, t * c_n:(t + 1) * c_n, :].reshape(n_g * c_n, n_chunks)
        o_ref[0, pl.ds(t, n_chunks, stride=t_n), :] = z.T


def _ssm(u_ssm, mats):
    bsz, seq, width = u_ssm.shape
    n_chunks = seq // SSM_CHUNK
    mT, pT, qT, a16 = mats
    gt = SSM_GROUP_TILE
    n_tiles = SSM_GROUPS // gt
    rows = SSM_CHUNK * SSM_GROUP_SIZE
    a16 = a16.reshape(2 * SSM_GROUPS, SSM_STATE)
    per_tile = lambda a: pl.BlockSpec((gt,) + a.shape[1:], lambda t, b: (t,) + (0,) * (a.ndim - 1))
    return pl.pallas_call(
        functools.partial(_ssm_kernel, n_chunks=n_chunks),
        grid=(n_tiles, bsz),
        in_specs=[
            pl.BlockSpec((1, seq, gt * SSM_GROUP_SIZE), lambda t, b: (b, 0, t)),
            per_tile(mT), per_tile(pT), per_tile(qT),
            pl.BlockSpec((2 * gt, SSM_STATE), lambda t, b: (t, 0)),
        ],
        out_specs=pl.BlockSpec((1, seq, gt * SSM_GROUP_SIZE), lambda t, b: (b, 0, t)),
        out_shape=jax.ShapeDtypeStruct(u_ssm.shape, F32),
        scratch_shapes=[
            pltpu.VMEM((gt, rows, n_chunks), BF16),
            pltpu.VMEM((gt, rows, n_chunks), F32),
            pltpu.VMEM((gt * (n_chunks + SSM_STATE_ROW_PAD), SSM_STATE), F32),
            pltpu.VMEM((gt * (n_chunks + SSM_STATE_ROW_PAD), SSM_STATE), F32),
        ],
        compiler_params=_params(2),
        name="ssm",
    )(u_ssm, mT, pT, qT, a16)


def _split3(x):
    hi = x.astype(BF16)
    lo = (x - hi.astype(F32)).astype(BF16)
    return jnp.concatenate([hi, lo, hi], axis=-1)


def _stack3(w):
    hi = w.astype(BF16)
    lo = (w - hi.astype(F32)).astype(BF16)
    return jnp.concatenate([hi, hi, lo], axis=0)


def _compress_kernel(kx_ref, vx_ref, pek_ref, pev_ref, wk1_ref, wk2_ref, wv1_ref, wv2_ref, kc_ref, vcT_ref):
    def mlp(x_ref, pe_ref, w1_ref, w2_ref):
        n = x_ref.shape[3] // CMP_STRIDE
        x = jnp.concatenate([x_ref[0, 0, 0, pl.ds(l, n, stride=CMP_STRIDE), :] for l in range(CMP_STRIDE)], axis=1)
        both = jnp.dot(_split3(x), w1_ref[...], preferred_element_type=F32)
        pe_both = jnp.dot(_split3(pe_ref[...]), w1_ref[...], preferred_element_type=F32)
        bias = pe_both[0:1, :CMP_HIDDEN] + pe_both[1:2, CMP_HIDDEN:]
        h = _gelu(both[:, :CMP_HIDDEN] + pltpu.roll(both[:, CMP_HIDDEN:], n - 1, 0) + bias)
        return jnp.dot(_split3(h), w2_ref[...], preferred_element_type=F32)

    kc = mlp(kx_ref, pek_ref, wk1_ref, wk2_ref)
    kc_hi = kc.astype(BF16)
    kc_lo = (kc - kc_hi.astype(F32)).astype(BF16)
    kc_ref[0, 0] = jnp.concatenate([kc_hi, kc_hi, kc_lo, jnp.zeros_like(kc_hi)], axis=1)
    vcT_ref[0, 0] = mlp(vx_ref, pev_ref, wv1_ref, wv2_ref).T.astype(BF16)


def _compress(kv, pe_k, pe_v, wk1, wk2, wv1, wv2):
    bsz, _, n_g, seq, width = kv.shape
    n_chunks = seq // CMP_STRIDE
    half = CMP_STRIDE * HEAD_DIM
    side_by_side = lambda w1: _stack3(jnp.concatenate([w1[:half], w1[half:]], axis=1))
    two_rows = lambda pe: jnp.pad(pe.reshape(2, half), ((0, BF16_ROWS - 2), (0, 0)))
    pe_k, pe_v = two_rows(pe_k), two_rows(pe_v)
    wk1, wv1, wk2, wv2 = side_by_side(wk1), side_by_side(wv1), _stack3(wk2), _stack3(wv2)
    blk = lambda s: pl.BlockSpec((1, 1, 1, seq, width), lambda b, g: (b, s, g, 0, 0))
    full = lambda a: pl.BlockSpec(a.shape, lambda b, g: (0, 0))
    return pl.pallas_call(
        _compress_kernel,
        grid=(bsz, n_g),
        in_specs=[blk(0), blk(1), full(pe_k), full(pe_v), full(wk1), full(wk2), full(wv1), full(wv2)],
        out_specs=(
            pl.BlockSpec((1, 1, n_chunks, 4 * HEAD_DIM), lambda b, g: (b, g, 0, 0)),
            pl.BlockSpec((1, 1, HEAD_DIM, n_chunks), lambda b, g: (b, g, 0, 0)),
        ),
        out_shape=(
            jax.ShapeDtypeStruct((bsz, n_g, n_chunks, 4 * HEAD_DIM), BF16),
            jax.ShapeDtypeStruct((bsz, n_g, HEAD_DIM, n_chunks), BF16),
        ),
        compiler_params=_params(2),
        name="compress",
    )(kv, kv, pe_k, pe_v, wk1, wk2, wv1, wv2)


def _nsa_kernel(qT_ref, kc_ref, vcT_ref, ks_ref, vs_ref, kw_ref, vw_ref, bg_ref, o_ref,
                negm_scr, qa_scr, m_scr, acc_scr, out_scr, used_smem, todo_smem, *, tq, n_cmp, n_sel):
    g = pl.program_id(1)
    i = pl.program_id(2)
    t_row = i * tq + lax.broadcasted_iota(jnp.int32, (1, tq), 1)
    slopes = [jnp.where(g == 0, 2.0 ** -(hh + 1), 2.0 ** -(HEADS_PER_GROUP + hh + 1)).astype(F32)
              for hh in range(HEADS_PER_GROUP)]

    def gate(hh, branch):
        r = hh * N_NSA_BRANCHES + branch
        return bg_ref[0, 0, r:r + 1, :]

    def head_rows(hh):
        return slice(hh * HEAD_DIM, (hh + 1) * HEAD_DIM)

    def head_lanes(hh):
        return slice(hh * tq, (hh + 1) * tq)

    kc3 = kc_ref[0, 0]
    vcT = vcT_ref[0, 0]
    cmp_end = lax.broadcasted_iota(jnp.int32, (n_cmp, 1), 0) * CMP_STRIDE + (CMP_BLOCK - 1)
    dist_c = (t_row - cmp_end).astype(F32)
    valid_c = dist_c >= 0.0
    piece = lax.broadcasted_iota(jnp.int32, (16, 1), 0)
    log2e_pieces = _bf16_pieces(LOG2E, N_PIECES)
    log2e_col = jnp.zeros((16, 1), F32)
    for k, value in enumerate(log2e_pieces):
        log2e_col = jnp.where((piece == k) | (piece == N_PIECES + k), value, log2e_col)
    for hh in range(HEADS_PER_GROUP):
        alibi = jnp.broadcast_to(log2e_col * slopes[hh], (16, tq)).astype(BF16)
        q_l2 = (qT_ref[0, head_rows(hh), :] * LOG2E).astype(BF16)
        qa_scr[:, head_lanes(hh)] = jnp.concatenate([q_l2, alibi], axis=0)

    def cmp_scores(hh):
        q_h = qT_ref[0, head_rows(hh), :]
        q_hi = q_h.astype(BF16)
        q_lo = (q_h - q_hi.astype(F32)).astype(BF16)
        q3 = jnp.concatenate([q_hi, q_lo, q_hi, jnp.zeros_like(q_hi)], axis=0)
        return jnp.dot(kc3, q3, preferred_element_type=F32)

    def cmp_attend(p_heads, hh, s):
        s = jnp.where(valid_c, s - slopes[hh] * dist_c, MASK_VALUE)
        m = jnp.max(s, axis=0, keepdims=True)
        p = jnp.where(valid_c, jnp.exp(s - m), 0.0)
        l = jnp.sum(p, axis=0, keepdims=True)
        p = p / jnp.where(l > 0.0, l, 1.0)
        o_c = jnp.dot(vcT, p.astype(BF16), preferred_element_type=F32)
        out_scr[head_rows(hh), :] = gate(hh, 0) * o_c
        p_heads.append(p)

    def select_blocks(p_heads, _):
        p_sum = sum(p_heads[1:], p_heads[0])
        blk = lax.broadcasted_iota(jnp.int32, (n_sel, n_cmp), 0)
        cmp_ix = lax.broadcasted_iota(jnp.int32, (n_sel, n_cmp), 1)
        ratio = SEL_BLOCK // CMP_STRIDE
        extra = CMP_BLOCK // CMP_STRIDE - 1
        overlap_t = jnp.where((cmp_ix >= ratio * blk - extra) & (cmp_ix <= ratio * blk + ratio - 1),
                              1.0, 0.0).astype(BF16)
        imp = jnp.zeros((n_sel, tq), F32)
        rest = p_sum
        for _ in range(3):
            part = rest.astype(BF16)
            imp = imp + jnp.dot(overlap_t, part, preferred_element_type=F32)
            rest = rest - part.astype(F32)

        j_col = lax.broadcasted_iota(jnp.int32, (n_sel, 1), 0)
        forced = (j_col == 0) | (j_col == lax.shift_right_logical(t_row, SEL_SHIFT))
        future = j_col * SEL_BLOCK > t_row
        score = jnp.where(forced, jnp.inf, jnp.where(future, -jnp.inf, imp))
        rank = jnp.zeros((n_sel, tq), F32)
        for jp in range(n_sel):
            other = score[jp:jp + 1, :]
            ahead = (other > score) | ((other == score) & (j_col > jp))
            rank = rank + jnp.where(ahead, 1.0, 0.0)
        neg = jnp.where(rank < float(min(SEL_TOP_K, n_sel)), 0.0, MASK_VALUE)
        pad_rows = jnp.zeros((BF16_ROWS - SEL_PER_KEY_BLOCK, tq), F32)
        for c in range(n_sel // SEL_PER_KEY_BLOCK):
            rows = neg[c * SEL_PER_KEY_BLOCK:(c + 1) * SEL_PER_KEY_BLOCK, :]
            negm_scr[c] = jnp.concatenate([rows, pad_rows], axis=0).astype(BF16)
            used_smem[c] = (jnp.max(rows) > 0.5 * MASK_VALUE).astype(jnp.int32)

    rel = (lax.broadcasted_iota(jnp.int32, (KEY_BLOCK, tq), 1)
           - lax.broadcasted_iota(jnp.int32, (KEY_BLOCK, tq), 0))
    zero_tail = jnp.zeros((KEY_AUG - QUERY_AUG, tq), BF16)
    no_mask = jnp.zeros((16, tq), BF16)

    sel_even, sel_odd, win_prev, win_diag = range(4)
    m_scr[...] = jnp.full(m_scr.shape, MASK_VALUE, F32)
    acc_scr[...] = jnp.zeros(acc_scr.shape, F32)

    def scores(job, hh):
        k_ref, _, kb, block_masked, _, _ = job
        k0 = pl.multiple_of(kb * KEY_BLOCK, KEY_BLOCK)
        k_aug = k_ref[0, 0, 0, pl.ds(k0, KEY_BLOCK), :]
        mask_tile = negm_scr[kb] if block_masked else no_mask
        q_aug = jnp.concatenate([qa_scr[:, head_lanes(hh)], mask_tile, zero_tail], axis=0)
        return jnp.dot(k_aug, q_aug, preferred_element_type=F32)

    def accumulate(job, hh, s):
        _, v_ref, kb, _, valid, slot = job
        cols = head_lanes(hh)
        if valid is not None:
            s = jnp.where(valid, s, MASK_VALUE)
        m_old = m_scr[slot, :, cols]
        m_new = jnp.maximum(m_old, jnp.max(s, axis=0, keepdims=True))
        p = jnp.exp2(s - jnp.maximum(m_new, 0.1 * MASK_VALUE))
        alpha = jnp.exp2(m_old - m_new)
        pv = jnp.dot(v_ref[0, 0, 0, kb], p.astype(BF16), preferred_element_type=F32)
        acc_scr[slot, :, cols] = alpha * acc_scr[slot, :, cols] + pv
        m_scr[slot, :, cols] = m_new

    def pipeline(units):
        issue = lambda unit: None if unit[0] is None else unit[0]()
        pending = [issue(unit) for unit in units[:SCORES_AHEAD]]
        for n, unit in enumerate(units):
            if n + SCORES_AHEAD < len(units):
                pending.append(issue(units[n + SCORES_AHEAD]))
            unit[1](pending.pop(0))

    def attention_units(jobs):
        return [(functools.partial(scores, job, hh), functools.partial(accumulate, job, hh))
                for job in jobs for hh in range(HEADS_PER_GROUP)]

    def finish(branch, slot_a, slot_b):
        m_a = m_scr[slot_a]
        m_b = m_scr[slot_b]
        m = jnp.maximum(m_a, m_b)
        w_a = jnp.exp2(m_a - m)
        w_b = jnp.exp2(m_b - m)
        for hh in range(HEADS_PER_GROUP):
            cols = head_lanes(hh)
            acc = acc_scr[slot_a, :, cols] * w_a[:, cols] + acc_scr[slot_b, :, cols] * w_b[:, cols]
            l = acc[HEAD_DIM:HEAD_DIM + 1, :]
            o_b = acc[:HEAD_DIM, :] / jnp.where(l > 0.0, l, 1.0)
            out_scr[head_rows(hh), :] = out_scr[head_rows(hh), :] + gate(hh, branch) * o_b

    prev = jnp.maximum(i - 1, 0)
    has_prev = jnp.broadcast_to(jnp.minimum(i, 1), (KEY_BLOCK, tq)) > 0
    causal = rel >= 0
    off_without_prev = jnp.where(i > 0, 0, WINDOW + KEY_BLOCK)
    dist_prev = rel + KEY_BLOCK + off_without_prev
    p_heads = []
    pipeline(
        [(functools.partial(cmp_scores, hh), functools.partial(cmp_attend, p_heads, hh))
         for hh in range(HEADS_PER_GROUP)]
        + [(None, functools.partial(select_blocks, p_heads))]
        + attention_units([(kw_ref, vw_ref, prev, False, dist_prev < WINDOW, win_prev),
                           (kw_ref, vw_ref, i, False, causal, win_diag),
                           (ks_ref, vs_ref, prev, True, has_prev, sel_even),
                           (ks_ref, vs_ref, i, True, causal, sel_odd)]))

    n_todo = jnp.int32(0)
    for c in range(used_smem.shape[0]):
        take = jnp.logical_and(c < i - 1, used_smem[c] > 0)

        @pl.when(take)
        def _(c=c, slot=n_todo):
            todo_smem[slot] = c

        n_todo = n_todo + take.astype(jnp.int32)
    todo_smem[n_todo] = 0

    def past_pair(j, carry):
        second_on = jnp.broadcast_to((2 * j + 1 < n_todo).astype(jnp.int32), (KEY_BLOCK, tq)) > 0
        pipeline(attention_units([(ks_ref, vs_ref, todo_smem[2 * j], True, None, sel_even),
                                  (ks_ref, vs_ref, todo_smem[2 * j + 1], True, second_on, sel_odd)]))
        return carry

    lax.fori_loop(0, lax.shift_right_logical(n_todo + 1, 1), past_pair, 0)
    finish(1, sel_even, sel_odd)
    finish(2, win_prev, win_diag)

    o_ref[0] = out_scr[...].T


def _nsa(qT, kc3, vcT, k_aug, v_aug, bgT, *, tq):
    bsz, _, seq = qT.shape
    assert tq == KEY_BLOCK == WINDOW and seq % tq == 0
    n_cmp = kc3.shape[2]
    n_sel = seq // SEL_BLOCK
    n_kb = seq // KEY_BLOCK
    gw = HEADS_PER_GROUP * HEAD_DIM
    bg4 = bgT.reshape(bsz, N_KV_GROUPS, GATE_ROWS, seq)
    k_spec = lambda s: pl.BlockSpec((1, 1, 1, seq, KEY_AUG), lambda b, g, i: (b, s, g, 0, 0))
    v_spec = lambda s: pl.BlockSpec((1, 1, 1, n_kb, VAL_AUG, KEY_BLOCK), lambda b, g, i: (b, s, g, 0, 0, 0))
    return pl.pallas_call(
        functools.partial(_nsa_kernel, tq=tq, n_cmp=n_cmp, n_sel=n_sel),
        grid=(bsz, N_KV_GROUPS, seq // tq),
        in_specs=[
            pl.BlockSpec((1, gw, tq), lambda b, g, i: (b, g, i)),
            pl.BlockSpec((1, 1, n_cmp, 4 * HEAD_DIM), lambda b, g, i: (b, g, 0, 0)),
            pl.BlockSpec((1, 1, HEAD_DIM, n_cmp), lambda b, g, i: (b, g, 0, 0)),
            k_spec(0), v_spec(0),
            k_spec(1), v_spec(1),
            pl.BlockSpec((1, 1, GATE_ROWS, tq), lambda b, g, i: (b, g, 0, i)),
        ],
        out_specs=pl.BlockSpec((1, tq, gw), lambda b, g, i: (b, i, g)),
        out_shape=jax.ShapeDtypeStruct((bsz, seq, ATTN_WIDTH), F32),
        scratch_shapes=[
            pltpu.VMEM((n_kb, 16, tq), BF16),
            pltpu.VMEM((HEAD_DIM + 16, HEADS_PER_GROUP * tq), BF16),
            pltpu.VMEM((4, 1, HEADS_PER_GROUP * tq), F32),
            pltpu.VMEM((4, VAL_AUG, HEADS_PER_GROUP * tq), F32),
            pltpu.VMEM((gw, tq), F32),
            pltpu.SMEM((n_kb,), jnp.int32),
            pltpu.SMEM((n_kb + 1,), jnp.int32),
        ],
        compiler_params=_params(3),
        name="nsa",
    )(qT, kc3, vcT, k_aug, v_aug, k_aug, v_aug, bg4)


def _merge_kernel(x_ref, g_ref, o_ref_in, sc_ref, sh_ref, gt_ref, wm_ref, wl_ref, wg_ref, wn_ref, wo_ref,
                  lg_ref, lb_ref, y_ref):
    tm = x_ref.shape[1]
    row_blocks = [slice(r * (tm // 2), (r + 1) * (tm // 2)) for r in range(2)]
    halves = [(0, D_MODEL // 2), (D_MODEL // 2, D_MODEL)]
    dot = functools.partial(jnp.dot, preferred_element_type=F32)

    def first_stage(rows):
        x = x_ref[0, rows, :]
        u = (x * (1.0 + sc_ref[0]) + sh_ref[0]).astype(BF16)
        g = g_ref[0, rows, :].astype(BF16)
        o = o_ref_in[0, rows, :].astype(BF16)
        return x, [(dot(u, wm_ref[:, lo:hi]), dot(u, wm_ref[:, D_MODEL + lo:D_MODEL + hi]),
                    dot(g, wl_ref[:, lo:hi]), dot(g, wg_ref[:, lo:hi]), dot(o, wn_ref[:, lo:hi])) for lo, hi in halves]

    def second_stage(stage):
        mix = None
        for (lo, hi), (m_ssm, m_nsa, lin, gate, z_nsa) in zip(halves, stage):
            merged = _sigmoid(m_ssm) * (lin * _sigmoid(gate)) + _sigmoid(m_nsa) * z_nsa
            part = dot(merged.astype(BF16), wo_ref[lo:hi, :])
            mix = part if mix is None else mix + part
        return mix

    firsts = [first_stage(rows) for rows in row_blocks]
    mixes = [second_stage(stage) for _, stage in firsts]
    for rows, (x, _), mix in zip(row_blocks, firsts, mixes):
        y = DEEPNORM_ALPHA * x + (1.0 + gt_ref[0]) * mix
        y_ref[0, rows, :] = _layer_norm(y, lg_ref[...], lb_ref[...])


def _merge(x, g_ssm, o_nsa, scale1, shift1, gate1, w_m, w_lin, w_gate, w_nsa, w_out, ln_g, ln_b, *, tm):
    bsz, seq, _ = x.shape
    row = pl.BlockSpec((1, 1, D_MODEL), lambda b, i: (b, 0, 0))
    full = lambda a: pl.BlockSpec(a.shape, lambda b, i: (0, 0), pipeline_mode=pl.Buffered(1))
    return pl.pallas_call(
        _merge_kernel,
        grid=(bsz, seq // tm),
        in_specs=[
            pl.BlockSpec((1, tm, D_MODEL), lambda b, i: (b, i, 0)),
            pl.BlockSpec((1, tm, SSM_WIDTH), lambda b, i: (b, i, 0)),
            pl.BlockSpec((1, tm, ATTN_WIDTH), lambda b, i: (b, i, 0)),
            row, row, row,
            full(w_m), full(w_lin), full(w_gate), full(w_nsa), full(w_out), full(ln_g), full(ln_b),
        ],
        out_specs=pl.BlockSpec((1, tm, D_MODEL), lambda b, i: (b, i, 0)),
        out_shape=jax.ShapeDtypeStruct(x.shape, F32),
        compiler_params=_params(2),
        name="merge",
    )(x, g_ssm, o_nsa, scale1, shift1, gate1, w_m, w_lin, w_gate, w_nsa, w_out, ln_g, ln_b)


def _ffn_kernel(x_ref, sc_ref, sh_ref, gt_ref, wg_ref, wu_ref, wd_ref, lg_ref, lb_ref, y_ref, *, n_chunks):
    tm = x_ref.shape[1]
    row_blocks = [slice(r * (tm // 2), (r + 1) * (tm // 2)) for r in range(2)]
    tiles = FFN_HIDDEN // MXU_WIDTH
    edges = [MXU_WIDTH * ((tiles * c + n_chunks - 1) // n_chunks) for c in range(n_chunks + 1)]
    chunks = [slice(lo, hi) for lo, hi in zip(edges[:-1], edges[1:])]
    dot = functools.partial(jnp.dot, preferred_element_type=F32)

    def gate_up(rows):
        x = x_ref[0, rows, :]
        u = (x * (1.0 + sc_ref[0]) + sh_ref[0]).astype(BF16)
        return x, [(dot(u, wg_ref[:, cols]), dot(u, wu_ref[:, cols])) for cols in chunks]

    def down(products):
        ffn = None
        for cols, (a, b) in zip(chunks, products):
            part = dot(((a * _sigmoid(a)) * b).astype(BF16), wd_ref[cols, :])
            ffn = part if ffn is None else ffn + part
        return ffn

    firsts = [gate_up(rows) for rows in row_blocks]
    ffns = [down(products) for _, products in firsts]
    for rows, (x, _), ffn in zip(row_blocks, firsts, ffns):
        y = DEEPNORM_ALPHA * x + (1.0 + gt_ref[0]) * ffn
        y_ref[0, rows, :] = _layer_norm(y, lg_ref[...], lb_ref[...])


def _ffn(x, scale2, shift2, gate2, w_gate, w_up, w_down, ln_g, ln_b, *, tm, n_chunks):
    bsz, seq, _ = x.shape
    row = pl.BlockSpec((1, 1, D_MODEL), lambda b, i: (b, 0, 0))
    resident = lambda a: pl.BlockSpec(a.shape, lambda b, i: (0, 0), pipeline_mode=pl.Buffered(1))
    return pl.pallas_call(
        functools.partial(_ffn_kernel, n_chunks=n_chunks),
        grid=(bsz, seq // tm),
        in_specs=[
            pl.BlockSpec((1, tm, D_MODEL), lambda b, i: (b, i, 0)),
            row, row, row,
            resident(w_gate), resident(w_up), resident(w_down), resident(ln_g), resident(ln_b),
        ],
        out_specs=pl.BlockSpec((1, tm, D_MODEL), lambda b, i: (b, i, 0)),
        out_shape=jax.ShapeDtypeStruct(x.shape, F32),
        compiler_params=_params(2),
        name="ffn",
    )(x, scale2, shift2, gate2, w_gate, w_up, w_down, ln_g, ln_b)


def _layer(x, c, w_ada, b_ada, w_in, ssm_a_re, ssm_a_im, ssm_log_dt, ssm_b_re, ssm_b_im, ssm_c_re, ssm_c_im,
           ssm_d, w_glu_lin, w_glu_gate, cmp_pe_k, cmp_pe_v, w_cmp_k1, w_cmp_k2, w_cmp_v1, w_cmp_v2,
           w_nsa_proj, w_out, ln1_g, ln1_b, w_ffn_gate, w_ffn_up, w_ffn_down, ln2_g, ln2_b):
    bsz, seq, _ = x.shape
    n_chunks = seq // SSM_CHUNK
    tq = min(256, seq)
    tm = min(512, seq)

    mod = _ada_mod(c, w_ada, b_ada).reshape(6, bsz, 1, D_MODEL)
    shift1, scale1, gate1, shift2, scale2, gate2 = (mod[k] for k in range(6))

    w_nat, w_tr, w_mg = _split_w_in(w_in)
    u_ssm, kv_cmp, k_aug, qT, v_aug, bgT = _in_proj(x, scale1, shift1, w_nat, w_tr, tm=tm)

    mats = _ssm_matrices(ssm_a_re, ssm_a_im, ssm_log_dt, ssm_b_re, ssm_b_im, ssm_c_re, ssm_c_im, ssm_d)
    g_ssm = _ssm(u_ssm, mats)

    kc3, vcT = _compress(kv_cmp,
                        cmp_pe_k.reshape(1, CMP_BLOCK * HEAD_DIM), cmp_pe_v.reshape(1, CMP_BLOCK * HEAD_DIM),
                        w_cmp_k1.reshape(CMP_BLOCK * HEAD_DIM, -1), w_cmp_k2,
                        w_cmp_v1.reshape(CMP_BLOCK * HEAD_DIM, -1), w_cmp_v2)
    o_nsa = _nsa(qT, kc3, vcT, k_aug, v_aug, bgT, tq=tq)

    x1 = _merge(x, g_ssm, o_nsa, scale1, shift1, gate1, w_mg,
                w_glu_lin.astype(BF16), w_glu_gate.astype(BF16), w_nsa_proj.astype(BF16), w_out.astype(BF16),
                ln1_g.reshape(1, D_MODEL), ln1_b.reshape(1, D_MODEL), tm=tm)
    return _ffn(x1, scale2, shift2, gate2, w_ffn_gate.astype(BF16), w_ffn_up.astype(BF16),
                w_ffn_down.astype(BF16), ln2_g.reshape(1, D_MODEL), ln2_b.reshape(1, D_MODEL), tm=tm, n_chunks=2)


def kernel(x, c, w_ada, b_ada, w_in, ssm_a_re, ssm_a_im, ssm_log_dt, ssm_b_re, ssm_b_im, ssm_c_re, ssm_c_im,
           ssm_d, w_glu_lin, w_glu_gate, cmp_pe_k, cmp_pe_v, w_cmp_k1, w_cmp_k2, w_cmp_v1, w_cmp_v2,
           w_nsa_proj, w_out, ln1_g, ln1_b, w_ffn_gate, w_ffn_up, w_ffn_down, ln2_g, ln2_b):
    for l in range(w_ada.shape[0]):
        x = _layer(x, c, w_ada[l], b_ada[l], w_in[l], ssm_a_re[l], ssm_a_im[l], ssm_log_dt[l],
                   ssm_b_re[l], ssm_b_im[l], ssm_c_re[l], ssm_c_im[l], ssm_d[l],
                   w_glu_lin[l], w_glu_gate[l], cmp_pe_k[l], cmp_pe_v[l],
                   w_cmp_k1[l], w_cmp_k2[l], w_cmp_v1[l], w_cmp_v2[l], w_nsa_proj[l], w_out[l],
                   ln1_g[l], ln1_b[l], w_ffn_gate[l], w_ffn_up[l], w_ffn_down[l], ln2_g[l], ln2_b[l])
    return x
```

```python
import functools
import math

import jax
import jax.numpy as jnp
from jax import lax
from jax.experimental import pallas as pl
from jax.experimental.pallas import tpu as pltpu

F32 = jnp.float32
BF16 = jnp.bfloat16
HIGHEST = lax.Precision.HIGHEST

D_MODEL = 1024
SSM_WIDTH = D_MODEL // 2
SSM_GROUP_SIZE = 16
SSM_GROUPS = SSM_WIDTH // SSM_GROUP_SIZE
SSM_STATE = 64
SSM_CHUNK = 16
SSM_GROUP_TILE = 8
SSM_STATE_ROW_PAD = 8
N_HEADS = 8
HEAD_DIM = 64
N_KV_GROUPS = 2
HEADS_PER_GROUP = N_HEADS // N_KV_GROUPS
ATTN_WIDTH = N_HEADS * HEAD_DIM
KV_WIDTH = N_KV_GROUPS * HEAD_DIM
CMP_BLOCK = 32
CMP_STRIDE = 16
CMP_HIDDEN = HEAD_DIM
SEL_BLOCK = 64
SEL_TOP_K = 8
WINDOW = 256
N_NSA_BRANCHES = 3
GATE_ROWS = 16
FFN_HIDDEN = (8 * D_MODEL + 3 * 256 - 1) // (3 * 256) * 256
DEEPNORM_ALPHA = 2.0 ** 0.25
LN_EPS = 1e-5
MASK_VALUE = -1e30

VMEM_LIMIT = 56 * 1024 * 1024
MXU_WIDTH = 256
BF16_ROWS = 16
ROW_BLOCK = 256

KEY_BLOCK = 256
SEL_SHIFT = SEL_BLOCK.bit_length() - 1
SEL_PER_KEY_BLOCK = KEY_BLOCK // SEL_BLOCK
KEY_AUG = 128
ALIBI_COL = HEAD_DIM
N_PIECES = 4
MASK_COL = HEAD_DIM + 16
QUERY_AUG = HEAD_DIM + 32
VAL_AUG = HEAD_DIM + 16
LOG2E = math.log2(math.e)
SCORES_AHEAD = 5

NAT_COLS = SSM_WIDTH + 4 * KV_WIDTH
TR_ROWS = ATTN_WIDTH + 2 * KV_WIDTH + N_KV_GROUPS * GATE_ROWS


def _bf16_pieces(value, n):
    pieces = []
    rest = value
    for _ in range(n):
        mant, expo = math.frexp(rest)
        piece = math.ldexp(round(mant * 256.0) / 256.0, expo)
        pieces.append(piece)
        rest -= piece
    return pieces


def _sigmoid(x):
    return 1.0 / (1.0 + jnp.exp(-x))


def _gelu(x):
    c = math.sqrt(2.0 / math.pi)
    return 0.5 * x * (1.0 + jnp.tanh(c * (x + 0.044715 * (x * x * x))))


def _layer_norm(y, gain, bias):
    mu = jnp.mean(y, axis=-1, keepdims=True)
    d = y - mu
    var = jnp.mean(d * d, axis=-1, keepdims=True)
    return d * lax.rsqrt(var + LN_EPS) * gain + bias


def _params(n_axes):
    return pltpu.CompilerParams(dimension_semantics=("arbitrary",) * n_axes, vmem_limit_bytes=VMEM_LIMIT)


def _ada_kernel(c_ref, w_ref, b_ref, o_ref):
    c = c_ref[...]
    a = c * _sigmoid(c)
    w = w_ref[...].astype(BF16)
    a_hi = a.astype(BF16)
    a_lo = (a - a_hi.astype(F32)).astype(BF16)
    mod = (jnp.dot(a_hi, w, preferred_element_type=F32) + jnp.dot(a_lo, w, preferred_element_type=F32)) + b_ref[...]
    for k in range(o_ref.shape[0]):
        o_ref[k] = mod[:, k * D_MODEL:(k + 1) * D_MODEL]


def _ada_mod(c, w_ada, b_ada):
    bsz = c.shape[0]
    per_step = 2
    return pl.pallas_call(
        _ada_kernel,
        grid=(6 // per_step,),
        in_specs=[
            pl.BlockSpec((bsz, D_MODEL), lambda j: (0, 0)),
            pl.BlockSpec((D_MODEL, per_step * D_MODEL), lambda j: (0, j)),
            pl.BlockSpec((1, per_step * D_MODEL), lambda j: (0, j)),
        ],
        out_specs=pl.BlockSpec((per_step, bsz, D_MODEL), lambda j: (j, 0, 0)),
        out_shape=jax.ShapeDtypeStruct((6, bsz, D_MODEL), F32),
        compiler_params=_params(1),
        name="ada_mod",
    )(c, w_ada, b_ada.reshape(1, 6 * D_MODEL))


def _in_proj_kernel(x_ref, sc_ref, sh_ref, wn_ref, wt_ref, ussm_ref, kc_ref, kp_ref, qT_ref, vp_ref, bgT_ref):
    i = pl.program_id(1)
    tm = x_ref.shape[1]
    u = (x_ref[0] * (1.0 + sc_ref[0]) + sh_ref[0]).astype(BF16)
    nat = jnp.dot(u, wn_ref[...], preferred_element_type=F32)
    ussm_ref[0] = nat[:, :SSM_WIDTH]
    for s in range(2):
        for g in range(N_KV_GROUPS):
            lo = SSM_WIDTH + s * KV_WIDTH + g * HEAD_DIM
            kc_ref[0, s, g] = nat[:, lo:lo + HEAD_DIM]
    pos = i * tm + lax.broadcasted_iota(jnp.int32, (tm, KEY_AUG), 0)
    col = lax.broadcasted_iota(jnp.int32, (tm, KEY_AUG), 1)
    blk = lax.shift_right_logical(pos, SEL_SHIFT)
    in_a = (col >= ALIBI_COL) & (col < ALIBI_COL + N_PIECES)
    in_b = (col >= ALIBI_COL + N_PIECES) & (col < ALIBI_COL + 2 * N_PIECES)
    hot = col == MASK_COL + (blk & (SEL_PER_KEY_BLOCK - 1))
    aux = jnp.where(in_a, blk * SEL_BLOCK, jnp.where(in_b, pos & (SEL_BLOCK - 1), jnp.where(hot, 1, 0))).astype(F32)
    head_lanes = col < HEAD_DIM
    for s in range(2):
        lo = SSM_WIDTH + (2 + s) * KV_WIDTH
        both = nat[:, lo:lo + KV_WIDTH]
        for g in range(N_KV_GROUPS):
            keys = both if g == 0 else pltpu.roll(both, HEAD_DIM, 1)
            kp_ref[0, s, g] = jnp.where(head_lanes, keys, aux).astype(BF16)
    tr = lax.dot_general(wt_ref[...], u, (((1,), (1,)), ((), ())), preferred_element_type=F32)
    qT_ref[0] = tr[:ATTN_WIDTH] * (HEAD_DIM ** -0.5)
    ones_rows = jnp.where(lax.broadcasted_iota(jnp.int32, (VAL_AUG - HEAD_DIM, tm), 0) == 0, 1.0, 0.0)
    for s in range(2):
        for g in range(N_KV_GROUPS):
            lo = ATTN_WIDTH + (s * N_KV_GROUPS + g) * HEAD_DIM
            v_aug = jnp.concatenate([tr[lo:lo + HEAD_DIM], ones_rows], axis=0).astype(BF16)
            for c in range(tm // KEY_BLOCK):
                vp_ref[0, s, g, c] = v_aug[:, c * KEY_BLOCK:(c + 1) * KEY_BLOCK]
    bgT_ref[0] = _sigmoid(tr[ATTN_WIDTH + 2 * KV_WIDTH:])


def _in_proj(x, scale1, shift1, w_nat, w_tr, *, tm):
    bsz, seq, _ = x.shape
    n_t = seq // tm
    n_kb = seq // KEY_BLOCK
    out_shapes = (
        jax.ShapeDtypeStruct((bsz, seq, SSM_WIDTH), F32),
        jax.ShapeDtypeStruct((bsz, 2, N_KV_GROUPS, seq, HEAD_DIM), F32),
        jax.ShapeDtypeStruct((bsz, 2, N_KV_GROUPS, seq, KEY_AUG), BF16),
        jax.ShapeDtypeStruct((bsz, ATTN_WIDTH, seq), F32),
        jax.ShapeDtypeStruct((bsz, 2, N_KV_GROUPS, n_kb, VAL_AUG, KEY_BLOCK), BF16),
        jax.ShapeDtypeStruct((bsz, N_KV_GROUPS * GATE_ROWS, seq), F32),
    )
    return pl.pallas_call(
        _in_proj_kernel,
        grid=(bsz, n_t),
        in_specs=[
            pl.BlockSpec((1, tm, D_MODEL), lambda b, i: (b, i, 0)),
            pl.BlockSpec((1, 1, D_MODEL), lambda b, i: (b, 0, 0)),
            pl.BlockSpec((1, 1, D_MODEL), lambda b, i: (b, 0, 0)),
            pl.BlockSpec((D_MODEL, NAT_COLS), lambda b, i: (0, 0)),
            pl.BlockSpec((TR_ROWS, D_MODEL), lambda b, i: (0, 0)),
        ],
        out_specs=(
            pl.BlockSpec((1, tm, SSM_WIDTH), lambda b, i: (b, i, 0)),
            pl.BlockSpec((1, 2, N_KV_GROUPS, tm, HEAD_DIM), lambda b, i: (b, 0, 0, i, 0)),
            pl.BlockSpec((1, 2, N_KV_GROUPS, tm, KEY_AUG), lambda b, i: (b, 0, 0, i, 0)),
            pl.BlockSpec((1, ATTN_WIDTH, tm), lambda b, i: (b, 0, i)),
            pl.BlockSpec((1, 2, N_KV_GROUPS, tm // KEY_BLOCK, VAL_AUG, KEY_BLOCK), lambda b, i: (b, 0, 0, i, 0, 0)),
            pl.BlockSpec((1, N_KV_GROUPS * GATE_ROWS, tm), lambda b, i: (b, 0, i)),
        ),
        out_shape=out_shapes,
        compiler_params=_params(2),
        name="in_proj",
    )(x, scale1, shift1, w_nat, w_tr)


def _split_w_in(w_in):
    o_q = SSM_WIDTH
    o_kv = o_q + ATTN_WIDTH
    o_bg = o_kv + 6 * KV_WIDTH
    o_mg = o_bg + N_NSA_BRANCHES * N_HEADS
    kv = [w_in[:, o_kv + s * KV_WIDTH:o_kv + (s + 1) * KV_WIDTH] for s in range(6)]
    w_nat = jnp.concatenate([w_in[:, :o_q], kv[0], kv[1], kv[2], kv[4]], axis=1)
    per_group = HEADS_PER_GROUP * N_NSA_BRANCHES
    bg = w_in[:, o_bg:o_mg].reshape(D_MODEL, N_KV_GROUPS, per_group)
    bg = jnp.pad(bg, ((0, 0), (0, 0), (0, GATE_ROWS - per_group))).reshape(D_MODEL, N_KV_GROUPS * GATE_ROWS)
    w_tr = jnp.concatenate([w_in[:, o_q:o_kv], kv[3], kv[5], bg], axis=1).T
    return w_nat.astype(BF16), w_tr.astype(BF16), w_in[:, o_mg:].astype(BF16)


def _ssm_prep_kernel(a_ref, logdt_ref, bT_ref, c_ref, d_ref, mT_ref, pT_ref, qT_ref, a16_ref):
    t_n = SSM_CHUNK
    c_n = SSM_GROUP_SIZE
    n_p = SSM_STATE
    rows = t_n * c_n
    dt = jnp.exp(logdt_ref[0])
    a_re = a_ref[0, 0]
    a_im = a_ref[0, 1]
    decay = jnp.exp(a_re * dt)
    ab_re = decay * jnp.cos(a_im * dt)
    ab_im = decay * jnp.sin(a_im * dt)
    n_re = ab_re - 1.0
    denom = a_re * a_re + a_im * a_im
    f_re = (n_re * a_re + ab_im * a_im) / denom
    f_im = (ab_im * a_re - n_re * a_im) / denom
    powers = [(jnp.ones_like(ab_re), jnp.zeros_like(ab_re))]
    for _ in range(t_n):
        p_re, p_im = powers[-1]
        powers.append((p_re * ab_re - p_im * ab_im, p_re * ab_im + p_im * ab_re))

    def per_token(first, step):
        picks = [powers[first + step * k] for k in range(t_n)]
        return (jnp.concatenate([jnp.broadcast_to(p[0], (c_n, 2 * n_p)) for p in picks], axis=0),
                jnp.concatenate([jnp.broadcast_to(p[1], (c_n, 2 * n_p)) for p in picks], axis=0))

    def cmul(x_re, x_im, y_re, y_im):
        return x_re * y_re - x_im * y_im, x_re * y_im + x_im * y_re

    bb_re, bb_im = cmul(f_re, f_im, bT_ref[0, 0], bT_ref[0, 1])
    bt_re = jnp.concatenate([bb_re] * t_n, axis=0)
    bt_im = jnp.concatenate([bb_im] * t_n, axis=0)
    ct_re = jnp.concatenate([c_ref[0, 0]] * t_n, axis=0)
    ct_im = jnp.concatenate([c_ref[0, 1]] * t_n, axis=0)

    p_re, p_im = cmul(*per_token(t_n - 1, -1), bt_re, bt_im)
    low_half = lax.broadcasted_iota(jnp.int32, (1, 2 * n_p), 1) < n_p
    pT_ref[0] = jnp.where(low_half, p_re, p_im).T.astype(BF16)

    q_re, q_im = cmul(ct_re, ct_im, *per_token(1, 1))
    qT_ref[0, 0] = q_re[:, :n_p].astype(BF16)
    qT_ref[0, 1] = (-q_im[:, :n_p]).astype(BF16)
    a16_ref[0] = jnp.concatenate([powers[t_n][0][:, :n_p], powers[t_n][1][:, :n_p]], axis=0)

    k_re, k_im = cmul(ct_re, ct_im, *per_token(0, 1))
    nt = (((1,), (1,)), ((), ()))
    strips = (lax.dot_general(_split3(k_re[:, :n_p]), _stack3(bt_re[:, :n_p], axis=1), nt, preferred_element_type=F32)
              - lax.dot_general(_split3(k_im[:, :n_p]), _stack3(bt_im[:, :n_p], axis=1), nt,
                                preferred_element_type=F32))
    lane = lax.broadcasted_iota(jnp.int32, (1, rows), 1)
    s_lane = lax.shift_right_logical(lane, c_n.bit_length() - 1)
    ci_lane = lane & (c_n - 1)
    row = lax.broadcasted_iota(jnp.int32, (rows, 1), 0)
    t_row = lax.shift_right_logical(row, c_n.bit_length() - 1)
    co_row = row & (c_n - 1)
    blocks = []
    for t in range(t_n):
        acc = jnp.zeros((c_n, rows), F32)
        for lag in range(t + 1):
            acc = jnp.where(s_lane == t - lag, strips[lag * c_n:(lag + 1) * c_n, :], acc)
        blocks.append(acc)
    d_rows = jnp.concatenate([d_ref[0]] * t_n, axis=0)
    skip = jnp.where((t_row == s_lane) & (co_row == ci_lane), d_rows, 0.0)
    mT_ref[0] = (jnp.concatenate(blocks, axis=0) + skip).astype(BF16)


def _ssm_matrices(a_re, a_im, log_dt, b_re, b_im, c_re, c_im, d_skip):
    n_g, n_p = a_re.shape
    c_n = SSM_GROUP_SIZE
    rows = SSM_CHUNK * c_n
    twice = lambda x: jnp.concatenate([x, x], axis=-1).astype(F32)
    a = twice(jnp.stack([a_re, a_im], axis=1)).reshape(n_g, 2, 1, 2 * n_p)
    bT = twice(jnp.swapaxes(jnp.stack([b_re, b_im], axis=1), 2, 3))
    c = twice(jnp.stack([c_re, c_im], axis=1))
    spec = lambda shape: pl.BlockSpec((1,) + shape, lambda g: (g,) + (0,) * len(shape))
    mT, pT, qT, a16 = pl.pallas_call(
        _ssm_prep_kernel,
        grid=(n_g,),
        in_specs=[spec((2, 1, 2 * n_p)), spec((1, 1)), spec((2, c_n, 2 * n_p)), spec((2, c_n, 2 * n_p)),
                  spec((c_n, 1))],
        out_specs=(spec((rows, rows)), spec((2 * n_p, rows)), spec((2, rows, n_p)), spec((2, n_p))),
        out_shape=(
            jax.ShapeDtypeStruct((n_g, rows, rows), BF16),
            jax.ShapeDtypeStruct((n_g, 2 * n_p, rows), BF16),
            jax.ShapeDtypeStruct((n_g, 2, rows, n_p), BF16),
            jax.ShapeDtypeStruct((n_g, 2, n_p), F32),
        ),
        compiler_params=_params(1),
        name="ssm_prep",
    )(a, log_dt.astype(F32).reshape(n_g, 1, 1), bT, c, d_skip.astype(F32).reshape(n_g, c_n, 1))
    return mT, pT.reshape(n_g, 2, n_p, rows), qT, a16


def _ssm_kernel(u_ref, mT_ref, pT_ref, qT_ref, a16_ref, o_ref, ut_scr, yt_scr, sre, sim, *, n_chunks):
    n_g = SSM_GROUP_TILE
    t_n = SSM_CHUNK
    c_n = SSM_GROUP_SIZE

    for s in range(t_n):
        x_s = u_ref[0, pl.ds(s, n_chunks, stride=t_n), :]
        ut_scr[:, s * c_n:(s + 1) * c_n, :] = x_s.T.reshape(n_g, c_n, n_chunks).astype(BF16)

    pitch = n_chunks + SSM_STATE_ROW_PAD

    def group_rows(g):
        return pl.ds(g * pitch, n_chunks)

    def chunk_states(g, carry):
        ut = ut_scr[g]
        sre[group_rows(g), :] = jnp.dot(pT_ref[g, 0], ut, preferred_element_type=F32).T
        sim[group_rows(g), :] = jnp.dot(pT_ref[g, 1], ut, preferred_element_type=F32).T
        return carry

    for g in range(n_g):
        chunk_states(g, 0)

    a_r = a16_ref[pl.ds(0, n_g, stride=2), :]
    a_i = a16_ref[pl.ds(1, n_g, stride=2), :]

    def carry_states(c, h):
        h_r, h_i = h
        rows = pl.ds(c, n_g, stride=pitch)
        s_r = sre[rows, :]
        s_i = sim[rows, :]
        sre[rows, :] = h_r
        sim[rows, :] = h_i
        return a_r * h_r - a_i * h_i + s_r, a_r * h_i + a_i * h_r + s_i

    zero = jnp.zeros((n_g, SSM_STATE), F32)
    lax.fori_loop(0, n_chunks, carry_states, (zero, zero), unroll=4)

    nt = (((1,), (1,)), ((), ()))

    def outputs(g, carry):
        yt = jnp.dot(mT_ref[g], ut_scr[g], preferred_element_type=F32)
        yt = yt + lax.dot_general(qT_ref[g, 0], sre[group_rows(g), :].astype(BF16), nt, preferred_element_type=F32)
        yt = yt + lax.dot_general(qT_ref[g, 1], sim[group_rows(g), :].astype(BF16), nt, preferred_element_type=F32)
        yt_scr[g] = _gelu(yt)
        return carry

    for g in range(n_g):
        outputs(g, 0)

    for t in range(t_n):
        z = yt_scr[:, t * c_n:(t + 1) * c_n, :].reshape(n_g * c_n, n_chunks)
        o_ref[0, pl.ds(t, n_chunks, stride=t_n), :] = z.T


def _ssm(u_ssm, mats):
    bsz, seq, width = u_ssm.shape
    n_chunks = seq // SSM_CHUNK
    mT, pT, qT, a16 = mats
    gt = SSM_GROUP_TILE
    n_tiles = SSM_GROUPS // gt
    rows = SSM_CHUNK * SSM_GROUP_SIZE
    a16 = a16.reshape(2 * SSM_GROUPS, SSM_STATE)
    per_tile = lambda a: pl.BlockSpec((gt,) + a.shape[1:], lambda t, b: (t,) + (0,) * (a.ndim - 1))
    return pl.pallas_call(
        functools.partial(_ssm_kernel, n_chunks=n_chunks),
        grid=(n_tiles, bsz),
        in_specs=[
            pl.BlockSpec((1, seq, gt * SSM_GROUP_SIZE), lambda t, b: (b, 0, t)),
            per_tile(mT), per_tile(pT), per_tile(qT),
            pl.BlockSpec((2 * gt, SSM_STATE), lambda t, b: (t, 0)),
        ],
        out_specs=pl.BlockSpec((1, seq, gt * SSM_GROUP_SIZE), lambda t, b: (b, 0, t)),
        out_shape=jax.ShapeDtypeStruct(u_ssm.shape, F32),
        scratch_shapes=[
            pltpu.VMEM((gt, rows, n_chunks), BF16),
            pltpu.VMEM((gt, rows, n_chunks), F32),
            pltpu.VMEM((gt * (n_chunks + SSM_STATE_ROW_PAD), SSM_STATE), F32),
            pltpu.VMEM((gt * (n_chunks + SSM_STATE_ROW_PAD), SSM_STATE), F32),
        ],
        compiler_params=_params(2),
        name="ssm",
    )(u_ssm, mT, pT, qT, a16)


def _split3(x):
    hi = x.astype(BF16)
    lo = (x - hi.astype(F32)).astype(BF16)
    return jnp.concatenate([hi, lo, hi], axis=-1)


def _stack3(w, axis=0):
    hi = w.astype(BF16)
    lo = (w - hi.astype(F32)).astype(BF16)
    return jnp.concatenate([hi, hi, lo], axis=axis)


def _compress_kernel(kx_ref, vx_ref, pek_ref, pev_ref, wk1_ref, wk2_ref, wv1_ref, wv2_ref, kc_ref, vcT_ref):
    def mlp(x_ref, pe_ref, w1_ref, w2_ref):
        n = x_ref.shape[3] // CMP_STRIDE
        x = jnp.concatenate([x_ref[0, 0, 0, pl.ds(l, n, stride=CMP_STRIDE), :] for l in range(CMP_STRIDE)], axis=1)
        both = jnp.dot(_split3(x), w1_ref[...], preferred_element_type=F32)
        pe_both = jnp.dot(_split3(pe_ref[...]), w1_ref[...], preferred_element_type=F32)
        bias = pe_both[0:1, :CMP_HIDDEN] + pe_both[1:2, CMP_HIDDEN:]
        h = _gelu(both[:, :CMP_HIDDEN] + pltpu.roll(both[:, CMP_HIDDEN:], n - 1, 0) + bias)
        return jnp.dot(_split3(h), w2_ref[...], preferred_element_type=F32)

    kc = mlp(kx_ref, pek_ref, wk1_ref, wk2_ref)
    kc_hi = kc.astype(BF16)
    kc_lo = (kc - kc_hi.astype(F32)).astype(BF16)
    kc_ref[0, 0] = jnp.concatenate([kc_hi, kc_hi, kc_lo, jnp.zeros_like(kc_hi)], axis=1)
    vcT_ref[0, 0] = mlp(vx_ref, pev_ref, wv1_ref, wv2_ref).T.astype(BF16)


def _compress(kv, pe_k, pe_v, wk1, wk2, wv1, wv2):
    bsz, _, n_g, seq, width = kv.shape
    n_chunks = seq // CMP_STRIDE
    half = CMP_STRIDE * HEAD_DIM
    side_by_side = lambda w1: _stack3(jnp.concatenate([w1[:half], w1[half:]], axis=1))
    two_rows = lambda pe: jnp.pad(pe.reshape(2, half), ((0, BF16_ROWS - 2), (0, 0)))
    pe_k, pe_v = two_rows(pe_k), two_rows(pe_v)
    wk1, wv1, wk2, wv2 = side_by_side(wk1), side_by_side(wv1), _stack3(wk2), _stack3(wv2)
    blk = lambda s: pl.BlockSpec((1, 1, 1, seq, width), lambda b, g: (b, s, g, 0, 0))
    full = lambda a: pl.BlockSpec(a.shape, lambda b, g: (0, 0))
    return pl.pallas_call(
        _compress_kernel,
        grid=(bsz, n_g),
        in_specs=[blk(0), blk(1), full(pe_k), full(pe_v), full(wk1), full(wk2), full(wv1), full(wv2)],
        out_specs=(
            pl.BlockSpec((1, 1, n_chunks, 4 * HEAD_DIM), lambda b, g: (b, g, 0, 0)),
            pl.BlockSpec((1, 1, HEAD_DIM, n_chunks), lambda b, g: (b, g, 0, 0)),
        ),
        out_shape=(
            jax.ShapeDtypeStruct((bsz, n_g, n_chunks, 4 * HEAD_DIM), BF16),
            jax.ShapeDtypeStruct((bsz, n_g, HEAD_DIM, n_chunks), BF16),
        ),
        compiler_params=_params(2),
        name="compress",
    )(kv, kv, pe_k, pe_v, wk1, wk2, wv1, wv2)


def _nsa_kernel(qT_ref, kc_ref, vcT_ref, ks_ref, vs_ref, kw_ref, vw_ref, bg_ref, o_ref,
                negm_scr, qa_scr, m_scr, acc_scr, out_scr, used_smem, todo_smem, *, tq, n_cmp, n_sel):
    g = pl.program_id(1)
    i = pl.program_id(2)
    t_row = i * tq + lax.broadcasted_iota(jnp.int32, (1, tq), 1)
    slopes = [jnp.where(g == 0, 2.0 ** -(hh + 1), 2.0 ** -(HEADS_PER_GROUP + hh + 1)).astype(F32)
              for hh in range(HEADS_PER_GROUP)]

    def gate(hh, branch):
        r = hh * N_NSA_BRANCHES + branch
        return bg_ref[0, 0, r:r + 1, :]

    def head_rows(hh):
        return slice(hh * HEAD_DIM, (hh + 1) * HEAD_DIM)

    def head_lanes(hh):
        return slice(hh * tq, (hh + 1) * tq)

    kc3 = kc_ref[0, 0]
    vcT = vcT_ref[0, 0]
    cmp_end = lax.broadcasted_iota(jnp.int32, (n_cmp, 1), 0) * CMP_STRIDE + (CMP_BLOCK - 1)
    dist_c = (t_row - cmp_end).astype(F32)
    valid_c = dist_c >= 0.0
    piece = lax.broadcasted_iota(jnp.int32, (16, 1), 0)
    log2e_pieces = _bf16_pieces(LOG2E, N_PIECES)
    log2e_col = jnp.zeros((16, 1), F32)
    for k, value in enumerate(log2e_pieces):
        log2e_col = jnp.where((piece == k) | (piece == N_PIECES + k), value, log2e_col)
    for hh in range(HEADS_PER_GROUP):
        alibi = jnp.broadcast_to(log2e_col * slopes[hh], (16, tq)).astype(BF16)
        q_l2 = (qT_ref[0, head_rows(hh), :] * LOG2E).astype(BF16)
        qa_scr[:, head_lanes(hh)] = jnp.concatenate([q_l2, alibi], axis=0)

    def cmp_scores(hh):
        q_h = qT_ref[0, head_rows(hh), :]
        q_hi = q_h.astype(BF16)
        q_lo = (q_h - q_hi.astype(F32)).astype(BF16)
        q3 = jnp.concatenate([q_hi, q_lo, q_hi, jnp.zeros_like(q_hi)], axis=0)
        return jnp.dot(kc3, q3, preferred_element_type=F32)

    def cmp_attend(p_heads, hh, s):
        s = jnp.where(valid_c, s - slopes[hh] * dist_c, MASK_VALUE)
        m = jnp.max(s, axis=0, keepdims=True)
        p = jnp.where(valid_c, jnp.exp(s - m), 0.0)
        l = jnp.sum(p, axis=0, keepdims=True)
        p = p / jnp.where(l > 0.0, l, 1.0)
        o_c = jnp.dot(vcT, p.astype(BF16), preferred_element_type=F32)
        out_scr[head_rows(hh), :] = gate(hh, 0) * o_c
        p_heads.append(p)

    def select_blocks(p_heads, _):
        p_sum = sum(p_heads[1:], p_heads[0])
        blk = lax.broadcasted_iota(jnp.int32, (n_sel, n_cmp), 0)
        cmp_ix = lax.broadcasted_iota(jnp.int32, (n_sel, n_cmp), 1)
        ratio = SEL_BLOCK // CMP_STRIDE
        extra = CMP_BLOCK // CMP_STRIDE - 1
        overlap_t = jnp.where((cmp_ix >= ratio * blk - extra) & (cmp_ix <= ratio * blk + ratio - 1),
                              1.0, 0.0).astype(BF16)
        imp = jnp.zeros((n_sel, tq), F32)
        rest = p_sum
        for _ in range(3):
            part = rest.astype(BF16)
            imp = imp + jnp.dot(overlap_t, part, preferred_element_type=F32)
            rest = rest - part.astype(F32)

        j_col = lax.broadcasted_iota(jnp.int32, (n_sel, 1), 0)
        forced = (j_col == 0) | (j_col == lax.shift_right_logical(t_row, SEL_SHIFT))
        future = j_col * SEL_BLOCK > t_row
        score = jnp.where(forced, jnp.inf, jnp.where(future, -jnp.inf, imp))
        rank = jnp.zeros((n_sel, tq), F32)
        for jp in range(n_sel):
            other = score[jp:jp + 1, :]
            ahead = (other > score) | ((other == score) & (j_col > jp))
            rank = rank + jnp.where(ahead, 1.0, 0.0)
        neg = jnp.where(rank < float(min(SEL_TOP_K, n_sel)), 0.0, MASK_VALUE)
        pad_rows = jnp.zeros((BF16_ROWS - SEL_PER_KEY_BLOCK, tq), F32)
        for c in range(n_sel // SEL_PER_KEY_BLOCK):
            rows = neg[c * SEL_PER_KEY_BLOCK:(c + 1) * SEL_PER_KEY_BLOCK, :]
            negm_scr[c] = jnp.concatenate([rows, pad_rows], axis=0).astype(BF16)
            used_smem[c] = (jnp.max(rows) > 0.5 * MASK_VALUE).astype(jnp.int32)

    rel = (lax.broadcasted_iota(jnp.int32, (KEY_BLOCK, tq), 1)
           - lax.broadcasted_iota(jnp.int32, (KEY_BLOCK, tq), 0))
    zero_tail = jnp.zeros((KEY_AUG - QUERY_AUG, tq), BF16)
    no_mask = jnp.zeros((BF16_ROWS, tq), BF16)
    all_masked = jnp.full((BF16_ROWS, tq), MASK_VALUE, BF16)

    sel_even, sel_odd, win_prev, win_diag = range(4)
    m_scr[...] = jnp.full(m_scr.shape, MASK_VALUE, F32)
    acc_scr[...] = jnp.zeros(acc_scr.shape, F32)

    def scores(job, hh):
        k_ref, _, kb, enabled, _, _ = job
        k0 = pl.multiple_of(kb * KEY_BLOCK, KEY_BLOCK)
        k_aug = k_ref[0, 0, 0, pl.ds(k0, KEY_BLOCK), :]
        if enabled is None:
            mask_tile = no_mask
        elif enabled is True:
            mask_tile = negm_scr[kb]
        else:
            mask_tile = jnp.where(enabled, negm_scr[kb], all_masked)
        q_aug = jnp.concatenate([qa_scr[:, head_lanes(hh)], mask_tile, zero_tail], axis=0)
        return jnp.dot(k_aug, q_aug, preferred_element_type=F32)

    def accumulate(job, hh, s):
        _, v_ref, kb, _, valid, slot = job
        cols = head_lanes(hh)
        if valid is not None:
            s = jnp.where(valid, s, MASK_VALUE)
        m_old = m_scr[slot, :, cols]
        m_new = jnp.maximum(m_old, jnp.max(s, axis=0, keepdims=True))
        p = jnp.exp2(s - jnp.maximum(m_new, 0.1 * MASK_VALUE))
        alpha = jnp.exp2(m_old - m_new)
        pv = jnp.dot(v_ref[0, 0, 0, kb], p.astype(BF16), preferred_element_type=F32)
        acc_scr[slot, :, cols] = alpha * acc_scr[slot, :, cols] + pv
        m_scr[slot, :, cols] = m_new

    def pipeline(units):
        issue = lambda unit: None if unit[0] is None else unit[0]()
        pending = [issue(unit) for unit in units[:SCORES_AHEAD]]
        for n, unit in enumerate(units):
            if n + SCORES_AHEAD < len(units):
                pending.append(issue(units[n + SCORES_AHEAD]))
            unit[1](pending.pop(0))

    def attention_units(jobs):
        return [(functools.partial(scores, job, hh), functools.partial(accumulate, job, hh))
                for job in jobs for hh in range(HEADS_PER_GROUP)]

    def finish(branch, slot_a, slot_b):
        m_a = m_scr[slot_a]
        m_b = m_scr[slot_b]
        m = jnp.maximum(m_a, m_b)
        w_a = jnp.exp2(m_a - m)
        w_b = jnp.exp2(m_b - m)
        for hh in range(HEADS_PER_GROUP):
            cols = head_lanes(hh)
            acc = acc_scr[slot_a, :, cols] * w_a[:, cols] + acc_scr[slot_b, :, cols] * w_b[:, cols]
            l = acc[HEAD_DIM:HEAD_DIM + 1, :]
            o_b = acc[:HEAD_DIM, :] / jnp.where(l > 0.0, l, 1.0)
            out_scr[head_rows(hh), :] = out_scr[head_rows(hh), :] + gate(hh, branch) * o_b

    prev = jnp.maximum(i - 1, 0)
    causal = rel >= 0
    off_without_prev = jnp.where(i > 0, 0, WINDOW + KEY_BLOCK)
    dist_prev = rel + KEY_BLOCK + off_without_prev
    p_heads = []
    pipeline(
        [(functools.partial(cmp_scores, hh), functools.partial(cmp_attend, p_heads, hh))
         for hh in range(HEADS_PER_GROUP)]
        + [(None, functools.partial(select_blocks, p_heads))]
        + attention_units([(kw_ref, vw_ref, prev, None, dist_prev < WINDOW, win_prev),
                           (kw_ref, vw_ref, i, None, causal, win_diag),
                           (ks_ref, vs_ref, prev, i > 0, None, sel_even),
                           (ks_ref, vs_ref, i, True, causal, sel_odd)]))

    n_todo = jnp.int32(0)
    for c in range(used_smem.shape[0]):
        take = jnp.logical_and(c < i - 1, used_smem[c] > 0)

        @pl.when(take)
        def _(c=c, slot=n_todo):
            todo_smem[slot] = c

        n_todo = n_todo + take.astype(jnp.int32)
    todo_smem[n_todo] = 0

    def past_pair(j, carry):
        pipeline(attention_units([(ks_ref, vs_ref, todo_smem[2 * j], True, None, sel_even),
                                  (ks_ref, vs_ref, todo_smem[2 * j + 1], 2 * j + 1 < n_todo, None, sel_odd)]))
        return carry

    lax.fori_loop(0, lax.shift_right_logical(n_todo + 1, 1), past_pair, 0)
    finish(1, sel_even, sel_odd)
    finish(2, win_prev, win_diag)

    o_ref[0] = out_scr[...].T


def _nsa(qT, kc3, vcT, k_aug, v_aug, bgT, *, tq):
    bsz, _, seq = qT.shape
    assert tq == KEY_BLOCK == WINDOW and seq % tq == 0
    n_cmp = kc3.shape[2]
    n_sel = seq // SEL_BLOCK
    n_kb = seq // KEY_BLOCK
    gw = HEADS_PER_GROUP * HEAD_DIM
    bg4 = bgT.reshape(bsz, N_KV_GROUPS, GATE_ROWS, seq)
    k_spec = lambda s: pl.BlockSpec((1, 1, 1, seq, KEY_AUG), lambda b, g, i: (b, s, g, 0, 0))
    v_spec = lambda s: pl.BlockSpec((1, 1, 1, n_kb, VAL_AUG, KEY_BLOCK), lambda b, g, i: (b, s, g, 0, 0, 0))
    return pl.pallas_call(
        functools.partial(_nsa_kernel, tq=tq, n_cmp=n_cmp, n_sel=n_sel),
        grid=(bsz, N_KV_GROUPS, seq // tq),
        in_specs=[
            pl.BlockSpec((1, gw, tq), lambda b, g, i: (b, g, i)),
            pl.BlockSpec((1, 1, n_cmp, 4 * HEAD_DIM), lambda b, g, i: (b, g, 0, 0)),
            pl.BlockSpec((1, 1, HEAD_DIM, n_cmp), lambda b, g, i: (b, g, 0, 0)),
            k_spec(0), v_spec(0),
            k_spec(1), v_spec(1),
            pl.BlockSpec((1, 1, GATE_ROWS, tq), lambda b, g, i: (b, g, 0, i)),
        ],
        out_specs=pl.BlockSpec((1, tq, gw), lambda b, g, i: (b, i, g)),
        out_shape=jax.ShapeDtypeStruct((bsz, seq, ATTN_WIDTH), F32),
        scratch_shapes=[
            pltpu.VMEM((n_kb, 16, tq), BF16),
            pltpu.VMEM((HEAD_DIM + 16, HEADS_PER_GROUP * tq), BF16),
            pltpu.VMEM((4, 1, HEADS_PER_GROUP * tq), F32),
            pltpu.VMEM((4, VAL_AUG, HEADS_PER_GROUP * tq), F32),
            pltpu.VMEM((gw, tq), F32),
            pltpu.SMEM((n_kb,), jnp.int32),
            pltpu.SMEM((n_kb + 1,), jnp.int32),
        ],
        compiler_params=_params(3),
        name="nsa",
    )(qT, kc3, vcT, k_aug, v_aug, k_aug, v_aug, bg4)


def _merge_kernel(x_ref, g_ref, o_ref_in, sc_ref, sh_ref, gt_ref, wm_ref, wl_ref, wg_ref, wn_ref, wo_ref,
                  lg_ref, lb_ref, y_ref):
    tm = x_ref.shape[1]
    row_blocks = [slice(r, r + ROW_BLOCK) for r in range(0, tm, ROW_BLOCK)]
    halves = [(0, D_MODEL // 2), (D_MODEL // 2, D_MODEL)]
    dot = functools.partial(jnp.dot, preferred_element_type=F32)

    def first_stage(rows):
        g = g_ref[0, rows, :].astype(BF16)
        o = o_ref_in[0, rows, :].astype(BF16)
        branch = [(dot(g, wl_ref[:, lo:hi]), dot(g, wg_ref[:, lo:hi]), dot(o, wn_ref[:, lo:hi])) for lo, hi in halves]
        x = x_ref[0, rows, :]
        u = (x * (1.0 + sc_ref[0]) + sh_ref[0]).astype(BF16)
        gates = [(dot(u, wm_ref[:, lo:hi]), dot(u, wm_ref[:, D_MODEL + lo:D_MODEL + hi])) for lo, hi in halves]
        return x, branch, gates

    def second_stage(rows, x, branch, gates):
        mix = None
        for (lo, hi), (lin, gate, z_nsa), (m_ssm, m_nsa) in zip(halves, branch, gates):
            merged = _sigmoid(m_ssm) * (lin * _sigmoid(gate)) + _sigmoid(m_nsa) * z_nsa
            part = dot(merged.astype(BF16), wo_ref[lo:hi, :])
            mix = part if mix is None else mix + part
        y = DEEPNORM_ALPHA * x + (1.0 + gt_ref[0]) * mix
        y_ref[0, rows, :] = _layer_norm(y, lg_ref[...], lb_ref[...])

    staged = None
    for rows in row_blocks + [None]:
        upcoming = first_stage(rows) if rows is not None else None
        if staged is not None:
            second_stage(*staged)
        staged = (rows,) + upcoming if rows is not None else None


def _merge(x, g_ssm, o_nsa, scale1, shift1, gate1, w_m, w_lin, w_gate, w_nsa, w_out, ln_g, ln_b, *, tm):
    bsz, seq, _ = x.shape
    row = pl.BlockSpec((1, 1, D_MODEL), lambda b, i: (b, 0, 0))
    full = lambda a: pl.BlockSpec(a.shape, lambda b, i: (0, 0), pipeline_mode=pl.Buffered(1))
    return pl.pallas_call(
        _merge_kernel,
        grid=(bsz, seq // tm),
        in_specs=[
            pl.BlockSpec((1, tm, D_MODEL), lambda b, i: (b, i, 0)),
            pl.BlockSpec((1, tm, SSM_WIDTH), lambda b, i: (b, i, 0)),
            pl.BlockSpec((1, tm, ATTN_WIDTH), lambda b, i: (b, i, 0)),
            row, row, row,
            full(w_m), full(w_lin), full(w_gate), full(w_nsa), full(w_out), full(ln_g), full(ln_b),
        ],
        out_specs=pl.BlockSpec((1, tm, D_MODEL), lambda b, i: (b, i, 0)),
        out_shape=jax.ShapeDtypeStruct(x.shape, F32),
        compiler_params=_params(2),
        name="merge",
    )(x, g_ssm, o_nsa, scale1, shift1, gate1, w_m, w_lin, w_gate, w_nsa, w_out, ln_g, ln_b)


def _ffn_kernel(x_ref, sc_ref, sh_ref, gt_ref, wg_ref, wu_ref, wd_ref, lg_ref, lb_ref, y_ref, *, n_chunks):
    tm = x_ref.shape[1]
    row_blocks = [slice(r, r + ROW_BLOCK) for r in range(0, tm, ROW_BLOCK)]
    tiles = FFN_HIDDEN // MXU_WIDTH
    edges = [MXU_WIDTH * ((tiles * c + n_chunks - 1) // n_chunks) for c in range(n_chunks + 1)]
    chunks = [slice(lo, hi) for lo, hi in zip(edges[:-1], edges[1:])]
    dot = functools.partial(jnp.dot, preferred_element_type=F32)

    def gate_up(rows):
        x = x_ref[0, rows, :]
        u = (x * (1.0 + sc_ref[0]) + sh_ref[0]).astype(BF16)
        return x, [(dot(u, wg_ref[:, cols]), dot(u, wu_ref[:, cols])) for cols in chunks]

    def down(rows, x, products):
        ffn = None
        for cols, (a, b) in zip(chunks, products):
            part = dot(((a * _sigmoid(a)) * b).astype(BF16), wd_ref[cols, :])
            ffn = part if ffn is None else ffn + part
        y = DEEPNORM_ALPHA * x + (1.0 + gt_ref[0]) * ffn
        y_ref[0, rows, :] = _layer_norm(y, lg_ref[...], lb_ref[...])

    staged = None
    for rows in row_blocks + [None]:
        upcoming = gate_up(rows) if rows is not None else None
        if staged is not None:
            down(*staged)
        staged = (rows,) + upcoming if rows is not None else None


def _ffn(x, scale2, shift2, gate2, w_gate, w_up, w_down, ln_g, ln_b, *, tm, n_chunks):
    bsz, seq, _ = x.shape
    row = pl.BlockSpec((1, 1, D_MODEL), lambda b, i: (b, 0, 0))
    resident = lambda a: pl.BlockSpec(a.shape, lambda b, i: (0, 0), pipeline_mode=pl.Buffered(1))
    return pl.pallas_call(
        functools.partial(_ffn_kernel, n_chunks=n_chunks),
        grid=(bsz, seq // tm),
        in_specs=[
            pl.BlockSpec((1, tm, D_MODEL), lambda b, i: (b, i, 0)),
            row, row, row,
            resident(w_gate), resident(w_up), resident(w_down), resident(ln_g), resident(ln_b),
        ],
        out_specs=pl.BlockSpec((1, tm, D_MODEL), lambda b, i: (b, i, 0)),
        out_shape=jax.ShapeDtypeStruct(x.shape, F32),
        compiler_params=_params(2),
        name="ffn",
    )(x, scale2, shift2, gate2, w_gate, w_up, w_down, ln_g, ln_b)


def _layer(x, c, w_ada, b_ada, w_in, ssm_a_re, ssm_a_im, ssm_log_dt, ssm_b_re, ssm_b_im, ssm_c_re, ssm_c_im,
           ssm_d, w_glu_lin, w_glu_gate, cmp_pe_k, cmp_pe_v, w_cmp_k1, w_cmp_k2, w_cmp_v1, w_cmp_v2,
           w_nsa_proj, w_out, ln1_g, ln1_b, w_ffn_gate, w_ffn_up, w_ffn_down, ln2_g, ln2_b):
    bsz, seq, _ = x.shape
    tq = min(KEY_BLOCK, seq)
    tm = min(512, seq)
    tm_dense = min(1024, seq)

    mod = _ada_mod(c, w_ada, b_ada).reshape(6, bsz, 1, D_MODEL)
    shift1, scale1, gate1, shift2, scale2, gate2 = (mod[k] for k in range(6))

    w_nat, w_tr, w_mg = _split_w_in(w_in)
    u_ssm, kv_cmp, k_aug, qT, v_aug, bgT = _in_proj(x, scale1, shift1, w_nat, w_tr, tm=tm)

    mats = _ssm_matrices(ssm_a_re, ssm_a_im, ssm_log_dt, ssm_b_re, ssm_b_im, ssm_c_re, ssm_c_im, ssm_d)
    g_ssm = _ssm(u_ssm, mats)

    kc3, vcT = _compress(kv_cmp,
                        cmp_pe_k.reshape(1, CMP_BLOCK * HEAD_DIM), cmp_pe_v.reshape(1, CMP_BLOCK * HEAD_DIM),
                        w_cmp_k1.reshape(CMP_BLOCK * HEAD_DIM, -1), w_cmp_k2,
                        w_cmp_v1.reshape(CMP_BLOCK * HEAD_DIM, -1), w_cmp_v2)
    o_nsa = _nsa(qT, kc3, vcT, k_aug, v_aug, bgT, tq=tq)

    x1 = _merge(x, g_ssm, o_nsa, scale1, shift1, gate1, w_mg,
                w_glu_lin.astype(BF16), w_glu_gate.astype(BF16), w_nsa_proj.astype(BF16), w_out.astype(BF16),
                ln1_g.reshape(1, D_MODEL), ln1_b.reshape(1, D_MODEL), tm=tm_dense)
    return _ffn(x1, scale2, shift2, gate2, w_ffn_gate.astype(BF16), w_ffn_up.astype(BF16),
                w_ffn_down.astype(BF16), ln2_g.reshape(1, D_MODEL), ln2_b.reshape(1, D_MODEL), tm=tm_dense, n_chunks=2)


def kernel(x, c, w_ada, b_ada, w_in, ssm_a_re, ssm_a_im, ssm_log_dt, ssm_b_re, ssm_b_im, ssm_c_re, ssm_c_im,
           ssm_d, w_glu_lin, w_glu_gate, cmp_pe_k, cmp_pe_v, w_cmp_k1, w_cmp_k2, w_cmp_v1, w_cmp_v2,
           w_nsa_proj, w_out, ln1_g, ln1_b, w_ffn_gate, w_ffn_up, w_ffn_down, ln2_g, ln2_b):
    for l in range(w_ada.shape[0]):
        x = _layer(x, c, w_ada[l], b_ada[l], w_in[l], ssm_a_re[l], ssm_a_im[l], ssm_log_dt[l],
                   ssm_b_re[l], ssm_b_im[l], ssm_c_re[l], ssm_c_im[l], ssm_d[l],
                   w_glu_lin[l], w_glu_gate[l], cmp_pe_k[l], cmp_pe_v[l],
                   w_cmp_k1[l], w_cmp_k2[l], w_cmp_v1[l], w_cmp_v2[l], w_nsa_proj[l], w_out[l],
                   ln1_g[l], ln1_b[l], w_ffn_gate[l], w_ffn_up[l], w_ffn_down[l], ln2_g[l], ln2_b[l])
    return x
```

```python
import functools
import math

import jax
import jax.numpy as jnp
from jax import lax
from jax.experimental import pallas as pl
from jax.experimental.pallas import tpu as pltpu

F32 = jnp.float32
BF16 = jnp.bfloat16
HIGHEST = lax.Precision.HIGHEST

D_MODEL = 1024
SSM_WIDTH = D_MODEL // 2
SSM_GROUP_SIZE = 16
SSM_GROUPS = SSM_WIDTH // SSM_GROUP_SIZE
SSM_STATE = 64
SSM_CHUNK = 16
SSM_GROUP_TILE = 8
SSM_STATE_ROW_PAD = 8
N_HEADS = 8
HEAD_DIM = 64
N_KV_GROUPS = 2
HEADS_PER_GROUP = N_HEADS // N_KV_GROUPS
ATTN_WIDTH = N_HEADS * HEAD_DIM
KV_WIDTH = N_KV_GROUPS * HEAD_DIM
CMP_BLOCK = 32
CMP_STRIDE = 16
CMP_HIDDEN = HEAD_DIM
SEL_BLOCK = 64
SEL_TOP_K = 8
WINDOW = 256
N_NSA_BRANCHES = 3
GATE_ROWS = 16
FFN_HIDDEN = (8 * D_MODEL + 3 * 256 - 1) // (3 * 256) * 256
DEEPNORM_ALPHA = 2.0 ** 0.25
LN_EPS = 1e-5
MASK_VALUE = -1e30

VMEM_LIMIT = 56 * 1024 * 1024
MXU_WIDTH = 256
BF16_ROWS = 16
ROW_BLOCK = 256

KEY_BLOCK = 256
SEL_SHIFT = SEL_BLOCK.bit_length() - 1
SEL_PER_KEY_BLOCK = KEY_BLOCK // SEL_BLOCK
KEY_AUG = 128
ALIBI_COL = HEAD_DIM
N_PIECES = 4
MASK_COL = HEAD_DIM + 16
QUERY_AUG = HEAD_DIM + 32
VAL_AUG = HEAD_DIM + 16
LOG2E = math.log2(math.e)
HEADS_PER_UNIT = 1
SCORES_AHEAD = 5

NAT_COLS = SSM_WIDTH + 4 * KV_WIDTH
TR_ROWS = ATTN_WIDTH + 2 * KV_WIDTH + N_KV_GROUPS * GATE_ROWS


def _bf16_pieces(value, n):
    pieces = []
    rest = value
    for _ in range(n):
        mant, expo = math.frexp(rest)
        piece = math.ldexp(round(mant * 256.0) / 256.0, expo)
        pieces.append(piece)
        rest -= piece
    return pieces


def _sigmoid(x):
    return 1.0 / (1.0 + jnp.exp(-x))


def _gelu(x):
    c = math.sqrt(2.0 / math.pi)
    return 0.5 * x * (1.0 + jnp.tanh(c * (x + 0.044715 * (x * x * x))))


def _layer_norm(y, gain, bias):
    mu = jnp.mean(y, axis=-1, keepdims=True)
    d = y - mu
    var = jnp.mean(d * d, axis=-1, keepdims=True)
    return d * lax.rsqrt(var + LN_EPS) * gain + bias


def _params(n_axes):
    return pltpu.CompilerParams(dimension_semantics=("arbitrary",) * n_axes, vmem_limit_bytes=VMEM_LIMIT)


def _ada_kernel(c_ref, w_ref, b_ref, o_ref):
    c = c_ref[...]
    a = c * _sigmoid(c)
    w = w_ref[...].astype(BF16)
    a_hi = a.astype(BF16)
    a_lo = (a - a_hi.astype(F32)).astype(BF16)
    mod = (jnp.dot(a_hi, w, preferred_element_type=F32) + jnp.dot(a_lo, w, preferred_element_type=F32)) + b_ref[...]
    for k in range(o_ref.shape[0]):
        o_ref[k] = mod[:, k * D_MODEL:(k + 1) * D_MODEL]


def _ada_mod(c, w_ada, b_ada):
    bsz = c.shape[0]
    per_step = 2
    return pl.pallas_call(
        _ada_kernel,
        grid=(6 // per_step,),
        in_specs=[
            pl.BlockSpec((bsz, D_MODEL), lambda j: (0, 0)),
            pl.BlockSpec((D_MODEL, per_step * D_MODEL), lambda j: (0, j)),
            pl.BlockSpec((1, per_step * D_MODEL), lambda j: (0, j)),
        ],
        out_specs=pl.BlockSpec((per_step, bsz, D_MODEL), lambda j: (j, 0, 0)),
        out_shape=jax.ShapeDtypeStruct((6, bsz, D_MODEL), F32),
        compiler_params=_params(1),
        name="ada_mod",
    )(c, w_ada, b_ada.reshape(1, 6 * D_MODEL))


def _in_proj_kernel(x_ref, sc_ref, sh_ref, wn_ref, wt_ref, ussm_ref, kc_ref, kp_ref, qT_ref, vp_ref, bgT_ref):
    i = pl.program_id(1)
    tm = x_ref.shape[1]
    u = (x_ref[0] * (1.0 + sc_ref[0]) + sh_ref[0]).astype(BF16)
    nat = jnp.dot(u, wn_ref[...], preferred_element_type=F32)
    ussm_ref[0] = nat[:, :SSM_WIDTH]
    for s in range(2):
        for g in range(N_KV_GROUPS):
            lo = SSM_WIDTH + s * KV_WIDTH + g * HEAD_DIM
            kc_ref[0, s, g] = nat[:, lo:lo + HEAD_DIM]
    pos = i * tm + lax.broadcasted_iota(jnp.int32, (tm, KEY_AUG), 0)
    col = lax.broadcasted_iota(jnp.int32, (1, KEY_AUG), 1)
    blk = lax.shift_right_logical(pos, SEL_SHIFT)
    in_a = (col >= ALIBI_COL) & (col < ALIBI_COL + N_PIECES)
    in_b = (col >= ALIBI_COL + N_PIECES) & (col < ALIBI_COL + 2 * N_PIECES)
    hot = col == MASK_COL + (blk & (SEL_PER_KEY_BLOCK - 1))
    aux = jnp.where(in_a, blk * SEL_BLOCK, jnp.where(in_b, pos & (SEL_BLOCK - 1), jnp.where(hot, 1, 0))).astype(F32)
    head_lanes = col < HEAD_DIM
    for s in range(2):
        lo = SSM_WIDTH + (2 + s) * KV_WIDTH
        both = nat[:, lo:lo + KV_WIDTH]
        for g in range(N_KV_GROUPS):
            keys = both if g == 0 else pltpu.roll(both, HEAD_DIM, 1)
            kp_ref[0, s, g] = jnp.where(head_lanes, keys, aux).astype(BF16)
    tr = lax.dot_general(wt_ref[...], u, (((1,), (1,)), ((), ())), preferred_element_type=F32)
    qT_ref[0] = tr[:ATTN_WIDTH] * (HEAD_DIM ** -0.5)
    ones_rows = jnp.where(lax.broadcasted_iota(jnp.int32, (VAL_AUG - HEAD_DIM, tm), 0) == 0, 1.0, 0.0)
    for s in range(2):
        for g in range(N_KV_GROUPS):
            lo = ATTN_WIDTH + (s * N_KV_GROUPS + g) * HEAD_DIM
            v_aug = jnp.concatenate([tr[lo:lo + HEAD_DIM], ones_rows], axis=0).astype(BF16)
            for c in range(tm // KEY_BLOCK):
                vp_ref[0, s, g, c] = v_aug[:, c * KEY_BLOCK:(c + 1) * KEY_BLOCK]
    bgT_ref[0] = _sigmoid(tr[ATTN_WIDTH + 2 * KV_WIDTH:])


def _in_proj(x, scale1, shift1, w_nat, w_tr, *, tm):
    bsz, seq, _ = x.shape
    n_t = seq // tm
    n_kb = seq // KEY_BLOCK
    out_shapes = (
        jax.ShapeDtypeStruct((bsz, seq, SSM_WIDTH), F32),
        jax.ShapeDtypeStruct((bsz, 2, N_KV_GROUPS, seq, HEAD_DIM), F32),
        jax.ShapeDtypeStruct((bsz, 2, N_KV_GROUPS, seq, KEY_AUG), BF16),
        jax.ShapeDtypeStruct((bsz, ATTN_WIDTH, seq), F32),
        jax.ShapeDtypeStruct((bsz, 2, N_KV_GROUPS, n_kb, VAL_AUG, KEY_BLOCK), BF16),
        jax.ShapeDtypeStruct((bsz, N_KV_GROUPS * GATE_ROWS, seq), F32),
    )
    return pl.pallas_call(
        _in_proj_kernel,
        grid=(bsz, n_t),
        in_specs=[
            pl.BlockSpec((1, tm, D_MODEL), lambda b, i: (b, i, 0)),
            pl.BlockSpec((1, 1, D_MODEL), lambda b, i: (b, 0, 0)),
            pl.BlockSpec((1, 1, D_MODEL), lambda b, i: (b, 0, 0)),
            pl.BlockSpec((D_MODEL, NAT_COLS), lambda b, i: (0, 0)),
            pl.BlockSpec((TR_ROWS, D_MODEL), lambda b, i: (0, 0)),
        ],
        out_specs=(
            pl.BlockSpec((1, tm, SSM_WIDTH), lambda b, i: (b, i, 0)),
            pl.BlockSpec((1, 2, N_KV_GROUPS, tm, HEAD_DIM), lambda b, i: (b, 0, 0, i, 0)),
            pl.BlockSpec((1, 2, N_KV_GROUPS, tm, KEY_AUG), lambda b, i: (b, 0, 0, i, 0)),
            pl.BlockSpec((1, ATTN_WIDTH, tm), lambda b, i: (b, 0, i)),
            pl.BlockSpec((1, 2, N_KV_GROUPS, tm // KEY_BLOCK, VAL_AUG, KEY_BLOCK), lambda b, i: (b, 0, 0, i, 0, 0)),
            pl.BlockSpec((1, N_KV_GROUPS * GATE_ROWS, tm), lambda b, i: (b, 0, i)),
        ),
        out_shape=out_shapes,
        compiler_params=_params(2),
        name="in_proj",
    )(x, scale1, shift1, w_nat, w_tr)


def _split_w_in(w_in):
    o_q = SSM_WIDTH
    o_kv = o_q + ATTN_WIDTH
    o_bg = o_kv + 6 * KV_WIDTH
    o_mg = o_bg + N_NSA_BRANCHES * N_HEADS
    kv = [w_in[:, o_kv + s * KV_WIDTH:o_kv + (s + 1) * KV_WIDTH] for s in range(6)]
    w_nat = jnp.concatenate([w_in[:, :o_q], kv[0], kv[1], kv[2], kv[4]], axis=1)
    per_group = HEADS_PER_GROUP * N_NSA_BRANCHES
    bg = w_in[:, o_bg:o_mg].reshape(D_MODEL, N_KV_GROUPS, per_group)
    bg = jnp.pad(bg, ((0, 0), (0, 0), (0, GATE_ROWS - per_group))).reshape(D_MODEL, N_KV_GROUPS * GATE_ROWS)
    w_tr = jnp.concatenate([w_in[:, o_q:o_kv], kv[3], kv[5], bg], axis=1).T
    return w_nat.astype(BF16), w_tr.astype(BF16), w_in[:, o_mg:].astype(BF16)


def _ssm_prep_kernel(a_ref, logdt_ref, bT_ref, c_ref, d_ref, mT_ref, pT_ref, qT_ref, a16_ref):
    t_n = SSM_CHUNK
    c_n = SSM_GROUP_SIZE
    n_p = SSM_STATE
    rows = t_n * c_n
    dt = jnp.exp(logdt_ref[0])
    a_re = a_ref[0, 0]
    a_im = a_ref[0, 1]
    decay = jnp.exp(a_re * dt)
    ab_re = decay * jnp.cos(a_im * dt)
    ab_im = decay * jnp.sin(a_im * dt)
    n_re = ab_re - 1.0
    denom = a_re * a_re + a_im * a_im
    f_re = (n_re * a_re + ab_im * a_im) / denom
    f_im = (ab_im * a_re - n_re * a_im) / denom
    powers = [(jnp.ones_like(ab_re), jnp.zeros_like(ab_re))]
    for _ in range(t_n):
        p_re, p_im = powers[-1]
        powers.append((p_re * ab_re - p_im * ab_im, p_re * ab_im + p_im * ab_re))

    def per_token(first, step):
        picks = [powers[first + step * k] for k in range(t_n)]
        return (jnp.concatenate([jnp.broadcast_to(p[0], (c_n, 2 * n_p)) for p in picks], axis=0),
                jnp.concatenate([jnp.broadcast_to(p[1], (c_n, 2 * n_p)) for p in picks], axis=0))

    def cmul(x_re, x_im, y_re, y_im):
        return x_re * y_re - x_im * y_im, x_re * y_im + x_im * y_re

    bb_re, bb_im = cmul(f_re, f_im, bT_ref[0, 0], bT_ref[0, 1])
    bt_re = jnp.concatenate([bb_re] * t_n, axis=0)
    bt_im = jnp.concatenate([bb_im] * t_n, axis=0)
    ct_re = jnp.concatenate([c_ref[0, 0]] * t_n, axis=0)
    ct_im = jnp.concatenate([c_ref[0, 1]] * t_n, axis=0)

    p_re, p_im = cmul(*per_token(t_n - 1, -1), bt_re, bt_im)
    low_half = lax.broadcasted_iota(jnp.int32, (1, 2 * n_p), 1) < n_p
    pT_ref[0] = jnp.where(low_half, p_re, p_im).T.astype(BF16)

    q_re, q_im = cmul(ct_re, ct_im, *per_token(1, 1))
    qT_ref[0, 0] = q_re[:, :n_p].astype(BF16)
    qT_ref[0, 1] = (-q_im[:, :n_p]).astype(BF16)
    a16_ref[0] = jnp.concatenate([powers[t_n][0][:, :n_p], powers[t_n][1][:, :n_p]], axis=0)

    k_re, k_im = cmul(ct_re, ct_im, *per_token(0, 1))
    nt = (((1,), (1,)), ((), ()))
    strips = (lax.dot_general(_split3(k_re[:, :n_p]), _stack3(bt_re[:, :n_p], axis=1), nt, preferred_element_type=F32)
              - lax.dot_general(_split3(k_im[:, :n_p]), _stack3(bt_im[:, :n_p], axis=1), nt,
                                preferred_element_type=F32))
    lane = lax.broadcasted_iota(jnp.int32, (1, rows), 1)
    s_lane = lax.shift_right_logical(lane, c_n.bit_length() - 1)
    ci_lane = lane & (c_n - 1)
    row = lax.broadcasted_iota(jnp.int32, (rows, 1), 0)
    t_row = lax.shift_right_logical(row, c_n.bit_length() - 1)
    co_row = row & (c_n - 1)
    blocks = []
    for t in range(t_n):
        acc = jnp.zeros((c_n, rows), F32)
        for lag in range(t + 1):
            acc = jnp.where(s_lane == t - lag, strips[lag * c_n:(lag + 1) * c_n, :], acc)
        blocks.append(acc)
    d_rows = jnp.concatenate([d_ref[0]] * t_n, axis=0)
    skip = jnp.where((t_row == s_lane) & (co_row == ci_lane), d_rows, 0.0)
    mT_ref[0] = (jnp.concatenate(blocks, axis=0) + skip).astype(BF16)


def _ssm_matrices(a_re, a_im, log_dt, b_re, b_im, c_re, c_im, d_skip):
    n_g, n_p = a_re.shape
    c_n = SSM_GROUP_SIZE
    rows = SSM_CHUNK * c_n
    twice = lambda x: jnp.concatenate([x, x], axis=-1).astype(F32)
    a = twice(jnp.stack([a_re, a_im], axis=1)).reshape(n_g, 2, 1, 2 * n_p)
    bT = twice(jnp.swapaxes(jnp.stack([b_re, b_im], axis=1), 2, 3))
    c = twice(jnp.stack([c_re, c_im], axis=1))
    spec = lambda shape: pl.BlockSpec((1,) + shape, lambda g: (g,) + (0,) * len(shape))
    mT, pT, qT, a16 = pl.pallas_call(
        _ssm_prep_kernel,
        grid=(n_g,),
        in_specs=[spec((2, 1, 2 * n_p)), spec((1, 1)), spec((2, c_n, 2 * n_p)), spec((2, c_n, 2 * n_p)),
                  spec((c_n, 1))],
        out_specs=(spec((rows, rows)), spec((2 * n_p, rows)), spec((2, rows, n_p)), spec((2, n_p))),
        out_shape=(
            jax.ShapeDtypeStruct((n_g, rows, rows), BF16),
            jax.ShapeDtypeStruct((n_g, 2 * n_p, rows), BF16),
            jax.ShapeDtypeStruct((n_g, 2, rows, n_p), BF16),
            jax.ShapeDtypeStruct((n_g, 2, n_p), F32),
        ),
        compiler_params=_params(1),
        name="ssm_prep",
    )(a, log_dt.astype(F32).reshape(n_g, 1, 1), bT, c, d_skip.astype(F32).reshape(n_g, c_n, 1))
    return mT, pT.reshape(n_g, 2, n_p, rows), qT, a16


def _ssm_kernel(u_ref, mT_ref, pT_ref, qT_ref, a16_ref, o_ref, ut_scr, yt_scr, sre, sim, *, n_chunks):
    n_g = SSM_GROUP_TILE
    t_n = SSM_CHUNK
    c_n = SSM_GROUP_SIZE

    for s in range(t_n):
        x_s = u_ref[0, pl.ds(s, n_chunks, stride=t_n), :]
        ut_scr[:, s * c_n:(s + 1) * c_n, :] = x_s.T.reshape(n_g, c_n, n_chunks).astype(BF16)

    pitch = n_chunks + SSM_STATE_ROW_PAD

    def group_rows(g):
        return pl.ds(g * pitch, n_chunks)

    def chunk_states(g, carry):
        ut = ut_scr[g]
        sre[group_rows(g), :] = jnp.dot(pT_ref[g, 0], ut, preferred_element_type=F32).T
        sim[group_rows(g), :] = jnp.dot(pT_ref[g, 1], ut, preferred_element_type=F32).T
        return carry

    for g in range(n_g):
        chunk_states(g, 0)

    a_r = a16_ref[pl.ds(0, n_g, stride=2), :]
    a_i = a16_ref[pl.ds(1, n_g, stride=2), :]

    def carry_states(c, h):
        h_r, h_i = h
        rows = pl.ds(c, n_g, stride=pitch)
        s_r = sre[rows, :]
        s_i = sim[rows, :]
        sre[rows, :] = h_r
        sim[rows, :] = h_i
        return a_r * h_r - a_i * h_i + s_r, a_r * h_i + a_i * h_r + s_i

    zero = jnp.zeros((n_g, SSM_STATE), F32)
    lax.fori_loop(0, n_chunks, carry_states, (zero, zero), unroll=4)

    nt = (((1,), (1,)), ((), ()))

    def outputs(g, carry):
        yt = jnp.dot(mT_ref[g], ut_scr[g], preferred_element_type=F32)
        yt = yt + lax.dot_general(qT_ref[g, 0], sre[group_rows(g), :].astype(BF16), nt, preferred_element_type=F32)
        yt = yt + lax.dot_general(qT_ref[g, 1], sim[group_rows(g), :].astype(BF16), nt, preferred_element_type=F32)
        yt_scr[g] = _gelu(yt)
        return carry

    for g in range(n_g):
        outputs(g, 0)

    for t in range(t_n):
        z = yt_scr[:, t * c_n:(t + 1) * c_n, :].reshape(n_g * c_n, n_chunks)
        o_ref[0, pl.ds(t, n_chunks, stride=t_n), :] = z.T


def _ssm(u_ssm, mats):
    bsz, seq, width = u_ssm.shape
    n_chunks = seq // SSM_CHUNK
    mT, pT, qT, a16 = mats
    gt = SSM_GROUP_TILE
    n_tiles = SSM_GROUPS // gt
    rows = SSM_CHUNK * SSM_GROUP_SIZE
    a16 = a16.reshape(2 * SSM_GROUPS, SSM_STATE)
    per_tile = lambda a: pl.BlockSpec((gt,) + a.shape[1:], lambda t, b: (t,) + (0,) * (a.ndim - 1))
    return pl.pallas_call(
        functools.partial(_ssm_kernel, n_chunks=n_chunks),
        grid=(n_tiles, bsz),
        in_specs=[
            pl.BlockSpec((1, seq, gt * SSM_GROUP_SIZE), lambda t, b: (b, 0, t)),
            per_tile(mT), per_tile(pT), per_tile(qT),
            pl.BlockSpec((2 * gt, SSM_STATE), lambda t, b: (t, 0)),
        ],
        out_specs=pl.BlockSpec((1, seq, gt * SSM_GROUP_SIZE), lambda t, b: (b, 0, t)),
        out_shape=jax.ShapeDtypeStruct(u_ssm.shape, F32),
        scratch_shapes=[
            pltpu.VMEM((gt, rows, n_chunks), BF16),
            pltpu.VMEM((gt, rows, n_chunks), F32),
            pltpu.VMEM((gt * (n_chunks + SSM_STATE_ROW_PAD), SSM_STATE), F32),
            pltpu.VMEM((gt * (n_chunks + SSM_STATE_ROW_PAD), SSM_STATE), F32),
        ],
        compiler_params=_params(2),
        name="ssm",
    )(u_ssm, mT, pT, qT, a16)


def _split3(x):
    hi = x.astype(BF16)
    lo = (x - hi.astype(F32)).astype(BF16)
    return jnp.concatenate([hi, lo, hi], axis=-1)


def _stack3(w, axis=0):
    hi = w.astype(BF16)
    lo = (w - hi.astype(F32)).astype(BF16)
    return jnp.concatenate([hi, hi, lo], axis=axis)


def _compress_kernel(kx_ref, vx_ref, pek_ref, pev_ref, wk1_ref, wk2_ref, wv1_ref, wv2_ref, kc_ref, vcT_ref):
    def mlp(x_ref, pe_ref, w1_ref, w2_ref):
        n = x_ref.shape[3] // CMP_STRIDE
        x = jnp.concatenate([x_ref[0, 0, 0, pl.ds(l, n, stride=CMP_STRIDE), :] for l in range(CMP_STRIDE)], axis=1)
        both = jnp.dot(_split3(x), w1_ref[...], preferred_element_type=F32)
        pe_both = jnp.dot(_split3(pe_ref[...]), w1_ref[...], preferred_element_type=F32)
        bias = pe_both[0:1, :CMP_HIDDEN] + pe_both[1:2, CMP_HIDDEN:]
        h = _gelu(both[:, :CMP_HIDDEN] + pltpu.roll(both[:, CMP_HIDDEN:], n - 1, 0) + bias)
        return jnp.dot(_split3(h), w2_ref[...], preferred_element_type=F32)

    kc = mlp(kx_ref, pek_ref, wk1_ref, wk2_ref)
    kc_hi = kc.astype(BF16)
    kc_lo = (kc - kc_hi.astype(F32)).astype(BF16)
    kc_ref[0, 0] = jnp.concatenate([kc_hi, kc_hi, kc_lo, jnp.zeros_like(kc_hi)], axis=1)
    vcT_ref[0, 0] = mlp(vx_ref, pev_ref, wv1_ref, wv2_ref).T.astype(BF16)


def _compress(kv, pe_k, pe_v, wk1, wk2, wv1, wv2):
    bsz, _, n_g, seq, width = kv.shape
    n_chunks = seq // CMP_STRIDE
    half = CMP_STRIDE * HEAD_DIM
    side_by_side = lambda w1: _stack3(jnp.concatenate([w1[:half], w1[half:]], axis=1))
    two_rows = lambda pe: jnp.pad(pe.reshape(2, half), ((0, BF16_ROWS - 2), (0, 0)))
    pe_k, pe_v = two_rows(pe_k), two_rows(pe_v)
    wk1, wv1, wk2, wv2 = side_by_side(wk1), side_by_side(wv1), _stack3(wk2), _stack3(wv2)
    blk = lambda s: pl.BlockSpec((1, 1, 1, seq, width), lambda b, g: (b, s, g, 0, 0))
    full = lambda a: pl.BlockSpec(a.shape, lambda b, g: (0, 0))
    return pl.pallas_call(
        _compress_kernel,
        grid=(bsz, n_g),
        in_specs=[blk(0), blk(1), full(pe_k), full(pe_v), full(wk1), full(wk2), full(wv1), full(wv2)],
        out_specs=(
            pl.BlockSpec((1, 1, n_chunks, 4 * HEAD_DIM), lambda b, g: (b, g, 0, 0)),
            pl.BlockSpec((1, 1, HEAD_DIM, n_chunks), lambda b, g: (b, g, 0, 0)),
        ),
        out_shape=(
            jax.ShapeDtypeStruct((bsz, n_g, n_chunks, 4 * HEAD_DIM), BF16),
            jax.ShapeDtypeStruct((bsz, n_g, HEAD_DIM, n_chunks), BF16),
        ),
        compiler_params=_params(2),
        name="compress",
    )(kv, kv, pe_k, pe_v, wk1, wk2, wv1, wv2)


def _nsa_kernel(qT_ref, kc_ref, vcT_ref, ks_ref, vs_ref, kw_ref, vw_ref, bg_ref, o_ref,
                negm_scr, qa_scr, m_scr, acc_scr, out_scr, used_smem, todo_smem, *, tq, n_cmp, n_sel):
    g = pl.program_id(1)
    i = pl.program_id(2)
    t_row = i * tq + lax.broadcasted_iota(jnp.int32, (1, tq), 1)
    slopes = [jnp.where(g == 0, 2.0 ** -(hh + 1), 2.0 ** -(HEADS_PER_GROUP + hh + 1)).astype(F32)
              for hh in range(HEADS_PER_GROUP)]

    def gate(hh, branch):
        r = hh * N_NSA_BRANCHES + branch
        return bg_ref[0, 0, r:r + 1, :]

    def head_rows(hh):
        return slice(hh * HEAD_DIM, (hh + 1) * HEAD_DIM)

    def head_lanes(hh):
        return slice(hh * tq, (hh + 1) * tq)

    kc3 = kc_ref[0, 0]
    vcT = vcT_ref[0, 0]
    cmp_end = lax.broadcasted_iota(jnp.int32, (n_cmp, 1), 0) * CMP_STRIDE + (CMP_BLOCK - 1)
    dist_c = (t_row - cmp_end).astype(F32)
    valid_c = dist_c >= 0.0
    piece = lax.broadcasted_iota(jnp.int32, (16, 1), 0)
    log2e_pieces = _bf16_pieces(LOG2E, N_PIECES)
    log2e_col = jnp.zeros((16, 1), F32)
    for k, value in enumerate(log2e_pieces):
        log2e_col = jnp.where((piece == k) | (piece == N_PIECES + k), value, log2e_col)
    for hh in range(HEADS_PER_GROUP):
        alibi = jnp.broadcast_to(log2e_col * slopes[hh], (16, tq)).astype(BF16)
        q_l2 = (qT_ref[0, head_rows(hh), :] * LOG2E).astype(BF16)
        qa_scr[:, head_lanes(hh)] = jnp.concatenate([q_l2, alibi], axis=0)

    def cmp_scores(hh):
        q_h = qT_ref[0, head_rows(hh), :]
        q_hi = q_h.astype(BF16)
        q_lo = (q_h - q_hi.astype(F32)).astype(BF16)
        q3 = jnp.concatenate([q_hi, q_lo, q_hi, jnp.zeros_like(q_hi)], axis=0)
        return jnp.dot(kc3, q3, preferred_element_type=F32)

    def cmp_attend(p_heads, hh, s):
        s = jnp.where(valid_c, s - slopes[hh] * dist_c, MASK_VALUE)
        m = jnp.max(s, axis=0, keepdims=True)
        p = jnp.where(valid_c, jnp.exp(s - m), 0.0)
        l = jnp.sum(p, axis=0, keepdims=True)
        p = p * (1.0 / jnp.where(l > 0.0, l, 1.0))
        o_c = jnp.dot(vcT, p.astype(BF16), preferred_element_type=F32)
        out_scr[head_rows(hh), :] = gate(hh, 0) * o_c
        p_heads.append(p)

    def select_blocks(p_heads, _):
        p_sum = sum(p_heads[1:], p_heads[0])
        blk = lax.broadcasted_iota(jnp.int32, (n_sel, n_cmp), 0)
        cmp_ix = lax.broadcasted_iota(jnp.int32, (n_sel, n_cmp), 1)
        ratio = SEL_BLOCK // CMP_STRIDE
        extra = CMP_BLOCK // CMP_STRIDE - 1
        overlap_t = jnp.where((cmp_ix >= ratio * blk - extra) & (cmp_ix <= ratio * blk + ratio - 1),
                              1.0, 0.0).astype(BF16)
        imp = jnp.zeros((n_sel, tq), F32)
        rest = p_sum
        for _ in range(3):
            part = rest.astype(BF16)
            imp = imp + jnp.dot(overlap_t, part, preferred_element_type=F32)
            rest = rest - part.astype(F32)

        j_col = lax.broadcasted_iota(jnp.int32, (n_sel, 1), 0)
        forced = (j_col == 0) | (j_col == lax.shift_right_logical(t_row, SEL_SHIFT))
        future = j_col * SEL_BLOCK > t_row
        score = jnp.where(forced, jnp.inf, jnp.where(future, -jnp.inf, imp))
        rank = jnp.zeros((n_sel, tq), F32)
        for jp in range(n_sel):
            other = score[jp:jp + 1, :]
            wins_tie = jnp.where(j_col > jp, 1.0, 0.0)
            rank = rank + jnp.where(other > score, 1.0, jnp.where(other == score, wins_tie, 0.0))
        neg = jnp.where(rank < float(min(SEL_TOP_K, n_sel)), 0.0, MASK_VALUE)
        pad_rows = jnp.zeros((BF16_ROWS - SEL_PER_KEY_BLOCK, tq), F32)
        for c in range(n_sel // SEL_PER_KEY_BLOCK):
            rows = neg[c * SEL_PER_KEY_BLOCK:(c + 1) * SEL_PER_KEY_BLOCK, :]
            negm_scr[c] = jnp.concatenate([rows, pad_rows], axis=0).astype(BF16)
            used_smem[c] = (jnp.max(rows) > 0.5 * MASK_VALUE).astype(jnp.int32)

    unit_lanes = HEADS_PER_UNIT * tq
    rel = ((lax.broadcasted_iota(jnp.int32, (KEY_BLOCK, unit_lanes), 1) & (tq - 1))
           - lax.broadcasted_iota(jnp.int32, (KEY_BLOCK, unit_lanes), 0))
    zero_tail = jnp.zeros((KEY_AUG - QUERY_AUG, unit_lanes), BF16)
    no_mask = jnp.zeros((BF16_ROWS, unit_lanes), BF16)
    all_masked = jnp.full((BF16_ROWS, unit_lanes), MASK_VALUE, BF16)

    def unit_cols(hu):
        return slice(hu * unit_lanes, (hu + 1) * unit_lanes)

    def block_mask(kb):
        return jnp.concatenate([negm_scr[kb]] * HEADS_PER_UNIT, axis=1)

    sel_even, sel_odd, win_prev, win_diag = range(4)

    def scores(job, hh):
        k_ref, _, kb, enabled, _, _ = job
        k0 = pl.multiple_of(kb * KEY_BLOCK, KEY_BLOCK)
        k_aug = k_ref[0, 0, 0, pl.ds(k0, KEY_BLOCK), :]
        if enabled is None:
            mask_tile = no_mask
        elif enabled is True:
            mask_tile = block_mask(kb)
        else:
            mask_tile = jnp.where(enabled, block_mask(kb), all_masked)
        q_aug = jnp.concatenate([qa_scr[:, unit_cols(hh)], mask_tile, zero_tail], axis=0)
        return jnp.dot(k_aug, q_aug, preferred_element_type=F32)

    def accumulate(job, first, hh, s):
        _, v_ref, kb, _, valid, slot = job
        cols = unit_cols(hh)
        if valid is not None:
            s = jnp.where(valid, s, MASK_VALUE)
        m_new = jnp.max(s, axis=0, keepdims=True)
        if not first:
            m_old = m_scr[slot, :, cols]
            m_new = jnp.maximum(m_old, m_new)
        p = jnp.exp2(s - jnp.maximum(m_new, 0.1 * MASK_VALUE))
        pv = jnp.dot(v_ref[0, 0, 0, kb], p.astype(BF16), preferred_element_type=F32)
        if not first:
            pv = jnp.exp2(m_old - m_new) * acc_scr[slot, :, cols] + pv
        acc_scr[slot, :, cols] = pv
        m_scr[slot, :, cols] = m_new

    def pipeline(units):
        issue = lambda unit: None if unit[0] is None else unit[0]()
        pending = [issue(unit) for unit in units[:SCORES_AHEAD]]
        for n, unit in enumerate(units):
            if n + SCORES_AHEAD < len(units):
                pending.append(issue(units[n + SCORES_AHEAD]))
            unit[1](pending.pop(0))

    def attention_units(jobs, first):
        return [(functools.partial(scores, job, hu), functools.partial(accumulate, job, first, hu))
                for job in jobs for hu in range(HEADS_PER_GROUP // HEADS_PER_UNIT)]

    def finish(branch, slot_a, slot_b):
        m_a = m_scr[slot_a]
        m_b = m_scr[slot_b]
        m = jnp.maximum(m_a, m_b)
        w_a = jnp.exp2(m_a - m)
        w_b = jnp.exp2(m_b - m)
        for hh in range(HEADS_PER_GROUP):
            cols = head_lanes(hh)
            acc = acc_scr[slot_a, :, cols] * w_a[:, cols] + acc_scr[slot_b, :, cols] * w_b[:, cols]
            l = acc[HEAD_DIM:HEAD_DIM + 1, :]
            o_b = acc[:HEAD_DIM, :] * (1.0 / jnp.where(l > 0.0, l, 1.0))
            out_scr[head_rows(hh), :] = out_scr[head_rows(hh), :] + gate(hh, branch) * o_b

    prev = jnp.maximum(i - 1, 0)
    causal = rel >= 0
    off_without_prev = jnp.where(i > 0, 0, WINDOW + KEY_BLOCK)
    dist_prev = rel + KEY_BLOCK + off_without_prev
    p_heads = []
    pipeline(
        [(functools.partial(cmp_scores, hh), functools.partial(cmp_attend, p_heads, hh))
         for hh in range(HEADS_PER_GROUP)]
        + [(None, functools.partial(select_blocks, p_heads))]
        + attention_units([(kw_ref, vw_ref, prev, None, dist_prev < WINDOW, win_prev),
                           (kw_ref, vw_ref, i, None, causal, win_diag),
                           (ks_ref, vs_ref, prev, i > 0, None, sel_even),
                           (ks_ref, vs_ref, i, True, causal, sel_odd)], first=True))

    n_todo = jnp.int32(0)
    for c in range(used_smem.shape[0]):
        take = jnp.logical_and(c < i - 1, used_smem[c] > 0)

        @pl.when(take)
        def _(c=c, slot=n_todo):
            todo_smem[slot] = c

        n_todo = n_todo + take.astype(jnp.int32)
    todo_smem[n_todo] = 0

    def past_pair(j, carry):
        pipeline(attention_units([(ks_ref, vs_ref, todo_smem[2 * j], True, None, sel_even),
                                  (ks_ref, vs_ref, todo_smem[2 * j + 1], 2 * j + 1 < n_todo, None, sel_odd)],
                                 first=False))
        return carry

    lax.fori_loop(0, lax.shift_right_logical(n_todo + 1, 1), past_pair, 0)
    finish(1, sel_even, sel_odd)
    finish(2, win_prev, win_diag)

    o_ref[0] = out_scr[...].T


def _nsa(qT, kc3, vcT, k_aug, v_aug, bgT, *, tq):
    bsz, _, seq = qT.shape
    assert tq == KEY_BLOCK == WINDOW and seq % tq == 0
    n_cmp = kc3.shape[2]
    n_sel = seq // SEL_BLOCK
    n_kb = seq // KEY_BLOCK
    gw = HEADS_PER_GROUP * HEAD_DIM
    bg4 = bgT.reshape(bsz, N_KV_GROUPS, GATE_ROWS, seq)
    k_spec = lambda s: pl.BlockSpec((1, 1, 1, seq, KEY_AUG), lambda b, g, i: (b, s, g, 0, 0))
    v_spec = lambda s: pl.BlockSpec((1, 1, 1, n_kb, VAL_AUG, KEY_BLOCK), lambda b, g, i: (b, s, g, 0, 0, 0))
    return pl.pallas_call(
        functools.partial(_nsa_kernel, tq=tq, n_cmp=n_cmp, n_sel=n_sel),
        grid=(bsz, N_KV_GROUPS, seq // tq),
        in_specs=[
            pl.BlockSpec((1, gw, tq), lambda b, g, i: (b, g, i)),
            pl.BlockSpec((1, 1, n_cmp, 4 * HEAD_DIM), lambda b, g, i: (b, g, 0, 0)),
            pl.BlockSpec((1, 1, HEAD_DIM, n_cmp), lambda b, g, i: (b, g, 0, 0)),
            k_spec(0), v_spec(0),
            k_spec(1), v_spec(1),
            pl.BlockSpec((1, 1, GATE_ROWS, tq), lambda b, g, i: (b, g, 0, i)),
        ],
        out_specs=pl.BlockSpec((1, tq, gw), lambda b, g, i: (b, i, g)),
        out_shape=jax.ShapeDtypeStruct((bsz, seq, ATTN_WIDTH), F32),
        scratch_shapes=[
            pltpu.VMEM((n_kb, 16, tq), BF16),
            pltpu.VMEM((HEAD_DIM + 16, HEADS_PER_GROUP * tq), BF16),
            pltpu.VMEM((4, 1, HEADS_PER_GROUP * tq), F32),
            pltpu.VMEM((4, VAL_AUG, HEADS_PER_GROUP * tq), F32),
            pltpu.VMEM((gw, tq), F32),
            pltpu.SMEM((n_kb,), jnp.int32),
            pltpu.SMEM((n_kb + 1,), jnp.int32),
        ],
        compiler_params=_params(3),
        name="nsa",
    )(qT, kc3, vcT, k_aug, v_aug, k_aug, v_aug, bg4)


def _merge_kernel(x_ref, g_ref, o_ref_in, sc_ref, sh_ref, gt_ref, wm_ref, wl_ref, wg_ref, wn_ref, wo_ref,
                  lg_ref, lb_ref, y_ref):
    tm = x_ref.shape[1]
    row_blocks = [slice(r, r + ROW_BLOCK) for r in range(0, tm, ROW_BLOCK)]
    halves = [(0, D_MODEL // 2), (D_MODEL // 2, D_MODEL)]
    dot = functools.partial(jnp.dot, preferred_element_type=F32)

    def first_stage(rows):
        g = g_ref[0, rows, :].astype(BF16)
        o = o_ref_in[0, rows, :].astype(BF16)
        branch = [(dot(g, wl_ref[:, lo:hi]), dot(g, wg_ref[:, lo:hi]), dot(o, wn_ref[:, lo:hi])) for lo, hi in halves]
        x = x_ref[0, rows, :]
        u = (x * (1.0 + sc_ref[0]) + sh_ref[0]).astype(BF16)
        gates = [(dot(u, wm_ref[:, lo:hi]), dot(u, wm_ref[:, D_MODEL + lo:D_MODEL + hi])) for lo, hi in halves]
        return x, branch, gates

    def second_stage(rows, x, branch, gates):
        mix = None
        for (lo, hi), (lin, gate, z_nsa), (m_ssm, m_nsa) in zip(halves, branch, gates):
            merged = _sigmoid(m_ssm) * (lin * _sigmoid(gate)) + _sigmoid(m_nsa) * z_nsa
            part = dot(merged.astype(BF16), wo_ref[lo:hi, :])
            mix = part if mix is None else mix + part
        y = DEEPNORM_ALPHA * x + (1.0 + gt_ref[0]) * mix
        y_ref[0, rows, :] = _layer_norm(y, lg_ref[...], lb_ref[...])

    staged = None
    for rows in row_blocks + [None]:
        upcoming = first_stage(rows) if rows is not None else None
        if staged is not None:
            second_stage(*staged)
        staged = (rows,) + upcoming if rows is not None else None


def _merge(x, g_ssm, o_nsa, scale1, shift1, gate1, w_m, w_lin, w_gate, w_nsa, w_out, ln_g, ln_b, *, tm):
    bsz, seq, _ = x.shape
    row = pl.BlockSpec((1, 1, D_MODEL), lambda b, i: (b, 0, 0))
    full = lambda a: pl.BlockSpec(a.shape, lambda b, i: (0, 0), pipeline_mode=pl.Buffered(1))
    return pl.pallas_call(
        _merge_kernel,
        grid=(bsz, seq // tm),
        in_specs=[
            pl.BlockSpec((1, tm, D_MODEL), lambda b, i: (b, i, 0)),
            pl.BlockSpec((1, tm, SSM_WIDTH), lambda b, i: (b, i, 0)),
            pl.BlockSpec((1, tm, ATTN_WIDTH), lambda b, i: (b, i, 0)),
            row, row, row,
            full(w_m), full(w_lin), full(w_gate), full(w_nsa), full(w_out), full(ln_g), full(ln_b),
        ],
        out_specs=pl.BlockSpec((1, tm, D_MODEL), lambda b, i: (b, i, 0)),
        out_shape=jax.ShapeDtypeStruct(x.shape, F32),
        compiler_params=_params(2),
        name="merge",
    )(x, g_ssm, o_nsa, scale1, shift1, gate1, w_m, w_lin, w_gate, w_nsa, w_out, ln_g, ln_b)


def _ffn_kernel(x_ref, sc_ref, sh_ref, gt_ref, wg_ref, wu_ref, wd_ref, lg_ref, lb_ref, y_ref, *, n_chunks):
    tm = x_ref.shape[1]
    row_blocks = [slice(r, r + ROW_BLOCK) for r in range(0, tm, ROW_BLOCK)]
    tiles = FFN_HIDDEN // MXU_WIDTH
    edges = [MXU_WIDTH * ((tiles * c + n_chunks - 1) // n_chunks) for c in range(n_chunks + 1)]
    chunks = [slice(lo, hi) for lo, hi in zip(edges[:-1], edges[1:])]
    dot = functools.partial(jnp.dot, preferred_element_type=F32)

    def gate_up(rows):
        x = x_ref[0, rows, :]
        u = (x * (1.0 + sc_ref[0]) + sh_ref[0]).astype(BF16)
        return x, [(dot(u, wg_ref[:, cols]), dot(u, wu_ref[:, cols])) for cols in chunks]

    def down(rows, x, products):
        ffn = None
        for cols, (a, b) in zip(chunks, products):
            part = dot(((a * _sigmoid(a)) * b).astype(BF16), wd_ref[cols, :])
            ffn = part if ffn is None else ffn + part
        y = DEEPNORM_ALPHA * x + (1.0 + gt_ref[0]) * ffn
        y_ref[0, rows, :] = _layer_norm(y, lg_ref[...], lb_ref[...])

    staged = None
    for rows in row_blocks + [None]:
        upcoming = gate_up(rows) if rows is not None else None
        if staged is not None:
            down(*staged)
        staged = (rows,) + upcoming if rows is not None else None


def _ffn(x, scale2, shift2, gate2, w_gate, w_up, w_down, ln_g, ln_b, *, tm, n_chunks):
    bsz, seq, _ = x.shape
    row = pl.BlockSpec((1, 1, D_MODEL), lambda b, i: (b, 0, 0))
    resident = lambda a: pl.BlockSpec(a.shape, lambda b, i: (0, 0), pipeline_mode=pl.Buffered(1))
    return pl.pallas_call(
        functools.partial(_ffn_kernel, n_chunks=n_chunks),
        grid=(bsz, seq // tm),
        in_specs=[
            pl.BlockSpec((1, tm, D_MODEL), lambda b, i: (b, i, 0)),
            row, row, row,
            resident(w_gate), resident(w_up), resident(w_down), resident(ln_g), resident(ln_b),
        ],
        out_specs=pl.BlockSpec((1, tm, D_MODEL), lambda b, i: (b, i, 0)),
        out_shape=jax.ShapeDtypeStruct(x.shape, F32),
        compiler_params=_params(2),
        name="ffn",
    )(x, scale2, shift2, gate2, w_gate, w_up, w_down, ln_g, ln_b)


def _layer(x, c, w_ada, b_ada, w_in, ssm_a_re, ssm_a_im, ssm_log_dt, ssm_b_re, ssm_b_im, ssm_c_re, ssm_c_im,
           ssm_d, w_glu_lin, w_glu_gate, cmp_pe_k, cmp_pe_v, w_cmp_k1, w_cmp_k2, w_cmp_v1, w_cmp_v2,
           w_nsa_proj, w_out, ln1_g, ln1_b, w_ffn_gate, w_ffn_up, w_ffn_down, ln2_g, ln2_b):
    bsz, seq, _ = x.shape
    tq = min(KEY_BLOCK, seq)
    tm = min(512, seq)
    tm_dense = min(1024, seq)

    mod = _ada_mod(c, w_ada, b_ada).reshape(6, bsz, 1, D_MODEL)
    shift1, scale1, gate1, shift2, scale2, gate2 = (mod[k] for k in range(6))

    w_nat, w_tr, w_mg = _split_w_in(w_in)
    u_ssm, kv_cmp, k_aug, qT, v_aug, bgT = _in_proj(x, scale1, shift1, w_nat, w_tr, tm=tm)

    mats = _ssm_matrices(ssm_a_re, ssm_a_im, ssm_log_dt, ssm_b_re, ssm_b_im, ssm_c_re, ssm_c_im, ssm_d)
    g_ssm = _ssm(u_ssm, mats)

    kc3, vcT = _compress(kv_cmp,
                        cmp_pe_k.reshape(1, CMP_BLOCK * HEAD_DIM), cmp_pe_v.reshape(1, CMP_BLOCK * HEAD_DIM),
                        w_cmp_k1.reshape(CMP_BLOCK * HEAD_DIM, -1), w_cmp_k2,
                        w_cmp_v1.reshape(CMP_BLOCK * HEAD_DIM, -1), w_cmp_v2)
    o_nsa = _nsa(qT, kc3, vcT, k_aug, v_aug, bgT, tq=tq)

    x1 = _merge(x, g_ssm, o_nsa, scale1, shift1, gate1, w_mg,
                w_glu_lin.astype(BF16), w_glu_gate.astype(BF16), w_nsa_proj.astype(BF16), w_out.astype(BF16),
                ln1_g.reshape(1, D_MODEL), ln1_b.reshape(1, D_MODEL), tm=tm_dense)
    return _ffn(x1, scale2, shift2, gate2, w_ffn_gate.astype(BF16), w_ffn_up.astype(BF16),
                w_ffn_down.astype(BF16), ln2_g.reshape(1, D_MODEL), ln2_b.reshape(1, D_MODEL), tm=tm_dense, n_chunks=2)


def kernel(x, c, w_ada, b_ada, w_in, ssm_a_re, ssm_a_im, ssm_log_dt, ssm_b_re, ssm_b_im, ssm_c_re, ssm_c_im,
           ssm_d, w_glu_lin, w_glu_gate, cmp_pe_k, cmp_pe_v, w_cmp_k1, w_cmp_k2, w_cmp_v1, w_cmp_v2,
           w_nsa_proj, w_out, ln1_g, ln1_b, w_ffn_gate, w_ffn_up, w_ffn_down, ln2_g, ln2_b):
    for l in range(w_ada.shape[0]):
        x = _layer(x, c, w_ada[l], b_ada[l], w_in[l], ssm_a_re[l], ssm_a_im[l], ssm_log_dt[l],
                   ssm_b_re[l], ssm_b_im[l], ssm_c_re[l], ssm_c_im[l], ssm_d[l],
                   w_glu_lin[l], w_glu_gate[l], cmp_pe_k[l], cmp_pe_v[l],
                   w_cmp_k1[l], w_cmp_k2[l], w_cmp_v1[l], w_cmp_v2[l], w_nsa_proj[l], w_out[l],
                   ln1_g[l], ln1_b[l], w_ffn_gate[l], w_ffn_up[l], w_ffn_down[l], ln2_g[l], ln2_b[l])
    return x
```

```python
import functools
import math

import jax
import jax.numpy as jnp
from jax import lax
from jax.experimental import pallas as pl
from jax.experimental.pallas import tpu as pltpu

F32 = jnp.float32
BF16 = jnp.bfloat16
HIGHEST = lax.Precision.HIGHEST

D_MODEL = 1024
SSM_WIDTH = D_MODEL // 2
SSM_GROUP_SIZE = 16
SSM_GROUPS = SSM_WIDTH // SSM_GROUP_SIZE
SSM_STATE = 64
SSM_CHUNK = 16
SSM_GROUP_TILE = 8
SSM_STATE_ROW_PAD = 8
N_HEADS = 8
HEAD_DIM = 64
N_KV_GROUPS = 2
HEADS_PER_GROUP = N_HEADS // N_KV_GROUPS
ATTN_WIDTH = N_HEADS * HEAD_DIM
KV_WIDTH = N_KV_GROUPS * HEAD_DIM
CMP_BLOCK = 32
CMP_STRIDE = 16
CMP_HIDDEN = HEAD_DIM
SEL_BLOCK = 64
SEL_TOP_K = 8
WINDOW = 256
N_NSA_BRANCHES = 3
GATE_ROWS = 16
FFN_HIDDEN = (8 * D_MODEL + 3 * 256 - 1) // (3 * 256) * 256
DEEPNORM_ALPHA = 2.0 ** 0.25
LN_EPS = 1e-5
MASK_VALUE = -1e30

VMEM_LIMIT = 56 * 1024 * 1024
MXU_WIDTH = 256
BF16_ROWS = 16
ROW_BLOCK = 256

KEY_BLOCK = 256
SEL_SHIFT = SEL_BLOCK.bit_length() - 1
SEL_PER_KEY_BLOCK = KEY_BLOCK // SEL_BLOCK
KEY_AUG = 128
ALIBI_COL = HEAD_DIM
N_PIECES = 4
MASK_COL = HEAD_DIM + 16
QUERY_AUG = HEAD_DIM + 32
VAL_AUG = HEAD_DIM + 16
LOG2E = math.log2(math.e)
HEADS_PER_UNIT = 1
SCORES_AHEAD = 4

NAT_COLS = SSM_WIDTH + 4 * KV_WIDTH
TR_ROWS = ATTN_WIDTH + 2 * KV_WIDTH + N_KV_GROUPS * GATE_ROWS


def _bf16_pieces(value, n):
    pieces = []
    rest = value
    for _ in range(n):
        mant, expo = math.frexp(rest)
        piece = math.ldexp(round(mant * 256.0) / 256.0, expo)
        pieces.append(piece)
        rest -= piece
    return pieces


def _sigmoid(x):
    return 1.0 / (1.0 + jnp.exp(-x))


def _gelu(x):
    c = math.sqrt(2.0 / math.pi)
    return 0.5 * x * (1.0 + jnp.tanh(c * (x + 0.044715 * (x * x * x))))


def _layer_norm(y, gain, bias):
    mu = jnp.mean(y, axis=-1, keepdims=True)
    d = y - mu
    var = jnp.mean(d * d, axis=-1, keepdims=True)
    return d * lax.rsqrt(var + LN_EPS) * gain + bias


def _params(n_axes):
    return pltpu.CompilerParams(dimension_semantics=("arbitrary",) * n_axes, vmem_limit_bytes=VMEM_LIMIT)


def _ada_kernel(c_ref, w_ref, b_ref, o_ref):
    c = c_ref[...]
    a = c * _sigmoid(c)
    w = w_ref[...].astype(BF16)
    a_hi = a.astype(BF16)
    a_lo = (a - a_hi.astype(F32)).astype(BF16)
    mod = (jnp.dot(a_hi, w, preferred_element_type=F32) + jnp.dot(a_lo, w, preferred_element_type=F32)) + b_ref[...]
    for k in range(o_ref.shape[0]):
        o_ref[k] = mod[:, k * D_MODEL:(k + 1) * D_MODEL]


def _ada_mod(c, w_ada, b_ada):
    bsz = c.shape[0]
    per_step = 2
    return pl.pallas_call(
        _ada_kernel,
        grid=(6 // per_step,),
        in_specs=[
            pl.BlockSpec((bsz, D_MODEL), lambda j: (0, 0)),
            pl.BlockSpec((D_MODEL, per_step * D_MODEL), lambda j: (0, j)),
            pl.BlockSpec((1, per_step * D_MODEL), lambda j: (0, j)),
        ],
        out_specs=pl.BlockSpec((per_step, bsz, D_MODEL), lambda j: (j, 0, 0)),
        out_shape=jax.ShapeDtypeStruct((6, bsz, D_MODEL), F32),
        compiler_params=_params(1),
        name="ada_mod",
    )(c, w_ada, b_ada.reshape(1, 6 * D_MODEL))


def _in_proj_kernel(x_ref, sc_ref, sh_ref, wn_ref, wt_ref, ussm_ref, kc_ref, kp_ref, qT_ref, vp_ref, bgT_ref):
    i = pl.program_id(1)
    tm = x_ref.shape[1]
    u = (x_ref[0] * (1.0 + sc_ref[0]) + sh_ref[0]).astype(BF16)
    nat = jnp.dot(u, wn_ref[...], preferred_element_type=F32)
    ussm_ref[0] = nat[:, :SSM_WIDTH]
    for s in range(2):
        for g in range(N_KV_GROUPS):
            lo = SSM_WIDTH + s * KV_WIDTH + g * HEAD_DIM
            kc_ref[0, s, g] = nat[:, lo:lo + HEAD_DIM]
    pos = i * tm + lax.broadcasted_iota(jnp.int32, (tm, KEY_AUG), 0)
    col = lax.broadcasted_iota(jnp.int32, (1, KEY_AUG), 1)
    blk = lax.shift_right_logical(pos, SEL_SHIFT)
    in_a = (col >= ALIBI_COL) & (col < ALIBI_COL + N_PIECES)
    in_b = (col >= ALIBI_COL + N_PIECES) & (col < ALIBI_COL + 2 * N_PIECES)
    hot = col == MASK_COL + (blk & (SEL_PER_KEY_BLOCK - 1))
    aux = jnp.where(in_a, blk * SEL_BLOCK, jnp.where(in_b, pos & (SEL_BLOCK - 1), jnp.where(hot, 1, 0))).astype(F32)
    head_lanes = col < HEAD_DIM
    for s in range(2):
        lo = SSM_WIDTH + (2 + s) * KV_WIDTH
        both = nat[:, lo:lo + KV_WIDTH]
        for g in range(N_KV_GROUPS):
            keys = both if g == 0 else pltpu.roll(both, HEAD_DIM, 1)
            kp_ref[0, s, g] = jnp.where(head_lanes, keys, aux).astype(BF16)
    tr = lax.dot_general(wt_ref[...], u, (((1,), (1,)), ((), ())), preferred_element_type=F32)
    qT_ref[0] = tr[:ATTN_WIDTH] * (HEAD_DIM ** -0.5)
    ones_rows = jnp.where(lax.broadcasted_iota(jnp.int32, (VAL_AUG - HEAD_DIM, tm), 0) == 0, 1.0, 0.0)
    for s in range(2):
        for g in range(N_KV_GROUPS):
            lo = ATTN_WIDTH + (s * N_KV_GROUPS + g) * HEAD_DIM
            v_aug = jnp.concatenate([tr[lo:lo + HEAD_DIM], ones_rows], axis=0).astype(BF16)
            for c in range(tm // KEY_BLOCK):
                vp_ref[0, s, g, c] = v_aug[:, c * KEY_BLOCK:(c + 1) * KEY_BLOCK]
    bgT_ref[0] = _sigmoid(tr[ATTN_WIDTH + 2 * KV_WIDTH:])


def _in_proj(x, scale1, shift1, w_nat, w_tr, *, tm):
    bsz, seq, _ = x.shape
    n_t = seq // tm
    n_kb = seq // KEY_BLOCK
    out_shapes = (
        jax.ShapeDtypeStruct((bsz, seq, SSM_WIDTH), F32),
        jax.ShapeDtypeStruct((bsz, 2, N_KV_GROUPS, seq, HEAD_DIM), F32),
        jax.ShapeDtypeStruct((bsz, 2, N_KV_GROUPS, seq, KEY_AUG), BF16),
        jax.ShapeDtypeStruct((bsz, ATTN_WIDTH, seq), F32),
        jax.ShapeDtypeStruct((bsz, 2, N_KV_GROUPS, n_kb, VAL_AUG, KEY_BLOCK), BF16),
        jax.ShapeDtypeStruct((bsz, N_KV_GROUPS * GATE_ROWS, seq), F32),
    )
    return pl.pallas_call(
        _in_proj_kernel,
        grid=(bsz, n_t),
        in_specs=[
            pl.BlockSpec((1, tm, D_MODEL), lambda b, i: (b, i, 0)),
            pl.BlockSpec((1, 1, D_MODEL), lambda b, i: (b, 0, 0)),
            pl.BlockSpec((1, 1, D_MODEL), lambda b, i: (b, 0, 0)),
            pl.BlockSpec((D_MODEL, NAT_COLS), lambda b, i: (0, 0)),
            pl.BlockSpec((TR_ROWS, D_MODEL), lambda b, i: (0, 0)),
        ],
        out_specs=(
            pl.BlockSpec((1, tm, SSM_WIDTH), lambda b, i: (b, i, 0)),
            pl.BlockSpec((1, 2, N_KV_GROUPS, tm, HEAD_DIM), lambda b, i: (b, 0, 0, i, 0)),
            pl.BlockSpec((1, 2, N_KV_GROUPS, tm, KEY_AUG), lambda b, i: (b, 0, 0, i, 0)),
            pl.BlockSpec((1, ATTN_WIDTH, tm), lambda b, i: (b, 0, i)),
            pl.BlockSpec((1, 2, N_KV_GROUPS, tm // KEY_BLOCK, VAL_AUG, KEY_BLOCK), lambda b, i: (b, 0, 0, i, 0, 0)),
            pl.BlockSpec((1, N_KV_GROUPS * GATE_ROWS, tm), lambda b, i: (b, 0, i)),
        ),
        out_shape=out_shapes,
        compiler_params=_params(2),
        name="in_proj",
    )(x, scale1, shift1, w_nat, w_tr)


def _split_w_in(w_in):
    o_q = SSM_WIDTH
    o_kv = o_q + ATTN_WIDTH
    o_bg = o_kv + 6 * KV_WIDTH
    o_mg = o_bg + N_NSA_BRANCHES * N_HEADS
    kv = [w_in[:, o_kv + s * KV_WIDTH:o_kv + (s + 1) * KV_WIDTH] for s in range(6)]
    w_nat = jnp.concatenate([w_in[:, :o_q], kv[0], kv[1], kv[2], kv[4]], axis=1)
    per_group = HEADS_PER_GROUP * N_NSA_BRANCHES
    bg = w_in[:, o_bg:o_mg].reshape(D_MODEL, N_KV_GROUPS, per_group)
    bg = jnp.pad(bg, ((0, 0), (0, 0), (0, GATE_ROWS - per_group))).reshape(D_MODEL, N_KV_GROUPS * GATE_ROWS)
    w_tr = jnp.concatenate([w_in[:, o_q:o_kv], kv[3], kv[5], bg], axis=1).T
    return w_nat.astype(BF16), w_tr.astype(BF16), w_in[:, o_mg:].astype(BF16)


def _ssm_prep_kernel(a_ref, logdt_ref, bT_ref, c_ref, d_ref, mT_ref, pT_ref, qT_ref, a16_ref):
    t_n = SSM_CHUNK
    c_n = SSM_GROUP_SIZE
    n_p = SSM_STATE
    rows = t_n * c_n
    dt = jnp.exp(logdt_ref[0])
    a_re = a_ref[0, 0]
    a_im = a_ref[0, 1]
    decay = jnp.exp(a_re * dt)
    ab_re = decay * jnp.cos(a_im * dt)
    ab_im = decay * jnp.sin(a_im * dt)
    n_re = ab_re - 1.0
    denom = a_re * a_re + a_im * a_im
    f_re = (n_re * a_re + ab_im * a_im) / denom
    f_im = (ab_im * a_re - n_re * a_im) / denom
    powers = [(jnp.ones_like(ab_re), jnp.zeros_like(ab_re))]
    for _ in range(t_n):
        p_re, p_im = powers[-1]
        powers.append((p_re * ab_re - p_im * ab_im, p_re * ab_im + p_im * ab_re))

    def per_token(first, step):
        picks = [powers[first + step * k] for k in range(t_n)]
        return (jnp.concatenate([jnp.broadcast_to(p[0], (c_n, 2 * n_p)) for p in picks], axis=0),
                jnp.concatenate([jnp.broadcast_to(p[1], (c_n, 2 * n_p)) for p in picks], axis=0))

    def cmul(x_re, x_im, y_re, y_im):
        return x_re * y_re - x_im * y_im, x_re * y_im + x_im * y_re

    bb_re, bb_im = cmul(f_re, f_im, bT_ref[0, 0], bT_ref[0, 1])
    bt_re = jnp.concatenate([bb_re] * t_n, axis=0)
    bt_im = jnp.concatenate([bb_im] * t_n, axis=0)
    ct_re = jnp.concatenate([c_ref[0, 0]] * t_n, axis=0)
    ct_im = jnp.concatenate([c_ref[0, 1]] * t_n, axis=0)

    p_re, p_im = cmul(*per_token(t_n - 1, -1), bt_re, bt_im)
    low_half = lax.broadcasted_iota(jnp.int32, (1, 2 * n_p), 1) < n_p
    pT_ref[0] = jnp.where(low_half, p_re, p_im).T.astype(BF16)

    q_re, q_im = cmul(ct_re, ct_im, *per_token(1, 1))
    qT_ref[0, 0] = q_re[:, :n_p].astype(BF16)
    qT_ref[0, 1] = (-q_im[:, :n_p]).astype(BF16)
    a16_ref[0] = jnp.concatenate([powers[t_n][0][:, :n_p], powers[t_n][1][:, :n_p]], axis=0)

    k_re, k_im = cmul(ct_re, ct_im, *per_token(0, 1))
    nt = (((1,), (1,)), ((), ()))
    strips = (lax.dot_general(_split3(k_re[:, :n_p]), _stack3(bt_re[:, :n_p], axis=1), nt, preferred_element_type=F32)
              - lax.dot_general(_split3(k_im[:, :n_p]), _stack3(bt_im[:, :n_p], axis=1), nt,
                                preferred_element_type=F32))
    lane = lax.broadcasted_iota(jnp.int32, (1, rows), 1)
    s_lane = lax.shift_right_logical(lane, c_n.bit_length() - 1)
    ci_lane = lane & (c_n - 1)
    row = lax.broadcasted_iota(jnp.int32, (rows, 1), 0)
    t_row = lax.shift_right_logical(row, c_n.bit_length() - 1)
    co_row = row & (c_n - 1)
    blocks = []
    for t in range(t_n):
        acc = jnp.zeros((c_n, rows), F32)
        for lag in range(t + 1):
            acc = jnp.where(s_lane == t - lag, strips[lag * c_n:(lag + 1) * c_n, :], acc)
        blocks.append(acc)
    d_rows = jnp.concatenate([d_ref[0]] * t_n, axis=0)
    skip = jnp.where((t_row == s_lane) & (co_row == ci_lane), d_rows, 0.0)
    mT_ref[0] = (jnp.concatenate(blocks, axis=0) + skip).astype(BF16)


def _ssm_matrices(a_re, a_im, log_dt, b_re, b_im, c_re, c_im, d_skip):
    n_g, n_p = a_re.shape
    c_n = SSM_GROUP_SIZE
    rows = SSM_CHUNK * c_n
    twice = lambda x: jnp.concatenate([x, x], axis=-1).astype(F32)
    a = twice(jnp.stack([a_re, a_im], axis=1)).reshape(n_g, 2, 1, 2 * n_p)
    bT = twice(jnp.swapaxes(jnp.stack([b_re, b_im], axis=1), 2, 3))
    c = twice(jnp.stack([c_re, c_im], axis=1))
    spec = lambda shape: pl.BlockSpec((1,) + shape, lambda g: (g,) + (0,) * len(shape))
    mT, pT, qT, a16 = pl.pallas_call(
        _ssm_prep_kernel,
        grid=(n_g,),
        in_specs=[spec((2, 1, 2 * n_p)), spec((1, 1)), spec((2, c_n, 2 * n_p)), spec((2, c_n, 2 * n_p)),
                  spec((c_n, 1))],
        out_specs=(spec((rows, rows)), spec((2 * n_p, rows)), spec((2, rows, n_p)), spec((2, n_p))),
        out_shape=(
            jax.ShapeDtypeStruct((n_g, rows, rows), BF16),
            jax.ShapeDtypeStruct((n_g, 2 * n_p, rows), BF16),
            jax.ShapeDtypeStruct((n_g, 2, rows, n_p), BF16),
            jax.ShapeDtypeStruct((n_g, 2, n_p), F32),
        ),
        compiler_params=_params(1),
        name="ssm_prep",
    )(a, log_dt.astype(F32).reshape(n_g, 1, 1), bT, c, d_skip.astype(F32).reshape(n_g, c_n, 1))
    return mT, pT.reshape(n_g, 2, n_p, rows), qT, a16


def _ssm_kernel(u_ref, mT_ref, pT_ref, qT_ref, a16_ref, o_ref, ut_scr, yt_scr, sre, sim, *, n_chunks):
    n_g = SSM_GROUP_TILE
    t_n = SSM_CHUNK
    c_n = SSM_GROUP_SIZE

    for s in range(t_n):
        x_s = u_ref[0, pl.ds(s, n_chunks, stride=t_n), :]
        ut_scr[:, s * c_n:(s + 1) * c_n, :] = x_s.T.reshape(n_g, c_n, n_chunks).astype(BF16)

    pitch = n_chunks + SSM_STATE_ROW_PAD

    def group_rows(g):
        return pl.ds(g * pitch, n_chunks)

    def chunk_states(g, carry):
        ut = ut_scr[g]
        sre[group_rows(g), :] = jnp.dot(pT_ref[g, 0], ut, preferred_element_type=F32).T
        sim[group_rows(g), :] = jnp.dot(pT_ref[g, 1], ut, preferred_element_type=F32).T
        return carry

    for g in range(n_g):
        chunk_states(g, 0)

    a_r = a16_ref[pl.ds(0, n_g, stride=2), :]
    a_i = a16_ref[pl.ds(1, n_g, stride=2), :]

    def carry_states(c, h):
        h_r, h_i = h
        rows = pl.ds(c, n_g, stride=pitch)
        s_r = sre[rows, :]
        s_i = sim[rows, :]
        sre[rows, :] = h_r
        sim[rows, :] = h_i
        return a_r * h_r - a_i * h_i + s_r, a_r * h_i + a_i * h_r + s_i

    zero = jnp.zeros((n_g, SSM_STATE), F32)
    lax.fori_loop(0, n_chunks, carry_states, (zero, zero), unroll=4)

    nt = (((1,), (1,)), ((), ()))

    def outputs(g, carry):
        yt = jnp.dot(mT_ref[g], ut_scr[g], preferred_element_type=F32)
        yt = yt + lax.dot_general(qT_ref[g, 0], sre[group_rows(g), :].astype(BF16), nt, preferred_element_type=F32)
        yt = yt + lax.dot_general(qT_ref[g, 1], sim[group_rows(g), :].astype(BF16), nt, preferred_element_type=F32)
        yt_scr[g] = _gelu(yt)
        return carry

    for g in range(n_g):
        outputs(g, 0)

    for t in range(t_n):
        z = yt_scr[:, t * c_n:(t + 1) * c_n, :].reshape(n_g * c_n, n_chunks)
        o_ref[0, pl.ds(t, n_chunks, stride=t_n), :] = z.T


def _ssm(u_ssm, mats):
    bsz, seq, width = u_ssm.shape
    n_chunks = seq // SSM_CHUNK
    mT, pT, qT, a16 = mats
    gt = SSM_GROUP_TILE
    n_tiles = SSM_GROUPS // gt
    rows = SSM_CHUNK * SSM_GROUP_SIZE
    a16 = a16.reshape(2 * SSM_GROUPS, SSM_STATE)
    per_tile = lambda a: pl.BlockSpec((gt,) + a.shape[1:], lambda t, b: (t,) + (0,) * (a.ndim - 1))
    return pl.pallas_call(
        functools.partial(_ssm_kernel, n_chunks=n_chunks),
        grid=(n_tiles, bsz),
        in_specs=[
            pl.BlockSpec((1, seq, gt * SSM_GROUP_SIZE), lambda t, b: (b, 0, t)),
            per_tile(mT), per_tile(pT), per_tile(qT),
            pl.BlockSpec((2 * gt, SSM_STATE), lambda t, b: (t, 0)),
        ],
        out_specs=pl.BlockSpec((1, seq, gt * SSM_GROUP_SIZE), lambda t, b: (b, 0, t)),
        out_shape=jax.ShapeDtypeStruct(u_ssm.shape, F32),
        scratch_shapes=[
            pltpu.VMEM((gt, rows, n_chunks), BF16),
            pltpu.VMEM((gt, rows, n_chunks), F32),
            pltpu.VMEM((gt * (n_chunks + SSM_STATE_ROW_PAD), SSM_STATE), F32),
            pltpu.VMEM((gt * (n_chunks + SSM_STATE_ROW_PAD), SSM_STATE), F32),
        ],
        compiler_params=_params(2),
        name="ssm",
    )(u_ssm, mT, pT, qT, a16)


def _split3(x):
    hi = x.astype(BF16)
    lo = (x - hi.astype(F32)).astype(BF16)
    return jnp.concatenate([hi, lo, hi], axis=-1)


def _stack3(w, axis=0):
    hi = w.astype(BF16)
    lo = (w - hi.astype(F32)).astype(BF16)
    return jnp.concatenate([hi, hi, lo], axis=axis)


def _compress_kernel(kx_ref, vx_ref, pek_ref, pev_ref, wk1_ref, wk2_ref, wv1_ref, wv2_ref, kc_ref, vcT_ref):
    def mlp(x_ref, pe_ref, w1_ref, w2_ref):
        n = x_ref.shape[3] // CMP_STRIDE
        x = jnp.concatenate([x_ref[0, 0, 0, pl.ds(l, n, stride=CMP_STRIDE), :] for l in range(CMP_STRIDE)], axis=1)
        both = jnp.dot(_split3(x), w1_ref[...], preferred_element_type=F32)
        pe_both = jnp.dot(_split3(pe_ref[...]), w1_ref[...], preferred_element_type=F32)
        bias = pe_both[0:1, :CMP_HIDDEN] + pe_both[1:2, CMP_HIDDEN:]
        h = _gelu(both[:, :CMP_HIDDEN] + pltpu.roll(both[:, CMP_HIDDEN:], n - 1, 0) + bias)
        return jnp.dot(_split3(h), w2_ref[...], preferred_element_type=F32)

    kc = mlp(kx_ref, pek_ref, wk1_ref, wk2_ref)
    kc_hi = kc.astype(BF16)
    kc_lo = (kc - kc_hi.astype(F32)).astype(BF16)
    kc_ref[0, 0] = jnp.concatenate([kc_hi, kc_hi, kc_lo, jnp.zeros_like(kc_hi)], axis=1)
    vcT_ref[0, 0] = mlp(vx_ref, pev_ref, wv1_ref, wv2_ref).T.astype(BF16)


def _compress(kv, pe_k, pe_v, wk1, wk2, wv1, wv2):
    bsz, _, n_g, seq, width = kv.shape
    n_chunks = seq // CMP_STRIDE
    half = CMP_STRIDE * HEAD_DIM
    side_by_side = lambda w1: _stack3(jnp.concatenate([w1[:half], w1[half:]], axis=1))
    two_rows = lambda pe: jnp.pad(pe.reshape(2, half), ((0, BF16_ROWS - 2), (0, 0)))
    pe_k, pe_v = two_rows(pe_k), two_rows(pe_v)
    wk1, wv1, wk2, wv2 = side_by_side(wk1), side_by_side(wv1), _stack3(wk2), _stack3(wv2)
    blk = lambda s: pl.BlockSpec((1, 1, 1, seq, width), lambda b, g: (b, s, g, 0, 0))
    full = lambda a: pl.BlockSpec(a.shape, lambda b, g: (0, 0))
    return pl.pallas_call(
        _compress_kernel,
        grid=(bsz, n_g),
        in_specs=[blk(0), blk(1), full(pe_k), full(pe_v), full(wk1), full(wk2), full(wv1), full(wv2)],
        out_specs=(
            pl.BlockSpec((1, 1, n_chunks, 4 * HEAD_DIM), lambda b, g: (b, g, 0, 0)),
            pl.BlockSpec((1, 1, HEAD_DIM, n_chunks), lambda b, g: (b, g, 0, 0)),
        ),
        out_shape=(
            jax.ShapeDtypeStruct((bsz, n_g, n_chunks, 4 * HEAD_DIM), BF16),
            jax.ShapeDtypeStruct((bsz, n_g, HEAD_DIM, n_chunks), BF16),
        ),
        compiler_params=_params(2),
        name="compress",
    )(kv, kv, pe_k, pe_v, wk1, wk2, wv1, wv2)


def _nsa_kernel(qT_ref, kc_ref, vcT_ref, ks_ref, vs_ref, kw_ref, vw_ref, bg_ref, o_ref,
                negm_scr, qa_scr, m_scr, acc_scr, out_scr, used_smem, todo_smem, *, tq, n_cmp, n_sel):
    g = pl.program_id(1)
    i = pl.program_id(2)
    t_row = i * tq + lax.broadcasted_iota(jnp.int32, (1, tq), 1)
    slopes = [jnp.where(g == 0, 2.0 ** -(hh + 1), 2.0 ** -(HEADS_PER_GROUP + hh + 1)).astype(F32)
              for hh in range(HEADS_PER_GROUP)]

    def gate(hh, branch):
        r = hh * N_NSA_BRANCHES + branch
        return bg_ref[0, 0, r:r + 1, :]

    def head_rows(hh):
        return slice(hh * HEAD_DIM, (hh + 1) * HEAD_DIM)

    def head_lanes(hh):
        return slice(hh * tq, (hh + 1) * tq)

    kc3 = kc_ref[0, 0]
    vcT = vcT_ref[0, 0]
    cmp_end = lax.broadcasted_iota(jnp.int32, (n_cmp, 1), 0) * CMP_STRIDE + (CMP_BLOCK - 1)
    dist_c = (t_row - cmp_end).astype(F32)
    valid_c = dist_c >= 0.0
    piece = lax.broadcasted_iota(jnp.int32, (16, 1), 0)
    log2e_pieces = _bf16_pieces(LOG2E, N_PIECES)
    log2e_col = jnp.zeros((16, 1), F32)
    for k, value in enumerate(log2e_pieces):
        log2e_col = jnp.where((piece == k) | (piece == N_PIECES + k), value, log2e_col)
    for hh in range(HEADS_PER_GROUP):
        alibi = jnp.broadcast_to(log2e_col * slopes[hh], (16, tq)).astype(BF16)
        q_l2 = (qT_ref[0, head_rows(hh), :] * LOG2E).astype(BF16)
        qa_scr[:, head_lanes(hh)] = jnp.concatenate([q_l2, alibi], axis=0)

    def cmp_scores(hh):
        q_h = qT_ref[0, head_rows(hh), :]
        q_hi = q_h.astype(BF16)
        q_lo = (q_h - q_hi.astype(F32)).astype(BF16)
        q3 = jnp.concatenate([q_hi, q_lo, q_hi, jnp.zeros_like(q_hi)], axis=0)
        return jnp.dot(kc3, q3, preferred_element_type=F32)

    def cmp_attend(p_heads, hh, s):
        s = jnp.where(valid_c, s - slopes[hh] * dist_c, MASK_VALUE)
        m = jnp.max(s, axis=0, keepdims=True)
        p = jnp.where(valid_c, jnp.exp(s - m), 0.0)
        l = jnp.sum(p, axis=0, keepdims=True)
        p = p * (1.0 / jnp.where(l > 0.0, l, 1.0))
        o_c = jnp.dot(vcT, p.astype(BF16), preferred_element_type=F32)
        out_scr[head_rows(hh), :] = gate(hh, 0) * o_c
        p_heads.append(p)

    def select_blocks(p_heads, _):
        p_sum = sum(p_heads[1:], p_heads[0])
        blk = lax.broadcasted_iota(jnp.int32, (n_sel, n_cmp), 0)
        cmp_ix = lax.broadcasted_iota(jnp.int32, (n_sel, n_cmp), 1)
        ratio = SEL_BLOCK // CMP_STRIDE
        extra = CMP_BLOCK // CMP_STRIDE - 1
        overlap_t = jnp.where((cmp_ix >= ratio * blk - extra) & (cmp_ix <= ratio * blk + ratio - 1),
                              1.0, 0.0).astype(BF16)
        imp = jnp.zeros((n_sel, tq), F32)
        rest = p_sum
        for _ in range(3):
            part = rest.astype(BF16)
            imp = imp + jnp.dot(overlap_t, part, preferred_element_type=F32)
            rest = rest - part.astype(F32)

        j_col = lax.broadcasted_iota(jnp.int32, (n_sel, 1), 0)
        forced = (j_col == 0) | (j_col == lax.shift_right_logical(t_row, SEL_SHIFT))
        future = j_col * SEL_BLOCK > t_row
        score = jnp.where(forced, jnp.inf, jnp.where(future, -jnp.inf, imp))
        rank = jnp.zeros((n_sel, tq), F32)
        for jp in range(n_sel):
            other = score[jp:jp + 1, :]
            wins_tie = jnp.where(j_col > jp, 1.0, 0.0)
            rank = rank + jnp.where(other > score, 1.0, jnp.where(other == score, wins_tie, 0.0))
        neg = jnp.where(rank < float(min(SEL_TOP_K, n_sel)), 0.0, MASK_VALUE)
        pad_rows = jnp.zeros((BF16_ROWS - SEL_PER_KEY_BLOCK, tq), F32)
        for c in range(n_sel // SEL_PER_KEY_BLOCK):
            rows = neg[c * SEL_PER_KEY_BLOCK:(c + 1) * SEL_PER_KEY_BLOCK, :]
            negm_scr[c] = jnp.concatenate([rows, pad_rows], axis=0).astype(BF16)
            used_smem[c] = (jnp.max(rows) > 0.5 * MASK_VALUE).astype(jnp.int32)

    unit_lanes = HEADS_PER_UNIT * tq
    rel = ((lax.broadcasted_iota(jnp.int32, (KEY_BLOCK, unit_lanes), 1) & (tq - 1))
           - lax.broadcasted_iota(jnp.int32, (KEY_BLOCK, unit_lanes), 0))
    zero_tail = jnp.zeros((KEY_AUG - QUERY_AUG, unit_lanes), BF16)
    no_mask = jnp.zeros((BF16_ROWS, unit_lanes), BF16)
    all_masked = jnp.full((BF16_ROWS, unit_lanes), MASK_VALUE, BF16)

    def unit_cols(hu):
        return slice(hu * unit_lanes, (hu + 1) * unit_lanes)

    def block_mask(kb):
        return jnp.concatenate([negm_scr[kb]] * HEADS_PER_UNIT, axis=1)

    sel_even, sel_odd, win_prev, win_diag = range(4)

    def scores(job, hh):
        k_ref, _, kb, enabled, _, _ = job
        k0 = pl.multiple_of(kb * KEY_BLOCK, KEY_BLOCK)
        k_aug = k_ref[0, 0, 0, pl.ds(k0, KEY_BLOCK), :]
        if enabled is None:
            mask_tile = no_mask
        elif enabled is True:
            mask_tile = block_mask(kb)
        else:
            mask_tile = jnp.where(enabled, block_mask(kb), all_masked)
        q_aug = jnp.concatenate([qa_scr[:, unit_cols(hh)], mask_tile, zero_tail], axis=0)
        return jnp.dot(k_aug, q_aug, preferred_element_type=F32)

    def accumulate(job, first, hh, s):
        _, v_ref, kb, _, valid, slot = job
        cols = unit_cols(hh)
        if valid is not None:
            s = jnp.where(valid, s, MASK_VALUE)
        m_new = jnp.max(s, axis=0, keepdims=True)
        if not first:
            m_old = m_scr[slot, :, cols]
            m_new = jnp.maximum(m_old, m_new)
        p = jnp.exp2(s - jnp.maximum(m_new, 0.1 * MASK_VALUE))
        pv = jnp.dot(v_ref[0, 0, 0, kb], p.astype(BF16), preferred_element_type=F32)
        if not first:
            pv = jnp.exp2(m_old - m_new) * acc_scr[slot, :, cols] + pv
        acc_scr[slot, :, cols] = pv
        m_scr[slot, :, cols] = m_new

    def pipeline(units):
        issue = lambda unit: None if unit[0] is None else unit[0]()
        pending = [issue(unit) for unit in units[:SCORES_AHEAD]]
        for n, unit in enumerate(units):
            if n + SCORES_AHEAD < len(units):
                pending.append(issue(units[n + SCORES_AHEAD]))
            unit[1](pending.pop(0))

    def attention_units(jobs, first):
        return [(functools.partial(scores, job, hu), functools.partial(accumulate, job, first, hu))
                for job in jobs for hu in range(HEADS_PER_GROUP // HEADS_PER_UNIT)]

    def finish(branch, slot_a, slot_b):
        m_a = m_scr[slot_a]
        m_b = m_scr[slot_b]
        m = jnp.maximum(m_a, m_b)
        w_a = jnp.exp2(m_a - m)
        w_b = jnp.exp2(m_b - m)
        for hh in range(HEADS_PER_GROUP):
            cols = head_lanes(hh)
            acc = acc_scr[slot_a, :, cols] * w_a[:, cols] + acc_scr[slot_b, :, cols] * w_b[:, cols]
            l = acc[HEAD_DIM:HEAD_DIM + 1, :]
            o_b = acc[:HEAD_DIM, :] * (1.0 / jnp.where(l > 0.0, l, 1.0))
            out_scr[head_rows(hh), :] = out_scr[head_rows(hh), :] + gate(hh, branch) * o_b

    prev = jnp.maximum(i - 1, 0)
    causal = rel >= 0
    off_without_prev = jnp.where(i > 0, 0, WINDOW + KEY_BLOCK)
    dist_prev = rel + KEY_BLOCK + off_without_prev
    p_heads = []
    pipeline(
        [(functools.partial(cmp_scores, hh), functools.partial(cmp_attend, p_heads, hh))
         for hh in range(HEADS_PER_GROUP)]
        + [(None, functools.partial(select_blocks, p_heads))]
        + attention_units([(kw_ref, vw_ref, prev, None, dist_prev < WINDOW, win_prev),
                           (kw_ref, vw_ref, i, None, causal, win_diag),
                           (ks_ref, vs_ref, prev, i > 0, None, sel_even),
                           (ks_ref, vs_ref, i, True, causal, sel_odd)], first=True))

    n_todo = jnp.int32(0)
    for c in range(used_smem.shape[0]):
        take = jnp.logical_and(c < i - 1, used_smem[c] > 0)

        @pl.when(take)
        def _(c=c, slot=n_todo):
            todo_smem[slot] = c

        n_todo = n_todo + take.astype(jnp.int32)
    todo_smem[n_todo] = 0

    def past_pair(j, carry):
        pipeline(attention_units([(ks_ref, vs_ref, todo_smem[2 * j], True, None, sel_even),
                                  (ks_ref, vs_ref, todo_smem[2 * j + 1], 2 * j + 1 < n_todo, None, sel_odd)],
                                 first=False))
        return carry

    lax.fori_loop(0, lax.shift_right_logical(n_todo + 1, 1), past_pair, 0)
    finish(1, sel_even, sel_odd)
    finish(2, win_prev, win_diag)

    o_ref[0] = out_scr[...].T


def _nsa(qT, kc3, vcT, k_aug, v_aug, bgT, *, tq):
    bsz, _, seq = qT.shape
    assert tq == KEY_BLOCK == WINDOW and seq % tq == 0
    n_cmp = kc3.shape[2]
    n_sel = seq // SEL_BLOCK
    n_kb = seq // KEY_BLOCK
    gw = HEADS_PER_GROUP * HEAD_DIM
    bg4 = bgT.reshape(bsz, N_KV_GROUPS, GATE_ROWS, seq)
    k_spec = lambda s: pl.BlockSpec((1, 1, 1, seq, KEY_AUG), lambda b, g, i: (b, s, g, 0, 0))
    v_spec = lambda s: pl.BlockSpec((1, 1, 1, n_kb, VAL_AUG, KEY_BLOCK), lambda b, g, i: (b, s, g, 0, 0, 0))
    return pl.pallas_call(
        functools.partial(_nsa_kernel, tq=tq, n_cmp=n_cmp, n_sel=n_sel),
        grid=(bsz, N_KV_GROUPS, seq // tq),
        in_specs=[
            pl.BlockSpec((1, gw, tq), lambda b, g, i: (b, g, i)),
            pl.BlockSpec((1, 1, n_cmp, 4 * HEAD_DIM), lambda b, g, i: (b, g, 0, 0)),
            pl.BlockSpec((1, 1, HEAD_DIM, n_cmp), lambda b, g, i: (b, g, 0, 0)),
            k_spec(0), v_spec(0),
            k_spec(1), v_spec(1),
            pl.BlockSpec((1, 1, GATE_ROWS, tq), lambda b, g, i: (b, g, 0, i)),
        ],
        out_specs=pl.BlockSpec((1, tq, gw), lambda b, g, i: (b, i, g)),
        out_shape=jax.ShapeDtypeStruct((bsz, seq, ATTN_WIDTH), F32),
        scratch_shapes=[
            pltpu.VMEM((n_kb, 16, tq), BF16),
            pltpu.VMEM((HEAD_DIM + 16, HEADS_PER_GROUP * tq), BF16),
            pltpu.VMEM((4, 1, HEADS_PER_GROUP * tq), F32),
            pltpu.VMEM((4, VAL_AUG, HEADS_PER_GROUP * tq), F32),
            pltpu.VMEM((gw, tq), F32),
            pltpu.SMEM((n_kb,), jnp.int32),
            pltpu.SMEM((n_kb + 1,), jnp.int32),
        ],
        compiler_params=_params(3),
        name="nsa",
    )(qT, kc3, vcT, k_aug, v_aug, k_aug, v_aug, bg4)


def _merge_kernel(x_ref, g_ref, o_ref_in, sc_ref, sh_ref, gt_ref, wm_ref, wl_ref, wg_ref, wn_ref, wo_ref,
                  lg_ref, lb_ref, y_ref):
    tm = x_ref.shape[1]
    row_blocks = [slice(r, r + ROW_BLOCK) for r in range(0, tm, ROW_BLOCK)]
    halves = [(0, D_MODEL // 2), (D_MODEL // 2, D_MODEL)]
    dot = functools.partial(jnp.dot, preferred_element_type=F32)

    def first_stage(rows):
        g = g_ref[0, rows, :].astype(BF16)
        o = o_ref_in[0, rows, :].astype(BF16)
        branch = [(dot(g, wl_ref[:, lo:hi]), dot(g, wg_ref[:, lo:hi]), dot(o, wn_ref[:, lo:hi])) for lo, hi in halves]
        x = x_ref[0, rows, :]
        u = (x * (1.0 + sc_ref[0]) + sh_ref[0]).astype(BF16)
        gates = [(dot(u, wm_ref[:, lo:hi]), dot(u, wm_ref[:, D_MODEL + lo:D_MODEL + hi])) for lo, hi in halves]
        return x, branch, gates

    def second_stage(rows, x, branch, gates):
        mix = None
        for (lo, hi), (lin, gate, z_nsa), (m_ssm, m_nsa) in zip(halves, branch, gates):
            merged = _sigmoid(m_ssm) * (lin * _sigmoid(gate)) + _sigmoid(m_nsa) * z_nsa
            part = dot(merged.astype(BF16), wo_ref[lo:hi, :])
            mix = part if mix is None else mix + part
        y = DEEPNORM_ALPHA * x + (1.0 + gt_ref[0]) * mix
        y_ref[0, rows, :] = _layer_norm(y, lg_ref[...], lb_ref[...])

    staged = None
    for rows in row_blocks + [None]:
        upcoming = first_stage(rows) if rows is not None else None
        if staged is not None:
            second_stage(*staged)
        staged = (rows,) + upcoming if rows is not None else None


def _merge(x, g_ssm, o_nsa, scale1, shift1, gate1, w_m, w_lin, w_gate, w_nsa, w_out, ln_g, ln_b, *, tm):
    bsz, seq, _ = x.shape
    row = pl.BlockSpec((1, 1, D_MODEL), lambda b, i: (b, 0, 0))
    full = lambda a: pl.BlockSpec(a.shape, lambda b, i: (0, 0), pipeline_mode=pl.Buffered(1))
    return pl.pallas_call(
        _merge_kernel,
        grid=(bsz, seq // tm),
        in_specs=[
            pl.BlockSpec((1, tm, D_MODEL), lambda b, i: (b, i, 0)),
            pl.BlockSpec((1, tm, SSM_WIDTH), lambda b, i: (b, i, 0)),
            pl.BlockSpec((1, tm, ATTN_WIDTH), lambda b, i: (b, i, 0)),
            row, row, row,
            full(w_m), full(w_lin), full(w_gate), full(w_nsa), full(w_out), full(ln_g), full(ln_b),
        ],
        out_specs=pl.BlockSpec((1, tm, D_MODEL), lambda b, i: (b, i, 0)),
        out_shape=jax.ShapeDtypeStruct(x.shape, F32),
        compiler_params=_params(2),
        name="merge",
    )(x, g_ssm, o_nsa, scale1, shift1, gate1, w_m, w_lin, w_gate, w_nsa, w_out, ln_g, ln_b)


def _ffn_kernel(x_ref, sc_ref, sh_ref, gt_ref, wg_ref, wu_ref, wd_ref, lg_ref, lb_ref, y_ref, *, n_chunks):
    tm = x_ref.shape[1]
    row_blocks = [slice(r, r + ROW_BLOCK) for r in range(0, tm, ROW_BLOCK)]
    tiles = FFN_HIDDEN // MXU_WIDTH
    edges = [MXU_WIDTH * ((tiles * c + n_chunks - 1) // n_chunks) for c in range(n_chunks + 1)]
    chunks = [slice(lo, hi) for lo, hi in zip(edges[:-1], edges[1:])]
    dot = functools.partial(jnp.dot, preferred_element_type=F32)

    def gate_up(rows):
        x = x_ref[0, rows, :]
        u = (x * (1.0 + sc_ref[0]) + sh_ref[0]).astype(BF16)
        return x, [(dot(u, wg_ref[:, cols]), dot(u, wu_ref[:, cols])) for cols in chunks]

    def down(rows, x, products):
        ffn = None
        for cols, (a, b) in zip(chunks, products):
            part = dot(((a * _sigmoid(a)) * b).astype(BF16), wd_ref[cols, :])
            ffn = part if ffn is None else ffn + part
        y = DEEPNORM_ALPHA * x + (1.0 + gt_ref[0]) * ffn
        y_ref[0, rows, :] = _layer_norm(y, lg_ref[...], lb_ref[...])

    staged = None
    for rows in row_blocks + [None]:
        upcoming = gate_up(rows) if rows is not None else None
        if staged is not None:
            down(*staged)
        staged = (rows,) + upcoming if rows is not None else None


def _ffn(x, scale2, shift2, gate2, w_gate, w_up, w_down, ln_g, ln_b, *, tm, n_chunks):
    bsz, seq, _ = x.shape
    row = pl.BlockSpec((1, 1, D_MODEL), lambda b, i: (b, 0, 0))
    resident = lambda a: pl.BlockSpec(a.shape, lambda b, i: (0, 0), pipeline_mode=pl.Buffered(1))
    return pl.pallas_call(
        functools.partial(_ffn_kernel, n_chunks=n_chunks),
        grid=(bsz, seq // tm),
        in_specs=[
            pl.BlockSpec((1, tm, D_MODEL), lambda b, i: (b, i, 0)),
            row, row, row,
            resident(w_gate), resident(w_up), resident(w_down), resident(ln_g), resident(ln_b),
        ],
        out_specs=pl.BlockSpec((1, tm, D_MODEL), lambda b, i: (b, i, 0)),
        out_shape=jax.ShapeDtypeStruct(x.shape, F32),
        compiler_params=_params(2),
        name="ffn",
    )(x, scale2, shift2, gate2, w_gate, w_up, w_down, ln_g, ln_b)


def _layer(x, c, w_ada, b_ada, w_in, ssm_a_re, ssm_a_im, ssm_log_dt, ssm_b_re, ssm_b_im, ssm_c_re, ssm_c_im,
           ssm_d, w_glu_lin, w_glu_gate, cmp_pe_k, cmp_pe_v, w_cmp_k1, w_cmp_k2, w_cmp_v1, w_cmp_v2,
           w_nsa_proj, w_out, ln1_g, ln1_b, w_ffn_gate, w_ffn_up, w_ffn_down, ln2_g, ln2_b):
    bsz, seq, _ = x.shape
    tq = min(KEY_BLOCK, seq)
    tm = min(512, seq)
    tm_dense = min(1024, seq)

    mod = _ada_mod(c, w_ada, b_ada).reshape(6, bsz, 1, D_MODEL)
    shift1, scale1, gate1, shift2, scale2, gate2 = (mod[k] for k in range(6))

    w_nat, w_tr, w_mg = _split_w_in(w_in)
    u_ssm, kv_cmp, k_aug, qT, v_aug, bgT = _in_proj(x, scale1, shift1, w_nat, w_tr, tm=tm)

    mats = _ssm_matrices(ssm_a_re, ssm_a_im, ssm_log_dt, ssm_b_re, ssm_b_im, ssm_c_re, ssm_c_im, ssm_d)
    g_ssm = _ssm(u_ssm, mats)

    kc3, vcT = _compress(kv_cmp,
                        cmp_pe_k.reshape(1, CMP_BLOCK * HEAD_DIM), cmp_pe_v.reshape(1, CMP_BLOCK * HEAD_DIM),
                        w_cmp_k1.reshape(CMP_BLOCK * HEAD_DIM, -1), w_cmp_k2,
                        w_cmp_v1.reshape(CMP_BLOCK * HEAD_DIM, -1), w_cmp_v2)
    o_nsa = _nsa(qT, kc3, vcT, k_aug, v_aug, bgT, tq=tq)

    x1 = _merge(x, g_ssm, o_nsa, scale1, shift1, gate1, w_mg,
                w_glu_lin.astype(BF16), w_glu_gate.astype(BF16), w_nsa_proj.astype(BF16), w_out.astype(BF16),
                ln1_g.reshape(1, D_MODEL), ln1_b.reshape(1, D_MODEL), tm=tm_dense)
    return _ffn(x1, scale2, shift2, gate2, w_ffn_gate.astype(BF16), w_ffn_up.astype(BF16),
                w_ffn_down.astype(BF16), ln2_g.reshape(1, D_MODEL), ln2_b.reshape(1, D_MODEL), tm=tm_dense, n_chunks=2)


def kernel(x, c, w_ada, b_ada, w_in, ssm_a_re, ssm_a_im, ssm_log_dt, ssm_b_re, ssm_b_im, ssm_c_re, ssm_c_im,
           ssm_d, w_glu_lin, w_glu_gate, cmp_pe_k, cmp_pe_v, w_cmp_k1, w_cmp_k2, w_cmp_v1, w_cmp_v2,
           w_nsa_proj, w_out, ln1_g, ln1_b, w_ffn_gate, w_ffn_up, w_ffn_down, ln2_g, ln2_b):
    for l in range(w_ada.shape[0]):
        x = _layer(x, c, w_ada[l], b_ada[l], w_in[l], ssm_a_re[l], ssm_a_im[l], ssm_log_dt[l],
                   ssm_b_re[l], ssm_b_im[l], ssm_c_re[l], ssm_c_im[l], ssm_d[l],
                   w_glu_lin[l], w_glu_gate[l], cmp_pe_k[l], cmp_pe_v[l],
                   w_cmp_k1[l], w_cmp_k2[l], w_cmp_v1[l], w_cmp_v2[l], w_nsa_proj[l], w_out[l],
                   ln1_g[l], ln1_b[l], w_ffn_gate[l], w_ffn_up[l], w_ffn_down[l], ln2_g[l], ln2_b[l])
    return x
```

```python
import functools
import math

import jax
import jax.numpy as jnp
from jax import lax
from jax.experimental import pallas as pl
from jax.experimental.pallas import tpu as pltpu

F32 = jnp.float32
BF16 = jnp.bfloat16
HIGHEST = lax.Precision.HIGHEST

D_MODEL = 1024
SSM_WIDTH = D_MODEL // 2
SSM_GROUP_SIZE = 16
SSM_GROUPS = SSM_WIDTH // SSM_GROUP_SIZE
SSM_STATE = 64
SSM_CHUNK = 16
SSM_GROUP_TILE = 8
SSM_STATE_ROW_PAD = 8
N_HEADS = 8
HEAD_DIM = 64
N_KV_GROUPS = 2
HEADS_PER_GROUP = N_HEADS // N_KV_GROUPS
ATTN_WIDTH = N_HEADS * HEAD_DIM
KV_WIDTH = N_KV_GROUPS * HEAD_DIM
CMP_BLOCK = 32
CMP_STRIDE = 16
CMP_HIDDEN = HEAD_DIM
SEL_BLOCK = 64
SEL_TOP_K = 8
WINDOW = 256
N_NSA_BRANCHES = 3
GATE_ROWS = 16
FFN_HIDDEN = (8 * D_MODEL + 3 * 256 - 1) // (3 * 256) * 256
DEEPNORM_ALPHA = 2.0 ** 0.25
LN_EPS = 1e-5
MASK_VALUE = -1e30

VMEM_LIMIT = 56 * 1024 * 1024
MXU_WIDTH = 256
BF16_ROWS = 16
ROW_BLOCK = 256

KEY_BLOCK = 256
SEL_SHIFT = SEL_BLOCK.bit_length() - 1
SEL_PER_KEY_BLOCK = KEY_BLOCK // SEL_BLOCK
KEY_AUG = 128
ALIBI_COL = HEAD_DIM
N_PIECES = 4
MASK_COL = HEAD_DIM + 16
QUERY_AUG = HEAD_DIM + 32
VAL_AUG = HEAD_DIM + 16
LOG2E = math.log2(math.e)
HEADS_PER_UNIT = 1
SCORES_AHEAD = 6

NAT_COLS = SSM_WIDTH + 4 * KV_WIDTH
TR_ROWS = ATTN_WIDTH + 2 * KV_WIDTH + N_KV_GROUPS * GATE_ROWS


def _bf16_pieces(value, n):
    pieces = []
    rest = value
    for _ in range(n):
        mant, expo = math.frexp(rest)
        piece = math.ldexp(round(mant * 256.0) / 256.0, expo)
        pieces.append(piece)
        rest -= piece
    return pieces


def _sigmoid(x):
    return 1.0 / (1.0 + jnp.exp2(x * (-LOG2E)))


def _gelu(x):
    c = math.sqrt(2.0 / math.pi)
    half = 0.5 * x
    return half + half * jnp.tanh(x * (c + (0.044715 * c) * (x * x)))


def _layer_norm(y, gain, bias):
    mu = jnp.mean(y, axis=-1, keepdims=True)
    d = y - mu
    var = jnp.mean(d * d, axis=-1, keepdims=True)
    return d * lax.rsqrt(var + LN_EPS) * gain + bias


def _params(n_axes):
    return pltpu.CompilerParams(dimension_semantics=("arbitrary",) * n_axes, vmem_limit_bytes=VMEM_LIMIT)


def _ada_kernel(c_ref, w_ref, b_ref, o_ref):
    c = c_ref[...]
    a = c * _sigmoid(c)
    w = w_ref[...].astype(BF16)
    a_hi = a.astype(BF16)
    a_lo = (a - a_hi.astype(F32)).astype(BF16)
    mod = (jnp.dot(a_hi, w, preferred_element_type=F32) + jnp.dot(a_lo, w, preferred_element_type=F32)) + b_ref[...]
    for k in range(o_ref.shape[0]):
        o_ref[k] = mod[:, k * D_MODEL:(k + 1) * D_MODEL]


def _ada_mod(c, w_ada, b_ada):
    bsz = c.shape[0]
    per_step = 2
    return pl.pallas_call(
        _ada_kernel,
        grid=(6 // per_step,),
        in_specs=[
            pl.BlockSpec((bsz, D_MODEL), lambda j: (0, 0)),
            pl.BlockSpec((D_MODEL, per_step * D_MODEL), lambda j: (0, j)),
            pl.BlockSpec((1, per_step * D_MODEL), lambda j: (0, j)),
        ],
        out_specs=pl.BlockSpec((per_step, bsz, D_MODEL), lambda j: (j, 0, 0)),
        out_shape=jax.ShapeDtypeStruct((6, bsz, D_MODEL), F32),
        compiler_params=_params(1),
        name="ada_mod",
    )(c, w_ada, b_ada.reshape(1, 6 * D_MODEL))


def _in_proj_kernel(x_ref, sc_ref, sh_ref, wn_ref, wt_ref, ussm_ref, kc_ref, kp_ref, qT_ref, vp_ref, bgT_ref):
    i = pl.program_id(1)
    tm = x_ref.shape[1]
    u = (x_ref[0] * (1.0 + sc_ref[0]) + sh_ref[0]).astype(BF16)
    nat = jnp.dot(u, wn_ref[...], preferred_element_type=F32)
    tr = lax.dot_general(wt_ref[...], u, (((1,), (1,)), ((), ())), preferred_element_type=F32)
    ussm_ref[0] = nat[:, :SSM_WIDTH]
    for s in range(2):
        for g in range(N_KV_GROUPS):
            lo = SSM_WIDTH + s * KV_WIDTH + g * HEAD_DIM
            kc_ref[0, s, g] = nat[:, lo:lo + HEAD_DIM]
    pos = i * tm + lax.broadcasted_iota(jnp.int32, (tm, KEY_AUG), 0)
    col = lax.broadcasted_iota(jnp.int32, (1, KEY_AUG), 1)
    blk = lax.shift_right_logical(pos, SEL_SHIFT)
    in_a = (col >= ALIBI_COL) & (col < ALIBI_COL + N_PIECES)
    in_b = (col >= ALIBI_COL + N_PIECES) & (col < ALIBI_COL + 2 * N_PIECES)
    hot = col == MASK_COL + (blk & (SEL_PER_KEY_BLOCK - 1))
    aux = jnp.where(in_a, blk * SEL_BLOCK, jnp.where(in_b, pos & (SEL_BLOCK - 1), jnp.where(hot, 1, 0))).astype(F32)
    head_lanes = col < HEAD_DIM
    for s in range(2):
        lo = SSM_WIDTH + (2 + s) * KV_WIDTH
        both = nat[:, lo:lo + KV_WIDTH]
        for g in range(N_KV_GROUPS):
            keys = both if g == 0 else pltpu.roll(both, HEAD_DIM, 1)
            kp_ref[0, s, g] = jnp.where(head_lanes, keys, aux).astype(BF16)
    qT_ref[0] = tr[:ATTN_WIDTH] * (HEAD_DIM ** -0.5)
    ones_rows = jnp.where(lax.broadcasted_iota(jnp.int32, (VAL_AUG - HEAD_DIM, tm), 0) == 0, 1.0, 0.0)
    for s in range(2):
        for g in range(N_KV_GROUPS):
            lo = ATTN_WIDTH + (s * N_KV_GROUPS + g) * HEAD_DIM
            v_aug = jnp.concatenate([tr[lo:lo + HEAD_DIM], ones_rows], axis=0).astype(BF16)
            for c in range(tm // KEY_BLOCK):
                vp_ref[0, s, g, c] = v_aug[:, c * KEY_BLOCK:(c + 1) * KEY_BLOCK]
    bgT_ref[0] = _sigmoid(tr[ATTN_WIDTH + 2 * KV_WIDTH:])


def _in_proj(x, scale1, shift1, w_nat, w_tr, *, tm):
    bsz, seq, _ = x.shape
    n_t = seq // tm
    n_kb = seq // KEY_BLOCK
    out_shapes = (
        jax.ShapeDtypeStruct((bsz, seq, SSM_WIDTH), F32),
        jax.ShapeDtypeStruct((bsz, 2, N_KV_GROUPS, seq, HEAD_DIM), F32),
        jax.ShapeDtypeStruct((bsz, 2, N_KV_GROUPS, seq, KEY_AUG), BF16),
        jax.ShapeDtypeStruct((bsz, ATTN_WIDTH, seq), F32),
        jax.ShapeDtypeStruct((bsz, 2, N_KV_GROUPS, n_kb, VAL_AUG, KEY_BLOCK), BF16),
        jax.ShapeDtypeStruct((bsz, N_KV_GROUPS * GATE_ROWS, seq), F32),
    )
    return pl.pallas_call(
        _in_proj_kernel,
        grid=(bsz, n_t),
        in_specs=[
            pl.BlockSpec((1, tm, D_MODEL), lambda b, i: (b, i, 0)),
            pl.BlockSpec((1, 1, D_MODEL), lambda b, i: (b, 0, 0)),
            pl.BlockSpec((1, 1, D_MODEL), lambda b, i: (b, 0, 0)),
            pl.BlockSpec((D_MODEL, NAT_COLS), lambda b, i: (0, 0)),
            pl.BlockSpec((TR_ROWS, D_MODEL), lambda b, i: (0, 0)),
        ],
        out_specs=(
            pl.BlockSpec((1, tm, SSM_WIDTH), lambda b, i: (b, i, 0)),
            pl.BlockSpec((1, 2, N_KV_GROUPS, tm, HEAD_DIM), lambda b, i: (b, 0, 0, i, 0)),
            pl.BlockSpec((1, 2, N_KV_GROUPS, tm, KEY_AUG), lambda b, i: (b, 0, 0, i, 0)),
            pl.BlockSpec((1, ATTN_WIDTH, tm), lambda b, i: (b, 0, i)),
            pl.BlockSpec((1, 2, N_KV_GROUPS, tm // KEY_BLOCK, VAL_AUG, KEY_BLOCK), lambda b, i: (b, 0, 0, i, 0, 0)),
            pl.BlockSpec((1, N_KV_GROUPS * GATE_ROWS, tm), lambda b, i: (b, 0, i)),
        ),
        out_shape=out_shapes,
        compiler_params=_params(2),
        name="in_proj",
    )(x, scale1, shift1, w_nat, w_tr)


def _split_w_in(w_in):
    o_q = SSM_WIDTH
    o_kv = o_q + ATTN_WIDTH
    o_bg = o_kv + 6 * KV_WIDTH
    o_mg = o_bg + N_NSA_BRANCHES * N_HEADS
    kv = [w_in[:, o_kv + s * KV_WIDTH:o_kv + (s + 1) * KV_WIDTH] for s in range(6)]
    w_nat = jnp.concatenate([w_in[:, :o_q], kv[0], kv[1], kv[2], kv[4]], axis=1)
    per_group = HEADS_PER_GROUP * N_NSA_BRANCHES
    bg = w_in[:, o_bg:o_mg].reshape(D_MODEL, N_KV_GROUPS, per_group)
    bg = jnp.pad(bg, ((0, 0), (0, 0), (0, GATE_ROWS - per_group))).reshape(D_MODEL, N_KV_GROUPS * GATE_ROWS)
    w_tr = jnp.concatenate([w_in[:, o_q:o_kv], kv[3], kv[5], bg], axis=1).T
    return w_nat.astype(BF16), w_tr.astype(BF16), w_in[:, o_mg:].astype(BF16)


def _ssm_prep_kernel(a_ref, logdt_ref, bT_ref, c_ref, d_ref, mT_ref, pT_ref, qT_ref, a16_ref):
    t_n = SSM_CHUNK
    c_n = SSM_GROUP_SIZE
    n_p = SSM_STATE
    rows = t_n * c_n
    dt = jnp.exp(logdt_ref[0])
    a_re = a_ref[0, 0]
    a_im = a_ref[0, 1]
    decay = jnp.exp(a_re * dt)
    ab_re = decay * jnp.cos(a_im * dt)
    ab_im = decay * jnp.sin(a_im * dt)
    n_re = ab_re - 1.0
    denom = a_re * a_re + a_im * a_im
    f_re = (n_re * a_re + ab_im * a_im) / denom
    f_im = (ab_im * a_re - n_re * a_im) / denom
    powers = [(jnp.ones_like(ab_re), jnp.zeros_like(ab_re))]
    for _ in range(t_n):
        p_re, p_im = powers[-1]
        powers.append((p_re * ab_re - p_im * ab_im, p_re * ab_im + p_im * ab_re))

    def per_token(first, step):
        picks = [powers[first + step * k] for k in range(t_n)]
        return (jnp.concatenate([jnp.broadcast_to(p[0], (c_n, 2 * n_p)) for p in picks], axis=0),
                jnp.concatenate([jnp.broadcast_to(p[1], (c_n, 2 * n_p)) for p in picks], axis=0))

    def cmul(x_re, x_im, y_re, y_im):
        return x_re * y_re - x_im * y_im, x_re * y_im + x_im * y_re

    bb_re, bb_im = cmul(f_re, f_im, bT_ref[0, 0], bT_ref[0, 1])
    bt_re = jnp.concatenate([bb_re] * t_n, axis=0)
    bt_im = jnp.concatenate([bb_im] * t_n, axis=0)
    ct_re = jnp.concatenate([c_ref[0, 0]] * t_n, axis=0)
    ct_im = jnp.concatenate([c_ref[0, 1]] * t_n, axis=0)

    p_re, p_im = cmul(*per_token(t_n - 1, -1), bt_re, bt_im)
    low_half = lax.broadcasted_iota(jnp.int32, (1, 2 * n_p), 1) < n_p
    pT_ref[0] = jnp.where(low_half, p_re, p_im).T.astype(BF16)

    q_re, q_im = cmul(ct_re, ct_im, *per_token(1, 1))
    qT_ref[0, 0] = q_re[:, :n_p].astype(BF16)
    qT_ref[0, 1] = (-q_im[:, :n_p]).astype(BF16)
    a16_ref[0] = jnp.concatenate([powers[t_n][0][:, :n_p], powers[t_n][1][:, :n_p]], axis=0)

    k_re, k_im = cmul(ct_re, ct_im, *per_token(0, 1))
    nt = (((1,), (1,)), ((), ()))
    strips = (lax.dot_general(_split3(k_re[:, :n_p]), _stack3(bt_re[:, :n_p], axis=1), nt, preferred_element_type=F32)
              - lax.dot_general(_split3(k_im[:, :n_p]), _stack3(bt_im[:, :n_p], axis=1), nt,
                                preferred_element_type=F32))
    lane = lax.broadcasted_iota(jnp.int32, (1, rows), 1)
    s_lane = lax.shift_right_logical(lane, c_n.bit_length() - 1)
    ci_lane = lane & (c_n - 1)
    row = lax.broadcasted_iota(jnp.int32, (rows, 1), 0)
    t_row = lax.shift_right_logical(row, c_n.bit_length() - 1)
    co_row = row & (c_n - 1)
    blocks = []
    for t in range(t_n):
        acc = jnp.zeros((c_n, rows), F32)
        for lag in range(t + 1):
            acc = jnp.where(s_lane == t - lag, strips[lag * c_n:(lag + 1) * c_n, :], acc)
        blocks.append(acc)
    d_rows = jnp.concatenate([d_ref[0]] * t_n, axis=0)
    skip = jnp.where((t_row == s_lane) & (co_row == ci_lane), d_rows, 0.0)
    mT_ref[0] = (jnp.concatenate(blocks, axis=0) + skip).astype(BF16)


def _ssm_matrices(a_re, a_im, log_dt, b_re, b_im, c_re, c_im, d_skip):
    n_g, n_p = a_re.shape
    c_n = SSM_GROUP_SIZE
    rows = SSM_CHUNK * c_n
    twice = lambda x: jnp.concatenate([x, x], axis=-1).astype(F32)
    a = twice(jnp.stack([a_re, a_im], axis=1)).reshape(n_g, 2, 1, 2 * n_p)
    bT = twice(jnp.swapaxes(jnp.stack([b_re, b_im], axis=1), 2, 3))
    c = twice(jnp.stack([c_re, c_im], axis=1))
    spec = lambda shape: pl.BlockSpec((1,) + shape, lambda g: (g,) + (0,) * len(shape))
    mT, pT, qT, a16 = pl.pallas_call(
        _ssm_prep_kernel,
        grid=(n_g,),
        in_specs=[spec((2, 1, 2 * n_p)), spec((1, 1)), spec((2, c_n, 2 * n_p)), spec((2, c_n, 2 * n_p)),
                  spec((c_n, 1))],
        out_specs=(spec((rows, rows)), spec((2 * n_p, rows)), spec((2, rows, n_p)), spec((2, n_p))),
        out_shape=(
            jax.ShapeDtypeStruct((n_g, rows, rows), BF16),
            jax.ShapeDtypeStruct((n_g, 2 * n_p, rows), BF16),
            jax.ShapeDtypeStruct((n_g, 2, rows, n_p), BF16),
            jax.ShapeDtypeStruct((n_g, 2, n_p), F32),
        ),
        compiler_params=_params(1),
        name="ssm_prep",
    )(a, log_dt.astype(F32).reshape(n_g, 1, 1), bT, c, d_skip.astype(F32).reshape(n_g, c_n, 1))
    return mT, pT.reshape(n_g, 2, n_p, rows), qT, a16


def _ssm_kernel(u_ref, mT_ref, pT_ref, qT_ref, a16_ref, o_ref, ut_scr, yt_scr, sre, sim, *, n_chunks):
    n_g = SSM_GROUP_TILE
    t_n = SSM_CHUNK
    c_n = SSM_GROUP_SIZE

    for s in range(t_n):
        x_s = u_ref[0, pl.ds(s, n_chunks, stride=t_n), :]
        ut_scr[:, s * c_n:(s + 1) * c_n, :] = x_s.T.reshape(n_g, c_n, n_chunks).astype(BF16)

    pitch = n_chunks + SSM_STATE_ROW_PAD

    def group_rows(g):
        return pl.ds(g * pitch, n_chunks)

    def chunk_states(g, carry):
        ut = ut_scr[g]
        sre[group_rows(g), :] = jnp.dot(pT_ref[g, 0], ut, preferred_element_type=F32).T
        sim[group_rows(g), :] = jnp.dot(pT_ref[g, 1], ut, preferred_element_type=F32).T
        return carry

    for g in range(n_g):
        chunk_states(g, 0)

    a_r = a16_ref[pl.ds(0, n_g, stride=2), :]
    a_i = a16_ref[pl.ds(1, n_g, stride=2), :]

    def carry_states(c, h):
        h_r, h_i = h
        rows = pl.ds(c, n_g, stride=pitch)
        s_r = sre[rows, :]
        s_i = sim[rows, :]
        sre[rows, :] = h_r
        sim[rows, :] = h_i
        return a_r * h_r - a_i * h_i + s_r, a_r * h_i + a_i * h_r + s_i

    zero = jnp.zeros((n_g, SSM_STATE), F32)
    lax.fori_loop(0, n_chunks, carry_states, (zero, zero), unroll=4)

    nt = (((1,), (1,)), ((), ()))

    def outputs(g, carry):
        yt = jnp.dot(mT_ref[g], ut_scr[g], preferred_element_type=F32)
        yt = yt + lax.dot_general(qT_ref[g, 0], sre[group_rows(g), :].astype(BF16), nt, preferred_element_type=F32)
        yt = yt + lax.dot_general(qT_ref[g, 1], sim[group_rows(g), :].astype(BF16), nt, preferred_element_type=F32)
        yt_scr[g] = _gelu(yt)
        return carry

    for g in range(n_g):
        outputs(g, 0)

    for t in range(t_n):
        z = yt_scr[:, t * c_n:(t + 1) * c_n, :].reshape(n_g * c_n, n_chunks)
        o_ref[0, pl.ds(t, n_chunks, stride=t_n), :] = z.T


def _ssm(u_ssm, mats):
    bsz, seq, width = u_ssm.shape
    n_chunks = seq // SSM_CHUNK
    mT, pT, qT, a16 = mats
    gt = SSM_GROUP_TILE
    n_tiles = SSM_GROUPS // gt
    rows = SSM_CHUNK * SSM_GROUP_SIZE
    a16 = a16.reshape(2 * SSM_GROUPS, SSM_STATE)
    per_tile = lambda a: pl.BlockSpec((gt,) + a.shape[1:], lambda t, b: (t,) + (0,) * (a.ndim - 1))
    return pl.pallas_call(
        functools.partial(_ssm_kernel, n_chunks=n_chunks),
        grid=(n_tiles, bsz),
        in_specs=[
            pl.BlockSpec((1, seq, gt * SSM_GROUP_SIZE), lambda t, b: (b, 0, t)),
            per_tile(mT), per_tile(pT), per_tile(qT),
            pl.BlockSpec((2 * gt, SSM_STATE), lambda t, b: (t, 0)),
        ],
        out_specs=pl.BlockSpec((1, seq, gt * SSM_GROUP_SIZE), lambda t, b: (b, 0, t)),
        out_shape=jax.ShapeDtypeStruct(u_ssm.shape, F32),
        scratch_shapes=[
            pltpu.VMEM((gt, rows, n_chunks), BF16),
            pltpu.VMEM((gt, rows, n_chunks), F32),
            pltpu.VMEM((gt * (n_chunks + SSM_STATE_ROW_PAD), SSM_STATE), F32),
            pltpu.VMEM((gt * (n_chunks + SSM_STATE_ROW_PAD), SSM_STATE), F32),
        ],
        compiler_params=_params(2),
        name="ssm",
    )(u_ssm, mT, pT, qT, a16)


def _split3(x):
    hi = x.astype(BF16)
    lo = (x - hi.astype(F32)).astype(BF16)
    return jnp.concatenate([hi, lo, hi], axis=-1)


def _stack3(w, axis=0):
    hi = w.astype(BF16)
    lo = (w - hi.astype(F32)).astype(BF16)
    return jnp.concatenate([hi, hi, lo], axis=axis)


def _compress_kernel(kx_ref, vx_ref, pek_ref, pev_ref, wk1_ref, wk2_ref, wv1_ref, wv2_ref, kc_ref, vcT_ref):
    def mlp(x_ref, pe_ref, w1_ref, w2_ref):
        n = x_ref.shape[3] // CMP_STRIDE
        x = jnp.concatenate([x_ref[0, 0, 0, pl.ds(l, n, stride=CMP_STRIDE), :] for l in range(CMP_STRIDE)], axis=1)
        both = jnp.dot(_split3(x), w1_ref[...], preferred_element_type=F32)
        pe_both = jnp.dot(_split3(pe_ref[...]), w1_ref[...], preferred_element_type=F32)
        bias = pe_both[0:1, :CMP_HIDDEN] + pe_both[1:2, CMP_HIDDEN:]
        h = _gelu(both[:, :CMP_HIDDEN] + pltpu.roll(both[:, CMP_HIDDEN:], n - 1, 0) + bias)
        return jnp.dot(_split3(h), w2_ref[...], preferred_element_type=F32)

    kc = mlp(kx_ref, pek_ref, wk1_ref, wk2_ref)
    kc_hi = kc.astype(BF16)
    kc_lo = (kc - kc_hi.astype(F32)).astype(BF16)
    kc_ref[0, 0] = jnp.concatenate([kc_hi, kc_hi, kc_lo, jnp.zeros_like(kc_hi)], axis=1)
    vcT_ref[0, 0] = mlp(vx_ref, pev_ref, wv1_ref, wv2_ref).T.astype(BF16)


def _compress(kv, pe_k, pe_v, wk1, wk2, wv1, wv2):
    bsz, _, n_g, seq, width = kv.shape
    n_chunks = seq // CMP_STRIDE
    half = CMP_STRIDE * HEAD_DIM
    side_by_side = lambda w1: _stack3(jnp.concatenate([w1[:half], w1[half:]], axis=1))
    two_rows = lambda pe: jnp.pad(pe.reshape(2, half), ((0, BF16_ROWS - 2), (0, 0)))
    pe_k, pe_v = two_rows(pe_k), two_rows(pe_v)
    wk1, wv1, wk2, wv2 = side_by_side(wk1), side_by_side(wv1), _stack3(wk2), _stack3(wv2)
    blk = lambda s: pl.BlockSpec((1, 1, 1, seq, width), lambda b, g: (b, s, g, 0, 0))
    full = lambda a: pl.BlockSpec(a.shape, lambda b, g: (0, 0))
    return pl.pallas_call(
        _compress_kernel,
        grid=(bsz, n_g),
        in_specs=[blk(0), blk(1), full(pe_k), full(pe_v), full(wk1), full(wk2), full(wv1), full(wv2)],
        out_specs=(
            pl.BlockSpec((1, 1, n_chunks, 4 * HEAD_DIM), lambda b, g: (b, g, 0, 0)),
            pl.BlockSpec((1, 1, HEAD_DIM, n_chunks), lambda b, g: (b, g, 0, 0)),
        ),
        out_shape=(
            jax.ShapeDtypeStruct((bsz, n_g, n_chunks, 4 * HEAD_DIM), BF16),
            jax.ShapeDtypeStruct((bsz, n_g, HEAD_DIM, n_chunks), BF16),
        ),
        compiler_params=_params(2),
        name="compress",
    )(kv, kv, pe_k, pe_v, wk1, wk2, wv1, wv2)


def _nsa_kernel(qT_ref, kc_ref, vcT_ref, ks_ref, vs_ref, kw_ref, vw_ref, bg_ref, o_ref,
                negm_scr, qa_scr, m_scr, acc_scr, out_scr, used_smem, todo_smem, *, tq, n_cmp, n_sel):
    g = pl.program_id(1)
    i = pl.program_id(2)
    t_row = i * tq + lax.broadcasted_iota(jnp.int32, (1, tq), 1)
    slopes = [jnp.where(g == 0, 2.0 ** -(hh + 1), 2.0 ** -(HEADS_PER_GROUP + hh + 1)).astype(F32)
              for hh in range(HEADS_PER_GROUP)]

    def gate(hh, branch):
        r = hh * N_NSA_BRANCHES + branch
        return bg_ref[0, 0, r:r + 1, :]

    def head_rows(hh):
        return slice(hh * HEAD_DIM, (hh + 1) * HEAD_DIM)

    def head_lanes(hh):
        return slice(hh * tq, (hh + 1) * tq)

    kc3 = kc_ref[0, 0]
    vcT = vcT_ref[0, 0]
    cmp_end = lax.broadcasted_iota(jnp.int32, (n_cmp, 1), 0) * CMP_STRIDE + (CMP_BLOCK - 1)
    dist_c = (t_row - cmp_end).astype(F32)
    valid_c = dist_c >= 0.0
    piece = lax.broadcasted_iota(jnp.int32, (16, 1), 0)
    log2e_pieces = _bf16_pieces(LOG2E, N_PIECES)
    log2e_col = jnp.zeros((16, 1), F32)
    for k, value in enumerate(log2e_pieces):
        log2e_col = jnp.where((piece == k) | (piece == N_PIECES + k), value, log2e_col)
    for hh in range(HEADS_PER_GROUP):
        alibi = jnp.broadcast_to(log2e_col * slopes[hh], (16, tq)).astype(BF16)
        q_l2 = (qT_ref[0, head_rows(hh), :] * LOG2E).astype(BF16)
        qa_scr[:, head_lanes(hh)] = jnp.concatenate([q_l2, alibi], axis=0)

    def cmp_scores(hh):
        q_h = qT_ref[0, head_rows(hh), :]
        q_hi = q_h.astype(BF16)
        q_lo = (q_h - q_hi.astype(F32)).astype(BF16)
        q3 = jnp.concatenate([q_hi, q_lo, q_hi, jnp.zeros_like(q_hi)], axis=0)
        return jnp.dot(kc3, q3, preferred_element_type=F32)

    def cmp_attend(p_heads, hh, s):
        s = jnp.where(valid_c, s - slopes[hh] * dist_c, MASK_VALUE)
        m = jnp.max(s, axis=0, keepdims=True)
        p = jnp.where(valid_c, jnp.exp(s - m), 0.0)
        l = jnp.sum(p, axis=0, keepdims=True)
        p = p * (1.0 / jnp.where(l > 0.0, l, 1.0))
        o_c = jnp.dot(vcT, p.astype(BF16), preferred_element_type=F32)
        out_scr[head_rows(hh), :] = gate(hh, 0) * o_c
        p_heads.append(p)

    def select_blocks(p_heads, _):
        p_sum = sum(p_heads[1:], p_heads[0])
        blk = lax.broadcasted_iota(jnp.int32, (n_sel, n_cmp), 0)
        cmp_ix = lax.broadcasted_iota(jnp.int32, (n_sel, n_cmp), 1)
        ratio = SEL_BLOCK // CMP_STRIDE
        extra = CMP_BLOCK // CMP_STRIDE - 1
        overlap_t = jnp.where((cmp_ix >= ratio * blk - extra) & (cmp_ix <= ratio * blk + ratio - 1),
                              1.0, 0.0).astype(BF16)
        imp = jnp.zeros((n_sel, tq), F32)
        rest = p_sum
        for _ in range(3):
            part = rest.astype(BF16)
            imp = imp + jnp.dot(overlap_t, part, preferred_element_type=F32)
            rest = rest - part.astype(F32)

        j_col = lax.broadcasted_iota(jnp.int32, (n_sel, 1), 0)
        forced = (j_col == 0) | (j_col == lax.shift_right_logical(t_row, SEL_SHIFT))
        future = j_col * SEL_BLOCK > t_row
        score = jnp.where(forced, jnp.inf, jnp.where(future, -jnp.inf, imp))
        rank = jnp.zeros((n_sel, tq), F32)
        for jp in range(n_sel):
            other = score[jp:jp + 1, :]
            wins_tie = jnp.where(j_col > jp, 1.0, 0.0)
            rank = rank + jnp.where(other > score, 1.0, jnp.where(other == score, wins_tie, 0.0))
        neg = jnp.where(rank < float(min(SEL_TOP_K, n_sel)), 0.0, MASK_VALUE)
        pad_rows = jnp.zeros((BF16_ROWS - SEL_PER_KEY_BLOCK, tq), F32)
        for c in range(n_sel // SEL_PER_KEY_BLOCK):
            rows = neg[c * SEL_PER_KEY_BLOCK:(c + 1) * SEL_PER_KEY_BLOCK, :]
            negm_scr[c] = jnp.concatenate([rows, pad_rows], axis=0).astype(BF16)
            used_smem[c] = (jnp.max(rows) > 0.5 * MASK_VALUE).astype(jnp.int32)

    unit_lanes = HEADS_PER_UNIT * tq
    rel = ((lax.broadcasted_iota(jnp.int32, (KEY_BLOCK, unit_lanes), 1) & (tq - 1))
           - lax.broadcasted_iota(jnp.int32, (KEY_BLOCK, unit_lanes), 0))
    zero_tail = jnp.zeros((KEY_AUG - QUERY_AUG, unit_lanes), BF16)
    no_mask = jnp.zeros((BF16_ROWS, unit_lanes), BF16)
    all_masked = jnp.full((BF16_ROWS, unit_lanes), MASK_VALUE, BF16)

    def unit_cols(hu):
        return slice(hu * unit_lanes, (hu + 1) * unit_lanes)

    def block_mask(kb):
        return jnp.concatenate([negm_scr[kb]] * HEADS_PER_UNIT, axis=1)

    sel_even, sel_odd, win_prev, win_diag = range(4)

    def scores(job, hh):
        k_ref, _, kb, enabled, _, _ = job
        k0 = pl.multiple_of(kb * KEY_BLOCK, KEY_BLOCK)
        k_aug = k_ref[0, 0, 0, pl.ds(k0, KEY_BLOCK), :]
        if enabled is None:
            mask_tile = no_mask
        elif enabled is True:
            mask_tile = block_mask(kb)
        else:
            mask_tile = jnp.where(enabled, block_mask(kb), all_masked)
        q_aug = jnp.concatenate([qa_scr[:, unit_cols(hh)], mask_tile, zero_tail], axis=0)
        return jnp.dot(k_aug, q_aug, preferred_element_type=F32)

    def accumulate(job, first, hh, s):
        _, v_ref, kb, _, valid, slot = job
        cols = unit_cols(hh)
        if valid is not None:
            s = jnp.where(valid, s, MASK_VALUE)
        m_new = jnp.max(s, axis=0, keepdims=True)
        if not first:
            m_old = m_scr[slot, :, cols]
            m_new = jnp.maximum(m_old, m_new)
        p = jnp.exp2(s - jnp.maximum(m_new, 0.1 * MASK_VALUE))
        pv = jnp.dot(v_ref[0, 0, 0, kb], p.astype(BF16), preferred_element_type=F32)
        if not first:
            pv = jnp.exp2(m_old - m_new) * acc_scr[slot, :, cols] + pv
        acc_scr[slot, :, cols] = pv
        m_scr[slot, :, cols] = m_new

    def pipeline(units):
        issue = lambda unit: None if unit[0] is None else unit[0]()
        pending = [issue(unit) for unit in units[:SCORES_AHEAD]]
        for n, unit in enumerate(units):
            if n + SCORES_AHEAD < len(units):
                pending.append(issue(units[n + SCORES_AHEAD]))
            unit[1](pending.pop(0))

    def attention_units(jobs, first):
        return [(functools.partial(scores, job, hu), functools.partial(accumulate, job, first, hu))
                for job in jobs for hu in range(HEADS_PER_GROUP // HEADS_PER_UNIT)]

    def finish(branch, slot_a, slot_b):
        m_a = m_scr[slot_a]
        m_b = m_scr[slot_b]
        m = jnp.maximum(m_a, m_b)
        w_a = jnp.exp2(m_a - m)
        w_b = jnp.exp2(m_b - m)
        for hh in range(HEADS_PER_GROUP):
            cols = head_lanes(hh)
            acc = acc_scr[slot_a, :, cols] * w_a[:, cols] + acc_scr[slot_b, :, cols] * w_b[:, cols]
            l = acc[HEAD_DIM:HEAD_DIM + 1, :]
            o_b = acc[:HEAD_DIM, :] * (1.0 / jnp.where(l > 0.0, l, 1.0))
            out_scr[head_rows(hh), :] = out_scr[head_rows(hh), :] + gate(hh, branch) * o_b

    prev = jnp.maximum(i - 1, 0)
    causal = rel >= 0
    off_without_prev = jnp.where(i > 0, 0, WINDOW + KEY_BLOCK)
    dist_prev = rel + KEY_BLOCK + off_without_prev
    p_heads = []
    pipeline(
        [(functools.partial(cmp_scores, hh), functools.partial(cmp_attend, p_heads, hh))
         for hh in range(HEADS_PER_GROUP)]
        + [(None, functools.partial(select_blocks, p_heads))]
        + attention_units([(kw_ref, vw_ref, prev, None, dist_prev < WINDOW, win_prev),
                           (kw_ref, vw_ref, i, None, causal, win_diag),
                           (ks_ref, vs_ref, prev, i > 0, None, sel_even),
                           (ks_ref, vs_ref, i, True, causal, sel_odd)], first=True))

    n_todo = jnp.int32(0)
    for c in range(used_smem.shape[0]):
        take = jnp.logical_and(c < i - 1, used_smem[c] > 0)

        @pl.when(take)
        def _(c=c, slot=n_todo):
            todo_smem[slot] = c

        n_todo = n_todo + take.astype(jnp.int32)
    todo_smem[n_todo] = 0

    def past_pair(j, carry):
        pipeline(attention_units([(ks_ref, vs_ref, todo_smem[2 * j], True, None, sel_even),
                                  (ks_ref, vs_ref, todo_smem[2 * j + 1], 2 * j + 1 < n_todo, None, sel_odd)],
                                 first=False))
        return carry

    lax.fori_loop(0, lax.shift_right_logical(n_todo + 1, 1), past_pair, 0)
    finish(1, sel_even, sel_odd)
    finish(2, win_prev, win_diag)

    o_ref[0] = out_scr[...].T


def _nsa(qT, kc3, vcT, k_aug, v_aug, bgT, *, tq):
    bsz, _, seq = qT.shape
    assert tq == KEY_BLOCK == WINDOW and seq % tq == 0
    n_cmp = kc3.shape[2]
    n_sel = seq // SEL_BLOCK
    n_kb = seq // KEY_BLOCK
    gw = HEADS_PER_GROUP * HEAD_DIM
    bg4 = bgT.reshape(bsz, N_KV_GROUPS, GATE_ROWS, seq)
    k_spec = lambda s: pl.BlockSpec((1, 1, 1, seq, KEY_AUG), lambda b, g, i: (b, s, g, 0, 0))
    v_spec = lambda s: pl.BlockSpec((1, 1, 1, n_kb, VAL_AUG, KEY_BLOCK), lambda b, g, i: (b, s, g, 0, 0, 0))
    return pl.pallas_call(
        functools.partial(_nsa_kernel, tq=tq, n_cmp=n_cmp, n_sel=n_sel),
        grid=(bsz, N_KV_GROUPS, seq // tq),
        in_specs=[
            pl.BlockSpec((1, gw, tq), lambda b, g, i: (b, g, i)),
            pl.BlockSpec((1, 1, n_cmp, 4 * HEAD_DIM), lambda b, g, i: (b, g, 0, 0)),
            pl.BlockSpec((1, 1, HEAD_DIM, n_cmp), lambda b, g, i: (b, g, 0, 0)),
            k_spec(0), v_spec(0),
            k_spec(1), v_spec(1),
            pl.BlockSpec((1, 1, GATE_ROWS, tq), lambda b, g, i: (b, g, 0, i)),
        ],
        out_specs=pl.BlockSpec((1, tq, gw), lambda b, g, i: (b, i, g)),
        out_shape=jax.ShapeDtypeStruct((bsz, seq, ATTN_WIDTH), F32),
        scratch_shapes=[
            pltpu.VMEM((n_kb, 16, tq), BF16),
            pltpu.VMEM((HEAD_DIM + 16, HEADS_PER_GROUP * tq), BF16),
            pltpu.VMEM((4, 1, HEADS_PER_GROUP * tq), F32),
            pltpu.VMEM((4, VAL_AUG, HEADS_PER_GROUP * tq), F32),
            pltpu.VMEM((gw, tq), F32),
            pltpu.SMEM((n_kb,), jnp.int32),
            pltpu.SMEM((n_kb + 1,), jnp.int32),
        ],
        compiler_params=_params(3),
        name="nsa",
    )(qT, kc3, vcT, k_aug, v_aug, k_aug, v_aug, bg4)


def _merge_kernel(x_ref, g_ref, o_ref_in, sc_ref, sh_ref, gt_ref, wm_ref, wl_ref, wg_ref, wn_ref, wo_ref,
                  lg_ref, lb_ref, y_ref):
    tm = x_ref.shape[1]
    row_blocks = [slice(r, r + ROW_BLOCK) for r in range(0, tm, ROW_BLOCK)]
    halves = [(0, D_MODEL // 2), (D_MODEL // 2, D_MODEL)]
    dot = functools.partial(jnp.dot, preferred_element_type=F32)

    def first_stage(rows):
        g = g_ref[0, rows, :].astype(BF16)
        o = o_ref_in[0, rows, :].astype(BF16)
        branch = [(dot(g, wl_ref[:, lo:hi]), dot(g, wg_ref[:, lo:hi]), dot(o, wn_ref[:, lo:hi])) for lo, hi in halves]
        x = x_ref[0, rows, :]
        u = (x * (1.0 + sc_ref[0]) + sh_ref[0]).astype(BF16)
        gates = [(dot(u, wm_ref[:, lo:hi]), dot(u, wm_ref[:, D_MODEL + lo:D_MODEL + hi])) for lo, hi in halves]
        return x, branch, gates

    def second_stage(rows, x, branch, gates):
        mix = None
        for (lo, hi), (lin, gate, z_nsa), (m_ssm, m_nsa) in zip(halves, branch, gates):
            merged = _sigmoid(m_ssm) * (lin * _sigmoid(gate)) + _sigmoid(m_nsa) * z_nsa
            part = dot(merged.astype(BF16), wo_ref[lo:hi, :])
            mix = part if mix is None else mix + part
        y = DEEPNORM_ALPHA * x + (1.0 + gt_ref[0]) * mix
        y_ref[0, rows, :] = _layer_norm(y, lg_ref[...], lb_ref[...])

    staged = None
    for rows in row_blocks + [None]:
        upcoming = first_stage(rows) if rows is not None else None
        if staged is not None:
            second_stage(*staged)
        staged = (rows,) + upcoming if rows is not None else None


def _merge(x, g_ssm, o_nsa, scale1, shift1, gate1, w_m, w_lin, w_gate, w_nsa, w_out, ln_g, ln_b, *, tm):
    bsz, seq, _ = x.shape
    row = pl.BlockSpec((1, 1, D_MODEL), lambda b, i: (b, 0, 0))
    full = lambda a: pl.BlockSpec(a.shape, lambda b, i: (0, 0), pipeline_mode=pl.Buffered(1))
    return pl.pallas_call(
        _merge_kernel,
        grid=(bsz, seq // tm),
        in_specs=[
            pl.BlockSpec((1, tm, D_MODEL), lambda b, i: (b, i, 0)),
            pl.BlockSpec((1, tm, SSM_WIDTH), lambda b, i: (b, i, 0)),
            pl.BlockSpec((1, tm, ATTN_WIDTH), lambda b, i: (b, i, 0)),
            row, row, row,
            full(w_m), full(w_lin), full(w_gate), full(w_nsa), full(w_out), full(ln_g), full(ln_b),
        ],
        out_specs=pl.BlockSpec((1, tm, D_MODEL), lambda b, i: (b, i, 0)),
        out_shape=jax.ShapeDtypeStruct(x.shape, F32),
        compiler_params=_params(2),
        name="merge",
    )(x, g_ssm, o_nsa, scale1, shift1, gate1, w_m, w_lin, w_gate, w_nsa, w_out, ln_g, ln_b)


def _ffn_kernel(x_ref, sc_ref, sh_ref, gt_ref, wg_ref, wu_ref, wd_ref, lg_ref, lb_ref, y_ref, *, n_chunks):
    tm = x_ref.shape[1]
    row_blocks = [slice(r, r + ROW_BLOCK) for r in range(0, tm, ROW_BLOCK)]
    tiles = FFN_HIDDEN // MXU_WIDTH
    edges = [MXU_WIDTH * ((tiles * c + n_chunks - 1) // n_chunks) for c in range(n_chunks + 1)]
    chunks = [slice(lo, hi) for lo, hi in zip(edges[:-1], edges[1:])]
    dot = functools.partial(jnp.dot, preferred_element_type=F32)

    def gate_up(rows):
        x = x_ref[0, rows, :]
        u = (x * (1.0 + sc_ref[0]) + sh_ref[0]).astype(BF16)
        return x, [(dot(u, wg_ref[:, cols]), dot(u, wu_ref[:, cols])) for cols in chunks]

    def down(rows, x, products):
        ffn = None
        for cols, (a, b) in zip(chunks, products):
            part = dot(((a * _sigmoid(a)) * b).astype(BF16), wd_ref[cols, :])
            ffn = part if ffn is None else ffn + part
        y = DEEPNORM_ALPHA * x + (1.0 + gt_ref[0]) * ffn
        y_ref[0, rows, :] = _layer_norm(y, lg_ref[...], lb_ref[...])

    staged = None
    for rows in row_blocks + [None]:
        upcoming = gate_up(rows) if rows is not None else None
        if staged is not None:
            down(*staged)
        staged = (rows,) + upcoming if rows is not None else None


def _ffn(x, scale2, shift2, gate2, w_gate, w_up, w_down, ln_g, ln_b, *, tm, n_chunks):
    bsz, seq, _ = x.shape
    row = pl.BlockSpec((1, 1, D_MODEL), lambda b, i: (b, 0, 0))
    resident = lambda a: pl.BlockSpec(a.shape, lambda b, i: (0, 0), pipeline_mode=pl.Buffered(1))
    return pl.pallas_call(
        functools.partial(_ffn_kernel, n_chunks=n_chunks),
        grid=(bsz, seq // tm),
        in_specs=[
            pl.BlockSpec((1, tm, D_MODEL), lambda b, i: (b, i, 0)),
            row, row, row,
            resident(w_gate), resident(w_up), resident(w_down), resident(ln_g), resident(ln_b),
        ],
        out_specs=pl.BlockSpec((1, tm, D_MODEL), lambda b, i: (b, i, 0)),
        out_shape=jax.ShapeDtypeStruct(x.shape, F32),
        compiler_params=_params(2),
        name="ffn",
    )(x, scale2, shift2, gate2, w_gate, w_up, w_down, ln_g, ln_b)


def _layer(x, c, w_ada, b_ada, w_in, ssm_a_re, ssm_a_im, ssm_log_dt, ssm_b_re, ssm_b_im, ssm_c_re, ssm_c_im,
           ssm_d, w_glu_lin, w_glu_gate, cmp_pe_k, cmp_pe_v, w_cmp_k1, w_cmp_k2, w_cmp_v1, w_cmp_v2,
           w_nsa_proj, w_out, ln1_g, ln1_b, w_ffn_gate, w_ffn_up, w_ffn_down, ln2_g, ln2_b):
    bsz, seq, _ = x.shape
    tq = min(KEY_BLOCK, seq)
    tm = min(512, seq)
    tm_dense = min(1024, seq)

    mod = _ada_mod(c, w_ada, b_ada).reshape(6, bsz, 1, D_MODEL)
    shift1, scale1, gate1, shift2, scale2, gate2 = (mod[k] for k in range(6))

    w_nat, w_tr, w_mg = _split_w_in(w_in)
    u_ssm, kv_cmp, k_aug, qT, v_aug, bgT = _in_proj(x, scale1, shift1, w_nat, w_tr, tm=tm)

    mats = _ssm_matrices(ssm_a_re, ssm_a_im, ssm_log_dt, ssm_b_re, ssm_b_im, ssm_c_re, ssm_c_im, ssm_d)
    g_ssm = _ssm(u_ssm, mats)

    kc3, vcT = _compress(kv_cmp,
                        cmp_pe_k.reshape(1, CMP_BLOCK * HEAD_DIM), cmp_pe_v.reshape(1, CMP_BLOCK * HEAD_DIM),
                        w_cmp_k1.reshape(CMP_BLOCK * HEAD_DIM, -1), w_cmp_k2,
                        w_cmp_v1.reshape(CMP_BLOCK * HEAD_DIM, -1), w_cmp_v2)
    o_nsa = _nsa(qT, kc3, vcT, k_aug, v_aug, bgT, tq=tq)

    x1 = _merge(x, g_ssm, o_nsa, scale1, shift1, gate1, w_mg,
                w_glu_lin.astype(BF16), w_glu_gate.astype(BF16), w_nsa_proj.astype(BF16), w_out.astype(BF16),
                ln1_g.reshape(1, D_MODEL), ln1_b.reshape(1, D_MODEL), tm=tm_dense)
    return _ffn(x1, scale2, shift2, gate2, w_ffn_gate.astype(BF16), w_ffn_up.astype(BF16),
                w_ffn_down.astype(BF16), ln2_g.reshape(1, D_MODEL), ln2_b.reshape(1, D_MODEL), tm=tm_dense, n_chunks=2)


def kernel(x, c, w_ada, b_ada, w_in, ssm_a_re, ssm_a_im, ssm_log_dt, ssm_b_re, ssm_b_im, ssm_c_re, ssm_c_im,
           ssm_d, w_glu_lin, w_glu_gate, cmp_pe_k, cmp_pe_v, w_cmp_k1, w_cmp_k2, w_cmp_v1, w_cmp_v2,
           w_nsa_proj, w_out, ln1_g, ln1_b, w_ffn_gate, w_ffn_up, w_ffn_down, ln2_g, ln2_b):
    for l in range(w_ada.shape[0]):
        x = _layer(x, c, w_ada[l], b_ada[l], w_in[l], ssm_a_re[l], ssm_a_im[l], ssm_log_dt[l],
                   ssm_b_re[l], ssm_b_im[l], ssm_c_re[l], ssm_c_im[l], ssm_d[l],
                   w_glu_lin[l], w_glu_gate[l], cmp_pe_k[l], cmp_pe_v[l],
                   w_cmp_k1[l], w_cmp_k2[l], w_cmp_v1[l], w_cmp_v2[l], w_nsa_proj[l], w_out[l],
                   ln1_g[l], ln1_b[l], w_ffn_gate[l], w_ffn_up[l], w_ffn_down[l], ln2_g[l], ln2_b[l])
    return x
```

```python
import functools
import math

import jax
import jax.numpy as jnp
from jax import lax
from jax.experimental import pallas as pl
from jax.experimental.pallas import tpu as pltpu

F32 = jnp.float32
BF16 = jnp.bfloat16
HIGHEST = lax.Precision.HIGHEST

D_MODEL = 1024
SSM_WIDTH = D_MODEL // 2
SSM_GROUP_SIZE = 16
SSM_GROUPS = SSM_WIDTH // SSM_GROUP_SIZE
SSM_STATE = 64
SSM_CHUNK = 16
SSM_GROUP_TILE = 8
SSM_BATCH_TILE = 2
SSM_STATE_ROW_PAD = 8
N_HEADS = 8
HEAD_DIM = 64
N_KV_GROUPS = 2
HEADS_PER_GROUP = N_HEADS // N_KV_GROUPS
ATTN_WIDTH = N_HEADS * HEAD_DIM
KV_WIDTH = N_KV_GROUPS * HEAD_DIM
CMP_BLOCK = 32
CMP_STRIDE = 16
CMP_HIDDEN = HEAD_DIM
SEL_BLOCK = 64
SEL_TOP_K = 8
WINDOW = 256
N_NSA_BRANCHES = 3
GATE_ROWS = 16
FFN_HIDDEN = (8 * D_MODEL + 3 * 256 - 1) // (3 * 256) * 256
DEEPNORM_ALPHA = 2.0 ** 0.25
LN_EPS = 1e-5
MASK_VALUE = -1e30

VMEM_LIMIT = 56 * 1024 * 1024
MXU_WIDTH = 256
BF16_ROWS = 16
ROW_BLOCK = 256

KEY_BLOCK = 256
SEL_SHIFT = SEL_BLOCK.bit_length() - 1
SEL_PER_KEY_BLOCK = KEY_BLOCK // SEL_BLOCK
KEY_AUG = 128
ALIBI_COL = HEAD_DIM
N_PIECES = 4
MASK_COL = HEAD_DIM + 16
QUERY_AUG = HEAD_DIM + 32
VAL_AUG = HEAD_DIM + 16
LOG2E = math.log2(math.e)
SCORES_AHEAD = 5

NAT_COLS = SSM_WIDTH + 4 * KV_WIDTH
TR_ROWS = ATTN_WIDTH + 2 * KV_WIDTH + N_KV_GROUPS * GATE_ROWS


def _bf16_pieces(value, n):
    pieces = []
    rest = value
    for _ in range(n):
        mant, expo = math.frexp(rest)
        piece = math.ldexp(round(mant * 256.0) / 256.0, expo)
        pieces.append(piece)
        rest -= piece
    return pieces


def _sigmoid(x):
    return 1.0 / (1.0 + jnp.exp(-x))


def _gelu(x):
    c = math.sqrt(2.0 / math.pi)
    return 0.5 * x * (1.0 + jnp.tanh(c * (x + 0.044715 * (x * x * x))))


def _layer_norm(y, gain, bias):
    mu = jnp.mean(y, axis=-1, keepdims=True)
    d = y - mu
    var = jnp.mean(d * d, axis=-1, keepdims=True)
    return d * lax.rsqrt(var + LN_EPS) * gain + bias


def _params(n_axes):
    return pltpu.CompilerParams(dimension_semantics=("arbitrary",) * n_axes, vmem_limit_bytes=VMEM_LIMIT)


def _ada_kernel(c_ref, w_ref, b_ref, o_ref):
    c = c_ref[...]
    a = c * _sigmoid(c)
    w = w_ref[...].astype(BF16)
    a_hi = a.astype(BF16)
    a_lo = (a - a_hi.astype(F32)).astype(BF16)
    mod = (jnp.dot(a_hi, w, preferred_element_type=F32) + jnp.dot(a_lo, w, preferred_element_type=F32)) + b_ref[...]
    for k in range(o_ref.shape[0]):
        o_ref[k] = mod[:, k * D_MODEL:(k + 1) * D_MODEL]


def _ada_mod(c, w_ada, b_ada):
    bsz = c.shape[0]
    per_step = 2
    return pl.pallas_call(
        _ada_kernel,
        grid=(6 // per_step,),
        in_specs=[
            pl.BlockSpec((bsz, D_MODEL), lambda j: (0, 0)),
            pl.BlockSpec((D_MODEL, per_step * D_MODEL), lambda j: (0, j)),
            pl.BlockSpec((1, per_step * D_MODEL), lambda j: (0, j)),
        ],
        out_specs=pl.BlockSpec((per_step, bsz, D_MODEL), lambda j: (j, 0, 0)),
        out_shape=jax.ShapeDtypeStruct((6, bsz, D_MODEL), F32),
        compiler_params=_params(1),
        name="ada_mod",
    )(c, w_ada, b_ada.reshape(1, 6 * D_MODEL))


def _in_proj_kernel(x_ref, sc_ref, sh_ref, wn_ref, wt_ref, ussm_ref, kc_ref, kp_ref, qT_ref, vp_ref, bgT_ref):
    i = pl.program_id(1)
    tm = x_ref.shape[1]
    u = (x_ref[0] * (1.0 + sc_ref[0]) + sh_ref[0]).astype(BF16)
    nat = jnp.dot(u, wn_ref[...], preferred_element_type=F32)
    ussm_ref[0] = nat[:, :SSM_WIDTH]
    for s in range(2):
        for g in range(N_KV_GROUPS):
            lo = SSM_WIDTH + s * KV_WIDTH + g * HEAD_DIM
            kc_ref[0, s, g] = nat[:, lo:lo + HEAD_DIM]
    pos = i * tm + lax.broadcasted_iota(jnp.int32, (tm, KEY_AUG), 0)
    col = lax.broadcasted_iota(jnp.int32, (tm, KEY_AUG), 1)
    blk = lax.shift_right_logical(pos, SEL_SHIFT)
    in_a = (col >= ALIBI_COL) & (col < ALIBI_COL + N_PIECES)
    in_b = (col >= ALIBI_COL + N_PIECES) & (col < ALIBI_COL + 2 * N_PIECES)
    hot = col == MASK_COL + (blk & (SEL_PER_KEY_BLOCK - 1))
    aux = jnp.where(in_a, blk * SEL_BLOCK, jnp.where(in_b, pos & (SEL_BLOCK - 1), jnp.where(hot, 1, 0))).astype(F32)
    head_lanes = col < HEAD_DIM
    for s in range(2):
        lo = SSM_WIDTH + (2 + s) * KV_WIDTH
        both = nat[:, lo:lo + KV_WIDTH]
        for g in range(N_KV_GROUPS):
            keys = both if g == 0 else pltpu.roll(both, HEAD_DIM, 1)
            kp_ref[0, s, g] = jnp.where(head_lanes, keys, aux).astype(BF16)
    tr = lax.dot_general(wt_ref[...], u, (((1,), (1,)), ((), ())), preferred_element_type=F32)
    qT_ref[0] = tr[:ATTN_WIDTH] * (HEAD_DIM ** -0.5)
    ones_rows = jnp.where(lax.broadcasted_iota(jnp.int32, (VAL_AUG - HEAD_DIM, tm), 0) == 0, 1.0, 0.0)
    for s in range(2):
        for g in range(N_KV_GROUPS):
            lo = ATTN_WIDTH + (s * N_KV_GROUPS + g) * HEAD_DIM
            v_aug = jnp.concatenate([tr[lo:lo + HEAD_DIM], ones_rows], axis=0).astype(BF16)
            for c in range(tm // KEY_BLOCK):
                vp_ref[0, s, g, c] = v_aug[:, c * KEY_BLOCK:(c + 1) * KEY_BLOCK]
    bgT_ref[0] = _sigmoid(tr[ATTN_WIDTH + 2 * KV_WIDTH:])


def _in_proj(x, scale1, shift1, w_nat, w_tr, *, tm):
    bsz, seq, _ = x.shape
    n_t = seq // tm
    n_kb = seq // KEY_BLOCK
    out_shapes = (
        jax.ShapeDtypeStruct((bsz, seq, SSM_WIDTH), F32),
        jax.ShapeDtypeStruct((bsz, 2, N_KV_GROUPS, seq, HEAD_DIM), F32),
        jax.ShapeDtypeStruct((bsz, 2, N_KV_GROUPS, seq, KEY_AUG), BF16),
        jax.ShapeDtypeStruct((bsz, ATTN_WIDTH, seq), F32),
        jax.ShapeDtypeStruct((bsz, 2, N_KV_GROUPS, n_kb, VAL_AUG, KEY_BLOCK), BF16),
        jax.ShapeDtypeStruct((bsz, N_KV_GROUPS * GATE_ROWS, seq), F32),
    )
    return pl.pallas_call(
        _in_proj_kernel,
        grid=(bsz, n_t),
        in_specs=[
            pl.BlockSpec((1, tm, D_MODEL), lambda b, i: (b, i, 0)),
            pl.BlockSpec((1, 1, D_MODEL), lambda b, i: (b, 0, 0)),
            pl.BlockSpec((1, 1, D_MODEL), lambda b, i: (b, 0, 0)),
            pl.BlockSpec((D_MODEL, NAT_COLS), lambda b, i: (0, 0)),
            pl.BlockSpec((TR_ROWS, D_MODEL), lambda b, i: (0, 0)),
        ],
        out_specs=(
            pl.BlockSpec((1, tm, SSM_WIDTH), lambda b, i: (b, i, 0)),
            pl.BlockSpec((1, 2, N_KV_GROUPS, tm, HEAD_DIM), lambda b, i: (b, 0, 0, i, 0)),
            pl.BlockSpec((1, 2, N_KV_GROUPS, tm, KEY_AUG), lambda b, i: (b, 0, 0, i, 0)),
            pl.BlockSpec((1, ATTN_WIDTH, tm), lambda b, i: (b, 0, i)),
            pl.BlockSpec((1, 2, N_KV_GROUPS, tm // KEY_BLOCK, VAL_AUG, KEY_BLOCK), lambda b, i: (b, 0, 0, i, 0, 0)),
            pl.BlockSpec((1, N_KV_GROUPS * GATE_ROWS, tm), lambda b, i: (b, 0, i)),
        ),
        out_shape=out_shapes,
        compiler_params=_params(2),
        name="in_proj",
    )(x, scale1, shift1, w_nat, w_tr)


def _split_w_in(w_in):
    o_q = SSM_WIDTH
    o_kv = o_q + ATTN_WIDTH
    o_bg = o_kv + 6 * KV_WIDTH
    o_mg = o_bg + N_NSA_BRANCHES * N_HEADS
    kv = [w_in[:, o_kv + s * KV_WIDTH:o_kv + (s + 1) * KV_WIDTH] for s in range(6)]
    w_nat = jnp.concatenate([w_in[:, :o_q], kv[0], kv[1], kv[2], kv[4]], axis=1)
    per_group = HEADS_PER_GROUP * N_NSA_BRANCHES
    bg = w_in[:, o_bg:o_mg].reshape(D_MODEL, N_KV_GROUPS, per_group)
    bg = jnp.pad(bg, ((0, 0), (0, 0), (0, GATE_ROWS - per_group))).reshape(D_MODEL, N_KV_GROUPS * GATE_ROWS)
    w_tr = jnp.concatenate([w_in[:, o_q:o_kv], kv[3], kv[5], bg], axis=1).T
    return w_nat.astype(BF16), w_tr.astype(BF16), w_in[:, o_mg:].astype(BF16)


def _ssm_prep_kernel(a_ref, logdt_ref, bT_ref, c_ref, d_ref, mT_ref, pT_ref, qT_ref, a16_ref):
    t_n = SSM_CHUNK
    c_n = SSM_GROUP_SIZE
    n_p = SSM_STATE
    rows = t_n * c_n
    dt = jnp.exp(logdt_ref[0])
    a_re = a_ref[0, 0]
    a_im = a_ref[0, 1]
    decay = jnp.exp(a_re * dt)
    ab_re = decay * jnp.cos(a_im * dt)
    ab_im = decay * jnp.sin(a_im * dt)
    n_re = ab_re - 1.0
    denom = a_re * a_re + a_im * a_im
    f_re = (n_re * a_re + ab_im * a_im) / denom
    f_im = (ab_im * a_re - n_re * a_im) / denom
    powers = [(jnp.ones_like(ab_re), jnp.zeros_like(ab_re))]
    for _ in range(t_n):
        p_re, p_im = powers[-1]
        powers.append((p_re * ab_re - p_im * ab_im, p_re * ab_im + p_im * ab_re))

    def per_token(first, step):
        picks = [powers[first + step * k] for k in range(t_n)]
        return (jnp.concatenate([jnp.broadcast_to(p[0], (c_n, 2 * n_p)) for p in picks], axis=0),
                jnp.concatenate([jnp.broadcast_to(p[1], (c_n, 2 * n_p)) for p in picks], axis=0))

    def cmul(x_re, x_im, y_re, y_im):
        return x_re * y_re - x_im * y_im, x_re * y_im + x_im * y_re

    bb_re, bb_im = cmul(f_re, f_im, bT_ref[0, 0], bT_ref[0, 1])
    bt_re = jnp.concatenate([bb_re] * t_n, axis=0)
    bt_im = jnp.concatenate([bb_im] * t_n, axis=0)
    ct_re = jnp.concatenate([c_ref[0, 0]] * t_n, axis=0)
    ct_im = jnp.concatenate([c_ref[0, 1]] * t_n, axis=0)

    p_re, p_im = cmul(*per_token(t_n - 1, -1), bt_re, bt_im)
    low_half = lax.broadcasted_iota(jnp.int32, (1, 2 * n_p), 1) < n_p
    pT_ref[0] = jnp.where(low_half, p_re, p_im).T.astype(BF16)

    q_re, q_im = cmul(ct_re, ct_im, *per_token(1, 1))
    qT_ref[0] = jnp.where(low_half, q_re, -q_im).astype(BF16)
    a16_ref[0] = jnp.concatenate([powers[t_n][0][:, :n_p], powers[t_n][1][:, :n_p]], axis=0)

    k_re, k_im = cmul(ct_re, ct_im, *per_token(0, 1))
    nt = (((1,), (1,)), ((), ()))
    strips = (lax.dot_general(_split3(k_re[:, :n_p]), _stack3(bt_re[:, :n_p], axis=1), nt, preferred_element_type=F32)
              - lax.dot_general(_split3(k_im[:, :n_p]), _stack3(bt_im[:, :n_p], axis=1), nt,
                                preferred_element_type=F32))
    lane = lax.broadcasted_iota(jnp.int32, (1, rows), 1)
    s_lane = lax.shift_right_logical(lane, c_n.bit_length() - 1)
    ci_lane = lane & (c_n - 1)
    row = lax.broadcasted_iota(jnp.int32, (rows, 1), 0)
    t_row = lax.shift_right_logical(row, c_n.bit_length() - 1)
    co_row = row & (c_n - 1)
    blocks = []
    for t in range(t_n):
        acc = jnp.zeros((c_n, rows), F32)
        for lag in range(t + 1):
            acc = jnp.where(s_lane == t - lag, strips[lag * c_n:(lag + 1) * c_n, :], acc)
        blocks.append(acc)
    d_rows = jnp.concatenate([d_ref[0]] * t_n, axis=0)
    skip = jnp.where((t_row == s_lane) & (co_row == ci_lane), d_rows, 0.0)
    mT_ref[0] = (jnp.concatenate(blocks, axis=0) + skip).astype(BF16)


def _ssm_matrices(a_re, a_im, log_dt, b_re, b_im, c_re, c_im, d_skip):
    n_g, n_p = a_re.shape
    c_n = SSM_GROUP_SIZE
    rows = SSM_CHUNK * c_n
    twice = lambda x: jnp.concatenate([x, x], axis=-1).astype(F32)
    a = twice(jnp.stack([a_re, a_im], axis=1)).reshape(n_g, 2, 1, 2 * n_p)
    bT = twice(jnp.swapaxes(jnp.stack([b_re, b_im], axis=1), 2, 3))
    c = twice(jnp.stack([c_re, c_im], axis=1))
    spec = lambda shape: pl.BlockSpec((1,) + shape, lambda g: (g,) + (0,) * len(shape))
    mT, pT, qT, a16 = pl.pallas_call(
        _ssm_prep_kernel,
        grid=(n_g,),
        in_specs=[spec((2, 1, 2 * n_p)), spec((1, 1)), spec((2, c_n, 2 * n_p)), spec((2, c_n, 2 * n_p)),
                  spec((c_n, 1))],
        out_specs=(spec((rows, rows)), spec((2 * n_p, rows)), spec((rows, 2 * n_p)), spec((2, n_p))),
        out_shape=(
            jax.ShapeDtypeStruct((n_g, rows, rows), BF16),
            jax.ShapeDtypeStruct((n_g, 2 * n_p, rows), BF16),
            jax.ShapeDtypeStruct((n_g, rows, 2 * n_p), BF16),
            jax.ShapeDtypeStruct((n_g, 2, n_p), F32),
        ),
        compiler_params=_params(1),
        name="ssm_prep",
    )(a, log_dt.astype(F32).reshape(n_g, 1, 1), bT, c, d_skip.astype(F32).reshape(n_g, c_n, 1))
    return mT, pT.reshape(n_g, 2, n_p, rows), qT, a16


def _ssm_kernel(u_ref, mT_ref, pT_ref, qT_ref, a16_ref, o_ref, ut_scr, yt_scr, sre, sim, *, n_chunks):
    n_g = SSM_GROUP_TILE
    n_b = u_ref.shape[0]
    t_n = SSM_CHUNK
    c_n = SSM_GROUP_SIZE
    lanes = [(e, g) for e in range(n_b) for g in range(n_g)]

    for e in range(n_b):
        for s in range(t_n):
            x_s = u_ref[e, pl.ds(s, n_chunks, stride=t_n), :]
            ut_scr[e * n_g:(e + 1) * n_g, s * c_n:(s + 1) * c_n, :] = (
                x_s.T.reshape(n_g, c_n, n_chunks).astype(BF16))

    pitch = n_chunks + SSM_STATE_ROW_PAD

    def lane_rows(k):
        return pl.ds(k * pitch, n_chunks)

    for k, (e, g) in enumerate(lanes):
        ut = ut_scr[k]
        sre[lane_rows(k), :] = jnp.dot(pT_ref[g, 0], ut, preferred_element_type=F32).T
        sim[lane_rows(k), :] = jnp.dot(pT_ref[g, 1], ut, preferred_element_type=F32).T

    a_r = jnp.concatenate([a16_ref[pl.ds(0, n_g, stride=2), :]] * n_b, axis=0)
    a_i = jnp.concatenate([a16_ref[pl.ds(1, n_g, stride=2), :]] * n_b, axis=0)

    def carry_states(c, h):
        h_r, h_i = h
        rows = pl.ds(c, len(lanes), stride=pitch)
        s_r = sre[rows, :]
        s_i = sim[rows, :]
        sre[rows, :] = h_r
        sim[rows, :] = h_i
        return a_r * h_r - a_i * h_i + s_r, a_r * h_i + a_i * h_r + s_i

    zero = jnp.zeros((len(lanes), SSM_STATE), F32)
    lax.fori_loop(0, n_chunks, carry_states, (zero, zero), unroll=4)

    nt = (((1,), (1,)), ((), ()))
    for k, (e, g) in enumerate(lanes):
        yt = jnp.dot(mT_ref[g], ut_scr[k], preferred_element_type=F32)
        entering = jnp.concatenate([sre[lane_rows(k), :], sim[lane_rows(k), :]], axis=1).astype(BF16)
        yt = yt + lax.dot_general(qT_ref[g], entering, nt, preferred_element_type=F32)
        yt_scr[k] = _gelu(yt)

    for e in range(n_b):
        for t in range(t_n):
            z = yt_scr[e * n_g:(e + 1) * n_g, t * c_n:(t + 1) * c_n, :].reshape(n_g * c_n, n_chunks)
            o_ref[e, pl.ds(t, n_chunks, stride=t_n), :] = z.T


def _ssm(u_ssm, mats):
    bsz, seq, width = u_ssm.shape
    n_chunks = seq // SSM_CHUNK
    mT, pT, qT, a16 = mats
    gt = SSM_GROUP_TILE
    n_tiles = SSM_GROUPS // gt
    rows = SSM_CHUNK * SSM_GROUP_SIZE
    a16 = a16.reshape(2 * SSM_GROUPS, SSM_STATE)
    bt = SSM_BATCH_TILE if bsz % SSM_BATCH_TILE == 0 else 1
    n_lanes = bt * gt
    per_tile = lambda a: pl.BlockSpec((gt,) + a.shape[1:], lambda t, b: (t,) + (0,) * (a.ndim - 1))
    return pl.pallas_call(
        functools.partial(_ssm_kernel, n_chunks=n_chunks),
        grid=(n_tiles, bsz // bt),
        in_specs=[
            pl.BlockSpec((bt, seq, gt * SSM_GROUP_SIZE), lambda t, b: (b, 0, t)),
            per_tile(mT), per_tile(pT), per_tile(qT),
            pl.BlockSpec((2 * gt, SSM_STATE), lambda t, b: (t, 0)),
        ],
        out_specs=pl.BlockSpec((bt, seq, gt * SSM_GROUP_SIZE), lambda t, b: (b, 0, t)),
        out_shape=jax.ShapeDtypeStruct(u_ssm.shape, F32),
        scratch_shapes=[
            pltpu.VMEM((n_lanes, rows, n_chunks), BF16),
            pltpu.VMEM((n_lanes, rows, n_chunks), F32),
            pltpu.VMEM((n_lanes * (n_chunks + SSM_STATE_ROW_PAD), SSM_STATE), F32),
            pltpu.VMEM((n_lanes * (n_chunks + SSM_STATE_ROW_PAD), SSM_STATE), F32),
        ],
        compiler_params=_params(2),
        name="ssm",
    )(u_ssm, mT, pT, qT, a16)


def _split3(x):
    hi = x.astype(BF16)
    lo = (x - hi.astype(F32)).astype(BF16)
    return jnp.concatenate([hi, lo, hi], axis=-1)


def _stack3(w, axis=0):
    hi = w.astype(BF16)
    lo = (w - hi.astype(F32)).astype(BF16)
    return jnp.concatenate([hi, hi, lo], axis=axis)


def _compress_kernel(kx_ref, vx_ref, pek_ref, pev_ref, wk1_ref, wk2_ref, wv1_ref, wv2_ref, kc_ref, vcT_ref):
    def mlp(x_ref, pe_ref, w1_ref, w2_ref):
        n = x_ref.shape[3] // CMP_STRIDE
        x = jnp.concatenate([x_ref[0, 0, 0, pl.ds(l, n, stride=CMP_STRIDE), :] for l in range(CMP_STRIDE)], axis=1)
        both = jnp.dot(_split3(x), w1_ref[...], preferred_element_type=F32)
        pe_both = jnp.dot(_split3(pe_ref[...]), w1_ref[...], preferred_element_type=F32)
        bias = pe_both[0:1, :CMP_HIDDEN] + pe_both[1:2, CMP_HIDDEN:]
        h = _gelu(both[:, :CMP_HIDDEN] + pltpu.roll(both[:, CMP_HIDDEN:], n - 1, 0) + bias)
        return jnp.dot(_split3(h), w2_ref[...], preferred_element_type=F32)

    kc = mlp(kx_ref, pek_ref, wk1_ref, wk2_ref)
    kc_hi = kc.astype(BF16)
    kc_lo = (kc - kc_hi.astype(F32)).astype(BF16)
    kc_ref[0, 0] = jnp.concatenate([kc_hi, kc_hi, kc_lo, jnp.zeros_like(kc_hi)], axis=1)
    vcT_ref[0, 0] = mlp(vx_ref, pev_ref, wv1_ref, wv2_ref).T.astype(BF16)


def _compress(kv, pe_k, pe_v, wk1, wk2, wv1, wv2):
    bsz, _, n_g, seq, width = kv.shape
    n_chunks = seq // CMP_STRIDE
    half = CMP_STRIDE * HEAD_DIM
    side_by_side = lambda w1: _stack3(jnp.concatenate([w1[:half], w1[half:]], axis=1))
    two_rows = lambda pe: jnp.pad(pe.reshape(2, half), ((0, BF16_ROWS - 2), (0, 0)))
    pe_k, pe_v = two_rows(pe_k), two_rows(pe_v)
    wk1, wv1, wk2, wv2 = side_by_side(wk1), side_by_side(wv1), _stack3(wk2), _stack3(wv2)
    blk = lambda s: pl.BlockSpec((1, 1, 1, seq, width), lambda b, g: (b, s, g, 0, 0))
    full = lambda a: pl.BlockSpec(a.shape, lambda b, g: (0, 0))
    return pl.pallas_call(
        _compress_kernel,
        grid=(bsz, n_g),
        in_specs=[blk(0), blk(1), full(pe_k), full(pe_v), full(wk1), full(wk2), full(wv1), full(wv2)],
        out_specs=(
            pl.BlockSpec((1, 1, n_chunks, 4 * HEAD_DIM), lambda b, g: (b, g, 0, 0)),
            pl.BlockSpec((1, 1, HEAD_DIM, n_chunks), lambda b, g: (b, g, 0, 0)),
        ),
        out_shape=(
            jax.ShapeDtypeStruct((bsz, n_g, n_chunks, 4 * HEAD_DIM), BF16),
            jax.ShapeDtypeStruct((bsz, n_g, HEAD_DIM, n_chunks), BF16),
        ),
        compiler_params=_params(2),
        name="compress",
    )(kv, kv, pe_k, pe_v, wk1, wk2, wv1, wv2)


def _nsa_kernel(qT_ref, kc_ref, vcT_ref, ks_ref, vs_ref, kw_ref, vw_ref, bg_ref, o_ref,
                negm_scr, qa_scr, m_scr, acc_scr, out_scr, used_smem, todo_smem, *, tq, n_cmp, n_sel):
    g = pl.program_id(1)
    i = pl.program_id(2)
    t_row = i * tq + lax.broadcasted_iota(jnp.int32, (1, tq), 1)
    slopes = [jnp.where(g == 0, 2.0 ** -(hh + 1), 2.0 ** -(HEADS_PER_GROUP + hh + 1)).astype(F32)
              for hh in range(HEADS_PER_GROUP)]

    def gate(hh, branch):
        r = hh * N_NSA_BRANCHES + branch
        return bg_ref[0, 0, r:r + 1, :]

    def head_rows(hh):
        return slice(hh * HEAD_DIM, (hh + 1) * HEAD_DIM)

    def head_lanes(hh):
        return slice(hh * tq, (hh + 1) * tq)

    kc3 = kc_ref[0, 0]
    vcT = vcT_ref[0, 0]
    cmp_end = lax.broadcasted_iota(jnp.int32, (n_cmp, 1), 0) * CMP_STRIDE + (CMP_BLOCK - 1)
    dist_c = (t_row - cmp_end).astype(F32)
    valid_c = dist_c >= 0.0
    piece = lax.broadcasted_iota(jnp.int32, (16, 1), 0)
    log2e_pieces = _bf16_pieces(LOG2E, N_PIECES)
    log2e_col = jnp.zeros((16, 1), F32)
    for k, value in enumerate(log2e_pieces):
        log2e_col = jnp.where((piece == k) | (piece == N_PIECES + k), value, log2e_col)
    for hh in range(HEADS_PER_GROUP):
        alibi = jnp.broadcast_to(log2e_col * slopes[hh], (16, tq)).astype(BF16)
        q_l2 = (qT_ref[0, head_rows(hh), :] * LOG2E).astype(BF16)
        qa_scr[:, head_lanes(hh)] = jnp.concatenate([q_l2, alibi], axis=0)

    def cmp_scores(hh):
        q_h = qT_ref[0, head_rows(hh), :]
        q_hi = q_h.astype(BF16)
        q_lo = (q_h - q_hi.astype(F32)).astype(BF16)
        q3 = jnp.concatenate([q_hi, q_lo, q_hi, jnp.zeros_like(q_hi)], axis=0)
        return jnp.dot(kc3, q3, preferred_element_type=F32)

    def cmp_attend(p_heads, hh, s):
        s = jnp.where(valid_c, s - slopes[hh] * dist_c, MASK_VALUE)
        m = jnp.max(s, axis=0, keepdims=True)
        p = jnp.where(valid_c, jnp.exp(s - m), 0.0)
        l = jnp.sum(p, axis=0, keepdims=True)
        p = p / jnp.where(l > 0.0, l, 1.0)
        o_c = jnp.dot(vcT, p.astype(BF16), preferred_element_type=F32)
        out_scr[head_rows(hh), :] = gate(hh, 0) * o_c
        p_heads.append(p)

    def select_blocks(p_heads, _):
        p_sum = sum(p_heads[1:], p_heads[0])
        blk = lax.broadcasted_iota(jnp.int32, (n_sel, n_cmp), 0)
        cmp_ix = lax.broadcasted_iota(jnp.int32, (n_sel, n_cmp), 1)
        ratio = SEL_BLOCK // CMP_STRIDE
        extra = CMP_BLOCK // CMP_STRIDE - 1
        overlap_t = jnp.where((cmp_ix >= ratio * blk - extra) & (cmp_ix <= ratio * blk + ratio - 1),
                              1.0, 0.0).astype(BF16)
        imp = jnp.zeros((n_sel, tq), F32)
        rest = p_sum
        for _ in range(3):
            part = rest.astype(BF16)
            imp = imp + jnp.dot(overlap_t, part, preferred_element_type=F32)
            rest = rest - part.astype(F32)

        j_col = lax.broadcasted_iota(jnp.int32, (n_sel, 1), 0)
        forced = (j_col == 0) | (j_col == lax.shift_right_logical(t_row, SEL_SHIFT))
        future = j_col * SEL_BLOCK > t_row
        score = jnp.where(forced, jnp.inf, jnp.where(future, -jnp.inf, imp))
        rank = jnp.zeros((n_sel, tq), F32)
        for jp in range(n_sel):
            other = score[jp:jp + 1, :]
            ahead = (other > score) | ((other == score) & (j_col > jp))
            rank = rank + jnp.where(ahead, 1.0, 0.0)
        neg = jnp.where(rank < float(min(SEL_TOP_K, n_sel)), 0.0, MASK_VALUE)
        pad_rows = jnp.zeros((BF16_ROWS - SEL_PER_KEY_BLOCK, tq), F32)
        for c in range(n_sel // SEL_PER_KEY_BLOCK):
            rows = neg[c * SEL_PER_KEY_BLOCK:(c + 1) * SEL_PER_KEY_BLOCK, :]
            negm_scr[c] = jnp.concatenate([rows, pad_rows], axis=0).astype(BF16)
            used_smem[c] = (jnp.max(rows) > 0.5 * MASK_VALUE).astype(jnp.int32)

    rel = (lax.broadcasted_iota(jnp.int32, (KEY_BLOCK, tq), 1)
           - lax.broadcasted_iota(jnp.int32, (KEY_BLOCK, tq), 0))
    zero_tail = jnp.zeros((KEY_AUG - QUERY_AUG, tq), BF16)
    no_mask = jnp.zeros((BF16_ROWS, tq), BF16)
    all_masked = jnp.full((BF16_ROWS, tq), MASK_VALUE, BF16)

    sel_even, sel_odd, win_prev, win_diag = range(4)
    m_scr[...] = jnp.full(m_scr.shape, MASK_VALUE, F32)
    acc_scr[...] = jnp.zeros(acc_scr.shape, F32)

    def scores(job, hh):
        k_ref, _, kb, enabled, _, _ = job
        k0 = pl.multiple_of(kb * KEY_BLOCK, KEY_BLOCK)
        k_aug = k_ref[0, 0, 0, pl.ds(k0, KEY_BLOCK), :]
        if enabled is None:
            mask_tile = no_mask
        elif enabled is True:
            mask_tile = negm_scr[kb]
        else:
            mask_tile = jnp.where(enabled, negm_scr[kb], all_masked)
        q_aug = jnp.concatenate([qa_scr[:, head_lanes(hh)], mask_tile, zero_tail], axis=0)
        return jnp.dot(k_aug, q_aug, preferred_element_type=F32)

    def accumulate(job, hh, s):
        _, v_ref, kb, _, valid, slot = job
        cols = head_lanes(hh)
        if valid is not None:
            s = jnp.where(valid, s, MASK_VALUE)
        m_old = m_scr[slot, :, cols]
        m_new = jnp.maximum(m_old, jnp.max(s, axis=0, keepdims=True))
        p = jnp.exp2(s - jnp.maximum(m_new, 0.1 * MASK_VALUE))
        alpha = jnp.exp2(m_old - m_new)
        pv = jnp.dot(v_ref[0, 0, 0, kb], p.astype(BF16), preferred_element_type=F32)
        acc_scr[slot, :, cols] = alpha * acc_scr[slot, :, cols] + pv
        m_scr[slot, :, cols] = m_new

    def pipeline(units):
        issue = lambda unit: None if unit[0] is None else unit[0]()
        pending = [issue(unit) for unit in units[:SCORES_AHEAD]]
        for n, unit in enumerate(units):
            if n + SCORES_AHEAD < len(units):
                pending.append(issue(units[n + SCORES_AHEAD]))
            unit[1](pending.pop(0))

    def attention_units(jobs):
        return [(functools.partial(scores, job, hh), functools.partial(accumulate, job, hh))
                for job in jobs for hh in range(HEADS_PER_GROUP)]

    def finish(branch, slot_a, slot_b):
        m_a = m_scr[slot_a]
        m_b = m_scr[slot_b]
        m = jnp.maximum(m_a, m_b)
        w_a = jnp.exp2(m_a - m)
        w_b = jnp.exp2(m_b - m)
        for hh in range(HEADS_PER_GROUP):
            cols = head_lanes(hh)
            acc = acc_scr[slot_a, :, cols] * w_a[:, cols] + acc_scr[slot_b, :, cols] * w_b[:, cols]
            l = acc[HEAD_DIM:HEAD_DIM + 1, :]
            o_b = acc[:HEAD_DIM, :] / jnp.where(l > 0.0, l, 1.0)
            out_scr[head_rows(hh), :] = out_scr[head_rows(hh), :] + gate(hh, branch) * o_b

    prev = jnp.maximum(i - 1, 0)
    causal = rel >= 0
    off_without_prev = jnp.where(i > 0, 0, WINDOW + KEY_BLOCK)
    dist_prev = rel + KEY_BLOCK + off_without_prev
    p_heads = []
    pipeline(
        [(functools.partial(cmp_scores, hh), functools.partial(cmp_attend, p_heads, hh))
         for hh in range(HEADS_PER_GROUP)]
        + [(None, functools.partial(select_blocks, p_heads))]
        + attention_units([(kw_ref, vw_ref, prev, None, dist_prev < WINDOW, win_prev),
                           (kw_ref, vw_ref, i, None, causal, win_diag),
                           (ks_ref, vs_ref, prev, i > 0, None, sel_even),
                           (ks_ref, vs_ref, i, True, causal, sel_odd)]))

    n_todo = jnp.int32(0)
    for c in range(used_smem.shape[0]):
        take = jnp.logical_and(c < i - 1, used_smem[c] > 0)

        @pl.when(take)
        def _(c=c, slot=n_todo):
            todo_smem[slot] = c

        n_todo = n_todo + take.astype(jnp.int32)
    todo_smem[n_todo] = 0

    def past_pair(j, carry):
        pipeline(attention_units([(ks_ref, vs_ref, todo_smem[2 * j], True, None, sel_even),
                                  (ks_ref, vs_ref, todo_smem[2 * j + 1], 2 * j + 1 < n_todo, None, sel_odd)]))
        return carry

    lax.fori_loop(0, lax.shift_right_logical(n_todo + 1, 1), past_pair, 0)
    finish(1, sel_even, sel_odd)
    finish(2, win_prev, win_diag)

    o_ref[0] = out_scr[...].T


def _nsa(qT, kc3, vcT, k_aug, v_aug, bgT, *, tq):
    bsz, _, seq = qT.shape
    assert tq == KEY_BLOCK == WINDOW and seq % tq == 0
    n_cmp = kc3.shape[2]
    n_sel = seq // SEL_BLOCK
    n_kb = seq // KEY_BLOCK
    gw = HEADS_PER_GROUP * HEAD_DIM
    bg4 = bgT.reshape(bsz, N_KV_GROUPS, GATE_ROWS, seq)
    k_spec = lambda s: pl.BlockSpec((1, 1, 1, seq, KEY_AUG), lambda b, g, i: (b, s, g, 0, 0))
    v_spec = lambda s: pl.BlockSpec((1, 1, 1, n_kb, VAL_AUG, KEY_BLOCK), lambda b, g, i: (b, s, g, 0, 0, 0))
    return pl.pallas_call(
        functools.partial(_nsa_kernel, tq=tq, n_cmp=n_cmp, n_sel=n_sel),
        grid=(bsz, N_KV_GROUPS, seq // tq),
        in_specs=[
            pl.BlockSpec((1, gw, tq), lambda b, g, i: (b, g, i)),
            pl.BlockSpec((1, 1, n_cmp, 4 * HEAD_DIM), lambda b, g, i: (b, g, 0, 0)),
            pl.BlockSpec((1, 1, HEAD_DIM, n_cmp), lambda b, g, i: (b, g, 0, 0)),
            k_spec(0), v_spec(0),
            k_spec(1), v_spec(1),
            pl.BlockSpec((1, 1, GATE_ROWS, tq), lambda b, g, i: (b, g, 0, i)),
        ],
        out_specs=pl.BlockSpec((1, tq, gw), lambda b, g, i: (b, i, g)),
        out_shape=jax.ShapeDtypeStruct((bsz, seq, ATTN_WIDTH), F32),
        scratch_shapes=[
            pltpu.VMEM((n_kb, 16, tq), BF16),
            pltpu.VMEM((HEAD_DIM + 16, HEADS_PER_GROUP * tq), BF16),
            pltpu.VMEM((4, 1, HEADS_PER_GROUP * tq), F32),
            pltpu.VMEM((4, VAL_AUG, HEADS_PER_GROUP * tq), F32),
            pltpu.VMEM((gw, tq), F32),
            pltpu.SMEM((n_kb,), jnp.int32),
            pltpu.SMEM((n_kb + 1,), jnp.int32),
        ],
        compiler_params=_params(3),
        name="nsa",
    )(qT, kc3, vcT, k_aug, v_aug, k_aug, v_aug, bg4)


def _merge_kernel(x_ref, g_ref, o_ref_in, sc_ref, sh_ref, gt_ref, wm_ref, wl_ref, wg_ref, wn_ref, wo_ref,
                  lg_ref, lb_ref, y_ref):
    tm = x_ref.shape[1]
    row_blocks = [slice(r, r + ROW_BLOCK) for r in range(0, tm, ROW_BLOCK)]
    halves = [(0, D_MODEL // 2), (D_MODEL // 2, D_MODEL)]
    dot = functools.partial(jnp.dot, preferred_element_type=F32)

    def first_stage(rows):
        g = g_ref[0, rows, :].astype(BF16)
        o = o_ref_in[0, rows, :].astype(BF16)
        branch = [(dot(g, wl_ref[:, lo:hi]), dot(g, wg_ref[:, lo:hi]), dot(o, wn_ref[:, lo:hi])) for lo, hi in halves]
        x = x_ref[0, rows, :]
        u = (x * (1.0 + sc_ref[0]) + sh_ref[0]).astype(BF16)
        gates = [(dot(u, wm_ref[:, lo:hi]), dot(u, wm_ref[:, D_MODEL + lo:D_MODEL + hi])) for lo, hi in halves]
        return x, branch, gates

    def second_stage(rows, x, branch, gates):
        mix = None
        for (lo, hi), (lin, gate, z_nsa), (m_ssm, m_nsa) in zip(halves, branch, gates):
            merged = _sigmoid(m_ssm) * (lin * _sigmoid(gate)) + _sigmoid(m_nsa) * z_nsa
            part = dot(merged.astype(BF16), wo_ref[lo:hi, :])
            mix = part if mix is None else mix + part
        y = DEEPNORM_ALPHA * x + (1.0 + gt_ref[0]) * mix
        y_ref[0, rows, :] = _layer_norm(y, lg_ref[...], lb_ref[...])

    staged = None
    for rows in row_blocks + [None]:
        upcoming = first_stage(rows) if rows is not None else None
        if staged is not None:
            second_stage(*staged)
        staged = (rows,) + upcoming if rows is not None else None


def _merge(x, g_ssm, o_nsa, scale1, shift1, gate1, w_m, w_lin, w_gate, w_nsa, w_out, ln_g, ln_b, *, tm):
    bsz, seq, _ = x.shape
    row = pl.BlockSpec((1, 1, D_MODEL), lambda b, i: (b, 0, 0))
    full = lambda a: pl.BlockSpec(a.shape, lambda b, i: (0, 0), pipeline_mode=pl.Buffered(1))
    return pl.pallas_call(
        _merge_kernel,
        grid=(bsz, seq // tm),
        in_specs=[
            pl.BlockSpec((1, tm, D_MODEL), lambda b, i: (b, i, 0)),
            pl.BlockSpec((1, tm, SSM_WIDTH), lambda b, i: (b, i, 0)),
            pl.BlockSpec((1, tm, ATTN_WIDTH), lambda b, i: (b, i, 0)),
            row, row, row,
            full(w_m), full(w_lin), full(w_gate), full(w_nsa), full(w_out), full(ln_g), full(ln_b),
        ],
        out_specs=pl.BlockSpec((1, tm, D_MODEL), lambda b, i: (b, i, 0)),
        out_shape=jax.ShapeDtypeStruct(x.shape, F32),
        compiler_params=_params(2),
        name="merge",
    )(x, g_ssm, o_nsa, scale1, shift1, gate1, w_m, w_lin, w_gate, w_nsa, w_out, ln_g, ln_b)


def _ffn_kernel(x_ref, sc_ref, sh_ref, gt_ref, wg_ref, wu_ref, wd_ref, lg_ref, lb_ref, y_ref, *, n_chunks):
    tm = x_ref.shape[1]
    row_blocks = [slice(r, r + ROW_BLOCK) for r in range(0, tm, ROW_BLOCK)]
    tiles = FFN_HIDDEN // MXU_WIDTH
    edges = [MXU_WIDTH * ((tiles * c + n_chunks - 1) // n_chunks) for c in range(n_chunks + 1)]
    chunks = [slice(lo, hi) for lo, hi in zip(edges[:-1], edges[1:])]
    dot = functools.partial(jnp.dot, preferred_element_type=F32)

    def gate_up(rows):
        x = x_ref[0, rows, :]
        u = (x * (1.0 + sc_ref[0]) + sh_ref[0]).astype(BF16)
        return x, [(dot(u, wg_ref[:, cols]), dot(u, wu_ref[:, cols])) for cols in chunks]

    def down(rows, x, products):
        ffn = None
        for cols, (a, b) in zip(chunks, products):
            part = dot(((a * _sigmoid(a)) * b).astype(BF16), wd_ref[cols, :])
            ffn = part if ffn is None else ffn + part
        y = DEEPNORM_ALPHA * x + (1.0 + gt_ref[0]) * ffn
        y_ref[0, rows, :] = _layer_norm(y, lg_ref[...], lb_ref[...])

    staged = None
    for rows in row_blocks + [None]:
        upcoming = gate_up(rows) if rows is not None else None
        if staged is not None:
            down(*staged)
        staged = (rows,) + upcoming if rows is not None else None


def _ffn(x, scale2, shift2, gate2, w_gate, w_up, w_down, ln_g, ln_b, *, tm, n_chunks):
    bsz, seq, _ = x.shape
    row = pl.BlockSpec((1, 1, D_MODEL), lambda b, i: (b, 0, 0))
    resident = lambda a: pl.BlockSpec(a.shape, lambda b, i: (0, 0), pipeline_mode=pl.Buffered(1))
    return pl.pallas_call(
        functools.partial(_ffn_kernel, n_chunks=n_chunks),
        grid=(bsz, seq // tm),
        in_specs=[
            pl.BlockSpec((1, tm, D_MODEL), lambda b, i: (b, i, 0)),
            row, row, row,
            resident(w_gate), resident(w_up), resident(w_down), resident(ln_g), resident(ln_b),
        ],
        out_specs=pl.BlockSpec((1, tm, D_MODEL), lambda b, i: (b, i, 0)),
        out_shape=jax.ShapeDtypeStruct(x.shape, F32),
        compiler_params=_params(2),
        name="ffn",
    )(x, scale2, shift2, gate2, w_gate, w_up, w_down, ln_g, ln_b)


def _layer(x, c, w_ada, b_ada, w_in, ssm_a_re, ssm_a_im, ssm_log_dt, ssm_b_re, ssm_b_im, ssm_c_re, ssm_c_im,
           ssm_d, w_glu_lin, w_glu_gate, cmp_pe_k, cmp_pe_v, w_cmp_k1, w_cmp_k2, w_cmp_v1, w_cmp_v2,
           w_nsa_proj, w_out, ln1_g, ln1_b, w_ffn_gate, w_ffn_up, w_ffn_down, ln2_g, ln2_b):
    bsz, seq, _ = x.shape
    tq = min(KEY_BLOCK, seq)
    tm = min(512, seq)
    tm_dense = min(1024, seq)

    mod = _ada_mod(c, w_ada, b_ada).reshape(6, bsz, 1, D_MODEL)
    shift1, scale1, gate1, shift2, scale2, gate2 = (mod[k] for k in range(6))

    w_nat, w_tr, w_mg = _split_w_in(w_in)
    u_ssm, kv_cmp, k_aug, qT, v_aug, bgT = _in_proj(x, scale1, shift1, w_nat, w_tr, tm=tm)

    mats = _ssm_matrices(ssm_a_re, ssm_a_im, ssm_log_dt, ssm_b_re, ssm_b_im, ssm_c_re, ssm_c_im, ssm_d)
    g_ssm = _ssm(u_ssm, mats)

    kc3, vcT = _compress(kv_cmp,
                        cmp_pe_k.reshape(1, CMP_BLOCK * HEAD_DIM), cmp_pe_v.reshape(1, CMP_BLOCK * HEAD_DIM),
                        w_cmp_k1.reshape(CMP_BLOCK * HEAD_DIM, -1), w_cmp_k2,
                        w_cmp_v1.reshape(CMP_BLOCK * HEAD_DIM, -1), w_cmp_v2)
    o_nsa = _nsa(qT, kc3, vcT, k_aug, v_aug, bgT, tq=tq)

    x1 = _merge(x, g_ssm, o_nsa, scale1, shift1, gate1, w_mg,
                w_glu_lin.astype(BF16), w_glu_gate.astype(BF16), w_nsa_proj.astype(BF16), w_out.astype(BF16),
                ln1_g.reshape(1, D_MODEL), ln1_b.reshape(1, D_MODEL), tm=tm_dense)
    return _ffn(x1, scale2, shift2, gate2, w_ffn_gate.astype(BF16), w_ffn_up.astype(BF16),
                w_ffn_down.astype(BF16), ln2_g.reshape(1, D_MODEL), ln2_b.reshape(1, D_MODEL), tm=tm_dense, n_chunks=2)


def kernel(x, c, w_ada, b_ada, w_in, ssm_a_re, ssm_a_im, ssm_log_dt, ssm_b_re, ssm_b_im, ssm_c_re, ssm_c_im,
           ssm_d, w_glu_lin, w_glu_gate, cmp_pe_k, cmp_pe_v, w_cmp_k1, w_cmp_k2, w_cmp_v1, w_cmp_v2,
           w_nsa_proj, w_out, ln1_g, ln1_b, w_ffn_gate, w_ffn_up, w_ffn_down, ln2_g, ln2_b):
    for l in range(w_ada.shape[0]):
        x = _layer(x, c, w_ada[l], b_ada[l], w_in[l], ssm_a_re[l], ssm_a_im[l], ssm_log_dt[l],
                   ssm_b_re[l], ssm_b_im[l], ssm_c_re[l], ssm_c_im[l], ssm_d[l],
                   w_glu_lin[l], w_glu_gate[l], cmp_pe_k[l], cmp_pe_v[l],
                   w_cmp_k1[l], w_cmp_k2[l], w_cmp_v1[l], w_cmp_v2[l], w_nsa_proj[l], w_out[l],
                   ln1_g[l], ln1_b[l], w_ffn_gate[l], w_ffn_up[l], w_ffn_down[l], ln2_g[l], ln2_b[l])
    return x
```

```python
import functools
import math

import jax
import jax.numpy as jnp
from jax import lax
from jax.experimental import pallas as pl
from jax.experimental.pallas import tpu as pltpu

F32 = jnp.float32
BF16 = jnp.bfloat16
HIGHEST = lax.Precision.HIGHEST

D_MODEL = 1024
SSM_WIDTH = D_MODEL // 2
SSM_GROUP_SIZE = 16
SSM_GROUPS = SSM_WIDTH // SSM_GROUP_SIZE
SSM_STATE = 64
SSM_CHUNK = 16
SSM_GROUP_TILE = 8
SSM_PREP_GROUPS = 4
SSM_BATCH_TILE = 4
SSM_STATE_ROW_PAD = 8
N_HEADS = 8
HEAD_DIM = 64
N_KV_GROUPS = 2
HEADS_PER_GROUP = N_HEADS // N_KV_GROUPS
ATTN_WIDTH = N_HEADS * HEAD_DIM
KV_WIDTH = N_KV_GROUPS * HEAD_DIM
CMP_BLOCK = 32
CMP_STRIDE = 16
CMP_HIDDEN = HEAD_DIM
SEL_BLOCK = 64
SEL_TOP_K = 8
WINDOW = 256
N_NSA_BRANCHES = 3
GATE_ROWS = 16
FFN_HIDDEN = (8 * D_MODEL + 3 * 256 - 1) // (3 * 256) * 256
DEEPNORM_ALPHA = 2.0 ** 0.25
LN_EPS = 1e-5
MASK_VALUE = -1e30

VMEM_LIMIT = 56 * 1024 * 1024
MXU_WIDTH = 256
BF16_ROWS = 16
ROW_BLOCK = 256

KEY_BLOCK = 256
SEL_SHIFT = SEL_BLOCK.bit_length() - 1
SEL_PER_KEY_BLOCK = KEY_BLOCK // SEL_BLOCK
KEY_AUG = 128
ALIBI_COL = HEAD_DIM
N_PIECES = 4
MASK_COL = HEAD_DIM + 16
QUERY_AUG = HEAD_DIM + 32
VAL_AUG = HEAD_DIM + 16
LOG2E = math.log2(math.e)
SCORES_AHEAD = 5

NAT_COLS = SSM_WIDTH + 4 * KV_WIDTH
TR_ROWS = ATTN_WIDTH + 2 * KV_WIDTH + N_KV_GROUPS * GATE_ROWS


def _bf16_pieces(value, n):
    pieces = []
    rest = value
    for _ in range(n):
        mant, expo = math.frexp(rest)
        piece = math.ldexp(round(mant * 256.0) / 256.0, expo)
        pieces.append(piece)
        rest -= piece
    return pieces


def _sigmoid(x):
    return 1.0 / (1.0 + jnp.exp(-x))


def _gelu(x):
    c = math.sqrt(2.0 / math.pi)
    return 0.5 * x * (1.0 + jnp.tanh(c * (x + 0.044715 * (x * x * x))))


def _layer_norm(y, gain, bias):
    mu = jnp.mean(y, axis=-1, keepdims=True)
    d = y - mu
    var = jnp.mean(d * d, axis=-1, keepdims=True)
    return d * lax.rsqrt(var + LN_EPS) * gain + bias


def _params(n_axes):
    return pltpu.CompilerParams(dimension_semantics=("arbitrary",) * n_axes, vmem_limit_bytes=VMEM_LIMIT)


def _ada_kernel(c_ref, w_ref, b_ref, o_ref):
    c = c_ref[...]
    a = c * _sigmoid(c)
    w = w_ref[...].astype(BF16)
    a_hi = a.astype(BF16)
    a_lo = (a - a_hi.astype(F32)).astype(BF16)
    mod = (jnp.dot(a_hi, w, preferred_element_type=F32) + jnp.dot(a_lo, w, preferred_element_type=F32)) + b_ref[...]
    for k in range(o_ref.shape[0]):
        o_ref[k] = mod[:, k * D_MODEL:(k + 1) * D_MODEL]


def _ada_mod(c, w_ada, b_ada):
    bsz = c.shape[0]
    per_step = 2
    return pl.pallas_call(
        _ada_kernel,
        grid=(6 // per_step,),
        in_specs=[
            pl.BlockSpec((bsz, D_MODEL), lambda j: (0, 0)),
            pl.BlockSpec((D_MODEL, per_step * D_MODEL), lambda j: (0, j)),
            pl.BlockSpec((1, per_step * D_MODEL), lambda j: (0, j)),
        ],
        out_specs=pl.BlockSpec((per_step, bsz, D_MODEL), lambda j: (j, 0, 0)),
        out_shape=jax.ShapeDtypeStruct((6, bsz, D_MODEL), F32),
        compiler_params=_params(1),
        name="ada_mod",
    )(c, w_ada, b_ada.reshape(1, 6 * D_MODEL))


def _in_proj_kernel(x_ref, sc_ref, sh_ref, wn_ref, wt_ref, ussm_ref, kc_ref, kp_ref, qT_ref, vp_ref, bgT_ref):
    i = pl.program_id(1)
    tm = x_ref.shape[1]
    u = (x_ref[0] * (1.0 + sc_ref[0]) + sh_ref[0]).astype(BF16)
    nat = jnp.dot(u, wn_ref[...], preferred_element_type=F32)
    ussm_ref[0] = nat[:, :SSM_WIDTH]
    for s in range(2):
        for g in range(N_KV_GROUPS):
            lo = SSM_WIDTH + s * KV_WIDTH + g * HEAD_DIM
            kc_ref[0, s, g] = nat[:, lo:lo + HEAD_DIM]
    pos = i * tm + lax.broadcasted_iota(jnp.int32, (tm, KEY_AUG), 0)
    col = lax.broadcasted_iota(jnp.int32, (tm, KEY_AUG), 1)
    blk = lax.shift_right_logical(pos, SEL_SHIFT)
    in_a = (col >= ALIBI_COL) & (col < ALIBI_COL + N_PIECES)
    in_b = (col >= ALIBI_COL + N_PIECES) & (col < ALIBI_COL + 2 * N_PIECES)
    hot = col == MASK_COL + (blk & (SEL_PER_KEY_BLOCK - 1))
    aux = jnp.where(in_a, blk * SEL_BLOCK, jnp.where(in_b, pos & (SEL_BLOCK - 1), jnp.where(hot, 1, 0))).astype(F32)
    head_lanes = col < HEAD_DIM
    for s in range(2):
        lo = SSM_WIDTH + (2 + s) * KV_WIDTH
        both = nat[:, lo:lo + KV_WIDTH]
        for g in range(N_KV_GROUPS):
            keys = both if g == 0 else pltpu.roll(both, HEAD_DIM, 1)
            kp_ref[0, s, g] = jnp.where(head_lanes, keys, aux).astype(BF16)
    tr = lax.dot_general(wt_ref[...], u, (((1,), (1,)), ((), ())), preferred_element_type=F32)
    qT_ref[0] = tr[:ATTN_WIDTH] * (HEAD_DIM ** -0.5)
    ones_rows = jnp.where(lax.broadcasted_iota(jnp.int32, (VAL_AUG - HEAD_DIM, tm), 0) == 0, 1.0, 0.0)
    for s in range(2):
        for g in range(N_KV_GROUPS):
            lo = ATTN_WIDTH + (s * N_KV_GROUPS + g) * HEAD_DIM
            v_aug = jnp.concatenate([tr[lo:lo + HEAD_DIM], ones_rows], axis=0).astype(BF16)
            for c in range(tm // KEY_BLOCK):
                vp_ref[0, s, g, c] = v_aug[:, c * KEY_BLOCK:(c + 1) * KEY_BLOCK]
    bgT_ref[0] = _sigmoid(tr[ATTN_WIDTH + 2 * KV_WIDTH:])


def _in_proj(x, scale1, shift1, w_nat, w_tr, *, tm):
    bsz, seq, _ = x.shape
    n_t = seq // tm
    n_kb = seq // KEY_BLOCK
    out_shapes = (
        jax.ShapeDtypeStruct((bsz, seq, SSM_WIDTH), F32),
        jax.ShapeDtypeStruct((bsz, 2, N_KV_GROUPS, seq, HEAD_DIM), F32),
        jax.ShapeDtypeStruct((bsz, 2, N_KV_GROUPS, seq, KEY_AUG), BF16),
        jax.ShapeDtypeStruct((bsz, ATTN_WIDTH, seq), F32),
        jax.ShapeDtypeStruct((bsz, 2, N_KV_GROUPS, n_kb, VAL_AUG, KEY_BLOCK), BF16),
        jax.ShapeDtypeStruct((bsz, N_KV_GROUPS * GATE_ROWS, seq), F32),
    )
    return pl.pallas_call(
        _in_proj_kernel,
        grid=(bsz, n_t),
        in_specs=[
            pl.BlockSpec((1, tm, D_MODEL), lambda b, i: (b, i, 0)),
            pl.BlockSpec((1, 1, D_MODEL), lambda b, i: (b, 0, 0)),
            pl.BlockSpec((1, 1, D_MODEL), lambda b, i: (b, 0, 0)),
            pl.BlockSpec((D_MODEL, NAT_COLS), lambda b, i: (0, 0)),
            pl.BlockSpec((TR_ROWS, D_MODEL), lambda b, i: (0, 0)),
        ],
        out_specs=(
            pl.BlockSpec((1, tm, SSM_WIDTH), lambda b, i: (b, i, 0)),
            pl.BlockSpec((1, 2, N_KV_GROUPS, tm, HEAD_DIM), lambda b, i: (b, 0, 0, i, 0)),
            pl.BlockSpec((1, 2, N_KV_GROUPS, tm, KEY_AUG), lambda b, i: (b, 0, 0, i, 0)),
            pl.BlockSpec((1, ATTN_WIDTH, tm), lambda b, i: (b, 0, i)),
            pl.BlockSpec((1, 2, N_KV_GROUPS, tm // KEY_BLOCK, VAL_AUG, KEY_BLOCK), lambda b, i: (b, 0, 0, i, 0, 0)),
            pl.BlockSpec((1, N_KV_GROUPS * GATE_ROWS, tm), lambda b, i: (b, 0, i)),
        ),
        out_shape=out_shapes,
        compiler_params=_params(2),
        name="in_proj",
    )(x, scale1, shift1, w_nat, w_tr)


def _split_w_in(w_in):
    o_q = SSM_WIDTH
    o_kv = o_q + ATTN_WIDTH
    o_bg = o_kv + 6 * KV_WIDTH
    o_mg = o_bg + N_NSA_BRANCHES * N_HEADS
    kv = [w_in[:, o_kv + s * KV_WIDTH:o_kv + (s + 1) * KV_WIDTH] for s in range(6)]
    w_nat = jnp.concatenate([w_in[:, :o_q], kv[0], kv[1], kv[2], kv[4]], axis=1)
    per_group = HEADS_PER_GROUP * N_NSA_BRANCHES
    bg = w_in[:, o_bg:o_mg].reshape(D_MODEL, N_KV_GROUPS, per_group)
    bg = jnp.pad(bg, ((0, 0), (0, 0), (0, GATE_ROWS - per_group))).reshape(D_MODEL, N_KV_GROUPS * GATE_ROWS)
    w_tr = jnp.concatenate([w_in[:, o_q:o_kv], kv[3], kv[5], bg], axis=1).T
    return w_nat.astype(BF16), w_tr.astype(BF16), w_in[:, o_mg:].astype(BF16)


def _ssm_prep_kernel(*refs):
    for k in range(refs[0].shape[0]):
        _ssm_prep_group(*[ref.at[pl.ds(k, 1)] for ref in refs])


def _ssm_prep_group(a_ref, logdt_ref, bT_ref, c_ref, d_ref, mT_ref, pT_ref, qT_ref, a16_ref):
    t_n = SSM_CHUNK
    c_n = SSM_GROUP_SIZE
    n_p = SSM_STATE
    rows = t_n * c_n
    dt = jnp.exp(logdt_ref[0])
    a_re = a_ref[0, 0]
    a_im = a_ref[0, 1]
    decay = jnp.exp(a_re * dt)
    ab_re = decay * jnp.cos(a_im * dt)
    ab_im = decay * jnp.sin(a_im * dt)
    n_re = ab_re - 1.0
    denom = a_re * a_re + a_im * a_im
    f_re = (n_re * a_re + ab_im * a_im) / denom
    f_im = (ab_im * a_re - n_re * a_im) / denom
    powers = [(jnp.ones_like(ab_re), jnp.zeros_like(ab_re))]
    for _ in range(t_n):
        p_re, p_im = powers[-1]
        powers.append((p_re * ab_re - p_im * ab_im, p_re * ab_im + p_im * ab_re))

    def per_token(first, step):
        picks = [powers[first + step * k] for k in range(t_n)]
        return (jnp.concatenate([jnp.broadcast_to(p[0], (c_n, 2 * n_p)) for p in picks], axis=0),
                jnp.concatenate([jnp.broadcast_to(p[1], (c_n, 2 * n_p)) for p in picks], axis=0))

    def cmul(x_re, x_im, y_re, y_im):
        return x_re * y_re - x_im * y_im, x_re * y_im + x_im * y_re

    bb_re, bb_im = cmul(f_re, f_im, bT_ref[0, 0], bT_ref[0, 1])
    bt_re = jnp.concatenate([bb_re] * t_n, axis=0)
    bt_im = jnp.concatenate([bb_im] * t_n, axis=0)
    ct_re = jnp.concatenate([c_ref[0, 0]] * t_n, axis=0)
    ct_im = jnp.concatenate([c_ref[0, 1]] * t_n, axis=0)

    p_re, p_im = cmul(*per_token(t_n - 1, -1), bt_re, bt_im)
    low_half = lax.broadcasted_iota(jnp.int32, (1, 2 * n_p), 1) < n_p
    pT_ref[0] = jnp.where(low_half, p_re, p_im).T.astype(BF16)

    q_re, q_im = cmul(ct_re, ct_im, *per_token(1, 1))
    qT_ref[0] = jnp.where(low_half, q_re, -q_im).astype(BF16)
    a16_ref[0] = jnp.concatenate([powers[t_n][0][:, :n_p], powers[t_n][1][:, :n_p]], axis=0)

    k_re, k_im = cmul(ct_re, ct_im, *per_token(0, 1))
    nt = (((1,), (1,)), ((), ()))
    strips = (lax.dot_general(_split3(k_re[:, :n_p]), _stack3(bt_re[:, :n_p], axis=1), nt, preferred_element_type=F32)
              - lax.dot_general(_split3(k_im[:, :n_p]), _stack3(bt_im[:, :n_p], axis=1), nt,
                                preferred_element_type=F32))
    lane = lax.broadcasted_iota(jnp.int32, (1, rows), 1)
    s_lane = lax.shift_right_logical(lane, c_n.bit_length() - 1)
    ci_lane = lane & (c_n - 1)
    row = lax.broadcasted_iota(jnp.int32, (rows, 1), 0)
    t_row = lax.shift_right_logical(row, c_n.bit_length() - 1)
    co_row = row & (c_n - 1)
    blocks = []
    for t in range(t_n):
        acc = jnp.zeros((c_n, rows), F32)
        for lag in range(t + 1):
            acc = jnp.where(s_lane == t - lag, strips[lag * c_n:(lag + 1) * c_n, :], acc)
        blocks.append(acc)
    d_rows = jnp.concatenate([d_ref[0]] * t_n, axis=0)
    skip = jnp.where((t_row == s_lane) & (co_row == ci_lane), d_rows, 0.0)
    mT_ref[0] = (jnp.concatenate(blocks, axis=0) + skip).astype(BF16)


def _ssm_matrices(a_re, a_im, log_dt, b_re, b_im, c_re, c_im, d_skip):
    n_g, n_p = a_re.shape
    c_n = SSM_GROUP_SIZE
    rows = SSM_CHUNK * c_n
    twice = lambda x: jnp.concatenate([x, x], axis=-1).astype(F32)
    a = twice(jnp.stack([a_re, a_im], axis=1)).reshape(n_g, 2, 1, 2 * n_p)
    bT = twice(jnp.swapaxes(jnp.stack([b_re, b_im], axis=1), 2, 3))
    c = twice(jnp.stack([c_re, c_im], axis=1))
    per_step = SSM_PREP_GROUPS
    spec = lambda shape: pl.BlockSpec((per_step,) + shape, lambda g: (g,) + (0,) * len(shape))
    mT, pT, qT, a16 = pl.pallas_call(
        _ssm_prep_kernel,
        grid=(n_g // per_step,),
        in_specs=[spec((2, 1, 2 * n_p)), spec((1, 1)), spec((2, c_n, 2 * n_p)), spec((2, c_n, 2 * n_p)),
                  spec((c_n, 1))],
        out_specs=(spec((rows, rows)), spec((2 * n_p, rows)), spec((rows, 2 * n_p)), spec((2, n_p))),
        out_shape=(
            jax.ShapeDtypeStruct((n_g, rows, rows), BF16),
            jax.ShapeDtypeStruct((n_g, 2 * n_p, rows), BF16),
            jax.ShapeDtypeStruct((n_g, rows, 2 * n_p), BF16),
            jax.ShapeDtypeStruct((n_g, 2, n_p), F32),
        ),
        compiler_params=_params(1),
        name="ssm_prep",
    )(a, log_dt.astype(F32).reshape(n_g, 1, 1), bT, c, d_skip.astype(F32).reshape(n_g, c_n, 1))
    return mT, pT.reshape(n_g, 2, n_p, rows), qT, a16


def _ssm_kernel(u_ref, mT_ref, pT_ref, qT_ref, a16_ref, o_ref, ut_scr, yt_scr, sre, sim, *, n_chunks):
    n_g = SSM_GROUP_TILE
    n_b = u_ref.shape[0]
    t_n = SSM_CHUNK
    c_n = SSM_GROUP_SIZE
    lanes = [(e, g) for e in range(n_b) for g in range(n_g)]

    for e in range(n_b):
        for s in range(t_n):
            x_s = u_ref[e, pl.ds(s, n_chunks, stride=t_n), :]
            ut_scr[e * n_g:(e + 1) * n_g, s * c_n:(s + 1) * c_n, :] = (
                x_s.T.reshape(n_g, c_n, n_chunks).astype(BF16))

    pitch = n_chunks + SSM_STATE_ROW_PAD

    def lane_rows(k):
        return pl.ds(k * pitch, n_chunks)

    for k, (e, g) in enumerate(lanes):
        ut = ut_scr[k]
        sre[lane_rows(k), :] = jnp.dot(pT_ref[g, 0], ut, preferred_element_type=F32).T
        sim[lane_rows(k), :] = jnp.dot(pT_ref[g, 1], ut, preferred_element_type=F32).T

    a_r = jnp.concatenate([a16_ref[pl.ds(0, n_g, stride=2), :]] * n_b, axis=0)
    a_i = jnp.concatenate([a16_ref[pl.ds(1, n_g, stride=2), :]] * n_b, axis=0)

    def carry_states(c, h):
        h_r, h_i = h
        rows = pl.ds(c, len(lanes), stride=pitch)
        s_r = sre[rows, :]
        s_i = sim[rows, :]
        sre[rows, :] = h_r
        sim[rows, :] = h_i
        return a_r * h_r - a_i * h_i + s_r, a_r * h_i + a_i * h_r + s_i

    zero = jnp.zeros((len(lanes), SSM_STATE), F32)
    lax.fori_loop(0, n_chunks, carry_states, (zero, zero), unroll=4)

    nt = (((1,), (1,)), ((), ()))
    for k, (e, g) in enumerate(lanes):
        yt = jnp.dot(mT_ref[g], ut_scr[k], preferred_element_type=F32)
        entering = jnp.concatenate([sre[lane_rows(k), :], sim[lane_rows(k), :]], axis=1).astype(BF16)
        yt = yt + lax.dot_general(qT_ref[g], entering, nt, preferred_element_type=F32)
        yt_scr[k] = _gelu(yt)

    for e in range(n_b):
        for t in range(t_n):
            z = yt_scr[e * n_g:(e + 1) * n_g, t * c_n:(t + 1) * c_n, :].reshape(n_g * c_n, n_chunks)
            o_ref[e, pl.ds(t, n_chunks, stride=t_n), :] = z.T


def _ssm(u_ssm, mats):
    bsz, seq, width = u_ssm.shape
    n_chunks = seq // SSM_CHUNK
    mT, pT, qT, a16 = mats
    gt = SSM_GROUP_TILE
    n_tiles = SSM_GROUPS // gt
    rows = SSM_CHUNK * SSM_GROUP_SIZE
    a16 = a16.reshape(2 * SSM_GROUPS, SSM_STATE)
    bt = SSM_BATCH_TILE if bsz % SSM_BATCH_TILE == 0 else 1
    n_lanes = bt * gt
    per_tile = lambda a: pl.BlockSpec((gt,) + a.shape[1:], lambda t, b: (t,) + (0,) * (a.ndim - 1))
    return pl.pallas_call(
        functools.partial(_ssm_kernel, n_chunks=n_chunks),
        grid=(n_tiles, bsz // bt),
        in_specs=[
            pl.BlockSpec((bt, seq, gt * SSM_GROUP_SIZE), lambda t, b: (b, 0, t)),
            per_tile(mT), per_tile(pT), per_tile(qT),
            pl.BlockSpec((2 * gt, SSM_STATE), lambda t, b: (t, 0)),
        ],
        out_specs=pl.BlockSpec((bt, seq, gt * SSM_GROUP_SIZE), lambda t, b: (b, 0, t)),
        out_shape=jax.ShapeDtypeStruct(u_ssm.shape, F32),
        scratch_shapes=[
            pltpu.VMEM((n_lanes, rows, n_chunks), BF16),
            pltpu.VMEM((n_lanes, rows, n_chunks), F32),
            pltpu.VMEM((n_lanes * (n_chunks + SSM_STATE_ROW_PAD), SSM_STATE), F32),
            pltpu.VMEM((n_lanes * (n_chunks + SSM_STATE_ROW_PAD), SSM_STATE), F32),
        ],
        compiler_params=_params(2),
        name="ssm",
    )(u_ssm, mT, pT, qT, a16)


def _split3(x):
    hi = x.astype(BF16)
    lo = (x - hi.astype(F32)).astype(BF16)
    return jnp.concatenate([hi, lo, hi], axis=-1)


def _stack3(w, axis=0):
    hi = w.astype(BF16)
    lo = (w - hi.astype(F32)).astype(BF16)
    return jnp.concatenate([hi, hi, lo], axis=axis)


def _compress_kernel(kx_ref, vx_ref, pek_ref, pev_ref, wk1_ref, wk2_ref, wv1_ref, wv2_ref, kc_ref, vcT_ref):
    def mlp(x_ref, pe_ref, w1_ref, w2_ref):
        n = x_ref.shape[3] // CMP_STRIDE
        x = jnp.concatenate([x_ref[0, 0, 0, pl.ds(l, n, stride=CMP_STRIDE), :] for l in range(CMP_STRIDE)], axis=1)
        both = jnp.dot(_split3(x), w1_ref[...], preferred_element_type=F32)
        pe_both = jnp.dot(_split3(pe_ref[...]), w1_ref[...], preferred_element_type=F32)
        bias = pe_both[0:1, :CMP_HIDDEN] + pe_both[1:2, CMP_HIDDEN:]
        h = _gelu(both[:, :CMP_HIDDEN] + pltpu.roll(both[:, CMP_HIDDEN:], n - 1, 0) + bias)
        return jnp.dot(_split3(h), w2_ref[...], preferred_element_type=F32)

    kc = mlp(kx_ref, pek_ref, wk1_ref, wk2_ref)
    kc_hi = kc.astype(BF16)
    kc_lo = (kc - kc_hi.astype(F32)).astype(BF16)
    kc_ref[0, 0] = jnp.concatenate([kc_hi, kc_hi, kc_lo, jnp.zeros_like(kc_hi)], axis=1)
    vcT_ref[0, 0] = mlp(vx_ref, pev_ref, wv1_ref, wv2_ref).T.astype(BF16)


def _compress(kv, pe_k, pe_v, wk1, wk2, wv1, wv2):
    bsz, _, n_g, seq, width = kv.shape
    n_chunks = seq // CMP_STRIDE
    half = CMP_STRIDE * HEAD_DIM
    side_by_side = lambda w1: _stack3(jnp.concatenate([w1[:half], w1[half:]], axis=1))
    two_rows = lambda pe: jnp.pad(pe.reshape(2, half), ((0, BF16_ROWS - 2), (0, 0)))
    pe_k, pe_v = two_rows(pe_k), two_rows(pe_v)
    wk1, wv1, wk2, wv2 = side_by_side(wk1), side_by_side(wv1), _stack3(wk2), _stack3(wv2)
    blk = lambda s: pl.BlockSpec((1, 1, 1, seq, width), lambda b, g: (b, s, g, 0, 0))
    full = lambda a: pl.BlockSpec(a.shape, lambda b, g: (0, 0))
    return pl.pallas_call(
        _compress_kernel,
        grid=(bsz, n_g),
        in_specs=[blk(0), blk(1), full(pe_k), full(pe_v), full(wk1), full(wk2), full(wv1), full(wv2)],
        out_specs=(
            pl.BlockSpec((1, 1, n_chunks, 4 * HEAD_DIM), lambda b, g: (b, g, 0, 0)),
            pl.BlockSpec((1, 1, HEAD_DIM, n_chunks), lambda b, g: (b, g, 0, 0)),
        ),
        out_shape=(
            jax.ShapeDtypeStruct((bsz, n_g, n_chunks, 4 * HEAD_DIM), BF16),
            jax.ShapeDtypeStruct((bsz, n_g, HEAD_DIM, n_chunks), BF16),
        ),
        compiler_params=_params(2),
        name="compress",
    )(kv, kv, pe_k, pe_v, wk1, wk2, wv1, wv2)


def _nsa_kernel(qT_ref, kc_ref, vcT_ref, ks_ref, vs_ref, kw_ref, vw_ref, bg_ref, o_ref,
                negm_scr, qa_scr, m_scr, acc_scr, out_scr, used_smem, todo_smem, *, tq, n_cmp, n_sel):
    g = pl.program_id(1)
    i = pl.program_id(2)
    t_row = i * tq + lax.broadcasted_iota(jnp.int32, (1, tq), 1)
    slopes = [jnp.where(g == 0, 2.0 ** -(hh + 1), 2.0 ** -(HEADS_PER_GROUP + hh + 1)).astype(F32)
              for hh in range(HEADS_PER_GROUP)]

    def gate(hh, branch):
        r = hh * N_NSA_BRANCHES + branch
        return bg_ref[0, 0, r:r + 1, :]

    def head_rows(hh):
        return slice(hh * HEAD_DIM, (hh + 1) * HEAD_DIM)

    def head_lanes(hh):
        return slice(hh * tq, (hh + 1) * tq)

    kc3 = kc_ref[0, 0]
    vcT = vcT_ref[0, 0]
    cmp_end = lax.broadcasted_iota(jnp.int32, (n_cmp, 1), 0) * CMP_STRIDE + (CMP_BLOCK - 1)
    dist_c = (t_row - cmp_end).astype(F32)
    valid_c = dist_c >= 0.0
    piece = lax.broadcasted_iota(jnp.int32, (16, 1), 0)
    log2e_pieces = _bf16_pieces(LOG2E, N_PIECES)
    log2e_col = jnp.zeros((16, 1), F32)
    for k, value in enumerate(log2e_pieces):
        log2e_col = jnp.where((piece == k) | (piece == N_PIECES + k), value, log2e_col)
    for hh in range(HEADS_PER_GROUP):
        alibi = jnp.broadcast_to(log2e_col * slopes[hh], (16, tq)).astype(BF16)
        q_l2 = (qT_ref[0, head_rows(hh), :] * LOG2E).astype(BF16)
        qa_scr[:, head_lanes(hh)] = jnp.concatenate([q_l2, alibi], axis=0)

    def cmp_scores(hh):
        q_h = qT_ref[0, head_rows(hh), :]
        q_hi = q_h.astype(BF16)
        q_lo = (q_h - q_hi.astype(F32)).astype(BF16)
        q3 = jnp.concatenate([q_hi, q_lo, q_hi, jnp.zeros_like(q_hi)], axis=0)
        return jnp.dot(kc3, q3, preferred_element_type=F32)

    def cmp_attend(p_heads, hh, s):
        s = jnp.where(valid_c, s - slopes[hh] * dist_c, MASK_VALUE)
        m = jnp.max(s, axis=0, keepdims=True)
        p = jnp.where(valid_c, jnp.exp(s - m), 0.0)
        l = jnp.sum(p, axis=0, keepdims=True)
        p = p / jnp.where(l > 0.0, l, 1.0)
        o_c = jnp.dot(vcT, p.astype(BF16), preferred_element_type=F32)
        out_scr[head_rows(hh), :] = gate(hh, 0) * o_c
        p_heads.append(p)

    def select_blocks(p_heads, _):
        p_sum = sum(p_heads[1:], p_heads[0])
        blk = lax.broadcasted_iota(jnp.int32, (n_sel, n_cmp), 0)
        cmp_ix = lax.broadcasted_iota(jnp.int32, (n_sel, n_cmp), 1)
        ratio = SEL_BLOCK // CMP_STRIDE
        extra = CMP_BLOCK // CMP_STRIDE - 1
        overlap_t = jnp.where((cmp_ix >= ratio * blk - extra) & (cmp_ix <= ratio * blk + ratio - 1),
                              1.0, 0.0).astype(BF16)
        imp = jnp.zeros((n_sel, tq), F32)
        rest = p_sum
        for _ in range(3):
            part = rest.astype(BF16)
            imp = imp + jnp.dot(overlap_t, part, preferred_element_type=F32)
            rest = rest - part.astype(F32)

        j_col = lax.broadcasted_iota(jnp.int32, (n_sel, 1), 0)
        forced = (j_col == 0) | (j_col == lax.shift_right_logical(t_row, SEL_SHIFT))
        future = j_col * SEL_BLOCK > t_row
        score = jnp.where(forced, jnp.inf, jnp.where(future, -jnp.inf, imp))
        rank = jnp.zeros((n_sel, tq), F32)
        for jp in range(n_sel):
            other = score[jp:jp + 1, :]
            ahead = (other > score) | ((other == score) & (j_col > jp))
            rank = rank + jnp.where(ahead, 1.0, 0.0)
        neg = jnp.where(rank < float(min(SEL_TOP_K, n_sel)), 0.0, MASK_VALUE)
        pad_rows = jnp.zeros((BF16_ROWS - SEL_PER_KEY_BLOCK, tq), F32)
        for c in range(n_sel // SEL_PER_KEY_BLOCK):
            rows = neg[c * SEL_PER_KEY_BLOCK:(c + 1) * SEL_PER_KEY_BLOCK, :]
            negm_scr[c] = jnp.concatenate([rows, pad_rows], axis=0).astype(BF16)
            used_smem[c] = (jnp.max(rows) > 0.5 * MASK_VALUE).astype(jnp.int32)

    rel = (lax.broadcasted_iota(jnp.int32, (KEY_BLOCK, tq), 1)
           - lax.broadcasted_iota(jnp.int32, (KEY_BLOCK, tq), 0))
    zero_tail = jnp.zeros((KEY_AUG - QUERY_AUG, tq), BF16)
    no_mask = jnp.zeros((BF16_ROWS, tq), BF16)
    all_masked = jnp.full((BF16_ROWS, tq), MASK_VALUE, BF16)

    sel_even, sel_odd, win_prev, win_diag = range(4)
    m_scr[...] = jnp.full(m_scr.shape, MASK_VALUE, F32)
    acc_scr[...] = jnp.zeros(acc_scr.shape, F32)

    def scores(job, hh):
        k_ref, _, kb, enabled, _, _ = job
        k0 = pl.multiple_of(kb * KEY_BLOCK, KEY_BLOCK)
        k_aug = k_ref[0, 0, 0, pl.ds(k0, KEY_BLOCK), :]
        if enabled is None:
            mask_tile = no_mask
        elif enabled is True:
            mask_tile = negm_scr[kb]
        else:
            mask_tile = jnp.where(enabled, negm_scr[kb], all_masked)
        q_aug = jnp.concatenate([qa_scr[:, head_lanes(hh)], mask_tile, zero_tail], axis=0)
        return jnp.dot(k_aug, q_aug, preferred_element_type=F32)

    def accumulate(job, hh, s):
        _, v_ref, kb, _, valid, slot = job
        cols = head_lanes(hh)
        if valid is not None:
            s = jnp.where(valid, s, MASK_VALUE)
        m_old = m_scr[slot, :, cols]
        m_new = jnp.maximum(m_old, jnp.max(s, axis=0, keepdims=True))
        p = jnp.exp2(s - jnp.maximum(m_new, 0.1 * MASK_VALUE))
        alpha = jnp.exp2(m_old - m_new)
        pv = jnp.dot(v_ref[0, 0, 0, kb], p.astype(BF16), preferred_element_type=F32)
        acc_scr[slot, :, cols] = alpha * acc_scr[slot, :, cols] + pv
        m_scr[slot, :, cols] = m_new

    def pipeline(units):
        issue = lambda unit: None if unit[0] is None else unit[0]()
        pending = [issue(unit) for unit in units[:SCORES_AHEAD]]
        for n, unit in enumerate(units):
            if n + SCORES_AHEAD < len(units):
                pending.append(issue(units[n + SCORES_AHEAD]))
            unit[1](pending.pop(0))

    def attention_units(jobs):
        return [(functools.partial(scores, job, hh), functools.partial(accumulate, job, hh))
                for job in jobs for hh in range(HEADS_PER_GROUP)]

    def finish(branch, slot_a, slot_b):
        m_a = m_scr[slot_a]
        m_b = m_scr[slot_b]
        m = jnp.maximum(m_a, m_b)
        w_a = jnp.exp2(m_a - m)
        w_b = jnp.exp2(m_b - m)
        for hh in range(HEADS_PER_GROUP):
            cols = head_lanes(hh)
            acc = acc_scr[slot_a, :, cols] * w_a[:, cols] + acc_scr[slot_b, :, cols] * w_b[:, cols]
            l = acc[HEAD_DIM:HEAD_DIM + 1, :]
            o_b = acc[:HEAD_DIM, :] / jnp.where(l > 0.0, l, 1.0)
            out_scr[head_rows(hh), :] = out_scr[head_rows(hh), :] + gate(hh, branch) * o_b

    prev = jnp.maximum(i - 1, 0)
    causal = rel >= 0
    off_without_prev = jnp.where(i > 0, 0, WINDOW + KEY_BLOCK)
    dist_prev = rel + KEY_BLOCK + off_without_prev
    p_heads = []
    pipeline(
        [(functools.partial(cmp_scores, hh), functools.partial(cmp_attend, p_heads, hh))
         for hh in range(HEADS_PER_GROUP)]
        + [(None, functools.partial(select_blocks, p_heads))]
        + attention_units([(kw_ref, vw_ref, prev, None, dist_prev < WINDOW, win_prev),
                           (kw_ref, vw_ref, i, None, causal, win_diag),
                           (ks_ref, vs_ref, prev, i > 0, None, sel_even),
                           (ks_ref, vs_ref, i, True, causal, sel_odd)]))

    n_todo = jnp.int32(0)
    for c in range(used_smem.shape[0]):
        take = jnp.logical_and(c < i - 1, used_smem[c] > 0)

        @pl.when(take)
        def _(c=c, slot=n_todo):
            todo_smem[slot] = c

        n_todo = n_todo + take.astype(jnp.int32)
    todo_smem[n_todo] = 0

    def past_pair(j, carry):
        pipeline(attention_units([(ks_ref, vs_ref, todo_smem[2 * j], True, None, sel_even),
                                  (ks_ref, vs_ref, todo_smem[2 * j + 1], 2 * j + 1 < n_todo, None, sel_odd)]))
        return carry

    lax.fori_loop(0, lax.shift_right_logical(n_todo + 1, 1), past_pair, 0)
    finish(1, sel_even, sel_odd)
    finish(2, win_prev, win_diag)

    o_ref[0] = out_scr[...].T


def _nsa(qT, kc3, vcT, k_aug, v_aug, bgT, *, tq):
    bsz, _, seq = qT.shape
    assert tq == KEY_BLOCK == WINDOW and seq % tq == 0
    n_cmp = kc3.shape[2]
    n_sel = seq // SEL_BLOCK
    n_kb = seq // KEY_BLOCK
    gw = HEADS_PER_GROUP * HEAD_DIM
    bg4 = bgT.reshape(bsz, N_KV_GROUPS, GATE_ROWS, seq)
    k_spec = lambda s: pl.BlockSpec((1, 1, 1, seq, KEY_AUG), lambda b, g, i: (b, s, g, 0, 0))
    v_spec = lambda s: pl.BlockSpec((1, 1, 1, n_kb, VAL_AUG, KEY_BLOCK), lambda b, g, i: (b, s, g, 0, 0, 0))
    return pl.pallas_call(
        functools.partial(_nsa_kernel, tq=tq, n_cmp=n_cmp, n_sel=n_sel),
        grid=(bsz, N_KV_GROUPS, seq // tq),
        in_specs=[
            pl.BlockSpec((1, gw, tq), lambda b, g, i: (b, g, i)),
            pl.BlockSpec((1, 1, n_cmp, 4 * HEAD_DIM), lambda b, g, i: (b, g, 0, 0)),
            pl.BlockSpec((1, 1, HEAD_DIM, n_cmp), lambda b, g, i: (b, g, 0, 0)),
            k_spec(0), v_spec(0),
            k_spec(1), v_spec(1),
            pl.BlockSpec((1, 1, GATE_ROWS, tq), lambda b, g, i: (b, g, 0, i)),
        ],
        out_specs=pl.BlockSpec((1, tq, gw), lambda b, g, i: (b, i, g)),
        out_shape=jax.ShapeDtypeStruct((bsz, seq, ATTN_WIDTH), F32),
        scratch_shapes=[
            pltpu.VMEM((n_kb, 16, tq), BF16),
            pltpu.VMEM((HEAD_DIM + 16, HEADS_PER_GROUP * tq), BF16),
            pltpu.VMEM((4, 1, HEADS_PER_GROUP * tq), F32),
            pltpu.VMEM((4, VAL_AUG, HEADS_PER_GROUP * tq), F32),
            pltpu.VMEM((gw, tq), F32),
            pltpu.SMEM((n_kb,), jnp.int32),
            pltpu.SMEM((n_kb + 1,), jnp.int32),
        ],
        compiler_params=_params(3),
        name="nsa",
    )(qT, kc3, vcT, k_aug, v_aug, k_aug, v_aug, bg4)


def _merge_kernel(x_ref, g_ref, o_ref_in, sc_ref, sh_ref, gt_ref, wm_ref, wl_ref, wg_ref, wn_ref, wo_ref,
                  lg_ref, lb_ref, y_ref):
    tm = x_ref.shape[1]
    row_blocks = [slice(r, r + ROW_BLOCK) for r in range(0, tm, ROW_BLOCK)]
    halves = [(0, D_MODEL // 2), (D_MODEL // 2, D_MODEL)]
    dot = functools.partial(jnp.dot, preferred_element_type=F32)

    def first_stage(rows):
        g = g_ref[0, rows, :].astype(BF16)
        o = o_ref_in[0, rows, :].astype(BF16)
        branch = [(dot(g, wl_ref[:, lo:hi]), dot(g, wg_ref[:, lo:hi]), dot(o, wn_ref[:, lo:hi])) for lo, hi in halves]
        x = x_ref[0, rows, :]
        u = (x * (1.0 + sc_ref[0]) + sh_ref[0]).astype(BF16)
        gates = [(dot(u, wm_ref[:, lo:hi]), dot(u, wm_ref[:, D_MODEL + lo:D_MODEL + hi])) for lo, hi in halves]
        return x, branch, gates

    def second_stage(rows, x, branch, gates):
        mix = None
        for (lo, hi), (lin, gate, z_nsa), (m_ssm, m_nsa) in zip(halves, branch, gates):
            merged = _sigmoid(m_ssm) * (lin * _sigmoid(gate)) + _sigmoid(m_nsa) * z_nsa
            part = dot(merged.astype(BF16), wo_ref[lo:hi, :])
            mix = part if mix is None else mix + part
        y = DEEPNORM_ALPHA * x + (1.0 + gt_ref[0]) * mix
        y_ref[0, rows, :] = _layer_norm(y, lg_ref[...], lb_ref[...])

    staged = None
    for rows in row_blocks + [None]:
        upcoming = first_stage(rows) if rows is not None else None
        if staged is not None:
            second_stage(*staged)
        staged = (rows,) + upcoming if rows is not None else None


def _merge(x, g_ssm, o_nsa, scale1, shift1, gate1, w_m, w_lin, w_gate, w_nsa, w_out, ln_g, ln_b, *, tm):
    bsz, seq, _ = x.shape
    row = pl.BlockSpec((1, 1, D_MODEL), lambda b, i: (b, 0, 0))
    full = lambda a: pl.BlockSpec(a.shape, lambda b, i: (0, 0), pipeline_mode=pl.Buffered(1))
    return pl.pallas_call(
        _merge_kernel,
        grid=(bsz, seq // tm),
        in_specs=[
            pl.BlockSpec((1, tm, D_MODEL), lambda b, i: (b, i, 0)),
            pl.BlockSpec((1, tm, SSM_WIDTH), lambda b, i: (b, i, 0)),
            pl.BlockSpec((1, tm, ATTN_WIDTH), lambda b, i: (b, i, 0)),
            row, row, row,
            full(w_m), full(w_lin), full(w_gate), full(w_nsa), full(w_out), full(ln_g), full(ln_b),
        ],
        out_specs=pl.BlockSpec((1, tm, D_MODEL), lambda b, i: (b, i, 0)),
        out_shape=jax.ShapeDtypeStruct(x.shape, F32),
        compiler_params=_params(2),
        name="merge",
    )(x, g_ssm, o_nsa, scale1, shift1, gate1, w_m, w_lin, w_gate, w_nsa, w_out, ln_g, ln_b)


def _ffn_kernel(x_ref, sc_ref, sh_ref, gt_ref, wg_ref, wu_ref, wd_ref, lg_ref, lb_ref, y_ref, *, n_chunks):
    tm = x_ref.shape[1]
    row_blocks = [slice(r, r + ROW_BLOCK) for r in range(0, tm, ROW_BLOCK)]
    tiles = FFN_HIDDEN // MXU_WIDTH
    edges = [MXU_WIDTH * ((tiles * c + n_chunks - 1) // n_chunks) for c in range(n_chunks + 1)]
    chunks = [slice(lo, hi) for lo, hi in zip(edges[:-1], edges[1:])]
    dot = functools.partial(jnp.dot, preferred_element_type=F32)

    def gate_up(rows):
        x = x_ref[0, rows, :]
        u = (x * (1.0 + sc_ref[0]) + sh_ref[0]).astype(BF16)
        return x, [(dot(u, wg_ref[:, cols]), dot(u, wu_ref[:, cols])) for cols in chunks]

    def down(rows, x, products):
        ffn = None
        for cols, (a, b) in zip(chunks, products):
            part = dot(((a * _sigmoid(a)) * b).astype(BF16), wd_ref[cols, :])
            ffn = part if ffn is None else ffn + part
        y = DEEPNORM_ALPHA * x + (1.0 + gt_ref[0]) * ffn
        y_ref[0, rows, :] = _layer_norm(y, lg_ref[...], lb_ref[...])

    staged = None
    for rows in row_blocks + [None]:
        upcoming = gate_up(rows) if rows is not None else None
        if staged is not None:
            down(*staged)
        staged = (rows,) + upcoming if rows is not None else None


def _ffn(x, scale2, shift2, gate2, w_gate, w_up, w_down, ln_g, ln_b, *, tm, n_chunks):
    bsz, seq, _ = x.shape
    row = pl.BlockSpec((1, 1, D_MODEL), lambda b, i: (b, 0, 0))
    resident = lambda a: pl.BlockSpec(a.shape, lambda b, i: (0, 0), pipeline_mode=pl.Buffered(1))
    return pl.pallas_call(
        functools.partial(_ffn_kernel, n_chunks=n_chunks),
        grid=(bsz, seq // tm),
        in_specs=[
            pl.BlockSpec((1, tm, D_MODEL), lambda b, i: (b, i, 0)),
            row, row, row,
            resident(w_gate), resident(w_up), resident(w_down), resident(ln_g), resident(ln_b),
        ],
        out_specs=pl.BlockSpec((1, tm, D_MODEL), lambda b, i: (b, i, 0)),
        out_shape=jax.ShapeDtypeStruct(x.shape, F32),
        compiler_params=_params(2),
        name="ffn",
    )(x, scale2, shift2, gate2, w_gate, w_up, w_down, ln_g, ln_b)


def _layer(x, c, w_ada, b_ada, w_in, ssm_a_re, ssm_a_im, ssm_log_dt, ssm_b_re, ssm_b_im, ssm_c_re, ssm_c_im,
           ssm_d, w_glu_lin, w_glu_gate, cmp_pe_k, cmp_pe_v, w_cmp_k1, w_cmp_k2, w_cmp_v1, w_cmp_v2,
           w_nsa_proj, w_out, ln1_g, ln1_b, w_ffn_gate, w_ffn_up, w_ffn_down, ln2_g, ln2_b):
    bsz, seq, _ = x.shape
    tq = min(KEY_BLOCK, seq)
    tm = min(512, seq)
    tm_dense = min(1024, seq)

    mod = _ada_mod(c, w_ada, b_ada).reshape(6, bsz, 1, D_MODEL)
    shift1, scale1, gate1, shift2, scale2, gate2 = (mod[k] for k in range(6))

    w_nat, w_tr, w_mg = _split_w_in(w_in)
    u_ssm, kv_cmp, k_aug, qT, v_aug, bgT = _in_proj(x, scale1, shift1, w_nat, w_tr, tm=tm)

    mats = _ssm_matrices(ssm_a_re, ssm_a_im, ssm_log_dt, ssm_b_re, ssm_b_im, ssm_c_re, ssm_c_im, ssm_d)
    g_ssm = _ssm(u_ssm, mats)

    kc3, vcT = _compress(kv_cmp,
                        cmp_pe_k.reshape(1, CMP_BLOCK * HEAD_DIM), cmp_pe_v.reshape(1, CMP_BLOCK * HEAD_DIM),
                        w_cmp_k1.reshape(CMP_BLOCK * HEAD_DIM, -1), w_cmp_k2,
                        w_cmp_v1.reshape(CMP_BLOCK * HEAD_DIM, -1), w_cmp_v2)
    o_nsa = _nsa(qT, kc3, vcT, k_aug, v_aug, bgT, tq=tq)

    x1 = _merge(x, g_ssm, o_nsa, scale1, shift1, gate1, w_mg,
                w_glu_lin.astype(BF16), w_glu_gate.astype(BF16), w_nsa_proj.astype(BF16), w_out.astype(BF16),
                ln1_g.reshape(1, D_MODEL), ln1_b.reshape(1, D_MODEL), tm=tm_dense)
    return _ffn(x1, scale2, shift2, gate2, w_ffn_gate.astype(BF16), w_ffn_up.astype(BF16),
                w_ffn_down.astype(BF16), ln2_g.reshape(1, D_MODEL), ln2_b.reshape(1, D_MODEL), tm=tm_dense, n_chunks=2)


def kernel(x, c, w_ada, b_ada, w_in, ssm_a_re, ssm_a_im, ssm_log_dt, ssm_b_re, ssm_b_im, ssm_c_re, ssm_c_im,
           ssm_d, w_glu_lin, w_glu_gate, cmp_pe_k, cmp_pe_v, w_cmp_k1, w_cmp_k2, w_cmp_v1, w_cmp_v2,
           w_nsa_proj, w_out, ln1_g, ln1_b, w_ffn_gate, w_ffn_up, w_ffn_down, ln2_g, ln2_b):
    for l in range(w_ada.shape[0]):
        x = _layer(x, c, w_ada[l], b_ada[l], w_in[l], ssm_a_re[l], ssm_a_im[l], ssm_log_dt[l],
                   ssm_b_re[l], ssm_b_im[l], ssm_c_re[l], ssm_c_im[l], ssm_d[l],
                   w_glu_lin[l], w_glu_gate[l], cmp_pe_k[l], cmp_pe_v[l],
                   w_cmp_k1[l], w_cmp_k2[l], w_cmp_v1[l], w_cmp_v2[l], w_nsa_proj[l], w_out[l],
                   ln1_g[l], ln1_b[l], w_ffn_gate[l], w_ffn_up[l], w_ffn_down[l], ln2_g[l], ln2_b[l])
    return x
```

```python
import functools
import math

import jax
import jax.numpy as jnp
from jax import lax
from jax.experimental import pallas as pl
from jax.experimental.pallas import tpu as pltpu

F32 = jnp.float32
BF16 = jnp.bfloat16
HIGHEST = lax.Precision.HIGHEST

D_MODEL = 1024
SSM_WIDTH = D_MODEL // 2
SSM_GROUP_SIZE = 16
SSM_GROUPS = SSM_WIDTH // SSM_GROUP_SIZE
SSM_STATE = 64
SSM_CHUNK = 16
SSM_GROUP_TILE = 8
SSM_PREP_GROUPS = 4
SSM_BATCH_TILE = 4
SSM_STATE_ROW_PAD = 8
N_HEADS = 8
HEAD_DIM = 64
N_KV_GROUPS = 2
HEADS_PER_GROUP = N_HEADS // N_KV_GROUPS
ATTN_WIDTH = N_HEADS * HEAD_DIM
KV_WIDTH = N_KV_GROUPS * HEAD_DIM
CMP_BLOCK = 32
CMP_STRIDE = 16
CMP_HIDDEN = HEAD_DIM
SEL_BLOCK = 64
SEL_TOP_K = 8
WINDOW = 256
N_NSA_BRANCHES = 3
GATE_ROWS = 16
FFN_HIDDEN = (8 * D_MODEL + 3 * 256 - 1) // (3 * 256) * 256
DEEPNORM_ALPHA = 2.0 ** 0.25
LN_EPS = 1e-5
MASK_VALUE = -1e30

VMEM_LIMIT = 56 * 1024 * 1024
MXU_WIDTH = 256
BF16_ROWS = 16
ROW_BLOCK = 256

KEY_BLOCK = 256
SEL_SHIFT = SEL_BLOCK.bit_length() - 1
SEL_PER_KEY_BLOCK = KEY_BLOCK // SEL_BLOCK
KEY_AUG = 128
ALIBI_COL = HEAD_DIM
N_PIECES = 4
MASK_COL = HEAD_DIM + 16
QUERY_AUG = HEAD_DIM + 32
VAL_AUG = HEAD_DIM + 16
LOG2E = math.log2(math.e)
SCORES_AHEAD = 5

NAT_COLS = SSM_WIDTH + 4 * KV_WIDTH
TR_ROWS = ATTN_WIDTH + 2 * KV_WIDTH + N_KV_GROUPS * GATE_ROWS


def _bf16_pieces(value, n):
    pieces = []
    rest = value
    for _ in range(n):
        mant, expo = math.frexp(rest)
        piece = math.ldexp(round(mant * 256.0) / 256.0, expo)
        pieces.append(piece)
        rest -= piece
    return pieces


def _sigmoid(x):
    return 1.0 / (1.0 + jnp.exp(-x))


def _gelu(x):
    c = math.sqrt(2.0 / math.pi)
    return 0.5 * x * (1.0 + jnp.tanh(c * (x + 0.044715 * (x * x * x))))


def _layer_norm(y, gain, bias):
    mu = jnp.mean(y, axis=-1, keepdims=True)
    d = y - mu
    var = jnp.mean(d * d, axis=-1, keepdims=True)
    return d * lax.rsqrt(var + LN_EPS) * gain + bias


def _params(n_axes):
    return pltpu.CompilerParams(dimension_semantics=("arbitrary",) * n_axes, vmem_limit_bytes=VMEM_LIMIT)


def _ada_kernel(c_ref, w_ref, b_ref, o_ref):
    c = c_ref[...]
    a = c * _sigmoid(c)
    w = w_ref[...].astype(BF16)
    a_hi = a.astype(BF16)
    a_lo = (a - a_hi.astype(F32)).astype(BF16)
    mod = (jnp.dot(a_hi, w, preferred_element_type=F32) + jnp.dot(a_lo, w, preferred_element_type=F32)) + b_ref[...]
    for k in range(o_ref.shape[0]):
        o_ref[k] = mod[:, k * D_MODEL:(k + 1) * D_MODEL]


def _ada_mod(c, w_ada, b_ada):
    bsz = c.shape[0]
    per_step = 2
    return pl.pallas_call(
        _ada_kernel,
        grid=(6 // per_step,),
        in_specs=[
            pl.BlockSpec((bsz, D_MODEL), lambda j: (0, 0)),
            pl.BlockSpec((D_MODEL, per_step * D_MODEL), lambda j: (0, j)),
            pl.BlockSpec((1, per_step * D_MODEL), lambda j: (0, j)),
        ],
        out_specs=pl.BlockSpec((per_step, bsz, D_MODEL), lambda j: (j, 0, 0)),
        out_shape=jax.ShapeDtypeStruct((6, bsz, D_MODEL), F32),
        compiler_params=_params(1),
        name="ada_mod",
    )(c, w_ada, b_ada.reshape(1, 6 * D_MODEL))


def _in_proj_kernel(x_ref, sc_ref, sh_ref, wn_ref, wt_ref, ussm_ref, kc_ref, kp_ref, qT_ref, vp_ref, bgT_ref):
    i = pl.program_id(1)
    tm = x_ref.shape[1]
    u = (x_ref[0] * (1.0 + sc_ref[0]) + sh_ref[0]).astype(BF16)
    nat = jnp.dot(u, wn_ref[...], preferred_element_type=F32)
    ussm_ref[0] = nat[:, :SSM_WIDTH]
    for s in range(2):
        for g in range(N_KV_GROUPS):
            lo = SSM_WIDTH + s * KV_WIDTH + g * HEAD_DIM
            kc_ref[0, s, g] = nat[:, lo:lo + HEAD_DIM]
    pos = i * tm + lax.broadcasted_iota(jnp.int32, (tm, KEY_AUG), 0)
    col = lax.broadcasted_iota(jnp.int32, (tm, KEY_AUG), 1)
    blk = lax.shift_right_logical(pos, SEL_SHIFT)
    in_a = (col >= ALIBI_COL) & (col < ALIBI_COL + N_PIECES)
    in_b = (col >= ALIBI_COL + N_PIECES) & (col < ALIBI_COL + 2 * N_PIECES)
    hot = col == MASK_COL + (blk & (SEL_PER_KEY_BLOCK - 1))
    aux = jnp.where(in_a, blk * SEL_BLOCK, jnp.where(in_b, pos & (SEL_BLOCK - 1), jnp.where(hot, 1, 0))).astype(F32)
    head_lanes = col < HEAD_DIM
    for s in range(2):
        lo = SSM_WIDTH + (2 + s) * KV_WIDTH
        both = nat[:, lo:lo + KV_WIDTH]
        for g in range(N_KV_GROUPS):
            keys = both if g == 0 else pltpu.roll(both, HEAD_DIM, 1)
            kp_ref[0, s, g] = jnp.where(head_lanes, keys, aux).astype(BF16)
    tr = lax.dot_general(wt_ref[...], u, (((1,), (1,)), ((), ())), preferred_element_type=F32)
    qT_ref[0] = tr[:ATTN_WIDTH] * (HEAD_DIM ** -0.5)
    ones_rows = jnp.where(lax.broadcasted_iota(jnp.int32, (VAL_AUG - HEAD_DIM, tm), 0) == 0, 1.0, 0.0)
    for s in range(2):
        for g in range(N_KV_GROUPS):
            lo = ATTN_WIDTH + (s * N_KV_GROUPS + g) * HEAD_DIM
            v_aug = jnp.concatenate([tr[lo:lo + HEAD_DIM], ones_rows], axis=0).astype(BF16)
            for c in range(tm // KEY_BLOCK):
                vp_ref[0, s, g, c] = v_aug[:, c * KEY_BLOCK:(c + 1) * KEY_BLOCK]
    bgT_ref[0] = _sigmoid(tr[ATTN_WIDTH + 2 * KV_WIDTH:])


def _in_proj(x, scale1, shift1, w_nat, w_tr, *, tm):
    bsz, seq, _ = x.shape
    n_t = seq // tm
    n_kb = seq // KEY_BLOCK
    out_shapes = (
        jax.ShapeDtypeStruct((bsz, seq, SSM_WIDTH), F32),
        jax.ShapeDtypeStruct((bsz, 2, N_KV_GROUPS, seq, HEAD_DIM), F32),
        jax.ShapeDtypeStruct((bsz, 2, N_KV_GROUPS, seq, KEY_AUG), BF16),
        jax.ShapeDtypeStruct((bsz, ATTN_WIDTH, seq), F32),
        jax.ShapeDtypeStruct((bsz, 2, N_KV_GROUPS, n_kb, VAL_AUG, KEY_BLOCK), BF16),
        jax.ShapeDtypeStruct((bsz, N_KV_GROUPS * GATE_ROWS, seq), F32),
    )
    return pl.pallas_call(
        _in_proj_kernel,
        grid=(bsz, n_t),
        in_specs=[
            pl.BlockSpec((1, tm, D_MODEL), lambda b, i: (b, i, 0)),
            pl.BlockSpec((1, 1, D_MODEL), lambda b, i: (b, 0, 0)),
            pl.BlockSpec((1, 1, D_MODEL), lambda b, i: (b, 0, 0)),
            pl.BlockSpec((D_MODEL, NAT_COLS), lambda b, i: (0, 0)),
            pl.BlockSpec((TR_ROWS, D_MODEL), lambda b, i: (0, 0)),
        ],
        out_specs=(
            pl.BlockSpec((1, tm, SSM_WIDTH), lambda b, i: (b, i, 0)),
            pl.BlockSpec((1, 2, N_KV_GROUPS, tm, HEAD_DIM), lambda b, i: (b, 0, 0, i, 0)),
            pl.BlockSpec((1, 2, N_KV_GROUPS, tm, KEY_AUG), lambda b, i: (b, 0, 0, i, 0)),
            pl.BlockSpec((1, ATTN_WIDTH, tm), lambda b, i: (b, 0, i)),
            pl.BlockSpec((1, 2, N_KV_GROUPS, tm // KEY_BLOCK, VAL_AUG, KEY_BLOCK), lambda b, i: (b, 0, 0, i, 0, 0)),
            pl.BlockSpec((1, N_KV_GROUPS * GATE_ROWS, tm), lambda b, i: (b, 0, i)),
        ),
        out_shape=out_shapes,
        compiler_params=_params(2),
        name="in_proj",
    )(x, scale1, shift1, w_nat, w_tr)


def _split_w_in(w_in):
    o_q = SSM_WIDTH
    o_kv = o_q + ATTN_WIDTH
    o_bg = o_kv + 6 * KV_WIDTH
    o_mg = o_bg + N_NSA_BRANCHES * N_HEADS
    kv = [w_in[:, o_kv + s * KV_WIDTH:o_kv + (s + 1) * KV_WIDTH] for s in range(6)]
    w_nat = jnp.concatenate([w_in[:, :o_q], kv[0], kv[1], kv[2], kv[4]], axis=1)
    per_group = HEADS_PER_GROUP * N_NSA_BRANCHES
    bg = w_in[:, o_bg:o_mg].reshape(D_MODEL, N_KV_GROUPS, per_group)
    bg = jnp.pad(bg, ((0, 0), (0, 0), (0, GATE_ROWS - per_group))).reshape(D_MODEL, N_KV_GROUPS * GATE_ROWS)
    w_tr = jnp.concatenate([w_in[:, o_q:o_kv], kv[3], kv[5], bg], axis=1).T
    return w_nat.astype(BF16), w_tr.astype(BF16), w_in[:, o_mg:].astype(BF16)


def _ssm_prep_kernel(*refs):
    for k in range(refs[0].shape[0]):
        _ssm_prep_group(*[ref.at[pl.ds(k, 1)] for ref in refs])


def _ssm_prep_group(a_ref, logdt_ref, bT_ref, c_ref, d_ref, mT_ref, pT_ref, qT_ref, a16_ref):
    t_n = SSM_CHUNK
    c_n = SSM_GROUP_SIZE
    n_p = SSM_STATE
    rows = t_n * c_n
    dt = jnp.exp(logdt_ref[0])
    a_re = a_ref[0, 0]
    a_im = a_ref[0, 1]
    decay = jnp.exp(a_re * dt)
    ab_re = decay * jnp.cos(a_im * dt)
    ab_im = decay * jnp.sin(a_im * dt)
    n_re = ab_re - 1.0
    denom = a_re * a_re + a_im * a_im
    f_re = (n_re * a_re + ab_im * a_im) / denom
    f_im = (ab_im * a_re - n_re * a_im) / denom
    powers = [(jnp.ones_like(ab_re), jnp.zeros_like(ab_re))]
    for _ in range(t_n):
        p_re, p_im = powers[-1]
        powers.append((p_re * ab_re - p_im * ab_im, p_re * ab_im + p_im * ab_re))

    def per_token(first, step):
        picks = [powers[first + step * k] for k in range(t_n)]
        return (jnp.concatenate([jnp.broadcast_to(p[0], (c_n, 2 * n_p)) for p in picks], axis=0),
                jnp.concatenate([jnp.broadcast_to(p[1], (c_n, 2 * n_p)) for p in picks], axis=0))

    def cmul(x_re, x_im, y_re, y_im):
        return x_re * y_re - x_im * y_im, x_re * y_im + x_im * y_re

    bb_re, bb_im = cmul(f_re, f_im, bT_ref[0, 0], bT_ref[0, 1])
    bt_re = jnp.concatenate([bb_re] * t_n, axis=0)
    bt_im = jnp.concatenate([bb_im] * t_n, axis=0)
    ct_re = jnp.concatenate([c_ref[0, 0]] * t_n, axis=0)
    ct_im = jnp.concatenate([c_ref[0, 1]] * t_n, axis=0)

    p_re, p_im = cmul(*per_token(t_n - 1, -1), bt_re, bt_im)
    low_half = lax.broadcasted_iota(jnp.int32, (1, 2 * n_p), 1) < n_p
    pT_ref[0] = jnp.where(low_half, p_re, p_im).T.astype(BF16)

    q_re, q_im = cmul(ct_re, ct_im, *per_token(1, 1))
    qT_ref[0] = jnp.where(low_half, q_re, -q_im).astype(BF16)
    a16_ref[0] = jnp.concatenate([powers[t_n][0][:, :n_p], powers[t_n][1][:, :n_p]], axis=0)

    k_re, k_im = cmul(ct_re, ct_im, *per_token(0, 1))
    nt = (((1,), (1,)), ((), ()))
    strips = (lax.dot_general(_split3(k_re[:, :n_p]), _stack3(bt_re[:, :n_p], axis=1), nt, preferred_element_type=F32)
              - lax.dot_general(_split3(k_im[:, :n_p]), _stack3(bt_im[:, :n_p], axis=1), nt,
                                preferred_element_type=F32))
    lane = lax.broadcasted_iota(jnp.int32, (1, rows), 1)
    s_lane = lax.shift_right_logical(lane, c_n.bit_length() - 1)
    ci_lane = lane & (c_n - 1)
    row = lax.broadcasted_iota(jnp.int32, (rows, 1), 0)
    t_row = lax.shift_right_logical(row, c_n.bit_length() - 1)
    co_row = row & (c_n - 1)
    blocks = []
    for t in range(t_n):
        acc = jnp.zeros((c_n, rows), F32)
        for lag in range(t + 1):
            acc = jnp.where(s_lane == t - lag, strips[lag * c_n:(lag + 1) * c_n, :], acc)
        blocks.append(acc)
    d_rows = jnp.concatenate([d_ref[0]] * t_n, axis=0)
    skip = jnp.where((t_row == s_lane) & (co_row == ci_lane), d_rows, 0.0)
    mT_ref[0] = (jnp.concatenate(blocks, axis=0) + skip).astype(BF16)


def _ssm_matrices(a_re, a_im, log_dt, b_re, b_im, c_re, c_im, d_skip):
    n_g, n_p = a_re.shape
    c_n = SSM_GROUP_SIZE
    rows = SSM_CHUNK * c_n
    twice = lambda x: jnp.concatenate([x, x], axis=-1).astype(F32)
    a = twice(jnp.stack([a_re, a_im], axis=1)).reshape(n_g, 2, 1, 2 * n_p)
    bT = twice(jnp.swapaxes(jnp.stack([b_re, b_im], axis=1), 2, 3))
    c = twice(jnp.stack([c_re, c_im], axis=1))
    per_step = SSM_PREP_GROUPS
    spec = lambda shape: pl.BlockSpec((per_step,) + shape, lambda g: (g,) + (0,) * len(shape))
    mT, pT, qT, a16 = pl.pallas_call(
        _ssm_prep_kernel,
        grid=(n_g // per_step,),
        in_specs=[spec((2, 1, 2 * n_p)), spec((1, 1)), spec((2, c_n, 2 * n_p)), spec((2, c_n, 2 * n_p)),
                  spec((c_n, 1))],
        out_specs=(spec((rows, rows)), spec((2 * n_p, rows)), spec((rows, 2 * n_p)), spec((2, n_p))),
        out_shape=(
            jax.ShapeDtypeStruct((n_g, rows, rows), BF16),
            jax.ShapeDtypeStruct((n_g, 2 * n_p, rows), BF16),
            jax.ShapeDtypeStruct((n_g, rows, 2 * n_p), BF16),
            jax.ShapeDtypeStruct((n_g, 2, n_p), F32),
        ),
        compiler_params=_params(1),
        name="ssm_prep",
    )(a, log_dt.astype(F32).reshape(n_g, 1, 1), bT, c, d_skip.astype(F32).reshape(n_g, c_n, 1))
    return mT, pT.reshape(n_g, 2, n_p, rows), qT, a16


def _ssm_kernel(u_ref, mT_ref, pT_ref, qT_ref, a16_ref, o_ref, ut_scr, yt_scr, sre, sim, *, n_chunks):
    n_g = SSM_GROUP_TILE
    n_b = u_ref.shape[0]
    t_n = SSM_CHUNK
    c_n = SSM_GROUP_SIZE
    lanes = [(e, g) for e in range(n_b) for g in range(n_g)]

    for e in range(n_b):
        for s in range(t_n):
            x_s = u_ref[e, pl.ds(s, n_chunks, stride=t_n), :]
            ut_scr[e * n_g:(e + 1) * n_g, s * c_n:(s + 1) * c_n, :] = (
                x_s.T.reshape(n_g, c_n, n_chunks).astype(BF16))

    pitch = n_chunks + SSM_STATE_ROW_PAD

    def lane_rows(k):
        return pl.ds(k * pitch, n_chunks)

    for k, (e, g) in enumerate(lanes):
        ut = ut_scr[k]
        sre[lane_rows(k), :] = jnp.dot(pT_ref[g, 0], ut, preferred_element_type=F32).T
        sim[lane_rows(k), :] = jnp.dot(pT_ref[g, 1], ut, preferred_element_type=F32).T

    a_r = jnp.concatenate([a16_ref[pl.ds(0, n_g, stride=2), :]] * n_b, axis=0)
    a_i = jnp.concatenate([a16_ref[pl.ds(1, n_g, stride=2), :]] * n_b, axis=0)

    def carry_states(c, h):
        h_r, h_i = h
        rows = pl.ds(c, len(lanes), stride=pitch)
        s_r = sre[rows, :]
        s_i = sim[rows, :]
        sre[rows, :] = h_r
        sim[rows, :] = h_i
        return a_r * h_r - a_i * h_i + s_r, a_r * h_i + a_i * h_r + s_i

    zero = jnp.zeros((len(lanes), SSM_STATE), F32)
    lax.fori_loop(0, n_chunks, carry_states, (zero, zero), unroll=4)

    nt = (((1,), (1,)), ((), ()))
    for k, (e, g) in enumerate(lanes):
        yt = jnp.dot(mT_ref[g], ut_scr[k], preferred_element_type=F32)
        entering = jnp.concatenate([sre[lane_rows(k), :], sim[lane_rows(k), :]], axis=1).astype(BF16)
        yt = yt + lax.dot_general(qT_ref[g], entering, nt, preferred_element_type=F32)
        yt_scr[k] = _gelu(yt)

    for e in range(n_b):
        for t in range(t_n):
            z = yt_scr[e * n_g:(e + 1) * n_g, t * c_n:(t + 1) * c_n, :].reshape(n_g * c_n, n_chunks)
            o_ref[e, pl.ds(t, n_chunks, stride=t_n), :] = z.T


def _ssm(u_ssm, mats):
    bsz, seq, width = u_ssm.shape
    n_chunks = seq // SSM_CHUNK
    mT, pT, qT, a16 = mats
    gt = SSM_GROUP_TILE
    n_tiles = SSM_GROUPS // gt
    rows = SSM_CHUNK * SSM_GROUP_SIZE
    a16 = a16.reshape(2 * SSM_GROUPS, SSM_STATE)
    bt = SSM_BATCH_TILE if bsz % SSM_BATCH_TILE == 0 else 1
    n_lanes = bt * gt
    per_tile = lambda a: pl.BlockSpec((gt,) + a.shape[1:], lambda t, b: (t,) + (0,) * (a.ndim - 1))
    return pl.pallas_call(
        functools.partial(_ssm_kernel, n_chunks=n_chunks),
        grid=(n_tiles, bsz // bt),
        in_specs=[
            pl.BlockSpec((bt, seq, gt * SSM_GROUP_SIZE), lambda t, b: (b, 0, t)),
            per_tile(mT), per_tile(pT), per_tile(qT),
            pl.BlockSpec((2 * gt, SSM_STATE), lambda t, b: (t, 0)),
        ],
        out_specs=pl.BlockSpec((bt, seq, gt * SSM_GROUP_SIZE), lambda t, b: (b, 0, t)),
        out_shape=jax.ShapeDtypeStruct(u_ssm.shape, F32),
        scratch_shapes=[
            pltpu.VMEM((n_lanes, rows, n_chunks), BF16),
            pltpu.VMEM((n_lanes, rows, n_chunks), F32),
            pltpu.VMEM((n_lanes * (n_chunks + SSM_STATE_ROW_PAD), SSM_STATE), F32),
            pltpu.VMEM((n_lanes * (n_chunks + SSM_STATE_ROW_PAD), SSM_STATE), F32),
        ],
        compiler_params=_params(2),
        name="ssm",
    )(u_ssm, mT, pT, qT, a16)


def _split3(x):
    hi = x.astype(BF16)
    lo = (x - hi.astype(F32)).astype(BF16)
    return jnp.concatenate([hi, lo, hi], axis=-1)


def _stack3(w, axis=0):
    hi = w.astype(BF16)
    lo = (w - hi.astype(F32)).astype(BF16)
    return jnp.concatenate([hi, hi, lo], axis=axis)


def _compress_kernel(kx_ref, vx_ref, pek_ref, pev_ref, wk1_ref, wk2_ref, wv1_ref, wv2_ref, kc_ref, vcT_ref):
    def mlp(x_ref, g, pe_ref, w1_ref, w2_ref):
        n = x_ref.shape[3] // CMP_STRIDE
        x = jnp.concatenate([x_ref[0, 0, g, pl.ds(l, n, stride=CMP_STRIDE), :] for l in range(CMP_STRIDE)], axis=1)
        both = jnp.dot(_split3(x), w1_ref[...], preferred_element_type=F32)
        pe_both = jnp.dot(_split3(pe_ref[...]), w1_ref[...], preferred_element_type=F32)
        bias = pe_both[0:1, :CMP_HIDDEN] + pe_both[1:2, CMP_HIDDEN:]
        h = _gelu(both[:, :CMP_HIDDEN] + pltpu.roll(both[:, CMP_HIDDEN:], n - 1, 0) + bias)
        return jnp.dot(_split3(h), w2_ref[...], preferred_element_type=F32)

    for g in range(kx_ref.shape[2]):
        kc = mlp(kx_ref, g, pek_ref, wk1_ref, wk2_ref)
        kc_hi = kc.astype(BF16)
        kc_lo = (kc - kc_hi.astype(F32)).astype(BF16)
        kc_ref[0, g] = jnp.concatenate([kc_hi, kc_hi, kc_lo, jnp.zeros_like(kc_hi)], axis=1)
        vcT_ref[0, g] = mlp(vx_ref, g, pev_ref, wv1_ref, wv2_ref).T.astype(BF16)


def _compress(kv, pe_k, pe_v, wk1, wk2, wv1, wv2):
    bsz, _, n_g, seq, width = kv.shape
    n_chunks = seq // CMP_STRIDE
    half = CMP_STRIDE * HEAD_DIM
    side_by_side = lambda w1: _stack3(jnp.concatenate([w1[:half], w1[half:]], axis=1))
    two_rows = lambda pe: jnp.pad(pe.reshape(2, half), ((0, BF16_ROWS - 2), (0, 0)))
    pe_k, pe_v = two_rows(pe_k), two_rows(pe_v)
    wk1, wv1, wk2, wv2 = side_by_side(wk1), side_by_side(wv1), _stack3(wk2), _stack3(wv2)
    blk = lambda s: pl.BlockSpec((1, 1, n_g, seq, width), lambda b: (b, s, 0, 0, 0))
    full = lambda a: pl.BlockSpec(a.shape, lambda b: (0, 0))
    return pl.pallas_call(
        _compress_kernel,
        grid=(bsz,),
        in_specs=[blk(0), blk(1), full(pe_k), full(pe_v), full(wk1), full(wk2), full(wv1), full(wv2)],
        out_specs=(
            pl.BlockSpec((1, n_g, n_chunks, 4 * HEAD_DIM), lambda b: (b, 0, 0, 0)),
            pl.BlockSpec((1, n_g, HEAD_DIM, n_chunks), lambda b: (b, 0, 0, 0)),
        ),
        out_shape=(
            jax.ShapeDtypeStruct((bsz, n_g, n_chunks, 4 * HEAD_DIM), BF16),
            jax.ShapeDtypeStruct((bsz, n_g, HEAD_DIM, n_chunks), BF16),
        ),
        compiler_params=_params(1),
        name="compress",
    )(kv, kv, pe_k, pe_v, wk1, wk2, wv1, wv2)


def _nsa_kernel(qT_ref, kc_ref, vcT_ref, ks_ref, vs_ref, kw_ref, vw_ref, bg_ref, o_ref,
                negm_scr, qa_scr, m_scr, acc_scr, out_scr, used_smem, todo_smem, *, tq, n_cmp, n_sel):
    g = pl.program_id(1)
    i = pl.program_id(2)
    t_row = i * tq + lax.broadcasted_iota(jnp.int32, (1, tq), 1)
    slopes = [jnp.where(g == 0, 2.0 ** -(hh + 1), 2.0 ** -(HEADS_PER_GROUP + hh + 1)).astype(F32)
              for hh in range(HEADS_PER_GROUP)]

    def gate(hh, branch):
        r = hh * N_NSA_BRANCHES + branch
        return bg_ref[0, 0, r:r + 1, :]

    def head_rows(hh):
        return slice(hh * HEAD_DIM, (hh + 1) * HEAD_DIM)

    def head_lanes(hh):
        return slice(hh * tq, (hh + 1) * tq)

    kc3 = kc_ref[0, 0]
    vcT = vcT_ref[0, 0]
    cmp_end = lax.broadcasted_iota(jnp.int32, (n_cmp, 1), 0) * CMP_STRIDE + (CMP_BLOCK - 1)
    dist_c = (t_row - cmp_end).astype(F32)
    valid_c = dist_c >= 0.0
    piece = lax.broadcasted_iota(jnp.int32, (16, 1), 0)
    log2e_pieces = _bf16_pieces(LOG2E, N_PIECES)
    log2e_col = jnp.zeros((16, 1), F32)
    for k, value in enumerate(log2e_pieces):
        log2e_col = jnp.where((piece == k) | (piece == N_PIECES + k), value, log2e_col)
    for hh in range(HEADS_PER_GROUP):
        alibi = jnp.broadcast_to(log2e_col * slopes[hh], (16, tq)).astype(BF16)
        q_l2 = (qT_ref[0, head_rows(hh), :] * LOG2E).astype(BF16)
        qa_scr[:, head_lanes(hh)] = jnp.concatenate([q_l2, alibi], axis=0)

    def cmp_scores(hh):
        q_h = qT_ref[0, head_rows(hh), :]
        q_hi = q_h.astype(BF16)
        q_lo = (q_h - q_hi.astype(F32)).astype(BF16)
        q3 = jnp.concatenate([q_hi, q_lo, q_hi, jnp.zeros_like(q_hi)], axis=0)
        return jnp.dot(kc3, q3, preferred_element_type=F32)

    def cmp_attend(p_heads, hh, s):
        s = jnp.where(valid_c, s - slopes[hh] * dist_c, MASK_VALUE)
        m = jnp.max(s, axis=0, keepdims=True)
        p = jnp.where(valid_c, jnp.exp(s - m), 0.0)
        l = jnp.sum(p, axis=0, keepdims=True)
        p = p / jnp.where(l > 0.0, l, 1.0)
        o_c = jnp.dot(vcT, p.astype(BF16), preferred_element_type=F32)
        out_scr[head_rows(hh), :] = gate(hh, 0) * o_c
        p_heads.append(p)

    def select_blocks(p_heads, _):
        p_sum = sum(p_heads[1:], p_heads[0])
        blk = lax.broadcasted_iota(jnp.int32, (n_sel, n_cmp), 0)
        cmp_ix = lax.broadcasted_iota(jnp.int32, (n_sel, n_cmp), 1)
        ratio = SEL_BLOCK // CMP_STRIDE
        extra = CMP_BLOCK // CMP_STRIDE - 1
        overlap_t = jnp.where((cmp_ix >= ratio * blk - extra) & (cmp_ix <= ratio * blk + ratio - 1),
                              1.0, 0.0).astype(BF16)
        imp = jnp.zeros((n_sel, tq), F32)
        rest = p_sum
        for _ in range(3):
            part = rest.astype(BF16)
            imp = imp + jnp.dot(overlap_t, part, preferred_element_type=F32)
            rest = rest - part.astype(F32)

        j_col = lax.broadcasted_iota(jnp.int32, (n_sel, 1), 0)
        forced = (j_col == 0) | (j_col == lax.shift_right_logical(t_row, SEL_SHIFT))
        future = j_col * SEL_BLOCK > t_row
        score = jnp.where(forced, jnp.inf, jnp.where(future, -jnp.inf, imp))
        rank = jnp.zeros((n_sel, tq), F32)
        for jp in range(n_sel):
            other = score[jp:jp + 1, :]
            ahead = (other > score) | ((other == score) & (j_col > jp))
            rank = rank + jnp.where(ahead, 1.0, 0.0)
        neg = jnp.where(rank < float(min(SEL_TOP_K, n_sel)), 0.0, MASK_VALUE)
        pad_rows = jnp.zeros((BF16_ROWS - SEL_PER_KEY_BLOCK, tq), F32)
        for c in range(n_sel // SEL_PER_KEY_BLOCK):
            rows = neg[c * SEL_PER_KEY_BLOCK:(c + 1) * SEL_PER_KEY_BLOCK, :]
            negm_scr[c] = jnp.concatenate([rows, pad_rows], axis=0).astype(BF16)
            used_smem[c] = (jnp.max(rows) > 0.5 * MASK_VALUE).astype(jnp.int32)

    rel = (lax.broadcasted_iota(jnp.int32, (KEY_BLOCK, tq), 1)
           - lax.broadcasted_iota(jnp.int32, (KEY_BLOCK, tq), 0))
    zero_tail = jnp.zeros((KEY_AUG - QUERY_AUG, tq), BF16)
    no_mask = jnp.zeros((BF16_ROWS, tq), BF16)
    all_masked = jnp.full((BF16_ROWS, tq), MASK_VALUE, BF16)

    sel_even, sel_odd, win_prev, win_diag = range(4)
    m_scr[...] = jnp.full(m_scr.shape, MASK_VALUE, F32)
    acc_scr[...] = jnp.zeros(acc_scr.shape, F32)

    def scores(job, hh):
        k_ref, _, kb, enabled, _, _ = job
        k0 = pl.multiple_of(kb * KEY_BLOCK, KEY_BLOCK)
        k_aug = k_ref[0, 0, 0, pl.ds(k0, KEY_BLOCK), :]
        if enabled is None:
            mask_tile = no_mask
        elif enabled is True:
            mask_tile = negm_scr[kb]
        else:
            mask_tile = jnp.where(enabled, negm_scr[kb], all_masked)
        q_aug = jnp.concatenate([qa_scr[:, head_lanes(hh)], mask_tile, zero_tail], axis=0)
        return jnp.dot(k_aug, q_aug, preferred_element_type=F32)

    def accumulate(job, hh, s):
        _, v_ref, kb, _, valid, slot = job
        cols = head_lanes(hh)
        if valid is not None:
            s = jnp.where(valid, s, MASK_VALUE)
        m_old = m_scr[slot, :, cols]
        m_new = jnp.maximum(m_old, jnp.max(s, axis=0, keepdims=True))
        p = jnp.exp2(s - jnp.maximum(m_new, 0.1 * MASK_VALUE))
        alpha = jnp.exp2(m_old - m_new)
        pv = jnp.dot(v_ref[0, 0, 0, kb], p.astype(BF16), preferred_element_type=F32)
        acc_scr[slot, :, cols] = alpha * acc_scr[slot, :, cols] + pv
        m_scr[slot, :, cols] = m_new

    def pipeline(units):
        issue = lambda unit: None if unit[0] is None else unit[0]()
        pending = [issue(unit) for unit in units[:SCORES_AHEAD]]
        for n, unit in enumerate(units):
            if n + SCORES_AHEAD < len(units):
                pending.append(issue(units[n + SCORES_AHEAD]))
            unit[1](pending.pop(0))

    def attention_units(jobs):
        return [(functools.partial(scores, job, hh), functools.partial(accumulate, job, hh))
                for job in jobs for hh in range(HEADS_PER_GROUP)]

    def finish(branch, slot_a, slot_b):
        m_a = m_scr[slot_a]
        m_b = m_scr[slot_b]
        m = jnp.maximum(m_a, m_b)
        w_a = jnp.exp2(m_a - m)
        w_b = jnp.exp2(m_b - m)
        for hh in range(HEADS_PER_GROUP):
            cols = head_lanes(hh)
            acc = acc_scr[slot_a, :, cols] * w_a[:, cols] + acc_scr[slot_b, :, cols] * w_b[:, cols]
            l = acc[HEAD_DIM:HEAD_DIM + 1, :]
            o_b = acc[:HEAD_DIM, :] / jnp.where(l > 0.0, l, 1.0)
            out_scr[head_rows(hh), :] = out_scr[head_rows(hh), :] + gate(hh, branch) * o_b

    prev = jnp.maximum(i - 1, 0)
    causal = rel >= 0
    off_without_prev = jnp.where(i > 0, 0, WINDOW + KEY_BLOCK)
    dist_prev = rel + KEY_BLOCK + off_without_prev
    p_heads = []
    pipeline(
        [(functools.partial(cmp_scores, hh), functools.partial(cmp_attend, p_heads, hh))
         for hh in range(HEADS_PER_GROUP)]
        + [(None, functools.partial(select_blocks, p_heads))]
        + attention_units([(kw_ref, vw_ref, prev, None, dist_prev < WINDOW, win_prev),
                           (kw_ref, vw_ref, i, None, causal, win_diag),
                           (ks_ref, vs_ref, prev, i > 0, None, sel_even),
                           (ks_ref, vs_ref, i, True, causal, sel_odd)]))

    n_todo = jnp.int32(0)
    for c in range(used_smem.shape[0]):
        take = jnp.logical_and(c < i - 1, used_smem[c] > 0)

        @pl.when(take)
        def _(c=c, slot=n_todo):
            todo_smem[slot] = c

        n_todo = n_todo + take.astype(jnp.int32)
    todo_smem[n_todo] = 0

    def past_pair(j, carry):
        pipeline(attention_units([(ks_ref, vs_ref, todo_smem[2 * j], True, None, sel_even),
                                  (ks_ref, vs_ref, todo_smem[2 * j + 1], 2 * j + 1 < n_todo, None, sel_odd)]))
        return carry

    lax.fori_loop(0, lax.shift_right_logical(n_todo + 1, 1), past_pair, 0)
    finish(1, sel_even, sel_odd)
    finish(2, win_prev, win_diag)

    o_ref[0] = out_scr[...].T


def _nsa(qT, kc3, vcT, k_aug, v_aug, bgT, *, tq):
    bsz, _, seq = qT.shape
    assert tq == KEY_BLOCK == WINDOW and seq % tq == 0
    n_cmp = kc3.shape[2]
    n_sel = seq // SEL_BLOCK
    n_kb = seq // KEY_BLOCK
    gw = HEADS_PER_GROUP * HEAD_DIM
    bg4 = bgT.reshape(bsz, N_KV_GROUPS, GATE_ROWS, seq)
    k_spec = lambda s: pl.BlockSpec((1, 1, 1, seq, KEY_AUG), lambda b, g, i: (b, s, g, 0, 0))
    v_spec = lambda s: pl.BlockSpec((1, 1, 1, n_kb, VAL_AUG, KEY_BLOCK), lambda b, g, i: (b, s, g, 0, 0, 0))
    return pl.pallas_call(
        functools.partial(_nsa_kernel, tq=tq, n_cmp=n_cmp, n_sel=n_sel),
        grid=(bsz, N_KV_GROUPS, seq // tq),
        in_specs=[
            pl.BlockSpec((1, gw, tq), lambda b, g, i: (b, g, i)),
            pl.BlockSpec((1, 1, n_cmp, 4 * HEAD_DIM), lambda b, g, i: (b, g, 0, 0)),
            pl.BlockSpec((1, 1, HEAD_DIM, n_cmp), lambda b, g, i: (b, g, 0, 0)),
            k_spec(0), v_spec(0),
            k_spec(1), v_spec(1),
            pl.BlockSpec((1, 1, GATE_ROWS, tq), lambda b, g, i: (b, g, 0, i)),
        ],
        out_specs=pl.BlockSpec((1, tq, gw), lambda b, g, i: (b, i, g)),
        out_shape=jax.ShapeDtypeStruct((bsz, seq, ATTN_WIDTH), F32),
        scratch_shapes=[
            pltpu.VMEM((n_kb, 16, tq), BF16),
            pltpu.VMEM((HEAD_DIM + 16, HEADS_PER_GROUP * tq), BF16),
            pltpu.VMEM((4, 1, HEADS_PER_GROUP * tq), F32),
            pltpu.VMEM((4, VAL_AUG, HEADS_PER_GROUP * tq), F32),
            pltpu.VMEM((gw, tq), F32),
            pltpu.SMEM((n_kb,), jnp.int32),
            pltpu.SMEM((n_kb + 1,), jnp.int32),
        ],
        compiler_params=_params(3),
        name="nsa",
    )(qT, kc3, vcT, k_aug, v_aug, k_aug, v_aug, bg4)


def _merge_kernel(x_ref, g_ref, o_ref_in, sc_ref, sh_ref, gt_ref, wm_ref, wl_ref, wg_ref, wn_ref, wo_ref,
                  lg_ref, lb_ref, y_ref):
    tm = x_ref.shape[1]
    row_blocks = [slice(r, r + ROW_BLOCK) for r in range(0, tm, ROW_BLOCK)]
    halves = [(0, D_MODEL // 2), (D_MODEL // 2, D_MODEL)]
    dot = functools.partial(jnp.dot, preferred_element_type=F32)

    def first_stage(rows):
        g = g_ref[0, rows, :].astype(BF16)
        o = o_ref_in[0, rows, :].astype(BF16)
        branch = [(dot(g, wl_ref[:, lo:hi]), dot(g, wg_ref[:, lo:hi]), dot(o, wn_ref[:, lo:hi])) for lo, hi in halves]
        x = x_ref[0, rows, :]
        u = (x * (1.0 + sc_ref[0]) + sh_ref[0]).astype(BF16)
        gates = [(dot(u, wm_ref[:, lo:hi]), dot(u, wm_ref[:, D_MODEL + lo:D_MODEL + hi])) for lo, hi in halves]
        return x, branch, gates

    def second_stage(rows, x, branch, gates):
        mix = None
        for (lo, hi), (lin, gate, z_nsa), (m_ssm, m_nsa) in zip(halves, branch, gates):
            merged = _sigmoid(m_ssm) * (lin * _sigmoid(gate)) + _sigmoid(m_nsa) * z_nsa
            part = dot(merged.astype(BF16), wo_ref[lo:hi, :])
            mix = part if mix is None else mix + part
        y = DEEPNORM_ALPHA * x + (1.0 + gt_ref[0]) * mix
        y_ref[0, rows, :] = _layer_norm(y, lg_ref[...], lb_ref[...])

    staged = None
    for rows in row_blocks + [None]:
        upcoming = first_stage(rows) if rows is not None else None
        if staged is not None:
            second_stage(*staged)
        staged = (rows,) + upcoming if rows is not None else None


def _merge(x, g_ssm, o_nsa, scale1, shift1, gate1, w_m, w_lin, w_gate, w_nsa, w_out, ln_g, ln_b, *, tm):
    bsz, seq, _ = x.shape
    row = pl.BlockSpec((1, 1, D_MODEL), lambda b, i: (b, 0, 0))
    full = lambda a: pl.BlockSpec(a.shape, lambda b, i: (0, 0), pipeline_mode=pl.Buffered(1))
    return pl.pallas_call(
        _merge_kernel,
        grid=(bsz, seq // tm),
        in_specs=[
            pl.BlockSpec((1, tm, D_MODEL), lambda b, i: (b, i, 0)),
            pl.BlockSpec((1, tm, SSM_WIDTH), lambda b, i: (b, i, 0)),
            pl.BlockSpec((1, tm, ATTN_WIDTH), lambda b, i: (b, i, 0)),
            row, row, row,
            full(w_m), full(w_lin), full(w_gate), full(w_nsa), full(w_out), full(ln_g), full(ln_b),
        ],
        out_specs=pl.BlockSpec((1, tm, D_MODEL), lambda b, i: (b, i, 0)),
        out_shape=jax.ShapeDtypeStruct(x.shape, F32),
        compiler_params=_params(2),
        name="merge",
    )(x, g_ssm, o_nsa, scale1, shift1, gate1, w_m, w_lin, w_gate, w_nsa, w_out, ln_g, ln_b)


def _ffn_kernel(x_ref, sc_ref, sh_ref, gt_ref, wg_ref, wu_ref, wd_ref, lg_ref, lb_ref, y_ref, *, n_chunks):
    tm = x_ref.shape[1]
    row_blocks = [slice(r, r + ROW_BLOCK) for r in range(0, tm, ROW_BLOCK)]
    tiles = FFN_HIDDEN // MXU_WIDTH
    edges = [MXU_WIDTH * ((tiles * c + n_chunks - 1) // n_chunks) for c in range(n_chunks + 1)]
    chunks = [slice(lo, hi) for lo, hi in zip(edges[:-1], edges[1:])]
    dot = functools.partial(jnp.dot, preferred_element_type=F32)

    def gate_up(rows):
        x = x_ref[0, rows, :]
        u = (x * (1.0 + sc_ref[0]) + sh_ref[0]).astype(BF16)
        return x, [(dot(u, wg_ref[:, cols]), dot(u, wu_ref[:, cols])) for cols in chunks]

    def down(rows, x, products):
        ffn = None
        for cols, (a, b) in zip(chunks, products):
            part = dot(((a * _sigmoid(a)) * b).astype(BF16), wd_ref[cols, :])
            ffn = part if ffn is None else ffn + part
        y = DEEPNORM_ALPHA * x + (1.0 + gt_ref[0]) * ffn
        y_ref[0, rows, :] = _layer_norm(y, lg_ref[...], lb_ref[...])

    staged = None
    for rows in row_blocks + [None]:
        upcoming = gate_up(rows) if rows is not None else None
        if staged is not None:
            down(*staged)
        staged = (rows,) + upcoming if rows is not None else None


def _ffn(x, scale2, shift2, gate2, w_gate, w_up, w_down, ln_g, ln_b, *, tm, n_chunks):
    bsz, seq, _ = x.shape
    row = pl.BlockSpec((1, 1, D_MODEL), lambda b, i: (b, 0, 0))
    resident = lambda a: pl.BlockSpec(a.shape, lambda b, i: (0, 0), pipeline_mode=pl.Buffered(1))
    return pl.pallas_call(
        functools.partial(_ffn_kernel, n_chunks=n_chunks),
        grid=(bsz, seq // tm),
        in_specs=[
            pl.BlockSpec((1, tm, D_MODEL), lambda b, i: (b, i, 0)),
            row, row, row,
            resident(w_gate), resident(w_up), resident(w_down), resident(ln_g), resident(ln_b),
        ],
        out_specs=pl.BlockSpec((1, tm, D_MODEL), lambda b, i: (b, i, 0)),
        out_shape=jax.ShapeDtypeStruct(x.shape, F32),
        compiler_params=_params(2),
        name="ffn",
    )(x, scale2, shift2, gate2, w_gate, w_up, w_down, ln_g, ln_b)


def _layer(x, c, w_ada, b_ada, w_in, ssm_a_re, ssm_a_im, ssm_log_dt, ssm_b_re, ssm_b_im, ssm_c_re, ssm_c_im,
           ssm_d, w_glu_lin, w_glu_gate, cmp_pe_k, cmp_pe_v, w_cmp_k1, w_cmp_k2, w_cmp_v1, w_cmp_v2,
           w_nsa_proj, w_out, ln1_g, ln1_b, w_ffn_gate, w_ffn_up, w_ffn_down, ln2_g, ln2_b):
    bsz, seq, _ = x.shape
    tq = min(KEY_BLOCK, seq)
    tm = min(512, seq)
    tm_dense = min(1024, seq)

    mod = _ada_mod(c, w_ada, b_ada).reshape(6, bsz, 1, D_MODEL)
    shift1, scale1, gate1, shift2, scale2, gate2 = (mod[k] for k in range(6))

    w_nat, w_tr, w_mg = _split_w_in(w_in)
    u_ssm, kv_cmp, k_aug, qT, v_aug, bgT = _in_proj(x, scale1, shift1, w_nat, w_tr, tm=tm)

    mats = _ssm_matrices(ssm_a_re, ssm_a_im, ssm_log_dt, ssm_b_re, ssm_b_im, ssm_c_re, ssm_c_im, ssm_d)
    g_ssm = _ssm(u_ssm, mats)

    kc3, vcT = _compress(kv_cmp,
                        cmp_pe_k.reshape(1, CMP_BLOCK * HEAD_DIM), cmp_pe_v.reshape(1, CMP_BLOCK * HEAD_DIM),
                        w_cmp_k1.reshape(CMP_BLOCK * HEAD_DIM, -1), w_cmp_k2,
                        w_cmp_v1.reshape(CMP_BLOCK * HEAD_DIM, -1), w_cmp_v2)
    o_nsa = _nsa(qT, kc3, vcT, k_aug, v_aug, bgT, tq=tq)

    x1 = _merge(x, g_ssm, o_nsa, scale1, shift1, gate1, w_mg,
                w_glu_lin.astype(BF16), w_glu_gate.astype(BF16), w_nsa_proj.astype(BF16), w_out.astype(BF16),
                ln1_g.reshape(1, D_MODEL), ln1_b.reshape(1, D_MODEL), tm=tm_dense)
    return _ffn(x1, scale2, shift2, gate2, w_ffn_gate.astype(BF16), w_ffn_up.astype(BF16),
                w_ffn_down.astype(BF16), ln2_g.reshape(1, D_MODEL), ln2_b.reshape(1, D_MODEL), tm=tm_dense, n_chunks=2)


def kernel(x, c, w_ada, b_ada, w_in, ssm_a_re, ssm_a_im, ssm_log_dt, ssm_b_re, ssm_b_im, ssm_c_re, ssm_c_im,
           ssm_d, w_glu_lin, w_glu_gate, cmp_pe_k, cmp_pe_v, w_cmp_k1, w_cmp_k2, w_cmp_v1, w_cmp_v2,
           w_nsa_proj, w_out, ln1_g, ln1_b, w_ffn_gate, w_ffn_up, w_ffn_down, ln2_g, ln2_b):
    for l in range(w_ada.shape[0]):
        x = _layer(x, c, w_ada[l], b_ada[l], w_in[l], ssm_a_re[l], ssm_a_im[l], ssm_log_dt[l],
                   ssm_b_re[l], ssm_b_im[l], ssm_c_re[l], ssm_c_im[l], ssm_d[l],
                   w_glu_lin[l], w_glu_gate[l], cmp_pe_k[l], cmp_pe_v[l],
                   w_cmp_k1[l], w_cmp_k2[l], w_cmp_v1[l], w_cmp_v2[l], w_nsa_proj[l], w_out[l],
                   ln1_g[l], ln1_b[l], w_ffn_gate[l], w_ffn_up[l], w_ffn_down[l], ln2_g[l], ln2_b[l])
    return x
```

```python
import functools
import math

import jax
import jax.numpy as jnp
from jax import lax
from jax.experimental import pallas as pl
from jax.experimental.pallas import tpu as pltpu

F32 = jnp.float32
BF16 = jnp.bfloat16
HIGHEST = lax.Precision.HIGHEST

D_MODEL = 1024
SSM_WIDTH = D_MODEL // 2
SSM_GROUP_SIZE = 16
SSM_GROUPS = SSM_WIDTH // SSM_GROUP_SIZE
SSM_STATE = 64
SSM_CHUNK = 16
SSM_GROUP_TILE = 8
SSM_PREP_GROUPS = 4
SSM_BATCH_TILE = 4
SSM_STATE_ROW_PAD = 8
N_HEADS = 8
HEAD_DIM = 64
N_KV_GROUPS = 2
HEADS_PER_GROUP = N_HEADS // N_KV_GROUPS
ATTN_WIDTH = N_HEADS * HEAD_DIM
KV_WIDTH = N_KV_GROUPS * HEAD_DIM
CMP_BLOCK = 32
CMP_STRIDE = 16
CMP_HIDDEN = HEAD_DIM
SEL_BLOCK = 64
SEL_TOP_K = 8
WINDOW = 256
N_NSA_BRANCHES = 3
GATE_ROWS = 16
FFN_HIDDEN = (8 * D_MODEL + 3 * 256 - 1) // (3 * 256) * 256
DEEPNORM_ALPHA = 2.0 ** 0.25
LN_EPS = 1e-5
MASK_VALUE = -1e30

VMEM_LIMIT = 56 * 1024 * 1024
MXU_WIDTH = 256
BF16_ROWS = 16
ROW_BLOCK = 256

KEY_BLOCK = 256
SEL_SHIFT = SEL_BLOCK.bit_length() - 1
SEL_PER_KEY_BLOCK = KEY_BLOCK // SEL_BLOCK
KEY_AUG = 128
ALIBI_COL = HEAD_DIM
N_PIECES = 4
MASK_COL = HEAD_DIM + 16
QUERY_AUG = HEAD_DIM + 32
VAL_AUG = HEAD_DIM + 16
LOG2E = math.log2(math.e)
SCORES_AHEAD = 5

NAT_COLS = SSM_WIDTH + 4 * KV_WIDTH
TR_ROWS = ATTN_WIDTH + 2 * KV_WIDTH + N_KV_GROUPS * GATE_ROWS


def _bf16_pieces(value, n):
    pieces = []
    rest = value
    for _ in range(n):
        mant, expo = math.frexp(rest)
        piece = math.ldexp(round(mant * 256.0) / 256.0, expo)
        pieces.append(piece)
        rest -= piece
    return pieces


def _sigmoid(x):
    return 1.0 / (1.0 + jnp.exp(-x))


def _gelu(x):
    c = math.sqrt(2.0 / math.pi)
    return 0.5 * x * (1.0 + jnp.tanh(c * (x + 0.044715 * (x * x * x))))


def _layer_norm(y, gain, bias):
    mu = jnp.mean(y, axis=-1, keepdims=True)
    d = y - mu
    var = jnp.mean(d * d, axis=-1, keepdims=True)
    return d * lax.rsqrt(var + LN_EPS) * gain + bias


def _params(n_axes):
    return pltpu.CompilerParams(dimension_semantics=("arbitrary",) * n_axes, vmem_limit_bytes=VMEM_LIMIT)


def _ada_kernel(c_ref, w_ref, b_ref, o_ref):
    c = c_ref[...]
    a = c * _sigmoid(c)
    w = w_ref[...].astype(BF16)
    a_hi = a.astype(BF16)
    a_lo = (a - a_hi.astype(F32)).astype(BF16)
    mod = (jnp.dot(a_hi, w, preferred_element_type=F32) + jnp.dot(a_lo, w, preferred_element_type=F32)) + b_ref[...]
    for k in range(o_ref.shape[0]):
        o_ref[k] = mod[:, k * D_MODEL:(k + 1) * D_MODEL]


def _ada_mod(c, w_ada, b_ada):
    bsz = c.shape[0]
    per_step = 2
    return pl.pallas_call(
        _ada_kernel,
        grid=(6 // per_step,),
        in_specs=[
            pl.BlockSpec((bsz, D_MODEL), lambda j: (0, 0)),
            pl.BlockSpec((D_MODEL, per_step * D_MODEL), lambda j: (0, j)),
            pl.BlockSpec((1, per_step * D_MODEL), lambda j: (0, j)),
        ],
        out_specs=pl.BlockSpec((per_step, bsz, D_MODEL), lambda j: (j, 0, 0)),
        out_shape=jax.ShapeDtypeStruct((6, bsz, D_MODEL), F32),
        compiler_params=_params(1),
        name="ada_mod",
    )(c, w_ada, b_ada.reshape(1, 6 * D_MODEL))


def _in_proj_kernel(x_ref, sc_ref, sh_ref, wn_ref, wt_ref, ussm_ref, kc_ref, kp_ref, qT_ref, vp_ref, bgT_ref):
    i = pl.program_id(1)
    tm = x_ref.shape[1]
    u = (x_ref[0] * (1.0 + sc_ref[0]) + sh_ref[0]).astype(BF16)
    nat = jnp.dot(u, wn_ref[...], preferred_element_type=F32)
    ussm_ref[0] = nat[:, :SSM_WIDTH]
    for s in range(2):
        for g in range(N_KV_GROUPS):
            lo = SSM_WIDTH + s * KV_WIDTH + g * HEAD_DIM
            kc_ref[0, s, g] = nat[:, lo:lo + HEAD_DIM]
    pos = i * tm + lax.broadcasted_iota(jnp.int32, (tm, KEY_AUG), 0)
    col = lax.broadcasted_iota(jnp.int32, (tm, KEY_AUG), 1)
    blk = lax.shift_right_logical(pos, SEL_SHIFT)
    in_a = (col >= ALIBI_COL) & (col < ALIBI_COL + N_PIECES)
    in_b = (col >= ALIBI_COL + N_PIECES) & (col < ALIBI_COL + 2 * N_PIECES)
    hot = col == MASK_COL + (blk & (SEL_PER_KEY_BLOCK - 1))
    aux = jnp.where(in_a, blk * SEL_BLOCK, jnp.where(in_b, pos & (SEL_BLOCK - 1), jnp.where(hot, 1, 0))).astype(F32)
    head_lanes = col < HEAD_DIM
    for s in range(2):
        lo = SSM_WIDTH + (2 + s) * KV_WIDTH
        both = nat[:, lo:lo + KV_WIDTH]
        for g in range(N_KV_GROUPS):
            keys = both if g == 0 else pltpu.roll(both, HEAD_DIM, 1)
            kp_ref[0, s, g] = jnp.where(head_lanes, keys, aux).astype(BF16)
    tr = lax.dot_general(wt_ref[...], u, (((1,), (1,)), ((), ())), preferred_element_type=F32)
    qT_ref[0] = tr[:ATTN_WIDTH] * (HEAD_DIM ** -0.5)
    ones_rows = jnp.where(lax.broadcasted_iota(jnp.int32, (VAL_AUG - HEAD_DIM, tm), 0) == 0, 1.0, 0.0)
    for s in range(2):
        for g in range(N_KV_GROUPS):
            lo = ATTN_WIDTH + (s * N_KV_GROUPS + g) * HEAD_DIM
            v_aug = jnp.concatenate([tr[lo:lo + HEAD_DIM], ones_rows], axis=0).astype(BF16)
            for c in range(tm // KEY_BLOCK):
                vp_ref[0, s, g, c] = v_aug[:, c * KEY_BLOCK:(c + 1) * KEY_BLOCK]
    bgT_ref[0] = _sigmoid(tr[ATTN_WIDTH + 2 * KV_WIDTH:])


def _in_proj(x, scale1, shift1, w_nat, w_tr, *, tm):
    bsz, seq, _ = x.shape
    n_t = seq // tm
    n_kb = seq // KEY_BLOCK
    out_shapes = (
        jax.ShapeDtypeStruct((bsz, seq, SSM_WIDTH), F32),
        jax.ShapeDtypeStruct((bsz, 2, N_KV_GROUPS, seq, HEAD_DIM), F32),
        jax.ShapeDtypeStruct((bsz, 2, N_KV_GROUPS, seq, KEY_AUG), BF16),
        jax.ShapeDtypeStruct((bsz, ATTN_WIDTH, seq), F32),
        jax.ShapeDtypeStruct((bsz, 2, N_KV_GROUPS, n_kb, VAL_AUG, KEY_BLOCK), BF16),
        jax.ShapeDtypeStruct((bsz, N_KV_GROUPS * GATE_ROWS, seq), F32),
    )
    return pl.pallas_call(
        _in_proj_kernel,
        grid=(bsz, n_t),
        in_specs=[
            pl.BlockSpec((1, tm, D_MODEL), lambda b, i: (b, i, 0)),
            pl.BlockSpec((1, 1, D_MODEL), lambda b, i: (b, 0, 0)),
            pl.BlockSpec((1, 1, D_MODEL), lambda b, i: (b, 0, 0)),
            pl.BlockSpec((D_MODEL, NAT_COLS), lambda b, i: (0, 0)),
            pl.BlockSpec((TR_ROWS, D_MODEL), lambda b, i: (0, 0)),
        ],
        out_specs=(
            pl.BlockSpec((1, tm, SSM_WIDTH), lambda b, i: (b, i, 0)),
            pl.BlockSpec((1, 2, N_KV_GROUPS, tm, HEAD_DIM), lambda b, i: (b, 0, 0, i, 0)),
            pl.BlockSpec((1, 2, N_KV_GROUPS, tm, KEY_AUG), lambda b, i: (b, 0, 0, i, 0)),
            pl.BlockSpec((1, ATTN_WIDTH, tm), lambda b, i: (b, 0, i)),
            pl.BlockSpec((1, 2, N_KV_GROUPS, tm // KEY_BLOCK, VAL_AUG, KEY_BLOCK), lambda b, i: (b, 0, 0, i, 0, 0)),
            pl.BlockSpec((1, N_KV_GROUPS * GATE_ROWS, tm), lambda b, i: (b, 0, i)),
        ),
        out_shape=out_shapes,
        compiler_params=_params(2),
        name="in_proj",
    )(x, scale1, shift1, w_nat, w_tr)


def _split_w_in(w_in):
    o_q = SSM_WIDTH
    o_kv = o_q + ATTN_WIDTH
    o_bg = o_kv + 6 * KV_WIDTH
    o_mg = o_bg + N_NSA_BRANCHES * N_HEADS
    kv = [w_in[:, o_kv + s * KV_WIDTH:o_kv + (s + 1) * KV_WIDTH] for s in range(6)]
    w_nat = jnp.concatenate([w_in[:, :o_q], kv[0], kv[1], kv[2], kv[4]], axis=1)
    per_group = HEADS_PER_GROUP * N_NSA_BRANCHES
    bg = w_in[:, o_bg:o_mg].reshape(D_MODEL, N_KV_GROUPS, per_group)
    bg = jnp.pad(bg, ((0, 0), (0, 0), (0, GATE_ROWS - per_group))).reshape(D_MODEL, N_KV_GROUPS * GATE_ROWS)
    w_tr = jnp.concatenate([w_in[:, o_q:o_kv], kv[3], kv[5], bg], axis=1).T
    return w_nat.astype(BF16), w_tr.astype(BF16), w_in[:, o_mg:].astype(BF16)


def _ssm_prep_kernel(*refs):
    for k in range(refs[0].shape[0]):
        _ssm_prep_group(*[ref.at[pl.ds(k, 1)] for ref in refs])


def _ssm_prep_group(a_ref, logdt_ref, bT_ref, c_ref, d_ref, mT_ref, pT_ref, qT_ref, a16_ref):
    t_n = SSM_CHUNK
    c_n = SSM_GROUP_SIZE
    n_p = SSM_STATE
    rows = t_n * c_n
    dt = jnp.exp(logdt_ref[0])
    a_re = a_ref[0, 0]
    a_im = a_ref[0, 1]
    decay = jnp.exp(a_re * dt)
    ab_re = decay * jnp.cos(a_im * dt)
    ab_im = decay * jnp.sin(a_im * dt)
    n_re = ab_re - 1.0
    denom = a_re * a_re + a_im * a_im
    f_re = (n_re * a_re + ab_im * a_im) / denom
    f_im = (ab_im * a_re - n_re * a_im) / denom
    powers = [(jnp.ones_like(ab_re), jnp.zeros_like(ab_re))]
    for _ in range(t_n):
        p_re, p_im = powers[-1]
        powers.append((p_re * ab_re - p_im * ab_im, p_re * ab_im + p_im * ab_re))

    def per_token(first, step):
        picks = [powers[first + step * k] for k in range(t_n)]
        return (jnp.concatenate([jnp.broadcast_to(p[0], (c_n, 2 * n_p)) for p in picks], axis=0),
                jnp.concatenate([jnp.broadcast_to(p[1], (c_n, 2 * n_p)) for p in picks], axis=0))

    def cmul(x_re, x_im, y_re, y_im):
        return x_re * y_re - x_im * y_im, x_re * y_im + x_im * y_re

    bb_re, bb_im = cmul(f_re, f_im, bT_ref[0, 0], bT_ref[0, 1])
    bt_re = jnp.concatenate([bb_re] * t_n, axis=0)
    bt_im = jnp.concatenate([bb_im] * t_n, axis=0)
    ct_re = jnp.concatenate([c_ref[0, 0]] * t_n, axis=0)
    ct_im = jnp.concatenate([c_ref[0, 1]] * t_n, axis=0)

    p_re, p_im = cmul(*per_token(t_n - 1, -1), bt_re, bt_im)
    low_half = lax.broadcasted_iota(jnp.int32, (1, 2 * n_p), 1) < n_p
    pT_ref[0] = jnp.where(low_half, p_re, p_im).T.astype(BF16)

    q_re, q_im = cmul(ct_re, ct_im, *per_token(1, 1))
    qT_ref[0] = jnp.where(low_half, q_re, -q_im).astype(BF16)
    a16_ref[0] = jnp.concatenate([powers[t_n][0][:, :n_p], powers[t_n][1][:, :n_p]], axis=0)

    k_re, k_im = cmul(ct_re, ct_im, *per_token(0, 1))
    nt = (((1,), (1,)), ((), ()))
    strips = (lax.dot_general(_split3(k_re[:, :n_p]), _stack3(bt_re[:, :n_p], axis=1), nt, preferred_element_type=F32)
              - lax.dot_general(_split3(k_im[:, :n_p]), _stack3(bt_im[:, :n_p], axis=1), nt,
                                preferred_element_type=F32))
    lane = lax.broadcasted_iota(jnp.int32, (1, rows), 1)
    s_lane = lax.shift_right_logical(lane, c_n.bit_length() - 1)
    ci_lane = lane & (c_n - 1)
    row = lax.broadcasted_iota(jnp.int32, (rows, 1), 0)
    t_row = lax.shift_right_logical(row, c_n.bit_length() - 1)
    co_row = row & (c_n - 1)
    blocks = []
    for t in range(t_n):
        acc = jnp.zeros((c_n, rows), F32)
        for lag in range(t + 1):
            acc = jnp.where(s_lane == t - lag, strips[lag * c_n:(lag + 1) * c_n, :], acc)
        blocks.append(acc)
    d_rows = jnp.concatenate([d_ref[0]] * t_n, axis=0)
    skip = jnp.where((t_row == s_lane) & (co_row == ci_lane), d_rows, 0.0)
    mT_ref[0] = (jnp.concatenate(blocks, axis=0) + skip).astype(BF16)


def _ssm_matrices(a_re, a_im, log_dt, b_re, b_im, c_re, c_im, d_skip):
    n_g, n_p = a_re.shape
    c_n = SSM_GROUP_SIZE
    rows = SSM_CHUNK * c_n
    twice = lambda x: jnp.concatenate([x, x], axis=-1).astype(F32)
    a = twice(jnp.stack([a_re, a_im], axis=1)).reshape(n_g, 2, 1, 2 * n_p)
    bT = twice(jnp.swapaxes(jnp.stack([b_re, b_im], axis=1), 2, 3))
    c = twice(jnp.stack([c_re, c_im], axis=1))
    per_step = SSM_PREP_GROUPS
    spec = lambda shape: pl.BlockSpec((per_step,) + shape, lambda g: (g,) + (0,) * len(shape))
    mT, pT, qT, a16 = pl.pallas_call(
        _ssm_prep_kernel,
        grid=(n_g // per_step,),
        in_specs=[spec((2, 1, 2 * n_p)), spec((1, 1)), spec((2, c_n, 2 * n_p)), spec((2, c_n, 2 * n_p)),
                  spec((c_n, 1))],
        out_specs=(spec((rows, rows)), spec((2 * n_p, rows)), spec((rows, 2 * n_p)), spec((2, n_p))),
        out_shape=(
            jax.ShapeDtypeStruct((n_g, rows, rows), BF16),
            jax.ShapeDtypeStruct((n_g, 2 * n_p, rows), BF16),
            jax.ShapeDtypeStruct((n_g, rows, 2 * n_p), BF16),
            jax.ShapeDtypeStruct((n_g, 2, n_p), F32),
        ),
        compiler_params=_params(1),
        name="ssm_prep",
    )(a, log_dt.astype(F32).reshape(n_g, 1, 1), bT, c, d_skip.astype(F32).reshape(n_g, c_n, 1))
    return mT, pT.reshape(n_g, 2, n_p, rows), qT, a16


def _ssm_kernel(u_ref, mT_ref, pT_ref, qT_ref, a16_ref, o_ref, ut_scr, yt_scr, sre, sim, *, n_chunks):
    n_g = SSM_GROUP_TILE
    n_b = u_ref.shape[0]
    t_n = SSM_CHUNK
    c_n = SSM_GROUP_SIZE
    lanes = [(e, g) for e in range(n_b) for g in range(n_g)]

    for e in range(n_b):
        for s in range(t_n):
            x_s = u_ref[e, pl.ds(s, n_chunks, stride=t_n), :]
            ut_scr[e * n_g:(e + 1) * n_g, s * c_n:(s + 1) * c_n, :] = (
                x_s.T.reshape(n_g, c_n, n_chunks).astype(BF16))

    pitch = n_chunks + SSM_STATE_ROW_PAD

    def lane_rows(k):
        return pl.ds(k * pitch, n_chunks)

    for k, (e, g) in enumerate(lanes):
        ut = ut_scr[k]
        sre[lane_rows(k), :] = jnp.dot(pT_ref[g, 0], ut, preferred_element_type=F32).T
        sim[lane_rows(k), :] = jnp.dot(pT_ref[g, 1], ut, preferred_element_type=F32).T

    a_r = jnp.concatenate([a16_ref[pl.ds(0, n_g, stride=2), :]] * n_b, axis=0)
    a_i = jnp.concatenate([a16_ref[pl.ds(1, n_g, stride=2), :]] * n_b, axis=0)

    def carry_states(c, h):
        h_r, h_i = h
        rows = pl.ds(c, len(lanes), stride=pitch)
        s_r = sre[rows, :]
        s_i = sim[rows, :]
        sre[rows, :] = h_r
        sim[rows, :] = h_i
        return a_r * h_r - a_i * h_i + s_r, a_r * h_i + a_i * h_r + s_i

    zero = jnp.zeros((len(lanes), SSM_STATE), F32)
    lax.fori_loop(0, n_chunks, carry_states, (zero, zero), unroll=4)

    nt = (((1,), (1,)), ((), ()))
    for k, (e, g) in enumerate(lanes):
        yt = jnp.dot(mT_ref[g], ut_scr[k], preferred_element_type=F32)
        entering = jnp.concatenate([sre[lane_rows(k), :], sim[lane_rows(k), :]], axis=1).astype(BF16)
        yt = yt + lax.dot_general(qT_ref[g], entering, nt, preferred_element_type=F32)
        yt_scr[k] = _gelu(yt)

    for e in range(n_b):
        for t in range(t_n):
            z = yt_scr[e * n_g:(e + 1) * n_g, t * c_n:(t + 1) * c_n, :].reshape(n_g * c_n, n_chunks)
            o_ref[e, pl.ds(t, n_chunks, stride=t_n), :] = z.T


def _ssm(u_ssm, mats):
    bsz, seq, width = u_ssm.shape
    n_chunks = seq // SSM_CHUNK
    mT, pT, qT, a16 = mats
    gt = SSM_GROUP_TILE
    n_tiles = SSM_GROUPS // gt
    rows = SSM_CHUNK * SSM_GROUP_SIZE
    a16 = a16.reshape(2 * SSM_GROUPS, SSM_STATE)
    bt = SSM_BATCH_TILE if bsz % SSM_BATCH_TILE == 0 else 1
    n_lanes = bt * gt
    per_tile = lambda a: pl.BlockSpec((gt,) + a.shape[1:], lambda t, b: (t,) + (0,) * (a.ndim - 1))
    return pl.pallas_call(
        functools.partial(_ssm_kernel, n_chunks=n_chunks),
        grid=(n_tiles, bsz // bt),
        in_specs=[
            pl.BlockSpec((bt, seq, gt * SSM_GROUP_SIZE), lambda t, b: (b, 0, t)),
            per_tile(mT), per_tile(pT), per_tile(qT),
            pl.BlockSpec((2 * gt, SSM_STATE), lambda t, b: (t, 0)),
        ],
        out_specs=pl.BlockSpec((bt, seq, gt * SSM_GROUP_SIZE), lambda t, b: (b, 0, t)),
        out_shape=jax.ShapeDtypeStruct(u_ssm.shape, F32),
        scratch_shapes=[
            pltpu.VMEM((n_lanes, rows, n_chunks), BF16),
            pltpu.VMEM((n_lanes, rows, n_chunks), F32),
            pltpu.VMEM((n_lanes * (n_chunks + SSM_STATE_ROW_PAD), SSM_STATE), F32),
            pltpu.VMEM((n_lanes * (n_chunks + SSM_STATE_ROW_PAD), SSM_STATE), F32),
        ],
        compiler_params=_params(2),
        name="ssm",
    )(u_ssm, mT, pT, qT, a16)


def _split3(x):
    hi = x.astype(BF16)
    lo = (x - hi.astype(F32)).astype(BF16)
    return jnp.concatenate([hi, lo, hi], axis=-1)


def _stack3(w, axis=0):
    hi = w.astype(BF16)
    lo = (w - hi.astype(F32)).astype(BF16)
    return jnp.concatenate([hi, hi, lo], axis=axis)


def _compress_kernel(kx_ref, vx_ref, pek_ref, pev_ref, wk1_ref, wk2_ref, wv1_ref, wv2_ref, kc_ref, vcT_ref):
    def mlp(x_ref, g, pe_ref, w1_ref, w2_ref):
        n = x_ref.shape[3] // CMP_STRIDE
        x = jnp.concatenate([x_ref[0, 0, g, pl.ds(l, n, stride=CMP_STRIDE), :] for l in range(CMP_STRIDE)], axis=1)
        both = jnp.dot(_split3(x), w1_ref[...], preferred_element_type=F32)
        pe_both = jnp.dot(_split3(pe_ref[...]), w1_ref[...], preferred_element_type=F32)
        bias = pe_both[0:1, :CMP_HIDDEN] + pe_both[1:2, CMP_HIDDEN:]
        h = _gelu(both[:, :CMP_HIDDEN] + pltpu.roll(both[:, CMP_HIDDEN:], n - 1, 0) + bias)
        return jnp.dot(_split3(h), w2_ref[...], preferred_element_type=F32)

    for g in range(kx_ref.shape[2]):
        kc = mlp(kx_ref, g, pek_ref, wk1_ref, wk2_ref)
        kc_hi = kc.astype(BF16)
        kc_lo = (kc - kc_hi.astype(F32)).astype(BF16)
        kc_ref[0, g] = jnp.concatenate([kc_hi, kc_hi, kc_lo, jnp.zeros_like(kc_hi)], axis=1)
        vcT_ref[0, g] = mlp(vx_ref, g, pev_ref, wv1_ref, wv2_ref).T.astype(BF16)


def _compress(kv, pe_k, pe_v, wk1, wk2, wv1, wv2):
    bsz, _, n_g, seq, width = kv.shape
    n_chunks = seq // CMP_STRIDE
    half = CMP_STRIDE * HEAD_DIM
    side_by_side = lambda w1: _stack3(jnp.concatenate([w1[:half], w1[half:]], axis=1))
    two_rows = lambda pe: jnp.pad(pe.reshape(2, half), ((0, BF16_ROWS - 2), (0, 0)))
    pe_k, pe_v = two_rows(pe_k), two_rows(pe_v)
    wk1, wv1, wk2, wv2 = side_by_side(wk1), side_by_side(wv1), _stack3(wk2), _stack3(wv2)
    blk = lambda s: pl.BlockSpec((1, 1, n_g, seq, width), lambda b: (b, s, 0, 0, 0))
    full = lambda a: pl.BlockSpec(a.shape, lambda b: (0, 0))
    return pl.pallas_call(
        _compress_kernel,
        grid=(bsz,),
        in_specs=[blk(0), blk(1), full(pe_k), full(pe_v), full(wk1), full(wk2), full(wv1), full(wv2)],
        out_specs=(
            pl.BlockSpec((1, n_g, n_chunks, 4 * HEAD_DIM), lambda b: (b, 0, 0, 0)),
            pl.BlockSpec((1, n_g, HEAD_DIM, n_chunks), lambda b: (b, 0, 0, 0)),
        ),
        out_shape=(
            jax.ShapeDtypeStruct((bsz, n_g, n_chunks, 4 * HEAD_DIM), BF16),
            jax.ShapeDtypeStruct((bsz, n_g, HEAD_DIM, n_chunks), BF16),
        ),
        compiler_params=_params(1),
        name="compress",
    )(kv, kv, pe_k, pe_v, wk1, wk2, wv1, wv2)


def _pipeline(units):
    issue = lambda unit: None if unit[0] is None else unit[0]()
    pending = [issue(unit) for unit in units[:SCORES_AHEAD]]
    for n, unit in enumerate(units):
        if n + SCORES_AHEAD < len(units):
            pending.append(issue(units[n + SCORES_AHEAD]))
        unit[1](pending.pop(0))


def _nsa_kernel(qT_ref, kc_ref, vcT_ref, ks_ref, vs_ref, kw_ref, vw_ref, bg_ref, o_ref,
                negm_scr, qa_scr, m_scr, acc_scr, out_scr, *smem, tq, n_cmp, n_sel):
    i = pl.program_id(1)
    gw = HEADS_PER_GROUP * HEAD_DIM
    groups = [
        _nsa_group(g, i, qT_ref.at[:, g * gw:(g + 1) * gw], kc_ref.at[:, g:g + 1], vcT_ref.at[:, g:g + 1],
                   ks_ref.at[:, :, g:g + 1], vs_ref.at[:, :, g:g + 1], kw_ref.at[:, :, g:g + 1],
                   vw_ref.at[:, :, g:g + 1], bg_ref.at[:, g:g + 1], o_ref.at[:, :, g * gw:(g + 1) * gw],
                   negm_scr.at[g], qa_scr.at[g], m_scr.at[g], acc_scr.at[g], out_scr.at[g],
                   smem[2 * g], smem[2 * g + 1], tq=tq, n_cmp=n_cmp, n_sel=n_sel)
        for g in range(N_KV_GROUPS)]
    stages = [[unit for stage_units, _ in groups for unit in stage_units[stage]]
              for stage in range(3)]
    _pipeline(stages[0] + stages[1] + stages[2])
    for _, finish_group in groups:
        finish_group()


def _nsa_group(g, i, qT_ref, kc_ref, vcT_ref, ks_ref, vs_ref, kw_ref, vw_ref, bg_ref, o_ref,
               negm_scr, qa_scr, m_scr, acc_scr, out_scr, used_smem, todo_smem, *, tq, n_cmp, n_sel):
    t_row = i * tq + lax.broadcasted_iota(jnp.int32, (1, tq), 1)
    slopes = [2.0 ** -(g * HEADS_PER_GROUP + hh + 1) for hh in range(HEADS_PER_GROUP)]

    def gate(hh, branch):
        r = hh * N_NSA_BRANCHES + branch
        return bg_ref[0, 0, r:r + 1, :]

    def head_rows(hh):
        return slice(hh * HEAD_DIM, (hh + 1) * HEAD_DIM)

    def head_lanes(hh):
        return slice(hh * tq, (hh + 1) * tq)

    kc3 = kc_ref[0, 0]
    vcT = vcT_ref[0, 0]
    cmp_end = lax.broadcasted_iota(jnp.int32, (n_cmp, 1), 0) * CMP_STRIDE + (CMP_BLOCK - 1)
    dist_c = (t_row - cmp_end).astype(F32)
    valid_c = dist_c >= 0.0
    piece = lax.broadcasted_iota(jnp.int32, (16, 1), 0)
    log2e_pieces = _bf16_pieces(LOG2E, N_PIECES)
    log2e_col = jnp.zeros((16, 1), F32)
    for k, value in enumerate(log2e_pieces):
        log2e_col = jnp.where((piece == k) | (piece == N_PIECES + k), value, log2e_col)
    for hh in range(HEADS_PER_GROUP):
        alibi = jnp.broadcast_to(log2e_col * slopes[hh], (16, tq)).astype(BF16)
        q_l2 = (qT_ref[0, head_rows(hh), :] * LOG2E).astype(BF16)
        qa_scr[:, head_lanes(hh)] = jnp.concatenate([q_l2, alibi], axis=0)

    def cmp_scores(hh):
        q_h = qT_ref[0, head_rows(hh), :]
        q_hi = q_h.astype(BF16)
        q_lo = (q_h - q_hi.astype(F32)).astype(BF16)
        q3 = jnp.concatenate([q_hi, q_lo, q_hi, jnp.zeros_like(q_hi)], axis=0)
        return jnp.dot(kc3, q3, preferred_element_type=F32)

    def cmp_attend(p_heads, hh, s):
        s = jnp.where(valid_c, s - slopes[hh] * dist_c, MASK_VALUE)
        m = jnp.max(s, axis=0, keepdims=True)
        p = jnp.where(valid_c, jnp.exp(s - m), 0.0)
        l = jnp.sum(p, axis=0, keepdims=True)
        p = p / jnp.where(l > 0.0, l, 1.0)
        o_c = jnp.dot(vcT, p.astype(BF16), preferred_element_type=F32)
        out_scr[head_rows(hh), :] = gate(hh, 0) * o_c
        p_heads.append(p)

    def select_blocks(p_heads, _):
        p_sum = sum(p_heads[1:], p_heads[0])
        blk = lax.broadcasted_iota(jnp.int32, (n_sel, n_cmp), 0)
        cmp_ix = lax.broadcasted_iota(jnp.int32, (n_sel, n_cmp), 1)
        ratio = SEL_BLOCK // CMP_STRIDE
        extra = CMP_BLOCK // CMP_STRIDE - 1
        overlap_t = jnp.where((cmp_ix >= ratio * blk - extra) & (cmp_ix <= ratio * blk + ratio - 1),
                              1.0, 0.0).astype(BF16)
        imp = jnp.zeros((n_sel, tq), F32)
        rest = p_sum
        for _ in range(3):
            part = rest.astype(BF16)
            imp = imp + jnp.dot(overlap_t, part, preferred_element_type=F32)
            rest = rest - part.astype(F32)

        j_col = lax.broadcasted_iota(jnp.int32, (n_sel, 1), 0)
        forced = (j_col == 0) | (j_col == lax.shift_right_logical(t_row, SEL_SHIFT))
        future = j_col * SEL_BLOCK > t_row
        score = jnp.where(forced, jnp.inf, jnp.where(future, -jnp.inf, imp))
        rank = jnp.zeros((n_sel, tq), F32)
        for jp in range(n_sel):
            other = score[jp:jp + 1, :]
            ahead = (other > score) | ((other == score) & (j_col > jp))
            rank = rank + jnp.where(ahead, 1.0, 0.0)
        neg = jnp.where(rank < float(min(SEL_TOP_K, n_sel)), 0.0, MASK_VALUE)
        pad_rows = jnp.zeros((BF16_ROWS - SEL_PER_KEY_BLOCK, tq), F32)
        for c in range(n_sel // SEL_PER_KEY_BLOCK):
            rows = neg[c * SEL_PER_KEY_BLOCK:(c + 1) * SEL_PER_KEY_BLOCK, :]
            negm_scr[c] = jnp.concatenate([rows, pad_rows], axis=0).astype(BF16)
            used_smem[c] = (jnp.max(rows) > 0.5 * MASK_VALUE).astype(jnp.int32)

    rel = (lax.broadcasted_iota(jnp.int32, (KEY_BLOCK, tq), 1)
           - lax.broadcasted_iota(jnp.int32, (KEY_BLOCK, tq), 0))
    zero_tail = jnp.zeros((KEY_AUG - QUERY_AUG, tq), BF16)
    no_mask = jnp.zeros((BF16_ROWS, tq), BF16)
    all_masked = jnp.full((BF16_ROWS, tq), MASK_VALUE, BF16)

    sel_even, sel_odd, win_prev, win_diag = range(4)
    m_scr[...] = jnp.full(m_scr.shape, MASK_VALUE, F32)
    acc_scr[...] = jnp.zeros(acc_scr.shape, F32)

    def scores(job, hh):
        k_ref, _, kb, enabled, _, _ = job
        k0 = pl.multiple_of(kb * KEY_BLOCK, KEY_BLOCK)
        k_aug = k_ref[0, 0, 0, pl.ds(k0, KEY_BLOCK), :]
        if enabled is None:
            mask_tile = no_mask
        elif enabled is True:
            mask_tile = negm_scr[kb]
        else:
            mask_tile = jnp.where(enabled, negm_scr[kb], all_masked)
        q_aug = jnp.concatenate([qa_scr[:, head_lanes(hh)], mask_tile, zero_tail], axis=0)
        return jnp.dot(k_aug, q_aug, preferred_element_type=F32)

    def accumulate(job, hh, s):
        _, v_ref, kb, _, valid, slot = job
        cols = head_lanes(hh)
        if valid is not None:
            s = jnp.where(valid, s, MASK_VALUE)
        m_old = m_scr[slot, :, cols]
        m_new = jnp.maximum(m_old, jnp.max(s, axis=0, keepdims=True))
        p = jnp.exp2(s - jnp.maximum(m_new, 0.1 * MASK_VALUE))
        alpha = jnp.exp2(m_old - m_new)
        pv = jnp.dot(v_ref[0, 0, 0, kb], p.astype(BF16), preferred_element_type=F32)
        acc_scr[slot, :, cols] = alpha * acc_scr[slot, :, cols] + pv
        m_scr[slot, :, cols] = m_new

    def attention_units(jobs):
        return [(functools.partial(scores, job, hh), functools.partial(accumulate, job, hh))
                for job in jobs for hh in range(HEADS_PER_GROUP)]

    def finish(branch, slot_a, slot_b):
        m_a = m_scr[slot_a]
        m_b = m_scr[slot_b]
        m = jnp.maximum(m_a, m_b)
        w_a = jnp.exp2(m_a - m)
        w_b = jnp.exp2(m_b - m)
        for hh in range(HEADS_PER_GROUP):
            cols = head_lanes(hh)
            acc = acc_scr[slot_a, :, cols] * w_a[:, cols] + acc_scr[slot_b, :, cols] * w_b[:, cols]
            l = acc[HEAD_DIM:HEAD_DIM + 1, :]
            o_b = acc[:HEAD_DIM, :] / jnp.where(l > 0.0, l, 1.0)
            out_scr[head_rows(hh), :] = out_scr[head_rows(hh), :] + gate(hh, branch) * o_b

    prev = jnp.maximum(i - 1, 0)
    causal = rel >= 0
    off_without_prev = jnp.where(i > 0, 0, WINDOW + KEY_BLOCK)
    dist_prev = rel + KEY_BLOCK + off_without_prev
    p_heads = []
    stage_units = (
        [(functools.partial(cmp_scores, hh), functools.partial(cmp_attend, p_heads, hh))
         for hh in range(HEADS_PER_GROUP)] + [(None, functools.partial(select_blocks, p_heads))],
        attention_units([(kw_ref, vw_ref, prev, None, dist_prev < WINDOW, win_prev),
                         (kw_ref, vw_ref, i, None, causal, win_diag)]),
        attention_units([(ks_ref, vs_ref, prev, i > 0, None, sel_even),
                         (ks_ref, vs_ref, i, True, causal, sel_odd)]))

    def finish_group():
        n_todo = jnp.int32(0)
        for c in range(used_smem.shape[0]):
            take = jnp.logical_and(c < i - 1, used_smem[c] > 0)

            @pl.when(take)
            def _(c=c, slot=n_todo):
                todo_smem[slot] = c

            n_todo = n_todo + take.astype(jnp.int32)
        todo_smem[n_todo] = 0

        def past_pair(j, carry):
            _pipeline(attention_units([(ks_ref, vs_ref, todo_smem[2 * j], True, None, sel_even),
                                       (ks_ref, vs_ref, todo_smem[2 * j + 1], 2 * j + 1 < n_todo, None, sel_odd)]))
            return carry

        lax.fori_loop(0, lax.shift_right_logical(n_todo + 1, 1), past_pair, 0)
        finish(1, sel_even, sel_odd)
        finish(2, win_prev, win_diag)
        o_ref[0] = out_scr[...].T

    return stage_units, finish_group


def _nsa(qT, kc3, vcT, k_aug, v_aug, bgT, *, tq):
    bsz, _, seq = qT.shape
    assert tq == KEY_BLOCK == WINDOW and seq % tq == 0
    n_cmp = kc3.shape[2]
    n_sel = seq // SEL_BLOCK
    n_kb = seq // KEY_BLOCK
    gw = HEADS_PER_GROUP * HEAD_DIM
    bg4 = bgT.reshape(bsz, N_KV_GROUPS, GATE_ROWS, seq)
    n_g = N_KV_GROUPS
    k_spec = lambda s: pl.BlockSpec((1, 1, n_g, seq, KEY_AUG), lambda b, i: (b, s, 0, 0, 0))
    v_spec = lambda s: pl.BlockSpec((1, 1, n_g, n_kb, VAL_AUG, KEY_BLOCK), lambda b, i: (b, s, 0, 0, 0, 0))
    smem = []
    for _ in range(n_g):
        smem += [pltpu.SMEM((n_kb,), jnp.int32),
                 pltpu.SMEM((n_kb + 1,), jnp.int32)]
    return pl.pallas_call(
        functools.partial(_nsa_kernel, tq=tq, n_cmp=n_cmp, n_sel=n_sel),
        grid=(bsz, seq // tq),
        in_specs=[
            pl.BlockSpec((1, n_g * gw, tq), lambda b, i: (b, 0, i)),
            pl.BlockSpec((1, n_g, n_cmp, 4 * HEAD_DIM), lambda b, i: (b, 0, 0, 0)),
            pl.BlockSpec((1, n_g, HEAD_DIM, n_cmp), lambda b, i: (b, 0, 0, 0)),
            k_spec(0), v_spec(0),
            k_spec(1), v_spec(1),
            pl.BlockSpec((1, n_g, GATE_ROWS, tq), lambda b, i: (b, 0, 0, i)),
        ],
        out_specs=pl.BlockSpec((1, tq, n_g * gw), lambda b, i: (b, i, 0)),
        out_shape=jax.ShapeDtypeStruct((bsz, seq, ATTN_WIDTH), F32),
        scratch_shapes=[
            pltpu.VMEM((n_g, n_kb, 16, tq), BF16),
            pltpu.VMEM((n_g, HEAD_DIM + 16, HEADS_PER_GROUP * tq), BF16),
            pltpu.VMEM((n_g, 4, 1, HEADS_PER_GROUP * tq), F32),
            pltpu.VMEM((n_g, 4, VAL_AUG, HEADS_PER_GROUP * tq), F32),
            pltpu.VMEM((n_g, gw, tq), F32),
        ] + smem,
        compiler_params=_params(2),
        name="nsa",
    )(qT, kc3, vcT, k_aug, v_aug, k_aug, v_aug, bg4)


def _merge_kernel(x_ref, g_ref, o_ref_in, sc_ref, sh_ref, gt_ref, wm_ref, wl_ref, wg_ref, wn_ref, wo_ref,
                  lg_ref, lb_ref, y_ref):
    tm = x_ref.shape[1]
    row_blocks = [slice(r, r + ROW_BLOCK) for r in range(0, tm, ROW_BLOCK)]
    halves = [(0, D_MODEL // 2), (D_MODEL // 2, D_MODEL)]
    dot = functools.partial(jnp.dot, preferred_element_type=F32)

    def first_stage(rows):
        g = g_ref[0, rows, :].astype(BF16)
        o = o_ref_in[0, rows, :].astype(BF16)
        branch = [(dot(g, wl_ref[:, lo:hi]), dot(g, wg_ref[:, lo:hi]), dot(o, wn_ref[:, lo:hi])) for lo, hi in halves]
        x = x_ref[0, rows, :]
        u = (x * (1.0 + sc_ref[0]) + sh_ref[0]).astype(BF16)
        gates = [(dot(u, wm_ref[:, lo:hi]), dot(u, wm_ref[:, D_MODEL + lo:D_MODEL + hi])) for lo, hi in halves]
        return x, branch, gates

    def second_stage(rows, x, branch, gates):
        mix = None
        for (lo, hi), (lin, gate, z_nsa), (m_ssm, m_nsa) in zip(halves, branch, gates):
            merged = _sigmoid(m_ssm) * (lin * _sigmoid(gate)) + _sigmoid(m_nsa) * z_nsa
            part = dot(merged.astype(BF16), wo_ref[lo:hi, :])
            mix = part if mix is None else mix + part
        y = DEEPNORM_ALPHA * x + (1.0 + gt_ref[0]) * mix
        y_ref[0, rows, :] = _layer_norm(y, lg_ref[...], lb_ref[...])

    staged = None
    for rows in row_blocks + [None]:
        upcoming = first_stage(rows) if rows is not None else None
        if staged is not None:
            second_stage(*staged)
        staged = (rows,) + upcoming if rows is not None else None


def _merge(x, g_ssm, o_nsa, scale1, shift1, gate1, w_m, w_lin, w_gate, w_nsa, w_out, ln_g, ln_b, *, tm):
    bsz, seq, _ = x.shape
    row = pl.BlockSpec((1, 1, D_MODEL), lambda b, i: (b, 0, 0))
    full = lambda a: pl.BlockSpec(a.shape, lambda b, i: (0, 0), pipeline_mode=pl.Buffered(1))
    return pl.pallas_call(
        _merge_kernel,
        grid=(bsz, seq // tm),
        in_specs=[
            pl.BlockSpec((1, tm, D_MODEL), lambda b, i: (b, i, 0)),
            pl.BlockSpec((1, tm, SSM_WIDTH), lambda b, i: (b, i, 0)),
            pl.BlockSpec((1, tm, ATTN_WIDTH), lambda b, i: (b, i, 0)),
            row, row, row,
            full(w_m), full(w_lin), full(w_gate), full(w_nsa), full(w_out), full(ln_g), full(ln_b),
        ],
        out_specs=pl.BlockSpec((1, tm, D_MODEL), lambda b, i: (b, i, 0)),
        out_shape=jax.ShapeDtypeStruct(x.shape, F32),
        compiler_params=_params(2),
        name="merge",
    )(x, g_ssm, o_nsa, scale1, shift1, gate1, w_m, w_lin, w_gate, w_nsa, w_out, ln_g, ln_b)


def _ffn_kernel(x_ref, sc_ref, sh_ref, gt_ref, wg_ref, wu_ref, wd_ref, lg_ref, lb_ref, y_ref, *, n_chunks):
    tm = x_ref.shape[1]
    row_blocks = [slice(r, r + ROW_BLOCK) for r in range(0, tm, ROW_BLOCK)]
    tiles = FFN_HIDDEN // MXU_WIDTH
    edges = [MXU_WIDTH * ((tiles * c + n_chunks - 1) // n_chunks) for c in range(n_chunks + 1)]
    chunks = [slice(lo, hi) for lo, hi in zip(edges[:-1], edges[1:])]
    dot = functools.partial(jnp.dot, preferred_element_type=F32)

    def gate_up(rows):
        x = x_ref[0, rows, :]
        u = (x * (1.0 + sc_ref[0]) + sh_ref[0]).astype(BF16)
        return x, [(dot(u, wg_ref[:, cols]), dot(u, wu_ref[:, cols])) for cols in chunks]

    def down(rows, x, products):
        ffn = None
        for cols, (a, b) in zip(chunks, products):
            part = dot(((a * _sigmoid(a)) * b).astype(BF16), wd_ref[cols, :])
            ffn = part if ffn is None else ffn + part
        y = DEEPNORM_ALPHA * x + (1.0 + gt_ref[0]) * ffn
        y_ref[0, rows, :] = _layer_norm(y, lg_ref[...], lb_ref[...])

    staged = None
    for rows in row_blocks + [None]:
        upcoming = gate_up(rows) if rows is not None else None
        if staged is not None:
            down(*staged)
        staged = (rows,) + upcoming if rows is not None else None


def _ffn(x, scale2, shift2, gate2, w_gate, w_up, w_down, ln_g, ln_b, *, tm, n_chunks):
    bsz, seq, _ = x.shape
    row = pl.BlockSpec((1, 1, D_MODEL), lambda b, i: (b, 0, 0))
    resident = lambda a: pl.BlockSpec(a.shape, lambda b, i: (0, 0), pipeline_mode=pl.Buffered(1))
    return pl.pallas_call(
        functools.partial(_ffn_kernel, n_chunks=n_chunks),
        grid=(bsz, seq // tm),
        in_specs=[
            pl.BlockSpec((1, tm, D_MODEL), lambda b, i: (b, i, 0)),
            row, row, row,
            resident(w_gate), resident(w_up), resident(w_down), resident(ln_g), resident(ln_b),
        ],
        out_specs=pl.BlockSpec((1, tm, D_MODEL), lambda b, i: (b, i, 0)),
        out_shape=jax.ShapeDtypeStruct(x.shape, F32),
        compiler_params=_params(2),
        name="ffn",
    )(x, scale2, shift2, gate2, w_gate, w_up, w_down, ln_g, ln_b)


def _layer(x, c, w_ada, b_ada, w_in, ssm_a_re, ssm_a_im, ssm_log_dt, ssm_b_re, ssm_b_im, ssm_c_re, ssm_c_im,
           ssm_d, w_glu_lin, w_glu_gate, cmp_pe_k, cmp_pe_v, w_cmp_k1, w_cmp_k2, w_cmp_v1, w_cmp_v2,
           w_nsa_proj, w_out, ln1_g, ln1_b, w_ffn_gate, w_ffn_up, w_ffn_down, ln2_g, ln2_b):
    bsz, seq, _ = x.shape
    tq = min(KEY_BLOCK, seq)
    tm = min(512, seq)
    tm_dense = min(1024, seq)

    mod = _ada_mod(c, w_ada, b_ada).reshape(6, bsz, 1, D_MODEL)
    shift1, scale1, gate1, shift2, scale2, gate2 = (mod[k] for k in range(6))

    w_nat, w_tr, w_mg = _split_w_in(w_in)
    u_ssm, kv_cmp, k_aug, qT, v_aug, bgT = _in_proj(x, scale1, shift1, w_nat, w_tr, tm=tm)

    mats = _ssm_matrices(ssm_a_re, ssm_a_im, ssm_log_dt, ssm_b_re, ssm_b_im, ssm_c_re, ssm_c_im, ssm_d)
    g_ssm = _ssm(u_ssm, mats)

    kc3, vcT = _compress(kv_cmp,
                        cmp_pe_k.reshape(1, CMP_BLOCK * HEAD_DIM), cmp_pe_v.reshape(1, CMP_BLOCK * HEAD_DIM),
                        w_cmp_k1.reshape(CMP_BLOCK * HEAD_DIM, -1), w_cmp_k2,
                        w_cmp_v1.reshape(CMP_BLOCK * HEAD_DIM, -1), w_cmp_v2)
    o_nsa = _nsa(qT, kc3, vcT, k_aug, v_aug, bgT, tq=tq)

    x1 = _merge(x, g_ssm, o_nsa, scale1, shift1, gate1, w_mg,
                w_glu_lin.astype(BF16), w_glu_gate.astype(BF16), w_nsa_proj.astype(BF16), w_out.astype(BF16),
                ln1_g.reshape(1, D_MODEL), ln1_b.reshape(1, D_MODEL), tm=tm_dense)
    return _ffn(x1, scale2, shift2, gate2, w_ffn_gate.astype(BF16), w_ffn_up.astype(BF16),
                w_ffn_down.astype(BF16), ln2_g.reshape(1, D_MODEL), ln2_b.reshape(1, D_MODEL), tm=tm_dense, n_chunks=2)


def kernel(x, c, w_ada, b_ada, w_in, ssm_a_re, ssm_a_im, ssm_log_dt, ssm_b_re, ssm_b_im, ssm_c_re, ssm_c_im,
           ssm_d, w_glu_lin, w_glu_gate, cmp_pe_k, cmp_pe_v, w_cmp_k1, w_cmp_k2, w_cmp_v1, w_cmp_v2,
           w_nsa_proj, w_out, ln1_g, ln1_b, w_ffn_gate, w_ffn_up, w_ffn_down, ln2_g, ln2_b):
    for l in range(w_ada.shape[0]):
        x = _layer(x, c, w_ada[l], b_ada[l], w_in[l], ssm_a_re[l], ssm_a_im[l], ssm_log_dt[l],
                   ssm_b_re[l], ssm_b_im[l], ssm_c_re[l], ssm_c_im[l], ssm_d[l],
                   w_glu_lin[l], w_glu_gate[l], cmp_pe_k[l], cmp_pe_v[l],
                   w_cmp_k1[l], w_cmp_k2[l], w_cmp_v1[l], w_cmp_v2[l], w_nsa_proj[l], w_out[l],
                   ln1_g[l], ln1_b[l], w_ffn_gate[l], w_ffn_up[l], w_ffn_down[l], ln2_g[l], ln2_b[l])
    return x
```

```python
import functools
import math

import jax
import jax.numpy as jnp
from jax import lax
from jax.experimental import pallas as pl
from jax.experimental.pallas import tpu as pltpu

F32 = jnp.float32
BF16 = jnp.bfloat16
HIGHEST = lax.Precision.HIGHEST

D_MODEL = 1024
SSM_WIDTH = D_MODEL // 2
SSM_GROUP_SIZE = 16
SSM_GROUPS = SSM_WIDTH // SSM_GROUP_SIZE
SSM_STATE = 64
SSM_CHUNK = 16
SSM_GROUP_TILE = 8
SSM_PREP_GROUPS = 4
SSM_BATCH_TILE = 4
SSM_STATE_ROW_PAD = 8
N_HEADS = 8
HEAD_DIM = 64
N_KV_GROUPS = 2
HEADS_PER_GROUP = N_HEADS // N_KV_GROUPS
ATTN_WIDTH = N_HEADS * HEAD_DIM
KV_WIDTH = N_KV_GROUPS * HEAD_DIM
CMP_BLOCK = 32
CMP_STRIDE = 16
CMP_HIDDEN = HEAD_DIM
SEL_BLOCK = 64
SEL_TOP_K = 8
WINDOW = 256
N_NSA_BRANCHES = 3
GATE_ROWS = 16
FFN_HIDDEN = (8 * D_MODEL + 3 * 256 - 1) // (3 * 256) * 256
DEEPNORM_ALPHA = 2.0 ** 0.25
LN_EPS = 1e-5
MASK_VALUE = -1e30

VMEM_LIMIT = 56 * 1024 * 1024
MXU_WIDTH = 256
BF16_ROWS = 16
ROW_BLOCK = 256

KEY_BLOCK = 256
SEL_SHIFT = SEL_BLOCK.bit_length() - 1
SEL_PER_KEY_BLOCK = KEY_BLOCK // SEL_BLOCK
KEY_AUG = 128
ALIBI_COL = HEAD_DIM
N_PIECES = 4
MASK_COL = HEAD_DIM + 16
QUERY_AUG = HEAD_DIM + 32
VAL_AUG = HEAD_DIM + 16
LOG2E = math.log2(math.e)
SCORES_AHEAD = 5

NAT_COLS = SSM_WIDTH + 4 * KV_WIDTH
TR_ROWS = ATTN_WIDTH + 2 * KV_WIDTH + N_KV_GROUPS * GATE_ROWS


def _bf16_pieces(value, n):
    pieces = []
    rest = value
    for _ in range(n):
        mant, expo = math.frexp(rest)
        piece = math.ldexp(round(mant * 256.0) / 256.0, expo)
        pieces.append(piece)
        rest -= piece
    return pieces


def _sigmoid(x):
    return 1.0 / (1.0 + jnp.exp(-x))


def _gelu(x):
    c = math.sqrt(2.0 / math.pi)
    return 0.5 * x * (1.0 + jnp.tanh(c * (x + 0.044715 * (x * x * x))))


def _layer_norm(y, gain, bias):
    mu = jnp.mean(y, axis=-1, keepdims=True)
    d = y - mu
    var = jnp.mean(d * d, axis=-1, keepdims=True)
    return d * lax.rsqrt(var + LN_EPS) * gain + bias


def _params(n_axes):
    return pltpu.CompilerParams(dimension_semantics=("arbitrary",) * n_axes, vmem_limit_bytes=VMEM_LIMIT)


def _ada_kernel(c_ref, w_ref, b_ref, o_ref):
    c = c_ref[...]
    a = c * _sigmoid(c)
    w = w_ref[...].astype(BF16)
    a_hi = a.astype(BF16)
    a_lo = (a - a_hi.astype(F32)).astype(BF16)
    mod = (jnp.dot(a_hi, w, preferred_element_type=F32) + jnp.dot(a_lo, w, preferred_element_type=F32)) + b_ref[...]
    for k in range(o_ref.shape[0]):
        o_ref[k] = mod[:, k * D_MODEL:(k + 1) * D_MODEL]


def _ada_mod(c, w_ada, b_ada):
    bsz = c.shape[0]
    per_step = 2
    return pl.pallas_call(
        _ada_kernel,
        grid=(6 // per_step,),
        in_specs=[
            pl.BlockSpec((bsz, D_MODEL), lambda j: (0, 0)),
            pl.BlockSpec((D_MODEL, per_step * D_MODEL), lambda j: (0, j)),
            pl.BlockSpec((1, per_step * D_MODEL), lambda j: (0, j)),
        ],
        out_specs=pl.BlockSpec((per_step, bsz, D_MODEL), lambda j: (j, 0, 0)),
        out_shape=jax.ShapeDtypeStruct((6, bsz, D_MODEL), F32),
        compiler_params=_params(1),
        name="ada_mod",
    )(c, w_ada, b_ada.reshape(1, 6 * D_MODEL))


def _in_proj_kernel(x_ref, sc_ref, sh_ref, wn_ref, wt_ref, ussm_ref, kc_ref, kp_ref, qT_ref, vp_ref, bgT_ref):
    i = pl.program_id(1)
    tm = x_ref.shape[1]
    u = (x_ref[0] * (1.0 + sc_ref[0]) + sh_ref[0]).astype(BF16)
    nat = jnp.dot(u, wn_ref[...], preferred_element_type=F32)
    ussm_ref[0] = nat[:, :SSM_WIDTH]
    for s in range(2):
        for g in range(N_KV_GROUPS):
            lo = SSM_WIDTH + s * KV_WIDTH + g * HEAD_DIM
            kc_ref[0, s, g] = nat[:, lo:lo + HEAD_DIM]
    pos = i * tm + lax.broadcasted_iota(jnp.int32, (tm, KEY_AUG), 0)
    col = lax.broadcasted_iota(jnp.int32, (tm, KEY_AUG), 1)
    blk = lax.shift_right_logical(pos, SEL_SHIFT)
    in_a = (col >= ALIBI_COL) & (col < ALIBI_COL + N_PIECES)
    in_b = (col >= ALIBI_COL + N_PIECES) & (col < ALIBI_COL + 2 * N_PIECES)
    hot = col == MASK_COL + (blk & (SEL_PER_KEY_BLOCK - 1))
    aux = jnp.where(in_a, blk * SEL_BLOCK, jnp.where(in_b, pos & (SEL_BLOCK - 1), jnp.where(hot, 1, 0))).astype(F32)
    head_lanes = col < HEAD_DIM
    for s in range(2):
        lo = SSM_WIDTH + (2 + s) * KV_WIDTH
        both = nat[:, lo:lo + KV_WIDTH]
        for g in range(N_KV_GROUPS):
            keys = both if g == 0 else pltpu.roll(both, HEAD_DIM, 1)
            kp_ref[0, s, g] = jnp.where(head_lanes, keys, aux).astype(BF16)
    tr = lax.dot_general(wt_ref[...], u, (((1,), (1,)), ((), ())), preferred_element_type=F32)
    qT_ref[0] = tr[:ATTN_WIDTH] * (HEAD_DIM ** -0.5)
    ones_rows = jnp.where(lax.broadcasted_iota(jnp.int32, (VAL_AUG - HEAD_DIM, tm), 0) == 0, 1.0, 0.0)
    for s in range(2):
        for g in range(N_KV_GROUPS):
            lo = ATTN_WIDTH + (s * N_KV_GROUPS + g) * HEAD_DIM
            v_aug = jnp.concatenate([tr[lo:lo + HEAD_DIM], ones_rows], axis=0).astype(BF16)
            for c in range(tm // KEY_BLOCK):
                vp_ref[0, s, g, c] = v_aug[:, c * KEY_BLOCK:(c + 1) * KEY_BLOCK]
    bgT_ref[0] = _sigmoid(tr[ATTN_WIDTH + 2 * KV_WIDTH:])


def _in_proj(x, scale1, shift1, w_nat, w_tr, *, tm):
    bsz, seq, _ = x.shape
    n_t = seq // tm
    n_kb = seq // KEY_BLOCK
    out_shapes = (
        jax.ShapeDtypeStruct((bsz, seq, SSM_WIDTH), F32),
        jax.ShapeDtypeStruct((bsz, 2, N_KV_GROUPS, seq, HEAD_DIM), F32),
        jax.ShapeDtypeStruct((bsz, 2, N_KV_GROUPS, seq, KEY_AUG), BF16),
        jax.ShapeDtypeStruct((bsz, ATTN_WIDTH, seq), F32),
        jax.ShapeDtypeStruct((bsz, 2, N_KV_GROUPS, n_kb, VAL_AUG, KEY_BLOCK), BF16),
        jax.ShapeDtypeStruct((bsz, N_KV_GROUPS * GATE_ROWS, seq), F32),
    )
    return pl.pallas_call(
        _in_proj_kernel,
        grid=(bsz, n_t),
        in_specs=[
            pl.BlockSpec((1, tm, D_MODEL), lambda b, i: (b, i, 0)),
            pl.BlockSpec((1, 1, D_MODEL), lambda b, i: (b, 0, 0)),
            pl.BlockSpec((1, 1, D_MODEL), lambda b, i: (b, 0, 0)),
            pl.BlockSpec((D_MODEL, NAT_COLS), lambda b, i: (0, 0)),
            pl.BlockSpec((TR_ROWS, D_MODEL), lambda b, i: (0, 0)),
        ],
        out_specs=(
            pl.BlockSpec((1, tm, SSM_WIDTH), lambda b, i: (b, i, 0)),
            pl.BlockSpec((1, 2, N_KV_GROUPS, tm, HEAD_DIM), lambda b, i: (b, 0, 0, i, 0)),
            pl.BlockSpec((1, 2, N_KV_GROUPS, tm, KEY_AUG), lambda b, i: (b, 0, 0, i, 0)),
            pl.BlockSpec((1, ATTN_WIDTH, tm), lambda b, i: (b, 0, i)),
            pl.BlockSpec((1, 2, N_KV_GROUPS, tm // KEY_BLOCK, VAL_AUG, KEY_BLOCK), lambda b, i: (b, 0, 0, i, 0, 0)),
            pl.BlockSpec((1, N_KV_GROUPS * GATE_ROWS, tm), lambda b, i: (b, 0, i)),
        ),
        out_shape=out_shapes,
        compiler_params=_params(2),
        name="in_proj",
    )(x, scale1, shift1, w_nat, w_tr)


def _split_w_in(w_in):
    o_q = SSM_WIDTH
    o_kv = o_q + ATTN_WIDTH
    o_bg = o_kv + 6 * KV_WIDTH
    o_mg = o_bg + N_NSA_BRANCHES * N_HEADS
    kv = [w_in[:, o_kv + s * KV_WIDTH:o_kv + (s + 1) * KV_WIDTH] for s in range(6)]
    w_nat = jnp.concatenate([w_in[:, :o_q], kv[0], kv[1], kv[2], kv[4]], axis=1)
    per_group = HEADS_PER_GROUP * N_NSA_BRANCHES
    bg = w_in[:, o_bg:o_mg].reshape(D_MODEL, N_KV_GROUPS, per_group)
    bg = jnp.pad(bg, ((0, 0), (0, 0), (0, GATE_ROWS - per_group))).reshape(D_MODEL, N_KV_GROUPS * GATE_ROWS)
    w_tr = jnp.concatenate([w_in[:, o_q:o_kv], kv[3], kv[5], bg], axis=1).T
    return w_nat.astype(BF16), w_tr.astype(BF16), w_in[:, o_mg:].astype(BF16)


def _ssm_prep_kernel(*refs):
    for k in range(refs[0].shape[0]):
        _ssm_prep_group(*[ref.at[pl.ds(k, 1)] for ref in refs])


def _ssm_prep_group(a_ref, logdt_ref, bT_ref, c_ref, d_ref, mT_ref, pT_ref, qT_ref, a16_ref):
    t_n = SSM_CHUNK
    c_n = SSM_GROUP_SIZE
    n_p = SSM_STATE
    rows = t_n * c_n
    dt = jnp.exp(logdt_ref[0])
    a_re = a_ref[0, 0]
    a_im = a_ref[0, 1]
    decay = jnp.exp(a_re * dt)
    ab_re = decay * jnp.cos(a_im * dt)
    ab_im = decay * jnp.sin(a_im * dt)
    n_re = ab_re - 1.0
    denom = a_re * a_re + a_im * a_im
    f_re = (n_re * a_re + ab_im * a_im) / denom
    f_im = (ab_im * a_re - n_re * a_im) / denom
    powers = [(jnp.ones_like(ab_re), jnp.zeros_like(ab_re))]
    for _ in range(t_n):
        p_re, p_im = powers[-1]
        powers.append((p_re * ab_re - p_im * ab_im, p_re * ab_im + p_im * ab_re))

    def per_token(first, step):
        picks = [powers[first + step * k] for k in range(t_n)]
        return (jnp.concatenate([jnp.broadcast_to(p[0], (c_n, 2 * n_p)) for p in picks], axis=0),
                jnp.concatenate([jnp.broadcast_to(p[1], (c_n, 2 * n_p)) for p in picks], axis=0))

    def cmul(x_re, x_im, y_re, y_im):
        return x_re * y_re - x_im * y_im, x_re * y_im + x_im * y_re

    bb_re, bb_im = cmul(f_re, f_im, bT_ref[0, 0], bT_ref[0, 1])
    bt_re = jnp.concatenate([bb_re] * t_n, axis=0)
    bt_im = jnp.concatenate([bb_im] * t_n, axis=0)
    ct_re = jnp.concatenate([c_ref[0, 0]] * t_n, axis=0)
    ct_im = jnp.concatenate([c_ref[0, 1]] * t_n, axis=0)

    p_re, p_im = cmul(*per_token(t_n - 1, -1), bt_re, bt_im)
    low_half = lax.broadcasted_iota(jnp.int32, (1, 2 * n_p), 1) < n_p
    pT_ref[0] = jnp.where(low_half, p_re, p_im).T.astype(BF16)

    q_re, q_im = cmul(ct_re, ct_im, *per_token(1, 1))
    qT_ref[0] = jnp.where(low_half, q_re, -q_im).astype(BF16)
    a16_ref[0] = jnp.concatenate([powers[t_n][0][:, :n_p], powers[t_n][1][:, :n_p]], axis=0)

    k_re, k_im = cmul(ct_re, ct_im, *per_token(0, 1))
    nt = (((1,), (1,)), ((), ()))
    strips = (lax.dot_general(_split3(k_re[:, :n_p]), _stack3(bt_re[:, :n_p], axis=1), nt, preferred_element_type=F32)
              - lax.dot_general(_split3(k_im[:, :n_p]), _stack3(bt_im[:, :n_p], axis=1), nt,
                                preferred_element_type=F32))
    lane = lax.broadcasted_iota(jnp.int32, (1, rows), 1)
    s_lane = lax.shift_right_logical(lane, c_n.bit_length() - 1)
    ci_lane = lane & (c_n - 1)
    row = lax.broadcasted_iota(jnp.int32, (rows, 1), 0)
    t_row = lax.shift_right_logical(row, c_n.bit_length() - 1)
    co_row = row & (c_n - 1)
    blocks = []
    for t in range(t_n):
        acc = jnp.zeros((c_n, rows), F32)
        for lag in range(t + 1):
            acc = jnp.where(s_lane == t - lag, strips[lag * c_n:(lag + 1) * c_n, :], acc)
        blocks.append(acc)
    d_rows = jnp.concatenate([d_ref[0]] * t_n, axis=0)
    skip = jnp.where((t_row == s_lane) & (co_row == ci_lane), d_rows, 0.0)
    mT_ref[0] = (jnp.concatenate(blocks, axis=0) + skip).astype(BF16)


def _ssm_matrices(a_re, a_im, log_dt, b_re, b_im, c_re, c_im, d_skip):
    n_g, n_p = a_re.shape
    c_n = SSM_GROUP_SIZE
    rows = SSM_CHUNK * c_n
    twice = lambda x: jnp.concatenate([x, x], axis=-1).astype(F32)
    a = twice(jnp.stack([a_re, a_im], axis=1)).reshape(n_g, 2, 1, 2 * n_p)
    bT = twice(jnp.swapaxes(jnp.stack([b_re, b_im], axis=1), 2, 3))
    c = twice(jnp.stack([c_re, c_im], axis=1))
    per_step = SSM_PREP_GROUPS
    spec = lambda shape: pl.BlockSpec((per_step,) + shape, lambda g: (g,) + (0,) * len(shape))
    mT, pT, qT, a16 = pl.pallas_call(
        _ssm_prep_kernel,
        grid=(n_g // per_step,),
        in_specs=[spec((2, 1, 2 * n_p)), spec((1, 1)), spec((2, c_n, 2 * n_p)), spec((2, c_n, 2 * n_p)),
                  spec((c_n, 1))],
        out_specs=(spec((rows, rows)), spec((2 * n_p, rows)), spec((rows, 2 * n_p)), spec((2, n_p))),
        out_shape=(
            jax.ShapeDtypeStruct((n_g, rows, rows), BF16),
            jax.ShapeDtypeStruct((n_g, 2 * n_p, rows), BF16),
            jax.ShapeDtypeStruct((n_g, rows, 2 * n_p), BF16),
            jax.ShapeDtypeStruct((n_g, 2, n_p), F32),
        ),
        compiler_params=_params(1),
        name="ssm_prep",
    )(a, log_dt.astype(F32).reshape(n_g, 1, 1), bT, c, d_skip.astype(F32).reshape(n_g, c_n, 1))
    return mT, pT.reshape(n_g, 2, n_p, rows), qT, a16


def _ssm_kernel(u_ref, mT_ref, pT_ref, qT_ref, a16_ref, o_ref, ut_scr, yt_scr, sre, sim, *, n_chunks):
    n_g = SSM_GROUP_TILE
    n_b = u_ref.shape[0]
    t_n = SSM_CHUNK
    c_n = SSM_GROUP_SIZE
    lanes = [(e, g) for e in range(n_b) for g in range(n_g)]

    for e in range(n_b):
        for s in range(t_n):
            x_s = u_ref[e, pl.ds(s, n_chunks, stride=t_n), :]
            ut_scr[e * n_g:(e + 1) * n_g, s * c_n:(s + 1) * c_n, :] = (
                x_s.T.reshape(n_g, c_n, n_chunks).astype(BF16))

    pitch = n_chunks + SSM_STATE_ROW_PAD

    def lane_rows(k):
        return pl.ds(k * pitch, n_chunks)

    for k, (e, g) in enumerate(lanes):
        ut = ut_scr[k]
        sre[lane_rows(k), :] = jnp.dot(pT_ref[g, 0], ut, preferred_element_type=F32).T
        sim[lane_rows(k), :] = jnp.dot(pT_ref[g, 1], ut, preferred_element_type=F32).T

    a_r = jnp.concatenate([a16_ref[pl.ds(0, n_g, stride=2), :]] * n_b, axis=0)
    a_i = jnp.concatenate([a16_ref[pl.ds(1, n_g, stride=2), :]] * n_b, axis=0)

    def carry_states(c, h):
        h_r, h_i = h
        rows = pl.ds(c, len(lanes), stride=pitch)
        s_r = sre[rows, :]
        s_i = sim[rows, :]
        sre[rows, :] = h_r
        sim[rows, :] = h_i
        return a_r * h_r - a_i * h_i + s_r, a_r * h_i + a_i * h_r + s_i

    zero = jnp.zeros((len(lanes), SSM_STATE), F32)
    lax.fori_loop(0, n_chunks, carry_states, (zero, zero), unroll=4)

    nt = (((1,), (1,)), ((), ()))
    for k, (e, g) in enumerate(lanes):
        yt = jnp.dot(mT_ref[g], ut_scr[k], preferred_element_type=F32)
        entering = jnp.concatenate([sre[lane_rows(k), :], sim[lane_rows(k), :]], axis=1).astype(BF16)
        yt = yt + lax.dot_general(qT_ref[g], entering, nt, preferred_element_type=F32)
        yt_scr[k] = _gelu(yt)

    for e in range(n_b):
        for t in range(t_n):
            z = yt_scr[e * n_g:(e + 1) * n_g, t * c_n:(t + 1) * c_n, :].reshape(n_g * c_n, n_chunks)
            o_ref[e, pl.ds(t, n_chunks, stride=t_n), :] = z.T


def _ssm(u_ssm, mats):
    bsz, seq, width = u_ssm.shape
    n_chunks = seq // SSM_CHUNK
    mT, pT, qT, a16 = mats
    gt = SSM_GROUP_TILE
    n_tiles = SSM_GROUPS // gt
    rows = SSM_CHUNK * SSM_GROUP_SIZE
    a16 = a16.reshape(2 * SSM_GROUPS, SSM_STATE)
    bt = SSM_BATCH_TILE if bsz % SSM_BATCH_TILE == 0 else 1
    n_lanes = bt * gt
    per_tile = lambda a: pl.BlockSpec((gt,) + a.shape[1:], lambda t, b: (t,) + (0,) * (a.ndim - 1))
    return pl.pallas_call(
        functools.partial(_ssm_kernel, n_chunks=n_chunks),
        grid=(n_tiles, bsz // bt),
        in_specs=[
            pl.BlockSpec((bt, seq, gt * SSM_GROUP_SIZE), lambda t, b: (b, 0, t)),
            per_tile(mT), per_tile(pT), per_tile(qT),
            pl.BlockSpec((2 * gt, SSM_STATE), lambda t, b: (t, 0)),
        ],
        out_specs=pl.BlockSpec((bt, seq, gt * SSM_GROUP_SIZE), lambda t, b: (b, 0, t)),
        out_shape=jax.ShapeDtypeStruct(u_ssm.shape, F32),
        scratch_shapes=[
            pltpu.VMEM((n_lanes, rows, n_chunks), BF16),
            pltpu.VMEM((n_lanes, rows, n_chunks), F32),
            pltpu.VMEM((n_lanes * (n_chunks + SSM_STATE_ROW_PAD), SSM_STATE), F32),
            pltpu.VMEM((n_lanes * (n_chunks + SSM_STATE_ROW_PAD), SSM_STATE), F32),
        ],
        compiler_params=_params(2),
        name="ssm",
    )(u_ssm, mT, pT, qT, a16)


def _split3(x):
    hi = x.astype(BF16)
    lo = (x - hi.astype(F32)).astype(BF16)
    return jnp.concatenate([hi, lo, hi], axis=-1)


def _stack3(w, axis=0):
    hi = w.astype(BF16)
    lo = (w - hi.astype(F32)).astype(BF16)
    return jnp.concatenate([hi, hi, lo], axis=axis)


def _compress_kernel(kx_ref, vx_ref, pek_ref, pev_ref, wk1_ref, wk2_ref, wv1_ref, wv2_ref, kc_ref, vcT_ref):
    def mlp(x_ref, g, pe_ref, w1_ref, w2_ref):
        n = x_ref.shape[3] // CMP_STRIDE
        x = jnp.concatenate([x_ref[0, 0, g, pl.ds(l, n, stride=CMP_STRIDE), :] for l in range(CMP_STRIDE)], axis=1)
        both = jnp.dot(_split3(x), w1_ref[...], preferred_element_type=F32)
        pe_both = jnp.dot(_split3(pe_ref[...]), w1_ref[...], preferred_element_type=F32)
        bias = pe_both[0:1, :CMP_HIDDEN] + pe_both[1:2, CMP_HIDDEN:]
        h = _gelu(both[:, :CMP_HIDDEN] + pltpu.roll(both[:, CMP_HIDDEN:], n - 1, 0) + bias)
        return jnp.dot(_split3(h), w2_ref[...], preferred_element_type=F32)

    for g in range(kx_ref.shape[2]):
        kc = mlp(kx_ref, g, pek_ref, wk1_ref, wk2_ref)
        kc_hi = kc.astype(BF16)
        kc_lo = (kc - kc_hi.astype(F32)).astype(BF16)
        kc_ref[0, g] = jnp.concatenate([kc_hi, kc_hi, kc_lo, jnp.zeros_like(kc_hi)], axis=1)
        vcT_ref[0, g] = mlp(vx_ref, g, pev_ref, wv1_ref, wv2_ref).T.astype(BF16)


def _compress(kv, pe_k, pe_v, wk1, wk2, wv1, wv2):
    bsz, _, n_g, seq, width = kv.shape
    n_chunks = seq // CMP_STRIDE
    half = CMP_STRIDE * HEAD_DIM
    side_by_side = lambda w1: _stack3(jnp.concatenate([w1[:half], w1[half:]], axis=1))
    two_rows = lambda pe: jnp.pad(pe.reshape(2, half), ((0, BF16_ROWS - 2), (0, 0)))
    pe_k, pe_v = two_rows(pe_k), two_rows(pe_v)
    wk1, wv1, wk2, wv2 = side_by_side(wk1), side_by_side(wv1), _stack3(wk2), _stack3(wv2)
    blk = lambda s: pl.BlockSpec((1, 1, n_g, seq, width), lambda b: (b, s, 0, 0, 0))
    full = lambda a: pl.BlockSpec(a.shape, lambda b: (0, 0))
    return pl.pallas_call(
        _compress_kernel,
        grid=(bsz,),
        in_specs=[blk(0), blk(1), full(pe_k), full(pe_v), full(wk1), full(wk2), full(wv1), full(wv2)],
        out_specs=(
            pl.BlockSpec((1, n_g, n_chunks, 4 * HEAD_DIM), lambda b: (b, 0, 0, 0)),
            pl.BlockSpec((1, n_g, HEAD_DIM, n_chunks), lambda b: (b, 0, 0, 0)),
        ),
        out_shape=(
            jax.ShapeDtypeStruct((bsz, n_g, n_chunks, 4 * HEAD_DIM), BF16),
            jax.ShapeDtypeStruct((bsz, n_g, HEAD_DIM, n_chunks), BF16),
        ),
        compiler_params=_params(1),
        name="compress",
    )(kv, kv, pe_k, pe_v, wk1, wk2, wv1, wv2)


def _pipeline(units):
    issue = lambda unit: None if unit[0] is None else unit[0]()
    pending = [issue(unit) for unit in units[:SCORES_AHEAD]]
    for n, unit in enumerate(units):
        if n + SCORES_AHEAD < len(units):
            pending.append(issue(units[n + SCORES_AHEAD]))
        unit[1](pending.pop(0))


def _nsa_kernel(qT_ref, kc_ref, vcT_ref, ks_ref, vs_ref, kw_ref, vw_ref, bg_ref, o_ref,
                negm_scr, qa_scr, m_scr, acc_scr, out_scr, *smem, tq, n_cmp, n_sel):
    i = pl.program_id(1)
    gw = HEADS_PER_GROUP * HEAD_DIM
    groups = [
        _nsa_group(g, i, qT_ref.at[:, g * gw:(g + 1) * gw], kc_ref.at[:, g:g + 1], vcT_ref.at[:, g:g + 1],
                   ks_ref.at[:, :, g:g + 1], vs_ref.at[:, :, g:g + 1], kw_ref.at[:, :, g:g + 1],
                   vw_ref.at[:, :, g:g + 1], bg_ref.at[:, g:g + 1], o_ref.at[:, :, g * gw:(g + 1) * gw],
                   negm_scr.at[g], qa_scr.at[g], m_scr.at[g], acc_scr.at[g], out_scr.at[g],
                   smem[2 * g], smem[2 * g + 1], tq=tq, n_cmp=n_cmp, n_sel=n_sel)
        for g in range(N_KV_GROUPS)]
    stages = [[unit for stage_units, _ in groups for unit in stage_units[stage]]
              for stage in range(3)]
    _pipeline(stages[0] + stages[1] + stages[2])
    for _, finish_group in groups:
        finish_group()


def _nsa_group(g, i, qT_ref, kc_ref, vcT_ref, ks_ref, vs_ref, kw_ref, vw_ref, bg_ref, o_ref,
               negm_scr, qa_scr, m_scr, acc_scr, out_scr, used_smem, todo_smem, *, tq, n_cmp, n_sel):
    t_row = i * tq + lax.broadcasted_iota(jnp.int32, (1, tq), 1)
    slopes = [2.0 ** -(g * HEADS_PER_GROUP + hh + 1) for hh in range(HEADS_PER_GROUP)]

    def gate(hh, branch):
        r = hh * N_NSA_BRANCHES + branch
        return bg_ref[0, 0, r:r + 1, :]

    def head_rows(hh):
        return slice(hh * HEAD_DIM, (hh + 1) * HEAD_DIM)

    def head_lanes(hh):
        return slice(hh * tq, (hh + 1) * tq)

    kc3 = kc_ref[0, 0]
    vcT = vcT_ref[0, 0]
    cmp_end = lax.broadcasted_iota(jnp.int32, (n_cmp, 1), 0) * CMP_STRIDE + (CMP_BLOCK - 1)
    dist_c = (t_row - cmp_end).astype(F32)
    valid_c = dist_c >= 0.0
    piece = lax.broadcasted_iota(jnp.int32, (16, 1), 0)
    log2e_pieces = _bf16_pieces(LOG2E, N_PIECES)
    log2e_col = jnp.zeros((16, 1), F32)
    for k, value in enumerate(log2e_pieces):
        log2e_col = jnp.where((piece == k) | (piece == N_PIECES + k), value, log2e_col)
    for hh in range(HEADS_PER_GROUP):
        alibi = jnp.broadcast_to(log2e_col * slopes[hh], (16, tq)).astype(BF16)
        q_l2 = (qT_ref[0, head_rows(hh), :] * LOG2E).astype(BF16)
        qa_scr[:, head_lanes(hh)] = jnp.concatenate([q_l2, alibi], axis=0)

    def cmp_scores(hh):
        q_h = qT_ref[0, head_rows(hh), :]
        q_hi = q_h.astype(BF16)
        q_lo = (q_h - q_hi.astype(F32)).astype(BF16)
        q3 = jnp.concatenate([q_hi, q_lo, q_hi, jnp.zeros_like(q_hi)], axis=0)
        return jnp.dot(kc3, q3, preferred_element_type=F32)

    def cmp_attend(p_heads, hh, s):
        s = jnp.where(valid_c, s - slopes[hh] * dist_c, MASK_VALUE)
        m = jnp.max(s, axis=0, keepdims=True)
        p = jnp.where(valid_c, jnp.exp(s - m), 0.0)
        l = jnp.sum(p, axis=0, keepdims=True)
        p = p / jnp.where(l > 0.0, l, 1.0)
        o_c = jnp.dot(vcT, p.astype(BF16), preferred_element_type=F32)
        out_scr[head_rows(hh), :] = gate(hh, 0) * o_c
        p_heads.append(p)

    def select_blocks(p_heads, _):
        p_sum = sum(p_heads[1:], p_heads[0])
        blk = lax.broadcasted_iota(jnp.int32, (n_sel, n_cmp), 0)
        cmp_ix = lax.broadcasted_iota(jnp.int32, (n_sel, n_cmp), 1)
        ratio = SEL_BLOCK // CMP_STRIDE
        extra = CMP_BLOCK // CMP_STRIDE - 1
        overlap_t = jnp.where((cmp_ix >= ratio * blk - extra) & (cmp_ix <= ratio * blk + ratio - 1),
                              1.0, 0.0).astype(BF16)
        imp = jnp.zeros((n_sel, tq), F32)
        rest = p_sum
        for _ in range(3):
            part = rest.astype(BF16)
            imp = imp + jnp.dot(overlap_t, part, preferred_element_type=F32)
            rest = rest - part.astype(F32)

        j_col = lax.broadcasted_iota(jnp.int32, (n_sel, 1), 0)
        forced = (j_col == 0) | (j_col == lax.shift_right_logical(t_row, SEL_SHIFT))
        future = j_col * SEL_BLOCK > t_row
        score = jnp.where(forced, jnp.inf, jnp.where(future, -jnp.inf, imp))
        rank = jnp.zeros((n_sel, tq), F32)
        for jp in range(n_sel):
            other = score[jp:jp + 1, :]
            ahead = (other > score) | ((other == score) & (j_col > jp))
            rank = rank + jnp.where(ahead, 1.0, 0.0)
        neg = jnp.where(rank < float(min(SEL_TOP_K, n_sel)), 0.0, MASK_VALUE)
        pad_rows = jnp.zeros((BF16_ROWS - SEL_PER_KEY_BLOCK, tq), F32)
        for c in range(n_sel // SEL_PER_KEY_BLOCK):
            rows = neg[c * SEL_PER_KEY_BLOCK:(c + 1) * SEL_PER_KEY_BLOCK, :]
            negm_scr[c] = jnp.concatenate([rows, pad_rows], axis=0).astype(BF16)
            used_smem[c] = (jnp.max(rows) > 0.5 * MASK_VALUE).astype(jnp.int32)

    rel = (lax.broadcasted_iota(jnp.int32, (KEY_BLOCK, tq), 1)
           - lax.broadcasted_iota(jnp.int32, (KEY_BLOCK, tq), 0))
    zero_tail = jnp.zeros((KEY_AUG - QUERY_AUG, tq), BF16)
    no_mask = jnp.zeros((BF16_ROWS, tq), BF16)
    all_masked = jnp.full((BF16_ROWS, tq), MASK_VALUE, BF16)

    sel_even, sel_odd, win_prev, win_diag = range(4)

    def scores(job, hh):
        k_ref, _, kb, enabled, _, _ = job
        k0 = pl.multiple_of(kb * KEY_BLOCK, KEY_BLOCK)
        k_aug = k_ref[0, 0, 0, pl.ds(k0, KEY_BLOCK), :]
        if enabled is None:
            mask_tile = no_mask
        elif enabled is True:
            mask_tile = negm_scr[kb]
        else:
            mask_tile = jnp.where(enabled, negm_scr[kb], all_masked)
        q_aug = jnp.concatenate([qa_scr[:, head_lanes(hh)], mask_tile, zero_tail], axis=0)
        return jnp.dot(k_aug, q_aug, preferred_element_type=F32)

    def accumulate(job, first, hh, s):
        _, v_ref, kb, _, valid, slot = job
        cols = head_lanes(hh)
        if valid is not None:
            s = jnp.where(valid, s, MASK_VALUE)
        m_new = jnp.max(s, axis=0, keepdims=True)
        if not first:
            m_old = m_scr[slot, :, cols]
            m_new = jnp.maximum(m_old, m_new)
        p = jnp.exp2(s - jnp.maximum(m_new, 0.1 * MASK_VALUE))
        pv = jnp.dot(v_ref[0, 0, 0, kb], p.astype(BF16), preferred_element_type=F32)
        if not first:
            pv = jnp.exp2(m_old - m_new) * acc_scr[slot, :, cols] + pv
        acc_scr[slot, :, cols] = pv
        m_scr[slot, :, cols] = m_new

    def attention_units(jobs, first):
        return [(functools.partial(scores, job, hh), functools.partial(accumulate, job, first, hh))
                for job in jobs for hh in range(HEADS_PER_GROUP)]

    def finish(branch, slot_a, slot_b):
        m_a = m_scr[slot_a]
        m_b = m_scr[slot_b]
        m = jnp.maximum(m_a, m_b)
        w_a = jnp.exp2(m_a - m)
        w_b = jnp.exp2(m_b - m)
        for hh in range(HEADS_PER_GROUP):
            cols = head_lanes(hh)
            acc = acc_scr[slot_a, :, cols] * w_a[:, cols] + acc_scr[slot_b, :, cols] * w_b[:, cols]
            l = acc[HEAD_DIM:HEAD_DIM + 1, :]
            o_b = acc[:HEAD_DIM, :] / jnp.where(l > 0.0, l, 1.0)
            out_scr[head_rows(hh), :] = out_scr[head_rows(hh), :] + gate(hh, branch) * o_b

    prev = jnp.maximum(i - 1, 0)
    causal = rel >= 0
    off_without_prev = jnp.where(i > 0, 0, WINDOW + KEY_BLOCK)
    dist_prev = rel + KEY_BLOCK + off_without_prev
    p_heads = []
    stage_units = (
        [(functools.partial(cmp_scores, hh), functools.partial(cmp_attend, p_heads, hh))
         for hh in range(HEADS_PER_GROUP)] + [(None, functools.partial(select_blocks, p_heads))],
        attention_units([(kw_ref, vw_ref, prev, None, dist_prev < WINDOW, win_prev),
                         (kw_ref, vw_ref, i, None, causal, win_diag)], first=True),
        attention_units([(ks_ref, vs_ref, prev, i > 0, None, sel_even),
                         (ks_ref, vs_ref, i, True, causal, sel_odd)], first=True))

    def finish_group():
        n_todo = jnp.int32(0)
        for c in range(used_smem.shape[0]):
            take = jnp.logical_and(c < i - 1, used_smem[c] > 0)

            @pl.when(take)
            def _(c=c, slot=n_todo):
                todo_smem[slot] = c

            n_todo = n_todo + take.astype(jnp.int32)
        todo_smem[n_todo] = 0

        def past_pair(j, carry):
            _pipeline(attention_units([(ks_ref, vs_ref, todo_smem[2 * j], True, None, sel_even),
                                       (ks_ref, vs_ref, todo_smem[2 * j + 1], 2 * j + 1 < n_todo, None, sel_odd)],
                                      first=False))
            return carry

        lax.fori_loop(0, lax.shift_right_logical(n_todo + 1, 1), past_pair, 0)
        finish(1, sel_even, sel_odd)
        finish(2, win_prev, win_diag)
        o_ref[0] = out_scr[...].T

    return stage_units, finish_group


def _nsa(qT, kc3, vcT, k_aug, v_aug, bgT, *, tq):
    bsz, _, seq = qT.shape
    assert tq == KEY_BLOCK == WINDOW and seq % tq == 0
    n_cmp = kc3.shape[2]
    n_sel = seq // SEL_BLOCK
    n_kb = seq // KEY_BLOCK
    gw = HEADS_PER_GROUP * HEAD_DIM
    bg4 = bgT.reshape(bsz, N_KV_GROUPS, GATE_ROWS, seq)
    n_g = N_KV_GROUPS
    k_spec = lambda s: pl.BlockSpec((1, 1, n_g, seq, KEY_AUG), lambda b, i: (b, s, 0, 0, 0))
    v_spec = lambda s: pl.BlockSpec((1, 1, n_g, n_kb, VAL_AUG, KEY_BLOCK), lambda b, i: (b, s, 0, 0, 0, 0))
    smem = []
    for _ in range(n_g):
        smem += [pltpu.SMEM((n_kb,), jnp.int32),
                 pltpu.SMEM((n_kb + 1,), jnp.int32)]
    return pl.pallas_call(
        functools.partial(_nsa_kernel, tq=tq, n_cmp=n_cmp, n_sel=n_sel),
        grid=(bsz, seq // tq),
        in_specs=[
            pl.BlockSpec((1, n_g * gw, tq), lambda b, i: (b, 0, i)),
            pl.BlockSpec((1, n_g, n_cmp, 4 * HEAD_DIM), lambda b, i: (b, 0, 0, 0)),
            pl.BlockSpec((1, n_g, HEAD_DIM, n_cmp), lambda b, i: (b, 0, 0, 0)),
            k_spec(0), v_spec(0),
            k_spec(1), v_spec(1),
            pl.BlockSpec((1, n_g, GATE_ROWS, tq), lambda b, i: (b, 0, 0, i)),
        ],
        out_specs=pl.BlockSpec((1, tq, n_g * gw), lambda b, i: (b, i, 0)),
        out_shape=jax.ShapeDtypeStruct((bsz, seq, ATTN_WIDTH), F32),
        scratch_shapes=[
            pltpu.VMEM((n_g, n_kb, 16, tq), BF16),
            pltpu.VMEM((n_g, HEAD_DIM + 16, HEADS_PER_GROUP * tq), BF16),
            pltpu.VMEM((n_g, 4, 1, HEADS_PER_GROUP * tq), F32),
            pltpu.VMEM((n_g, 4, VAL_AUG, HEADS_PER_GROUP * tq), F32),
            pltpu.VMEM((n_g, gw, tq), F32),
        ] + smem,
        compiler_params=_params(2),
        name="nsa",
    )(qT, kc3, vcT, k_aug, v_aug, k_aug, v_aug, bg4)


def _merge_kernel(x_ref, g_ref, o_ref_in, sc_ref, sh_ref, gt_ref, wm_ref, wl_ref, wg_ref, wn_ref, wo_ref,
                  lg_ref, lb_ref, y_ref):
    tm = x_ref.shape[1]
    row_blocks = [slice(r, r + ROW_BLOCK) for r in range(0, tm, ROW_BLOCK)]
    halves = [(0, D_MODEL // 2), (D_MODEL // 2, D_MODEL)]
    dot = functools.partial(jnp.dot, preferred_element_type=F32)

    def first_stage(rows):
        g = g_ref[0, rows, :].astype(BF16)
        o = o_ref_in[0, rows, :].astype(BF16)
        branch = [(dot(g, wl_ref[:, lo:hi]), dot(g, wg_ref[:, lo:hi]), dot(o, wn_ref[:, lo:hi])) for lo, hi in halves]
        x = x_ref[0, rows, :]
        u = (x * (1.0 + sc_ref[0]) + sh_ref[0]).astype(BF16)
        gates = [(dot(u, wm_ref[:, lo:hi]), dot(u, wm_ref[:, D_MODEL + lo:D_MODEL + hi])) for lo, hi in halves]
        return x, branch, gates

    def second_stage(rows, x, branch, gates):
        mix = None
        for (lo, hi), (lin, gate, z_nsa), (m_ssm, m_nsa) in zip(halves, branch, gates):
            merged = _sigmoid(m_ssm) * (lin * _sigmoid(gate)) + _sigmoid(m_nsa) * z_nsa
            part = dot(merged.astype(BF16), wo_ref[lo:hi, :])
            mix = part if mix is None else mix + part
        y = DEEPNORM_ALPHA * x + (1.0 + gt_ref[0]) * mix
        y_ref[0, rows, :] = _layer_norm(y, lg_ref[...], lb_ref[...])

    staged = None
    for rows in row_blocks + [None]:
        upcoming = first_stage(rows) if rows is not None else None
        if staged is not None:
            second_stage(*staged)
        staged = (rows,) + upcoming if rows is not None else None


def _merge(x, g_ssm, o_nsa, scale1, shift1, gate1, w_m, w_lin, w_gate, w_nsa, w_out, ln_g, ln_b, *, tm):
    bsz, seq, _ = x.shape
    row = pl.BlockSpec((1, 1, D_MODEL), lambda b, i: (b, 0, 0))
    full = lambda a: pl.BlockSpec(a.shape, lambda b, i: (0, 0), pipeline_mode=pl.Buffered(1))
    return pl.pallas_call(
        _merge_kernel,
        grid=(bsz, seq // tm),
        in_specs=[
            pl.BlockSpec((1, tm, D_MODEL), lambda b, i: (b, i, 0)),
            pl.BlockSpec((1, tm, SSM_WIDTH), lambda b, i: (b, i, 0)),
            pl.BlockSpec((1, tm, ATTN_WIDTH), lambda b, i: (b, i, 0)),
            row, row, row,
            full(w_m), full(w_lin), full(w_gate), full(w_nsa), full(w_out), full(ln_g), full(ln_b),
        ],
        out_specs=pl.BlockSpec((1, tm, D_MODEL), lambda b, i: (b, i, 0)),
        out_shape=jax.ShapeDtypeStruct(x.shape, F32),
        compiler_params=_params(2),
        name="merge",
    )(x, g_ssm, o_nsa, scale1, shift1, gate1, w_m, w_lin, w_gate, w_nsa, w_out, ln_g, ln_b)


def _ffn_kernel(x_ref, sc_ref, sh_ref, gt_ref, wg_ref, wu_ref, wd_ref, lg_ref, lb_ref, y_ref, *, n_chunks):
    tm = x_ref.shape[1]
    row_blocks = [slice(r, r + ROW_BLOCK) for r in range(0, tm, ROW_BLOCK)]
    tiles = FFN_HIDDEN // MXU_WIDTH
    edges = [MXU_WIDTH * ((tiles * c + n_chunks - 1) // n_chunks) for c in range(n_chunks + 1)]
    chunks = [slice(lo, hi) for lo, hi in zip(edges[:-1], edges[1:])]
    dot = functools.partial(jnp.dot, preferred_element_type=F32)

    def gate_up(rows):
        x = x_ref[0, rows, :]
        u = (x * (1.0 + sc_ref[0]) + sh_ref[0]).astype(BF16)
        return x, [(dot(u, wg_ref[:, cols]), dot(u, wu_ref[:, cols])) for cols in chunks]

    def down(rows, x, products):
        ffn = None
        for cols, (a, b) in zip(chunks, products):
            part = dot(((a * _sigmoid(a)) * b).astype(BF16), wd_ref[cols, :])
            ffn = part if ffn is None else ffn + part
        y = DEEPNORM_ALPHA * x + (1.0 + gt_ref[0]) * ffn
        y_ref[0, rows, :] = _layer_norm(y, lg_ref[...], lb_ref[...])

    staged = None
    for rows in row_blocks + [None]:
        upcoming = gate_up(rows) if rows is not None else None
        if staged is not None:
            down(*staged)
        staged = (rows,) + upcoming if rows is not None else None


def _ffn(x, scale2, shift2, gate2, w_gate, w_up, w_down, ln_g, ln_b, *, tm, n_chunks):
    bsz, seq, _ = x.shape
    row = pl.BlockSpec((1, 1, D_MODEL), lambda b, i: (b, 0, 0))
    resident = lambda a: pl.BlockSpec(a.shape, lambda b, i: (0, 0), pipeline_mode=pl.Buffered(1))
    return pl.pallas_call(
        functools.partial(_ffn_kernel, n_chunks=n_chunks),
        grid=(bsz, seq // tm),
        in_specs=[
            pl.BlockSpec((1, tm, D_MODEL), lambda b, i: (b, i, 0)),
            row, row, row,
            resident(w_gate), resident(w_up), resident(w_down), resident(ln_g), resident(ln_b),
        ],
        out_specs=pl.BlockSpec((1, tm, D_MODEL), lambda b, i: (b, i, 0)),
        out_shape=jax.ShapeDtypeStruct(x.shape, F32),
        compiler_params=_params(2),
        name="ffn",
    )(x, scale2, shift2, gate2, w_gate, w_up, w_down, ln_g, ln_b)


def _layer(x, c, w_ada, b_ada, w_in, ssm_a_re, ssm_a_im, ssm_log_dt, ssm_b_re, ssm_b_im, ssm_c_re, ssm_c_im,
           ssm_d, w_glu_lin, w_glu_gate, cmp_pe_k, cmp_pe_v, w_cmp_k1, w_cmp_k2, w_cmp_v1, w_cmp_v2,
           w_nsa_proj, w_out, ln1_g, ln1_b, w_ffn_gate, w_ffn_up, w_ffn_down, ln2_g, ln2_b):
    bsz, seq, _ = x.shape
    tq = min(KEY_BLOCK, seq)
    tm = min(512, seq)
    tm_dense = min(1024, seq)

    mod = _ada_mod(c, w_ada, b_ada).reshape(6, bsz, 1, D_MODEL)
    shift1, scale1, gate1, shift2, scale2, gate2 = (mod[k] for k in range(6))

    w_nat, w_tr, w_mg = _split_w_in(w_in)
    u_ssm, kv_cmp, k_aug, qT, v_aug, bgT = _in_proj(x, scale1, shift1, w_nat, w_tr, tm=tm)

    mats = _ssm_matrices(ssm_a_re, ssm_a_im, ssm_log_dt, ssm_b_re, ssm_b_im, ssm_c_re, ssm_c_im, ssm_d)
    g_ssm = _ssm(u_ssm, mats)

    kc3, vcT = _compress(kv_cmp,
                        cmp_pe_k.reshape(1, CMP_BLOCK * HEAD_DIM), cmp_pe_v.reshape(1, CMP_BLOCK * HEAD_DIM),
                        w_cmp_k1.reshape(CMP_BLOCK * HEAD_DIM, -1), w_cmp_k2,
                        w_cmp_v1.reshape(CMP_BLOCK * HEAD_DIM, -1), w_cmp_v2)
    o_nsa = _nsa(qT, kc3, vcT, k_aug, v_aug, bgT, tq=tq)

    x1 = _merge(x, g_ssm, o_nsa, scale1, shift1, gate1, w_mg,
                w_glu_lin.astype(BF16), w_glu_gate.astype(BF16), w_nsa_proj.astype(BF16), w_out.astype(BF16),
                ln1_g.reshape(1, D_MODEL), ln1_b.reshape(1, D_MODEL), tm=tm_dense)
    return _ffn(x1, scale2, shift2, gate2, w_ffn_gate.astype(BF16), w_ffn_up.astype(BF16),
                w_ffn_down.astype(BF16), ln2_g.reshape(1, D_MODEL), ln2_b.reshape(1, D_MODEL), tm=tm_dense, n_chunks=2)


def kernel(x, c, w_ada, b_ada, w_in, ssm_a_re, ssm_a_im, ssm_log_dt, ssm_b_re, ssm_b_im, ssm_c_re, ssm_c_im,
           ssm_d, w_glu_lin, w_glu_gate, cmp_pe_k, cmp_pe_v, w_cmp_k1, w_cmp_k2, w_cmp_v1, w_cmp_v2,
           w_nsa_proj, w_out, ln1_g, ln1_b, w_ffn_gate, w_ffn_up, w_ffn_down, ln2_g, ln2_b):
    for l in range(w_ada.shape[0]):
        x = _layer(x, c, w_ada[l], b_ada[l], w_in[l], ssm_a_re[l], ssm_a_im[l], ssm_log_dt[l],
                   ssm_b_re[l], ssm_b_im[l], ssm_c_re[l], ssm_c_im[l], ssm_d[l],
                   w_glu_lin[l], w_glu_gate[l], cmp_pe_k[l], cmp_pe_v[l],
                   w_cmp_k1[l], w_cmp_k2[l], w_cmp_v1[l], w_cmp_v2[l], w_nsa_proj[l], w_out[l],
                   ln1_g[l], ln1_b[l], w_ffn_gate[l], w_ffn_up[l], w_ffn_down[l], ln2_g[l], ln2_b[l])
    return x
```

```python
import functools
import math

import jax
import jax.numpy as jnp
from jax import lax
from jax.experimental import pallas as pl
from jax.experimental.pallas import tpu as pltpu

F32 = jnp.float32
BF16 = jnp.bfloat16
HIGHEST = lax.Precision.HIGHEST

D_MODEL = 1024
SSM_WIDTH = D_MODEL // 2
SSM_GROUP_SIZE = 16
SSM_GROUPS = SSM_WIDTH // SSM_GROUP_SIZE
SSM_STATE = 64
SSM_CHUNK = 16
SSM_GROUP_TILE = 8
SSM_PREP_GROUPS = 4
SSM_BATCH_TILE = 4
SSM_STATE_ROW_PAD = 8
N_HEADS = 8
HEAD_DIM = 64
N_KV_GROUPS = 2
HEADS_PER_GROUP = N_HEADS // N_KV_GROUPS
ATTN_WIDTH = N_HEADS * HEAD_DIM
KV_WIDTH = N_KV_GROUPS * HEAD_DIM
CMP_BLOCK = 32
CMP_STRIDE = 16
CMP_HIDDEN = HEAD_DIM
SEL_BLOCK = 64
SEL_TOP_K = 8
WINDOW = 256
N_NSA_BRANCHES = 3
GATE_ROWS = 16
FFN_HIDDEN = (8 * D_MODEL + 3 * 256 - 1) // (3 * 256) * 256
DEEPNORM_ALPHA = 2.0 ** 0.25
LN_EPS = 1e-5
MASK_VALUE = -1e30

VMEM_LIMIT = 56 * 1024 * 1024
MXU_WIDTH = 256
BF16_ROWS = 16
ROW_BLOCK = 256

KEY_BLOCK = 256
SEL_SHIFT = SEL_BLOCK.bit_length() - 1
SEL_PER_KEY_BLOCK = KEY_BLOCK // SEL_BLOCK
KEY_AUG = 128
ALIBI_COL = HEAD_DIM
N_PIECES = 4
MASK_COL = HEAD_DIM + 16
QUERY_AUG = HEAD_DIM + 32
VAL_AUG = HEAD_DIM + 16
LOG2E = math.log2(math.e)
NSA_TILES = 2
SCORES_AHEAD = 5

NAT_COLS = SSM_WIDTH + 4 * KV_WIDTH
TR_ROWS = ATTN_WIDTH + 2 * KV_WIDTH + N_KV_GROUPS * GATE_ROWS


def _bf16_pieces(value, n):
    pieces = []
    rest = value
    for _ in range(n):
        mant, expo = math.frexp(rest)
        piece = math.ldexp(round(mant * 256.0) / 256.0, expo)
        pieces.append(piece)
        rest -= piece
    return pieces


def _sigmoid(x):
    return 1.0 / (1.0 + jnp.exp(-x))


def _gelu(x):
    c = math.sqrt(2.0 / math.pi)
    return 0.5 * x * (1.0 + jnp.tanh(c * (x + 0.044715 * (x * x * x))))


def _layer_norm(y, gain, bias):
    mu = jnp.mean(y, axis=-1, keepdims=True)
    d = y - mu
    var = jnp.mean(d * d, axis=-1, keepdims=True)
    return d * lax.rsqrt(var + LN_EPS) * gain + bias


def _params(n_axes):
    return pltpu.CompilerParams(dimension_semantics=("arbitrary",) * n_axes, vmem_limit_bytes=VMEM_LIMIT)


def _ada_kernel(c_ref, w_ref, b_ref, o_ref):
    c = c_ref[...]
    a = c * _sigmoid(c)
    w = w_ref[...].astype(BF16)
    a_hi = a.astype(BF16)
    a_lo = (a - a_hi.astype(F32)).astype(BF16)
    mod = (jnp.dot(a_hi, w, preferred_element_type=F32) + jnp.dot(a_lo, w, preferred_element_type=F32)) + b_ref[...]
    for k in range(o_ref.shape[0]):
        o_ref[k] = mod[:, k * D_MODEL:(k + 1) * D_MODEL]


def _ada_mod(c, w_ada, b_ada):
    bsz = c.shape[0]
    per_step = 2
    return pl.pallas_call(
        _ada_kernel,
        grid=(6 // per_step,),
        in_specs=[
            pl.BlockSpec((bsz, D_MODEL), lambda j: (0, 0)),
            pl.BlockSpec((D_MODEL, per_step * D_MODEL), lambda j: (0, j)),
            pl.BlockSpec((1, per_step * D_MODEL), lambda j: (0, j)),
        ],
        out_specs=pl.BlockSpec((per_step, bsz, D_MODEL), lambda j: (j, 0, 0)),
        out_shape=jax.ShapeDtypeStruct((6, bsz, D_MODEL), F32),
        compiler_params=_params(1),
        name="ada_mod",
    )(c, w_ada, b_ada.reshape(1, 6 * D_MODEL))


def _in_proj_kernel(x_ref, sc_ref, sh_ref, wn_ref, wt_ref, ussm_ref, kc_ref, kp_ref, qT_ref, vp_ref, bgT_ref):
    i = pl.program_id(1)
    tm = x_ref.shape[1]
    u = (x_ref[0] * (1.0 + sc_ref[0]) + sh_ref[0]).astype(BF16)
    nat = jnp.dot(u, wn_ref[...], preferred_element_type=F32)
    ussm_ref[0] = nat[:, :SSM_WIDTH]
    for s in range(2):
        for g in range(N_KV_GROUPS):
            lo = SSM_WIDTH + s * KV_WIDTH + g * HEAD_DIM
            kc_ref[0, s, g] = nat[:, lo:lo + HEAD_DIM]
    pos = i * tm + lax.broadcasted_iota(jnp.int32, (tm, KEY_AUG), 0)
    col = lax.broadcasted_iota(jnp.int32, (tm, KEY_AUG), 1)
    blk = lax.shift_right_logical(pos, SEL_SHIFT)
    in_a = (col >= ALIBI_COL) & (col < ALIBI_COL + N_PIECES)
    in_b = (col >= ALIBI_COL + N_PIECES) & (col < ALIBI_COL + 2 * N_PIECES)
    hot = col == MASK_COL + (blk & (SEL_PER_KEY_BLOCK - 1))
    aux = jnp.where(in_a, blk * SEL_BLOCK, jnp.where(in_b, pos & (SEL_BLOCK - 1), jnp.where(hot, 1, 0))).astype(F32)
    head_lanes = col < HEAD_DIM
    for s in range(2):
        lo = SSM_WIDTH + (2 + s) * KV_WIDTH
        both = nat[:, lo:lo + KV_WIDTH]
        for g in range(N_KV_GROUPS):
            keys = both if g == 0 else pltpu.roll(both, HEAD_DIM, 1)
            kp_ref[0, s, g] = jnp.where(head_lanes, keys, aux).astype(BF16)
    tr = lax.dot_general(wt_ref[...], u, (((1,), (1,)), ((), ())), preferred_element_type=F32)
    qT_ref[0] = tr[:ATTN_WIDTH] * (HEAD_DIM ** -0.5)
    ones_rows = jnp.where(lax.broadcasted_iota(jnp.int32, (VAL_AUG - HEAD_DIM, tm), 0) == 0, 1.0, 0.0)
    for s in range(2):
        for g in range(N_KV_GROUPS):
            lo = ATTN_WIDTH + (s * N_KV_GROUPS + g) * HEAD_DIM
            v_aug = jnp.concatenate([tr[lo:lo + HEAD_DIM], ones_rows], axis=0).astype(BF16)
            for c in range(tm // KEY_BLOCK):
                vp_ref[0, s, g, c] = v_aug[:, c * KEY_BLOCK:(c + 1) * KEY_BLOCK]
    bgT_ref[0] = _sigmoid(tr[ATTN_WIDTH + 2 * KV_WIDTH:])


def _in_proj(x, scale1, shift1, w_nat, w_tr, *, tm):
    bsz, seq, _ = x.shape
    n_t = seq // tm
    n_kb = seq // KEY_BLOCK
    out_shapes = (
        jax.ShapeDtypeStruct((bsz, seq, SSM_WIDTH), F32),
        jax.ShapeDtypeStruct((bsz, 2, N_KV_GROUPS, seq, HEAD_DIM), F32),
        jax.ShapeDtypeStruct((bsz, 2, N_KV_GROUPS, seq, KEY_AUG), BF16),
        jax.ShapeDtypeStruct((bsz, ATTN_WIDTH, seq), F32),
        jax.ShapeDtypeStruct((bsz, 2, N_KV_GROUPS, n_kb, VAL_AUG, KEY_BLOCK), BF16),
        jax.ShapeDtypeStruct((bsz, N_KV_GROUPS * GATE_ROWS, seq), F32),
    )
    return pl.pallas_call(
        _in_proj_kernel,
        grid=(bsz, n_t),
        in_specs=[
            pl.BlockSpec((1, tm, D_MODEL), lambda b, i: (b, i, 0)),
            pl.BlockSpec((1, 1, D_MODEL), lambda b, i: (b, 0, 0)),
            pl.BlockSpec((1, 1, D_MODEL), lambda b, i: (b, 0, 0)),
            pl.BlockSpec((D_MODEL, NAT_COLS), lambda b, i: (0, 0)),
            pl.BlockSpec((TR_ROWS, D_MODEL), lambda b, i: (0, 0)),
        ],
        out_specs=(
            pl.BlockSpec((1, tm, SSM_WIDTH), lambda b, i: (b, i, 0)),
            pl.BlockSpec((1, 2, N_KV_GROUPS, tm, HEAD_DIM), lambda b, i: (b, 0, 0, i, 0)),
            pl.BlockSpec((1, 2, N_KV_GROUPS, tm, KEY_AUG), lambda b, i: (b, 0, 0, i, 0)),
            pl.BlockSpec((1, ATTN_WIDTH, tm), lambda b, i: (b, 0, i)),
            pl.BlockSpec((1, 2, N_KV_GROUPS, tm // KEY_BLOCK, VAL_AUG, KEY_BLOCK), lambda b, i: (b, 0, 0, i, 0, 0)),
            pl.BlockSpec((1, N_KV_GROUPS * GATE_ROWS, tm), lambda b, i: (b, 0, i)),
        ),
        out_shape=out_shapes,
        compiler_params=_params(2),
        name="in_proj",
    )(x, scale1, shift1, w_nat, w_tr)


def _split_w_in(w_in):
    o_q = SSM_WIDTH
    o_kv = o_q + ATTN_WIDTH
    o_bg = o_kv + 6 * KV_WIDTH
    o_mg = o_bg + N_NSA_BRANCHES * N_HEADS
    kv = [w_in[:, o_kv + s * KV_WIDTH:o_kv + (s + 1) * KV_WIDTH] for s in range(6)]
    w_nat = jnp.concatenate([w_in[:, :o_q], kv[0], kv[1], kv[2], kv[4]], axis=1)
    per_group = HEADS_PER_GROUP * N_NSA_BRANCHES
    bg = w_in[:, o_bg:o_mg].reshape(D_MODEL, N_KV_GROUPS, per_group)
    bg = jnp.pad(bg, ((0, 0), (0, 0), (0, GATE_ROWS - per_group))).reshape(D_MODEL, N_KV_GROUPS * GATE_ROWS)
    w_tr = jnp.concatenate([w_in[:, o_q:o_kv], kv[3], kv[5], bg], axis=1).T
    return w_nat.astype(BF16), w_tr.astype(BF16), w_in[:, o_mg:].astype(BF16)


def _ssm_prep_kernel(*refs):
    for k in range(refs[0].shape[0]):
        _ssm_prep_group(*[ref.at[pl.ds(k, 1)] for ref in refs])


def _ssm_prep_group(a_ref, logdt_ref, bT_ref, c_ref, d_ref, mT_ref, pT_ref, qT_ref, a16_ref):
    t_n = SSM_CHUNK
    c_n = SSM_GROUP_SIZE
    n_p = SSM_STATE
    rows = t_n * c_n
    dt = jnp.exp(logdt_ref[0])
    a_re = a_ref[0, 0]
    a_im = a_ref[0, 1]
    decay = jnp.exp(a_re * dt)
    ab_re = decay * jnp.cos(a_im * dt)
    ab_im = decay * jnp.sin(a_im * dt)
    n_re = ab_re - 1.0
    denom = a_re * a_re + a_im * a_im
    f_re = (n_re * a_re + ab_im * a_im) / denom
    f_im = (ab_im * a_re - n_re * a_im) / denom
    powers = [(jnp.ones_like(ab_re), jnp.zeros_like(ab_re))]
    for _ in range(t_n):
        p_re, p_im = powers[-1]
        powers.append((p_re * ab_re - p_im * ab_im, p_re * ab_im + p_im * ab_re))

    def per_token(first, step):
        picks = [powers[first + step * k] for k in range(t_n)]
        return (jnp.concatenate([jnp.broadcast_to(p[0], (c_n, 2 * n_p)) for p in picks], axis=0),
                jnp.concatenate([jnp.broadcast_to(p[1], (c_n, 2 * n_p)) for p in picks], axis=0))

    def cmul(x_re, x_im, y_re, y_im):
        return x_re * y_re - x_im * y_im, x_re * y_im + x_im * y_re

    bb_re, bb_im = cmul(f_re, f_im, bT_ref[0, 0], bT_ref[0, 1])
    bt_re = jnp.concatenate([bb_re] * t_n, axis=0)
    bt_im = jnp.concatenate([bb_im] * t_n, axis=0)
    ct_re = jnp.concatenate([c_ref[0, 0]] * t_n, axis=0)
    ct_im = jnp.concatenate([c_ref[0, 1]] * t_n, axis=0)

    p_re, p_im = cmul(*per_token(t_n - 1, -1), bt_re, bt_im)
    low_half = lax.broadcasted_iota(jnp.int32, (1, 2 * n_p), 1) < n_p
    pT_ref[0] = jnp.where(low_half, p_re, p_im).T.astype(BF16)

    q_re, q_im = cmul(ct_re, ct_im, *per_token(1, 1))
    qT_ref[0] = jnp.where(low_half, q_re, -q_im).astype(BF16)
    a16_ref[0] = jnp.concatenate([powers[t_n][0][:, :n_p], powers[t_n][1][:, :n_p]], axis=0)

    k_re, k_im = cmul(ct_re, ct_im, *per_token(0, 1))
    nt = (((1,), (1,)), ((), ()))
    strips = (lax.dot_general(_split3(k_re[:, :n_p]), _stack3(bt_re[:, :n_p], axis=1), nt, preferred_element_type=F32)
              - lax.dot_general(_split3(k_im[:, :n_p]), _stack3(bt_im[:, :n_p], axis=1), nt,
                                preferred_element_type=F32))
    lane = lax.broadcasted_iota(jnp.int32, (1, rows), 1)
    s_lane = lax.shift_right_logical(lane, c_n.bit_length() - 1)
    ci_lane = lane & (c_n - 1)
    row = lax.broadcasted_iota(jnp.int32, (rows, 1), 0)
    t_row = lax.shift_right_logical(row, c_n.bit_length() - 1)
    co_row = row & (c_n - 1)
    blocks = []
    for t in range(t_n):
        acc = jnp.zeros((c_n, rows), F32)
        for lag in range(t + 1):
            acc = jnp.where(s_lane == t - lag, strips[lag * c_n:(lag + 1) * c_n, :], acc)
        blocks.append(acc)
    d_rows = jnp.concatenate([d_ref[0]] * t_n, axis=0)
    skip = jnp.where((t_row == s_lane) & (co_row == ci_lane), d_rows, 0.0)
    mT_ref[0] = (jnp.concatenate(blocks, axis=0) + skip).astype(BF16)


def _ssm_matrices(a_re, a_im, log_dt, b_re, b_im, c_re, c_im, d_skip):
    n_g, n_p = a_re.shape
    c_n = SSM_GROUP_SIZE
    rows = SSM_CHUNK * c_n
    twice = lambda x: jnp.concatenate([x, x], axis=-1).astype(F32)
    a = twice(jnp.stack([a_re, a_im], axis=1)).reshape(n_g, 2, 1, 2 * n_p)
    bT = twice(jnp.swapaxes(jnp.stack([b_re, b_im], axis=1), 2, 3))
    c = twice(jnp.stack([c_re, c_im], axis=1))
    per_step = SSM_PREP_GROUPS
    spec = lambda shape: pl.BlockSpec((per_step,) + shape, lambda g: (g,) + (0,) * len(shape))
    mT, pT, qT, a16 = pl.pallas_call(
        _ssm_prep_kernel,
        grid=(n_g // per_step,),
        in_specs=[spec((2, 1, 2 * n_p)), spec((1, 1)), spec((2, c_n, 2 * n_p)), spec((2, c_n, 2 * n_p)),
                  spec((c_n, 1))],
        out_specs=(spec((rows, rows)), spec((2 * n_p, rows)), spec((rows, 2 * n_p)), spec((2, n_p))),
        out_shape=(
            jax.ShapeDtypeStruct((n_g, rows, rows), BF16),
            jax.ShapeDtypeStruct((n_g, 2 * n_p, rows), BF16),
            jax.ShapeDtypeStruct((n_g, rows, 2 * n_p), BF16),
            jax.ShapeDtypeStruct((n_g, 2, n_p), F32),
        ),
        compiler_params=_params(1),
        name="ssm_prep",
    )(a, log_dt.astype(F32).reshape(n_g, 1, 1), bT, c, d_skip.astype(F32).reshape(n_g, c_n, 1))
    return mT, pT.reshape(n_g, 2, n_p, rows), qT, a16


def _ssm_kernel(u_ref, mT_ref, pT_ref, qT_ref, a16_ref, o_ref, ut_scr, yt_scr, sre, sim, *, n_chunks):
    n_g = SSM_GROUP_TILE
    n_b = u_ref.shape[0]
    t_n = SSM_CHUNK
    c_n = SSM_GROUP_SIZE
    lanes = [(e, g) for e in range(n_b) for g in range(n_g)]

    for e in range(n_b):
        for s in range(t_n):
            x_s = u_ref[e, pl.ds(s, n_chunks, stride=t_n), :]
            ut_scr[e * n_g:(e + 1) * n_g, s * c_n:(s + 1) * c_n, :] = (
                x_s.T.reshape(n_g, c_n, n_chunks).astype(BF16))

    pitch = n_chunks + SSM_STATE_ROW_PAD

    def lane_rows(k):
        return pl.ds(k * pitch, n_chunks)

    for k, (e, g) in enumerate(lanes):
        ut = ut_scr[k]
        sre[lane_rows(k), :] = jnp.dot(pT_ref[g, 0], ut, preferred_element_type=F32).T
        sim[lane_rows(k), :] = jnp.dot(pT_ref[g, 1], ut, preferred_element_type=F32).T

    a_r = jnp.concatenate([a16_ref[pl.ds(0, n_g, stride=2), :]] * n_b, axis=0)
    a_i = jnp.concatenate([a16_ref[pl.ds(1, n_g, stride=2), :]] * n_b, axis=0)

    def carry_states(c, h):
        h_r, h_i = h
        rows = pl.ds(c, len(lanes), stride=pitch)
        s_r = sre[rows, :]
        s_i = sim[rows, :]
        sre[rows, :] = h_r
        sim[rows, :] = h_i
        return a_r * h_r - a_i * h_i + s_r, a_r * h_i + a_i * h_r + s_i

    zero = jnp.zeros((len(lanes), SSM_STATE), F32)
    lax.fori_loop(0, n_chunks, carry_states, (zero, zero), unroll=4)

    nt = (((1,), (1,)), ((), ()))
    for k, (e, g) in enumerate(lanes):
        yt = jnp.dot(mT_ref[g], ut_scr[k], preferred_element_type=F32)
        entering = jnp.concatenate([sre[lane_rows(k), :], sim[lane_rows(k), :]], axis=1).astype(BF16)
        yt = yt + lax.dot_general(qT_ref[g], entering, nt, preferred_element_type=F32)
        yt_scr[k] = _gelu(yt)

    for e in range(n_b):
        for t in range(t_n):
            z = yt_scr[e * n_g:(e + 1) * n_g, t * c_n:(t + 1) * c_n, :].reshape(n_g * c_n, n_chunks)
            o_ref[e, pl.ds(t, n_chunks, stride=t_n), :] = z.T


def _ssm(u_ssm, mats):
    bsz, seq, width = u_ssm.shape
    n_chunks = seq // SSM_CHUNK
    mT, pT, qT, a16 = mats
    gt = SSM_GROUP_TILE
    n_tiles = SSM_GROUPS // gt
    rows = SSM_CHUNK * SSM_GROUP_SIZE
    a16 = a16.reshape(2 * SSM_GROUPS, SSM_STATE)
    bt = SSM_BATCH_TILE if bsz % SSM_BATCH_TILE == 0 else 1
    n_lanes = bt * gt
    per_tile = lambda a: pl.BlockSpec((gt,) + a.shape[1:], lambda t, b: (t,) + (0,) * (a.ndim - 1))
    return pl.pallas_call(
        functools.partial(_ssm_kernel, n_chunks=n_chunks),
        grid=(n_tiles, bsz // bt),
        in_specs=[
            pl.BlockSpec((bt, seq, gt * SSM_GROUP_SIZE), lambda t, b: (b, 0, t)),
            per_tile(mT), per_tile(pT), per_tile(qT),
            pl.BlockSpec((2 * gt, SSM_STATE), lambda t, b: (t, 0)),
        ],
        out_specs=pl.BlockSpec((bt, seq, gt * SSM_GROUP_SIZE), lambda t, b: (b, 0, t)),
        out_shape=jax.ShapeDtypeStruct(u_ssm.shape, F32),
        scratch_shapes=[
            pltpu.VMEM((n_lanes, rows, n_chunks), BF16),
            pltpu.VMEM((n_lanes, rows, n_chunks), F32),
            pltpu.VMEM((n_lanes * (n_chunks + SSM_STATE_ROW_PAD), SSM_STATE), F32),
            pltpu.VMEM((n_lanes * (n_chunks + SSM_STATE_ROW_PAD), SSM_STATE), F32),
        ],
        compiler_params=_params(2),
        name="ssm",
    )(u_ssm, mT, pT, qT, a16)


def _split3(x):
    hi = x.astype(BF16)
    lo = (x - hi.astype(F32)).astype(BF16)
    return jnp.concatenate([hi, lo, hi], axis=-1)


def _stack3(w, axis=0):
    hi = w.astype(BF16)
    lo = (w - hi.astype(F32)).astype(BF16)
    return jnp.concatenate([hi, hi, lo], axis=axis)


def _compress_kernel(kx_ref, vx_ref, pek_ref, pev_ref, wk1_ref, wk2_ref, wv1_ref, wv2_ref, kc_ref, vcT_ref):
    def mlp(x_ref, g, pe_ref, w1_ref, w2_ref):
        n = x_ref.shape[3] // CMP_STRIDE
        x = jnp.concatenate([x_ref[0, 0, g, pl.ds(l, n, stride=CMP_STRIDE), :] for l in range(CMP_STRIDE)], axis=1)
        both = jnp.dot(_split3(x), w1_ref[...], preferred_element_type=F32)
        pe_both = jnp.dot(_split3(pe_ref[...]), w1_ref[...], preferred_element_type=F32)
        bias = pe_both[0:1, :CMP_HIDDEN] + pe_both[1:2, CMP_HIDDEN:]
        h = _gelu(both[:, :CMP_HIDDEN] + pltpu.roll(both[:, CMP_HIDDEN:], n - 1, 0) + bias)
        return jnp.dot(_split3(h), w2_ref[...], preferred_element_type=F32)

    for g in range(kx_ref.shape[2]):
        kc = mlp(kx_ref, g, pek_ref, wk1_ref, wk2_ref)
        kc_hi = kc.astype(BF16)
        kc_lo = (kc - kc_hi.astype(F32)).astype(BF16)
        kc_ref[0, g] = jnp.concatenate([kc_hi, kc_hi, kc_lo, jnp.zeros_like(kc_hi)], axis=1)
        vcT_ref[0, g] = mlp(vx_ref, g, pev_ref, wv1_ref, wv2_ref).T.astype(BF16)


def _compress(kv, pe_k, pe_v, wk1, wk2, wv1, wv2):
    bsz, _, n_g, seq, width = kv.shape
    n_chunks = seq // CMP_STRIDE
    half = CMP_STRIDE * HEAD_DIM
    side_by_side = lambda w1: _stack3(jnp.concatenate([w1[:half], w1[half:]], axis=1))
    two_rows = lambda pe: jnp.pad(pe.reshape(2, half), ((0, BF16_ROWS - 2), (0, 0)))
    pe_k, pe_v = two_rows(pe_k), two_rows(pe_v)
    wk1, wv1, wk2, wv2 = side_by_side(wk1), side_by_side(wv1), _stack3(wk2), _stack3(wv2)
    blk = lambda s: pl.BlockSpec((1, 1, n_g, seq, width), lambda b: (b, s, 0, 0, 0))
    full = lambda a: pl.BlockSpec(a.shape, lambda b: (0, 0))
    return pl.pallas_call(
        _compress_kernel,
        grid=(bsz,),
        in_specs=[blk(0), blk(1), full(pe_k), full(pe_v), full(wk1), full(wk2), full(wv1), full(wv2)],
        out_specs=(
            pl.BlockSpec((1, n_g, n_chunks, 4 * HEAD_DIM), lambda b: (b, 0, 0, 0)),
            pl.BlockSpec((1, n_g, HEAD_DIM, n_chunks), lambda b: (b, 0, 0, 0)),
        ),
        out_shape=(
            jax.ShapeDtypeStruct((bsz, n_g, n_chunks, 4 * HEAD_DIM), BF16),
            jax.ShapeDtypeStruct((bsz, n_g, HEAD_DIM, n_chunks), BF16),
        ),
        compiler_params=_params(1),
        name="compress",
    )(kv, kv, pe_k, pe_v, wk1, wk2, wv1, wv2)


def _pipeline(units):
    issue = lambda unit: None if unit[0] is None else unit[0]()
    pending = [issue(unit) for unit in units[:SCORES_AHEAD]]
    for n, unit in enumerate(units):
        if n + SCORES_AHEAD < len(units):
            pending.append(issue(units[n + SCORES_AHEAD]))
        unit[1](pending.pop(0))


def _nsa_kernel(qT_ref, kc_ref, vcT_ref, ks_ref, vs_ref, kw_ref, vw_ref, bg_ref, o_ref,
                negm_scr, qa_scr, m_scr, acc_scr, out_scr, *smem, tq, n_cmp, n_sel):
    gw = HEADS_PER_GROUP * HEAD_DIM
    groups = []
    for tile in range(NSA_TILES):
        i = pl.program_id(1) * NSA_TILES + tile
        span = slice(tile * tq, (tile + 1) * tq)
        for g in range(N_KV_GROUPS):
            k = tile * N_KV_GROUPS + g
            groups.append(_nsa_group(
                g, i, qT_ref.at[:, g * gw:(g + 1) * gw, span], kc_ref.at[:, g:g + 1], vcT_ref.at[:, g:g + 1],
                ks_ref.at[:, :, g:g + 1], vs_ref.at[:, :, g:g + 1], kw_ref.at[:, :, g:g + 1],
                vw_ref.at[:, :, g:g + 1], bg_ref.at[:, g:g + 1, :, span], o_ref.at[:, span, g * gw:(g + 1) * gw],
                negm_scr.at[k], qa_scr.at[k], m_scr.at[k], acc_scr.at[k], out_scr.at[k],
                smem[2 * k], smem[2 * k + 1], tq=tq, n_cmp=n_cmp, n_sel=n_sel))
    stages = [[unit for stage_units, _ in groups for unit in stage_units[stage]]
              for stage in range(3)]
    _pipeline(stages[0] + stages[1] + stages[2])
    for _, finish_group in groups:
        finish_group()


def _nsa_group(g, i, qT_ref, kc_ref, vcT_ref, ks_ref, vs_ref, kw_ref, vw_ref, bg_ref, o_ref,
               negm_scr, qa_scr, m_scr, acc_scr, out_scr, used_smem, todo_smem, *, tq, n_cmp, n_sel):
    t_row = i * tq + lax.broadcasted_iota(jnp.int32, (1, tq), 1)
    slopes = [2.0 ** -(g * HEADS_PER_GROUP + hh + 1) for hh in range(HEADS_PER_GROUP)]

    def gate(hh, branch):
        r = hh * N_NSA_BRANCHES + branch
        return bg_ref[0, 0, r:r + 1, :]

    def head_rows(hh):
        return slice(hh * HEAD_DIM, (hh + 1) * HEAD_DIM)

    def head_lanes(hh):
        return slice(hh * tq, (hh + 1) * tq)

    kc3 = kc_ref[0, 0]
    vcT = vcT_ref[0, 0]
    cmp_end = lax.broadcasted_iota(jnp.int32, (n_cmp, 1), 0) * CMP_STRIDE + (CMP_BLOCK - 1)
    dist_c = (t_row - cmp_end).astype(F32)
    valid_c = dist_c >= 0.0
    piece = lax.broadcasted_iota(jnp.int32, (16, 1), 0)
    log2e_pieces = _bf16_pieces(LOG2E, N_PIECES)
    log2e_col = jnp.zeros((16, 1), F32)
    for k, value in enumerate(log2e_pieces):
        log2e_col = jnp.where((piece == k) | (piece == N_PIECES + k), value, log2e_col)
    for hh in range(HEADS_PER_GROUP):
        alibi = jnp.broadcast_to(log2e_col * slopes[hh], (16, tq)).astype(BF16)
        q_l2 = (qT_ref[0, head_rows(hh), :] * LOG2E).astype(BF16)
        qa_scr[:, head_lanes(hh)] = jnp.concatenate([q_l2, alibi], axis=0)

    def cmp_scores(hh):
        q_h = qT_ref[0, head_rows(hh), :]
        q_hi = q_h.astype(BF16)
        q_lo = (q_h - q_hi.astype(F32)).astype(BF16)
        q3 = jnp.concatenate([q_hi, q_lo, q_hi, jnp.zeros_like(q_hi)], axis=0)
        return jnp.dot(kc3, q3, preferred_element_type=F32)

    def cmp_attend(p_heads, hh, s):
        s = jnp.where(valid_c, s - slopes[hh] * dist_c, MASK_VALUE)
        m = jnp.max(s, axis=0, keepdims=True)
        p = jnp.where(valid_c, jnp.exp(s - m), 0.0)
        l = jnp.sum(p, axis=0, keepdims=True)
        p = p / jnp.where(l > 0.0, l, 1.0)
        o_c = jnp.dot(vcT, p.astype(BF16), preferred_element_type=F32)
        out_scr[head_rows(hh), :] = gate(hh, 0) * o_c
        p_heads.append(p)

    def select_blocks(p_heads, _):
        p_sum = sum(p_heads[1:], p_heads[0])
        blk = lax.broadcasted_iota(jnp.int32, (n_sel, n_cmp), 0)
        cmp_ix = lax.broadcasted_iota(jnp.int32, (n_sel, n_cmp), 1)
        ratio = SEL_BLOCK // CMP_STRIDE
        extra = CMP_BLOCK // CMP_STRIDE - 1
        overlap_t = jnp.where((cmp_ix >= ratio * blk - extra) & (cmp_ix <= ratio * blk + ratio - 1),
                              1.0, 0.0).astype(BF16)
        imp = jnp.zeros((n_sel, tq), F32)
        rest = p_sum
        for _ in range(3):
            part = rest.astype(BF16)
            imp = imp + jnp.dot(overlap_t, part, preferred_element_type=F32)
            rest = rest - part.astype(F32)

        j_col = lax.broadcasted_iota(jnp.int32, (n_sel, 1), 0)
        forced = (j_col == 0) | (j_col == lax.shift_right_logical(t_row, SEL_SHIFT))
        future = j_col * SEL_BLOCK > t_row
        score = jnp.where(forced, jnp.inf, jnp.where(future, -jnp.inf, imp))
        rank = jnp.zeros((n_sel, tq), F32)
        for jp in range(n_sel):
            other = score[jp:jp + 1, :]
            ahead = (other > score) | ((other == score) & (j_col > jp))
            rank = rank + jnp.where(ahead, 1.0, 0.0)
        neg = jnp.where(rank < float(min(SEL_TOP_K, n_sel)), 0.0, MASK_VALUE)
        pad_rows = jnp.zeros((BF16_ROWS - SEL_PER_KEY_BLOCK, tq), F32)
        for c in range(n_sel // SEL_PER_KEY_BLOCK):
            rows = neg[c * SEL_PER_KEY_BLOCK:(c + 1) * SEL_PER_KEY_BLOCK, :]
            negm_scr[c] = jnp.concatenate([rows, pad_rows], axis=0).astype(BF16)
            used_smem[c] = (jnp.max(rows) > 0.5 * MASK_VALUE).astype(jnp.int32)

    rel = (lax.broadcasted_iota(jnp.int32, (KEY_BLOCK, tq), 1)
           - lax.broadcasted_iota(jnp.int32, (KEY_BLOCK, tq), 0))
    zero_tail = jnp.zeros((KEY_AUG - QUERY_AUG, tq), BF16)
    no_mask = jnp.zeros((BF16_ROWS, tq), BF16)
    all_masked = jnp.full((BF16_ROWS, tq), MASK_VALUE, BF16)

    sel_even, sel_odd, win_prev, win_diag = range(4)

    def scores(job, hh):
        k_ref, _, kb, enabled, _, _ = job
        k0 = pl.multiple_of(kb * KEY_BLOCK, KEY_BLOCK)
        k_aug = k_ref[0, 0, 0, pl.ds(k0, KEY_BLOCK), :]
        if enabled is None:
            mask_tile = no_mask
        elif enabled is True:
            mask_tile = negm_scr[kb]
        else:
            mask_tile = jnp.where(enabled, negm_scr[kb], all_masked)
        q_aug = jnp.concatenate([qa_scr[:, head_lanes(hh)], mask_tile, zero_tail], axis=0)
        return jnp.dot(k_aug, q_aug, preferred_element_type=F32)

    def accumulate(job, first, hh, s):
        _, v_ref, kb, _, valid, slot = job
        cols = head_lanes(hh)
        if valid is not None:
            s = jnp.where(valid, s, MASK_VALUE)
        m_new = jnp.max(s, axis=0, keepdims=True)
        if not first:
            m_old = m_scr[slot, :, cols]
            m_new = jnp.maximum(m_old, m_new)
        p = jnp.exp2(s - jnp.maximum(m_new, 0.1 * MASK_VALUE))
        pv = jnp.dot(v_ref[0, 0, 0, kb], p.astype(BF16), preferred_element_type=F32)
        if not first:
            pv = jnp.exp2(m_old - m_new) * acc_scr[slot, :, cols] + pv
        acc_scr[slot, :, cols] = pv
        m_scr[slot, :, cols] = m_new

    def attention_units(jobs, first):
        return [(functools.partial(scores, job, hh), functools.partial(accumulate, job, first, hh))
                for job in jobs for hh in range(HEADS_PER_GROUP)]

    def finish(branch, slot_a, slot_b):
        m_a = m_scr[slot_a]
        m_b = m_scr[slot_b]
        m = jnp.maximum(m_a, m_b)
        w_a = jnp.exp2(m_a - m)
        w_b = jnp.exp2(m_b - m)
        for hh in range(HEADS_PER_GROUP):
            cols = head_lanes(hh)
            acc = acc_scr[slot_a, :, cols] * w_a[:, cols] + acc_scr[slot_b, :, cols] * w_b[:, cols]
            l = acc[HEAD_DIM:HEAD_DIM + 1, :]
            o_b = acc[:HEAD_DIM, :] / jnp.where(l > 0.0, l, 1.0)
            out_scr[head_rows(hh), :] = out_scr[head_rows(hh), :] + gate(hh, branch) * o_b

    prev = jnp.maximum(i - 1, 0)
    causal = rel >= 0
    off_without_prev = jnp.where(i > 0, 0, WINDOW + KEY_BLOCK)
    dist_prev = rel + KEY_BLOCK + off_without_prev
    p_heads = []
    stage_units = (
        [(functools.partial(cmp_scores, hh), functools.partial(cmp_attend, p_heads, hh))
         for hh in range(HEADS_PER_GROUP)] + [(None, functools.partial(select_blocks, p_heads))],
        attention_units([(kw_ref, vw_ref, prev, None, dist_prev < WINDOW, win_prev),
                         (kw_ref, vw_ref, i, None, causal, win_diag)], first=True),
        attention_units([(ks_ref, vs_ref, prev, i > 0, None, sel_even),
                         (ks_ref, vs_ref, i, True, causal, sel_odd)], first=True))

    def finish_group():
        n_todo = jnp.int32(0)
        for c in range(used_smem.shape[0]):
            take = jnp.logical_and(c < i - 1, used_smem[c] > 0)

            @pl.when(take)
            def _(c=c, slot=n_todo):
                todo_smem[slot] = c

            n_todo = n_todo + take.astype(jnp.int32)
        todo_smem[n_todo] = 0

        def past_pair(j, carry):
            _pipeline(attention_units([(ks_ref, vs_ref, todo_smem[2 * j], True, None, sel_even),
                                       (ks_ref, vs_ref, todo_smem[2 * j + 1], 2 * j + 1 < n_todo, None, sel_odd)],
                                      first=False))
            return carry

        lax.fori_loop(0, lax.shift_right_logical(n_todo + 1, 1), past_pair, 0)
        finish(1, sel_even, sel_odd)
        finish(2, win_prev, win_diag)
        o_ref[0] = out_scr[...].T

    return stage_units, finish_group


def _nsa(qT, kc3, vcT, k_aug, v_aug, bgT, *, tq):
    bsz, _, seq = qT.shape
    assert tq == KEY_BLOCK == WINDOW and seq % tq == 0
    n_cmp = kc3.shape[2]
    n_sel = seq // SEL_BLOCK
    n_kb = seq // KEY_BLOCK
    gw = HEADS_PER_GROUP * HEAD_DIM
    bg4 = bgT.reshape(bsz, N_KV_GROUPS, GATE_ROWS, seq)
    n_g = N_KV_GROUPS
    k_spec = lambda s: pl.BlockSpec((1, 1, n_g, seq, KEY_AUG), lambda b, i: (b, s, 0, 0, 0))
    v_spec = lambda s: pl.BlockSpec((1, 1, n_g, n_kb, VAL_AUG, KEY_BLOCK), lambda b, i: (b, s, 0, 0, 0, 0))
    n_inst = NSA_TILES * n_g
    step_q = NSA_TILES * tq
    assert seq % step_q == 0
    smem = []
    for _ in range(n_inst):
        smem += [pltpu.SMEM((n_kb,), jnp.int32),
                 pltpu.SMEM((n_kb + 1,), jnp.int32)]
    return pl.pallas_call(
        functools.partial(_nsa_kernel, tq=tq, n_cmp=n_cmp, n_sel=n_sel),
        grid=(bsz, seq // step_q),
        in_specs=[
            pl.BlockSpec((1, n_g * gw, step_q), lambda b, i: (b, 0, i)),
            pl.BlockSpec((1, n_g, n_cmp, 4 * HEAD_DIM), lambda b, i: (b, 0, 0, 0)),
            pl.BlockSpec((1, n_g, HEAD_DIM, n_cmp), lambda b, i: (b, 0, 0, 0)),
            k_spec(0), v_spec(0),
            k_spec(1), v_spec(1),
            pl.BlockSpec((1, n_g, GATE_ROWS, step_q), lambda b, i: (b, 0, 0, i)),
        ],
        out_specs=pl.BlockSpec((1, step_q, n_g * gw), lambda b, i: (b, i, 0)),
        out_shape=jax.ShapeDtypeStruct((bsz, seq, ATTN_WIDTH), F32),
        scratch_shapes=[
            pltpu.VMEM((n_inst, n_kb, 16, tq), BF16),
            pltpu.VMEM((n_inst, HEAD_DIM + 16, HEADS_PER_GROUP * tq), BF16),
            pltpu.VMEM((n_inst, 4, 1, HEADS_PER_GROUP * tq), F32),
            pltpu.VMEM((n_inst, 4, VAL_AUG, HEADS_PER_GROUP * tq), F32),
            pltpu.VMEM((n_inst, gw, tq), F32),
        ] + smem,
        compiler_params=_params(2),
        name="nsa",
    )(qT, kc3, vcT, k_aug, v_aug, k_aug, v_aug, bg4)


def _merge_kernel(x_ref, g_ref, o_ref_in, sc_ref, sh_ref, gt_ref, wm_ref, wl_ref, wg_ref, wn_ref, wo_ref,
                  lg_ref, lb_ref, y_ref):
    tm = x_ref.shape[1]
    row_blocks = [slice(r, r + ROW_BLOCK) for r in range(0, tm, ROW_BLOCK)]
    halves = [(0, D_MODEL // 2), (D_MODEL // 2, D_MODEL)]
    dot = functools.partial(jnp.dot, preferred_element_type=F32)

    def first_stage(rows):
        g = g_ref[0, rows, :].astype(BF16)
        o = o_ref_in[0, rows, :].astype(BF16)
        branch = [(dot(g, wl_ref[:, lo:hi]), dot(g, wg_ref[:, lo:hi]), dot(o, wn_ref[:, lo:hi])) for lo, hi in halves]
        x = x_ref[0, rows, :]
        u = (x * (1.0 + sc_ref[0]) + sh_ref[0]).astype(BF16)
        gates = [(dot(u, wm_ref[:, lo:hi]), dot(u, wm_ref[:, D_MODEL + lo:D_MODEL + hi])) for lo, hi in halves]
        return x, branch, gates

    def second_stage(rows, x, branch, gates):
        mix = None
        for (lo, hi), (lin, gate, z_nsa), (m_ssm, m_nsa) in zip(halves, branch, gates):
            merged = _sigmoid(m_ssm) * (lin * _sigmoid(gate)) + _sigmoid(m_nsa) * z_nsa
            part = dot(merged.astype(BF16), wo_ref[lo:hi, :])
            mix = part if mix is None else mix + part
        y = DEEPNORM_ALPHA * x + (1.0 + gt_ref[0]) * mix
        y_ref[0, rows, :] = _layer_norm(y, lg_ref[...], lb_ref[...])

    staged = None
    for rows in row_blocks + [None]:
        upcoming = first_stage(rows) if rows is not None else None
        if staged is not None:
            second_stage(*staged)
        staged = (rows,) + upcoming if rows is not None else None


def _merge(x, g_ssm, o_nsa, scale1, shift1, gate1, w_m, w_lin, w_gate, w_nsa, w_out, ln_g, ln_b, *, tm):
    bsz, seq, _ = x.shape
    row = pl.BlockSpec((1, 1, D_MODEL), lambda b, i: (b, 0, 0))
    full = lambda a: pl.BlockSpec(a.shape, lambda b, i: (0, 0), pipeline_mode=pl.Buffered(1))
    return pl.pallas_call(
        _merge_kernel,
        grid=(bsz, seq // tm),
        in_specs=[
            pl.BlockSpec((1, tm, D_MODEL), lambda b, i: (b, i, 0)),
            pl.BlockSpec((1, tm, SSM_WIDTH), lambda b, i: (b, i, 0)),
            pl.BlockSpec((1, tm, ATTN_WIDTH), lambda b, i: (b, i, 0)),
            row, row, row,
            full(w_m), full(w_lin), full(w_gate), full(w_nsa), full(w_out), full(ln_g), full(ln_b),
        ],
        out_specs=pl.BlockSpec((1, tm, D_MODEL), lambda b, i: (b, i, 0)),
        out_shape=jax.ShapeDtypeStruct(x.shape, F32),
        compiler_params=_params(2),
        name="merge",
    )(x, g_ssm, o_nsa, scale1, shift1, gate1, w_m, w_lin, w_gate, w_nsa, w_out, ln_g, ln_b)


def _ffn_kernel(x_ref, sc_ref, sh_ref, gt_ref, wg_ref, wu_ref, wd_ref, lg_ref, lb_ref, y_ref, *, n_chunks):
    tm = x_ref.shape[1]
    row_blocks = [slice(r, r + ROW_BLOCK) for r in range(0, tm, ROW_BLOCK)]
    tiles = FFN_HIDDEN // MXU_WIDTH
    edges = [MXU_WIDTH * ((tiles * c + n_chunks - 1) // n_chunks) for c in range(n_chunks + 1)]
    chunks = [slice(lo, hi) for lo, hi in zip(edges[:-1], edges[1:])]
    dot = functools.partial(jnp.dot, preferred_element_type=F32)

    def gate_up(rows):
        x = x_ref[0, rows, :]
        u = (x * (1.0 + sc_ref[0]) + sh_ref[0]).astype(BF16)
        return x, [(dot(u, wg_ref[:, cols]), dot(u, wu_ref[:, cols])) for cols in chunks]

    def down(rows, x, products):
        ffn = None
        for cols, (a, b) in zip(chunks, products):
            part = dot(((a * _sigmoid(a)) * b).astype(BF16), wd_ref[cols, :])
            ffn = part if ffn is None else ffn + part
        y = DEEPNORM_ALPHA * x + (1.0 + gt_ref[0]) * ffn
        y_ref[0, rows, :] = _layer_norm(y, lg_ref[...], lb_ref[...])

    staged = None
    for rows in row_blocks + [None]:
        upcoming = gate_up(rows) if rows is not None else None
        if staged is not None:
            down(*staged)
        staged = (rows,) + upcoming if rows is not None else None


def _ffn(x, scale2, shift2, gate2, w_gate, w_up, w_down, ln_g, ln_b, *, tm, n_chunks):
    bsz, seq, _ = x.shape
    row = pl.BlockSpec((1, 1, D_MODEL), lambda b, i: (b, 0, 0))
    resident = lambda a: pl.BlockSpec(a.shape, lambda b, i: (0, 0), pipeline_mode=pl.Buffered(1))
    return pl.pallas_call(
        functools.partial(_ffn_kernel, n_chunks=n_chunks),
        grid=(bsz, seq // tm),
        in_specs=[
            pl.BlockSpec((1, tm, D_MODEL), lambda b, i: (b, i, 0)),
            row, row, row,
            resident(w_gate), resident(w_up), resident(w_down), resident(ln_g), resident(ln_b),
        ],
        out_specs=pl.BlockSpec((1, tm, D_MODEL), lambda b, i: (b, i, 0)),
        out_shape=jax.ShapeDtypeStruct(x.shape, F32),
        compiler_params=_params(2),
        name="ffn",
    )(x, scale2, shift2, gate2, w_gate, w_up, w_down, ln_g, ln_b)


def _layer(x, c, w_ada, b_ada, w_in, ssm_a_re, ssm_a_im, ssm_log_dt, ssm_b_re, ssm_b_im, ssm_c_re, ssm_c_im,
           ssm_d, w_glu_lin, w_glu_gate, cmp_pe_k, cmp_pe_v, w_cmp_k1, w_cmp_k2, w_cmp_v1, w_cmp_v2,
           w_nsa_proj, w_out, ln1_g, ln1_b, w_ffn_gate, w_ffn_up, w_ffn_down, ln2_g, ln2_b):
    bsz, seq, _ = x.shape
    tq = min(KEY_BLOCK, seq)
    tm = min(512, seq)
    tm_dense = min(1024, seq)

    mod = _ada_mod(c, w_ada, b_ada).reshape(6, bsz, 1, D_MODEL)
    shift1, scale1, gate1, shift2, scale2, gate2 = (mod[k] for k in range(6))

    w_nat, w_tr, w_mg = _split_w_in(w_in)
    u_ssm, kv_cmp, k_aug, qT, v_aug, bgT = _in_proj(x, scale1, shift1, w_nat, w_tr, tm=tm)

    mats = _ssm_matrices(ssm_a_re, ssm_a_im, ssm_log_dt, ssm_b_re, ssm_b_im, ssm_c_re, ssm_c_im, ssm_d)
    g_ssm = _ssm(u_ssm, mats)

    kc3, vcT = _compress(kv_cmp,
                        cmp_pe_k.reshape(1, CMP_BLOCK * HEAD_DIM), cmp_pe_v.reshape(1, CMP_BLOCK * HEAD_DIM),
                        w_cmp_k1.reshape(CMP_BLOCK * HEAD_DIM, -1), w_cmp_k2,
                        w_cmp_v1.reshape(CMP_BLOCK * HEAD_DIM, -1), w_cmp_v2)
    o_nsa = _nsa(qT, kc3, vcT, k_aug, v_aug, bgT, tq=tq)

    x1 = _merge(x, g_ssm, o_nsa, scale1, shift1, gate1, w_mg,
                w_glu_lin.astype(BF16), w_glu_gate.astype(BF16), w_nsa_proj.astype(BF16), w_out.astype(BF16),
                ln1_g.reshape(1, D_MODEL), ln1_b.reshape(1, D_MODEL), tm=tm_dense)
    return _ffn(x1, scale2, shift2, gate2, w_ffn_gate.astype(BF16), w_ffn_up.astype(BF16),
                w_ffn_down.astype(BF16), ln2_g.reshape(1, D_MODEL), ln2_b.reshape(1, D_MODEL), tm=tm_dense, n_chunks=2)


def kernel(x, c, w_ada, b_ada, w_in, ssm_a_re, ssm_a_im, ssm_log_dt, ssm_b_re, ssm_b_im, ssm_c_re, ssm_c_im,
           ssm_d, w_glu_lin, w_glu_gate, cmp_pe_k, cmp_pe_v, w_cmp_k1, w_cmp_k2, w_cmp_v1, w_cmp_v2,
           w_nsa_proj, w_out, ln1_g, ln1_b, w_ffn_gate, w_ffn_up, w_ffn_down, ln2_g, ln2_b):
    for l in range(w_ada.shape[0]):
        x = _layer(x, c, w_ada[l], b_ada[l], w_in[l], ssm_a_re[l], ssm_a_im[l], ssm_log_dt[l],
                   ssm_b_re[l], ssm_b_im[l], ssm_c_re[l], ssm_c_im[l], ssm_d[l],
                   w_glu_lin[l], w_glu_gate[l], cmp_pe_k[l], cmp_pe_v[l],
                   w_cmp_k1[l], w_cmp_k2[l], w_cmp_v1[l], w_cmp_v2[l], w_nsa_proj[l], w_out[l],
                   ln1_g[l], ln1_b[l], w_ffn_gate[l], w_ffn_up[l], w_ffn_down[l], ln2_g[l], ln2_b[l])
    return x
```

```python
import functools
import math

import jax
import jax.numpy as jnp
from jax import lax
from jax.experimental import pallas as pl
from jax.experimental.pallas import tpu as pltpu

F32 = jnp.float32
BF16 = jnp.bfloat16
HIGHEST = lax.Precision.HIGHEST

D_MODEL = 1024
SSM_WIDTH = D_MODEL // 2
SSM_GROUP_SIZE = 16
SSM_GROUPS = SSM_WIDTH // SSM_GROUP_SIZE
SSM_STATE = 64
SSM_CHUNK = 16
SSM_GROUP_TILE = 8
SSM_PREP_GROUPS = 4
SSM_BATCH_TILE = 4
SSM_STATE_ROW_PAD = 8
N_HEADS = 8
HEAD_DIM = 64
N_KV_GROUPS = 2
HEADS_PER_GROUP = N_HEADS // N_KV_GROUPS
ATTN_WIDTH = N_HEADS * HEAD_DIM
KV_WIDTH = N_KV_GROUPS * HEAD_DIM
CMP_BLOCK = 32
CMP_STRIDE = 16
CMP_HIDDEN = HEAD_DIM
SEL_BLOCK = 64
SEL_TOP_K = 8
WINDOW = 256
N_NSA_BRANCHES = 3
GATE_ROWS = 16
FFN_HIDDEN = (8 * D_MODEL + 3 * 256 - 1) // (3 * 256) * 256
DEEPNORM_ALPHA = 2.0 ** 0.25
LN_EPS = 1e-5
MASK_VALUE = -1e30

VMEM_LIMIT = 56 * 1024 * 1024
MXU_WIDTH = 256
BF16_ROWS = 16
ROW_BLOCK = 256

KEY_BLOCK = 256
SEL_SHIFT = SEL_BLOCK.bit_length() - 1
SEL_PER_KEY_BLOCK = KEY_BLOCK // SEL_BLOCK
KEY_AUG = 128
ALIBI_COL = HEAD_DIM
N_PIECES = 4
MASK_COL = HEAD_DIM + 16
QUERY_AUG = HEAD_DIM + 32
VAL_AUG = HEAD_DIM + 16
LOG2E = math.log2(math.e)
NSA_TILES = 4
SCORES_AHEAD = 5

NAT_COLS = SSM_WIDTH + 4 * KV_WIDTH
TR_ROWS = ATTN_WIDTH + 2 * KV_WIDTH + N_KV_GROUPS * GATE_ROWS


def _bf16_pieces(value, n):
    pieces = []
    rest = value
    for _ in range(n):
        mant, expo = math.frexp(rest)
        piece = math.ldexp(round(mant * 256.0) / 256.0, expo)
        pieces.append(piece)
        rest -= piece
    return pieces


def _sigmoid(x):
    return 1.0 / (1.0 + jnp.exp(-x))


def _gelu(x):
    c = math.sqrt(2.0 / math.pi)
    return 0.5 * x * (1.0 + jnp.tanh(c * (x + 0.044715 * (x * x * x))))


def _layer_norm(y, gain, bias):
    mu = jnp.mean(y, axis=-1, keepdims=True)
    d = y - mu
    var = jnp.mean(d * d, axis=-1, keepdims=True)
    return d * lax.rsqrt(var + LN_EPS) * gain + bias


def _params(n_axes):
    return pltpu.CompilerParams(dimension_semantics=("arbitrary",) * n_axes, vmem_limit_bytes=VMEM_LIMIT)


def _ada_kernel(c_ref, w_ref, b_ref, o_ref):
    c = c_ref[...]
    a = c * _sigmoid(c)
    w = w_ref[...].astype(BF16)
    a_hi = a.astype(BF16)
    a_lo = (a - a_hi.astype(F32)).astype(BF16)
    mod = (jnp.dot(a_hi, w, preferred_element_type=F32) + jnp.dot(a_lo, w, preferred_element_type=F32)) + b_ref[...]
    for k in range(o_ref.shape[0]):
        o_ref[k] = mod[:, k * D_MODEL:(k + 1) * D_MODEL]


def _ada_mod(c, w_ada, b_ada):
    bsz = c.shape[0]
    per_step = 2
    return pl.pallas_call(
        _ada_kernel,
        grid=(6 // per_step,),
        in_specs=[
            pl.BlockSpec((bsz, D_MODEL), lambda j: (0, 0)),
            pl.BlockSpec((D_MODEL, per_step * D_MODEL), lambda j: (0, j)),
            pl.BlockSpec((1, per_step * D_MODEL), lambda j: (0, j)),
        ],
        out_specs=pl.BlockSpec((per_step, bsz, D_MODEL), lambda j: (j, 0, 0)),
        out_shape=jax.ShapeDtypeStruct((6, bsz, D_MODEL), F32),
        compiler_params=_params(1),
        name="ada_mod",
    )(c, w_ada, b_ada.reshape(1, 6 * D_MODEL))


def _in_proj_kernel(x_ref, sc_ref, sh_ref, wn_ref, wt_ref, ussm_ref, kc_ref, kp_ref, qT_ref, vp_ref, bgT_ref):
    i = pl.program_id(1)
    tm = x_ref.shape[1]
    u = (x_ref[0] * (1.0 + sc_ref[0]) + sh_ref[0]).astype(BF16)
    nat = jnp.dot(u, wn_ref[...], preferred_element_type=F32)
    ussm_ref[0] = nat[:, :SSM_WIDTH]
    for s in range(2):
        for g in range(N_KV_GROUPS):
            lo = SSM_WIDTH + s * KV_WIDTH + g * HEAD_DIM
            kc_ref[0, s, g] = nat[:, lo:lo + HEAD_DIM]
    pos = i * tm + lax.broadcasted_iota(jnp.int32, (tm, KEY_AUG), 0)
    col = lax.broadcasted_iota(jnp.int32, (tm, KEY_AUG), 1)
    blk = lax.shift_right_logical(pos, SEL_SHIFT)
    in_a = (col >= ALIBI_COL) & (col < ALIBI_COL + N_PIECES)
    in_b = (col >= ALIBI_COL + N_PIECES) & (col < ALIBI_COL + 2 * N_PIECES)
    hot = col == MASK_COL + (blk & (SEL_PER_KEY_BLOCK - 1))
    aux = jnp.where(in_a, blk * SEL_BLOCK, jnp.where(in_b, pos & (SEL_BLOCK - 1), jnp.where(hot, 1, 0))).astype(F32)
    head_lanes = col < HEAD_DIM
    for s in range(2):
        lo = SSM_WIDTH + (2 + s) * KV_WIDTH
        both = nat[:, lo:lo + KV_WIDTH]
        for g in range(N_KV_GROUPS):
            keys = both if g == 0 else pltpu.roll(both, HEAD_DIM, 1)
            kp_ref[0, s, g] = jnp.where(head_lanes, keys, aux).astype(BF16)
    tr = lax.dot_general(wt_ref[...], u, (((1,), (1,)), ((), ())), preferred_element_type=F32)
    qT_ref[0] = tr[:ATTN_WIDTH] * (HEAD_DIM ** -0.5)
    ones_rows = jnp.where(lax.broadcasted_iota(jnp.int32, (VAL_AUG - HEAD_DIM, tm), 0) == 0, 1.0, 0.0)
    for s in range(2):
        for g in range(N_KV_GROUPS):
            lo = ATTN_WIDTH + (s * N_KV_GROUPS + g) * HEAD_DIM
            v_aug = jnp.concatenate([tr[lo:lo + HEAD_DIM], ones_rows], axis=0).astype(BF16)
            for c in range(tm // KEY_BLOCK):
                vp_ref[0, s, g, c] = v_aug[:, c * KEY_BLOCK:(c + 1) * KEY_BLOCK]
    bgT_ref[0] = _sigmoid(tr[ATTN_WIDTH + 2 * KV_WIDTH:])


def _in_proj(x, scale1, shift1, w_nat, w_tr, *, tm):
    bsz, seq, _ = x.shape
    n_t = seq // tm
    n_kb = seq // KEY_BLOCK
    out_shapes = (
        jax.ShapeDtypeStruct((bsz, seq, SSM_WIDTH), F32),
        jax.ShapeDtypeStruct((bsz, 2, N_KV_GROUPS, seq, HEAD_DIM), F32),
        jax.ShapeDtypeStruct((bsz, 2, N_KV_GROUPS, seq, KEY_AUG), BF16),
        jax.ShapeDtypeStruct((bsz, ATTN_WIDTH, seq), F32),
        jax.ShapeDtypeStruct((bsz, 2, N_KV_GROUPS, n_kb, VAL_AUG, KEY_BLOCK), BF16),
        jax.ShapeDtypeStruct((bsz, N_KV_GROUPS * GATE_ROWS, seq), F32),
    )
    return pl.pallas_call(
        _in_proj_kernel,
        grid=(bsz, n_t),
        in_specs=[
            pl.BlockSpec((1, tm, D_MODEL), lambda b, i: (b, i, 0)),
            pl.BlockSpec((1, 1, D_MODEL), lambda b, i: (b, 0, 0)),
            pl.BlockSpec((1, 1, D_MODEL), lambda b, i: (b, 0, 0)),
            pl.BlockSpec((D_MODEL, NAT_COLS), lambda b, i: (0, 0)),
            pl.BlockSpec((TR_ROWS, D_MODEL), lambda b, i: (0, 0)),
        ],
        out_specs=(
            pl.BlockSpec((1, tm, SSM_WIDTH), lambda b, i: (b, i, 0)),
            pl.BlockSpec((1, 2, N_KV_GROUPS, tm, HEAD_DIM), lambda b, i: (b, 0, 0, i, 0)),
            pl.BlockSpec((1, 2, N_KV_GROUPS, tm, KEY_AUG), lambda b, i: (b, 0, 0, i, 0)),
            pl.BlockSpec((1, ATTN_WIDTH, tm), lambda b, i: (b, 0, i)),
            pl.BlockSpec((1, 2, N_KV_GROUPS, tm // KEY_BLOCK, VAL_AUG, KEY_BLOCK), lambda b, i: (b, 0, 0, i, 0, 0)),
            pl.BlockSpec((1, N_KV_GROUPS * GATE_ROWS, tm), lambda b, i: (b, 0, i)),
        ),
        out_shape=out_shapes,
        compiler_params=_params(2),
        name="in_proj",
    )(x, scale1, shift1, w_nat, w_tr)


def _split_w_in(w_in):
    o_q = SSM_WIDTH
    o_kv = o_q + ATTN_WIDTH
    o_bg = o_kv + 6 * KV_WIDTH
    o_mg = o_bg + N_NSA_BRANCHES * N_HEADS
    kv = [w_in[:, o_kv + s * KV_WIDTH:o_kv + (s + 1) * KV_WIDTH] for s in range(6)]
    w_nat = jnp.concatenate([w_in[:, :o_q], kv[0], kv[1], kv[2], kv[4]], axis=1)
    per_group = HEADS_PER_GROUP * N_NSA_BRANCHES
    bg = w_in[:, o_bg:o_mg].reshape(D_MODEL, N_KV_GROUPS, per_group)
    bg = jnp.pad(bg, ((0, 0), (0, 0), (0, GATE_ROWS - per_group))).reshape(D_MODEL, N_KV_GROUPS * GATE_ROWS)
    w_tr = jnp.concatenate([w_in[:, o_q:o_kv], kv[3], kv[5], bg], axis=1).T
    return w_nat.astype(BF16), w_tr.astype(BF16), w_in[:, o_mg:].astype(BF16)


def _ssm_prep_kernel(*refs):
    for k in range(refs[0].shape[0]):
        _ssm_prep_group(*[ref.at[pl.ds(k, 1)] for ref in refs])


def _ssm_prep_group(a_ref, logdt_ref, bT_ref, c_ref, d_ref, mT_ref, pT_ref, qT_ref, a16_ref):
    t_n = SSM_CHUNK
    c_n = SSM_GROUP_SIZE
    n_p = SSM_STATE
    rows = t_n * c_n
    dt = jnp.exp(logdt_ref[0])
    a_re = a_ref[0, 0]
    a_im = a_ref[0, 1]
    decay = jnp.exp(a_re * dt)
    ab_re = decay * jnp.cos(a_im * dt)
    ab_im = decay * jnp.sin(a_im * dt)
    n_re = ab_re - 1.0
    denom = a_re * a_re + a_im * a_im
    f_re = (n_re * a_re + ab_im * a_im) / denom
    f_im = (ab_im * a_re - n_re * a_im) / denom
    powers = [(jnp.ones_like(ab_re), jnp.zeros_like(ab_re))]
    for _ in range(t_n):
        p_re, p_im = powers[-1]
        powers.append((p_re * ab_re - p_im * ab_im, p_re * ab_im + p_im * ab_re))

    def per_token(first, step):
        picks = [powers[first + step * k] for k in range(t_n)]
        return (jnp.concatenate([jnp.broadcast_to(p[0], (c_n, 2 * n_p)) for p in picks], axis=0),
                jnp.concatenate([jnp.broadcast_to(p[1], (c_n, 2 * n_p)) for p in picks], axis=0))

    def cmul(x_re, x_im, y_re, y_im):
        return x_re * y_re - x_im * y_im, x_re * y_im + x_im * y_re

    bb_re, bb_im = cmul(f_re, f_im, bT_ref[0, 0], bT_ref[0, 1])
    bt_re = jnp.concatenate([bb_re] * t_n, axis=0)
    bt_im = jnp.concatenate([bb_im] * t_n, axis=0)
    ct_re = jnp.concatenate([c_ref[0, 0]] * t_n, axis=0)
    ct_im = jnp.concatenate([c_ref[0, 1]] * t_n, axis=0)

    p_re, p_im = cmul(*per_token(t_n - 1, -1), bt_re, bt_im)
    low_half = lax.broadcasted_iota(jnp.int32, (1, 2 * n_p), 1) < n_p
    pT_ref[0] = jnp.where(low_half, p_re, p_im).T.astype(BF16)

    q_re, q_im = cmul(ct_re, ct_im, *per_token(1, 1))
    qT_ref[0] = jnp.where(low_half, q_re, -q_im).astype(BF16)
    a16_ref[0] = jnp.concatenate([powers[t_n][0][:, :n_p], powers[t_n][1][:, :n_p]], axis=0)

    k_re, k_im = cmul(ct_re, ct_im, *per_token(0, 1))
    nt = (((1,), (1,)), ((), ()))
    strips = (lax.dot_general(_split3(k_re[:, :n_p]), _stack3(bt_re[:, :n_p], axis=1), nt, preferred_element_type=F32)
              - lax.dot_general(_split3(k_im[:, :n_p]), _stack3(bt_im[:, :n_p], axis=1), nt,
                                preferred_element_type=F32))
    lane = lax.broadcasted_iota(jnp.int32, (1, rows), 1)
    s_lane = lax.shift_right_logical(lane, c_n.bit_length() - 1)
    ci_lane = lane & (c_n - 1)
    row = lax.broadcasted_iota(jnp.int32, (rows, 1), 0)
    t_row = lax.shift_right_logical(row, c_n.bit_length() - 1)
    co_row = row & (c_n - 1)
    blocks = []
    for t in range(t_n):
        acc = jnp.zeros((c_n, rows), F32)
        for lag in range(t + 1):
            acc = jnp.where(s_lane == t - lag, strips[lag * c_n:(lag + 1) * c_n, :], acc)
        blocks.append(acc)
    d_rows = jnp.concatenate([d_ref[0]] * t_n, axis=0)
    skip = jnp.where((t_row == s_lane) & (co_row == ci_lane), d_rows, 0.0)
    mT_ref[0] = (jnp.concatenate(blocks, axis=0) + skip).astype(BF16)


def _ssm_matrices(a_re, a_im, log_dt, b_re, b_im, c_re, c_im, d_skip):
    n_g, n_p = a_re.shape
    c_n = SSM_GROUP_SIZE
    rows = SSM_CHUNK * c_n
    twice = lambda x: jnp.concatenate([x, x], axis=-1).astype(F32)
    a = twice(jnp.stack([a_re, a_im], axis=1)).reshape(n_g, 2, 1, 2 * n_p)
    bT = twice(jnp.swapaxes(jnp.stack([b_re, b_im], axis=1), 2, 3))
    c = twice(jnp.stack([c_re, c_im], axis=1))
    per_step = SSM_PREP_GROUPS
    spec = lambda shape: pl.BlockSpec((per_step,) + shape, lambda g: (g,) + (0,) * len(shape))
    mT, pT, qT, a16 = pl.pallas_call(
        _ssm_prep_kernel,
        grid=(n_g // per_step,),
        in_specs=[spec((2, 1, 2 * n_p)), spec((1, 1)), spec((2, c_n, 2 * n_p)), spec((2, c_n, 2 * n_p)),
                  spec((c_n, 1))],
        out_specs=(spec((rows, rows)), spec((2 * n_p, rows)), spec((rows, 2 * n_p)), spec((2, n_p))),
        out_shape=(
            jax.ShapeDtypeStruct((n_g, rows, rows), BF16),
            jax.ShapeDtypeStruct((n_g, 2 * n_p, rows), BF16),
            jax.ShapeDtypeStruct((n_g, rows, 2 * n_p), BF16),
            jax.ShapeDtypeStruct((n_g, 2, n_p), F32),
        ),
        compiler_params=_params(1),
        name="ssm_prep",
    )(a, log_dt.astype(F32).reshape(n_g, 1, 1), bT, c, d_skip.astype(F32).reshape(n_g, c_n, 1))
    return mT, pT.reshape(n_g, 2, n_p, rows), qT, a16


def _ssm_kernel(u_ref, mT_ref, pT_ref, qT_ref, a16_ref, o_ref, ut_scr, yt_scr, sre, sim, *, n_chunks):
    n_g = SSM_GROUP_TILE
    n_b = u_ref.shape[0]
    t_n = SSM_CHUNK
    c_n = SSM_GROUP_SIZE
    lanes = [(e, g) for e in range(n_b) for g in range(n_g)]

    for e in range(n_b):
        for s in range(t_n):
            x_s = u_ref[e, pl.ds(s, n_chunks, stride=t_n), :]
            ut_scr[e * n_g:(e + 1) * n_g, s * c_n:(s + 1) * c_n, :] = (
                x_s.T.reshape(n_g, c_n, n_chunks).astype(BF16))

    pitch = n_chunks + SSM_STATE_ROW_PAD

    def lane_rows(k):
        return pl.ds(k * pitch, n_chunks)

    for k, (e, g) in enumerate(lanes):
        ut = ut_scr[k]
        sre[lane_rows(k), :] = jnp.dot(pT_ref[g, 0], ut, preferred_element_type=F32).T
        sim[lane_rows(k), :] = jnp.dot(pT_ref[g, 1], ut, preferred_element_type=F32).T

    a_r = jnp.concatenate([a16_ref[pl.ds(0, n_g, stride=2), :]] * n_b, axis=0)
    a_i = jnp.concatenate([a16_ref[pl.ds(1, n_g, stride=2), :]] * n_b, axis=0)

    def carry_states(c, h):
        h_r, h_i = h
        rows = pl.ds(c, len(lanes), stride=pitch)
        s_r = sre[rows, :]
        s_i = sim[rows, :]
        sre[rows, :] = h_r
        sim[rows, :] = h_i
        return a_r * h_r - a_i * h_i + s_r, a_r * h_i + a_i * h_r + s_i

    zero = jnp.zeros((len(lanes), SSM_STATE), F32)
    lax.fori_loop(0, n_chunks, carry_states, (zero, zero), unroll=4)

    nt = (((1,), (1,)), ((), ()))
    for k, (e, g) in enumerate(lanes):
        yt = jnp.dot(mT_ref[g], ut_scr[k], preferred_element_type=F32)
        entering = jnp.concatenate([sre[lane_rows(k), :], sim[lane_rows(k), :]], axis=1).astype(BF16)
        yt = yt + lax.dot_general(qT_ref[g], entering, nt, preferred_element_type=F32)
        yt_scr[k] = _gelu(yt)

    for e in range(n_b):
        for t in range(t_n):
            z = yt_scr[e * n_g:(e + 1) * n_g, t * c_n:(t + 1) * c_n, :].reshape(n_g * c_n, n_chunks)
            o_ref[e, pl.ds(t, n_chunks, stride=t_n), :] = z.T


def _ssm(u_ssm, mats):
    bsz, seq, width = u_ssm.shape
    n_chunks = seq // SSM_CHUNK
    mT, pT, qT, a16 = mats
    gt = SSM_GROUP_TILE
    n_tiles = SSM_GROUPS // gt
    rows = SSM_CHUNK * SSM_GROUP_SIZE
    a16 = a16.reshape(2 * SSM_GROUPS, SSM_STATE)
    bt = SSM_BATCH_TILE if bsz % SSM_BATCH_TILE == 0 else 1
    n_lanes = bt * gt
    per_tile = lambda a: pl.BlockSpec((gt,) + a.shape[1:], lambda t, b: (t,) + (0,) * (a.ndim - 1))
    return pl.pallas_call(
        functools.partial(_ssm_kernel, n_chunks=n_chunks),
        grid=(n_tiles, bsz // bt),
        in_specs=[
            pl.BlockSpec((bt, seq, gt * SSM_GROUP_SIZE), lambda t, b: (b, 0, t)),
            per_tile(mT), per_tile(pT), per_tile(qT),
            pl.BlockSpec((2 * gt, SSM_STATE), lambda t, b: (t, 0)),
        ],
        out_specs=pl.BlockSpec((bt, seq, gt * SSM_GROUP_SIZE), lambda t, b: (b, 0, t)),
        out_shape=jax.ShapeDtypeStruct(u_ssm.shape, F32),
        scratch_shapes=[
            pltpu.VMEM((n_lanes, rows, n_chunks), BF16),
            pltpu.VMEM((n_lanes, rows, n_chunks), F32),
            pltpu.VMEM((n_lanes * (n_chunks + SSM_STATE_ROW_PAD), SSM_STATE), F32),
            pltpu.VMEM((n_lanes * (n_chunks + SSM_STATE_ROW_PAD), SSM_STATE), F32),
        ],
        compiler_params=_params(2),
        name="ssm",
    )(u_ssm, mT, pT, qT, a16)


def _split3(x):
    hi = x.astype(BF16)
    lo = (x - hi.astype(F32)).astype(BF16)
    return jnp.concatenate([hi, lo, hi], axis=-1)


def _stack3(w, axis=0):
    hi = w.astype(BF16)
    lo = (w - hi.astype(F32)).astype(BF16)
    return jnp.concatenate([hi, hi, lo], axis=axis)


def _compress_kernel(kx_ref, vx_ref, pek_ref, pev_ref, wk1_ref, wk2_ref, wv1_ref, wv2_ref, kc_ref, vcT_ref):
    def mlp(x_ref, g, pe_ref, w1_ref, w2_ref):
        n = x_ref.shape[3] // CMP_STRIDE
        x = jnp.concatenate([x_ref[0, 0, g, pl.ds(l, n, stride=CMP_STRIDE), :] for l in range(CMP_STRIDE)], axis=1)
        both = jnp.dot(_split3(x), w1_ref[...], preferred_element_type=F32)
        pe_both = jnp.dot(_split3(pe_ref[...]), w1_ref[...], preferred_element_type=F32)
        bias = pe_both[0:1, :CMP_HIDDEN] + pe_both[1:2, CMP_HIDDEN:]
        h = _gelu(both[:, :CMP_HIDDEN] + pltpu.roll(both[:, CMP_HIDDEN:], n - 1, 0) + bias)
        return jnp.dot(_split3(h), w2_ref[...], preferred_element_type=F32)

    for g in range(kx_ref.shape[2]):
        kc = mlp(kx_ref, g, pek_ref, wk1_ref, wk2_ref)
        kc_hi = kc.astype(BF16)
        kc_lo = (kc - kc_hi.astype(F32)).astype(BF16)
        kc_ref[0, g] = jnp.concatenate([kc_hi, kc_hi, kc_lo, jnp.zeros_like(kc_hi)], axis=1)
        vcT_ref[0, g] = mlp(vx_ref, g, pev_ref, wv1_ref, wv2_ref).T.astype(BF16)


def _compress(kv, pe_k, pe_v, wk1, wk2, wv1, wv2):
    bsz, _, n_g, seq, width = kv.shape
    n_chunks = seq // CMP_STRIDE
    half = CMP_STRIDE * HEAD_DIM
    side_by_side = lambda w1: _stack3(jnp.concatenate([w1[:half], w1[half:]], axis=1))
    two_rows = lambda pe: jnp.pad(pe.reshape(2, half), ((0, BF16_ROWS - 2), (0, 0)))
    pe_k, pe_v = two_rows(pe_k), two_rows(pe_v)
    wk1, wv1, wk2, wv2 = side_by_side(wk1), side_by_side(wv1), _stack3(wk2), _stack3(wv2)
    blk = lambda s: pl.BlockSpec((1, 1, n_g, seq, width), lambda b: (b, s, 0, 0, 0))
    full = lambda a: pl.BlockSpec(a.shape, lambda b: (0, 0))
    return pl.pallas_call(
        _compress_kernel,
        grid=(bsz,),
        in_specs=[blk(0), blk(1), full(pe_k), full(pe_v), full(wk1), full(wk2), full(wv1), full(wv2)],
        out_specs=(
            pl.BlockSpec((1, n_g, n_chunks, 4 * HEAD_DIM), lambda b: (b, 0, 0, 0)),
            pl.BlockSpec((1, n_g, HEAD_DIM, n_chunks), lambda b: (b, 0, 0, 0)),
        ),
        out_shape=(
            jax.ShapeDtypeStruct((bsz, n_g, n_chunks, 4 * HEAD_DIM), BF16),
            jax.ShapeDtypeStruct((bsz, n_g, HEAD_DIM, n_chunks), BF16),
        ),
        compiler_params=_params(1),
        name="compress",
    )(kv, kv, pe_k, pe_v, wk1, wk2, wv1, wv2)


def _pipeline(units):
    issue = lambda unit: None if unit[0] is None else unit[0]()
    pending = [issue(unit) for unit in units[:SCORES_AHEAD]]
    for n, unit in enumerate(units):
        if n + SCORES_AHEAD < len(units):
            pending.append(issue(units[n + SCORES_AHEAD]))
        unit[1](pending.pop(0))


def _nsa_kernel(qT_ref, kc_ref, vcT_ref, ks_ref, vs_ref, kw_ref, vw_ref, bg_ref, o_ref,
                negm_scr, qa_scr, m_scr, acc_scr, out_scr, *smem, tq, n_cmp, n_sel):
    gw = HEADS_PER_GROUP * HEAD_DIM
    groups = []
    for tile in range(NSA_TILES):
        i = pl.program_id(1) * NSA_TILES + tile
        span = slice(tile * tq, (tile + 1) * tq)
        for g in range(N_KV_GROUPS):
            k = tile * N_KV_GROUPS + g
            groups.append(_nsa_group(
                g, i, qT_ref.at[:, g * gw:(g + 1) * gw, span], kc_ref.at[:, g:g + 1], vcT_ref.at[:, g:g + 1],
                ks_ref.at[:, :, g:g + 1], vs_ref.at[:, :, g:g + 1], kw_ref.at[:, :, g:g + 1],
                vw_ref.at[:, :, g:g + 1], bg_ref.at[:, g:g + 1, :, span], o_ref.at[:, span, g * gw:(g + 1) * gw],
                negm_scr.at[k], qa_scr.at[k], m_scr.at[k], acc_scr.at[k], out_scr.at[k],
                smem[2 * k], smem[2 * k + 1], tq=tq, n_cmp=n_cmp, n_sel=n_sel))
    stages = [[unit for stage_units, _ in groups for unit in stage_units[stage]]
              for stage in range(3)]
    _pipeline(stages[0] + stages[1] + stages[2])
    for _, finish_group in groups:
        finish_group()


def _nsa_group(g, i, qT_ref, kc_ref, vcT_ref, ks_ref, vs_ref, kw_ref, vw_ref, bg_ref, o_ref,
               negm_scr, qa_scr, m_scr, acc_scr, out_scr, used_smem, todo_smem, *, tq, n_cmp, n_sel):
    t_row = i * tq + lax.broadcasted_iota(jnp.int32, (1, tq), 1)
    slopes = [2.0 ** -(g * HEADS_PER_GROUP + hh + 1) for hh in range(HEADS_PER_GROUP)]

    def gate(hh, branch):
        r = hh * N_NSA_BRANCHES + branch
        return bg_ref[0, 0, r:r + 1, :]

    def head_rows(hh):
        return slice(hh * HEAD_DIM, (hh + 1) * HEAD_DIM)

    def head_lanes(hh):
        return slice(hh * tq, (hh + 1) * tq)

    kc3 = kc_ref[0, 0]
    vcT = vcT_ref[0, 0]
    cmp_end = lax.broadcasted_iota(jnp.int32, (n_cmp, 1), 0) * CMP_STRIDE + (CMP_BLOCK - 1)
    dist_c = (t_row - cmp_end).astype(F32)
    valid_c = dist_c >= 0.0
    piece = lax.broadcasted_iota(jnp.int32, (16, 1), 0)
    log2e_pieces = _bf16_pieces(LOG2E, N_PIECES)
    log2e_col = jnp.zeros((16, 1), F32)
    for k, value in enumerate(log2e_pieces):
        log2e_col = jnp.where((piece == k) | (piece == N_PIECES + k), value, log2e_col)
    for hh in range(HEADS_PER_GROUP):
        alibi = jnp.broadcast_to(log2e_col * slopes[hh], (16, tq)).astype(BF16)
        q_l2 = (qT_ref[0, head_rows(hh), :] * LOG2E).astype(BF16)
        qa_scr[:, head_lanes(hh)] = jnp.concatenate([q_l2, alibi], axis=0)

    def cmp_scores(hh):
        q_h = qT_ref[0, head_rows(hh), :]
        q_hi = q_h.astype(BF16)
        q_lo = (q_h - q_hi.astype(F32)).astype(BF16)
        q3 = jnp.concatenate([q_hi, q_lo, q_hi, jnp.zeros_like(q_hi)], axis=0)
        return jnp.dot(kc3, q3, preferred_element_type=F32)

    def cmp_attend(p_heads, hh, s):
        s = jnp.where(valid_c, s - slopes[hh] * dist_c, MASK_VALUE)
        m = jnp.max(s, axis=0, keepdims=True)
        p = jnp.where(valid_c, jnp.exp(s - m), 0.0)
        l = jnp.sum(p, axis=0, keepdims=True)
        p = p / jnp.where(l > 0.0, l, 1.0)
        o_c = jnp.dot(vcT, p.astype(BF16), preferred_element_type=F32)
        out_scr[head_rows(hh), :] = gate(hh, 0) * o_c
        p_heads.append(p)

    def select_blocks(p_heads, _):
        p_sum = sum(p_heads[1:], p_heads[0])
        blk = lax.broadcasted_iota(jnp.int32, (n_sel, n_cmp), 0)
        cmp_ix = lax.broadcasted_iota(jnp.int32, (n_sel, n_cmp), 1)
        ratio = SEL_BLOCK // CMP_STRIDE
        extra = CMP_BLOCK // CMP_STRIDE - 1
        overlap_t = jnp.where((cmp_ix >= ratio * blk - extra) & (cmp_ix <= ratio * blk + ratio - 1),
                              1.0, 0.0).astype(BF16)
        imp = jnp.zeros((n_sel, tq), F32)
        rest = p_sum
        for _ in range(3):
            part = rest.astype(BF16)
            imp = imp + jnp.dot(overlap_t, part, preferred_element_type=F32)
            rest = rest - part.astype(F32)

        j_col = lax.broadcasted_iota(jnp.int32, (n_sel, 1), 0)
        forced = (j_col == 0) | (j_col == lax.shift_right_logical(t_row, SEL_SHIFT))
        future = j_col * SEL_BLOCK > t_row
        score = jnp.where(forced, jnp.inf, jnp.where(future, -jnp.inf, imp))
        rank = jnp.zeros((n_sel, tq), F32)
        for jp in range(n_sel):
            other = score[jp:jp + 1, :]
            ahead = (other > score) | ((other == score) & (j_col > jp))
            rank = rank + jnp.where(ahead, 1.0, 0.0)
        neg = jnp.where(rank < float(min(SEL_TOP_K, n_sel)), 0.0, MASK_VALUE)
        pad_rows = jnp.zeros((BF16_ROWS - SEL_PER_KEY_BLOCK, tq), F32)
        for c in range(n_sel // SEL_PER_KEY_BLOCK):
            rows = neg[c * SEL_PER_KEY_BLOCK:(c + 1) * SEL_PER_KEY_BLOCK, :]
            negm_scr[c] = jnp.concatenate([rows, pad_rows], axis=0).astype(BF16)
            used_smem[c] = (jnp.max(rows) > 0.5 * MASK_VALUE).astype(jnp.int32)

    rel = (lax.broadcasted_iota(jnp.int32, (KEY_BLOCK, tq), 1)
           - lax.broadcasted_iota(jnp.int32, (KEY_BLOCK, tq), 0))
    zero_tail = jnp.zeros((KEY_AUG - QUERY_AUG, tq), BF16)
    no_mask = jnp.zeros((BF16_ROWS, tq), BF16)
    all_masked = jnp.full((BF16_ROWS, tq), MASK_VALUE, BF16)

    sel_even, sel_odd, win_prev, win_diag = range(4)

    def scores(job, hh):
        k_ref, _, kb, enabled, _, _ = job
        k0 = pl.multiple_of(kb * KEY_BLOCK, KEY_BLOCK)
        k_aug = k_ref[0, 0, 0, pl.ds(k0, KEY_BLOCK), :]
        if enabled is None:
            mask_tile = no_mask
        elif enabled is True:
            mask_tile = negm_scr[kb]
        else:
            mask_tile = jnp.where(enabled, negm_scr[kb], all_masked)
        q_aug = jnp.concatenate([qa_scr[:, head_lanes(hh)], mask_tile, zero_tail], axis=0)
        return jnp.dot(k_aug, q_aug, preferred_element_type=F32)

    def accumulate(job, first, hh, s):
        _, v_ref, kb, _, valid, slot = job
        cols = head_lanes(hh)
        if valid is not None:
            s = jnp.where(valid, s, MASK_VALUE)
        m_new = jnp.max(s, axis=0, keepdims=True)
        if not first:
            m_old = m_scr[slot, :, cols]
            m_new = jnp.maximum(m_old, m_new)
        p = jnp.exp2(s - jnp.maximum(m_new, 0.1 * MASK_VALUE))
        pv = jnp.dot(v_ref[0, 0, 0, kb], p.astype(BF16), preferred_element_type=F32)
        if not first:
            pv = jnp.exp2(m_old - m_new) * acc_scr[slot, :, cols] + pv
        acc_scr[slot, :, cols] = pv
        m_scr[slot, :, cols] = m_new

    def attention_units(jobs, first):
        return [(functools.partial(scores, job, hh), functools.partial(accumulate, job, first, hh))
                for job in jobs for hh in range(HEADS_PER_GROUP)]

    def finish(branch, slot_a, slot_b):
        m_a = m_scr[slot_a]
        m_b = m_scr[slot_b]
        m = jnp.maximum(m_a, m_b)
        w_a = jnp.exp2(m_a - m)
        w_b = jnp.exp2(m_b - m)
        for hh in range(HEADS_PER_GROUP):
            cols = head_lanes(hh)
            acc = acc_scr[slot_a, :, cols] * w_a[:, cols] + acc_scr[slot_b, :, cols] * w_b[:, cols]
            l = acc[HEAD_DIM:HEAD_DIM + 1, :]
            o_b = acc[:HEAD_DIM, :] / jnp.where(l > 0.0, l, 1.0)
            out_scr[head_rows(hh), :] = out_scr[head_rows(hh), :] + gate(hh, branch) * o_b

    prev = jnp.maximum(i - 1, 0)
    causal = rel >= 0
    off_without_prev = jnp.where(i > 0, 0, WINDOW + KEY_BLOCK)
    dist_prev = rel + KEY_BLOCK + off_without_prev
    p_heads = []
    stage_units = (
        [(functools.partial(cmp_scores, hh), functools.partial(cmp_attend, p_heads, hh))
         for hh in range(HEADS_PER_GROUP)] + [(None, functools.partial(select_blocks, p_heads))],
        attention_units([(kw_ref, vw_ref, prev, None, dist_prev < WINDOW, win_prev),
                         (kw_ref, vw_ref, i, None, causal, win_diag)], first=True),
        attention_units([(ks_ref, vs_ref, prev, i > 0, None, sel_even),
                         (ks_ref, vs_ref, i, True, causal, sel_odd)], first=True))

    def finish_group():
        n_todo = jnp.int32(0)
        for c in range(used_smem.shape[0]):
            take = jnp.logical_and(c < i - 1, used_smem[c] > 0)

            @pl.when(take)
            def _(c=c, slot=n_todo):
                todo_smem[slot] = c

            n_todo = n_todo + take.astype(jnp.int32)
        todo_smem[n_todo] = 0

        def past_pair(j, carry):
            _pipeline(attention_units([(ks_ref, vs_ref, todo_smem[2 * j], True, None, sel_even),
                                       (ks_ref, vs_ref, todo_smem[2 * j + 1], 2 * j + 1 < n_todo, None, sel_odd)],
                                      first=False))
            return carry

        lax.fori_loop(0, lax.shift_right_logical(n_todo + 1, 1), past_pair, 0)
        finish(1, sel_even, sel_odd)
        finish(2, win_prev, win_diag)
        o_ref[0] = out_scr[...].T

    return stage_units, finish_group


def _nsa(qT, kc3, vcT, k_aug, v_aug, bgT, *, tq):
    bsz, _, seq = qT.shape
    assert tq == KEY_BLOCK == WINDOW and seq % tq == 0
    n_cmp = kc3.shape[2]
    n_sel = seq // SEL_BLOCK
    n_kb = seq // KEY_BLOCK
    gw = HEADS_PER_GROUP * HEAD_DIM
    bg4 = bgT.reshape(bsz, N_KV_GROUPS, GATE_ROWS, seq)
    n_g = N_KV_GROUPS
    k_spec = lambda s: pl.BlockSpec((1, 1, n_g, seq, KEY_AUG), lambda b, i: (b, s, 0, 0, 0))
    v_spec = lambda s: pl.BlockSpec((1, 1, n_g, n_kb, VAL_AUG, KEY_BLOCK), lambda b, i: (b, s, 0, 0, 0, 0))
    n_inst = NSA_TILES * n_g
    step_q = NSA_TILES * tq
    assert seq % step_q == 0
    smem = []
    for _ in range(n_inst):
        smem += [pltpu.SMEM((n_kb,), jnp.int32),
                 pltpu.SMEM((n_kb + 1,), jnp.int32)]
    return pl.pallas_call(
        functools.partial(_nsa_kernel, tq=tq, n_cmp=n_cmp, n_sel=n_sel),
        grid=(bsz, seq // step_q),
        in_specs=[
            pl.BlockSpec((1, n_g * gw, step_q), lambda b, i: (b, 0, i)),
            pl.BlockSpec((1, n_g, n_cmp, 4 * HEAD_DIM), lambda b, i: (b, 0, 0, 0)),
            pl.BlockSpec((1, n_g, HEAD_DIM, n_cmp), lambda b, i: (b, 0, 0, 0)),
            k_spec(0), v_spec(0),
            k_spec(1), v_spec(1),
            pl.BlockSpec((1, n_g, GATE_ROWS, step_q), lambda b, i: (b, 0, 0, i)),
        ],
        out_specs=pl.BlockSpec((1, step_q, n_g * gw), lambda b, i: (b, i, 0)),
        out_shape=jax.ShapeDtypeStruct((bsz, seq, ATTN_WIDTH), F32),
        scratch_shapes=[
            pltpu.VMEM((n_inst, n_kb, 16, tq), BF16),
            pltpu.VMEM((n_inst, HEAD_DIM + 16, HEADS_PER_GROUP * tq), BF16),
            pltpu.VMEM((n_inst, 4, 1, HEADS_PER_GROUP * tq), F32),
            pltpu.VMEM((n_inst, 4, VAL_AUG, HEADS_PER_GROUP * tq), F32),
            pltpu.VMEM((n_inst, gw, tq), F32),
        ] + smem,
        compiler_params=_params(2),
        name="nsa",
    )(qT, kc3, vcT, k_aug, v_aug, k_aug, v_aug, bg4)


def _merge_kernel(x_ref, g_ref, o_ref_in, sc_ref, sh_ref, gt_ref, wm_ref, wl_ref, wg_ref, wn_ref, wo_ref,
                  lg_ref, lb_ref, y_ref):
    tm = x_ref.shape[1]
    row_blocks = [slice(r, r + ROW_BLOCK) for r in range(0, tm, ROW_BLOCK)]
    halves = [(0, D_MODEL // 2), (D_MODEL // 2, D_MODEL)]
    dot = functools.partial(jnp.dot, preferred_element_type=F32)

    def first_stage(rows):
        g = g_ref[0, rows, :].astype(BF16)
        o = o_ref_in[0, rows, :].astype(BF16)
        branch = [(dot(g, wl_ref[:, lo:hi]), dot(g, wg_ref[:, lo:hi]), dot(o, wn_ref[:, lo:hi])) for lo, hi in halves]
        x = x_ref[0, rows, :]
        u = (x * (1.0 + sc_ref[0]) + sh_ref[0]).astype(BF16)
        gates = [(dot(u, wm_ref[:, lo:hi]), dot(u, wm_ref[:, D_MODEL + lo:D_MODEL + hi])) for lo, hi in halves]
        return x, branch, gates

    def second_stage(rows, x, branch, gates):
        mix = None
        for (lo, hi), (lin, gate, z_nsa), (m_ssm, m_nsa) in zip(halves, branch, gates):
            merged = _sigmoid(m_ssm) * (lin * _sigmoid(gate)) + _sigmoid(m_nsa) * z_nsa
            part = dot(merged.astype(BF16), wo_ref[lo:hi, :])
            mix = part if mix is None else mix + part
        y = DEEPNORM_ALPHA * x + (1.0 + gt_ref[0]) * mix
        y_ref[0, rows, :] = _layer_norm(y, lg_ref[...], lb_ref[...])

    staged = None
    for rows in row_blocks + [None]:
        upcoming = first_stage(rows) if rows is not None else None
        if staged is not None:
            second_stage(*staged)
        staged = (rows,) + upcoming if rows is not None else None


def _merge(x, g_ssm, o_nsa, scale1, shift1, gate1, w_m, w_lin, w_gate, w_nsa, w_out, ln_g, ln_b, *, tm):
    bsz, seq, _ = x.shape
    row = pl.BlockSpec((1, 1, D_MODEL), lambda b, i: (b, 0, 0))
    full = lambda a: pl.BlockSpec(a.shape, lambda b, i: (0, 0), pipeline_mode=pl.Buffered(1))
    return pl.pallas_call(
        _merge_kernel,
        grid=(bsz, seq // tm),
        in_specs=[
            pl.BlockSpec((1, tm, D_MODEL), lambda b, i: (b, i, 0)),
            pl.BlockSpec((1, tm, SSM_WIDTH), lambda b, i: (b, i, 0)),
            pl.BlockSpec((1, tm, ATTN_WIDTH), lambda b, i: (b, i, 0)),
            row, row, row,
            full(w_m), full(w_lin), full(w_gate), full(w_nsa), full(w_out), full(ln_g), full(ln_b),
        ],
        out_specs=pl.BlockSpec((1, tm, D_MODEL), lambda b, i: (b, i, 0)),
        out_shape=jax.ShapeDtypeStruct(x.shape, F32),
        compiler_params=_params(2),
        name="merge",
    )(x, g_ssm, o_nsa, scale1, shift1, gate1, w_m, w_lin, w_gate, w_nsa, w_out, ln_g, ln_b)


def _ffn_kernel(x_ref, sc_ref, sh_ref, gt_ref, wg_ref, wu_ref, wd_ref, lg_ref, lb_ref, y_ref, *, n_chunks):
    tm = x_ref.shape[1]
    row_blocks = [slice(r, r + ROW_BLOCK) for r in range(0, tm, ROW_BLOCK)]
    tiles = FFN_HIDDEN // MXU_WIDTH
    edges = [MXU_WIDTH * ((tiles * c + n_chunks - 1) // n_chunks) for c in range(n_chunks + 1)]
    chunks = [slice(lo, hi) for lo, hi in zip(edges[:-1], edges[1:])]
    dot = functools.partial(jnp.dot, preferred_element_type=F32)

    def gate_up(rows):
        x = x_ref[0, rows, :]
        u = (x * (1.0 + sc_ref[0]) + sh_ref[0]).astype(BF16)
        return x, [(dot(u, wg_ref[:, cols]), dot(u, wu_ref[:, cols])) for cols in chunks]

    def down(rows, x, products):
        ffn = None
        for cols, (a, b) in zip(chunks, products):
            part = dot(((a * _sigmoid(a)) * b).astype(BF16), wd_ref[cols, :])
            ffn = part if ffn is None else ffn + part
        y = DEEPNORM_ALPHA * x + (1.0 + gt_ref[0]) * ffn
        y_ref[0, rows, :] = _layer_norm(y, lg_ref[...], lb_ref[...])

    staged = None
    for rows in row_blocks + [None]:
        upcoming = gate_up(rows) if rows is not None else None
        if staged is not None:
            down(*staged)
        staged = (rows,) + upcoming if rows is not None else None


def _ffn(x, scale2, shift2, gate2, w_gate, w_up, w_down, ln_g, ln_b, *, tm, n_chunks):
    bsz, seq, _ = x.shape
    row = pl.BlockSpec((1, 1, D_MODEL), lambda b, i: (b, 0, 0))
    resident = lambda a: pl.BlockSpec(a.shape, lambda b, i: (0, 0), pipeline_mode=pl.Buffered(1))
    return pl.pallas_call(
        functools.partial(_ffn_kernel, n_chunks=n_chunks),
        grid=(bsz, seq // tm),
        in_specs=[
            pl.BlockSpec((1, tm, D_MODEL), lambda b, i: (b, i, 0)),
            row, row, row,
            resident(w_gate), resident(w_up), resident(w_down), resident(ln_g), resident(ln_b),
        ],
        out_specs=pl.BlockSpec((1, tm, D_MODEL), lambda b, i: (b, i, 0)),
        out_shape=jax.ShapeDtypeStruct(x.shape, F32),
        compiler_params=_params(2),
        name="ffn",
    )(x, scale2, shift2, gate2, w_gate, w_up, w_down, ln_g, ln_b)


def _layer(x, c, w_ada, b_ada, w_in, ssm_a_re, ssm_a_im, ssm_log_dt, ssm_b_re, ssm_b_im, ssm_c_re, ssm_c_im,
           ssm_d, w_glu_lin, w_glu_gate, cmp_pe_k, cmp_pe_v, w_cmp_k1, w_cmp_k2, w_cmp_v1, w_cmp_v2,
           w_nsa_proj, w_out, ln1_g, ln1_b, w_ffn_gate, w_ffn_up, w_ffn_down, ln2_g, ln2_b):
    bsz, seq, _ = x.shape
    tq = min(KEY_BLOCK, seq)
    tm = min(512, seq)
    tm_dense = min(1024, seq)

    mod = _ada_mod(c, w_ada, b_ada).reshape(6, bsz, 1, D_MODEL)
    shift1, scale1, gate1, shift2, scale2, gate2 = (mod[k] for k in range(6))

    w_nat, w_tr, w_mg = _split_w_in(w_in)
    u_ssm, kv_cmp, k_aug, qT, v_aug, bgT = _in_proj(x, scale1, shift1, w_nat, w_tr, tm=tm)

    mats = _ssm_matrices(ssm_a_re, ssm_a_im, ssm_log_dt, ssm_b_re, ssm_b_im, ssm_c_re, ssm_c_im, ssm_d)
    g_ssm = _ssm(u_ssm, mats)

    kc3, vcT = _compress(kv_cmp,
                        cmp_pe_k.reshape(1, CMP_BLOCK * HEAD_DIM), cmp_pe_v.reshape(1, CMP_BLOCK * HEAD_DIM),
                        w_cmp_k1.reshape(CMP_BLOCK * HEAD_DIM, -1), w_cmp_k2,
                        w_cmp_v1.reshape(CMP_BLOCK * HEAD_DIM, -1), w_cmp_v2)
    o_nsa = _nsa(qT, kc3, vcT, k_aug, v_aug, bgT, tq=tq)

    x1 = _merge(x, g_ssm, o_nsa, scale1, shift1, gate1, w_mg,
                w_glu_lin.astype(BF16), w_glu_gate.astype(BF16), w_nsa_proj.astype(BF16), w_out.astype(BF16),
                ln1_g.reshape(1, D_MODEL), ln1_b.reshape(1, D_MODEL), tm=tm_dense)
    return _ffn(x1, scale2, shift2, gate2, w_ffn_gate.astype(BF16), w_ffn_up.astype(BF16),
                w_ffn_down.astype(BF16), ln2_g.reshape(1, D_MODEL), ln2_b.reshape(1, D_MODEL), tm=tm_dense, n_chunks=2)


def kernel(x, c, w_ada, b_ada, w_in, ssm_a_re, ssm_a_im, ssm_log_dt, ssm_b_re, ssm_b_im, ssm_c_re, ssm_c_im,
           ssm_d, w_glu_lin, w_glu_gate, cmp_pe_k, cmp_pe_v, w_cmp_k1, w_cmp_k2, w_cmp_v1, w_cmp_v2,
           w_nsa_proj, w_out, ln1_g, ln1_b, w_ffn_gate, w_ffn_up, w_ffn_down, ln2_g, ln2_b):
    for l in range(w_ada.shape[0]):
        x = _layer(x, c, w_ada[l], b_ada[l], w_in[l], ssm_a_re[l], ssm_a_im[l], ssm_log_dt[l],
                   ssm_b_re[l], ssm_b_im[l], ssm_c_re[l], ssm_c_im[l], ssm_d[l],
                   w_glu_lin[l], w_glu_gate[l], cmp_pe_k[l], cmp_pe_v[l],
                   w_cmp_k1[l], w_cmp_k2[l], w_cmp_v1[l], w_cmp_v2[l], w_nsa_proj[l], w_out[l],
                   ln1_g[l], ln1_b[l], w_ffn_gate[l], w_ffn_up[l], w_ffn_down[l], ln2_g[l], ln2_b[l])
    return x
```
